```python
import jax, jax.numpy as jnp
from jax import lax
import numpy as np

D_MODEL = 1024
BATCH = 8
SEQ = 8192
DEPTH = 1

POOL_WINDOWS = (2, 4, 8, 16)
POOL_GROUPS = len(POOL_WINDOWS)
POOL_GROUP_DIM = D_MODEL // 16
POOL_WIDTH = POOL_GROUPS * POOL_GROUP_DIM
ATTN_PATTERNS = ((128, 1), (512, 4), (2048, 16))
N_ATTN_GROUPS = len(ATTN_PATTERNS)
HEAD_DIM = 64
HEADS_PER_GROUP = 4
N_ATTN_HEADS = N_ATTN_GROUPS * HEADS_PER_GROUP
ATTN_WIDTH = N_ATTN_HEADS * HEAD_DIM
ATTN_OUT_WIDTH = HEADS_PER_GROUP * HEAD_DIM
ROPE_THETA = 10000.0
BLOCK = 128
N_BRANCHES = 2
IN_WIDTH = POOL_WIDTH + 3 * ATTN_WIDTH + N_BRANCHES * D_MODEL
D_FF = ((8 * D_MODEL // 3 + 255) // 256) * 256
N_MOD = 9
EPS = 1e-6

kernel_name = "hybrid_pool_dilated_attn_macaron_block"


def rmsnorm(x, g):
    xf = x.astype(jnp.float32)
    y = xf * lax.rsqrt(jnp.mean(xf * xf, axis=-1, keepdims=True) + EPS)
    return (y * g.astype(jnp.float32)).astype(x.dtype)


def modulate(x, shift, scale):
    return x * (1 + scale) + shift


def swiglu(u, w_in, w_out):
    a, b = jnp.split(u @ w_in, 2, axis=-1)
    return (jax.nn.silu(a) * b) @ w_out


def rope_tables(positions, dtype):
    inv_freq = ROPE_THETA ** (-jnp.arange(0, HEAD_DIM, 2, dtype=jnp.float32) / HEAD_DIM)
    ang = positions.astype(jnp.float32)[..., None] * inv_freq
    return jnp.cos(ang)[:, :, None, :].astype(dtype), jnp.sin(ang)[:, :, None, :].astype(dtype)


def apply_rope(t, cos, sin):
    t1, t2 = jnp.split(t, 2, axis=-1)
    return jnp.concatenate([t1 * cos - t2 * sin, t2 * cos + t1 * sin], axis=-1)


def multiscale_pool(p, w_pool, pool_scale):
    B, S, _ = p.shape
    pf = p.astype(jnp.float32)
    cs = jnp.pad(jnp.cumsum(pf, axis=1), ((0, 0), (1, 0), (0, 0)))
    t = jnp.arange(S)
    outs = []
    for gi, w in enumerate(POOL_WINDOWS):
        sl = slice(gi * POOL_GROUP_DIM, (gi + 1) * POOL_GROUP_DIM)
        csg = cs[..., sl]
        lagged = jnp.pad(csg[:, :S + 1 - w], ((0, 0), (w - 1, 0), (0, 0)))
        count = jnp.minimum(t + 1, w).astype(jnp.float32)[None, :, None]
        outs.append((csg[:, 1:] - lagged) / count - pf[..., sl])
    d = jnp.stack(outs, axis=2).astype(p.dtype)
    y = jnp.einsum('bsgc,gcd->bsgd', d, w_pool)
    return y.reshape(B, S, POOL_WIDTH) * pool_scale


def dilated_window_attention(q, k, v, steps, dilation):
    B, S, H, Dh = q.shape
    L = S // dilation
    nb = -(-L // BLOCK)
    Lp = nb * BLOCK

    def to_strided(t):
        t = t.reshape(B, L, dilation, H, Dh).transpose(0, 3, 2, 1, 4)
        return jnp.pad(t, ((0, 0), (0, 0), (0, 0), (0, Lp - L), (0, 0)))

    qs, ks, vs = to_strided(q), to_strided(k), to_strided(v)
    qb = qs.reshape(B, H, dilation, nb, BLOCK, Dh)

    def band(t):
        tp = jnp.pad(t, ((0, 0), (0, 0), (0, 0), (BLOCK, 0), (0, 0)))
        prev = tp[..., :Lp, :].reshape(B, H, dilation, nb, BLOCK, Dh)
        cur = t.reshape(B, H, dilation, nb, BLOCK, Dh)
        return jnp.concatenate([prev, cur], axis=-2)

    kb, vb = band(ks), band(vs)
    a = jnp.arange(BLOCK)[:, None]
    cidx = jnp.arange(2 * BLOCK)[None, :]
    rel = a + BLOCK - cidx
    in_band = (rel >= 0) & (rel <= steps)
    key_pos = jnp.arange(nb)[:, None, None] * BLOCK - BLOCK + cidx[None]
    mask = in_band[None] & (key_pos >= 0)

    s = jnp.einsum('bhrnqd,bhrnkd->bhrnqk', qb, kb).astype(jnp.float32) * (HEAD_DIM ** -0.5)
    s = jnp.where(mask, s, -jnp.inf)
    lse = jax.nn.logsumexp(s, axis=-1)
    pr = jnp.exp(s - lse[..., None]).astype(v.dtype)
    o = jnp.einsum('bhrnqk,bhrnkd->bhrnqd', pr, vb)
    o = o.reshape(B, H, dilation, Lp, Dh)[:, :, :, :L]
    o = o.transpose(0, 3, 2, 1, 4).reshape(B, S, H, Dh)
    lse = lse.reshape(B, H, dilation, Lp)[:, :, :, :L].transpose(0, 3, 2, 1).reshape(B, S, H)
    return o, lse


def token_mixing(u, cos, sin, w_in, w_pool, pool_scale, w_pool_branch, w_attn_branch, w_out):
    B, S, _ = u.shape
    proj = u @ w_in
    cuts = [POOL_WIDTH, POOL_WIDTH + ATTN_WIDTH, POOL_WIDTH + 2 * ATTN_WIDTH,
            POOL_WIDTH + 3 * ATTN_WIDTH]
    p, q, k, v, gate_logits = jnp.split(proj, cuts, axis=-1)

    y_pool = multiscale_pool(p, w_pool, pool_scale)

    q = apply_rope(q.reshape(B, S, N_ATTN_HEADS, HEAD_DIM), cos, sin)
    k = apply_rope(k.reshape(B, S, N_ATTN_HEADS, HEAD_DIM), cos, sin)
    v = v.reshape(B, S, N_ATTN_HEADS, HEAD_DIM)
    outs, lses = [], []
    for gi, (window, dilation) in enumerate(ATTN_PATTERNS):
        hs = slice(gi * HEADS_PER_GROUP, (gi + 1) * HEADS_PER_GROUP)
        o, lse = dilated_window_attention(q[:, :, hs], k[:, :, hs], v[:, :, hs],
                                          window // dilation, dilation)
        outs.append(o)
        lses.append(lse)
    wts = jax.nn.softmax(jnp.stack(lses, axis=0), axis=0)
    y_attn = jnp.einsum('gbsh,gbshd->bshd', wts.astype(v.dtype), jnp.stack(outs, axis=0))
    y_attn = y_attn.reshape(B, S, ATTN_OUT_WIDTH)

    gates = jax.nn.sigmoid(gate_logits.astype(jnp.float32)).astype(u.dtype)
    g_pool, g_attn = jnp.split(gates, N_BRANCHES, axis=-1)
    merged = g_pool * (y_pool @ w_pool_branch) + g_attn * (y_attn @ w_attn_branch)
    return merged @ w_out


def _fwd_setup_inputs(seed: int = 0) -> dict:
    key = jax.random.key(seed)
    ks = jax.random.split(key, 20)
    f32 = jnp.float32

    def lin(k, shape, fan_in, scale=1.0):
        return jax.random.normal(k, shape, f32) * (scale * fan_in ** -0.5)

    def gain(k, shape):
        return 1.0 + 0.05 * jax.random.normal(k, shape, f32)

    x = jax.random.normal(ks[0], (BATCH, SEQ, D_MODEL), f32)
    c = jax.random.normal(ks[1], (BATCH, D_MODEL), f32)
    offset = jax.random.randint(ks[2], (BATCH, 1), 0, 1024, dtype=jnp.int32)
    positions = (jnp.arange(SEQ, dtype=jnp.int32)[None, :] + offset).astype(jnp.int32)
    return {
        "x": x,
        "c": c,
        "positions": positions,
        "w_ada": lin(ks[3], (DEPTH, D_MODEL, N_MOD * D_MODEL), D_MODEL, 0.5),
        "b_ada": 0.02 * jax.random.normal(ks[4], (DEPTH, N_MOD * D_MODEL), f32),
        "g_norm_ffn1": gain(ks[5], (DEPTH, D_MODEL)),
        "w_ffn1_in": lin(ks[6], (DEPTH, D_MODEL, 2 * D_FF), D_MODEL),
        "w_ffn1_out": lin(ks[7], (DEPTH, D_FF, D_MODEL), D_FF),
        "g_norm_mix": gain(ks[8], (DEPTH, D_MODEL)),
        "w_in": lin(ks[9], (DEPTH, D_MODEL, IN_WIDTH), D_MODEL),
        "w_pool": lin(ks[10], (DEPTH, POOL_GROUPS, POOL_GROUP_DIM, POOL_GROUP_DIM), POOL_GROUP_DIM),
        "pool_scale": 1.0 + 0.1 * jax.random.normal(ks[11], (DEPTH, POOL_WIDTH), f32),
        "w_pool_branch": lin(ks[12], (DEPTH, POOL_WIDTH, D_MODEL), POOL_WIDTH),
        "w_attn_branch": lin(ks[13], (DEPTH, ATTN_OUT_WIDTH, D_MODEL), ATTN_OUT_WIDTH),
        "w_out": lin(ks[14], (DEPTH, D_MODEL, D_MODEL), D_MODEL),
        "g_norm_ffn2": gain(ks[15], (DEPTH, D_MODEL)),
        "w_ffn2_in": lin(ks[16], (DEPTH, D_MODEL, 2 * D_FF), D_MODEL),
        "w_ffn2_out": lin(ks[17], (DEPTH, D_FF, D_MODEL), D_FF),
        "g_final": gain(ks[18], (D_MODEL,)),
    }


def _fwd_reference(x, c, positions, w_ada, b_ada, g_norm_ffn1, w_ffn1_in, w_ffn1_out,
              g_norm_mix, w_in, w_pool, pool_scale, w_pool_branch, w_attn_branch, w_out,
              g_norm_ffn2, w_ffn2_in, w_ffn2_out, g_final):
    cos, sin = rope_tables(positions, x.dtype)
    cond = jax.nn.silu(c)
    h = x
    for l in range(DEPTH):
        mod = cond @ w_ada[l] + b_ada[l]
        sh1, sc1, gt1, sh2, sc2, gt2, sh3, sc3, gt3 = [
            m[:, None, :] for m in jnp.split(mod, N_MOD, axis=-1)]
        u = modulate(rmsnorm(h, g_norm_ffn1[l]), sh1, sc1)
        h = h + 0.5 * gt1 * swiglu(u, w_ffn1_in[l], w_ffn1_out[l])
        u = modulate(rmsnorm(h, g_norm_mix[l]), sh2, sc2)
        h = h + gt2 * token_mixing(u, cos, sin, w_in[l], w_pool[l], pool_scale[l],
                                   w_pool_branch[l], w_attn_branch[l], w_out[l])
        u = modulate(rmsnorm(h, g_norm_ffn2[l]), sh3, sc3)
        h = h + 0.5 * gt3 * swiglu(u, w_ffn2_in[l], w_ffn2_out[l])
    return rmsnorm(h, g_final)


import jax as _jax
import jax.numpy as _jnp

TWIN_FORMAT = 'train_step'
FWD_PARAMS = ['x', 'c', 'positions', 'w_ada', 'b_ada', 'g_norm_ffn1', 'w_ffn1_in', 'w_ffn1_out', 'g_norm_mix', 'w_in', 'w_pool', 'pool_scale', 'w_pool_branch', 'w_attn_branch', 'w_out', 'g_norm_ffn2', 'w_ffn2_in', 'w_ffn2_out', 'g_final']
TWIN_WEIGHTS = ['w_ada', 'b_ada', 'g_norm_ffn1', 'w_ffn1_in', 'w_ffn1_out', 'g_norm_mix', 'w_in', 'w_pool', 'pool_scale', 'w_pool_branch', 'w_attn_branch', 'w_out', 'g_norm_ffn2', 'w_ffn2_in', 'w_ffn2_out', 'g_final']
TWIN_DIFF_INPUT = 'x'
TWIN_INPUTS = ['x', 'c', 'positions', 'w_ada', 'b_ada', 'g_norm_ffn1', 'w_ffn1_in', 'w_ffn1_out', 'g_norm_mix', 'w_in', 'w_pool', 'pool_scale', 'w_pool_branch', 'w_attn_branch', 'w_out', 'g_norm_ffn2', 'w_ffn2_in', 'w_ffn2_out', 'g_final', 'loss_target', 'm_w_ada', 'm_b_ada', 'm_g_norm_ffn1', 'm_w_ffn1_in', 'm_w_ffn1_out', 'm_g_norm_mix', 'm_w_in', 'm_w_pool', 'm_pool_scale', 'm_w_pool_branch', 'm_w_attn_branch', 'm_w_out', 'm_g_norm_ffn2', 'm_w_ffn2_in', 'm_w_ffn2_out', 'm_g_final', 'v_w_ada', 'v_b_ada', 'v_g_norm_ffn1', 'v_w_ffn1_in', 'v_w_ffn1_out', 'v_g_norm_mix', 'v_w_in', 'v_w_pool', 'v_pool_scale', 'v_w_pool_branch', 'v_w_attn_branch', 'v_w_out', 'v_g_norm_ffn2', 'v_w_ffn2_in', 'v_w_ffn2_out', 'v_g_final']
TWIN_OUTPUTS = ['loss', 'grad_x', 'grad_w_ada', 'grad_b_ada', 'grad_g_norm_ffn1', 'grad_w_ffn1_in', 'grad_w_ffn1_out', 'grad_g_norm_mix', 'grad_w_in', 'grad_w_pool', 'grad_pool_scale', 'grad_w_pool_branch', 'grad_w_attn_branch', 'grad_w_out', 'grad_g_norm_ffn2', 'grad_w_ffn2_in', 'grad_w_ffn2_out', 'grad_g_final', 'delta_w_ada', 'delta_b_ada', 'delta_g_norm_ffn1', 'delta_w_ffn1_in', 'delta_w_ffn1_out', 'delta_g_norm_mix', 'delta_w_in', 'delta_w_pool', 'delta_pool_scale', 'delta_w_pool_branch', 'delta_w_attn_branch', 'delta_w_out', 'delta_g_norm_ffn2', 'delta_w_ffn2_in', 'delta_w_ffn2_out', 'delta_g_final', 'new_m_w_ada', 'new_m_b_ada', 'new_m_g_norm_ffn1', 'new_m_w_ffn1_in', 'new_m_w_ffn1_out', 'new_m_g_norm_mix', 'new_m_w_in', 'new_m_w_pool', 'new_m_pool_scale', 'new_m_w_pool_branch', 'new_m_w_attn_branch', 'new_m_w_out', 'new_m_g_norm_ffn2', 'new_m_w_ffn2_in', 'new_m_w_ffn2_out', 'new_m_g_final', 'new_v_w_ada', 'new_v_b_ada', 'new_v_g_norm_ffn1', 'new_v_w_ffn1_in', 'new_v_w_ffn1_out', 'new_v_g_norm_mix', 'new_v_w_in', 'new_v_w_pool', 'new_v_pool_scale', 'new_v_w_pool_branch', 'new_v_w_attn_branch', 'new_v_w_out', 'new_v_g_norm_ffn2', 'new_v_w_ffn2_in', 'new_v_w_ffn2_out', 'new_v_g_final']
TWIN_LEAF_KINDS = {'loss': 'loss', 'grad_x': 'grad_x', 'grad_w_ada': 'grad_w', 'grad_b_ada': 'grad_w', 'grad_g_norm_ffn1': 'grad_w', 'grad_w_ffn1_in': 'grad_w', 'grad_w_ffn1_out': 'grad_w', 'grad_g_norm_mix': 'grad_w', 'grad_w_in': 'grad_w', 'grad_w_pool': 'grad_w', 'grad_pool_scale': 'grad_w', 'grad_w_pool_branch': 'grad_w', 'grad_w_attn_branch': 'grad_w', 'grad_w_out': 'grad_w', 'grad_g_norm_ffn2': 'grad_w', 'grad_w_ffn2_in': 'grad_w', 'grad_w_ffn2_out': 'grad_w', 'grad_g_final': 'grad_w', 'delta_w_ada': 'delta_w', 'delta_b_ada': 'delta_w', 'delta_g_norm_ffn1': 'delta_w', 'delta_w_ffn1_in': 'delta_w', 'delta_w_ffn1_out': 'delta_w', 'delta_g_norm_mix': 'delta_w', 'delta_w_in': 'delta_w', 'delta_w_pool': 'delta_w', 'delta_pool_scale': 'delta_w', 'delta_w_pool_branch': 'delta_w', 'delta_w_attn_branch': 'delta_w', 'delta_w_out': 'delta_w', 'delta_g_norm_ffn2': 'delta_w', 'delta_w_ffn2_in': 'delta_w', 'delta_w_ffn2_out': 'delta_w', 'delta_g_final': 'delta_w', 'new_m_w_ada': 'new_m', 'new_m_b_ada': 'new_m', 'new_m_g_norm_ffn1': 'new_m', 'new_m_w_ffn1_in': 'new_m', 'new_m_w_ffn1_out': 'new_m', 'new_m_g_norm_mix': 'new_m', 'new_m_w_in': 'new_m', 'new_m_w_pool': 'new_m', 'new_m_pool_scale': 'new_m', 'new_m_w_pool_branch': 'new_m', 'new_m_w_attn_branch': 'new_m', 'new_m_w_out': 'new_m', 'new_m_g_norm_ffn2': 'new_m', 'new_m_w_ffn2_in': 'new_m', 'new_m_w_ffn2_out': 'new_m', 'new_m_g_final': 'new_m', 'new_v_w_ada': 'new_v', 'new_v_b_ada': 'new_v', 'new_v_g_norm_ffn1': 'new_v', 'new_v_w_ffn1_in': 'new_v', 'new_v_w_ffn1_out': 'new_v', 'new_v_g_norm_mix': 'new_v', 'new_v_w_in': 'new_v', 'new_v_w_pool': 'new_v', 'new_v_pool_scale': 'new_v', 'new_v_w_pool_branch': 'new_v', 'new_v_w_attn_branch': 'new_v', 'new_v_w_out': 'new_v', 'new_v_g_norm_ffn2': 'new_v', 'new_v_w_ffn2_in': 'new_v', 'new_v_w_ffn2_out': 'new_v', 'new_v_g_final': 'new_v'}


def _forward(args):
    return _fwd_reference(*[args[k] for k in FWD_PARAMS])


def _output_shape():
    def fwd():
        inp = _fwd_setup_inputs(0)
        return _fwd_reference(*[inp[k] for k in FWD_PARAMS])
    out = _jax.eval_shape(fwd)
    return out.shape, out.dtype

N_MICROBATCH = 1
ADAM_LR = 0.001
ADAM_B1 = 0.9
ADAM_B2 = 0.999
ADAM_EPS = 1e-08
ADAM_WD = 0.01
ADAM_STEP = 10
PER_EXAMPLE_BATCH_AXIS = {'x': 0, 'c': 0, 'positions': 0, 'loss_target': 0}
SHARED_INPUTS = []
_WEIGHT_DTYPES = {'w_ada': _jnp.float32, 'b_ada': _jnp.float32, 'g_norm_ffn1': _jnp.float32, 'w_ffn1_in': _jnp.float32, 'w_ffn1_out': _jnp.float32, 'g_norm_mix': _jnp.float32, 'w_in': _jnp.float32, 'w_pool': _jnp.float32, 'pool_scale': _jnp.float32, 'w_pool_branch': _jnp.float32, 'w_attn_branch': _jnp.float32, 'w_out': _jnp.float32, 'g_norm_ffn2': _jnp.float32, 'w_ffn2_in': _jnp.float32, 'w_ffn2_out': _jnp.float32, 'g_final': _jnp.float32}
MOMENT_SCALE = {'w_ada': 4.726074e-02, 'b_ada': 8.856505e-02, 'g_norm_ffn1': 3.843520e-02, 'w_ffn1_in': 1.680370e-02, 'w_ffn1_out': 2.744572e-02, 'g_norm_mix': 4.547655e-02, 'w_in': 2.152662e-02, 'w_pool': 7.735252e-02, 'pool_scale': 8.634312e-02, 'w_pool_branch': 3.885883e-02, 'w_attn_branch': 1.344362e-02, 'w_out': 4.019908e-02, 'g_norm_ffn2': 3.820433e-02, 'w_ffn2_in': 1.642118e-02, 'w_ffn2_out': 2.689241e-02, 'g_final': 6.410876e+01}


def _to_microbatches(a, axis):
    t = _jnp.moveaxis(a, axis, 0)
    t = t.reshape((N_MICROBATCH, t.shape[0] // N_MICROBATCH) + t.shape[1:])
    return _jnp.moveaxis(t, 1, axis + 1)


def setup_inputs(seed: int = 0) -> dict:
    inp = _fwd_setup_inputs(seed)
    key = _jax.random.fold_in(_jax.random.key(seed), 7919)
    shape, _ = _output_shape()
    out = dict(inp)
    out["loss_target"] = _jax.random.normal(_jax.random.fold_in(key, 0), shape, _jnp.float32)
    for i, name in enumerate(TWIN_WEIGHTS):
        w = inp[name].astype(_jnp.float32)
        if MOMENT_SCALE is None:
            s = _jnp.sqrt(_jnp.mean(_jnp.square(w)) + 1e-30)
        else:
            s = MOMENT_SCALE[name]
        km, kv = _jax.random.split(_jax.random.fold_in(key, i + 1))
        out[name] = w
        out["m_" + name] = s * _jax.random.normal(km, w.shape, _jnp.float32)
        out["v_" + name] = (s * s) * _jax.random.uniform(kv, w.shape, _jnp.float32, 0.5, 1.5)
    if N_MICROBATCH > 1:
        for name, axis in PER_EXAMPLE_BATCH_AXIS.items():
            out[name] = _to_microbatches(out[name], axis)
    return {'x': out['x'], 'c': out['c'], 'positions': out['positions'], 'w_ada': out['w_ada'], 'b_ada': out['b_ada'], 'g_norm_ffn1': out['g_norm_ffn1'], 'w_ffn1_in': out['w_ffn1_in'], 'w_ffn1_out': out['w_ffn1_out'], 'g_norm_mix': out['g_norm_mix'], 'w_in': out['w_in'], 'w_pool': out['w_pool'], 'pool_scale': out['pool_scale'], 'w_pool_branch': out['w_pool_branch'], 'w_attn_branch': out['w_attn_branch'], 'w_out': out['w_out'], 'g_norm_ffn2': out['g_norm_ffn2'], 'w_ffn2_in': out['w_ffn2_in'], 'w_ffn2_out': out['w_ffn2_out'], 'g_final': out['g_final'], 'loss_target': out['loss_target'], 'm_w_ada': out['m_w_ada'], 'm_b_ada': out['m_b_ada'], 'm_g_norm_ffn1': out['m_g_norm_ffn1'], 'm_w_ffn1_in': out['m_w_ffn1_in'], 'm_w_ffn1_out': out['m_w_ffn1_out'], 'm_g_norm_mix': out['m_g_norm_mix'], 'm_w_in': out['m_w_in'], 'm_w_pool': out['m_w_pool'], 'm_pool_scale': out['m_pool_scale'], 'm_w_pool_branch': out['m_w_pool_branch'], 'm_w_attn_branch': out['m_w_attn_branch'], 'm_w_out': out['m_w_out'], 'm_g_norm_ffn2': out['m_g_norm_ffn2'], 'm_w_ffn2_in': out['m_w_ffn2_in'], 'm_w_ffn2_out': out['m_w_ffn2_out'], 'm_g_final': out['m_g_final'], 'v_w_ada': out['v_w_ada'], 'v_b_ada': out['v_b_ada'], 'v_g_norm_ffn1': out['v_g_norm_ffn1'], 'v_w_ffn1_in': out['v_w_ffn1_in'], 'v_w_ffn1_out': out['v_w_ffn1_out'], 'v_g_norm_mix': out['v_g_norm_mix'], 'v_w_in': out['v_w_in'], 'v_w_pool': out['v_w_pool'], 'v_pool_scale': out['v_pool_scale'], 'v_w_pool_branch': out['v_w_pool_branch'], 'v_w_attn_branch': out['v_w_attn_branch'], 'v_w_out': out['v_w_out'], 'v_g_norm_ffn2': out['v_g_norm_ffn2'], 'v_w_ffn2_in': out['v_w_ffn2_in'], 'v_w_ffn2_out': out['v_w_ffn2_out'], 'v_g_final': out['v_g_final']}


def _loss(weights, diff, rest, loss_target):
    with _jax.named_scope("forward"):
        args = {**rest, TWIN_DIFF_INPUT: diff, **{k: w.astype(_WEIGHT_DTYPES[k]) for k, w in weights.items()}}
        y = _forward(args)
    with _jax.named_scope("loss_head"):
        err = _jnp.square(y.astype(_jnp.float32) - loss_target)
        return 0.5 * _jnp.sum(_jnp.mean(err, axis=-1)) if err.ndim else 0.5 * err


def _adamw(w, g, m, v):
    m = ADAM_B1 * m + (1.0 - ADAM_B1) * g
    v = ADAM_B2 * v + (1.0 - ADAM_B2) * _jnp.square(g)
    m_hat = m / (1.0 - ADAM_B1 ** ADAM_STEP)
    v_hat = v / (1.0 - ADAM_B2 ** ADAM_STEP)
    delta = -ADAM_LR * (m_hat / (_jnp.sqrt(v_hat) + ADAM_EPS) + ADAM_WD * w)
    return delta, m, v


def reference(x, c, positions, w_ada, b_ada, g_norm_ffn1, w_ffn1_in, w_ffn1_out, g_norm_mix, w_in, w_pool, pool_scale, w_pool_branch, w_attn_branch, w_out, g_norm_ffn2, w_ffn2_in, w_ffn2_out, g_final, loss_target, m_w_ada, m_b_ada, m_g_norm_ffn1, m_w_ffn1_in, m_w_ffn1_out, m_g_norm_mix, m_w_in, m_w_pool, m_pool_scale, m_w_pool_branch, m_w_attn_branch, m_w_out, m_g_norm_ffn2, m_w_ffn2_in, m_w_ffn2_out, m_g_final, v_w_ada, v_b_ada, v_g_norm_ffn1, v_w_ffn1_in, v_w_ffn1_out, v_g_norm_mix, v_w_in, v_w_pool, v_pool_scale, v_w_pool_branch, v_w_attn_branch, v_w_out, v_g_norm_ffn2, v_w_ffn2_in, v_w_ffn2_out, v_g_final):
    given = dict(x=x, c=c, positions=positions, w_ada=w_ada, b_ada=b_ada, g_norm_ffn1=g_norm_ffn1, w_ffn1_in=w_ffn1_in, w_ffn1_out=w_ffn1_out, g_norm_mix=g_norm_mix, w_in=w_in, w_pool=w_pool, pool_scale=pool_scale, w_pool_branch=w_pool_branch, w_attn_branch=w_attn_branch, w_out=w_out, g_norm_ffn2=g_norm_ffn2, w_ffn2_in=w_ffn2_in, w_ffn2_out=w_ffn2_out, g_final=g_final, loss_target=loss_target, m_w_ada=m_w_ada, m_b_ada=m_b_ada, m_g_norm_ffn1=m_g_norm_ffn1, m_w_ffn1_in=m_w_ffn1_in, m_w_ffn1_out=m_w_ffn1_out, m_g_norm_mix=m_g_norm_mix, m_w_in=m_w_in, m_w_pool=m_w_pool, m_pool_scale=m_pool_scale, m_w_pool_branch=m_w_pool_branch, m_w_attn_branch=m_w_attn_branch, m_w_out=m_w_out, m_g_norm_ffn2=m_g_norm_ffn2, m_w_ffn2_in=m_w_ffn2_in, m_w_ffn2_out=m_w_ffn2_out, m_g_final=m_g_final, v_w_ada=v_w_ada, v_b_ada=v_b_ada, v_g_norm_ffn1=v_g_norm_ffn1, v_w_ffn1_in=v_w_ffn1_in, v_w_ffn1_out=v_w_ffn1_out, v_g_norm_mix=v_g_norm_mix, v_w_in=v_w_in, v_w_pool=v_w_pool, v_pool_scale=v_pool_scale, v_w_pool_branch=v_w_pool_branch, v_w_attn_branch=v_w_attn_branch, v_w_out=v_w_out, v_g_norm_ffn2=v_g_norm_ffn2, v_w_ffn2_in=v_w_ffn2_in, v_w_ffn2_out=v_w_ffn2_out, v_g_final=v_g_final)
    weights = {n: given[n] for n in TWIN_WEIGHTS}
    shared = {n: given[n] for n in SHARED_INPUTS}
    per_example = {n: given[n] for n in ['x', 'c', 'positions']}
    grad_fn = _jax.value_and_grad(_loss, argnums=(0, 1))

    def one_microbatch(ex, loss_target):
        ex = dict(ex)
        diff = ex.pop(TWIN_DIFF_INPUT)
        return grad_fn(weights, diff, {**shared, **ex}, loss_target)

    if N_MICROBATCH == 1:
        loss, (grad_w, grad_x) = one_microbatch(per_example, given["loss_target"])
    else:
        def body(carry, xs):
            loss_sum, grad_sum = carry
            l_k, (gw_k, gx_k) = one_microbatch(xs[0], xs[1])
            with _jax.named_scope("update"):
                return (loss_sum + l_k, _jax.tree.map(_jnp.add, grad_sum, gw_k)), gx_k

        init = (_jnp.zeros((), _jnp.float32), _jax.tree.map(_jnp.zeros_like, weights))
        (loss, grad_w), grad_x = _jax.lax.scan(body, init, (per_example, given["loss_target"]))
    with _jax.named_scope("update"):
        delta_w, new_m, new_v = {}, {}, {}
        for n in TWIN_WEIGHTS:
            delta_w[n], new_m[n], new_v[n] = _adamw(weights[n], grad_w[n], given["m_" + n], given["v_" + n])
    return (loss, grad_x, *[grad_w[n] for n in TWIN_WEIGHTS], *[delta_w[n] for n in TWIN_WEIGHTS],
            *[new_m[n] for n in TWIN_WEIGHTS], *[new_v[n] for n in TWIN_WEIGHTS])
```

```python
import functools

import jax
import jax.numpy as jnp
from jax import lax
from jax.experimental import pallas as pl
from jax.experimental.pallas import tpu as pltpu

F32 = jnp.float32
BF16 = jnp.bfloat16

D = 1024
FF = 2816
FC = 1408
PW = 256
GA = 256
HD = 64
NG = 3
DIL = (1, 4, 16)
BLK = 128
GW = 2 * D
INW = PW + 3 * NG * GA + GW
NMOD = 9
POOL_WINDOWS = (2, 4, 8, 16)
HALO = 16
EPS = 1e-6
SCALE = HD ** -0.5
NEG = -1e30

LR, B1, B2, AEPS, WD, STEP = 0.001, 0.9, 0.999, 1e-08, 0.01, 10

VMEM_BIG = 56 * 1024 * 1024
TM = 256

MESH = pl.DeviceIdType.MESH
ANY = pl.BlockSpec(memory_space=pl.ANY)


def _call(body, name, grid, in_specs, out_specs, out_shape, scratch=(), vmem=None):
    return pl.pallas_call(
        body, name=name, grid=grid, in_specs=list(in_specs), out_specs=out_specs, out_shape=out_shape,
        scratch_shapes=list(scratch),
        compiler_params=pltpu.CompilerParams(dimension_semantics=("arbitrary",) * len(grid), vmem_limit_bytes=vmem))


def _rows(tm, n):
    return pl.BlockSpec((tm, n), lambda i: (i, 0))


def _const(shape):
    return pl.BlockSpec(shape, lambda i: (0,) * len(shape))


def _sds(shape, dtype):
    return jax.ShapeDtypeStruct(shape, dtype)


def _dot(a, b):
    return jnp.dot(a, b, preferred_element_type=F32)


def _dot_nt(a, b):
    return lax.dot_general(a, b, (((1,), (1,)), ((), ())), preferred_element_type=F32)


def _dot_tn(a, b):
    return lax.dot_general(a, b, (((0,), (0,)), ((), ())), preferred_element_type=F32)


def _colsum(v):
    return jnp.sum(v, axis=0, keepdims=True)


def _norm_fwd(h, g, sh, sc):
    r = lax.rsqrt(jnp.mean(h * h, axis=-1, keepdims=True) + EPS)
    xh = h * r
    n = xh * g
    return xh, r, n, n * (1.0 + sc) + sh


def _norm_bwd(du, xh, r, n, g, sc):
    dn = du * (1.0 + sc)
    dxh = dn * g
    dh = r * (dxh - xh * jnp.mean(dxh * xh, axis=-1, keepdims=True))
    return dh, _colsum(du), _colsum(du * n), _colsum(dn * xh)


def _load_once(pairs, sems):
    @pl.when(pl.program_id(0) == 0)
    def _():
        cps = [pltpu.make_async_copy(src, dst, sems.at[j]) for j, (src, dst) in enumerate(pairs)]
        for cp in cps:
            cp.start()
        for cp in cps:
            cp.wait()


def _zero_first(ref):
    @pl.when(pl.program_id(0) == 0)
    def _():
        ref[...] = jnp.zeros(ref.shape, ref.dtype)


def _ffn_fwd(h, vec, win, wout, name):
    T = h.shape[0]

    def body(h_ref, vec_ref, win_hbm, wout_hbm, ho_ref, u_ref, a_ref, b_ref, f_ref, win_v, wout_v, sems):
        _load_once([(win_hbm, win_v), (wout_hbm, wout_v)], sems)
        hh = h_ref[...]
        g, sh, sc, gt = vec_ref[0:1, :], vec_ref[1:2, :], vec_ref[2:3, :], vec_ref[3:4, :]
        _, _, _, u = _norm_fwd(hh, g, sh, sc)
        ub = u.astype(BF16)
        u_ref[...] = ub
        acc = None
        for j in range(FF // FC):
            lo, hi = j * FC, (j + 1) * FC
            a = _dot(ub, win_v[:, lo:hi])
            b = _dot(ub, win_v[:, FF + lo:FF + hi])
            a_ref[:, lo:hi] = a.astype(BF16)
            b_ref[:, lo:hi] = b.astype(BF16)
            s = (a * jax.nn.sigmoid(a) * b).astype(BF16)
            part = _dot(s, wout_v[lo:hi, :])
            acc = part if acc is None else acc + part
        f_ref[...] = acc.astype(BF16)
        ho_ref[...] = hh + 0.5 * gt * acc

    return _call(
        body, name, (T // TM,),
        [_rows(TM, D), _const((8, D)), ANY, ANY],
        [_rows(TM, D), _rows(TM, D), _rows(TM, FF), _rows(TM, FF), _rows(TM, D)],
        [_sds((T, D), F32), _sds((T, D), BF16), _sds((T, FF), BF16), _sds((T, FF), BF16), _sds((T, D), BF16)],
        scratch=[pltpu.VMEM((D, 2 * FF), BF16), pltpu.VMEM((FF, D), BF16), pltpu.SemaphoreType.DMA((2,))],
        vmem=VMEM_BIG,
    )(h, vec, win, wout)


def _ffn_bwd(dh, h, a, b, f, vec, win, wout, name):
    T = h.shape[0]

    def body(dh_ref, h_ref, a_ref, b_ref, f_ref, vec_ref, win_hbm, wout_hbm,
             dhi_ref, dab_ref, s_ref, df_ref, acc_ref, win_v, wout_v, sems):
        _load_once([(win_hbm, win_v), (wout_hbm, wout_v)], sems)
        _zero_first(acc_ref)
        g, sh, sc, gt = vec_ref[0:1, :], vec_ref[1:2, :], vec_ref[2:3, :], vec_ref[3:4, :]
        dho = dh_ref[...]
        df = (0.5 * gt * dho).astype(BF16)
        df_ref[...] = df
        dgt = _colsum(0.5 * dho * f_ref[...].astype(F32))
        du = None
        for j in range(FF // FC):
            lo, hi = j * FC, (j + 1) * FC
            av = a_ref[:, lo:hi].astype(F32)
            bv = b_ref[:, lo:hi].astype(F32)
            ds = _dot_nt(df, wout_v[lo:hi, :])
            sig = jax.nn.sigmoid(av)
            sa = av * sig
            s_ref[:, lo:hi] = (sa * bv).astype(BF16)
            da = (ds * bv * (sig * (1.0 + av * (1.0 - sig)))).astype(BF16)
            db = (ds * sa).astype(BF16)
            dab_ref[:, lo:hi] = da
            dab_ref[:, FF + lo:FF + hi] = db
            part = _dot_nt(da, win_v[:, lo:hi]) + _dot_nt(db, win_v[:, FF + lo:FF + hi])
            du = part if du is None else du + part
        xh, r, n, _ = _norm_fwd(h_ref[...], g, sh, sc)
        dhn, dsh, dsc, dg = _norm_bwd(du, xh, r, n, g, sc)
        dhi_ref[...] = dho + dhn
        acc_ref[0:1, :] += dsh
        acc_ref[1:2, :] += dsc
        acc_ref[2:3, :] += dg
        acc_ref[3:4, :] += dgt

    return _call(
        body, name, (T // TM,),
        [_rows(TM, D), _rows(TM, D), _rows(TM, FF), _rows(TM, FF), _rows(TM, D), _const((8, D)), ANY, ANY],
        [_rows(TM, D), _rows(TM, 2 * FF), _rows(TM, FF), _rows(TM, D), _const((8, D))],
        [_sds((T, D), F32), _sds((T, 2 * FF), BF16), _sds((T, FF), BF16), _sds((T, D), BF16), _sds((8, D), F32)],
        scratch=[pltpu.VMEM((D, 2 * FF), BF16), pltpu.VMEM((FF, D), BF16), pltpu.SemaphoreType.DMA((2,))],
        vmem=VMEM_BIG,
    )(dh, h, a, b, f, vec, win, wout)


def _wgrad(x, y, name, tk, tn, tt, out_dtype=BF16):
    T, K = x.shape
    N = y.shape[1]
    nt = T // tt

    def body(x_ref, y_ref, o_ref, acc_ref):
        t = pl.program_id(2)
        part = _dot_tn(x_ref[...], y_ref[...])

        @pl.when(t == 0)
        def _():
            acc_ref[...] = part

        @pl.when(t > 0)
        def _():
            acc_ref[...] += part

        @pl.when(t == nt - 1)
        def _():
            o_ref[...] = acc_ref[...].astype(out_dtype)

    return pl.pallas_call(
        body, name=name, grid=(K // tk, N // tn, nt),
        in_specs=[pl.BlockSpec((tt, tk), lambda i, j, t: (t, i)), pl.BlockSpec((tt, tn), lambda i, j, t: (t, j))],
        out_specs=pl.BlockSpec((tk, tn), lambda i, j, t: (i, j)),
        out_shape=_sds((K, N), out_dtype),
        scratch_shapes=[pltpu.VMEM((tk, tn), F32)],
        compiler_params=pltpu.CompilerParams(dimension_semantics=("arbitrary",) * 3, vmem_limit_bytes=VMEM_BIG),
    )(x, y)


def _final_loss(h, tgt, gvec):
    T = h.shape[0]

    def body(h_ref, t_ref, g_ref, dh_ref, acc_ref):
        _zero_first(acc_ref)
        hh = h_ref[...]
        g = g_ref[0:1, :]
        r = lax.rsqrt(jnp.mean(hh * hh, axis=-1, keepdims=True) + EPS)
        xh = hh * r
        err = xh * g - t_ref[...]
        dy = err * (1.0 / D)
        dxh = dy * g
        dh_ref[...] = r * (dxh - xh * jnp.mean(dxh * xh, axis=-1, keepdims=True))
        acc_ref[0:1, :] += _colsum(err * err)
        acc_ref[1:2, :] += _colsum(dy * xh)

    return _call(
        body, "final_loss", (T // TM,),
        [_rows(TM, D), _rows(TM, D), _const((8, D))],
        [_rows(TM, D), _const((8, D))],
        [_sds((T, D), F32), _sds((8, D), F32)],
    )(h, tgt, gvec)


def _swap_halves(t):
    w = t.shape[1]
    lane = lax.broadcasted_iota(jnp.int32, t.shape, 1)
    return jnp.where(lane % HD < HD // 2, pltpu.roll(t, w - HD // 2, 1), pltpu.roll(t, HD // 2, 1))


def _rope(t, cos, sin_signed):
    c = jnp.tile(cos, (1, t.shape[1] // cos.shape[1]))
    s = jnp.tile(sin_signed, (1, t.shape[1] // sin_signed.shape[1]))
    return t * c + _swap_halves(t) * s


def _rope_bwd(dt, cos, sin_signed):
    c = jnp.tile(cos, (1, dt.shape[1] // cos.shape[1]))
    s = jnp.tile(sin_signed, (1, dt.shape[1] // sin_signed.shape[1]))
    return dt * c + _swap_halves(dt * s)


def _mix_proj(h, vec, win, cos, sin):
    T = h.shape[0]

    def body(h_ref, vec_ref, win_hbm, cos_ref, sin_ref, u_ref, p_ref, *rest):
        qkv_refs, gates_ref, win_v, sems = rest[:3 * NG], rest[3 * NG], rest[3 * NG + 1], rest[3 * NG + 2]
        _load_once([(win_hbm, win_v)], sems)
        g, sh, sc = vec_ref[0:1, :], vec_ref[1:2, :], vec_ref[2:3, :]
        _, _, _, u = _norm_fwd(h_ref[...], g, sh, sc)
        ub = u.astype(BF16)
        u_ref[...] = ub
        p_ref[...] = _dot(ub, win_v[:, 0:PW])
        cos_t, sin_t = cos_ref[...], sin_ref[...]
        for j in range(3 * NG):
            col = PW + j * GA
            t = _dot(ub, win_v[:, col:col + GA])
            if j < 2 * NG:
                t = _rope(t, cos_t, sin_t)
            qkv_refs[j][...] = t.astype(BF16)
        for j in range(GW // 512):
            col = PW + 3 * NG * GA + j * 512
            gates_ref[:, j * 512:(j + 1) * 512] = jax.nn.sigmoid(_dot(ub, win_v[:, col:col + 512])).astype(BF16)

    outs = _call(
        body, "mix_proj", (T // TM,),
        [_rows(TM, D), _const((8, D)), ANY, _rows(TM, 128), _rows(TM, 128)],
        [_rows(TM, D), _rows(TM, PW)] + [_rows(TM, GA)] * (3 * NG) + [_rows(TM, GW)],
        [_sds((T, D), BF16), _sds((T, PW), F32)] + [_sds((T, GA), BF16)] * (3 * NG) + [_sds((T, GW), BF16)],
        scratch=[pltpu.VMEM((D, INW), BF16), pltpu.SemaphoreType.DMA((1,))],
        vmem=VMEM_BIG,
    )(h, vec, win, cos, sin)
    return outs[0], outs[1], outs[2:2 + NG], outs[2 + NG:2 + 2 * NG], outs[2 + 2 * NG:2 + 3 * NG], outs[2 + 3 * NG]


def _band_masks():
    a = lax.broadcasted_iota(jnp.int32, (BLK, BLK), 0)
    c = lax.broadcasted_iota(jnp.int32, (BLK, BLK), 1)
    return c <= a, c >= a


def _attn_fwd(q, k, v, nb, name):
    T = q.shape[0]

    def body(q_ref, kc_ref, kp_ref, vc_ref, vp_ref, o_ref, lse_ref):
        i = pl.program_id(0)
        has_prev = (i % nb) != 0
        m_cur, m_prev = _band_masks()
        m_prev = jnp.logical_and(m_prev, has_prev)
        os, ls = [], []
        for hd in range(GA // HD):
            sl = slice(hd * HD, (hd + 1) * HD)
            qh = q_ref[:, sl]
            s_c = jnp.where(m_cur, _dot_nt(qh, kc_ref[:, sl]) * SCALE, NEG)
            s_p = jnp.where(m_prev, _dot_nt(qh, kp_ref[:, sl]) * SCALE, NEG)
            mx = jnp.maximum(jnp.max(s_c, axis=-1, keepdims=True), jnp.max(s_p, axis=-1, keepdims=True))
            l = jnp.sum(jnp.exp(s_c - mx), axis=-1, keepdims=True) + jnp.sum(jnp.exp(s_p - mx), axis=-1, keepdims=True)
            lse = mx + jnp.log(l)
            p_c = jnp.exp(s_c - lse).astype(BF16)
            p_p = jnp.exp(s_p - lse).astype(BF16)
            os.append(_dot(p_c, vc_ref[:, sl]) + _dot(p_p, vp_ref[:, sl]))
            ls.append(jnp.broadcast_to(lse, (BLK, HD)))
        o_ref[...] = jnp.concatenate(os, axis=1)
        lse_ref[...] = jnp.concatenate(ls, axis=1)

    cur = pl.BlockSpec((BLK, GA), lambda i: (i, 0))
    prev = pl.BlockSpec((BLK, GA), lambda i: (jnp.maximum(i - 1, 0), 0))
    return _call(body, name, (T // BLK,), [cur, cur, prev, cur, prev], [cur, cur],
                 [_sds((T, GA), F32), _sds((T, GA), F32)])(q, k, k, v, v)


def _attn_bwd(q, k, v, do, lse, e, nb, name):
    T = q.shape[0]
    nbt = T // BLK

    def body(q_ref, kc_ref, vc_ref, do_ref, lse_ref, e_ref, kp_ref, vp_ref, qn_ref, don_ref, lsen_ref, en_ref,
             dq_ref, dk_ref, dv_ref):
        i = pl.program_id(0)
        has_prev = (i % nb) != 0
        has_next = ((i + 1) % nb) != 0
        m_cur, m_band = _band_masks()
        m_prev = jnp.logical_and(m_band, has_prev)
        m_next = jnp.logical_and(m_band, has_next)
        dqs, dks, dvs = [], [], []
        for hd in range(GA // HD):
            sl = slice(hd * HD, (hd + 1) * HD)
            one = slice(hd * HD, hd * HD + 1)
            qh, kc, kp, vc, vp, doh = q_ref[:, sl], kc_ref[:, sl], kp_ref[:, sl], vc_ref[:, sl], vp_ref[:, sl], do_ref[:, sl]
            lse_h, e_h = lse_ref[:, one], e_ref[:, one]
            p_c = jnp.where(m_cur, jnp.exp(_dot_nt(qh, kc) * SCALE - lse_h), 0.0)
            p_p = jnp.where(m_prev, jnp.exp(_dot_nt(qh, kp) * SCALE - lse_h), 0.0)
            ds_c = (p_c * (_dot_nt(doh, vc) + e_h)).astype(BF16)
            ds_p = (p_p * (_dot_nt(doh, vp) + e_h)).astype(BF16)
            dqs.append((_dot(ds_c, kc) + _dot(ds_p, kp)) * SCALE)
            qn, don = qn_ref[:, sl], don_ref[:, sl]
            p_n = jnp.where(m_next, jnp.exp(_dot_nt(qn, kc) * SCALE - lsen_ref[:, one]), 0.0)
            ds_n = (p_n * (_dot_nt(don, vc) + en_ref[:, one])).astype(BF16)
            dks.append((_dot_tn(ds_c, qh) + _dot_tn(ds_n, qn)) * SCALE)
            dvs.append(_dot_tn(p_c.astype(BF16), doh) + _dot_tn(p_n.astype(BF16), don))
        dq_ref[...] = jnp.concatenate(dqs, axis=1)
        dk_ref[...] = jnp.concatenate(dks, axis=1)
        dv_ref[...] = jnp.concatenate(dvs, axis=1).astype(BF16)

    cur = pl.BlockSpec((BLK, GA), lambda i: (i, 0))
    prev = pl.BlockSpec((BLK, GA), lambda i: (jnp.maximum(i - 1, 0), 0))
    nxt = pl.BlockSpec((BLK, GA), lambda i: (jnp.minimum(i + 1, nbt - 1), 0))
    return _call(body, name, (nbt,), [cur] * 6 + [prev, prev] + [nxt] * 4, [cur, cur, cur],
                 [_sds((T, GA), F32), _sds((T, GA), F32), _sds((T, GA), BF16)])(q, k, v, do, lse, e, k, v, q, do, lse, e)


def _to_residue_major(t, dil):
    if dil == 1:
        return t
    T, w = t.shape
    return t.reshape(T // dil, dil, w).transpose(1, 0, 2).reshape(T, w)


def _from_residue_major(t, dil):
    if dil == 1:
        return t
    T, w = t.shape
    return t.reshape(dil, T // dil, w).transpose(1, 0, 2).reshape(T, w)


def _pool_consts(shape, row0):
    lane = lax.broadcasted_iota(jnp.int32, shape, 1)
    t = lax.broadcasted_iota(jnp.int32, shape, 0) + row0
    grp = lane // (PW // len(POOL_WINDOWS))
    win = jnp.where(grp == 0, POOL_WINDOWS[0], jnp.where(grp == 1, POOL_WINDOWS[1],
                    jnp.where(grp == 2, POOL_WINDOWS[2], POOL_WINDOWS[3])))
    cnt = jnp.minimum(t + 1, win).astype(F32)
    return grp, cnt


def _window_sums(ext_ref, base, step, tm):
    outs, run = [], None
    for j in range(POOL_WINDOWS[-1]):
        sl = ext_ref[pl.ds(base + step * j, tm), :]
        run = sl if run is None else run + sl
        if j + 1 in POOL_WINDOWS:
            outs.append(run)
    return outs


def _select_group(grp, vals):
    return jnp.where(grp == 0, vals[0], jnp.where(grp == 1, vals[1], jnp.where(grp == 2, vals[2], vals[3])))


def _pool_d(pc_ref, pp_ref, ext_ref, i, tm):
    ext_ref[0:HALO, :] = jnp.where(i > 0, pp_ref[tm - HALO:tm, :], 0.0)
    ext_ref[HALO:HALO + tm, :] = pc_ref[...]
    grp, cnt = _pool_consts((tm, PW), i * tm)
    sums = _window_sums(ext_ref, HALO, -1, tm)
    return _select_group(grp, sums) / cnt - pc_ref[...]


def _group_weights(lse_refs):
    ls = [r[...] for r in lse_refs]
    mx = jnp.maximum(jnp.maximum(ls[0], ls[1]), ls[2])
    es = [jnp.exp(l - mx) for l in ls]
    inv = 1.0 / (es[0] + es[1] + es[2])
    return [e * inv for e in es]


def _mix_merge(h, vec, p, os, lses, gates, wp_bd, pscale, wpb, wab, wout):
    T = h.shape[0]

    def body(h_ref, vec_ref, pc_ref, pp_ref, o0, o1, o2, l0, l1, l2, gates_ref, wp_ref, ps_ref, wpb_ref, wab_ref, wout_ref,
             ho_ref, yp_ref, ya_ref, mg_ref, mo_ref, d_ref, ext_ref):
        i = pl.program_id(0)
        gt = vec_ref[3:4, :]
        d = _pool_d(pc_ref, pp_ref, ext_ref, i, TM).astype(BF16)
        d_ref[...] = d
        ypool = (_dot(d, wp_ref[...]) * ps_ref[0:1, :]).astype(BF16)
        yp_ref[...] = ypool
        w = _group_weights((l0, l1, l2))
        yattn = (w[0] * o0[...] + w[1] * o1[...] + w[2] * o2[...]).astype(BF16)
        ya_ref[...] = yattn
        merged = (gates_ref[:, 0:D].astype(F32) * _dot(ypool, wpb_ref[...])
                  + gates_ref[:, D:GW].astype(F32) * _dot(yattn, wab_ref[...])).astype(BF16)
        mg_ref[...] = merged
        mo = _dot(merged, wout_ref[...])
        mo_ref[...] = mo.astype(BF16)
        ho_ref[...] = h_ref[...] + gt * mo

    prev = pl.BlockSpec((TM, PW), lambda i: (jnp.maximum(i - 1, 0), 0))
    return _call(
        body, "mix_merge", (T // TM,),
        [_rows(TM, D), _const((8, D)), _rows(TM, PW), prev] + [_rows(TM, GA)] * 6 + [_rows(TM, GW), _const((PW, PW)),
         _const((8, PW)), _const((PW, D)), _const((GA, D)), _const((D, D))],
        [_rows(TM, D), _rows(TM, PW), _rows(TM, GA), _rows(TM, D), _rows(TM, D), _rows(TM, PW)],
        [_sds((T, D), F32), _sds((T, PW), BF16), _sds((T, GA), BF16), _sds((T, D), BF16), _sds((T, D), BF16), _sds((T, PW), BF16)],
        scratch=[pltpu.VMEM((TM + HALO, PW), F32)],
        vmem=VMEM_BIG,
    )(h, vec, p, p, *os, *lses, gates, wp_bd, pscale, wpb, wab, wout)


def _mix_bwd_a(dh, vec, mixout, gates, ypool, yattn, dpool, os, lses, wp_bd, pscale, wpb, wab, wout, ones_bd):
    T = dh.shape[0]

    def body(dh_ref, vec_ref, mo_ref, gates_ref, yp_ref, ya_ref, d_ref, o0, o1, o2, l0, l1, l2,
             wp_ref, ps_ref, wpb_ref, wab_ref, wout_ref, ones_ref,
             dmo_ref, dp_ref, da_ref, dgates_ref, do0, do1, do2, e0, e1, e2, dd_ref, dyp_ref, acc_ref, acc2_ref):
        _zero_first(acc_ref)
        _zero_first(acc2_ref)
        gt = vec_ref[3:4, :]
        dho = dh_ref[...]
        acc_ref[3:4, :] += _colsum(dho * mo_ref[...].astype(F32))
        dmo = (gt * dho).astype(BF16)
        dmo_ref[...] = dmo
        dmerged = _dot_nt(dmo, wout_ref[...])
        gp = gates_ref[:, 0:D].astype(F32)
        ga = gates_ref[:, D:GW].astype(F32)
        bp = _dot(yp_ref[...], wpb_ref[...])
        ba = _dot(ya_ref[...], wab_ref[...])
        dgates_ref[:, 0:D] = (dmerged * bp * gp * (1.0 - gp)).astype(BF16)
        dgates_ref[:, D:GW] = (dmerged * ba * ga * (1.0 - ga)).astype(BF16)
        dbp = (dmerged * gp).astype(BF16)
        dba = (dmerged * ga).astype(BF16)
        dp_ref[...] = dbp
        da_ref[...] = dba
        dypool = _dot_nt(dbp, wpb_ref[...])
        ypre = _dot(d_ref[...], wp_ref[...])
        acc2_ref[0:1, :] += _colsum(dypool * ypre)
        dyp = (dypool * ps_ref[0:1, :]).astype(BF16)
        dyp_ref[...] = dyp
        dd_ref[...] = _dot_nt(dyp, wp_ref[...])
        dya = _dot_nt(dba, wab_ref[...])
        w = _group_weights((l0, l1, l2))
        ovals = (o0[...], o1[...], o2[...])
        ya = w[0] * ovals[0] + w[1] * ovals[1] + w[2] * ovals[2]
        prod = dya * ya
        hi = prod.astype(BF16)
        lo = (prod - hi.astype(F32)).astype(BF16)
        tot = _dot(hi, ones_ref[...]) + _dot(lo, ones_ref[...])
        for wg, do_ref, e_ref in zip(w, (do0, do1, do2), (e0, e1, e2)):
            do_ref[...] = (wg * dya).astype(BF16)
            e_ref[...] = -wg * tot

    return _call(
        body, "mix_bwd_a", (T // TM,),
        [_rows(TM, D), _const((8, D)), _rows(TM, D), _rows(TM, GW), _rows(TM, PW), _rows(TM, GA), _rows(TM, PW)]
        + [_rows(TM, GA)] * 6
        + [_const((PW, PW)), _const((8, PW)), _const((PW, D)), _const((GA, D)), _const((D, D)), _const((GA, GA))],
        [_rows(TM, D)] * 3 + [_rows(TM, GW)] + [_rows(TM, GA)] * 6 + [_rows(TM, PW), _rows(TM, PW), _const((8, D)), _const((8, PW))],
        [_sds((T, D), BF16)] * 3 + [_sds((T, GW), BF16)] + [_sds((T, GA), BF16)] * 3 + [_sds((T, GA), F32)] * 3
        + [_sds((T, PW), F32), _sds((T, PW), BF16), _sds((8, D), F32), _sds((8, PW), F32)],
        vmem=VMEM_BIG,
    )(dh, vec, mixout, gates, ypool, yattn, dpool, *os, *lses, wp_bd, pscale, wpb, wab, wout, ones_bd)


def _mix_bwd_b(dh, h, vec, dd, dqs, dks, dvs, dgates, cos, sin, win):
    T = h.shape[0]
    nt = T // TM

    def body(dh_ref, h_ref, vec_ref, ddc_ref, ddn_ref, *rest):
        qk_refs, dv_refs = rest[:2 * NG], rest[2 * NG:3 * NG]
        dgates_ref, cos_ref, sin_ref, win_hbm, dhi_ref, dproj_ref, acc_ref, win_v, ext_ref, sems = rest[3 * NG:]
        i = pl.program_id(0)
        _load_once([(win_hbm, win_v)], sems)
        _zero_first(acc_ref)
        g, sh, sc = vec_ref[0:1, :], vec_ref[1:2, :], vec_ref[2:3, :]
        grp, cnt = _pool_consts((TM, PW), i * TM)
        _, cnt_n = _pool_consts((HALO, PW), (i + 1) * TM)
        ext_ref[0:TM, :] = ddc_ref[...] / cnt
        ext_ref[TM:TM + HALO, :] = jnp.where(i < nt - 1, ddn_ref[0:HALO, :] / cnt_n, 0.0)
        dp = _select_group(grp, _window_sums(ext_ref, 0, 1, TM)) - ddc_ref[...]
        dproj_ref[:, 0:PW] = dp.astype(BF16)
        cos_t, sin_t = cos_ref[...], sin_ref[...]
        for j in range(2 * NG):
            col = PW + j * GA
            dproj_ref[:, col:col + GA] = _rope_bwd(qk_refs[j][...], cos_t, sin_t).astype(BF16)
        for j in range(NG):
            col = PW + (2 * NG + j) * GA
            dproj_ref[:, col:col + GA] = dv_refs[j][...]
        dproj_ref[:, PW + 3 * NG * GA:INW] = dgates_ref[...]
        du = None
        for j in range(INW // 512):
            part = _dot_nt(dproj_ref[:, j * 512:(j + 1) * 512], win_v[:, j * 512:(j + 1) * 512])
            du = part if du is None else du + part
        xh, r, n, _ = _norm_fwd(h_ref[...], g, sh, sc)
        dhn, dsh, dsc, dg = _norm_bwd(du, xh, r, n, g, sc)
        dhi_ref[...] = dh_ref[...] + dhn
        acc_ref[0:1, :] += dsh
        acc_ref[1:2, :] += dsc
        acc_ref[2:3, :] += dg

    nxt = pl.BlockSpec((TM, PW), lambda i: (jnp.minimum(i + 1, nt - 1), 0))
    return _call(
        body, "mix_bwd_b", (nt,),
        [_rows(TM, D), _rows(TM, D), _const((8, D)), _rows(TM, PW), nxt] + [_rows(TM, GA)] * (3 * NG)
        + [_rows(TM, GW), _rows(TM, 128), _rows(TM, 128), ANY],
        [_rows(TM, D), _rows(TM, INW), _const((8, D))],
        [_sds((T, D), F32), _sds((T, INW), BF16), _sds((8, D), F32)],
        scratch=[pltpu.VMEM((D, INW), BF16), pltpu.VMEM((TM + HALO, PW), F32), pltpu.SemaphoreType.DMA((1,))],
        vmem=VMEM_BIG,
    )(dh, h, vec, dd, dd, *dqs, *dks, *dvs, dgates, cos, sin, win)


def _ada_fwd(c_all, w_shard, b_shard):
    n = w_shard.shape[1]

    def body(c_ref, w_ref, b_ref, o_ref):
        cv = c_ref[...]
        cond = (cv * jax.nn.sigmoid(cv)).astype(BF16)
        o_ref[...] = _dot(cond, w_ref[...].astype(BF16)) + b_ref[...]

    tn = n // 3
    return pl.pallas_call(
        body, name="ada_fwd", grid=(3,),
        in_specs=[pl.BlockSpec((8, D), lambda j: (0, 0)), pl.BlockSpec((D, tn), lambda j: (0, j)), pl.BlockSpec((1, tn), lambda j: (0, j))],
        out_specs=pl.BlockSpec((8, tn), lambda j: (0, j)), out_shape=_sds((8, n), F32),
        compiler_params=pltpu.CompilerParams(dimension_semantics=("arbitrary",)),
    )(c_all, w_shard, b_shard)


def _ada_bwd(c_all, dmod_shard):
    n = dmod_shard.shape[1]

    def body(c_ref, d_ref, o_ref):
        cv = c_ref[...]
        cond = (cv * jax.nn.sigmoid(cv)).astype(BF16)
        o_ref[...] = _dot_tn(cond, d_ref[...].astype(BF16))

    tn = n // 3
    return pl.pallas_call(
        body, name="ada_bwd", grid=(3,),
        in_specs=[pl.BlockSpec((8, D), lambda j: (0, 0)), pl.BlockSpec((8, tn), lambda j: (0, j))],
        out_specs=pl.BlockSpec((D, tn), lambda j: (0, j)), out_shape=_sds((D, n), F32),
        compiler_params=pltpu.CompilerParams(dimension_semantics=("arbitrary",)),
    )(c_all, dmod_shard)


def _adam_math(w, g, m, v):
    m2 = B1 * m + (1.0 - B1) * g
    v2 = B2 * v + (1.0 - B2) * (g * g)
    m_hat = m2 / (1.0 - B1 ** STEP)
    v_hat = v2 / (1.0 - B2 ** STEP)
    delta = -LR * (m_hat / (jnp.sqrt(v_hat) + AEPS) + WD * w)
    return delta, m2, v2


def _adam(w, m, v, parts, name):
    R, C = w.shape
    tr = R
    for cand in (128, 64, 32, 16, 8):
        if R % cand == 0:
            tr = cand
            break
    np_ = len(parts)

    def body(w_ref, m_ref, v_ref, *rest):
        p_refs, (g_ref, d_ref, m2_ref, v2_ref) = rest[:np_], rest[np_:]
        g = p_refs[0][...]
        for pr in p_refs[1:]:
            g = g + pr[...]
        delta, m2, v2 = _adam_math(w_ref[...], g, m_ref[...], v_ref[...])
        g_ref[...] = g
        d_ref[...] = delta
        m2_ref[...] = m2
        v2_ref[...] = v2

    spec = pl.BlockSpec((tr, C), lambda i: (i, 0))
    return pl.pallas_call(
        body, name=name, grid=(R // tr,), in_specs=[spec] * (3 + np_), out_specs=[spec] * 4,
        out_shape=[_sds((R, C), F32)] * 4,
        compiler_params=pltpu.CompilerParams(dimension_semantics=("arbitrary",), vmem_limit_bytes=VMEM_BIG),
    )(w, m, v, *parts)


def _adam_small(w, m, v, gathered):
    P = w.shape[1]

    def body(w_ref, m_ref, v_ref, ga_ref, g_ref, d_ref, m2_ref, v2_ref):
        g = ga_ref[0]
        for dev in range(1, 8):
            g = g + ga_ref[dev]
        delta, m2, v2 = _adam_math(w_ref[...], g, m_ref[...], v_ref[...])
        g_ref[...] = g
        d_ref[...] = delta
        m2_ref[...] = m2
        v2_ref[...] = v2

    return pl.pallas_call(body, name="adam_small", out_shape=[_sds((1, P), F32)] * 4)(w, m, v, gathered)


def _sum4(own, recv, name):
    R, C = own.shape
    tr = R
    for cand in (256, 128, 64, 32, 16):
        if R % cand == 0:
            tr = cand
            break

    def body(o_ref, r_ref, out_ref):
        out_ref[...] = ((o_ref[...].astype(F32) + r_ref[0].astype(F32)) + r_ref[1].astype(F32)) + r_ref[2].astype(F32)

    return pl.pallas_call(
        body, name=name, grid=(R // tr,),
        in_specs=[pl.BlockSpec((tr, C), lambda i: (i, 0)), pl.BlockSpec((3, tr, C), lambda i: (0, i, 0))],
        out_specs=pl.BlockSpec((tr, C), lambda i: (i, 0)), out_shape=_sds((R, C), F32),
        compiler_params=pltpu.CompilerParams(dimension_semantics=("arbitrary",)),
    )(own, recv)


def _place():
    return lax.axis_index("x"), lax.axis_index("y"), lax.axis_index("c")


def _gather_small(v):
    R, P = v.shape

    def body(v_ref, out_ref, send_sems, recv_sems):
        x, y, c = _place()
        me = 4 * x + 2 * y + c
        out_ref[me] = v_ref[...]
        copies = []
        for m in range(1, 8):
            peer = (x ^ (m >> 2), y ^ ((m >> 1) & 1), c ^ (m & 1))
            copies.append(pltpu.make_async_remote_copy(
                src_ref=v_ref, dst_ref=out_ref.at[me], send_sem=send_sems.at[m - 1], recv_sem=recv_sems.at[m - 1],
                device_id=peer, device_id_type=MESH))
        for cp in copies:
            cp.start()
        for m in range(1, 8):
            src = 4 * (x ^ (m >> 2)) + 2 * (y ^ ((m >> 1) & 1)) + (c ^ (m & 1))
            pltpu.make_async_remote_copy(
                src_ref=v_ref, dst_ref=out_ref.at[src], send_sem=send_sems.at[m - 1], recv_sem=recv_sems.at[m - 1],
                device_id=(x, y, c), device_id_type=MESH).wait_recv()
        for cp in copies:
            cp.wait_send()

    vm = pl.BlockSpec(memory_space=pltpu.VMEM)
    return pl.pallas_call(
        body, name="gather_small", in_specs=[vm], out_specs=vm, out_shape=_sds((8, R, P), F32),
        scratch_shapes=[pltpu.SemaphoreType.DMA((7,)), pltpu.SemaphoreType.DMA((7,))],
    )(v)


def _chip_peer(x, y, c, m):
    return (x ^ (m >> 1), y ^ (m & 1), c)


def _shard_ref(ref, axis, k, n):
    start = pl.multiple_of(k * n, 128 if axis == 1 else 16)
    return ref.at[:, pl.ds(start, n)] if axis == 1 else ref.at[pl.ds(start, n), :]


def _gather_weights(shards, axes):
    nw = len(shards)
    out_shapes = [_sds((s.shape[0] * (4 if ax == 0 else 1), s.shape[1] * (4 if ax == 1 else 1)), BF16)
                  for s, ax in zip(shards, axes)]

    def body(*refs):
        ins, outs = refs[:nw], refs[nw:2 * nw]
        local_sems, send_sems, recv_sems = refs[2 * nw:]
        x, y, c = _place()
        k = 2 * x + y
        local, remote = [], []
        for j in range(nw):
            n = ins[j].shape[axes[j]]
            mine = _shard_ref(outs[j], axes[j], k, n)
            local.append(pltpu.make_async_copy(ins[j], mine, local_sems.at[j]))
            for m in range(1, 4):
                remote.append(pltpu.make_async_remote_copy(
                    src_ref=ins[j], dst_ref=mine, send_sem=send_sems.at[j, m - 1], recv_sem=recv_sems.at[j, m - 1],
                    device_id=_chip_peer(x, y, c, m), device_id_type=MESH))
        for cp in local + remote:
            cp.start()
        for j in range(nw):
            n = ins[j].shape[axes[j]]
            for m in range(1, 4):
                theirs = _shard_ref(outs[j], axes[j], k ^ m, n)
                pltpu.make_async_remote_copy(
                    src_ref=ins[j], dst_ref=theirs, send_sem=send_sems.at[j, m - 1], recv_sem=recv_sems.at[j, m - 1],
                    device_id=(x, y, c), device_id_type=MESH).wait_recv()
        for cp in remote:
            cp.wait_send()
        for cp in local:
            cp.wait()

    return pl.pallas_call(
        body, name="gather_weights", in_specs=[ANY] * nw, out_specs=[ANY] * nw, out_shape=out_shapes,
        scratch_shapes=[pltpu.SemaphoreType.DMA((nw,)), pltpu.SemaphoreType.DMA((nw, 3)), pltpu.SemaphoreType.DMA((nw, 3))],
    )(*shards)


def _scatter_grads(grads, axes):
    nw = len(grads)
    shard_shapes = [(g.shape[0] // (4 if ax == 0 else 1), g.shape[1] // (4 if ax == 1 else 1)) for g, ax in zip(grads, axes)]

    def body(*refs):
        ins, outs = refs[:nw], refs[nw:2 * nw]
        send_sems, recv_sems = refs[2 * nw:]
        x, y, c = _place()
        k = 2 * x + y
        copies = []
        for j in range(nw):
            n = shard_shapes[j][axes[j]]
            for m in range(1, 4):
                copies.append(pltpu.make_async_remote_copy(
                    src_ref=_shard_ref(ins[j], axes[j], k ^ m, n), dst_ref=outs[j].at[m - 1],
                    send_sem=send_sems.at[j, m - 1], recv_sem=recv_sems.at[j, m - 1],
                    device_id=_chip_peer(x, y, c, m), device_id_type=MESH))
        for cp in copies:
            cp.start()
        for j in range(nw):
            n = shard_shapes[j][axes[j]]
            for m in range(1, 4):
                pltpu.make_async_remote_copy(
                    src_ref=_shard_ref(ins[j], axes[j], k, n), dst_ref=outs[j].at[m - 1],
                    send_sem=send_sems.at[j, m - 1], recv_sem=recv_sems.at[j, m - 1],
                    device_id=(x, y, c), device_id_type=MESH).wait_recv()
        for cp in copies:
            cp.wait_send()

    return pl.pallas_call(
        body, name="scatter_grads", in_specs=[ANY] * nw, out_specs=[ANY] * nw,
        out_shape=[_sds((3,) + s, BF16) for s in shard_shapes],
        scratch_shapes=[pltpu.SemaphoreType.DMA((nw, 3)), pltpu.SemaphoreType.DMA((nw, 3))],
    )(*grads)


def _swap_sibling(parts):
    nw = len(parts)

    def body(*refs):
        ins, outs = refs[:nw], refs[nw:2 * nw]
        send_sems, recv_sems = refs[2 * nw:]
        x, y, c = _place()
        copies = [pltpu.make_async_remote_copy(
            src_ref=ins[j], dst_ref=outs[j], send_sem=send_sems.at[j], recv_sem=recv_sems.at[j],
            device_id=(x, y, 1 - c), device_id_type=MESH) for j in range(nw)]
        for cp in copies:
            cp.start()
        for cp in copies:
            cp.wait()

    return pl.pallas_call(
        body, name="swap_sibling", in_specs=[ANY] * nw, out_specs=[ANY] * nw,
        out_shape=[_sds(p.shape, p.dtype) for p in parts],
        scratch_shapes=[pltpu.SemaphoreType.DMA((nw,)), pltpu.SemaphoreType.DMA((nw,))],
    )(*parts)


BIG = ("w_ffn1_in", "w_ffn1_out", "w_in", "w_pool_branch", "w_attn_branch", "w_out", "w_ffn2_in", "w_ffn2_out")
BIG_AXIS = {"w_ffn1_in": 1, "w_ffn1_out": 0, "w_in": 1, "w_pool_branch": 1, "w_attn_branch": 1, "w_out": 0,
            "w_ffn2_in": 1, "w_ffn2_out": 0}


def _vec(rows):
    pad = [jnp.zeros((1, D), F32)] * (8 - len(rows))
    return jnp.concatenate([r.reshape(1, D) for r in rows] + pad, axis=0)


def _block_diag(w_pool):
    n = len(POOL_WINDOWS)
    c = PW // n
    eye = jnp.eye(n, dtype=w_pool.dtype)
    return (eye[:, None, :, None] * w_pool[:, :, None, :]).reshape(PW, PW)


def _example_step(x, tgt, positions, mod, gains, w_pool, pool_scale, wb):
    T = x.shape[0]
    sh1, sc1, gt1, sh2, sc2, gt2, sh3, sc3, gt3 = [mod[j * D:(j + 1) * D] for j in range(NMOD)]
    g1, g2, g3, gf = gains
    vec1, vec2, vec3 = _vec([g1, sh1, sc1, gt1]), _vec([g2, sh2, sc2, gt2]), _vec([g3, sh3, sc3, gt3])
    inv_freq = 10000.0 ** (-jnp.arange(0, HD, 2, dtype=F32) / HD)
    ang = positions.astype(F32)[:, None] * inv_freq
    cos = jnp.tile(jnp.cos(ang), (1, 4))
    sin = jnp.tile(jnp.concatenate([-jnp.sin(ang), jnp.sin(ang)], axis=1), (1, 2))
    wp_bd = _block_diag(w_pool).astype(BF16)
    ones_bd = _block_diag(jnp.ones((GA // HD, HD, HD), F32)).astype(BF16)
    ps = jnp.concatenate([pool_scale.reshape(1, PW), jnp.zeros((7, PW), F32)], axis=0)

    h1, u1, a1, b1, f1 = _ffn_fwd(x, vec1, wb["w_ffn1_in"], wb["w_ffn1_out"], "ffn1_fwd")
    u2, p, qs, ks, vs, gates = _mix_proj(h1, vec2, wb["w_in"], cos, sin)
    qs = [_to_residue_major(t, d) for t, d in zip(qs, DIL)]
    ks = [_to_residue_major(t, d) for t, d in zip(ks, DIL)]
    vs = [_to_residue_major(t, d) for t, d in zip(vs, DIL)]
    nbs = [T // d // BLK for d in DIL]
    os, lses = [], []
    for gi in range(NG):
        o, lse = _attn_fwd(qs[gi], ks[gi], vs[gi], nbs[gi], f"attn_fwd{gi}")
        os.append(o)
        lses.append(lse)
    os_n = [_from_residue_major(t, d) for t, d in zip(os, DIL)]
    lses_n = [_from_residue_major(t, d) for t, d in zip(lses, DIL)]
    h2, ypool, yattn, merged, mixout, dpool = _mix_merge(
        h1, vec2, p, os_n, lses_n, gates, wp_bd, ps, wb["w_pool_branch"], wb["w_attn_branch"], wb["w_out"])
    h3, u3, a3, b3, f3 = _ffn_fwd(h2, vec3, wb["w_ffn2_in"], wb["w_ffn2_out"], "ffn2_fwd")
    dh3, lacc = _final_loss(h3, tgt, _vec([gf]))
    loss = 0.5 * jnp.sum(lacc[0]) / D

    grads = {}
    dh2, dab3, s3, df3, acc3 = _ffn_bwd(dh3, h2, a3, b3, f3, vec3, wb["w_ffn2_in"], wb["w_ffn2_out"], "ffn2_bwd")
    grads["w_ffn2_in"] = _wgrad(u3, dab3, "wg_ffn2_in", D, 512, 1024)
    grads["w_ffn2_out"] = _wgrad(s3, df3, "wg_ffn2_out", FC, 512, 1024)
    (dmo, dbp, dba, dgates, do0, do1, do2, e0, e1, e2, dd, dyp, acc2a, accps) = _mix_bwd_a(
        dh2, vec2, mixout, gates, ypool, yattn, dpool, os_n, lses_n, wp_bd, ps,
        wb["w_pool_branch"], wb["w_attn_branch"], wb["w_out"], ones_bd)
    grads["w_out"] = _wgrad(merged, dmo, "wg_out", D, 512, 1024)
    grads["w_pool_branch"] = _wgrad(ypool, dbp, "wg_pool_branch", PW, 512, 1024)
    grads["w_attn_branch"] = _wgrad(yattn, dba, "wg_attn_branch", GA, 512, 1024)
    gwp = _wgrad(dpool, dyp, "wg_pool", PW, PW, 1024, out_dtype=F32)
    n = len(POOL_WINDOWS)
    c = PW // n
    grad_w_pool = jnp.stack([gwp[j * c:(j + 1) * c, j * c:(j + 1) * c] for j in range(n)], axis=0)
    dqs, dks, dvs = [], [], []
    for gi, (do, e) in enumerate(((do0, e0), (do1, e1), (do2, e2))):
        dq, dk, dv = _attn_bwd(qs[gi], ks[gi], vs[gi], _to_residue_major(do, DIL[gi]), lses[gi],
                               _to_residue_major(e, DIL[gi]), nbs[gi], f"attn_bwd{gi}")
        dqs.append(_from_residue_major(dq, DIL[gi]))
        dks.append(_from_residue_major(dk, DIL[gi]))
        dvs.append(_from_residue_major(dv, DIL[gi]))
    dh1, dproj, acc2b = _mix_bwd_b(dh2, h1, vec2, dd, dqs, dks, dvs, dgates, cos, sin, wb["w_in"])
    grads["w_in"] = _wgrad(u2, dproj, "wg_in", D, 512, 1024)
    dx, dab1, s1, df1, acc1 = _ffn_bwd(dh1, x, a1, b1, f1, vec1, wb["w_ffn1_in"], wb["w_ffn1_out"], "ffn1_bwd")
    grads["w_ffn1_in"] = _wgrad(u1, dab1, "wg_ffn1_in", D, 512, 1024)
    grads["w_ffn1_out"] = _wgrad(s1, df1, "wg_ffn1_out", FC, 512, 1024)

    dmod = jnp.concatenate([acc1[0], acc1[1], acc1[3], acc2b[0], acc2b[1], acc2a[3], acc3[0], acc3[1], acc3[3]])
    dgains = jnp.stack([acc1[2], acc2b[2], acc3[2], lacc[1]], axis=0)
    return loss, dx, dmod, dgains, grad_w_pool, accps[0], grads


SMALL = ("b_ada", "g_norm_ffn1", "g_norm_mix", "g_norm_ffn2", "g_final", "pool_scale", "w_pool")
WEIGHTS = ("w_ada", "b_ada", "g_norm_ffn1", "w_ffn1_in", "w_ffn1_out", "g_norm_mix", "w_in", "w_pool", "pool_scale",
           "w_pool_branch", "w_attn_branch", "w_out", "g_norm_ffn2", "w_ffn2_in", "w_ffn2_out", "g_final")


def _pack_small(t):
    return jnp.concatenate([t[n].reshape(-1) for n in SMALL]).reshape(1, -1)


def _unpack_small(flat, like):
    out, off = {}, 0
    for n in SMALL:
        size = like[n].size
        out[n] = flat[0, off:off + size].reshape(like[n].shape)
        off += size
    return out


def kernel(x, c, positions, w_ada, b_ada, g_norm_ffn1, w_ffn1_in, w_ffn1_out, g_norm_mix, w_in, w_pool, pool_scale, w_pool_branch, w_attn_branch, w_out, g_norm_ffn2, w_ffn2_in, w_ffn2_out, g_final, loss_target, m_w_ada, m_b_ada, m_g_norm_ffn1, m_w_ffn1_in, m_w_ffn1_out, m_g_norm_mix, m_w_in, m_w_pool, m_pool_scale, m_w_pool_branch, m_w_attn_branch, m_w_out, m_g_norm_ffn2, m_w_ffn2_in, m_w_ffn2_out, m_g_final, v_w_ada, v_b_ada, v_g_norm_ffn1, v_w_ffn1_in, v_w_ffn1_out, v_g_norm_mix, v_w_in, v_w_pool, v_pool_scale, v_w_pool_branch, v_w_attn_branch, v_w_out, v_g_norm_ffn2, v_w_ffn2_in, v_w_ffn2_out, v_g_final):
    w = dict(w_ada=w_ada, b_ada=b_ada, g_norm_ffn1=g_norm_ffn1, w_ffn1_in=w_ffn1_in, w_ffn1_out=w_ffn1_out,
             g_norm_mix=g_norm_mix, w_in=w_in, w_pool=w_pool, pool_scale=pool_scale, w_pool_branch=w_pool_branch,
             w_attn_branch=w_attn_branch, w_out=w_out, g_norm_ffn2=g_norm_ffn2, w_ffn2_in=w_ffn2_in,
             w_ffn2_out=w_ffn2_out, g_final=g_final)
    mom = dict(w_ada=m_w_ada, b_ada=m_b_ada, g_norm_ffn1=m_g_norm_ffn1, w_ffn1_in=m_w_ffn1_in, w_ffn1_out=m_w_ffn1_out,
               g_norm_mix=m_g_norm_mix, w_in=m_w_in, w_pool=m_w_pool, pool_scale=m_pool_scale,
               w_pool_branch=m_w_pool_branch, w_attn_branch=m_w_attn_branch, w_out=m_w_out, g_norm_ffn2=m_g_norm_ffn2,
               w_ffn2_in=m_w_ffn2_in, w_ffn2_out=m_w_ffn2_out, g_final=m_g_final)
    var = dict(w_ada=v_w_ada, b_ada=v_b_ada, g_norm_ffn1=v_g_norm_ffn1, w_ffn1_in=v_w_ffn1_in, w_ffn1_out=v_w_ffn1_out,
               g_norm_mix=v_g_norm_mix, w_in=v_w_in, w_pool=v_w_pool, pool_scale=v_pool_scale,
               w_pool_branch=v_w_pool_branch, w_attn_branch=v_w_attn_branch, w_out=v_w_out, g_norm_ffn2=v_g_norm_ffn2,
               w_ffn2_in=v_w_ffn2_in, w_ffn2_out=v_w_ffn2_out, g_final=v_g_final)
    ix, iy, ic = _place()
    chip = 2 * ix + iy
    me = 4 * ix + 2 * iy + ic
    nada = w_ada.shape[2]

    c_all = _gather_small(c)[:, 0, :]
    b_shard = lax.dynamic_slice_in_dim(b_ada, chip * nada, nada, axis=1)
    mod_cols = _ada_fwd(c_all, w_ada[0], b_shard)
    mod_all = _gather_small(mod_cols)
    mod = jnp.concatenate([lax.dynamic_index_in_dim(mod_all[4 * (kk >> 1) + 2 * (kk & 1)], me, axis=0, keepdims=False)
                           for kk in range(4)])

    full = _gather_weights([w[n][0].astype(BF16) for n in BIG], [BIG_AXIS[n] for n in BIG])
    wb = dict(zip(BIG, full))

    loss, dx, dmod, dgains, g_w_pool, g_pool_scale, grads = _example_step(
        x[0], loss_target[0], positions[0], mod, (g_norm_ffn1[0], g_norm_mix[0], g_norm_ffn2[0], g_final),
        w_pool[0], pool_scale[0], wb)

    small_g = dict(b_ada=dmod, g_norm_ffn1=dgains[0], g_norm_mix=dgains[1], g_norm_ffn2=dgains[2], g_final=dgains[3],
                   pool_scale=g_pool_scale, w_pool=g_w_pool)
    gathered = _gather_small(_pack_small(small_g))
    sg, sd, sm, sv = _adam_small(_pack_small(w), _pack_small(mom), _pack_small(var), gathered)
    small_out = [_unpack_small(t, w) for t in (sg, sd, sm, sv)]

    dmod_all = gathered[:, 0, :NMOD * D]
    dmod_cols = lax.dynamic_slice_in_dim(dmod_all, chip * nada, nada, axis=1)
    g_ada = _ada_bwd(c_all, dmod_cols)
    ada_out = _adam(w_ada[0], m_w_ada[0], v_w_ada[0], [g_ada], "adam_w_ada")

    axes = [BIG_AXIS[n] for n in BIG]
    recv = _scatter_grads([grads[n] for n in BIG], axes)
    partial = []
    for n, ax, r in zip(BIG, axes, recv):
        size = r.shape[1 + ax]
        own = lax.dynamic_slice_in_dim(grads[n], chip * size, size, axis=ax)
        partial.append(_sum4(own, r, "sum_" + n))
    other = _swap_sibling(partial)
    big_out = {n: _adam(w[n][0], mom[n][0], var[n][0], [pa, pb], "adam_" + n) for n, pa, pb in zip(BIG, partial, other)}

    def leaf(kind, n):
        if n == "w_ada":
            return ada_out[kind][None]
        if n in big_out:
            return big_out[n][kind][None]
        return small_out[kind][n]

    loss = lax.psum(loss, ("x", "y", "c"))
    return (loss, dx[None], *[leaf(kind, n) for kind in range(4) for n in WEIGHTS])
```

```python
import jax
import jax.numpy as jnp
from jax import lax
from jax.experimental import pallas as pl
from jax.experimental.pallas import tpu as pltpu

F32 = jnp.float32
BF16 = jnp.bfloat16

D = 1024
FF = 2816
FC = 1408
PW = 256
GA = 256
HD = 64
NH = GA // HD
NG = 3
DIL = (1, 4, 16)
BLK = 128
GW = 2 * D
INW = PW + 3 * NG * GA + GW
NMOD = 9
POOL_WINDOWS = (2, 4, 8, 16)
HALO = 16
EPS = 1e-6
SCALE = HD ** -0.5
NEG = -1e30

LR, B1, B2, AEPS, WD, STEP = 0.001, 0.9, 0.999, 1e-08, 0.01, 10

VMEM_BIG = 56 * 1024 * 1024
TM = 256

MESH = pl.DeviceIdType.MESH
ANY = pl.BlockSpec(memory_space=pl.ANY)


def _call(body, name, grid, in_specs, out_specs, out_shape, scratch=(), vmem=None, comm=None):
    params = pltpu.CompilerParams(dimension_semantics=("arbitrary",) * len(grid), vmem_limit_bytes=vmem)
    n_in, n_out, n_scr = len(in_specs), len(out_shape), len(scratch)
    if comm is None:
        call = pl.pallas_call(body, name=name, grid=grid, in_specs=list(in_specs), out_specs=list(out_specs),
                              out_shape=list(out_shape), scratch_shapes=list(scratch), compiler_params=params)
        return lambda *args: (call(*args), ())
    nc = len(comm.inputs)

    def body_with_comm(*refs):
        ins, refs = refs[:n_in], refs[n_in:]
        c_ins, refs = refs[:nc], refs[nc:]
        outs, refs = refs[:n_out], refs[n_out:]
        c_outs, refs = refs[:nc], refs[nc:]
        scr, sems = refs[:n_scr], refs[n_scr:]
        first = pl.program_id(0) == 0
        last = pl.program_id(0) == grid[0] - 1
        for ax in range(1, len(grid)):
            first = jnp.logical_and(first, pl.program_id(ax) == 0)
            last = jnp.logical_and(last, pl.program_id(ax) == grid[ax] - 1)

        @pl.when(first)
        def _():
            comm.start(c_ins, c_outs, sems)

        body(*ins, *outs, *scr)

        @pl.when(last)
        def _():
            comm.wait(c_ins, c_outs, sems)

    call = pl.pallas_call(
        body_with_comm, name=name, grid=grid, in_specs=list(in_specs) + [ANY] * nc,
        out_specs=list(out_specs) + [ANY] * nc, out_shape=list(out_shape) + list(comm.out_shapes),
        scratch_shapes=list(scratch) + list(comm.sem_shapes), compiler_params=params)

    def run(*args):
        res = call(*args, *comm.inputs)
        return res[:n_out], res[n_out:]

    return run


def _rows(tm, n):
    return pl.BlockSpec((tm, n), lambda i: (i, 0))


def _const(shape):
    return pl.BlockSpec(shape, lambda i: (0,) * len(shape))


def _sds(shape, dtype):
    return jax.ShapeDtypeStruct(shape, dtype)


def _dot(a, b):
    return jnp.dot(a, b, preferred_element_type=F32)


def _dot_nt(a, b):
    return lax.dot_general(a, b, (((1,), (1,)), ((), ())), preferred_element_type=F32)


def _dot_tn(a, b):
    return lax.dot_general(a, b, (((0,), (0,)), ((), ())), preferred_element_type=F32)


def _colsum(v):
    return jnp.sum(v, axis=0, keepdims=True)


def _norm_fwd(h, g, sh, sc):
    r = lax.rsqrt(jnp.mean(h * h, axis=-1, keepdims=True) + EPS)
    xh = h * r
    n = xh * g
    return xh, r, n, n * (1.0 + sc) + sh


def _norm_bwd(du, xh, r, n, g, sc):
    dn = du * (1.0 + sc)
    dxh = dn * g
    dh = r * (dxh - xh * jnp.mean(dxh * xh, axis=-1, keepdims=True))
    return dh, _colsum(du), _colsum(du * n), _colsum(dn * xh)


def _load_once(pairs, sems):
    @pl.when(pl.program_id(0) == 0)
    def _():
        cps = [pltpu.make_async_copy(src, dst, sems.at[j]) for j, (src, dst) in enumerate(pairs)]
        for cp in cps:
            cp.start()
        for cp in cps:
            cp.wait()


def _zero_first(ref):
    @pl.when(pl.program_id(0) == 0)
    def _():
        ref[...] = jnp.zeros(ref.shape, ref.dtype)


def _ffn_fwd(h, vec, win, wout, name, comm=None):
    T = h.shape[0]

    def body(h_ref, vec_ref, win_hbm, wout_hbm, ho_ref, u_ref, a_ref, b_ref, f_ref, win_v, wout_v, sems):
        _load_once([(win_hbm, win_v), (wout_hbm, wout_v)], sems)
        hh = h_ref[...]
        g, sh, sc, gt = vec_ref[0:1, :], vec_ref[1:2, :], vec_ref[2:3, :], vec_ref[3:4, :]
        _, _, _, u = _norm_fwd(hh, g, sh, sc)
        ub = u.astype(BF16)
        u_ref[...] = ub
        acc = None
        for j in range(FF // FC):
            lo, hi = j * FC, (j + 1) * FC
            a = _dot(ub, win_v[:, lo:hi])
            b = _dot(ub, win_v[:, FF + lo:FF + hi])
            a_ref[:, lo:hi] = a.astype(BF16)
            b_ref[:, lo:hi] = b.astype(BF16)
            s = (a * jax.nn.sigmoid(a) * b).astype(BF16)
            part = _dot(s, wout_v[lo:hi, :])
            acc = part if acc is None else acc + part
        f_ref[...] = acc.astype(BF16)
        ho_ref[...] = hh + 0.5 * gt * acc

    return _call(
        body, name, (T // TM,),
        [_rows(TM, D), _const((8, D)), ANY, ANY],
        [_rows(TM, D), _rows(TM, D), _rows(TM, FF), _rows(TM, FF), _rows(TM, D)],
        [_sds((T, D), F32), _sds((T, D), BF16), _sds((T, FF), BF16), _sds((T, FF), BF16), _sds((T, D), BF16)],
        scratch=[pltpu.VMEM((D, 2 * FF), BF16), pltpu.VMEM((FF, D), BF16), pltpu.SemaphoreType.DMA((2,))],
        vmem=VMEM_BIG, comm=comm,
    )(h, vec, win, wout)


def _ffn_bwd(dh, h, a, b, f, vec, win, wout, name, comm=None):
    T = h.shape[0]

    def body(dh_ref, h_ref, a_ref, b_ref, f_ref, vec_ref, win_hbm, wout_hbm,
             dhi_ref, dab_ref, s_ref, df_ref, acc_ref, win_v, wout_v, sems):
        _load_once([(win_hbm, win_v), (wout_hbm, wout_v)], sems)
        _zero_first(acc_ref)
        g, sh, sc, gt = vec_ref[0:1, :], vec_ref[1:2, :], vec_ref[2:3, :], vec_ref[3:4, :]
        dho = dh_ref[...]
        df = (0.5 * gt * dho).astype(BF16)
        df_ref[...] = df
        dgt = _colsum(0.5 * dho * f_ref[...].astype(F32))
        du = None
        for j in range(FF // FC):
            lo, hi = j * FC, (j + 1) * FC
            av = a_ref[:, lo:hi].astype(F32)
            bv = b_ref[:, lo:hi].astype(F32)
            ds = _dot_nt(df, wout_v[lo:hi, :])
            sig = jax.nn.sigmoid(av)
            sa = av * sig
            s_ref[:, lo:hi] = (sa * bv).astype(BF16)
            da = (ds * bv * (sig * (1.0 + av * (1.0 - sig)))).astype(BF16)
            db = (ds * sa).astype(BF16)
            dab_ref[:, lo:hi] = da
            dab_ref[:, FF + lo:FF + hi] = db
            part = _dot_nt(da, win_v[:, lo:hi]) + _dot_nt(db, win_v[:, FF + lo:FF + hi])
            du = part if du is None else du + part
        xh, r, n, _ = _norm_fwd(h_ref[...], g, sh, sc)
        dhn, dsh, dsc, dg = _norm_bwd(du, xh, r, n, g, sc)
        dhi_ref[...] = dho + dhn
        acc_ref[0:1, :] += dsh
        acc_ref[1:2, :] += dsc
        acc_ref[2:3, :] += dg
        acc_ref[3:4, :] += dgt

    return _call(
        body, name, (T // TM,),
        [_rows(TM, D), _rows(TM, D), _rows(TM, FF), _rows(TM, FF), _rows(TM, D), _const((8, D)), ANY, ANY],
        [_rows(TM, D), _rows(TM, 2 * FF), _rows(TM, FF), _rows(TM, D), _const((8, D))],
        [_sds((T, D), F32), _sds((T, 2 * FF), BF16), _sds((T, FF), BF16), _sds((T, D), BF16), _sds((8, D), F32)],
        scratch=[pltpu.VMEM((D, 2 * FF), BF16), pltpu.VMEM((FF, D), BF16), pltpu.SemaphoreType.DMA((2,))],
        vmem=VMEM_BIG, comm=comm,
    )(dh, h, a, b, f, vec, win, wout)


def _wgrad(x, y, name, tk, tn, tt, out_dtype=BF16, comm=None):
    T, K = x.shape
    N = y.shape[1]
    nt = T // tt

    def body(x_ref, y_ref, o_ref, acc_ref):
        t = pl.program_id(2)
        part = _dot_tn(x_ref[...], y_ref[...])

        @pl.when(t == 0)
        def _():
            acc_ref[...] = part

        @pl.when(t > 0)
        def _():
            acc_ref[...] += part

        @pl.when(t == nt - 1)
        def _():
            o_ref[...] = acc_ref[...].astype(out_dtype)

    (out,), c_outs = _call(
        body, name, (K // tk, N // tn, nt),
        [pl.BlockSpec((tt, tk), lambda i, j, t: (t, i)), pl.BlockSpec((tt, tn), lambda i, j, t: (t, j))],
        [pl.BlockSpec((tk, tn), lambda i, j, t: (i, j))], [_sds((K, N), out_dtype)],
        scratch=[pltpu.VMEM((tk, tn), F32)], vmem=VMEM_BIG, comm=comm,
    )(x, y)
    return out, c_outs


def _final_loss(h, tgt, gvec):
    T = h.shape[0]

    def body(h_ref, t_ref, g_ref, dh_ref, acc_ref):
        _zero_first(acc_ref)
        hh = h_ref[...]
        g = g_ref[0:1, :]
        r = lax.rsqrt(jnp.mean(hh * hh, axis=-1, keepdims=True) + EPS)
        xh = hh * r
        err = xh * g - t_ref[...]
        dy = err * (1.0 / D)
        dxh = dy * g
        dh_ref[...] = r * (dxh - xh * jnp.mean(dxh * xh, axis=-1, keepdims=True))
        acc_ref[0:1, :] += _colsum(err * err)
        acc_ref[1:2, :] += _colsum(dy * xh)

    return _call(
        body, "final_loss", (T // TM,),
        [_rows(TM, D), _rows(TM, D), _const((8, D))],
        [_rows(TM, D), _const((8, D))],
        [_sds((T, D), F32), _sds((8, D), F32)],
    )(h, tgt, gvec)[0]


def _swap_halves(t):
    w = t.shape[1]
    lane = lax.broadcasted_iota(jnp.int32, t.shape, 1)
    return jnp.where(lane % HD < HD // 2, pltpu.roll(t, w - HD // 2, 1), pltpu.roll(t, HD // 2, 1))


def _rope(t, cos, sin_signed):
    c = jnp.tile(cos, (1, t.shape[1] // cos.shape[1]))
    s = jnp.tile(sin_signed, (1, t.shape[1] // sin_signed.shape[1]))
    return t * c + _swap_halves(t) * s


def _rope_bwd(dt, cos, sin_signed):
    c = jnp.tile(cos, (1, dt.shape[1] // cos.shape[1]))
    s = jnp.tile(sin_signed, (1, dt.shape[1] // sin_signed.shape[1]))
    return dt * c + _swap_halves(dt * s)


def _mix_proj(h, vec, win, cos, sin, comm=None):
    T = h.shape[0]

    def body(h_ref, vec_ref, win_hbm, cos_ref, sin_ref, u_ref, p_ref, *rest):
        qkv_refs, gates_ref, win_v, sems = rest[:3 * NG], rest[3 * NG], rest[3 * NG + 1], rest[3 * NG + 2]
        _load_once([(win_hbm, win_v)], sems)
        g, sh, sc = vec_ref[0:1, :], vec_ref[1:2, :], vec_ref[2:3, :]
        _, _, _, u = _norm_fwd(h_ref[...], g, sh, sc)
        ub = u.astype(BF16)
        u_ref[...] = ub
        p_ref[...] = _dot(ub, win_v[:, 0:PW])
        cos_t, sin_t = cos_ref[...], sin_ref[...]
        for j in range(3 * NG):
            col = PW + j * GA
            t = _dot(ub, win_v[:, col:col + GA])
            if j < 2 * NG:
                t = _rope(t, cos_t, sin_t)
            qkv_refs[j][...] = t.astype(BF16)
        for j in range(GW // 512):
            col = PW + 3 * NG * GA + j * 512
            gates_ref[:, j * 512:(j + 1) * 512] = jax.nn.sigmoid(_dot(ub, win_v[:, col:col + 512])).astype(BF16)

    outs, c_outs = _call(
        body, "mix_proj", (T // TM,),
        [_rows(TM, D), _const((8, D)), ANY, _rows(TM, 128), _rows(TM, 128)],
        [_rows(TM, D), _rows(TM, PW)] + [_rows(TM, GA)] * (3 * NG) + [_rows(TM, GW)],
        [_sds((T, D), BF16), _sds((T, PW), F32)] + [_sds((T, GA), BF16)] * (3 * NG) + [_sds((T, GW), BF16)],
        scratch=[pltpu.VMEM((D, INW), BF16), pltpu.SemaphoreType.DMA((1,))],
        vmem=VMEM_BIG, comm=comm,
    )(h, vec, win, cos, sin)
    return (outs[0], outs[1], outs[2:2 + NG], outs[2 + NG:2 + 2 * NG], outs[2 + 2 * NG:2 + 3 * NG], outs[2 + 3 * NG]), c_outs


def _head_masks():
    lane_head = lax.broadcasted_iota(jnp.int32, (BLK, GA), 1) // HD
    return [lane_head == hd for hd in range(NH)]


def _expand_heads(t, hm):
    return jnp.concatenate([jnp.where(m, t, jnp.zeros_like(t)) for m in hm], axis=0)


def _collapse_heads(tb, hm):
    out = None
    for hd, m in enumerate(hm):
        part = jnp.where(m, tb[hd * BLK:(hd + 1) * BLK, :], 0.0)
        out = part if out is None else out + part
    return out


def _head_rows(t):
    return jnp.concatenate([t[:, hd * HD:hd * HD + 1] for hd in range(NH)], axis=0)


def _band_masks():
    a = lax.broadcasted_iota(jnp.int32, (NH * BLK, BLK), 0) & (BLK - 1)
    c = lax.broadcasted_iota(jnp.int32, (NH * BLK, BLK), 1)
    return c <= a, c >= a


def _seq_blocks(nbt):
    g = pl.program_id(0)
    return jnp.where(g == 0, nbt // DIL[0], jnp.where(g == 1, nbt // DIL[1], nbt // DIL[2]))


def _attn_specs(nbt):
    cur = pl.BlockSpec((None, BLK, GA), lambda g, i: (g, i, 0))
    prev = pl.BlockSpec((None, BLK, GA), lambda g, i: (g, jnp.maximum(i - 1, 0), 0))
    nxt = pl.BlockSpec((None, BLK, GA), lambda g, i: (g, jnp.minimum(i + 1, nbt - 1), 0))
    return cur, prev, nxt


def _attn_fwd(q, k, v, comm=None):
    T = q.shape[1]
    nbt = T // BLK

    def body(q_ref, kc_ref, kp_ref, vc_ref, vp_ref, o_ref, lse_ref):
        i = pl.program_id(1)
        has_prev = (i & (_seq_blocks(nbt) - 1)) != 0
        hm = _head_masks()
        m_cur, m_prev = _band_masks()
        m_prev = jnp.logical_and(m_prev, has_prev)
        qb = _expand_heads(q_ref[...], hm)
        s_c = jnp.where(m_cur, _dot_nt(qb, kc_ref[...]) * SCALE, NEG)
        s_p = jnp.where(m_prev, _dot_nt(qb, kp_ref[...]) * SCALE, NEG)
        mx = jnp.maximum(jnp.max(s_c, axis=-1, keepdims=True), jnp.max(s_p, axis=-1, keepdims=True))
        e_c = jnp.exp(s_c - mx)
        e_p = jnp.exp(s_p - mx)
        l = jnp.sum(e_c, axis=-1, keepdims=True) + jnp.sum(e_p, axis=-1, keepdims=True)
        inv = 1.0 / l
        ob = _dot((e_c * inv).astype(BF16), vc_ref[...]) + _dot((e_p * inv).astype(BF16), vp_ref[...])
        o_ref[...] = _collapse_heads(ob, hm)
        lse_ref[...] = _collapse_heads(jnp.broadcast_to(mx + jnp.log(l), (NH * BLK, GA)), hm)

    cur, prev, _ = _attn_specs(nbt)
    return _call(body, "attn_fwd", (NG, nbt), [cur, cur, prev, cur, prev], [cur, cur],
                 [_sds((NG, T, GA), F32), _sds((NG, T, GA), F32)], comm=comm)(q, k, k, v, v)


def _attn_bwd(q, k, v, do, lse, e, comm=None):
    T = q.shape[1]
    nbt = T // BLK

    def body(q_ref, kc_ref, vc_ref, do_ref, lse_ref, e_ref, kp_ref, vp_ref, qn_ref, don_ref, lsen_ref, en_ref,
             dq_ref, dk_ref, dv_ref):
        i = pl.program_id(1)
        nb = _seq_blocks(nbt)
        has_prev = (i & (nb - 1)) != 0
        has_next = ((i + 1) & (nb - 1)) != 0
        hm = _head_masks()
        m_cur, m_band = _band_masks()
        m_prev = jnp.logical_and(m_band, has_prev)
        m_next = jnp.logical_and(m_band, has_next)
        kc, kp, vc, vp = kc_ref[...], kp_ref[...], vc_ref[...], vp_ref[...]
        qb, dob = _expand_heads(q_ref[...], hm), _expand_heads(do_ref[...], hm)
        lse_r, e_r = _head_rows(lse_ref[...]), _head_rows(e_ref[...])
        p_c = jnp.where(m_cur, jnp.exp(_dot_nt(qb, kc) * SCALE - lse_r), 0.0)
        p_p = jnp.where(m_prev, jnp.exp(_dot_nt(qb, kp) * SCALE - lse_r), 0.0)
        ds_c = (p_c * (_dot_nt(dob, vc) + e_r)).astype(BF16)
        ds_p = (p_p * (_dot_nt(dob, vp) + e_r)).astype(BF16)
        dq_ref[...] = _collapse_heads((_dot(ds_c, kc) + _dot(ds_p, kp)) * SCALE, hm)
        qnb, donb = _expand_heads(qn_ref[...], hm), _expand_heads(don_ref[...], hm)
        p_n = jnp.where(m_next, jnp.exp(_dot_nt(qnb, kc) * SCALE - _head_rows(lsen_ref[...])), 0.0)
        ds_n = (p_n * (_dot_nt(donb, vc) + _head_rows(en_ref[...]))).astype(BF16)
        dk_ref[...] = (_dot_tn(ds_c, qb) + _dot_tn(ds_n, qnb)) * SCALE
        dv_ref[...] = (_dot_tn(p_c.astype(BF16), dob) + _dot_tn(p_n.astype(BF16), donb)).astype(BF16)

    cur, prev, nxt = _attn_specs(nbt)
    return _call(body, "attn_bwd", (NG, nbt), [cur] * 6 + [prev, prev] + [nxt] * 4, [cur, cur, cur],
                 [_sds((NG, T, GA), F32), _sds((NG, T, GA), F32), _sds((NG, T, GA), BF16)],
                 comm=comm)(q, k, v, do, lse, e, k, v, q, do, lse, e)


def _to_residue_major(t, dil):
    if dil == 1:
        return t
    T, w = t.shape
    return t.reshape(T // dil, dil, w).transpose(1, 0, 2).reshape(T, w)


def _from_residue_major(t, dil):
    if dil == 1:
        return t
    T, w = t.shape
    return t.reshape(dil, T // dil, w).transpose(1, 0, 2).reshape(T, w)


def _stack_groups(ts):
    return jnp.stack([_to_residue_major(t, d) for t, d in zip(ts, DIL)], axis=0)


def _unstack_groups(t):
    return [_from_residue_major(t[g], d) for g, d in enumerate(DIL)]


def _pool_consts(shape, row0):
    lane = lax.broadcasted_iota(jnp.int32, shape, 1)
    t = lax.broadcasted_iota(jnp.int32, shape, 0) + row0
    grp = lane // (PW // len(POOL_WINDOWS))
    win = jnp.where(grp == 0, POOL_WINDOWS[0], jnp.where(grp == 1, POOL_WINDOWS[1],
                    jnp.where(grp == 2, POOL_WINDOWS[2], POOL_WINDOWS[3])))
    cnt = jnp.minimum(t + 1, win).astype(F32)
    return grp, cnt


def _window_sums(ext_ref, base, step, tm):
    outs, run = [], None
    for j in range(POOL_WINDOWS[-1]):
        sl = ext_ref[pl.ds(base + step * j, tm), :]
        run = sl if run is None else run + sl
        if j + 1 in POOL_WINDOWS:
            outs.append(run)
    return outs


def _select_group(grp, vals):
    return jnp.where(grp == 0, vals[0], jnp.where(grp == 1, vals[1], jnp.where(grp == 2, vals[2], vals[3])))


def _pool_d(pc_ref, pp_ref, ext_ref, i, tm):
    ext_ref[0:HALO, :] = jnp.where(i > 0, pp_ref[tm - HALO:tm, :], 0.0)
    ext_ref[HALO:HALO + tm, :] = pc_ref[...]
    grp, cnt = _pool_consts((tm, PW), i * tm)
    sums = _window_sums(ext_ref, HALO, -1, tm)
    return _select_group(grp, sums) / cnt - pc_ref[...]


def _group_weights(lse_refs):
    ls = [r[...] for r in lse_refs]
    mx = jnp.maximum(jnp.maximum(ls[0], ls[1]), ls[2])
    es = [jnp.exp(l - mx) for l in ls]
    inv = 1.0 / (es[0] + es[1] + es[2])
    return [e * inv for e in es]


def _mix_merge(h, vec, p, os, lses, gates, wp_bd, pscale, wpb, wab, wout):
    T = h.shape[0]

    def body(h_ref, vec_ref, pc_ref, pp_ref, o0, o1, o2, l0, l1, l2, gates_ref, wp_ref, ps_ref, wpb_ref, wab_ref, wout_ref,
             ho_ref, yp_ref, ya_ref, mg_ref, mo_ref, d_ref, ext_ref):
        i = pl.program_id(0)
        gt = vec_ref[3:4, :]
        d = _pool_d(pc_ref, pp_ref, ext_ref, i, TM).astype(BF16)
        d_ref[...] = d
        ypool = (_dot(d, wp_ref[...]) * ps_ref[0:1, :]).astype(BF16)
        yp_ref[...] = ypool
        w = _group_weights((l0, l1, l2))
        yattn = (w[0] * o0[...] + w[1] * o1[...] + w[2] * o2[...]).astype(BF16)
        ya_ref[...] = yattn
        merged = (gates_ref[:, 0:D].astype(F32) * _dot(ypool, wpb_ref[...])
                  + gates_ref[:, D:GW].astype(F32) * _dot(yattn, wab_ref[...])).astype(BF16)
        mg_ref[...] = merged
        mo = _dot(merged, wout_ref[...])
        mo_ref[...] = mo.astype(BF16)
        ho_ref[...] = h_ref[...] + gt * mo

    prev = pl.BlockSpec((TM, PW), lambda i: (jnp.maximum(i - 1, 0), 0))
    return _call(
        body, "mix_merge", (T // TM,),
        [_rows(TM, D), _const((8, D)), _rows(TM, PW), prev] + [_rows(TM, GA)] * 6 + [_rows(TM, GW), _const((PW, PW)),
         _const((8, PW)), _const((PW, D)), _const((GA, D)), _const((D, D))],
        [_rows(TM, D), _rows(TM, PW), _rows(TM, GA), _rows(TM, D), _rows(TM, D), _rows(TM, PW)],
        [_sds((T, D), F32), _sds((T, PW), BF16), _sds((T, GA), BF16), _sds((T, D), BF16), _sds((T, D), BF16), _sds((T, PW), BF16)],
        scratch=[pltpu.VMEM((TM + HALO, PW), F32)],
        vmem=VMEM_BIG,
    )(h, vec, p, p, *os, *lses, gates, wp_bd, pscale, wpb, wab, wout)[0]


def _mix_bwd_a(dh, vec, mixout, gates, ypool, yattn, dpool, os, lses, wp_bd, pscale, wpb, wab, wout, ones_bd, comm=None):
    T = dh.shape[0]

    def body(dh_ref, vec_ref, mo_ref, gates_ref, yp_ref, ya_ref, d_ref, o0, o1, o2, l0, l1, l2,
             wp_ref, ps_ref, wpb_ref, wab_ref, wout_ref, ones_ref,
             dmo_ref, dp_ref, da_ref, dgates_ref, do0, do1, do2, e0, e1, e2, dd_ref, dyp_ref, acc_ref, acc2_ref):
        _zero_first(acc_ref)
        _zero_first(acc2_ref)
        gt = vec_ref[3:4, :]
        dho = dh_ref[...]
        acc_ref[3:4, :] += _colsum(dho * mo_ref[...].astype(F32))
        dmo = (gt * dho).astype(BF16)
        dmo_ref[...] = dmo
        dmerged = _dot_nt(dmo, wout_ref[...])
        gp = gates_ref[:, 0:D].astype(F32)
        ga = gates_ref[:, D:GW].astype(F32)
        bp = _dot(yp_ref[...], wpb_ref[...])
        ba = _dot(ya_ref[...], wab_ref[...])
        dgates_ref[:, 0:D] = (dmerged * bp * gp * (1.0 - gp)).astype(BF16)
        dgates_ref[:, D:GW] = (dmerged * ba * ga * (1.0 - ga)).astype(BF16)
        dbp = (dmerged * gp).astype(BF16)
        dba = (dmerged * ga).astype(BF16)
        dp_ref[...] = dbp
        da_ref[...] = dba
        dypool = _dot_nt(dbp, wpb_ref[...])
        ypre = _dot(d_ref[...], wp_ref[...])
        acc2_ref[0:1, :] += _colsum(dypool * ypre)
        dyp = (dypool * ps_ref[0:1, :]).astype(BF16)
        dyp_ref[...] = dyp
        dd_ref[...] = _dot_nt(dyp, wp_ref[...])
        dya = _dot_nt(dba, wab_ref[...])
        w = _group_weights((l0, l1, l2))
        ya = w[0] * o0[...] + w[1] * o1[...] + w[2] * o2[...]
        prod = dya * ya
        hi = prod.astype(BF16)
        lo = (prod - hi.astype(F32)).astype(BF16)
        tot = _dot(hi, ones_ref[...]) + _dot(lo, ones_ref[...])
        for wg, do_ref, e_ref in zip(w, (do0, do1, do2), (e0, e1, e2)):
            do_ref[...] = (wg * dya).astype(BF16)
            e_ref[...] = -wg * tot

    return _call(
        body, "mix_bwd_a", (T // TM,),
        [_rows(TM, D), _const((8, D)), _rows(TM, D), _rows(TM, GW), _rows(TM, PW), _rows(TM, GA), _rows(TM, PW)]
        + [_rows(TM, GA)] * 6
        + [_const((PW, PW)), _const((8, PW)), _const((PW, D)), _const((GA, D)), _const((D, D)), _const((GA, GA))],
        [_rows(TM, D)] * 3 + [_rows(TM, GW)] + [_rows(TM, GA)] * 6 + [_rows(TM, PW), _rows(TM, PW), _const((8, D)), _const((8, PW))],
        [_sds((T, D), BF16)] * 3 + [_sds((T, GW), BF16)] + [_sds((T, GA), BF16)] * 3 + [_sds((T, GA), F32)] * 3
        + [_sds((T, PW), F32), _sds((T, PW), BF16), _sds((8, D), F32), _sds((8, PW), F32)],
        vmem=VMEM_BIG, comm=comm,
    )(dh, vec, mixout, gates, ypool, yattn, dpool, *os, *lses, wp_bd, pscale, wpb, wab, wout, ones_bd)


def _mix_bwd_b(dh, h, vec, dd, dqs, dks, dvs, dgates, cos, sin, win):
    T = h.shape[0]
    nt = T // TM

    def body(dh_ref, h_ref, vec_ref, ddc_ref, ddn_ref, *rest):
        qk_refs, dv_refs = rest[:2 * NG], rest[2 * NG:3 * NG]
        dgates_ref, cos_ref, sin_ref, win_hbm, dhi_ref, dproj_ref, acc_ref, win_v, ext_ref, sems = rest[3 * NG:]
        i = pl.program_id(0)
        _load_once([(win_hbm, win_v)], sems)
        _zero_first(acc_ref)
        g, sh, sc = vec_ref[0:1, :], vec_ref[1:2, :], vec_ref[2:3, :]
        grp, cnt = _pool_consts((TM, PW), i * TM)
        _, cnt_n = _pool_consts((HALO, PW), (i + 1) * TM)
        ext_ref[0:TM, :] = ddc_ref[...] / cnt
        ext_ref[TM:TM + HALO, :] = jnp.where(i < nt - 1, ddn_ref[0:HALO, :] / cnt_n, 0.0)
        dp = _select_group(grp, _window_sums(ext_ref, 0, 1, TM)) - ddc_ref[...]
        dproj_ref[:, 0:PW] = dp.astype(BF16)
        cos_t, sin_t = cos_ref[...], sin_ref[...]
        for j in range(2 * NG):
            col = PW + j * GA
            dproj_ref[:, col:col + GA] = _rope_bwd(qk_refs[j][...], cos_t, sin_t).astype(BF16)
        for j in range(NG):
            col = PW + (2 * NG + j) * GA
            dproj_ref[:, col:col + GA] = dv_refs[j][...]
        dproj_ref[:, PW + 3 * NG * GA:INW] = dgates_ref[...]
        du = None
        for j in range(INW // 512):
            part = _dot_nt(dproj_ref[:, j * 512:(j + 1) * 512], win_v[:, j * 512:(j + 1) * 512])
            du = part if du is None else du + part
        xh, r, n, _ = _norm_fwd(h_ref[...], g, sh, sc)
        dhn, dsh, dsc, dg = _norm_bwd(du, xh, r, n, g, sc)
        dhi_ref[...] = dh_ref[...] + dhn
        acc_ref[0:1, :] += dsh
        acc_ref[1:2, :] += dsc
        acc_ref[2:3, :] += dg

    nxt = pl.BlockSpec((TM, PW), lambda i: (jnp.minimum(i + 1, nt - 1), 0))
    return _call(
        body, "mix_bwd_b", (nt,),
        [_rows(TM, D), _rows(TM, D), _const((8, D)), _rows(TM, PW), nxt] + [_rows(TM, GA)] * (3 * NG)
        + [_rows(TM, GW), _rows(TM, 128), _rows(TM, 128), ANY],
        [_rows(TM, D), _rows(TM, INW), _const((8, D))],
        [_sds((T, D), F32), _sds((T, INW), BF16), _sds((8, D), F32)],
        scratch=[pltpu.VMEM((D, INW), BF16), pltpu.VMEM((TM + HALO, PW), F32), pltpu.SemaphoreType.DMA((1,))],
        vmem=VMEM_BIG,
    )(dh, h, vec, dd, dd, *dqs, *dks, *dvs, dgates, cos, sin, win)[0]


def _ada_fwd(c_all, w_shard, b_shard):
    n = w_shard.shape[1]

    def body(c_ref, w_ref, b_ref, o_ref):
        cv = c_ref[...]
        cond = (cv * jax.nn.sigmoid(cv)).astype(BF16)
        o_ref[...] = _dot(cond, w_ref[...].astype(BF16)) + b_ref[...]

    tn = n // 3
    return pl.pallas_call(
        body, name="ada_fwd", grid=(3,),
        in_specs=[pl.BlockSpec((8, D), lambda j: (0, 0)), pl.BlockSpec((D, tn), lambda j: (0, j)), pl.BlockSpec((1, tn), lambda j: (0, j))],
        out_specs=pl.BlockSpec((8, tn), lambda j: (0, j)), out_shape=_sds((8, n), F32),
        compiler_params=pltpu.CompilerParams(dimension_semantics=("arbitrary",)),
    )(c_all, w_shard, b_shard)


def _ada_bwd(c_all, dmod_shard):
    n = dmod_shard.shape[1]

    def body(c_ref, d_ref, o_ref):
        cv = c_ref[...]
        cond = (cv * jax.nn.sigmoid(cv)).astype(BF16)
        o_ref[...] = _dot_tn(cond, d_ref[...].astype(BF16))

    tn = n // 3
    return pl.pallas_call(
        body, name="ada_bwd", grid=(3,),
        in_specs=[pl.BlockSpec((8, D), lambda j: (0, 0)), pl.BlockSpec((8, tn), lambda j: (0, j))],
        out_specs=pl.BlockSpec((D, tn), lambda j: (0, j)), out_shape=_sds((D, n), F32),
        compiler_params=pltpu.CompilerParams(dimension_semantics=("arbitrary",)),
    )(c_all, dmod_shard)


def _adam_math(w, g, m, v):
    m2 = B1 * m + (1.0 - B1) * g
    v2 = B2 * v + (1.0 - B2) * (g * g)
    m_hat = m2 / (1.0 - B1 ** STEP)
    v_hat = v2 / (1.0 - B2 ** STEP)
    delta = -LR * (m_hat / (jnp.sqrt(v_hat) + AEPS) + WD * w)
    return delta, m2, v2


def _adam(w, m, v, parts, name):
    R, C = w.shape
    tr = R
    for cand in (128, 64, 32, 16, 8):
        if R % cand == 0:
            tr = cand
            break
    np_ = len(parts)

    def body(w_ref, m_ref, v_ref, *rest):
        p_refs, (g_ref, d_ref, m2_ref, v2_ref) = rest[:np_], rest[np_:]
        g = p_refs[0][...]
        for pr in p_refs[1:]:
            g = g + pr[...]
        delta, m2, v2 = _adam_math(w_ref[...], g, m_ref[...], v_ref[...])
        g_ref[...] = g
        d_ref[...] = delta
        m2_ref[...] = m2
        v2_ref[...] = v2

    spec = pl.BlockSpec((tr, C), lambda i: (i, 0))
    return pl.pallas_call(
        body, name=name, grid=(R // tr,), in_specs=[spec] * (3 + np_), out_specs=[spec] * 4,
        out_shape=[_sds((R, C), F32)] * 4,
        compiler_params=pltpu.CompilerParams(dimension_semantics=("arbitrary",), vmem_limit_bytes=VMEM_BIG),
    )(w, m, v, *parts)


def _adam_small(w, m, v, gathered):
    P = w.shape[1]

    def body(w_ref, m_ref, v_ref, ga_ref, g_ref, d_ref, m2_ref, v2_ref):
        g = ga_ref[0]
        for dev in range(1, 8):
            g = g + ga_ref[dev]
        delta, m2, v2 = _adam_math(w_ref[...], g, m_ref[...], v_ref[...])
        g_ref[...] = g
        d_ref[...] = delta
        m2_ref[...] = m2
        v2_ref[...] = v2

    return pl.pallas_call(body, name="adam_small", out_shape=[_sds((1, P), F32)] * 4)(w, m, v, gathered)


def _sum4(own, recv, name):
    R, C = own.shape
    tr = R
    for cand in (256, 128, 64, 32, 16):
        if R % cand == 0:
            tr = cand
            break

    def body(o_ref, r_ref, out_ref):
        out_ref[...] = ((o_ref[...].astype(F32) + r_ref[0].astype(F32)) + r_ref[1].astype(F32)) + r_ref[2].astype(F32)

    return pl.pallas_call(
        body, name=name, grid=(R // tr,),
        in_specs=[pl.BlockSpec((tr, C), lambda i: (i, 0)), pl.BlockSpec((3, tr, C), lambda i: (0, i, 0))],
        out_specs=pl.BlockSpec((tr, C), lambda i: (i, 0)), out_shape=_sds((R, C), F32),
        compiler_params=pltpu.CompilerParams(dimension_semantics=("arbitrary",)),
    )(own, recv)


def _place():
    return lax.axis_index("x"), lax.axis_index("y"), lax.axis_index("c")


def _gather_small(v):
    R, P = v.shape

    def body(v_ref, out_ref, send_sems, recv_sems):
        x, y, c = _place()
        me = 4 * x + 2 * y + c
        out_ref[me] = v_ref[...]
        copies = []
        for m in range(1, 8):
            peer = (x ^ (m >> 2), y ^ ((m >> 1) & 1), c ^ (m & 1))
            copies.append(pltpu.make_async_remote_copy(
                src_ref=v_ref, dst_ref=out_ref.at[me], send_sem=send_sems.at[m - 1], recv_sem=recv_sems.at[m - 1],
                device_id=peer, device_id_type=MESH))
        for cp in copies:
            cp.start()
        for m in range(1, 8):
            src = 4 * (x ^ (m >> 2)) + 2 * (y ^ ((m >> 1) & 1)) + (c ^ (m & 1))
            pltpu.make_async_remote_copy(
                src_ref=v_ref, dst_ref=out_ref.at[src], send_sem=send_sems.at[m - 1], recv_sem=recv_sems.at[m - 1],
                device_id=(x, y, c), device_id_type=MESH).wait_recv()
        for cp in copies:
            cp.wait_send()

    vm = pl.BlockSpec(memory_space=pltpu.VMEM)
    return pl.pallas_call(
        body, name="gather_small", in_specs=[vm], out_specs=vm, out_shape=_sds((8, R, P), F32),
        scratch_shapes=[pltpu.SemaphoreType.DMA((7,)), pltpu.SemaphoreType.DMA((7,))],
    )(v)


def _chip_peer(x, y, c, m):
    return (x ^ (m >> 1), y ^ (m & 1), c)


def _shard_ref(ref, axis, k, n):
    start = pl.multiple_of(k * n, 128 if axis == 1 else 16)
    return ref.at[:, pl.ds(start, n)] if axis == 1 else ref.at[pl.ds(start, n), :]


class _GatherPlan:
    def __init__(self, shards, axes):
        self.inputs, self.axes, nw = list(shards), list(axes), len(shards)
        self.out_shapes = [_sds((s.shape[0] * (4 if ax == 0 else 1), s.shape[1] * (4 if ax == 1 else 1)), BF16)
                           for s, ax in zip(shards, axes)]
        self.sem_shapes = [pltpu.SemaphoreType.DMA((nw,)), pltpu.SemaphoreType.DMA((nw, 3)), pltpu.SemaphoreType.DMA((nw, 3))]

    def _copies(self, ins, outs, sems):
        local_sems, send_sems, recv_sems = sems
        x, y, c = _place()
        k = 2 * x + y
        local, remote, arrivals = [], [], []
        for j, ax in enumerate(self.axes):
            n = ins[j].shape[ax]
            mine = _shard_ref(outs[j], ax, k, n)
            local.append(pltpu.make_async_copy(ins[j], mine, local_sems.at[j]))
            for m in range(1, 4):
                remote.append(pltpu.make_async_remote_copy(
                    src_ref=ins[j], dst_ref=mine, send_sem=send_sems.at[j, m - 1], recv_sem=recv_sems.at[j, m - 1],
                    device_id=_chip_peer(x, y, c, m), device_id_type=MESH))
                arrivals.append(pltpu.make_async_remote_copy(
                    src_ref=ins[j], dst_ref=_shard_ref(outs[j], ax, k ^ m, n), send_sem=send_sems.at[j, m - 1],
                    recv_sem=recv_sems.at[j, m - 1], device_id=(x, y, c), device_id_type=MESH))
        return local, remote, arrivals

    def start(self, ins, outs, sems):
        local, remote, _ = self._copies(ins, outs, sems)
        for cp in local + remote:
            cp.start()

    def wait(self, ins, outs, sems):
        local, remote, arrivals = self._copies(ins, outs, sems)
        for cp in arrivals:
            cp.wait_recv()
        for cp in remote:
            cp.wait_send()
        for cp in local:
            cp.wait()


class _ScatterPlan:
    def __init__(self, grads, axes):
        self.inputs, self.axes, nw = list(grads), list(axes), len(grads)
        self.shard_shapes = [(g.shape[0] // (4 if ax == 0 else 1), g.shape[1] // (4 if ax == 1 else 1))
                             for g, ax in zip(grads, axes)]
        self.out_shapes = [_sds((3,) + s, BF16) for s in self.shard_shapes]
        self.sem_shapes = [pltpu.SemaphoreType.DMA((nw, 3)), pltpu.SemaphoreType.DMA((nw, 3))]

    def _copies(self, ins, outs, sems):
        send_sems, recv_sems = sems
        x, y, c = _place()
        k = 2 * x + y
        remote, arrivals = [], []
        for j, ax in enumerate(self.axes):
            n = self.shard_shapes[j][ax]
            for m in range(1, 4):
                remote.append(pltpu.make_async_remote_copy(
                    src_ref=_shard_ref(ins[j], ax, k ^ m, n), dst_ref=outs[j].at[m - 1],
                    send_sem=send_sems.at[j, m - 1], recv_sem=recv_sems.at[j, m - 1],
                    device_id=_chip_peer(x, y, c, m), device_id_type=MESH))
                arrivals.append(pltpu.make_async_remote_copy(
                    src_ref=_shard_ref(ins[j], ax, k, n), dst_ref=outs[j].at[m - 1],
                    send_sem=send_sems.at[j, m - 1], recv_sem=recv_sems.at[j, m - 1],
                    device_id=(x, y, c), device_id_type=MESH))
        return remote, arrivals

    def start(self, ins, outs, sems):
        for cp in self._copies(ins, outs, sems)[0]:
            cp.start()

    def wait(self, ins, outs, sems):
        remote, arrivals = self._copies(ins, outs, sems)
        for cp in arrivals:
            cp.wait_recv()
        for cp in remote:
            cp.wait_send()


def _run_plan(plan, name):
    nc = len(plan.inputs)

    def body(*refs):
        ins, outs, sems = refs[:nc], refs[nc:2 * nc], refs[2 * nc:]
        plan.start(ins, outs, sems)
        plan.wait(ins, outs, sems)

    return pl.pallas_call(body, name=name, in_specs=[ANY] * nc, out_specs=[ANY] * nc, out_shape=list(plan.out_shapes),
                          scratch_shapes=list(plan.sem_shapes))(*plan.inputs)


def _swap_sibling(parts):
    nw = len(parts)

    def body(*refs):
        ins, outs = refs[:nw], refs[nw:2 * nw]
        send_sems, recv_sems = refs[2 * nw:]
        x, y, c = _place()
        copies = [pltpu.make_async_remote_copy(
            src_ref=ins[j], dst_ref=outs[j], send_sem=send_sems.at[j], recv_sem=recv_sems.at[j],
            device_id=(x, y, 1 - c), device_id_type=MESH) for j in range(nw)]
        for cp in copies:
            cp.start()
        for cp in copies:
            cp.wait()

    return pl.pallas_call(
        body, name="swap_sibling", in_specs=[ANY] * nw, out_specs=[ANY] * nw,
        out_shape=[_sds(p.shape, p.dtype) for p in parts],
        scratch_shapes=[pltpu.SemaphoreType.DMA((nw,)), pltpu.SemaphoreType.DMA((nw,))],
    )(*parts)


BIG = ("w_ffn1_in", "w_ffn1_out", "w_in", "w_pool_branch", "w_attn_branch", "w_out", "w_ffn2_in", "w_ffn2_out")
BIG_AXIS = {"w_ffn1_in": 1, "w_ffn1_out": 0, "w_in": 1, "w_pool_branch": 1, "w_attn_branch": 1, "w_out": 0,
            "w_ffn2_in": 1, "w_ffn2_out": 0}


class _Sharded:
    def __init__(self, shards):
        self.shards, self.full, self.recv = shards, {}, {}

    def gather_plan(self, names):
        return _GatherPlan([self.shards[n] for n in names], [BIG_AXIS[n] for n in names])

    def gather_now(self, names):
        self.gathered(names, _run_plan(self.gather_plan(names), "gather_" + names[0]))

    def gathered(self, names, outs):
        self.full.update(zip(names, outs))

    def scatter_plan(self, names, grads):
        return _ScatterPlan([grads[n] for n in names], [BIG_AXIS[n] for n in names])

    def scatter_now(self, names, grads):
        self.scattered(names, _run_plan(self.scatter_plan(names, grads), "scatter_" + names[0]))

    def scattered(self, names, outs):
        self.recv.update(zip(names, outs))


class _Whole:
    def __init__(self, full):
        self.full, self.recv = dict(full), {}

    def gather_plan(self, names):
        return None

    def gather_now(self, names):
        pass

    def gathered(self, names, outs):
        pass

    def scatter_plan(self, names, grads):
        return None

    def scatter_now(self, names, grads):
        pass

    def scattered(self, names, outs):
        pass


def _vec(rows):
    pad = [jnp.zeros((1, D), F32)] * (8 - len(rows))
    return jnp.concatenate([r.reshape(1, D) for r in rows] + pad, axis=0)


def _block_diag(w_pool):
    n, c = w_pool.shape[0], w_pool.shape[1]
    eye = jnp.eye(n, dtype=w_pool.dtype)
    return (eye[:, None, :, None] * w_pool[:, :, None, :]).reshape(n * c, n * c)


def _example_step(x, tgt, positions, mod, gains, w_pool, pool_scale, ws):
    T = x.shape[0]
    assert (T // BLK // DIL[-1]) & (T // BLK // DIL[-1] - 1) == 0, "blocks per sequence must be a power of two"
    sh1, sc1, gt1, sh2, sc2, gt2, sh3, sc3, gt3 = [mod[j * D:(j + 1) * D] for j in range(NMOD)]
    g1, g2, g3, gf = gains
    vec1, vec2, vec3 = _vec([g1, sh1, sc1, gt1]), _vec([g2, sh2, sc2, gt2]), _vec([g3, sh3, sc3, gt3])
    inv_freq = 10000.0 ** (-jnp.arange(0, HD, 2, dtype=F32) / HD)
    ang = positions.astype(F32)[:, None] * inv_freq
    cos = jnp.tile(jnp.cos(ang), (1, 4))
    sin = jnp.tile(jnp.concatenate([-jnp.sin(ang), jnp.sin(ang)], axis=1), (1, 2))
    wp_bd = _block_diag(w_pool).astype(BF16)
    ones_bd = _block_diag(jnp.ones((NH, HD, HD), F32)).astype(BF16)
    ps = jnp.concatenate([pool_scale.reshape(1, PW), jnp.zeros((7, PW), F32)], axis=0)
    wb = ws.full

    ws.gather_now(["w_ffn1_in", "w_ffn1_out"])
    mixw = ["w_in", "w_pool_branch", "w_attn_branch", "w_out"]
    (h1, u1, a1, b1, f1), got = _ffn_fwd(x, vec1, wb["w_ffn1_in"], wb["w_ffn1_out"], "ffn1_fwd", ws.gather_plan(mixw))
    ws.gathered(mixw, got)
    (u2, p, qs, ks, vs, gates), got = _mix_proj(h1, vec2, wb["w_in"], cos, sin, ws.gather_plan(["w_ffn2_out"]))
    ws.gathered(["w_ffn2_out"], got)
    q3, k3, v3 = _stack_groups(qs), _stack_groups(ks), _stack_groups(vs)
    (o3, lse3), got = _attn_fwd(q3, k3, v3, ws.gather_plan(["w_ffn2_in"]))
    ws.gathered(["w_ffn2_in"], got)
    os_n, lses_n = _unstack_groups(o3), _unstack_groups(lse3)
    h2, ypool, yattn, merged, mixout, dpool = _mix_merge(
        h1, vec2, p, os_n, lses_n, gates, wp_bd, ps, wb["w_pool_branch"], wb["w_attn_branch"], wb["w_out"])
    (h3, u3, a3, b3, f3), _ = _ffn_fwd(h2, vec3, wb["w_ffn2_in"], wb["w_ffn2_out"], "ffn2_fwd")
    dh3, lacc = _final_loss(h3, tgt, _vec([gf]))
    loss = 0.5 * jnp.sum(lacc[0]) / D

    grads = {}
    (dh2, dab3, s3, df3, acc3), _ = _ffn_bwd(dh3, h2, a3, b3, f3, vec3, wb["w_ffn2_in"], wb["w_ffn2_out"], "ffn2_bwd")
    grads["w_ffn2_out"], _ = _wgrad(s3, df3, "wg_ffn2_out", FC, 512, 1024)
    grads["w_ffn2_in"], got = _wgrad(u3, dab3, "wg_ffn2_in", D, 512, 1024, comm=ws.scatter_plan(["w_ffn2_out"], grads))
    ws.scattered(["w_ffn2_out"], got)
    (dmo, dbp, dba, dgates, do0, do1, do2, e0, e1, e2, dd, dyp, acc2a, accps), got = _mix_bwd_a(
        dh2, vec2, mixout, gates, ypool, yattn, dpool, os_n, lses_n, wp_bd, ps,
        wb["w_pool_branch"], wb["w_attn_branch"], wb["w_out"], ones_bd, ws.scatter_plan(["w_ffn2_in"], grads))
    ws.scattered(["w_ffn2_in"], got)
    grads["w_out"], _ = _wgrad(merged, dmo, "wg_out", D, 512, 1024)
    grads["w_pool_branch"], _ = _wgrad(ypool, dbp, "wg_pool_branch", PW, 512, 1024)
    grads["w_attn_branch"], _ = _wgrad(yattn, dba, "wg_attn_branch", GA, 512, 1024)
    gwp, _ = _wgrad(dpool, dyp, "wg_pool", PW, PW, 1024, out_dtype=F32)
    n = len(POOL_WINDOWS)
    c = PW // n
    grad_w_pool = jnp.stack([gwp[j * c:(j + 1) * c, j * c:(j + 1) * c] for j in range(n)], axis=0)
    small3 = ["w_out", "w_pool_branch", "w_attn_branch"]
    (dq3, dk3, dv3), got = _attn_bwd(q3, k3, v3, _stack_groups((do0, do1, do2)), lse3, _stack_groups((e0, e1, e2)),
                                     ws.scatter_plan(small3, grads))
    ws.scattered(small3, got)
    dh1, dproj, acc2b = _mix_bwd_b(dh2, h1, vec2, dd, _unstack_groups(dq3), _unstack_groups(dk3), _unstack_groups(dv3),
                                   dgates, cos, sin, wb["w_in"])
    grads["w_in"], _ = _wgrad(u2, dproj, "wg_in", D, 512, 1024)
    (dx, dab1, s1, df1, acc1), got = _ffn_bwd(dh1, x, a1, b1, f1, vec1, wb["w_ffn1_in"], wb["w_ffn1_out"], "ffn1_bwd",
                                              ws.scatter_plan(["w_in"], grads))
    ws.scattered(["w_in"], got)
    grads["w_ffn1_out"], _ = _wgrad(s1, df1, "wg_ffn1_out", FC, 512, 1024)
    grads["w_ffn1_in"], got = _wgrad(u1, dab1, "wg_ffn1_in", D, 512, 1024, comm=ws.scatter_plan(["w_ffn1_out"], grads))
    ws.scattered(["w_ffn1_out"], got)
    ws.scatter_now(["w_ffn1_in"], grads)

    dmod = jnp.concatenate([acc1[0], acc1[1], acc1[3], acc2b[0], acc2b[1], acc2a[3], acc3[0], acc3[1], acc3[3]])
    dgains = jnp.stack([acc1[2], acc2b[2], acc3[2], lacc[1]], axis=0)
    return loss, dx, dmod, dgains, grad_w_pool, accps[0], grads


SMALL = ("b_ada", "g_norm_ffn1", "g_norm_mix", "g_norm_ffn2", "g_final", "pool_scale", "w_pool")
WEIGHTS = ("w_ada", "b_ada", "g_norm_ffn1", "w_ffn1_in", "w_ffn1_out", "g_norm_mix", "w_in", "w_pool", "pool_scale",
           "w_pool_branch", "w_attn_branch", "w_out", "g_norm_ffn2", "w_ffn2_in", "w_ffn2_out", "g_final")


def _pack_small(t):
    return jnp.concatenate([t[n].reshape(-1) for n in SMALL]).reshape(1, -1)


def _unpack_small(flat, like):
    out, off = {}, 0
    for n in SMALL:
        size = like[n].size
        out[n] = flat[0, off:off + size].reshape(like[n].shape)
        off += size
    return out


def kernel(x, c, positions, w_ada, b_ada, g_norm_ffn1, w_ffn1_in, w_ffn1_out, g_norm_mix, w_in, w_pool, pool_scale, w_pool_branch, w_attn_branch, w_out, g_norm_ffn2, w_ffn2_in, w_ffn2_out, g_final, loss_target, m_w_ada, m_b_ada, m_g_norm_ffn1, m_w_ffn1_in, m_w_ffn1_out, m_g_norm_mix, m_w_in, m_w_pool, m_pool_scale, m_w_pool_branch, m_w_attn_branch, m_w_out, m_g_norm_ffn2, m_w_ffn2_in, m_w_ffn2_out, m_g_final, v_w_ada, v_b_ada, v_g_norm_ffn1, v_w_ffn1_in, v_w_ffn1_out, v_g_norm_mix, v_w_in, v_w_pool, v_pool_scale, v_w_pool_branch, v_w_attn_branch, v_w_out, v_g_norm_ffn2, v_w_ffn2_in, v_w_ffn2_out, v_g_final):
    w = dict(w_ada=w_ada, b_ada=b_ada, g_norm_ffn1=g_norm_ffn1, w_ffn1_in=w_ffn1_in, w_ffn1_out=w_ffn1_out,
             g_norm_mix=g_norm_mix, w_in=w_in, w_pool=w_pool, pool_scale=pool_scale, w_pool_branch=w_pool_branch,
             w_attn_branch=w_attn_branch, w_out=w_out, g_norm_ffn2=g_norm_ffn2, w_ffn2_in=w_ffn2_in,
             w_ffn2_out=w_ffn2_out, g_final=g_final)
    mom = dict(w_ada=m_w_ada, b_ada=m_b_ada, g_norm_ffn1=m_g_norm_ffn1, w_ffn1_in=m_w_ffn1_in, w_ffn1_out=m_w_ffn1_out,
               g_norm_mix=m_g_norm_mix, w_in=m_w_in, w_pool=m_w_pool, pool_scale=m_pool_scale,
               w_pool_branch=m_w_pool_branch, w_attn_branch=m_w_attn_branch, w_out=m_w_out, g_norm_ffn2=m_g_norm_ffn2,
               w_ffn2_in=m_w_ffn2_in, w_ffn2_out=m_w_ffn2_out, g_final=m_g_final)
    var = dict(w_ada=v_w_ada, b_ada=v_b_ada, g_norm_ffn1=v_g_norm_ffn1, w_ffn1_in=v_w_ffn1_in, w_ffn1_out=v_w_ffn1_out,
               g_norm_mix=v_g_norm_mix, w_in=v_w_in, w_pool=v_w_pool, pool_scale=v_pool_scale,
               w_pool_branch=v_w_pool_branch, w_attn_branch=v_w_attn_branch, w_out=v_w_out, g_norm_ffn2=v_g_norm_ffn2,
               w_ffn2_in=v_w_ffn2_in, w_ffn2_out=v_w_ffn2_out, g_final=v_g_final)
    ix, iy, ic = _place()
    chip = 2 * ix + iy
    me = 4 * ix + 2 * iy + ic
    nada = w_ada.shape[2]

    c_all = _gather_small(c)[:, 0, :]
    b_shard = lax.dynamic_slice_in_dim(b_ada, chip * nada, nada, axis=1)
    mod_cols = _ada_fwd(c_all, w_ada[0], b_shard)
    mod_all = _gather_small(mod_cols)
    mod = jnp.concatenate([lax.dynamic_index_in_dim(mod_all[4 * (kk >> 1) + 2 * (kk & 1)], me, axis=0, keepdims=False)
                           for kk in range(4)])

    ws = _Sharded({n: w[n][0].astype(BF16) for n in BIG})
    loss, dx, dmod, dgains, g_w_pool, g_pool_scale, grads = _example_step(
        x[0], loss_target[0], positions[0], mod, (g_norm_ffn1[0], g_norm_mix[0], g_norm_ffn2[0], g_final),
        w_pool[0], pool_scale[0], ws)

    small_g = dict(b_ada=dmod, g_norm_ffn1=dgains[0], g_norm_mix=dgains[1], g_norm_ffn2=dgains[2], g_final=dgains[3],
                   pool_scale=g_pool_scale, w_pool=g_w_pool)
    gathered = _gather_small(_pack_small(small_g))
    sg, sd, sm, sv = _adam_small(_pack_small(w), _pack_small(mom), _pack_small(var), gathered)
    small_out = [_unpack_small(t, w) for t in (sg, sd, sm, sv)]

    dmod_all = gathered[:, 0, :NMOD * D]
    dmod_cols = lax.dynamic_slice_in_dim(dmod_all, chip * nada, nada, axis=1)
    g_ada = _ada_bwd(c_all, dmod_cols)
    ada_out = _adam(w_ada[0], m_w_ada[0], v_w_ada[0], [g_ada], "adam_w_ada")

    partial = []
    for n in BIG:
        ax, r = BIG_AXIS[n], ws.recv[n]
        size = r.shape[1 + ax]
        own = lax.dynamic_slice_in_dim(grads[n], chip * size, size, axis=ax)
        partial.append(_sum4(own, r, "sum_" + n))
    other = _swap_sibling(partial)
    big_out = {n: _adam(w[n][0], mom[n][0], var[n][0], [pa, pb], "adam_" + n) for n, pa, pb in zip(BIG, partial, other)}

    def leaf(kind, n):
        if n == "w_ada":
            return ada_out[kind][None]
        if n in big_out:
            return big_out[n][kind][None]
        return small_out[kind][n]

    loss = lax.psum(loss, ("x", "y", "c"))
    return (loss, dx[None], *[leaf(kind, n) for kind in range(4) for n in WEIGHTS])
```

```python
import jax
import jax.numpy as jnp
from jax import lax
from jax.experimental import pallas as pl
from jax.experimental.pallas import tpu as pltpu

F32 = jnp.float32
BF16 = jnp.bfloat16

D = 1024
FF = 2816
FC = 1408
PW = 256
GA = 256
HD = 64
LANES = 128
NH = GA // HD
NG = 3
DIL = (1, 4, 16)
BLK = 128
GW = 2 * D
INW = PW + 3 * NG * GA + GW
NMOD = 9
POOL_WINDOWS = (2, 4, 8, 16)
HALO = 16
EPS = 1e-6
SCALE = HD ** -0.5
NEG = -1e30

LR, B1, B2, AEPS, WD, STEP = 0.001, 0.9, 0.999, 1e-08, 0.01, 10

VMEM_BIG = 56 * 1024 * 1024
TM = 256

MESH = pl.DeviceIdType.MESH
ANY = pl.BlockSpec(memory_space=pl.ANY)


def _call(body, name, grid, in_specs, out_specs, out_shape, scratch=(), vmem=None, comm=None):
    params = pltpu.CompilerParams(dimension_semantics=("arbitrary",) * len(grid), vmem_limit_bytes=vmem)
    n_in, n_out, n_scr = len(in_specs), len(out_shape), len(scratch)
    if comm is None:
        call = pl.pallas_call(body, name=name, grid=grid, in_specs=list(in_specs), out_specs=list(out_specs),
                              out_shape=list(out_shape), scratch_shapes=list(scratch), compiler_params=params)
        return lambda *args: (call(*args), ())
    nc = len(comm.inputs)

    def body_with_comm(*refs):
        ins, refs = refs[:n_in], refs[n_in:]
        c_ins, refs = refs[:nc], refs[nc:]
        outs, refs = refs[:n_out], refs[n_out:]
        c_outs, refs = refs[:nc], refs[nc:]
        scr, sems = refs[:n_scr], refs[n_scr:]
        first = pl.program_id(0) == 0
        last = pl.program_id(0) == grid[0] - 1
        for ax in range(1, len(grid)):
            first = jnp.logical_and(first, pl.program_id(ax) == 0)
            last = jnp.logical_and(last, pl.program_id(ax) == grid[ax] - 1)

        @pl.when(first)
        def _():
            comm.start(c_ins, c_outs, sems)

        body(*ins, *outs, *scr)

        @pl.when(last)
        def _():
            comm.wait(c_ins, c_outs, sems)

    call = pl.pallas_call(
        body_with_comm, name=name, grid=grid, in_specs=list(in_specs) + [ANY] * nc,
        out_specs=list(out_specs) + [ANY] * nc, out_shape=list(out_shape) + list(comm.out_shapes),
        scratch_shapes=list(scratch) + list(comm.sem_shapes), compiler_params=params)

    def run(*args):
        res = call(*args, *comm.inputs)
        return res[:n_out], res[n_out:]

    return run


def _rows(tm, n):
    return pl.BlockSpec((tm, n), lambda i: (i, 0))


def _const(shape):
    return pl.BlockSpec(shape, lambda i: (0,) * len(shape))


def _sds(shape, dtype):
    return jax.ShapeDtypeStruct(shape, dtype)


def _dot(a, b):
    return jnp.dot(a, b, preferred_element_type=F32)


def _dot_nt(a, b):
    return lax.dot_general(a, b, (((1,), (1,)), ((), ())), preferred_element_type=F32)


def _dot_tn(a, b):
    return lax.dot_general(a, b, (((0,), (0,)), ((), ())), preferred_element_type=F32)


def _colsum(v):
    return jnp.sum(v, axis=0, keepdims=True)


def _norm_fwd(h, g, sh, sc):
    r = lax.rsqrt(jnp.mean(h * h, axis=-1, keepdims=True) + EPS)
    xh = h * r
    n = xh * g
    return xh, r, n, n * (1.0 + sc) + sh


def _norm_bwd(du, xh, r, n, g, sc):
    dn = du * (1.0 + sc)
    dxh = dn * g
    dh = r * (dxh - xh * jnp.mean(dxh * xh, axis=-1, keepdims=True))
    return dh, _colsum(du), _colsum(du * n), _colsum(dn * xh)


def _load_once(pairs, sems):
    @pl.when(pl.program_id(0) == 0)
    def _():
        cps = [pltpu.make_async_copy(src, dst, sems.at[j]) for j, (src, dst) in enumerate(pairs)]
        for cp in cps:
            cp.start()
        for cp in cps:
            cp.wait()


def _zero_first(ref):
    @pl.when(pl.program_id(0) == 0)
    def _():
        ref[...] = jnp.zeros(ref.shape, ref.dtype)


def _ffn_fwd(h, vec, win, wout, name, comm=None):
    T = h.shape[0]

    def body(h_ref, vec_ref, win_hbm, wout_hbm, ho_ref, u_ref, a_ref, b_ref, f_ref, win_v, wout_v, sems):
        _load_once([(win_hbm, win_v), (wout_hbm, wout_v)], sems)
        hh = h_ref[...]
        g, sh, sc, gt = vec_ref[0:1, :], vec_ref[1:2, :], vec_ref[2:3, :], vec_ref[3:4, :]
        _, _, _, u = _norm_fwd(hh, g, sh, sc)
        ub = u.astype(BF16)
        u_ref[...] = ub
        acc = None
        for j in range(FF // FC):
            lo, hi = j * FC, (j + 1) * FC
            a = _dot(ub, win_v[:, lo:hi])
            b = _dot(ub, win_v[:, FF + lo:FF + hi])
            a_ref[:, lo:hi] = a.astype(BF16)
            b_ref[:, lo:hi] = b.astype(BF16)
            s = (a * jax.nn.sigmoid(a) * b).astype(BF16)
            part = _dot(s, wout_v[lo:hi, :])
            acc = part if acc is None else acc + part
        f_ref[...] = acc.astype(BF16)
        ho_ref[...] = hh + 0.5 * gt * acc

    return _call(
        body, name, (T // TM,),
        [_rows(TM, D), _const((8, D)), ANY, ANY],
        [_rows(TM, D), _rows(TM, D), _rows(TM, FF), _rows(TM, FF), _rows(TM, D)],
        [_sds((T, D), F32), _sds((T, D), BF16), _sds((T, FF), BF16), _sds((T, FF), BF16), _sds((T, D), BF16)],
        scratch=[pltpu.VMEM((D, 2 * FF), BF16), pltpu.VMEM((FF, D), BF16), pltpu.SemaphoreType.DMA((2,))],
        vmem=VMEM_BIG, comm=comm,
    )(h, vec, win, wout)


def _ffn_bwd(dh, h, a, b, f, vec, win, wout, name, comm=None):
    T = h.shape[0]

    def body(dh_ref, h_ref, a_ref, b_ref, f_ref, vec_ref, win_hbm, wout_hbm,
             dhi_ref, dab_ref, s_ref, df_ref, acc_ref, win_v, wout_v, sems):
        _load_once([(win_hbm, win_v), (wout_hbm, wout_v)], sems)
        _zero_first(acc_ref)
        g, sh, sc, gt = vec_ref[0:1, :], vec_ref[1:2, :], vec_ref[2:3, :], vec_ref[3:4, :]
        dho = dh_ref[...]
        df = (0.5 * gt * dho).astype(BF16)
        df_ref[...] = df
        dgt = _colsum(0.5 * dho * f_ref[...].astype(F32))
        du = None
        for j in range(FF // FC):
            lo, hi = j * FC, (j + 1) * FC
            av = a_ref[:, lo:hi].astype(F32)
            bv = b_ref[:, lo:hi].astype(F32)
            ds = _dot_nt(df, wout_v[lo:hi, :])
            sig = jax.nn.sigmoid(av)
            sa = av * sig
            s_ref[:, lo:hi] = (sa * bv).astype(BF16)
            da = (ds * bv * (sig * (1.0 + av * (1.0 - sig)))).astype(BF16)
            db = (ds * sa).astype(BF16)
            dab_ref[:, lo:hi] = da
            dab_ref[:, FF + lo:FF + hi] = db
            part = _dot_nt(da, win_v[:, lo:hi]) + _dot_nt(db, win_v[:, FF + lo:FF + hi])
            du = part if du is None else du + part
        xh, r, n, _ = _norm_fwd(h_ref[...], g, sh, sc)
        dhn, dsh, dsc, dg = _norm_bwd(du, xh, r, n, g, sc)
        dhi_ref[...] = dho + dhn
        acc_ref[0:1, :] += dsh
        acc_ref[1:2, :] += dsc
        acc_ref[2:3, :] += dg
        acc_ref[3:4, :] += dgt

    return _call(
        body, name, (T // TM,),
        [_rows(TM, D), _rows(TM, D), _rows(TM, FF), _rows(TM, FF), _rows(TM, D), _const((8, D)), ANY, ANY],
        [_rows(TM, D), _rows(TM, 2 * FF), _rows(TM, FF), _rows(TM, D), _const((8, D))],
        [_sds((T, D), F32), _sds((T, 2 * FF), BF16), _sds((T, FF), BF16), _sds((T, D), BF16), _sds((8, D), F32)],
        scratch=[pltpu.VMEM((D, 2 * FF), BF16), pltpu.VMEM((FF, D), BF16), pltpu.SemaphoreType.DMA((2,))],
        vmem=VMEM_BIG, comm=comm,
    )(dh, h, a, b, f, vec, win, wout)


def _wgrad(x, y, name, tk, tn, tt, out_dtype=BF16, comm=None):
    T, K = x.shape
    N = y.shape[1]
    nt = T // tt

    def body(x_ref, y_ref, o_ref, acc_ref):
        t = pl.program_id(2)
        part = _dot_tn(x_ref[...], y_ref[...])

        @pl.when(t == 0)
        def _():
            acc_ref[...] = part

        @pl.when(t > 0)
        def _():
            acc_ref[...] += part

        @pl.when(t == nt - 1)
        def _():
            o_ref[...] = acc_ref[...].astype(out_dtype)

    (out,), c_outs = _call(
        body, name, (K // tk, N // tn, nt),
        [pl.BlockSpec((tt, tk), lambda i, j, t: (t, i)), pl.BlockSpec((tt, tn), lambda i, j, t: (t, j))],
        [pl.BlockSpec((tk, tn), lambda i, j, t: (i, j))], [_sds((K, N), out_dtype)],
        scratch=[pltpu.VMEM((tk, tn), F32)], vmem=VMEM_BIG, comm=comm,
    )(x, y)
    return out, c_outs


def _final_loss(h, tgt, gvec):
    T = h.shape[0]

    def body(h_ref, t_ref, g_ref, dh_ref, acc_ref):
        _zero_first(acc_ref)
        hh = h_ref[...]
        g = g_ref[0:1, :]
        r = lax.rsqrt(jnp.mean(hh * hh, axis=-1, keepdims=True) + EPS)
        xh = hh * r
        err = xh * g - t_ref[...]
        dy = err * (1.0 / D)
        dxh = dy * g
        dh_ref[...] = r * (dxh - xh * jnp.mean(dxh * xh, axis=-1, keepdims=True))
        acc_ref[0:1, :] += _colsum(err * err)
        acc_ref[1:2, :] += _colsum(dy * xh)

    return _call(
        body, "final_loss", (T // TM,),
        [_rows(TM, D), _rows(TM, D), _const((8, D))],
        [_rows(TM, D), _const((8, D))],
        [_sds((T, D), F32), _sds((8, D), F32)],
    )(h, tgt, gvec)[0]


def _swap_halves(t):
    w = t.shape[1]
    lane = lax.broadcasted_iota(jnp.int32, t.shape, 1)
    return jnp.where(lane % HD < HD // 2, pltpu.roll(t, w - HD // 2, 1), pltpu.roll(t, HD // 2, 1))


def _rope(t, cos, sin_signed):
    c = jnp.tile(cos, (1, t.shape[1] // cos.shape[1]))
    s = jnp.tile(sin_signed, (1, t.shape[1] // sin_signed.shape[1]))
    return t * c + _swap_halves(t) * s


def _rope_bwd(dt, cos, sin_signed):
    c = jnp.tile(cos, (1, dt.shape[1] // cos.shape[1]))
    s = jnp.tile(sin_signed, (1, dt.shape[1] // sin_signed.shape[1]))
    return dt * c + _swap_halves(dt * s)


def _rm_spec(dil):
    return pl.BlockSpec((dil, TM // dil, GA), lambda i: (0, i, 0))


def _to_residues(t, dst_ref, scr_ref, dil):
    if dil == 1:
        dst_ref[0] = t.astype(dst_ref.dtype)
        return
    for j in range(GA // LANES):
        scr_ref[j] = t[:, j * LANES:(j + 1) * LANES]
    for r in range(dil):
        for j in range(GA // LANES):
            rows = scr_ref.at[j][pl.ds(r, TM // dil, stride=dil), :]
            dst_ref[r, :, j * LANES:(j + 1) * LANES] = rows.astype(dst_ref.dtype)


def _from_residues(src_ref, scr_ref, dil):
    if dil == 1:
        return src_ref[0].astype(F32)
    for r in range(dil):
        for j in range(GA // LANES):
            scr_ref.at[j][pl.ds(r, TM // dil, stride=dil), :] = src_ref[r, :, j * LANES:(j + 1) * LANES].astype(F32)
    return jnp.concatenate([scr_ref[j] for j in range(GA // LANES)], axis=1)


def _mix_proj(h, vec, win, cos, sin, comm=None):
    T = h.shape[0]

    def body(h_ref, vec_ref, win_hbm, cos_ref, sin_ref, u_ref, p_ref, *rest):
        qkv_refs, gates_ref, win_v, scr_ref, sems = rest[:3 * NG], rest[3 * NG], rest[3 * NG + 1], rest[3 * NG + 2], rest[3 * NG + 3]
        _load_once([(win_hbm, win_v)], sems)
        g, sh, sc = vec_ref[0:1, :], vec_ref[1:2, :], vec_ref[2:3, :]
        _, _, _, u = _norm_fwd(h_ref[...], g, sh, sc)
        ub = u.astype(BF16)
        u_ref[...] = ub
        p_ref[...] = _dot(ub, win_v[:, 0:PW])
        cos_t, sin_t = cos_ref[...], sin_ref[...]
        for j in range(3 * NG):
            col = PW + j * GA
            t = _dot(ub, win_v[:, col:col + GA])
            if j < 2 * NG:
                t = _rope(t, cos_t, sin_t)
            _to_residues(t, qkv_refs[j], scr_ref, DIL[j % NG])
        for j in range(GW // 512):
            col = PW + 3 * NG * GA + j * 512
            gates_ref[:, j * 512:(j + 1) * 512] = jax.nn.sigmoid(_dot(ub, win_v[:, col:col + 512])).astype(BF16)

    outs, c_outs = _call(
        body, "mix_proj", (T // TM,),
        [_rows(TM, D), _const((8, D)), ANY, _rows(TM, 128), _rows(TM, 128)],
        [_rows(TM, D), _rows(TM, PW)] + [_rm_spec(d) for d in DIL] * 3 + [_rows(TM, GW)],
        [_sds((T, D), BF16), _sds((T, PW), F32)] + [_sds((d, T // d, GA), BF16) for d in DIL] * 3 + [_sds((T, GW), BF16)],
        scratch=[pltpu.VMEM((D, INW), BF16), pltpu.VMEM((GA // LANES, TM, LANES), F32), pltpu.SemaphoreType.DMA((1,))],
        vmem=VMEM_BIG, comm=comm,
    )(h, vec, win, cos, sin)
    return (outs[0], outs[1], outs[2:2 + NG], outs[2 + NG:2 + 2 * NG], outs[2 + 2 * NG:2 + 3 * NG], outs[2 + 3 * NG]), c_outs


def _head_masks():
    lane_head = lax.broadcasted_iota(jnp.int32, (BLK, GA), 1) // HD
    return [lane_head == hd for hd in range(NH)]


def _expand_heads(t, hm):
    return jnp.concatenate([jnp.where(m, t, jnp.zeros_like(t)) for m in hm], axis=0)


def _collapse_heads(tb, hm):
    out = None
    for hd, m in enumerate(hm):
        part = jnp.where(m, tb[hd * BLK:(hd + 1) * BLK, :], 0.0)
        out = part if out is None else out + part
    return out


def _head_rows(t):
    return jnp.concatenate([t[:, hd * HD:hd * HD + 1] for hd in range(NH)], axis=0)


def _band_masks():
    a = lax.broadcasted_iota(jnp.int32, (NH * BLK, BLK), 0) & (BLK - 1)
    c = lax.broadcasted_iota(jnp.int32, (NH * BLK, BLK), 1)
    return c <= a, c >= a


def _attn_specs(nbt):
    cur = pl.BlockSpec((BLK, GA), lambda i: (i, 0))
    prev = pl.BlockSpec((BLK, GA), lambda i: (jnp.maximum(i - 1, 0), 0))
    nxt = pl.BlockSpec((BLK, GA), lambda i: (jnp.minimum(i + 1, nbt - 1), 0))
    return cur, prev, nxt


def _attn_fwd(q, k, v, nb, name, comm=None):
    T = q.shape[0]
    nbt = T // BLK

    def body(q_ref, kc_ref, kp_ref, vc_ref, vp_ref, o_ref, lse_ref):
        i = pl.program_id(0)
        has_prev = (i & (nb - 1)) != 0
        hm = _head_masks()
        m_cur, m_prev = _band_masks()
        m_prev = jnp.logical_and(m_prev, has_prev)
        qb = _expand_heads(q_ref[...], hm)
        s_c = jnp.where(m_cur, _dot_nt(qb, kc_ref[...]) * SCALE, NEG)
        s_p = jnp.where(m_prev, _dot_nt(qb, kp_ref[...]) * SCALE, NEG)
        mx = jnp.maximum(jnp.max(s_c, axis=-1, keepdims=True), jnp.max(s_p, axis=-1, keepdims=True))
        e_c = jnp.exp(s_c - mx)
        e_p = jnp.exp(s_p - mx)
        l = jnp.sum(e_c, axis=-1, keepdims=True) + jnp.sum(e_p, axis=-1, keepdims=True)
        inv = 1.0 / l
        ob = _dot((e_c * inv).astype(BF16), vc_ref[...]) + _dot((e_p * inv).astype(BF16), vp_ref[...])
        o_ref[...] = _collapse_heads(ob, hm)
        lse_ref[...] = _collapse_heads(jnp.broadcast_to(mx + jnp.log(l), (NH * BLK, GA)), hm)

    cur, prev, _ = _attn_specs(nbt)
    return _call(body, name, (nbt,), [cur, cur, prev, cur, prev], [cur, cur],
                 [_sds((T, GA), F32), _sds((T, GA), F32)], comm=comm)(q, k, k, v, v)


def _attn_bwd(q, k, v, do, lse, e, nb, name, comm=None):
    T = q.shape[0]
    nbt = T // BLK

    def body(q_ref, kc_ref, vc_ref, do_ref, lse_ref, e_ref, kp_ref, vp_ref, qn_ref, don_ref, lsen_ref, en_ref,
             dq_ref, dk_ref, dv_ref):
        i = pl.program_id(0)
        has_prev = (i & (nb - 1)) != 0
        has_next = ((i + 1) & (nb - 1)) != 0
        hm = _head_masks()
        m_cur, m_band = _band_masks()
        m_prev = jnp.logical_and(m_band, has_prev)
        m_next = jnp.logical_and(m_band, has_next)
        kc, kp, vc, vp = kc_ref[...], kp_ref[...], vc_ref[...], vp_ref[...]
        qb, dob = _expand_heads(q_ref[...], hm), _expand_heads(do_ref[...], hm)
        lse_r, e_r = _head_rows(lse_ref[...]), _head_rows(e_ref[...])
        p_c = jnp.where(m_cur, jnp.exp(_dot_nt(qb, kc) * SCALE - lse_r), 0.0)
        p_p = jnp.where(m_prev, jnp.exp(_dot_nt(qb, kp) * SCALE - lse_r), 0.0)
        ds_c = (p_c * (_dot_nt(dob, vc) + e_r)).astype(BF16)
        ds_p = (p_p * (_dot_nt(dob, vp) + e_r)).astype(BF16)
        dq_ref[...] = _collapse_heads((_dot(ds_c, kc) + _dot(ds_p, kp)) * SCALE, hm)
        qnb, donb = _expand_heads(qn_ref[...], hm), _expand_heads(don_ref[...], hm)
        p_n = jnp.where(m_next, jnp.exp(_dot_nt(qnb, kc) * SCALE - _head_rows(lsen_ref[...])), 0.0)
        ds_n = (p_n * (_dot_nt(donb, vc) + _head_rows(en_ref[...]))).astype(BF16)
        dk_ref[...] = (_dot_tn(ds_c, qb) + _dot_tn(ds_n, qnb)) * SCALE
        dv_ref[...] = (_dot_tn(p_c.astype(BF16), dob) + _dot_tn(p_n.astype(BF16), donb)).astype(BF16)

    cur, prev, nxt = _attn_specs(nbt)
    return _call(body, name, (nbt,), [cur] * 6 + [prev, prev] + [nxt] * 4, [cur, cur, cur],
                 [_sds((T, GA), F32), _sds((T, GA), F32), _sds((T, GA), BF16)],
                 comm=comm)(q, k, v, do, lse, e, k, v, q, do, lse, e)


def _flat(t):
    return t.reshape(t.shape[0] * t.shape[1], t.shape[2])


def _by_residue(t, dil):
    return t.reshape(dil, t.shape[0] // dil, t.shape[1])


def _pool_consts(shape, row0):
    lane = lax.broadcasted_iota(jnp.int32, shape, 1)
    t = lax.broadcasted_iota(jnp.int32, shape, 0) + row0
    grp = lane // (PW // len(POOL_WINDOWS))
    win = jnp.where(grp == 0, POOL_WINDOWS[0], jnp.where(grp == 1, POOL_WINDOWS[1],
                    jnp.where(grp == 2, POOL_WINDOWS[2], POOL_WINDOWS[3])))
    cnt = jnp.minimum(t + 1, win).astype(F32)
    return grp, cnt


def _window_sums(ext_ref, base, step, tm):
    outs, run = [], None
    for j in range(POOL_WINDOWS[-1]):
        sl = ext_ref[pl.ds(base + step * j, tm), :]
        run = sl if run is None else run + sl
        if j + 1 in POOL_WINDOWS:
            outs.append(run)
    return outs


def _select_group(grp, vals):
    return jnp.where(grp == 0, vals[0], jnp.where(grp == 1, vals[1], jnp.where(grp == 2, vals[2], vals[3])))


def _pool_d(pc_ref, pp_ref, ext_ref, i, tm):
    ext_ref[0:HALO, :] = jnp.where(i > 0, pp_ref[tm - HALO:tm, :], 0.0)
    ext_ref[HALO:HALO + tm, :] = pc_ref[...]
    grp, cnt = _pool_consts((tm, PW), i * tm)
    sums = _window_sums(ext_ref, HALO, -1, tm)
    return _select_group(grp, sums) / cnt - pc_ref[...]


def _group_weights(ls):
    mx = jnp.maximum(jnp.maximum(ls[0], ls[1]), ls[2])
    es = [jnp.exp(l - mx) for l in ls]
    inv = 1.0 / (es[0] + es[1] + es[2])
    return [e * inv for e in es]


def _mix_merge(h, vec, p, os, lses, gates, wp_bd, pscale, wpb, wab, wout):
    T = h.shape[0]

    def body(h_ref, vec_ref, pc_ref, pp_ref, o0, o1, o2, l0, l1, l2, gates_ref, wp_ref, ps_ref, wpb_ref, wab_ref, wout_ref,
             ho_ref, yp_ref, ya_ref, mg_ref, mo_ref, d_ref, ext_ref, scr_ref):
        i = pl.program_id(0)
        gt = vec_ref[3:4, :]
        d = _pool_d(pc_ref, pp_ref, ext_ref, i, TM).astype(BF16)
        d_ref[...] = d
        ypool = (_dot(d, wp_ref[...]) * ps_ref[0:1, :]).astype(BF16)
        yp_ref[...] = ypool
        w = _group_weights([_from_residues(r, scr_ref, dl) for r, dl in zip((l0, l1, l2), DIL)])
        yattn = None
        for wg, o_ref, dl in zip(w, (o0, o1, o2), DIL):
            part = wg * _from_residues(o_ref, scr_ref, dl)
            yattn = part if yattn is None else yattn + part
        yattn = yattn.astype(BF16)
        ya_ref[...] = yattn
        merged = (gates_ref[:, 0:D].astype(F32) * _dot(ypool, wpb_ref[...])
                  + gates_ref[:, D:GW].astype(F32) * _dot(yattn, wab_ref[...])).astype(BF16)
        mg_ref[...] = merged
        mo = _dot(merged, wout_ref[...])
        mo_ref[...] = mo.astype(BF16)
        ho_ref[...] = h_ref[...] + gt * mo

    prev = pl.BlockSpec((TM, PW), lambda i: (jnp.maximum(i - 1, 0), 0))
    return _call(
        body, "mix_merge", (T // TM,),
        [_rows(TM, D), _const((8, D)), _rows(TM, PW), prev] + [_rm_spec(dl) for dl in DIL] * 2 + [_rows(TM, GW), _const((PW, PW)),
         _const((8, PW)), _const((PW, D)), _const((GA, D)), _const((D, D))],
        [_rows(TM, D), _rows(TM, PW), _rows(TM, GA), _rows(TM, D), _rows(TM, D), _rows(TM, PW)],
        [_sds((T, D), F32), _sds((T, PW), BF16), _sds((T, GA), BF16), _sds((T, D), BF16), _sds((T, D), BF16), _sds((T, PW), BF16)],
        scratch=[pltpu.VMEM((TM + HALO, PW), F32), pltpu.VMEM((GA // LANES, TM, LANES), F32)],
        vmem=VMEM_BIG,
    )(h, vec, p, p, *os, *lses, gates, wp_bd, pscale, wpb, wab, wout)[0]


def _mix_bwd_a(dh, vec, mixout, gates, ypool, yattn, dpool, os, lses, wp_bd, pscale, wpb, wab, wout, ones_bd, comm=None):
    T = dh.shape[0]

    def body(dh_ref, vec_ref, mo_ref, gates_ref, yp_ref, ya_ref, d_ref, o0, o1, o2, l0, l1, l2,
             wp_ref, ps_ref, wpb_ref, wab_ref, wout_ref, ones_ref,
             dmo_ref, dp_ref, da_ref, dgates_ref, do0, do1, do2, e0, e1, e2, dd_ref, dyp_ref, acc_ref, acc2_ref, scr_ref):
        _zero_first(acc_ref)
        _zero_first(acc2_ref)
        gt = vec_ref[3:4, :]
        dho = dh_ref[...]
        acc_ref[3:4, :] += _colsum(dho * mo_ref[...].astype(F32))
        dmo = (gt * dho).astype(BF16)
        dmo_ref[...] = dmo
        dmerged = _dot_nt(dmo, wout_ref[...])
        gp = gates_ref[:, 0:D].astype(F32)
        ga = gates_ref[:, D:GW].astype(F32)
        bp = _dot(yp_ref[...], wpb_ref[...])
        ba = _dot(ya_ref[...], wab_ref[...])
        dgates_ref[:, 0:D] = (dmerged * bp * gp * (1.0 - gp)).astype(BF16)
        dgates_ref[:, D:GW] = (dmerged * ba * ga * (1.0 - ga)).astype(BF16)
        dbp = (dmerged * gp).astype(BF16)
        dba = (dmerged * ga).astype(BF16)
        dp_ref[...] = dbp
        da_ref[...] = dba
        dypool = _dot_nt(dbp, wpb_ref[...])
        ypre = _dot(d_ref[...], wp_ref[...])
        acc2_ref[0:1, :] += _colsum(dypool * ypre)
        dyp = (dypool * ps_ref[0:1, :]).astype(BF16)
        dyp_ref[...] = dyp
        dd_ref[...] = _dot_nt(dyp, wp_ref[...])
        dya = _dot_nt(dba, wab_ref[...])
        w = _group_weights([_from_residues(r, scr_ref, dl) for r, dl in zip((l0, l1, l2), DIL)])
        ya = None
        for wg, o_ref, dl in zip(w, (o0, o1, o2), DIL):
            part = wg * _from_residues(o_ref, scr_ref, dl)
            ya = part if ya is None else ya + part
        prod = dya * ya
        hi = prod.astype(BF16)
        lo = (prod - hi.astype(F32)).astype(BF16)
        tot = _dot(hi, ones_ref[...]) + _dot(lo, ones_ref[...])
        for wg, do_ref, e_ref, dl in zip(w, (do0, do1, do2), (e0, e1, e2), DIL):
            _to_residues(wg * dya, do_ref, scr_ref, dl)
            _to_residues(-wg * tot, e_ref, scr_ref, dl)

    return _call(
        body, "mix_bwd_a", (T // TM,),
        [_rows(TM, D), _const((8, D)), _rows(TM, D), _rows(TM, GW), _rows(TM, PW), _rows(TM, GA), _rows(TM, PW)]
        + [_rm_spec(dl) for dl in DIL] * 2
        + [_const((PW, PW)), _const((8, PW)), _const((PW, D)), _const((GA, D)), _const((D, D)), _const((GA, GA))],
        [_rows(TM, D)] * 3 + [_rows(TM, GW)] + [_rm_spec(dl) for dl in DIL] * 2
        + [_rows(TM, PW), _rows(TM, PW), _const((8, D)), _const((8, PW))],
        [_sds((T, D), BF16)] * 3 + [_sds((T, GW), BF16)] + [_sds((dl, T // dl, GA), BF16) for dl in DIL]
        + [_sds((dl, T // dl, GA), F32) for dl in DIL]
        + [_sds((T, PW), F32), _sds((T, PW), BF16), _sds((8, D), F32), _sds((8, PW), F32)],
        scratch=[pltpu.VMEM((GA // LANES, TM, LANES), F32)],
        vmem=VMEM_BIG, comm=comm,
    )(dh, vec, mixout, gates, ypool, yattn, dpool, *os, *lses, wp_bd, pscale, wpb, wab, wout, ones_bd)


def _mix_bwd_b(dh, h, vec, dd, dqs, dks, dvs, dgates, cos, sin, win):
    T = h.shape[0]
    nt = T // TM

    def body(dh_ref, h_ref, vec_ref, ddc_ref, ddn_ref, *rest):
        qk_refs, dv_refs = rest[:2 * NG], rest[2 * NG:3 * NG]
        dgates_ref, cos_ref, sin_ref, win_hbm, dhi_ref, dproj_ref, acc_ref, win_v, ext_ref, scr_ref, sems = rest[3 * NG:]
        i = pl.program_id(0)
        _load_once([(win_hbm, win_v)], sems)
        _zero_first(acc_ref)
        g, sh, sc = vec_ref[0:1, :], vec_ref[1:2, :], vec_ref[2:3, :]
        grp, cnt = _pool_consts((TM, PW), i * TM)
        _, cnt_n = _pool_consts((HALO, PW), (i + 1) * TM)
        ext_ref[0:TM, :] = ddc_ref[...] / cnt
        ext_ref[TM:TM + HALO, :] = jnp.where(i < nt - 1, ddn_ref[0:HALO, :] / cnt_n, 0.0)
        dp = _select_group(grp, _window_sums(ext_ref, 0, 1, TM)) - ddc_ref[...]
        dproj_ref[:, 0:PW] = dp.astype(BF16)
        cos_t, sin_t = cos_ref[...], sin_ref[...]
        for j in range(2 * NG):
            col = PW + j * GA
            dt = _from_residues(qk_refs[j], scr_ref, DIL[j % NG])
            dproj_ref[:, col:col + GA] = _rope_bwd(dt, cos_t, sin_t).astype(BF16)
        for j in range(NG):
            col = PW + (2 * NG + j) * GA
            dproj_ref[:, col:col + GA] = _from_residues(dv_refs[j], scr_ref, DIL[j]).astype(BF16)
        dproj_ref[:, PW + 3 * NG * GA:INW] = dgates_ref[...]
        du = None
        for j in range(INW // 512):
            part = _dot_nt(dproj_ref[:, j * 512:(j + 1) * 512], win_v[:, j * 512:(j + 1) * 512])
            du = part if du is None else du + part
        xh, r, n, _ = _norm_fwd(h_ref[...], g, sh, sc)
        dhn, dsh, dsc, dg = _norm_bwd(du, xh, r, n, g, sc)
        dhi_ref[...] = dh_ref[...] + dhn
        acc_ref[0:1, :] += dsh
        acc_ref[1:2, :] += dsc
        acc_ref[2:3, :] += dg

    nxt = pl.BlockSpec((TM, PW), lambda i: (jnp.minimum(i + 1, nt - 1), 0))
    return _call(
        body, "mix_bwd_b", (nt,),
        [_rows(TM, D), _rows(TM, D), _const((8, D)), _rows(TM, PW), nxt] + [_rm_spec(dl) for dl in DIL] * 3
        + [_rows(TM, GW), _rows(TM, 128), _rows(TM, 128), ANY],
        [_rows(TM, D), _rows(TM, INW), _const((8, D))],
        [_sds((T, D), F32), _sds((T, INW), BF16), _sds((8, D), F32)],
        scratch=[pltpu.VMEM((D, INW), BF16), pltpu.VMEM((TM + HALO, PW), F32), pltpu.VMEM((GA // LANES, TM, LANES), F32),
                 pltpu.SemaphoreType.DMA((1,))],
        vmem=VMEM_BIG,
    )(dh, h, vec, dd, dd, *dqs, *dks, *dvs, dgates, cos, sin, win)[0]


def _ada_fwd(c_all, w_shard, b_shard):
    n = w_shard.shape[1]

    def body(c_ref, w_ref, b_ref, o_ref):
        cv = c_ref[...]
        cond = (cv * jax.nn.sigmoid(cv)).astype(BF16)
        o_ref[...] = _dot(cond, w_ref[...].astype(BF16)) + b_ref[...]

    tn = n // 3
    return pl.pallas_call(
        body, name="ada_fwd", grid=(3,),
        in_specs=[pl.BlockSpec((8, D), lambda j: (0, 0)), pl.BlockSpec((D, tn), lambda j: (0, j)), pl.BlockSpec((1, tn), lambda j: (0, j))],
        out_specs=pl.BlockSpec((8, tn), lambda j: (0, j)), out_shape=_sds((8, n), F32),
        compiler_params=pltpu.CompilerParams(dimension_semantics=("arbitrary",)),
    )(c_all, w_shard, b_shard)


def _ada_bwd(c_all, dmod_shard):
    n = dmod_shard.shape[1]

    def body(c_ref, d_ref, o_ref):
        cv = c_ref[...]
        cond = (cv * jax.nn.sigmoid(cv)).astype(BF16)
        o_ref[...] = _dot_tn(cond, d_ref[...].astype(BF16))

    tn = n // 3
    return pl.pallas_call(
        body, name="ada_bwd", grid=(3,),
        in_specs=[pl.BlockSpec((8, D), lambda j: (0, 0)), pl.BlockSpec((8, tn), lambda j: (0, j))],
        out_specs=pl.BlockSpec((D, tn), lambda j: (0, j)), out_shape=_sds((D, n), F32),
        compiler_params=pltpu.CompilerParams(dimension_semantics=("arbitrary",)),
    )(c_all, dmod_shard)


def _adam_math(w, g, m, v):
    m2 = B1 * m + (1.0 - B1) * g
    v2 = B2 * v + (1.0 - B2) * (g * g)
    m_hat = m2 / (1.0 - B1 ** STEP)
    v_hat = v2 / (1.0 - B2 ** STEP)
    delta = -LR * (m_hat / (jnp.sqrt(v_hat) + AEPS) + WD * w)
    return delta, m2, v2


def _adam(w, m, v, parts, name):
    R, C = w.shape
    tr = R
    for cand in (128, 64, 32, 16, 8):
        if R % cand == 0:
            tr = cand
            break
    np_ = len(parts)

    def body(w_ref, m_ref, v_ref, *rest):
        p_refs, (g_ref, d_ref, m2_ref, v2_ref) = rest[:np_], rest[np_:]
        g = p_refs[0][...]
        for pr in p_refs[1:]:
            g = g + pr[...]
        delta, m2, v2 = _adam_math(w_ref[...], g, m_ref[...], v_ref[...])
        g_ref[...] = g
        d_ref[...] = delta
        m2_ref[...] = m2
        v2_ref[...] = v2

    spec = pl.BlockSpec((tr, C), lambda i: (i, 0))
    return pl.pallas_call(
        body, name=name, grid=(R // tr,), in_specs=[spec] * (3 + np_), out_specs=[spec] * 4,
        out_shape=[_sds((R, C), F32)] * 4,
        compiler_params=pltpu.CompilerParams(dimension_semantics=("arbitrary",), vmem_limit_bytes=VMEM_BIG),
    )(w, m, v, *parts)


def _adam_small(w, m, v, gathered):
    P = w.shape[1]

    def body(w_ref, m_ref, v_ref, ga_ref, g_ref, d_ref, m2_ref, v2_ref):
        g = ga_ref[0]
        for dev in range(1, 8):
            g = g + ga_ref[dev]
        delta, m2, v2 = _adam_math(w_ref[...], g, m_ref[...], v_ref[...])
        g_ref[...] = g
        d_ref[...] = delta
        m2_ref[...] = m2
        v2_ref[...] = v2

    return pl.pallas_call(body, name="adam_small", out_shape=[_sds((1, P), F32)] * 4)(w, m, v, gathered)


def _sum4(blocks, name):
    _, R, C = blocks.shape
    tr = R
    for cand in (256, 128, 64, 32, 16):
        if R % cand == 0:
            tr = cand
            break

    def body(r_ref, out_ref):
        out_ref[...] = ((r_ref[0].astype(F32) + r_ref[1].astype(F32)) + r_ref[2].astype(F32)) + r_ref[3].astype(F32)

    return pl.pallas_call(
        body, name=name, grid=(R // tr,),
        in_specs=[pl.BlockSpec((4, tr, C), lambda i: (0, i, 0))],
        out_specs=pl.BlockSpec((tr, C), lambda i: (i, 0)), out_shape=_sds((R, C), F32),
        compiler_params=pltpu.CompilerParams(dimension_semantics=("arbitrary",)),
    )(blocks)


def _place():
    return lax.axis_index("x"), lax.axis_index("y"), lax.axis_index("c")


def _gather_small(v):
    R, P = v.shape

    def body(v_ref, out_ref, send_sems, recv_sems):
        x, y, c = _place()
        me = 4 * x + 2 * y + c
        out_ref[me] = v_ref[...]
        copies = []
        for m in range(1, 8):
            peer = (x ^ (m >> 2), y ^ ((m >> 1) & 1), c ^ (m & 1))
            copies.append(pltpu.make_async_remote_copy(
                src_ref=v_ref, dst_ref=out_ref.at[me], send_sem=send_sems.at[m - 1], recv_sem=recv_sems.at[m - 1],
                device_id=peer, device_id_type=MESH))
        for cp in copies:
            cp.start()
        for m in range(1, 8):
            src = 4 * (x ^ (m >> 2)) + 2 * (y ^ ((m >> 1) & 1)) + (c ^ (m & 1))
            pltpu.make_async_remote_copy(
                src_ref=v_ref, dst_ref=out_ref.at[src], send_sem=send_sems.at[m - 1], recv_sem=recv_sems.at[m - 1],
                device_id=(x, y, c), device_id_type=MESH).wait_recv()
        for cp in copies:
            cp.wait_send()

    vm = pl.BlockSpec(memory_space=pltpu.VMEM)
    return pl.pallas_call(
        body, name="gather_small", in_specs=[vm], out_specs=vm, out_shape=_sds((8, R, P), F32),
        scratch_shapes=[pltpu.SemaphoreType.DMA((7,)), pltpu.SemaphoreType.DMA((7,))],
    )(v)


def _chip_peer(x, y, c, m):
    return (x ^ (m >> 1), y ^ (m & 1), c)


def _shard_ref(ref, axis, k, n):
    start = pl.multiple_of(k * n, 128 if axis == 1 else 16)
    return ref.at[:, pl.ds(start, n)] if axis == 1 else ref.at[pl.ds(start, n), :]


class _GatherPlan:
    def __init__(self, shards, axes):
        self.inputs, self.axes, nw = list(shards), list(axes), len(shards)
        self.out_shapes = [_sds((s.shape[0] * (4 if ax == 0 else 1), s.shape[1] * (4 if ax == 1 else 1)), BF16)
                           for s, ax in zip(shards, axes)]
        self.sem_shapes = [pltpu.SemaphoreType.DMA((nw,)), pltpu.SemaphoreType.DMA((nw, 3)), pltpu.SemaphoreType.DMA((nw, 3))]

    def _copies(self, ins, outs, sems):
        local_sems, send_sems, recv_sems = sems
        x, y, c = _place()
        k = 2 * x + y
        local, remote, arrivals = [], [], []
        for j, ax in enumerate(self.axes):
            n = ins[j].shape[ax]
            mine = _shard_ref(outs[j], ax, k, n)
            local.append(pltpu.make_async_copy(ins[j], mine, local_sems.at[j]))
            for m in range(1, 4):
                remote.append(pltpu.make_async_remote_copy(
                    src_ref=ins[j], dst_ref=mine, send_sem=send_sems.at[j, m - 1], recv_sem=recv_sems.at[j, m - 1],
                    device_id=_chip_peer(x, y, c, m), device_id_type=MESH))
                arrivals.append(pltpu.make_async_remote_copy(
                    src_ref=ins[j], dst_ref=_shard_ref(outs[j], ax, k ^ m, n), send_sem=send_sems.at[j, m - 1],
                    recv_sem=recv_sems.at[j, m - 1], device_id=(x, y, c), device_id_type=MESH))
        return local, remote, arrivals

    def start(self, ins, outs, sems):
        local, remote, _ = self._copies(ins, outs, sems)
        for cp in local + remote:
            cp.start()

    def wait(self, ins, outs, sems):
        local, remote, arrivals = self._copies(ins, outs, sems)
        for cp in arrivals:
            cp.wait_recv()
        for cp in remote:
            cp.wait_send()
        for cp in local:
            cp.wait()


class _ScatterPlan:
    def __init__(self, grads, axes):
        self.inputs, self.axes, nw = list(grads), list(axes), len(grads)
        self.shard_shapes = [(g.shape[0] // (4 if ax == 0 else 1), g.shape[1] // (4 if ax == 1 else 1))
                             for g, ax in zip(grads, axes)]
        self.out_shapes = [_sds((4,) + s, BF16) for s in self.shard_shapes]
        self.sem_shapes = [pltpu.SemaphoreType.DMA((nw,)), pltpu.SemaphoreType.DMA((nw, 3)), pltpu.SemaphoreType.DMA((nw, 3))]

    def _copies(self, ins, outs, sems):
        local_sems, send_sems, recv_sems = sems
        x, y, c = _place()
        k = 2 * x + y
        local, remote, arrivals = [], [], []
        for j, ax in enumerate(self.axes):
            n = self.shard_shapes[j][ax]
            local.append(pltpu.make_async_copy(_shard_ref(ins[j], ax, k, n), outs[j].at[0], local_sems.at[j]))
            for m in range(1, 4):
                remote.append(pltpu.make_async_remote_copy(
                    src_ref=_shard_ref(ins[j], ax, k ^ m, n), dst_ref=outs[j].at[m],
                    send_sem=send_sems.at[j, m - 1], recv_sem=recv_sems.at[j, m - 1],
                    device_id=_chip_peer(x, y, c, m), device_id_type=MESH))
                arrivals.append(pltpu.make_async_remote_copy(
                    src_ref=_shard_ref(ins[j], ax, k, n), dst_ref=outs[j].at[m],
                    send_sem=send_sems.at[j, m - 1], recv_sem=recv_sems.at[j, m - 1],
                    device_id=(x, y, c), device_id_type=MESH))
        return local, remote, arrivals

    def start(self, ins, outs, sems):
        local, remote, _ = self._copies(ins, outs, sems)
        for cp in local + remote:
            cp.start()

    def wait(self, ins, outs, sems):
        local, remote, arrivals = self._copies(ins, outs, sems)
        for cp in arrivals:
            cp.wait_recv()
        for cp in remote:
            cp.wait_send()
        for cp in local:
            cp.wait()


def _run_plan(plan, name):
    nc = len(plan.inputs)

    def body(*refs):
        ins, outs, sems = refs[:nc], refs[nc:2 * nc], refs[2 * nc:]
        plan.start(ins, outs, sems)
        plan.wait(ins, outs, sems)

    return pl.pallas_call(body, name=name, in_specs=[ANY] * nc, out_specs=[ANY] * nc, out_shape=list(plan.out_shapes),
                          scratch_shapes=list(plan.sem_shapes))(*plan.inputs)


def _swap_sibling(parts):
    nw = len(parts)

    def body(*refs):
        ins, outs = refs[:nw], refs[nw:2 * nw]
        send_sems, recv_sems = refs[2 * nw:]
        x, y, c = _place()
        copies = [pltpu.make_async_remote_copy(
            src_ref=ins[j], dst_ref=outs[j], send_sem=send_sems.at[j], recv_sem=recv_sems.at[j],
            device_id=(x, y, 1 - c), device_id_type=MESH) for j in range(nw)]
        for cp in copies:
            cp.start()
        for cp in copies:
            cp.wait()

    return pl.pallas_call(
        body, name="swap_sibling", in_specs=[ANY] * nw, out_specs=[ANY] * nw,
        out_shape=[_sds(p.shape, p.dtype) for p in parts],
        scratch_shapes=[pltpu.SemaphoreType.DMA((nw,)), pltpu.SemaphoreType.DMA((nw,))],
    )(*parts)


BIG = ("w_ffn1_in", "w_ffn1_out", "w_in", "w_pool_branch", "w_attn_branch", "w_out", "w_ffn2_in", "w_ffn2_out")
BIG_AXIS = {"w_ffn1_in": 1, "w_ffn1_out": 0, "w_in": 1, "w_pool_branch": 1, "w_attn_branch": 1, "w_out": 0,
            "w_ffn2_in": 1, "w_ffn2_out": 0}


class _Sharded:
    def __init__(self, shards):
        self.shards, self.full, self.recv = shards, {}, {}

    def gather_plan(self, names):
        return _GatherPlan([self.shards[n] for n in names], [BIG_AXIS[n] for n in names])

    def gather_now(self, names):
        self.gathered(names, _run_plan(self.gather_plan(names), "gather_" + names[0]))

    def gathered(self, names, outs):
        self.full.update(zip(names, outs))

    def scatter_plan(self, names, grads):
        return _ScatterPlan([grads[n] for n in names], [BIG_AXIS[n] for n in names])

    def scatter_now(self, names, grads):
        self.scattered(names, _run_plan(self.scatter_plan(names, grads), "scatter_" + names[0]))

    def scattered(self, names, outs):
        self.recv.update(zip(names, outs))


class _Whole:
    def __init__(self, full):
        self.full, self.recv = dict(full), {}

    def gather_plan(self, names):
        return None

    def gather_now(self, names):
        pass

    def gathered(self, names, outs):
        pass

    def scatter_plan(self, names, grads):
        return None

    def scatter_now(self, names, grads):
        pass

    def scattered(self, names, outs):
        pass


def _vec(rows):
    pad = [jnp.zeros((1, D), F32)] * (8 - len(rows))
    return jnp.concatenate([r.reshape(1, D) for r in rows] + pad, axis=0)


def _block_diag(w_pool):
    n, c = w_pool.shape[0], w_pool.shape[1]
    eye = jnp.eye(n, dtype=w_pool.dtype)
    return (eye[:, None, :, None] * w_pool[:, :, None, :]).reshape(n * c, n * c)


def _example_step(x, tgt, positions, mod, gains, w_pool, pool_scale, ws):
    T = x.shape[0]
    assert (T // BLK // DIL[-1]) & (T // BLK // DIL[-1] - 1) == 0, "blocks per sequence must be a power of two"
    sh1, sc1, gt1, sh2, sc2, gt2, sh3, sc3, gt3 = [mod[j * D:(j + 1) * D] for j in range(NMOD)]
    g1, g2, g3, gf = gains
    vec1, vec2, vec3 = _vec([g1, sh1, sc1, gt1]), _vec([g2, sh2, sc2, gt2]), _vec([g3, sh3, sc3, gt3])
    inv_freq = 10000.0 ** (-jnp.arange(0, HD, 2, dtype=F32) / HD)
    ang = positions.astype(F32)[:, None] * inv_freq
    cos = jnp.tile(jnp.cos(ang), (1, 4))
    sin = jnp.tile(jnp.concatenate([-jnp.sin(ang), jnp.sin(ang)], axis=1), (1, 2))
    wp_bd = _block_diag(w_pool).astype(BF16)
    ones_bd = _block_diag(jnp.ones((NH, HD, HD), F32)).astype(BF16)
    ps = jnp.concatenate([pool_scale.reshape(1, PW), jnp.zeros((7, PW), F32)], axis=0)
    wb = ws.full

    ws.gather_now(["w_ffn1_in", "w_ffn1_out"])
    mixw = ["w_in", "w_pool_branch", "w_attn_branch", "w_out"]
    (h1, u1, a1, b1, f1), got = _ffn_fwd(x, vec1, wb["w_ffn1_in"], wb["w_ffn1_out"], "ffn1_fwd", ws.gather_plan(mixw))
    ws.gathered(mixw, got)
    (u2, p, qs, ks, vs, gates), got = _mix_proj(h1, vec2, wb["w_in"], cos, sin, ws.gather_plan(["w_ffn2_in"]))
    ws.gathered(["w_ffn2_in"], got)
    qs, ks, vs = [_flat(t) for t in qs], [_flat(t) for t in ks], [_flat(t) for t in vs]
    nbs = [T // d // BLK for d in DIL]
    os, lses = [], []
    for gi in range(NG):
        plan = ws.gather_plan(["w_ffn2_out"]) if gi == 0 else None
        (o, lse), got = _attn_fwd(qs[gi], ks[gi], vs[gi], nbs[gi], f"attn_fwd{gi}", plan)
        if gi == 0:
            ws.gathered(["w_ffn2_out"], got)
        os.append(o)
        lses.append(lse)
    os_r = [_by_residue(t, d) for t, d in zip(os, DIL)]
    lses_r = [_by_residue(t, d) for t, d in zip(lses, DIL)]
    h2, ypool, yattn, merged, mixout, dpool = _mix_merge(
        h1, vec2, p, os_r, lses_r, gates, wp_bd, ps, wb["w_pool_branch"], wb["w_attn_branch"], wb["w_out"])
    (h3, u3, a3, b3, f3), _ = _ffn_fwd(h2, vec3, wb["w_ffn2_in"], wb["w_ffn2_out"], "ffn2_fwd")
    dh3, lacc = _final_loss(h3, tgt, _vec([gf]))
    loss = 0.5 * jnp.sum(lacc[0]) / D

    grads = {}
    (dh2, dab3, s3, df3, acc3), _ = _ffn_bwd(dh3, h2, a3, b3, f3, vec3, wb["w_ffn2_in"], wb["w_ffn2_out"], "ffn2_bwd")
    grads["w_ffn2_out"], _ = _wgrad(s3, df3, "wg_ffn2_out", FC, 512, 1024)
    grads["w_ffn2_in"], got = _wgrad(u3, dab3, "wg_ffn2_in", D, 512, 1024, comm=ws.scatter_plan(["w_ffn2_out"], grads))
    ws.scattered(["w_ffn2_out"], got)
    (dmo, dbp, dba, dgates, do0, do1, do2, e0, e1, e2, dd, dyp, acc2a, accps), got = _mix_bwd_a(
        dh2, vec2, mixout, gates, ypool, yattn, dpool, os_r, lses_r, wp_bd, ps,
        wb["w_pool_branch"], wb["w_attn_branch"], wb["w_out"], ones_bd, ws.scatter_plan(["w_ffn2_in"], grads))
    ws.scattered(["w_ffn2_in"], got)
    grads["w_out"], _ = _wgrad(merged, dmo, "wg_out", D, 512, 1024)
    grads["w_pool_branch"], _ = _wgrad(ypool, dbp, "wg_pool_branch", PW, 512, 1024)
    grads["w_attn_branch"], _ = _wgrad(yattn, dba, "wg_attn_branch", GA, 512, 1024)
    gwp, _ = _wgrad(dpool, dyp, "wg_pool", PW, PW, 1024, out_dtype=F32)
    n = len(POOL_WINDOWS)
    c = PW // n
    grad_w_pool = jnp.stack([gwp[j * c:(j + 1) * c, j * c:(j + 1) * c] for j in range(n)], axis=0)
    small3 = ["w_out", "w_pool_branch", "w_attn_branch"]
    dqs, dks, dvs = [], [], []
    for gi, (do, e) in enumerate(((do0, e0), (do1, e1), (do2, e2))):
        plan = ws.scatter_plan(small3, grads) if gi == 0 else None
        (dq, dk, dv), got = _attn_bwd(qs[gi], ks[gi], vs[gi], _flat(do), lses[gi], _flat(e), nbs[gi], f"attn_bwd{gi}", plan)
        if gi == 0:
            ws.scattered(small3, got)
        dqs.append(_by_residue(dq, DIL[gi]))
        dks.append(_by_residue(dk, DIL[gi]))
        dvs.append(_by_residue(dv, DIL[gi]))
    dh1, dproj, acc2b = _mix_bwd_b(dh2, h1, vec2, dd, dqs, dks, dvs, dgates, cos, sin, wb["w_in"])
    grads["w_in"], _ = _wgrad(u2, dproj, "wg_in", D, 512, 1024)
    (dx, dab1, s1, df1, acc1), got = _ffn_bwd(dh1, x, a1, b1, f1, vec1, wb["w_ffn1_in"], wb["w_ffn1_out"], "ffn1_bwd",
                                              ws.scatter_plan(["w_in"], grads))
    ws.scattered(["w_in"], got)
    grads["w_ffn1_out"], _ = _wgrad(s1, df1, "wg_ffn1_out", FC, 512, 1024)
    grads["w_ffn1_in"], got = _wgrad(u1, dab1, "wg_ffn1_in", D, 512, 1024, comm=ws.scatter_plan(["w_ffn1_out"], grads))
    ws.scattered(["w_ffn1_out"], got)
    ws.scatter_now(["w_ffn1_in"], grads)

    dmod = jnp.concatenate([acc1[0], acc1[1], acc1[3], acc2b[0], acc2b[1], acc2a[3], acc3[0], acc3[1], acc3[3]])
    dgains = jnp.stack([acc1[2], acc2b[2], acc3[2], lacc[1]], axis=0)
    return loss, dx, dmod, dgains, grad_w_pool, accps[0], grads


SMALL = ("b_ada", "g_norm_ffn1", "g_norm_mix", "g_norm_ffn2", "g_final", "pool_scale", "w_pool")
WEIGHTS = ("w_ada", "b_ada", "g_norm_ffn1", "w_ffn1_in", "w_ffn1_out", "g_norm_mix", "w_in", "w_pool", "pool_scale",
           "w_pool_branch", "w_attn_branch", "w_out", "g_norm_ffn2", "w_ffn2_in", "w_ffn2_out", "g_final")


def _pack_small(t):
    return jnp.concatenate([t[n].reshape(-1) for n in SMALL]).reshape(1, -1)


def _unpack_small(flat, like):
    out, off = {}, 0
    for n in SMALL:
        size = like[n].size
        out[n] = flat[0, off:off + size].reshape(like[n].shape)
        off += size
    return out


def kernel(x, c, positions, w_ada, b_ada, g_norm_ffn1, w_ffn1_in, w_ffn1_out, g_norm_mix, w_in, w_pool, pool_scale, w_pool_branch, w_attn_branch, w_out, g_norm_ffn2, w_ffn2_in, w_ffn2_out, g_final, loss_target, m_w_ada, m_b_ada, m_g_norm_ffn1, m_w_ffn1_in, m_w_ffn1_out, m_g_norm_mix, m_w_in, m_w_pool, m_pool_scale, m_w_pool_branch, m_w_attn_branch, m_w_out, m_g_norm_ffn2, m_w_ffn2_in, m_w_ffn2_out, m_g_final, v_w_ada, v_b_ada, v_g_norm_ffn1, v_w_ffn1_in, v_w_ffn1_out, v_g_norm_mix, v_w_in, v_w_pool, v_pool_scale, v_w_pool_branch, v_w_attn_branch, v_w_out, v_g_norm_ffn2, v_w_ffn2_in, v_w_ffn2_out, v_g_final):
    w = dict(w_ada=w_ada, b_ada=b_ada, g_norm_ffn1=g_norm_ffn1, w_ffn1_in=w_ffn1_in, w_ffn1_out=w_ffn1_out,
             g_norm_mix=g_norm_mix, w_in=w_in, w_pool=w_pool, pool_scale=pool_scale, w_pool_branch=w_pool_branch,
             w_attn_branch=w_attn_branch, w_out=w_out, g_norm_ffn2=g_norm_ffn2, w_ffn2_in=w_ffn2_in,
             w_ffn2_out=w_ffn2_out, g_final=g_final)
    mom = dict(w_ada=m_w_ada, b_ada=m_b_ada, g_norm_ffn1=m_g_norm_ffn1, w_ffn1_in=m_w_ffn1_in, w_ffn1_out=m_w_ffn1_out,
               g_norm_mix=m_g_norm_mix, w_in=m_w_in, w_pool=m_w_pool, pool_scale=m_pool_scale,
               w_pool_branch=m_w_pool_branch, w_attn_branch=m_w_attn_branch, w_out=m_w_out, g_norm_ffn2=m_g_norm_ffn2,
               w_ffn2_in=m_w_ffn2_in, w_ffn2_out=m_w_ffn2_out, g_final=m_g_final)
    var = dict(w_ada=v_w_ada, b_ada=v_b_ada, g_norm_ffn1=v_g_norm_ffn1, w_ffn1_in=v_w_ffn1_in, w_ffn1_out=v_w_ffn1_out,
               g_norm_mix=v_g_norm_mix, w_in=v_w_in, w_pool=v_w_pool, pool_scale=v_pool_scale,
               w_pool_branch=v_w_pool_branch, w_attn_branch=v_w_attn_branch, w_out=v_w_out, g_norm_ffn2=v_g_norm_ffn2,
               w_ffn2_in=v_w_ffn2_in, w_ffn2_out=v_w_ffn2_out, g_final=v_g_final)
    ix, iy, ic = _place()
    chip = 2 * ix + iy
    me = 4 * ix + 2 * iy + ic
    nada = w_ada.shape[2]

    c_all = _gather_small(c)[:, 0, :]
    b_shard = lax.dynamic_slice_in_dim(b_ada, chip * nada, nada, axis=1)
    mod_cols = _ada_fwd(c_all, w_ada[0], b_shard)
    mod_all = _gather_small(mod_cols)
    mod = jnp.concatenate([lax.dynamic_index_in_dim(mod_all[4 * (kk >> 1) + 2 * (kk & 1)], me, axis=0, keepdims=False)
                           for kk in range(4)])

    ws = _Sharded({n: w[n][0].astype(BF16) for n in BIG})
    loss, dx, dmod, dgains, g_w_pool, g_pool_scale, grads = _example_step(
        x[0], loss_target[0], positions[0], mod, (g_norm_ffn1[0], g_norm_mix[0], g_norm_ffn2[0], g_final),
        w_pool[0], pool_scale[0], ws)

    small_g = dict(b_ada=dmod, g_norm_ffn1=dgains[0], g_norm_mix=dgains[1], g_norm_ffn2=dgains[2], g_final=dgains[3],
                   pool_scale=g_pool_scale, w_pool=g_w_pool)
    tail = jnp.zeros((1, 128), F32)
    gathered = _gather_small(jnp.concatenate([_pack_small(small_g), jnp.pad(loss.reshape(1, 1), ((0, 0), (0, 127)))], axis=1))
    sg, sd, sm, sv = _adam_small(*[jnp.concatenate([_pack_small(t), tail], axis=1) for t in (w, mom, var)], gathered)
    small_out = [_unpack_small(t, w) for t in (sg, sd, sm, sv)]
    loss = sg[0, sg.shape[1] - 128]

    dmod_all = gathered[:, 0, :NMOD * D]
    dmod_cols = lax.dynamic_slice_in_dim(dmod_all, chip * nada, nada, axis=1)
    g_ada = _ada_bwd(c_all, dmod_cols)
    ada_out = _adam(w_ada[0], m_w_ada[0], v_w_ada[0], [g_ada], "adam_w_ada")

    partial = [_sum4(ws.recv[n], "sum_" + n) for n in BIG]
    other = _swap_sibling(partial)
    big_out = {n: _adam(w[n][0], mom[n][0], var[n][0], [pa, pb], "adam_" + n) for n, pa, pb in zip(BIG, partial, other)}

    def leaf(kind, n):
        if n == "w_ada":
            return ada_out[kind][None]
        if n in big_out:
            return big_out[n][kind][None]
        return small_out[kind][n]

    return (loss, dx[None], *[leaf(kind, n) for kind in range(4) for n in WEIGHTS])
```

```python
import jax
import jax.numpy as jnp
from jax import lax
from jax.experimental import pallas as pl
from jax.experimental.pallas import tpu as pltpu

F32 = jnp.float32
BF16 = jnp.bfloat16

D = 1024
FF = 2816
FC = 1408
PW = 256
GA = 256
HD = 64
LANES = 128
NH = GA // HD
NG = 3
DIL = (1, 4, 16)
BLK = 128
GW = 2 * D
INW = PW + 3 * NG * GA + GW
NMOD = 9
POOL_WINDOWS = (2, 4, 8, 16)
HALO = 16
EPS = 1e-6
SCALE = HD ** -0.5
NEG = -1e30

LR, B1, B2, AEPS, WD, STEP = 0.001, 0.9, 0.999, 1e-08, 0.01, 10

VMEM_BIG = 56 * 1024 * 1024
TM = 256

MESH = pl.DeviceIdType.MESH
ANY = pl.BlockSpec(memory_space=pl.ANY)


def _call(body, name, grid, in_specs, out_specs, out_shape, scratch=(), vmem=None, comm=None):
    params = pltpu.CompilerParams(dimension_semantics=("arbitrary",) * len(grid), vmem_limit_bytes=vmem)
    n_in, n_out, n_scr = len(in_specs), len(out_shape), len(scratch)
    if comm is None:
        call = pl.pallas_call(body, name=name, grid=grid, in_specs=list(in_specs), out_specs=list(out_specs),
                              out_shape=list(out_shape), scratch_shapes=list(scratch), compiler_params=params)
        return lambda *args: (call(*args), ())
    nc = len(comm.inputs)

    def body_with_comm(*refs):
        ins, refs = refs[:n_in], refs[n_in:]
        c_ins, refs = refs[:nc], refs[nc:]
        outs, refs = refs[:n_out], refs[n_out:]
        c_outs, refs = refs[:nc], refs[nc:]
        scr, sems = refs[:n_scr], refs[n_scr:]
        first = pl.program_id(0) == 0
        last = pl.program_id(0) == grid[0] - 1
        for ax in range(1, len(grid)):
            first = jnp.logical_and(first, pl.program_id(ax) == 0)
            last = jnp.logical_and(last, pl.program_id(ax) == grid[ax] - 1)

        @pl.when(first)
        def _():
            comm.start(c_ins, c_outs, sems)

        body(*ins, *outs, *scr)

        @pl.when(last)
        def _():
            comm.wait(c_ins, c_outs, sems)

    call = pl.pallas_call(
        body_with_comm, name=name, grid=grid, in_specs=list(in_specs) + [ANY] * nc,
        out_specs=list(out_specs) + [ANY] * nc, out_shape=list(out_shape) + list(comm.out_shapes),
        scratch_shapes=list(scratch) + list(comm.sem_shapes), compiler_params=params)

    def run(*args):
        res = call(*args, *comm.inputs)
        return res[:n_out], res[n_out:]

    return run


def _rows(tm, n):
    return pl.BlockSpec((tm, n), lambda i: (i, 0))


def _const(shape):
    return pl.BlockSpec(shape, lambda i: (0,) * len(shape))


def _sds(shape, dtype):
    return jax.ShapeDtypeStruct(shape, dtype)


def _dot(a, b):
    return jnp.dot(a, b, preferred_element_type=F32)


def _dot_nt(a, b):
    return lax.dot_general(a, b, (((1,), (1,)), ((), ())), preferred_element_type=F32)


def _dot_tn(a, b):
    return lax.dot_general(a, b, (((0,), (0,)), ((), ())), preferred_element_type=F32)


def _colsum(v):
    return jnp.sum(v, axis=0, keepdims=True)


def _norm_fwd(h, g, sh, sc):
    r = lax.rsqrt(jnp.mean(h * h, axis=-1, keepdims=True) + EPS)
    xh = h * r
    n = xh * g
    return xh, r, n, n * (1.0 + sc) + sh


def _norm_bwd(du, xh, r, n, g, sc):
    dn = du * (1.0 + sc)
    dxh = dn * g
    dh = r * (dxh - xh * jnp.mean(dxh * xh, axis=-1, keepdims=True))
    return dh, _colsum(du), _colsum(du * n), _colsum(dn * xh)


def _load_once(pairs, sems):
    @pl.when(pl.program_id(0) == 0)
    def _():
        cps = [pltpu.make_async_copy(src, dst, sems.at[j]) for j, (src, dst) in enumerate(pairs)]
        for cp in cps:
            cp.start()
        for cp in cps:
            cp.wait()


def _zero_first(ref):
    @pl.when(pl.program_id(0) == 0)
    def _():
        ref[...] = jnp.zeros(ref.shape, ref.dtype)


def _row_chunks(hbm_refs, vmem_ref):
    pairs, row = [], 0
    for ref in hbm_refs:
        pairs.append((ref, vmem_ref.at[pl.ds(row, ref.shape[0]), :]))
        row += ref.shape[0]
    return pairs


def _ffn_fwd(h, vec, wins, wout, name, comm=None):
    T = h.shape[0]
    nwin = len(wins)

    def body(h_ref, vec_ref, *rest):
        win_hbms, (wout_hbm, ho_ref, u_ref, a_ref, b_ref, f_ref, win_v, wout_v, sems) = rest[:nwin], rest[nwin:]
        _load_once(_row_chunks(win_hbms, win_v) + [(wout_hbm, wout_v)], sems)
        hh = h_ref[...]
        g, sh, sc, gt = vec_ref[0:1, :], vec_ref[1:2, :], vec_ref[2:3, :], vec_ref[3:4, :]
        _, _, _, u = _norm_fwd(hh, g, sh, sc)
        ub = u.astype(BF16)
        u_ref[...] = ub
        acc = None
        for j in range(FF // FC):
            lo, hi = j * FC, (j + 1) * FC
            a = _dot(ub, win_v[:, lo:hi])
            b = _dot(ub, win_v[:, FF + lo:FF + hi])
            a_ref[:, lo:hi] = a.astype(BF16)
            b_ref[:, lo:hi] = b.astype(BF16)
            s = (a * jax.nn.sigmoid(a) * b).astype(BF16)
            part = _dot(s, wout_v[lo:hi, :])
            acc = part if acc is None else acc + part
        f_ref[...] = acc.astype(BF16)
        ho_ref[...] = hh + 0.5 * gt * acc

    return _call(
        body, name, (T // TM,),
        [_rows(TM, D), _const((8, D))] + [ANY] * (nwin + 1),
        [_rows(TM, D), _rows(TM, D), _rows(TM, FF), _rows(TM, FF), _rows(TM, D)],
        [_sds((T, D), F32), _sds((T, D), BF16), _sds((T, FF), BF16), _sds((T, FF), BF16), _sds((T, D), BF16)],
        scratch=[pltpu.VMEM((D, 2 * FF), BF16), pltpu.VMEM((FF, D), BF16), pltpu.SemaphoreType.DMA((nwin + 1,))],
        vmem=VMEM_BIG, comm=comm,
    )(h, vec, *wins, wout)


def _ffn_bwd(dh, h, a, b, f, vec, wins, wout, name, comm=None):
    T = h.shape[0]
    nwin = len(wins)

    def body(dh_ref, h_ref, a_ref, b_ref, f_ref, vec_ref, *rest):
        win_hbms, (wout_hbm, dhi_ref, dab_ref, s_ref, df_ref, acc_ref, win_v, wout_v, sems) = rest[:nwin], rest[nwin:]
        _load_once(_row_chunks(win_hbms, win_v) + [(wout_hbm, wout_v)], sems)
        _zero_first(acc_ref)
        g, sh, sc, gt = vec_ref[0:1, :], vec_ref[1:2, :], vec_ref[2:3, :], vec_ref[3:4, :]
        dho = dh_ref[...]
        df = (0.5 * gt * dho).astype(BF16)
        df_ref[...] = df
        dgt = _colsum(0.5 * dho * f_ref[...].astype(F32))
        du = None
        for j in range(FF // FC):
            lo, hi = j * FC, (j + 1) * FC
            av = a_ref[:, lo:hi].astype(F32)
            bv = b_ref[:, lo:hi].astype(F32)
            ds = _dot_nt(df, wout_v[lo:hi, :])
            sig = jax.nn.sigmoid(av)
            sa = av * sig
            s_ref[:, lo:hi] = (sa * bv).astype(BF16)
            da = (ds * bv * (sig * (1.0 + av * (1.0 - sig)))).astype(BF16)
            db = (ds * sa).astype(BF16)
            dab_ref[:, lo:hi] = da
            dab_ref[:, FF + lo:FF + hi] = db
            part = _dot_nt(da, win_v[:, lo:hi]) + _dot_nt(db, win_v[:, FF + lo:FF + hi])
            du = part if du is None else du + part
        xh, r, n, _ = _norm_fwd(h_ref[...], g, sh, sc)
        dhn, dsh, dsc, dg = _norm_bwd(du, xh, r, n, g, sc)
        dhi_ref[...] = dho + dhn
        acc_ref[0:1, :] += dsh
        acc_ref[1:2, :] += dsc
        acc_ref[2:3, :] += dg
        acc_ref[3:4, :] += dgt

    return _call(
        body, name, (T // TM,),
        [_rows(TM, D), _rows(TM, D), _rows(TM, FF), _rows(TM, FF), _rows(TM, D), _const((8, D))] + [ANY] * (nwin + 1),
        [_rows(TM, D), _rows(TM, 2 * FF), _rows(TM, FF), _rows(TM, D), _const((8, D))],
        [_sds((T, D), F32), _sds((T, 2 * FF), BF16), _sds((T, FF), BF16), _sds((T, D), BF16), _sds((8, D), F32)],
        scratch=[pltpu.VMEM((D, 2 * FF), BF16), pltpu.VMEM((FF, D), BF16), pltpu.SemaphoreType.DMA((nwin + 1,))],
        vmem=VMEM_BIG, comm=comm,
    )(dh, h, a, b, f, vec, *wins, wout)


def _wgrad(x, y, name, tk, tn, tt, out_dtype=BF16, comm=None):
    T, K = x.shape
    N = y.shape[1]
    nt = T // tt

    def body(x_ref, y_ref, o_ref, acc_ref):
        t = pl.program_id(2)
        part = _dot_tn(x_ref[...], y_ref[...])

        @pl.when(t == 0)
        def _():
            acc_ref[...] = part

        @pl.when(t > 0)
        def _():
            acc_ref[...] += part

        @pl.when(t == nt - 1)
        def _():
            o_ref[...] = acc_ref[...].astype(out_dtype)

    (out,), c_outs = _call(
        body, name, (K // tk, N // tn, nt),
        [pl.BlockSpec((tt, tk), lambda i, j, t: (t, i)), pl.BlockSpec((tt, tn), lambda i, j, t: (t, j))],
        [pl.BlockSpec((tk, tn), lambda i, j, t: (i, j))], [_sds((K, N), out_dtype)],
        scratch=[pltpu.VMEM((tk, tn), F32)], vmem=VMEM_BIG, comm=comm,
    )(x, y)
    return out, c_outs


def _wgrad_scatter(x, y, name, tt, comm=None):
    T, K = x.shape
    n = y.shape[1] // 4
    nt = T // tt
    nc = 0 if comm is None else len(comm.inputs)

    def body(chip_ref, x_ref, y_ref, *refs):
        c_ins, refs = refs[:nc], refs[nc:]
        recv_ref, refs = refs[0], refs[1:]
        c_outs, refs = refs[:nc], refs[nc:]
        acc_ref, stage_ref, local_sem, send_sems, recv_sems = refs[:5]
        j, t = pl.program_id(0), pl.program_id(1)
        px, py, pc = _place()

        def sends():
            return [pltpu.make_async_remote_copy(
                src_ref=stage_ref.at[m], dst_ref=recv_ref.at[m], send_sem=send_sems.at[m - 1], recv_sem=recv_sems.at[m - 1],
                device_id=_chip_peer(px, py, pc, m), device_id_type=MESH) for m in range(1, 4)]

        own = pltpu.make_async_copy(stage_ref.at[0], recv_ref.at[0], local_sem.at[0])
        if comm is not None:
            @pl.when(jnp.logical_and(j == 0, t == 0))
            def _():
                comm.start(c_ins, c_outs, refs[5:])

        part = _dot_tn(x_ref[...], y_ref[...])

        @pl.when(t == 0)
        def _():
            acc_ref[...] = part

        @pl.when(t > 0)
        def _():
            acc_ref[...] += part

        for jj in range(4):
            m = (jj + 1) & 3

            @pl.when(jnp.logical_and(j == jj, t == nt - 1))
            def _():
                stage_ref[m] = acc_ref[...].astype(BF16)
                if m:
                    sends()[m - 1].start()
                else:
                    own.start()
                    for cp in sends():
                        cp.wait_recv()
                    for cp in sends():
                        cp.wait_send()
                    own.wait()
                    if comm is not None:
                        comm.wait(c_ins, c_outs, refs[5:])

    grid_spec = pltpu.PrefetchScalarGridSpec(
        num_scalar_prefetch=1, grid=(4, nt),
        in_specs=[pl.BlockSpec((tt, K), lambda j, t, chip: (t, 0)),
                  pl.BlockSpec((tt, n), lambda j, t, chip: (t, chip[0] ^ ((j + 1) & 3)))] + [ANY] * nc,
        out_specs=[ANY] * (1 + nc),
        scratch_shapes=[pltpu.VMEM((K, n), F32), pltpu.VMEM((4, K, n), BF16), pltpu.SemaphoreType.DMA((1,)),
                        pltpu.SemaphoreType.DMA((3,)), pltpu.SemaphoreType.DMA((3,))]
        + ([] if comm is None else list(comm.sem_shapes)))
    px, py, _ = _place()
    res = pl.pallas_call(
        body, name=name, grid_spec=grid_spec,
        out_shape=[_sds((4, K, n), BF16)] + ([] if comm is None else list(comm.out_shapes)),
        compiler_params=pltpu.CompilerParams(dimension_semantics=("arbitrary", "arbitrary"), vmem_limit_bytes=VMEM_BIG),
    )((2 * px + py).astype(jnp.int32).reshape(1), x, y, *([] if comm is None else comm.inputs))
    return res[0], res[1:]


def _final_loss(h, tgt, gvec):
    T = h.shape[0]

    def body(h_ref, t_ref, g_ref, dh_ref, acc_ref):
        _zero_first(acc_ref)
        hh = h_ref[...]
        g = g_ref[0:1, :]
        r = lax.rsqrt(jnp.mean(hh * hh, axis=-1, keepdims=True) + EPS)
        xh = hh * r
        err = xh * g - t_ref[...]
        dy = err * (1.0 / D)
        dxh = dy * g
        dh_ref[...] = r * (dxh - xh * jnp.mean(dxh * xh, axis=-1, keepdims=True))
        acc_ref[0:1, :] += _colsum(err * err)
        acc_ref[1:2, :] += _colsum(dy * xh)

    return _call(
        body, "final_loss", (T // TM,),
        [_rows(TM, D), _rows(TM, D), _const((8, D))],
        [_rows(TM, D), _const((8, D))],
        [_sds((T, D), F32), _sds((8, D), F32)],
    )(h, tgt, gvec)[0]


def _swap_halves(t):
    w = t.shape[1]
    lane = lax.broadcasted_iota(jnp.int32, t.shape, 1)
    return jnp.where(lane % HD < HD // 2, pltpu.roll(t, w - HD // 2, 1), pltpu.roll(t, HD // 2, 1))


def _rope(t, cos, sin_signed):
    c = jnp.tile(cos, (1, t.shape[1] // cos.shape[1]))
    s = jnp.tile(sin_signed, (1, t.shape[1] // sin_signed.shape[1]))
    return t * c + _swap_halves(t) * s


def _rope_bwd(dt, cos, sin_signed):
    c = jnp.tile(cos, (1, dt.shape[1] // cos.shape[1]))
    s = jnp.tile(sin_signed, (1, dt.shape[1] // sin_signed.shape[1]))
    return dt * c + _swap_halves(dt * s)


def _rm_spec(dil):
    return pl.BlockSpec((dil, TM // dil, GA), lambda i: (0, i, 0))


def _to_residues(t, dst_ref, scr_ref, dil):
    if dil == 1:
        dst_ref[0] = t.astype(dst_ref.dtype)
        return
    for j in range(GA // LANES):
        scr_ref[j] = t[:, j * LANES:(j + 1) * LANES]
    for r in range(dil):
        for j in range(GA // LANES):
            rows = scr_ref.at[j][pl.ds(r, TM // dil, stride=dil), :]
            dst_ref[r, :, j * LANES:(j + 1) * LANES] = rows.astype(dst_ref.dtype)


def _from_residues(src_ref, scr_ref, dil):
    if dil == 1:
        return src_ref[0].astype(F32)
    for r in range(dil):
        for j in range(GA // LANES):
            scr_ref.at[j][pl.ds(r, TM // dil, stride=dil), :] = src_ref[r, :, j * LANES:(j + 1) * LANES].astype(F32)
    return jnp.concatenate([scr_ref[j] for j in range(GA // LANES)], axis=1)


def _mix_proj(h, vec, win, cos, sin, comm=None):
    T = h.shape[0]

    def body(h_ref, vec_ref, win_hbm, cos_ref, sin_ref, u_ref, p_ref, *rest):
        qkv_refs, gates_ref, win_v, scr_ref, sems = rest[:3 * NG], rest[3 * NG], rest[3 * NG + 1], rest[3 * NG + 2], rest[3 * NG + 3]
        _load_once([(win_hbm, win_v)], sems)
        g, sh, sc = vec_ref[0:1, :], vec_ref[1:2, :], vec_ref[2:3, :]
        _, _, _, u = _norm_fwd(h_ref[...], g, sh, sc)
        ub = u.astype(BF16)
        u_ref[...] = ub
        p_ref[...] = _dot(ub, win_v[:, 0:PW])
        cos_t, sin_t = cos_ref[...], sin_ref[...]
        for j in range(3 * NG):
            col = PW + j * GA
            t = _dot(ub, win_v[:, col:col + GA])
            if j < 2 * NG:
                t = _rope(t, cos_t, sin_t)
            _to_residues(t, qkv_refs[j], scr_ref, DIL[j % NG])
        for j in range(GW // 512):
            col = PW + 3 * NG * GA + j * 512
            gates_ref[:, j * 512:(j + 1) * 512] = jax.nn.sigmoid(_dot(ub, win_v[:, col:col + 512])).astype(BF16)

    outs, c_outs = _call(
        body, "mix_proj", (T // TM,),
        [_rows(TM, D), _const((8, D)), ANY, _rows(TM, 128), _rows(TM, 128)],
        [_rows(TM, D), _rows(TM, PW)] + [_rm_spec(d) for d in DIL] * 3 + [_rows(TM, GW)],
        [_sds((T, D), BF16), _sds((T, PW), F32)] + [_sds((d, T // d, GA), BF16) for d in DIL] * 3 + [_sds((T, GW), BF16)],
        scratch=[pltpu.VMEM((D, INW), BF16), pltpu.VMEM((GA // LANES, TM, LANES), F32), pltpu.SemaphoreType.DMA((1,))],
        vmem=VMEM_BIG, comm=comm,
    )(h, vec, win, cos, sin)
    return (outs[0], outs[1], outs[2:2 + NG], outs[2 + NG:2 + 2 * NG], outs[2 + 2 * NG:2 + 3 * NG], outs[2 + 3 * NG]), c_outs


def _head_masks():
    lane_head = lax.broadcasted_iota(jnp.int32, (BLK, GA), 1) // HD
    return [lane_head == hd for hd in range(NH)]


def _expand_heads(t, hm):
    return jnp.concatenate([jnp.where(m, t, jnp.zeros_like(t)) for m in hm], axis=0)


def _collapse_heads(tb, hm):
    out = None
    for hd, m in enumerate(hm):
        part = jnp.where(m, tb[hd * BLK:(hd + 1) * BLK, :], 0.0)
        out = part if out is None else out + part
    return out


def _head_rows(t):
    return jnp.concatenate([t[:, hd * HD:hd * HD + 1] for hd in range(NH)], axis=0)


def _band_masks():
    a = lax.broadcasted_iota(jnp.int32, (NH * BLK, BLK), 0) & (BLK - 1)
    c = lax.broadcasted_iota(jnp.int32, (NH * BLK, BLK), 1)
    return c <= a, c >= a


def _attn_specs(nbt):
    cur = pl.BlockSpec((BLK, GA), lambda i: (i, 0))
    prev = pl.BlockSpec((BLK, GA), lambda i: (jnp.maximum(i - 1, 0), 0))
    nxt = pl.BlockSpec((BLK, GA), lambda i: (jnp.minimum(i + 1, nbt - 1), 0))
    return cur, prev, nxt


def _attn_fwd(q, k, v, nb, name, comm=None):
    T = q.shape[0]
    nbt = T // BLK

    def body(q_ref, kc_ref, kp_ref, vc_ref, vp_ref, o_ref, lse_ref):
        i = pl.program_id(0)
        has_prev = (i & (nb - 1)) != 0
        hm = _head_masks()
        m_cur, m_prev = _band_masks()
        m_prev = jnp.logical_and(m_prev, has_prev)
        qb = _expand_heads(q_ref[...], hm)
        s_c = jnp.where(m_cur, _dot_nt(qb, kc_ref[...]) * SCALE, NEG)
        s_p = jnp.where(m_prev, _dot_nt(qb, kp_ref[...]) * SCALE, NEG)
        mx = jnp.maximum(jnp.max(s_c, axis=-1, keepdims=True), jnp.max(s_p, axis=-1, keepdims=True))
        e_c = jnp.exp(s_c - mx)
        e_p = jnp.exp(s_p - mx)
        l = jnp.sum(e_c, axis=-1, keepdims=True) + jnp.sum(e_p, axis=-1, keepdims=True)
        inv = 1.0 / l
        ob = _dot((e_c * inv).astype(BF16), vc_ref[...]) + _dot((e_p * inv).astype(BF16), vp_ref[...])
        o_ref[...] = _collapse_heads(ob, hm)
        lse_ref[...] = _collapse_heads(jnp.broadcast_to(mx + jnp.log(l), (NH * BLK, GA)), hm)

    cur, prev, _ = _attn_specs(nbt)
    return _call(body, name, (nbt,), [cur, cur, prev, cur, prev], [cur, cur],
                 [_sds((T, GA), F32), _sds((T, GA), F32)], comm=comm)(q, k, k, v, v)


def _attn_bwd(q, k, v, do, lse, e, nb, name, comm=None):
    T = q.shape[0]
    nbt = T // BLK

    def body(q_ref, kc_ref, vc_ref, do_ref, lse_ref, e_ref, kp_ref, vp_ref, qn_ref, don_ref, lsen_ref, en_ref,
             dq_ref, dk_ref, dv_ref):
        i = pl.program_id(0)
        has_prev = (i & (nb - 1)) != 0
        has_next = ((i + 1) & (nb - 1)) != 0
        hm = _head_masks()
        m_cur, m_band = _band_masks()
        m_prev = jnp.logical_and(m_band, has_prev)
        m_next = jnp.logical_and(m_band, has_next)
        kc, kp, vc, vp = kc_ref[...], kp_ref[...], vc_ref[...], vp_ref[...]
        qb, dob = _expand_heads(q_ref[...], hm), _expand_heads(do_ref[...], hm)
        lse_r, e_r = _head_rows(lse_ref[...]), _head_rows(e_ref[...])
        p_c = jnp.where(m_cur, jnp.exp(_dot_nt(qb, kc) * SCALE - lse_r), 0.0)
        p_p = jnp.where(m_prev, jnp.exp(_dot_nt(qb, kp) * SCALE - lse_r), 0.0)
        ds_c = (p_c * (_dot_nt(dob, vc) + e_r)).astype(BF16)
        ds_p = (p_p * (_dot_nt(dob, vp) + e_r)).astype(BF16)
        dq_ref[...] = _collapse_heads((_dot(ds_c, kc) + _dot(ds_p, kp)) * SCALE, hm)
        qnb, donb = _expand_heads(qn_ref[...], hm), _expand_heads(don_ref[...], hm)
        p_n = jnp.where(m_next, jnp.exp(_dot_nt(qnb, kc) * SCALE - _head_rows(lsen_ref[...])), 0.0)
        ds_n = (p_n * (_dot_nt(donb, vc) + _head_rows(en_ref[...]))).astype(BF16)
        dk_ref[...] = (_dot_tn(ds_c, qb) + _dot_tn(ds_n, qnb)) * SCALE
        dv_ref[...] = (_dot_tn(p_c.astype(BF16), dob) + _dot_tn(p_n.astype(BF16), donb)).astype(BF16)

    cur, prev, nxt = _attn_specs(nbt)
    return _call(body, name, (nbt,), [cur] * 6 + [prev, prev] + [nxt] * 4, [cur, cur, cur],
                 [_sds((T, GA), F32), _sds((T, GA), F32), _sds((T, GA), BF16)],
                 comm=comm)(q, k, v, do, lse, e, k, v, q, do, lse, e)


def _flat(t):
    return t.reshape(t.shape[0] * t.shape[1], t.shape[2])


def _by_residue(t, dil):
    return t.reshape(dil, t.shape[0] // dil, t.shape[1])


def _pool_consts(shape, row0):
    lane = lax.broadcasted_iota(jnp.int32, shape, 1)
    t = lax.broadcasted_iota(jnp.int32, shape, 0) + row0
    grp = lane // (PW // len(POOL_WINDOWS))
    win = jnp.where(grp == 0, POOL_WINDOWS[0], jnp.where(grp == 1, POOL_WINDOWS[1],
                    jnp.where(grp == 2, POOL_WINDOWS[2], POOL_WINDOWS[3])))
    cnt = jnp.minimum(t + 1, win).astype(F32)
    return grp, cnt


def _window_sums(ext_ref, base, step, tm):
    outs, run = [], None
    for j in range(POOL_WINDOWS[-1]):
        sl = ext_ref[pl.ds(base + step * j, tm), :]
        run = sl if run is None else run + sl
        if j + 1 in POOL_WINDOWS:
            outs.append(run)
    return outs


def _select_group(grp, vals):
    return jnp.where(grp == 0, vals[0], jnp.where(grp == 1, vals[1], jnp.where(grp == 2, vals[2], vals[3])))


def _pool_d(pc_ref, pp_ref, ext_ref, i, tm):
    ext_ref[0:HALO, :] = jnp.where(i > 0, pp_ref[tm - HALO:tm, :], 0.0)
    ext_ref[HALO:HALO + tm, :] = pc_ref[...]
    grp, cnt = _pool_consts((tm, PW), i * tm)
    sums = _window_sums(ext_ref, HALO, -1, tm)
    return _select_group(grp, sums) / cnt - pc_ref[...]


def _group_weights(ls):
    mx = jnp.maximum(jnp.maximum(ls[0], ls[1]), ls[2])
    es = [jnp.exp(l - mx) for l in ls]
    inv = 1.0 / (es[0] + es[1] + es[2])
    return [e * inv for e in es]


def _mix_merge(h, vec, p, os, lses, gates, wp_bd, pscale, wpb, wab, wout):
    T = h.shape[0]

    def body(h_ref, vec_ref, pc_ref, pp_ref, o0, o1, o2, l0, l1, l2, gates_ref, wp_ref, ps_ref, wpb_ref, wab_ref, wout_ref,
             ho_ref, yp_ref, ya_ref, mg_ref, mo_ref, d_ref, ext_ref, scr_ref):
        i = pl.program_id(0)
        gt = vec_ref[3:4, :]
        d = _pool_d(pc_ref, pp_ref, ext_ref, i, TM).astype(BF16)
        d_ref[...] = d
        ypool = (_dot(d, wp_ref[...]) * ps_ref[0:1, :]).astype(BF16)
        yp_ref[...] = ypool
        w = _group_weights([_from_residues(r, scr_ref, dl) for r, dl in zip((l0, l1, l2), DIL)])
        yattn = None
        for wg, o_ref, dl in zip(w, (o0, o1, o2), DIL):
            part = wg * _from_residues(o_ref, scr_ref, dl)
            yattn = part if yattn is None else yattn + part
        yattn = yattn.astype(BF16)
        ya_ref[...] = yattn
        merged = (gates_ref[:, 0:D].astype(F32) * _dot(ypool, wpb_ref[...])
                  + gates_ref[:, D:GW].astype(F32) * _dot(yattn, wab_ref[...])).astype(BF16)
        mg_ref[...] = merged
        mo = _dot(merged, wout_ref[...])
        mo_ref[...] = mo.astype(BF16)
        ho_ref[...] = h_ref[...] + gt * mo

    prev = pl.BlockSpec((TM, PW), lambda i: (jnp.maximum(i - 1, 0), 0))
    return _call(
        body, "mix_merge", (T // TM,),
        [_rows(TM, D), _const((8, D)), _rows(TM, PW), prev] + [_rm_spec(dl) for dl in DIL] * 2 + [_rows(TM, GW), _const((PW, PW)),
         _const((8, PW)), _const((PW, D)), _const((GA, D)), _const((D, D))],
        [_rows(TM, D), _rows(TM, PW), _rows(TM, GA), _rows(TM, D), _rows(TM, D), _rows(TM, PW)],
        [_sds((T, D), F32), _sds((T, PW), BF16), _sds((T, GA), BF16), _sds((T, D), BF16), _sds((T, D), BF16), _sds((T, PW), BF16)],
        scratch=[pltpu.VMEM((TM + HALO, PW), F32), pltpu.VMEM((GA // LANES, TM, LANES), F32)],
        vmem=VMEM_BIG,
    )(h, vec, p, p, *os, *lses, gates, wp_bd, pscale, wpb, wab, wout)[0]


def _mix_bwd_a(dh, vec, mixout, gates, ypool, yattn, dpool, os, lses, wp_bd, pscale, wpb, wab, wout, ones_bd, comm=None):
    T = dh.shape[0]

    def body(dh_ref, vec_ref, mo_ref, gates_ref, yp_ref, ya_ref, d_ref, o0, o1, o2, l0, l1, l2,
             wp_ref, ps_ref, wpb_ref, wab_ref, wout_ref, ones_ref,
             dmo_ref, dp_ref, da_ref, dgates_ref, do0, do1, do2, e0, e1, e2, dd_ref, dyp_ref, acc_ref, acc2_ref, scr_ref):
        _zero_first(acc_ref)
        _zero_first(acc2_ref)
        gt = vec_ref[3:4, :]
        dho = dh_ref[...]
        acc_ref[3:4, :] += _colsum(dho * mo_ref[...].astype(F32))
        dmo = (gt * dho).astype(BF16)
        dmo_ref[...] = dmo
        dmerged = _dot_nt(dmo, wout_ref[...])
        gp = gates_ref[:, 0:D].astype(F32)
        ga = gates_ref[:, D:GW].astype(F32)
        bp = _dot(yp_ref[...], wpb_ref[...])
        ba = _dot(ya_ref[...], wab_ref[...])
        dgates_ref[:, 0:D] = (dmerged * bp * gp * (1.0 - gp)).astype(BF16)
        dgates_ref[:, D:GW] = (dmerged * ba * ga * (1.0 - ga)).astype(BF16)
        dbp = (dmerged * gp).astype(BF16)
        dba = (dmerged * ga).astype(BF16)
        dp_ref[...] = dbp
        da_ref[...] = dba
        dypool = _dot_nt(dbp, wpb_ref[...])
        ypre = _dot(d_ref[...], wp_ref[...])
        acc2_ref[0:1, :] += _colsum(dypool * ypre)
        dyp = (dypool * ps_ref[0:1, :]).astype(BF16)
        dyp_ref[...] = dyp
        dd_ref[...] = _dot_nt(dyp, wp_ref[...])
        dya = _dot_nt(dba, wab_ref[...])
        w = _group_weights([_from_residues(r, scr_ref, dl) for r, dl in zip((l0, l1, l2), DIL)])
        ya = None
        for wg, o_ref, dl in zip(w, (o0, o1, o2), DIL):
            part = wg * _from_residues(o_ref, scr_ref, dl)
            ya = part if ya is None else ya + part
        prod = dya * ya
        hi = prod.astype(BF16)
        lo = (prod - hi.astype(F32)).astype(BF16)
        tot = _dot(hi, ones_ref[...]) + _dot(lo, ones_ref[...])
        for wg, do_ref, e_ref, dl in zip(w, (do0, do1, do2), (e0, e1, e2), DIL):
            _to_residues(wg * dya, do_ref, scr_ref, dl)
            _to_residues(-wg * tot, e_ref, scr_ref, dl)

    return _call(
        body, "mix_bwd_a", (T // TM,),
        [_rows(TM, D), _const((8, D)), _rows(TM, D), _rows(TM, GW), _rows(TM, PW), _rows(TM, GA), _rows(TM, PW)]
        + [_rm_spec(dl) for dl in DIL] * 2
        + [_const((PW, PW)), _const((8, PW)), _const((PW, D)), _const((GA, D)), _const((D, D)), _const((GA, GA))],
        [_rows(TM, D)] * 3 + [_rows(TM, GW)] + [_rm_spec(dl) for dl in DIL] * 2
        + [_rows(TM, PW), _rows(TM, PW), _const((8, D)), _const((8, PW))],
        [_sds((T, D), BF16)] * 3 + [_sds((T, GW), BF16)] + [_sds((dl, T // dl, GA), BF16) for dl in DIL]
        + [_sds((dl, T // dl, GA), F32) for dl in DIL]
        + [_sds((T, PW), F32), _sds((T, PW), BF16), _sds((8, D), F32), _sds((8, PW), F32)],
        scratch=[pltpu.VMEM((GA // LANES, TM, LANES), F32)],
        vmem=VMEM_BIG, comm=comm,
    )(dh, vec, mixout, gates, ypool, yattn, dpool, *os, *lses, wp_bd, pscale, wpb, wab, wout, ones_bd)


def _mix_bwd_b(dh, h, vec, dd, dqs, dks, dvs, dgates, cos, sin, win):
    T = h.shape[0]
    nt = T // TM

    def body(dh_ref, h_ref, vec_ref, ddc_ref, ddn_ref, *rest):
        qk_refs, dv_refs = rest[:2 * NG], rest[2 * NG:3 * NG]
        dgates_ref, cos_ref, sin_ref, win_hbm, dhi_ref, dproj_ref, acc_ref, win_v, ext_ref, scr_ref, sems = rest[3 * NG:]
        i = pl.program_id(0)
        _load_once([(win_hbm, win_v)], sems)
        _zero_first(acc_ref)
        g, sh, sc = vec_ref[0:1, :], vec_ref[1:2, :], vec_ref[2:3, :]
        grp, cnt = _pool_consts((TM, PW), i * TM)
        _, cnt_n = _pool_consts((HALO, PW), (i + 1) * TM)
        ext_ref[0:TM, :] = ddc_ref[...] / cnt
        ext_ref[TM:TM + HALO, :] = jnp.where(i < nt - 1, ddn_ref[0:HALO, :] / cnt_n, 0.0)
        dp = _select_group(grp, _window_sums(ext_ref, 0, 1, TM)) - ddc_ref[...]
        dproj_ref[:, 0:PW] = dp.astype(BF16)
        cos_t, sin_t = cos_ref[...], sin_ref[...]
        for j in range(2 * NG):
            col = PW + j * GA
            dt = _from_residues(qk_refs[j], scr_ref, DIL[j % NG])
            dproj_ref[:, col:col + GA] = _rope_bwd(dt, cos_t, sin_t).astype(BF16)
        for j in range(NG):
            col = PW + (2 * NG + j) * GA
            dproj_ref[:, col:col + GA] = _from_residues(dv_refs[j], scr_ref, DIL[j]).astype(BF16)
        dproj_ref[:, PW + 3 * NG * GA:INW] = dgates_ref[...]
        du = None
        for j in range(INW // 512):
            part = _dot_nt(dproj_ref[:, j * 512:(j + 1) * 512], win_v[:, j * 512:(j + 1) * 512])
            du = part if du is None else du + part
        xh, r, n, _ = _norm_fwd(h_ref[...], g, sh, sc)
        dhn, dsh, dsc, dg = _norm_bwd(du, xh, r, n, g, sc)
        dhi_ref[...] = dh_ref[...] + dhn
        acc_ref[0:1, :] += dsh
        acc_ref[1:2, :] += dsc
        acc_ref[2:3, :] += dg

    nxt = pl.BlockSpec((TM, PW), lambda i: (jnp.minimum(i + 1, nt - 1), 0))
    return _call(
        body, "mix_bwd_b", (nt,),
        [_rows(TM, D), _rows(TM, D), _const((8, D)), _rows(TM, PW), nxt] + [_rm_spec(dl) for dl in DIL] * 3
        + [_rows(TM, GW), _rows(TM, 128), _rows(TM, 128), ANY],
        [_rows(TM, D), _rows(TM, INW), _const((8, D))],
        [_sds((T, D), F32), _sds((T, INW), BF16), _sds((8, D), F32)],
        scratch=[pltpu.VMEM((D, INW), BF16), pltpu.VMEM((TM + HALO, PW), F32), pltpu.VMEM((GA // LANES, TM, LANES), F32),
                 pltpu.SemaphoreType.DMA((1,))],
        vmem=VMEM_BIG,
    )(dh, h, vec, dd, dd, *dqs, *dks, *dvs, dgates, cos, sin, win)[0]


def _ada_fwd(c_all, w_shard, b_shard):
    n = w_shard.shape[1]

    def body(c_ref, w_ref, b_ref, o_ref):
        cv = c_ref[...]
        cond = (cv * jax.nn.sigmoid(cv)).astype(BF16)
        o_ref[...] = _dot(cond, w_ref[...].astype(BF16)) + b_ref[...]

    tn = n // 3
    return pl.pallas_call(
        body, name="ada_fwd", grid=(3,),
        in_specs=[pl.BlockSpec((8, D), lambda j: (0, 0)), pl.BlockSpec((D, tn), lambda j: (0, j)), pl.BlockSpec((1, tn), lambda j: (0, j))],
        out_specs=pl.BlockSpec((8, tn), lambda j: (0, j)), out_shape=_sds((8, n), F32),
        compiler_params=pltpu.CompilerParams(dimension_semantics=("arbitrary",)),
    )(c_all, w_shard, b_shard)


def _ada_bwd(c_all, dmod_shard):
    n = dmod_shard.shape[1]

    def body(c_ref, d_ref, o_ref):
        cv = c_ref[...]
        cond = (cv * jax.nn.sigmoid(cv)).astype(BF16)
        o_ref[...] = _dot_tn(cond, d_ref[...].astype(BF16))

    tn = n // 3
    return pl.pallas_call(
        body, name="ada_bwd", grid=(3,),
        in_specs=[pl.BlockSpec((8, D), lambda j: (0, 0)), pl.BlockSpec((8, tn), lambda j: (0, j))],
        out_specs=pl.BlockSpec((D, tn), lambda j: (0, j)), out_shape=_sds((D, n), F32),
        compiler_params=pltpu.CompilerParams(dimension_semantics=("arbitrary",)),
    )(c_all, dmod_shard)


def _adam_math(w, g, m, v):
    m2 = B1 * m + (1.0 - B1) * g
    v2 = B2 * v + (1.0 - B2) * (g * g)
    m_hat = m2 / (1.0 - B1 ** STEP)
    v_hat = v2 / (1.0 - B2 ** STEP)
    delta = -LR * (m_hat / (jnp.sqrt(v_hat) + AEPS) + WD * w)
    return delta, m2, v2


def _adam(w, m, v, parts, name):
    R, C = w.shape
    tr = R
    for cand in (128, 64, 32, 16, 8):
        if R % cand == 0:
            tr = cand
            break
    np_ = len(parts)

    def body(w_ref, m_ref, v_ref, *rest):
        p_refs, (g_ref, d_ref, m2_ref, v2_ref) = rest[:np_], rest[np_:]
        g = p_refs[0][...]
        for pr in p_refs[1:]:
            g = g + pr[...]
        delta, m2, v2 = _adam_math(w_ref[...], g, m_ref[...], v_ref[...])
        g_ref[...] = g
        d_ref[...] = delta
        m2_ref[...] = m2
        v2_ref[...] = v2

    spec = pl.BlockSpec((tr, C), lambda i: (i, 0))
    return pl.pallas_call(
        body, name=name, grid=(R // tr,), in_specs=[spec] * (3 + np_), out_specs=[spec] * 4,
        out_shape=[_sds((R, C), F32)] * 4,
        compiler_params=pltpu.CompilerParams(dimension_semantics=("arbitrary",), vmem_limit_bytes=VMEM_BIG),
    )(w, m, v, *parts)


def _adam_small(w, m, v, gathered):
    P = w.shape[1]

    def body(w_ref, m_ref, v_ref, ga_ref, g_ref, d_ref, m2_ref, v2_ref):
        g = ga_ref[0]
        for dev in range(1, 8):
            g = g + ga_ref[dev]
        delta, m2, v2 = _adam_math(w_ref[...], g, m_ref[...], v_ref[...])
        g_ref[...] = g
        d_ref[...] = delta
        m2_ref[...] = m2
        v2_ref[...] = v2

    return pl.pallas_call(body, name="adam_small", out_shape=[_sds((1, P), F32)] * 4)(w, m, v, gathered)


def _sum4(blocks, name):
    _, R, C = blocks.shape
    tr = R
    for cand in (256, 128, 64, 32, 16):
        if R % cand == 0:
            tr = cand
            break

    def body(r_ref, out_ref):
        out_ref[...] = ((r_ref[0].astype(F32) + r_ref[1].astype(F32)) + r_ref[2].astype(F32)) + r_ref[3].astype(F32)

    return pl.pallas_call(
        body, name=name, grid=(R // tr,),
        in_specs=[pl.BlockSpec((4, tr, C), lambda i: (0, i, 0))],
        out_specs=pl.BlockSpec((tr, C), lambda i: (i, 0)), out_shape=_sds((R, C), F32),
        compiler_params=pltpu.CompilerParams(dimension_semantics=("arbitrary",)),
    )(blocks)


def _place():
    return lax.axis_index("x"), lax.axis_index("y"), lax.axis_index("c")


def _gather_small(v):
    R, P = v.shape

    def body(v_ref, out_ref, send_sems, recv_sems):
        x, y, c = _place()
        me = 4 * x + 2 * y + c
        out_ref[me] = v_ref[...]
        copies = []
        for m in range(1, 8):
            peer = (x ^ (m >> 2), y ^ ((m >> 1) & 1), c ^ (m & 1))
            copies.append(pltpu.make_async_remote_copy(
                src_ref=v_ref, dst_ref=out_ref.at[me], send_sem=send_sems.at[m - 1], recv_sem=recv_sems.at[m - 1],
                device_id=peer, device_id_type=MESH))
        for cp in copies:
            cp.start()
        for m in range(1, 8):
            src = 4 * (x ^ (m >> 2)) + 2 * (y ^ ((m >> 1) & 1)) + (c ^ (m & 1))
            pltpu.make_async_remote_copy(
                src_ref=v_ref, dst_ref=out_ref.at[src], send_sem=send_sems.at[m - 1], recv_sem=recv_sems.at[m - 1],
                device_id=(x, y, c), device_id_type=MESH).wait_recv()
        for cp in copies:
            cp.wait_send()

    vm = pl.BlockSpec(memory_space=pltpu.VMEM)
    return pl.pallas_call(
        body, name="gather_small", in_specs=[vm], out_specs=vm, out_shape=_sds((8, R, P), F32),
        scratch_shapes=[pltpu.SemaphoreType.DMA((7,)), pltpu.SemaphoreType.DMA((7,))],
    )(v)


def _chip_peer(x, y, c, m):
    return (x ^ (m >> 1), y ^ (m & 1), c)


def _shard_ref(ref, axis, k, n):
    start = pl.multiple_of(k * n, 128 if axis == 1 else 16)
    return ref.at[:, pl.ds(start, n)] if axis == 1 else ref.at[pl.ds(start, n), :]


class _GatherPlan:
    def __init__(self, shards, axes):
        self.inputs, self.axes, nw = list(shards), list(axes), len(shards)
        self.out_shapes = [_sds((s.shape[0] * (4 if ax == 0 else 1), s.shape[1] * (4 if ax == 1 else 1)), BF16)
                           for s, ax in zip(shards, axes)]
        self.sem_shapes = [pltpu.SemaphoreType.DMA((nw,)), pltpu.SemaphoreType.DMA((nw, 3)), pltpu.SemaphoreType.DMA((nw, 3))]

    def _copies(self, ins, outs, sems):
        local_sems, send_sems, recv_sems = sems
        x, y, c = _place()
        k = 2 * x + y
        local, remote, arrivals = [], [], []
        for j, ax in enumerate(self.axes):
            n = ins[j].shape[ax]
            mine = _shard_ref(outs[j], ax, k, n)
            local.append(pltpu.make_async_copy(ins[j], mine, local_sems.at[j]))
            for m in range(1, 4):
                remote.append(pltpu.make_async_remote_copy(
                    src_ref=ins[j], dst_ref=mine, send_sem=send_sems.at[j, m - 1], recv_sem=recv_sems.at[j, m - 1],
                    device_id=_chip_peer(x, y, c, m), device_id_type=MESH))
                arrivals.append(pltpu.make_async_remote_copy(
                    src_ref=ins[j], dst_ref=_shard_ref(outs[j], ax, k ^ m, n), send_sem=send_sems.at[j, m - 1],
                    recv_sem=recv_sems.at[j, m - 1], device_id=(x, y, c), device_id_type=MESH))
        return local, remote, arrivals

    def start(self, ins, outs, sems):
        local, remote, _ = self._copies(ins, outs, sems)
        for cp in local + remote:
            cp.start()

    def wait(self, ins, outs, sems):
        local, remote, arrivals = self._copies(ins, outs, sems)
        for cp in arrivals:
            cp.wait_recv()
        for cp in remote:
            cp.wait_send()
        for cp in local:
            cp.wait()


class _ScatterPlan:
    def __init__(self, grads, axes):
        self.inputs, self.axes, nw = list(grads), list(axes), len(grads)
        self.shard_shapes = [(g.shape[0] // (4 if ax == 0 else 1), g.shape[1] // (4 if ax == 1 else 1))
                             for g, ax in zip(grads, axes)]
        self.out_shapes = [_sds((4,) + s, BF16) for s in self.shard_shapes]
        self.sem_shapes = [pltpu.SemaphoreType.DMA((nw,)), pltpu.SemaphoreType.DMA((nw, 3)), pltpu.SemaphoreType.DMA((nw, 3))]

    def _copies(self, ins, outs, sems):
        local_sems, send_sems, recv_sems = sems
        x, y, c = _place()
        k = 2 * x + y
        local, remote, arrivals = [], [], []
        for j, ax in enumerate(self.axes):
            n = self.shard_shapes[j][ax]
            local.append(pltpu.make_async_copy(_shard_ref(ins[j], ax, k, n), outs[j].at[0], local_sems.at[j]))
            for m in range(1, 4):
                remote.append(pltpu.make_async_remote_copy(
                    src_ref=_shard_ref(ins[j], ax, k ^ m, n), dst_ref=outs[j].at[m],
                    send_sem=send_sems.at[j, m - 1], recv_sem=recv_sems.at[j, m - 1],
                    device_id=_chip_peer(x, y, c, m), device_id_type=MESH))
                arrivals.append(pltpu.make_async_remote_copy(
                    src_ref=_shard_ref(ins[j], ax, k, n), dst_ref=outs[j].at[m],
                    send_sem=send_sems.at[j, m - 1], recv_sem=recv_sems.at[j, m - 1],
                    device_id=(x, y, c), device_id_type=MESH))
        return local, remote, arrivals

    def start(self, ins, outs, sems):
        local, remote, _ = self._copies(ins, outs, sems)
        for cp in local + remote:
            cp.start()

    def wait(self, ins, outs, sems):
        local, remote, arrivals = self._copies(ins, outs, sems)
        for cp in arrivals:
            cp.wait_recv()
        for cp in remote:
            cp.wait_send()
        for cp in local:
            cp.wait()


def _run_plan(plan, name):
    nc = len(plan.inputs)

    def body(*refs):
        ins, outs, sems = refs[:nc], refs[nc:2 * nc], refs[2 * nc:]
        plan.start(ins, outs, sems)
        plan.wait(ins, outs, sems)

    return pl.pallas_call(body, name=name, in_specs=[ANY] * nc, out_specs=[ANY] * nc, out_shape=list(plan.out_shapes),
                          scratch_shapes=list(plan.sem_shapes))(*plan.inputs)


def _swap_sibling(parts):
    nw = len(parts)

    def body(*refs):
        ins, outs = refs[:nw], refs[nw:2 * nw]
        send_sems, recv_sems = refs[2 * nw:]
        x, y, c = _place()
        copies = [pltpu.make_async_remote_copy(
            src_ref=ins[j], dst_ref=outs[j], send_sem=send_sems.at[j], recv_sem=recv_sems.at[j],
            device_id=(x, y, 1 - c), device_id_type=MESH) for j in range(nw)]
        for cp in copies:
            cp.start()
        for cp in copies:
            cp.wait()

    return pl.pallas_call(
        body, name="swap_sibling", in_specs=[ANY] * nw, out_specs=[ANY] * nw,
        out_shape=[_sds(p.shape, p.dtype) for p in parts],
        scratch_shapes=[pltpu.SemaphoreType.DMA((nw,)), pltpu.SemaphoreType.DMA((nw,))],
    )(*parts)


BIG = ("w_ffn1_in", "w_ffn1_out", "w_in", "w_pool_branch", "w_attn_branch", "w_out", "w_ffn2_in", "w_ffn2_out")
BIG_AXIS = {"w_ffn1_in": 1, "w_ffn1_out": 0, "w_in": 1, "w_pool_branch": 1, "w_attn_branch": 1, "w_out": 0,
            "w_ffn2_in": 1, "w_ffn2_out": 0}


class _Sharded:
    fused_scatter = True

    def __init__(self, shards):
        self.shards, self.full, self.recv = shards, {}, {}

    def gather_plan(self, names):
        return _GatherPlan([self.shards[n] for n in names], [BIG_AXIS[n.split("/")[0]] for n in names])

    def gather_now(self, names):
        self.gathered(names, _run_plan(self.gather_plan(names), "gather_" + names[0]))

    def gathered(self, names, outs):
        self.full.update(zip(names, outs))

    def scatter_plan(self, names, grads):
        return _ScatterPlan([grads[n] for n in names], [BIG_AXIS[n] for n in names])

    def scatter_now(self, names, grads):
        self.scattered(names, _run_plan(self.scatter_plan(names, grads), "scatter_" + names[0]))

    def scattered(self, names, outs):
        self.recv.update(zip(names, outs))


class _Whole:
    fused_scatter = False

    def __init__(self, full):
        self.full, self.recv = dict(full), {}

    def gather_plan(self, names):
        return None

    def gather_now(self, names):
        pass

    def gathered(self, names, outs):
        pass

    def scatter_plan(self, names, grads):
        return None

    def scatter_now(self, names, grads):
        pass

    def scattered(self, names, outs):
        pass


def _vec(rows):
    pad = [jnp.zeros((1, D), F32)] * (8 - len(rows))
    return jnp.concatenate([r.reshape(1, D) for r in rows] + pad, axis=0)


def _block_diag(w_pool):
    n, c = w_pool.shape[0], w_pool.shape[1]
    eye = jnp.eye(n, dtype=w_pool.dtype)
    return (eye[:, None, :, None] * w_pool[:, :, None, :]).reshape(n * c, n * c)


def _example_step(x, tgt, positions, mod, gains, w_pool, pool_scale, ws):
    T = x.shape[0]
    assert (T // BLK // DIL[-1]) & (T // BLK // DIL[-1] - 1) == 0, "blocks per sequence must be a power of two"
    sh1, sc1, gt1, sh2, sc2, gt2, sh3, sc3, gt3 = [mod[j * D:(j + 1) * D] for j in range(NMOD)]
    g1, g2, g3, gf = gains
    vec1, vec2, vec3 = _vec([g1, sh1, sc1, gt1]), _vec([g2, sh2, sc2, gt2]), _vec([g3, sh3, sc3, gt3])
    inv_freq = 10000.0 ** (-jnp.arange(0, HD, 2, dtype=F32) / HD)
    ang = positions.astype(F32)[:, None] * inv_freq
    cos = jnp.tile(jnp.cos(ang), (1, 4))
    sin = jnp.tile(jnp.concatenate([-jnp.sin(ang), jnp.sin(ang)], axis=1), (1, 2))
    wp_bd = _block_diag(w_pool).astype(BF16)
    ones_bd = _block_diag(jnp.ones((NH, HD, HD), F32)).astype(BF16)
    ps = jnp.concatenate([pool_scale.reshape(1, PW), jnp.zeros((7, PW), F32)], axis=0)
    wb = ws.full

    ws.gather_now(["w_ffn1_in", "w_ffn1_out"])
    mixw = ["w_in", "w_pool_branch", "w_attn_branch", "w_out"]
    (h1, u1, a1, b1, f1), got = _ffn_fwd(x, vec1, [wb["w_ffn1_in"]], wb["w_ffn1_out"], "ffn1_fwd", ws.gather_plan(mixw))
    ws.gathered(mixw, got)
    (u2, p, qs, ks, vs, gates), got = _mix_proj(h1, vec2, wb["w_in"], cos, sin, ws.gather_plan(["w_ffn2_in/0"]))
    ws.gathered(["w_ffn2_in/0"], got)
    qs, ks, vs = [_flat(t) for t in qs], [_flat(t) for t in ks], [_flat(t) for t in vs]
    nbs = [T // d // BLK for d in DIL]
    os, lses = [], []
    for gi, riders in enumerate((["w_ffn2_out"], ["w_ffn2_in/1"], None)):
        (o, lse), got = _attn_fwd(qs[gi], ks[gi], vs[gi], nbs[gi], f"attn_fwd{gi}", riders and ws.gather_plan(riders))
        ws.gathered(riders or [], got)
        os.append(o)
        lses.append(lse)
    win3 = [wb["w_ffn2_in/0"], wb["w_ffn2_in/1"]] if "w_ffn2_in/0" in wb else [wb["w_ffn2_in"]]
    os_r = [_by_residue(t, d) for t, d in zip(os, DIL)]
    lses_r = [_by_residue(t, d) for t, d in zip(lses, DIL)]
    h2, ypool, yattn, merged, mixout, dpool = _mix_merge(
        h1, vec2, p, os_r, lses_r, gates, wp_bd, ps, wb["w_pool_branch"], wb["w_attn_branch"], wb["w_out"])
    (h3, u3, a3, b3, f3), _ = _ffn_fwd(h2, vec3, win3, wb["w_ffn2_out"], "ffn2_fwd")
    dh3, lacc = _final_loss(h3, tgt, _vec([gf]))
    loss = 0.5 * jnp.sum(lacc[0]) / D

    grads = {}

    def wgrad_cols(name, xx, yy, riders):
        plan = ws.scatter_plan(riders, grads) if riders else None
        if ws.fused_scatter:
            blocks, got = _wgrad_scatter(xx, yy, "wg_" + name, 1024, comm=plan)
            ws.scattered([name], [blocks])
        else:
            grads[name], got = _wgrad(xx, yy, "wg_" + name, D, 512, 1024, comm=plan)
        ws.scattered(riders, got)

    (dh2, dab3, s3, df3, acc3), _ = _ffn_bwd(dh3, h2, a3, b3, f3, vec3, win3, wb["w_ffn2_out"], "ffn2_bwd")
    grads["w_ffn2_out"], _ = _wgrad(s3, df3, "wg_ffn2_out", FC, 512, 1024)
    wgrad_cols("w_ffn2_in", u3, dab3, ["w_ffn2_out"])
    (dmo, dbp, dba, dgates, do0, do1, do2, e0, e1, e2, dd, dyp, acc2a, accps), _ = _mix_bwd_a(
        dh2, vec2, mixout, gates, ypool, yattn, dpool, os_r, lses_r, wp_bd, ps,
        wb["w_pool_branch"], wb["w_attn_branch"], wb["w_out"], ones_bd)
    grads["w_out"], _ = _wgrad(merged, dmo, "wg_out", D, 512, 1024)
    grads["w_pool_branch"], _ = _wgrad(ypool, dbp, "wg_pool_branch", PW, 512, 1024)
    grads["w_attn_branch"], _ = _wgrad(yattn, dba, "wg_attn_branch", GA, 512, 1024)
    gwp, _ = _wgrad(dpool, dyp, "wg_pool", PW, PW, 1024, out_dtype=F32)
    n = len(POOL_WINDOWS)
    c = PW // n
    grad_w_pool = jnp.stack([gwp[j * c:(j + 1) * c, j * c:(j + 1) * c] for j in range(n)], axis=0)
    small3 = ["w_out", "w_pool_branch", "w_attn_branch"]
    dqs, dks, dvs = [], [], []
    for gi, (do, e) in enumerate(((do0, e0), (do1, e1), (do2, e2))):
        plan = ws.scatter_plan(small3, grads) if gi == 0 else None
        (dq, dk, dv), got = _attn_bwd(qs[gi], ks[gi], vs[gi], _flat(do), lses[gi], _flat(e), nbs[gi], f"attn_bwd{gi}", plan)
        if gi == 0:
            ws.scattered(small3, got)
        dqs.append(_by_residue(dq, DIL[gi]))
        dks.append(_by_residue(dk, DIL[gi]))
        dvs.append(_by_residue(dv, DIL[gi]))
    dh1, dproj, acc2b = _mix_bwd_b(dh2, h1, vec2, dd, dqs, dks, dvs, dgates, cos, sin, wb["w_in"])
    wgrad_cols("w_in", u2, dproj, [])
    (dx, dab1, s1, df1, acc1), _ = _ffn_bwd(dh1, x, a1, b1, f1, vec1, [wb["w_ffn1_in"]], wb["w_ffn1_out"], "ffn1_bwd")
    grads["w_ffn1_out"], _ = _wgrad(s1, df1, "wg_ffn1_out", FC, 512, 1024)
    wgrad_cols("w_ffn1_in", u1, dab1, ["w_ffn1_out"])

    dmod = jnp.concatenate([acc1[0], acc1[1], acc1[3], acc2b[0], acc2b[1], acc2a[3], acc3[0], acc3[1], acc3[3]])
    dgains = jnp.stack([acc1[2], acc2b[2], acc3[2], lacc[1]], axis=0)
    return loss, dx, dmod, dgains, grad_w_pool, accps[0], grads


SMALL = ("b_ada", "g_norm_ffn1", "g_norm_mix", "g_norm_ffn2", "g_final", "pool_scale", "w_pool")
WEIGHTS = ("w_ada", "b_ada", "g_norm_ffn1", "w_ffn1_in", "w_ffn1_out", "g_norm_mix", "w_in", "w_pool", "pool_scale",
           "w_pool_branch", "w_attn_branch", "w_out", "g_norm_ffn2", "w_ffn2_in", "w_ffn2_out", "g_final")


def _pack_small(t):
    return jnp.concatenate([t[n].reshape(-1) for n in SMALL]).reshape(1, -1)


def _unpack_small(flat, like):
    out, off = {}, 0
    for n in SMALL:
        size = like[n].size
        out[n] = flat[0, off:off + size].reshape(like[n].shape)
        off += size
    return out


def kernel(x, c, positions, w_ada, b_ada, g_norm_ffn1, w_ffn1_in, w_ffn1_out, g_norm_mix, w_in, w_pool, pool_scale, w_pool_branch, w_attn_branch, w_out, g_norm_ffn2, w_ffn2_in, w_ffn2_out, g_final, loss_target, m_w_ada, m_b_ada, m_g_norm_ffn1, m_w_ffn1_in, m_w_ffn1_out, m_g_norm_mix, m_w_in, m_w_pool, m_pool_scale, m_w_pool_branch, m_w_attn_branch, m_w_out, m_g_norm_ffn2, m_w_ffn2_in, m_w_ffn2_out, m_g_final, v_w_ada, v_b_ada, v_g_norm_ffn1, v_w_ffn1_in, v_w_ffn1_out, v_g_norm_mix, v_w_in, v_w_pool, v_pool_scale, v_w_pool_branch, v_w_attn_branch, v_w_out, v_g_norm_ffn2, v_w_ffn2_in, v_w_ffn2_out, v_g_final):
    w = dict(w_ada=w_ada, b_ada=b_ada, g_norm_ffn1=g_norm_ffn1, w_ffn1_in=w_ffn1_in, w_ffn1_out=w_ffn1_out,
             g_norm_mix=g_norm_mix, w_in=w_in, w_pool=w_pool, pool_scale=pool_scale, w_pool_branch=w_pool_branch,
             w_attn_branch=w_attn_branch, w_out=w_out, g_norm_ffn2=g_norm_ffn2, w_ffn2_in=w_ffn2_in,
             w_ffn2_out=w_ffn2_out, g_final=g_final)
    mom = dict(w_ada=m_w_ada, b_ada=m_b_ada, g_norm_ffn1=m_g_norm_ffn1, w_ffn1_in=m_w_ffn1_in, w_ffn1_out=m_w_ffn1_out,
               g_norm_mix=m_g_norm_mix, w_in=m_w_in, w_pool=m_w_pool, pool_scale=m_pool_scale,
               w_pool_branch=m_w_pool_branch, w_attn_branch=m_w_attn_branch, w_out=m_w_out, g_norm_ffn2=m_g_norm_ffn2,
               w_ffn2_in=m_w_ffn2_in, w_ffn2_out=m_w_ffn2_out, g_final=m_g_final)
    var = dict(w_ada=v_w_ada, b_ada=v_b_ada, g_norm_ffn1=v_g_norm_ffn1, w_ffn1_in=v_w_ffn1_in, w_ffn1_out=v_w_ffn1_out,
               g_norm_mix=v_g_norm_mix, w_in=v_w_in, w_pool=v_w_pool, pool_scale=v_pool_scale,
               w_pool_branch=v_w_pool_branch, w_attn_branch=v_w_attn_branch, w_out=v_w_out, g_norm_ffn2=v_g_norm_ffn2,
               w_ffn2_in=v_w_ffn2_in, w_ffn2_out=v_w_ffn2_out, g_final=v_g_final)
    ix, iy, ic = _place()
    chip = 2 * ix + iy
    me = 4 * ix + 2 * iy + ic
    nada = w_ada.shape[2]

    c_all = _gather_small(c)[:, 0, :]
    b_shard = lax.dynamic_slice_in_dim(b_ada, chip * nada, nada, axis=1)
    mod_cols = _ada_fwd(c_all, w_ada[0], b_shard)
    mod_all = _gather_small(mod_cols)
    mod = jnp.concatenate([lax.dynamic_index_in_dim(mod_all[4 * (kk >> 1) + 2 * (kk & 1)], me, axis=0, keepdims=False)
                           for kk in range(4)])

    shards = {n: w[n][0].astype(BF16) for n in BIG}
    half = D // 2
    shards["w_ffn2_in/0"], shards["w_ffn2_in/1"] = shards["w_ffn2_in"][:half], shards["w_ffn2_in"][half:]
    ws = _Sharded(shards)
    loss, dx, dmod, dgains, g_w_pool, g_pool_scale, grads = _example_step(
        x[0], loss_target[0], positions[0], mod, (g_norm_ffn1[0], g_norm_mix[0], g_norm_ffn2[0], g_final),
        w_pool[0], pool_scale[0], ws)

    small_g = dict(b_ada=dmod, g_norm_ffn1=dgains[0], g_norm_mix=dgains[1], g_norm_ffn2=dgains[2], g_final=dgains[3],
                   pool_scale=g_pool_scale, w_pool=g_w_pool)
    tail = jnp.zeros((1, 128), F32)
    gathered = _gather_small(jnp.concatenate([_pack_small(small_g), jnp.pad(loss.reshape(1, 1), ((0, 0), (0, 127)))], axis=1))
    sg, sd, sm, sv = _adam_small(*[jnp.concatenate([_pack_small(t), tail], axis=1) for t in (w, mom, var)], gathered)
    small_out = [_unpack_small(t, w) for t in (sg, sd, sm, sv)]
    loss = sg[0, sg.shape[1] - 128]

    dmod_all = gathered[:, 0, :NMOD * D]
    dmod_cols = lax.dynamic_slice_in_dim(dmod_all, chip * nada, nada, axis=1)
    g_ada = _ada_bwd(c_all, dmod_cols)
    ada_out = _adam(w_ada[0], m_w_ada[0], v_w_ada[0], [g_ada], "adam_w_ada")

    partial = [_sum4(ws.recv[n], "sum_" + n) for n in BIG]
    other = _swap_sibling(partial)
    big_out = {n: _adam(w[n][0], mom[n][0], var[n][0], [pa, pb], "adam_" + n) for n, pa, pb in zip(BIG, partial, other)}

    def leaf(kind, n):
        if n == "w_ada":
            return ada_out[kind][None]
        if n in big_out:
            return big_out[n][kind][None]
        return small_out[kind][n]

    return (loss, dx[None], *[leaf(kind, n) for kind in range(4) for n in WEIGHTS])
```

```python
import jax
import jax.numpy as jnp
from jax import lax
from jax.experimental import pallas as pl
from jax.experimental.pallas import tpu as pltpu

F32 = jnp.float32
BF16 = jnp.bfloat16

D = 1024
FF = 2816
FC = 1408
PW = 256
GA = 256
HD = 64
LANES = 128
NH = GA // HD
NG = 3
DIL = (1, 4, 16)
BLK = 128
GW = 2 * D
INW = PW + 3 * NG * GA + GW
NMOD = 9
POOL_WINDOWS = (2, 4, 8, 16)
HALO = 16
EPS = 1e-6
SCALE = HD ** -0.5
NEG = -1e30

LR, B1, B2, AEPS, WD, STEP = 0.001, 0.9, 0.999, 1e-08, 0.01, 10

VMEM_BIG = 56 * 1024 * 1024
TM = 256

MESH = pl.DeviceIdType.MESH
ANY = pl.BlockSpec(memory_space=pl.ANY)


def _call(body, name, grid, in_specs, out_specs, out_shape, scratch=(), vmem=None, comm=None):
    params = pltpu.CompilerParams(dimension_semantics=("arbitrary",) * len(grid), vmem_limit_bytes=vmem)
    n_in, n_out, n_scr = len(in_specs), len(out_shape), len(scratch)
    if comm is None:
        call = pl.pallas_call(body, name=name, grid=grid, in_specs=list(in_specs), out_specs=list(out_specs),
                              out_shape=list(out_shape), scratch_shapes=list(scratch), compiler_params=params)
        return lambda *args: (call(*args), ())
    nc = len(comm.inputs)

    def body_with_comm(*refs):
        ins, refs = refs[:n_in], refs[n_in:]
        c_ins, refs = refs[:nc], refs[nc:]
        outs, refs = refs[:n_out], refs[n_out:]
        c_outs, refs = refs[:nc], refs[nc:]
        scr, sems = refs[:n_scr], refs[n_scr:]
        first = pl.program_id(0) == 0
        last = pl.program_id(0) == grid[0] - 1
        for ax in range(1, len(grid)):
            first = jnp.logical_and(first, pl.program_id(ax) == 0)
            last = jnp.logical_and(last, pl.program_id(ax) == grid[ax] - 1)

        @pl.when(first)
        def _():
            comm.start(c_ins, c_outs, sems)

        body(*ins, *outs, *scr)

        @pl.when(last)
        def _():
            comm.wait(c_ins, c_outs, sems)

    call = pl.pallas_call(
        body_with_comm, name=name, grid=grid, in_specs=list(in_specs) + [ANY] * nc,
        out_specs=list(out_specs) + [ANY] * nc, out_shape=list(out_shape) + list(comm.out_shapes),
        scratch_shapes=list(scratch) + list(comm.sem_shapes), compiler_params=params)

    def run(*args):
        res = call(*args, *comm.inputs)
        return res[:n_out], res[n_out:]

    return run


def _rows(tm, n):
    return pl.BlockSpec((tm, n), lambda i: (i, 0))


def _const(shape):
    return pl.BlockSpec(shape, lambda i: (0,) * len(shape))


def _sds(shape, dtype):
    return jax.ShapeDtypeStruct(shape, dtype)


def _dot(a, b):
    return jnp.dot(a, b, preferred_element_type=F32)


def _dot_nt(a, b):
    return lax.dot_general(a, b, (((1,), (1,)), ((), ())), preferred_element_type=F32)


def _dot_tn(a, b):
    return lax.dot_general(a, b, (((0,), (0,)), ((), ())), preferred_element_type=F32)


def _colsum(v):
    return jnp.sum(v, axis=0, keepdims=True)


def _norm_fwd(h, g, sh, sc):
    r = lax.rsqrt(jnp.mean(h * h, axis=-1, keepdims=True) + EPS)
    xh = h * r
    n = xh * g
    return xh, r, n, n * (1.0 + sc) + sh


def _norm_bwd(du, xh, r, n, g, sc):
    dn = du * (1.0 + sc)
    dxh = dn * g
    dh = r * (dxh - xh * jnp.mean(dxh * xh, axis=-1, keepdims=True))
    return dh, _colsum(du), _colsum(du * n), _colsum(dn * xh)


def _load_once(pairs, sems):
    @pl.when(pl.program_id(0) == 0)
    def _():
        cps = [pltpu.make_async_copy(src, dst, sems.at[j]) for j, (src, dst) in enumerate(pairs)]
        for cp in cps:
            cp.start()
        for cp in cps:
            cp.wait()


def _zero_first(ref):
    @pl.when(pl.program_id(0) == 0)
    def _():
        ref[...] = jnp.zeros(ref.shape, ref.dtype)


def _row_chunks(hbm_refs, vmem_ref):
    pairs, row = [], 0
    for ref in hbm_refs:
        pairs.append((ref, vmem_ref.at[pl.ds(row, ref.shape[0]), :]))
        row += ref.shape[0]
    return pairs


def _ffn_fwd(h, vec, wins, wout, name, comm=None):
    T = h.shape[0]
    nwin = len(wins)

    def body(h_ref, vec_ref, *rest):
        win_hbms, (wout_hbm, ho_ref, u_ref, a_ref, b_ref, f_ref, win_v, wout_v, sems) = rest[:nwin], rest[nwin:]
        _load_once(_row_chunks(win_hbms, win_v) + [(wout_hbm, wout_v)], sems)
        hh = h_ref[...]
        g, sh, sc, gt = vec_ref[0:1, :], vec_ref[1:2, :], vec_ref[2:3, :], vec_ref[3:4, :]
        _, _, _, u = _norm_fwd(hh, g, sh, sc)
        ub = u.astype(BF16)
        u_ref[...] = ub
        acc = None
        for j in range(FF // FC):
            lo, hi = j * FC, (j + 1) * FC
            a = _dot(ub, win_v[:, lo:hi])
            b = _dot(ub, win_v[:, FF + lo:FF + hi])
            a_ref[:, lo:hi] = a.astype(BF16)
            b_ref[:, lo:hi] = b.astype(BF16)
            s = (a * jax.nn.sigmoid(a) * b).astype(BF16)
            part = _dot(s, wout_v[lo:hi, :])
            acc = part if acc is None else acc + part
        f_ref[...] = acc.astype(BF16)
        ho_ref[...] = hh + 0.5 * gt * acc

    return _call(
        body, name, (T // TM,),
        [_rows(TM, D), _const((8, D))] + [ANY] * (nwin + 1),
        [_rows(TM, D), _rows(TM, D), _rows(TM, FF), _rows(TM, FF), _rows(TM, D)],
        [_sds((T, D), F32), _sds((T, D), BF16), _sds((T, FF), BF16), _sds((T, FF), BF16), _sds((T, D), BF16)],
        scratch=[pltpu.VMEM((D, 2 * FF), BF16), pltpu.VMEM((FF, D), BF16), pltpu.SemaphoreType.DMA((nwin + 1,))],
        vmem=VMEM_BIG, comm=comm,
    )(h, vec, *wins, wout)


def _ffn_bwd(dh, h, a, b, f, vec, wins, wout, name, comm=None):
    T = h.shape[0]
    nwin = len(wins)

    def body(dh_ref, h_ref, a_ref, b_ref, f_ref, vec_ref, *rest):
        win_hbms, (wout_hbm, dhi_ref, dab_ref, s_ref, df_ref, acc_ref, win_v, wout_v, sems) = rest[:nwin], rest[nwin:]
        _load_once(_row_chunks(win_hbms, win_v) + [(wout_hbm, wout_v)], sems)
        _zero_first(acc_ref)
        g, sh, sc, gt = vec_ref[0:1, :], vec_ref[1:2, :], vec_ref[2:3, :], vec_ref[3:4, :]
        dho = dh_ref[...]
        df = (0.5 * gt * dho).astype(BF16)
        df_ref[...] = df
        dgt = _colsum(0.5 * dho * f_ref[...].astype(F32))
        du = None
        for j in range(FF // FC):
            lo, hi = j * FC, (j + 1) * FC
            av = a_ref[:, lo:hi].astype(F32)
            bv = b_ref[:, lo:hi].astype(F32)
            ds = _dot_nt(df, wout_v[lo:hi, :])
            sig = jax.nn.sigmoid(av)
            sa = av * sig
            s_ref[:, lo:hi] = (sa * bv).astype(BF16)
            da = (ds * bv * (sig * (1.0 + av * (1.0 - sig)))).astype(BF16)
            db = (ds * sa).astype(BF16)
            dab_ref[:, lo:hi] = da
            dab_ref[:, FF + lo:FF + hi] = db
            part = _dot_nt(da, win_v[:, lo:hi]) + _dot_nt(db, win_v[:, FF + lo:FF + hi])
            du = part if du is None else du + part
        xh, r, n, _ = _norm_fwd(h_ref[...], g, sh, sc)
        dhn, dsh, dsc, dg = _norm_bwd(du, xh, r, n, g, sc)
        dhi_ref[...] = dho + dhn
        acc_ref[0:1, :] += dsh
        acc_ref[1:2, :] += dsc
        acc_ref[2:3, :] += dg
        acc_ref[3:4, :] += dgt

    return _call(
        body, name, (T // TM,),
        [_rows(TM, D), _rows(TM, D), _rows(TM, FF), _rows(TM, FF), _rows(TM, D), _const((8, D))] + [ANY] * (nwin + 1),
        [_rows(TM, D), _rows(TM, 2 * FF), _rows(TM, FF), _rows(TM, D), _const((8, D))],
        [_sds((T, D), F32), _sds((T, 2 * FF), BF16), _sds((T, FF), BF16), _sds((T, D), BF16), _sds((8, D), F32)],
        scratch=[pltpu.VMEM((D, 2 * FF), BF16), pltpu.VMEM((FF, D), BF16), pltpu.SemaphoreType.DMA((nwin + 1,))],
        vmem=VMEM_BIG, comm=comm,
    )(dh, h, a, b, f, vec, *wins, wout)


def _wgrad(x, y, name, tk, tn, tt, out_dtype=BF16, comm=None):
    T, K = x.shape
    N = y.shape[1]
    nt = T // tt

    def body(x_ref, y_ref, o_ref, acc_ref):
        t = pl.program_id(2)
        part = _dot_tn(x_ref[...], y_ref[...])

        @pl.when(t == 0)
        def _():
            acc_ref[...] = part

        @pl.when(t > 0)
        def _():
            acc_ref[...] += part

        @pl.when(t == nt - 1)
        def _():
            o_ref[...] = acc_ref[...].astype(out_dtype)

    (out,), c_outs = _call(
        body, name, (K // tk, N // tn, nt),
        [pl.BlockSpec((tt, tk), lambda i, j, t: (t, i)), pl.BlockSpec((tt, tn), lambda i, j, t: (t, j))],
        [pl.BlockSpec((tk, tn), lambda i, j, t: (i, j))], [_sds((K, N), out_dtype)],
        scratch=[pltpu.VMEM((tk, tn), F32)], vmem=VMEM_BIG, comm=comm,
    )(x, y)
    return out, c_outs


def _wgrad_scatter(x, y, name, tt, comm=None):
    T, K = x.shape
    n = y.shape[1] // 4
    nt = T // tt
    nc = 0 if comm is None else len(comm.inputs)

    def body(chip_ref, x_ref, y_ref, *refs):
        c_ins, refs = refs[:nc], refs[nc:]
        recv_ref, refs = refs[0], refs[1:]
        c_outs, refs = refs[:nc], refs[nc:]
        acc_ref, stage_ref, local_sem, send_sems, recv_sems = refs[:5]
        j, t = pl.program_id(0), pl.program_id(1)
        px, py, pc = _place()

        def sends():
            return [pltpu.make_async_remote_copy(
                src_ref=stage_ref.at[m], dst_ref=recv_ref.at[m], send_sem=send_sems.at[m - 1], recv_sem=recv_sems.at[m - 1],
                device_id=_chip_peer(px, py, pc, m), device_id_type=MESH) for m in range(1, 4)]

        own = pltpu.make_async_copy(stage_ref.at[0], recv_ref.at[0], local_sem.at[0])
        if comm is not None:
            @pl.when(jnp.logical_and(j == 0, t == 0))
            def _():
                comm.start(c_ins, c_outs, refs[5:])

        part = _dot_tn(x_ref[...], y_ref[...])

        @pl.when(t == 0)
        def _():
            acc_ref[...] = part

        @pl.when(t > 0)
        def _():
            acc_ref[...] += part

        for jj in range(4):
            m = (jj + 1) & 3

            @pl.when(jnp.logical_and(j == jj, t == nt - 1))
            def _():
                stage_ref[m] = acc_ref[...].astype(BF16)
                if m:
                    sends()[m - 1].start()
                else:
                    own.start()
                    for cp in sends():
                        cp.wait_recv()
                    for cp in sends():
                        cp.wait_send()
                    own.wait()
                    if comm is not None:
                        comm.wait(c_ins, c_outs, refs[5:])

    grid_spec = pltpu.PrefetchScalarGridSpec(
        num_scalar_prefetch=1, grid=(4, nt),
        in_specs=[pl.BlockSpec((tt, K), lambda j, t, chip: (t, 0)),
                  pl.BlockSpec((tt, n), lambda j, t, chip: (t, chip[0] ^ ((j + 1) & 3)))] + [ANY] * nc,
        out_specs=[ANY] * (1 + nc),
        scratch_shapes=[pltpu.VMEM((K, n), F32), pltpu.VMEM((4, K, n), BF16), pltpu.SemaphoreType.DMA((1,)),
                        pltpu.SemaphoreType.DMA((3,)), pltpu.SemaphoreType.DMA((3,))]
        + ([] if comm is None else list(comm.sem_shapes)))
    px, py, _ = _place()
    res = pl.pallas_call(
        body, name=name, grid_spec=grid_spec,
        out_shape=[_sds((4, K, n), BF16)] + ([] if comm is None else list(comm.out_shapes)),
        compiler_params=pltpu.CompilerParams(dimension_semantics=("arbitrary", "arbitrary"), vmem_limit_bytes=VMEM_BIG),
    )((2 * px + py).astype(jnp.int32).reshape(1), x, y, *([] if comm is None else comm.inputs))
    return res[0], res[1:]


def _final_loss(h, tgt, gvec):
    T = h.shape[0]

    def body(h_ref, t_ref, g_ref, dh_ref, acc_ref):
        _zero_first(acc_ref)
        hh = h_ref[...]
        g = g_ref[0:1, :]
        r = lax.rsqrt(jnp.mean(hh * hh, axis=-1, keepdims=True) + EPS)
        xh = hh * r
        err = xh * g - t_ref[...]
        dy = err * (1.0 / D)
        dxh = dy * g
        dh_ref[...] = r * (dxh - xh * jnp.mean(dxh * xh, axis=-1, keepdims=True))
        acc_ref[0:1, :] += _colsum(err * err)
        acc_ref[1:2, :] += _colsum(dy * xh)

    return _call(
        body, "final_loss", (T // TM,),
        [_rows(TM, D), _rows(TM, D), _const((8, D))],
        [_rows(TM, D), _const((8, D))],
        [_sds((T, D), F32), _sds((8, D), F32)],
    )(h, tgt, gvec)[0]


def _swap_halves(t):
    w = t.shape[1]
    lane = lax.broadcasted_iota(jnp.int32, t.shape, 1)
    return jnp.where(lane % HD < HD // 2, pltpu.roll(t, w - HD // 2, 1), pltpu.roll(t, HD // 2, 1))


def _rope(t, cos, sin_signed):
    c = jnp.tile(cos, (1, t.shape[1] // cos.shape[1]))
    s = jnp.tile(sin_signed, (1, t.shape[1] // sin_signed.shape[1]))
    return t * c + _swap_halves(t) * s


def _rope_bwd(dt, cos, sin_signed):
    c = jnp.tile(cos, (1, dt.shape[1] // cos.shape[1]))
    s = jnp.tile(sin_signed, (1, dt.shape[1] // sin_signed.shape[1]))
    return dt * c + _swap_halves(dt * s)


def _rm_spec(dil):
    return pl.BlockSpec((dil, TM // dil, GA), lambda i: (0, i, 0))


def _to_residues(t, dst_ref, scr_ref, dil):
    if dil == 1:
        dst_ref[0] = t.astype(dst_ref.dtype)
        return
    for j in range(GA // LANES):
        scr_ref[j] = t[:, j * LANES:(j + 1) * LANES]
    for r in range(dil):
        for j in range(GA // LANES):
            rows = scr_ref.at[j][pl.ds(r, TM // dil, stride=dil), :]
            dst_ref[r, :, j * LANES:(j + 1) * LANES] = rows.astype(dst_ref.dtype)


def _from_residues(src_ref, scr_ref, dil):
    if dil == 1:
        return src_ref[0].astype(F32)
    for r in range(dil):
        for j in range(GA // LANES):
            scr_ref.at[j][pl.ds(r, TM // dil, stride=dil), :] = src_ref[r, :, j * LANES:(j + 1) * LANES].astype(F32)
    return jnp.concatenate([scr_ref[j] for j in range(GA // LANES)], axis=1)


def _mix_proj(h, vec, win, cos, sin, comm=None):
    T = h.shape[0]

    def body(h_ref, vec_ref, win_hbm, cos_ref, sin_ref, u_ref, p_ref, *rest):
        qkv_refs, gates_ref, win_v, scr_ref, sems = rest[:3 * NG], rest[3 * NG], rest[3 * NG + 1], rest[3 * NG + 2], rest[3 * NG + 3]
        _load_once([(win_hbm, win_v)], sems)
        g, sh, sc = vec_ref[0:1, :], vec_ref[1:2, :], vec_ref[2:3, :]
        _, _, _, u = _norm_fwd(h_ref[...], g, sh, sc)
        ub = u.astype(BF16)
        u_ref[...] = ub
        p_ref[...] = _dot(ub, win_v[:, 0:PW])
        cos_t, sin_t = cos_ref[...], sin_ref[...]
        for j in range(3 * NG):
            col = PW + j * GA
            t = _dot(ub, win_v[:, col:col + GA])
            if j < 2 * NG:
                t = _rope(t, cos_t, sin_t)
            _to_residues(t, qkv_refs[j], scr_ref, DIL[j % NG])
        for j in range(GW // 512):
            col = PW + 3 * NG * GA + j * 512
            gates_ref[:, j * 512:(j + 1) * 512] = jax.nn.sigmoid(_dot(ub, win_v[:, col:col + 512])).astype(BF16)

    outs, c_outs = _call(
        body, "mix_proj", (T // TM,),
        [_rows(TM, D), _const((8, D)), ANY, _rows(TM, 128), _rows(TM, 128)],
        [_rows(TM, D), _rows(TM, PW)] + [_rm_spec(d) for d in DIL] * 3 + [_rows(TM, GW)],
        [_sds((T, D), BF16), _sds((T, PW), F32)] + [_sds((d, T // d, GA), BF16) for d in DIL] * 3 + [_sds((T, GW), BF16)],
        scratch=[pltpu.VMEM((D, INW), BF16), pltpu.VMEM((GA // LANES, TM, LANES), F32), pltpu.SemaphoreType.DMA((1,))],
        vmem=VMEM_BIG, comm=comm,
    )(h, vec, win, cos, sin)
    return (outs[0], outs[1], outs[2:2 + NG], outs[2 + NG:2 + 2 * NG], outs[2 + 2 * NG:2 + 3 * NG], outs[2 + 3 * NG]), c_outs


def _head_masks():
    lane_head = lax.broadcasted_iota(jnp.int32, (BLK, GA), 1) // HD
    return [lane_head == hd for hd in range(NH)]


def _expand_heads(t, hm):
    return jnp.concatenate([jnp.where(m, t, jnp.zeros_like(t)) for m in hm], axis=0)


def _collapse_heads(tb, hm):
    out = None
    for hd, m in enumerate(hm):
        part = jnp.where(m, tb[hd * BLK:(hd + 1) * BLK, :], 0.0)
        out = part if out is None else out + part
    return out


def _head_rows(t):
    return jnp.concatenate([t[:, hd * HD:hd * HD + 1] for hd in range(NH)], axis=0)


def _band_masks():
    a = lax.broadcasted_iota(jnp.int32, (NH * BLK, BLK), 0) & (BLK - 1)
    c = lax.broadcasted_iota(jnp.int32, (NH * BLK, BLK), 1)
    return c <= a, c >= a


def _attn_specs(nbt):
    cur = pl.BlockSpec((2 * BLK, GA), lambda i: (i, 0))
    prev = pl.BlockSpec((BLK, GA), lambda i: (jnp.maximum(2 * i - 1, 0), 0))
    nxt = pl.BlockSpec((BLK, GA), lambda i: (jnp.minimum(2 * i + 2, nbt - 1), 0))
    return cur, prev, nxt


def _attn_fwd(q, k, v, nb, name, comm=None):
    T = q.shape[0]
    nbt = T // BLK
    lo, hi = slice(0, BLK), slice(BLK, 2 * BLK)

    def block(qv, kc, kp, vc, vp, has_prev, hm):
        m_cur, m_prev = _band_masks()
        m_prev = jnp.logical_and(m_prev, has_prev)
        qb = _expand_heads(qv, hm)
        s_c = jnp.where(m_cur, _dot_nt(qb, kc) * SCALE, NEG)
        s_p = jnp.where(m_prev, _dot_nt(qb, kp) * SCALE, NEG)
        mx = jnp.maximum(jnp.max(s_c, axis=-1, keepdims=True), jnp.max(s_p, axis=-1, keepdims=True))
        e_c = jnp.exp(s_c - mx)
        e_p = jnp.exp(s_p - mx)
        l = jnp.sum(e_c, axis=-1, keepdims=True) + jnp.sum(e_p, axis=-1, keepdims=True)
        inv = 1.0 / l
        ob = _dot((e_c * inv).astype(BF16), vc) + _dot((e_p * inv).astype(BF16), vp)
        return _collapse_heads(ob, hm), _collapse_heads(jnp.broadcast_to(mx + jnp.log(l), (NH * BLK, GA)), hm)

    def body(q_ref, k_ref, kp_ref, v_ref, vp_ref, o_ref, lse_ref):
        b0 = 2 * pl.program_id(0)
        hm = _head_masks()
        o_ref[lo, :], lse_ref[lo, :] = block(q_ref[lo, :], k_ref[lo, :], kp_ref[...], v_ref[lo, :], vp_ref[...],
                                             (b0 & (nb - 1)) != 0, hm)
        o_ref[hi, :], lse_ref[hi, :] = block(q_ref[hi, :], k_ref[hi, :], k_ref[lo, :], v_ref[hi, :], v_ref[lo, :],
                                             ((b0 + 1) & (nb - 1)) != 0, hm)

    cur, prev, _ = _attn_specs(nbt)
    return _call(body, name, (nbt // 2,), [cur, cur, prev, cur, prev], [cur, cur],
                 [_sds((T, GA), F32), _sds((T, GA), F32)], comm=comm)(q, k, k, v, v)


def _attn_bwd(q, k, v, do, lse, e, nb, name, comm=None):
    T = q.shape[0]
    nbt = T // BLK

    lo, hi = slice(0, BLK), slice(BLK, 2 * BLK)

    def block(qv, kc, vc, dov, lsev, ev, kp, vp, qn, don, lsen, en, has_prev, has_next, hm):
        m_cur, m_band = _band_masks()
        m_prev = jnp.logical_and(m_band, has_prev)
        m_next = jnp.logical_and(m_band, has_next)
        qb, dob = _expand_heads(qv, hm), _expand_heads(dov, hm)
        lse_r, e_r = _head_rows(lsev), _head_rows(ev)
        p_c = jnp.where(m_cur, jnp.exp(_dot_nt(qb, kc) * SCALE - lse_r), 0.0)
        p_p = jnp.where(m_prev, jnp.exp(_dot_nt(qb, kp) * SCALE - lse_r), 0.0)
        ds_c = (p_c * (_dot_nt(dob, vc) + e_r)).astype(BF16)
        ds_p = (p_p * (_dot_nt(dob, vp) + e_r)).astype(BF16)
        dq = _collapse_heads((_dot(ds_c, kc) + _dot(ds_p, kp)) * SCALE, hm)
        qnb, donb = _expand_heads(qn, hm), _expand_heads(don, hm)
        p_n = jnp.where(m_next, jnp.exp(_dot_nt(qnb, kc) * SCALE - _head_rows(lsen)), 0.0)
        ds_n = (p_n * (_dot_nt(donb, vc) + _head_rows(en))).astype(BF16)
        dk = (_dot_tn(ds_c, qb) + _dot_tn(ds_n, qnb)) * SCALE
        dv = (_dot_tn(p_c.astype(BF16), dob) + _dot_tn(p_n.astype(BF16), donb)).astype(BF16)
        return dq, dk, dv

    def body(q_ref, k_ref, v_ref, do_ref, lse_ref, e_ref, kp_ref, vp_ref, qn_ref, don_ref, lsen_ref, en_ref,
             dq_ref, dk_ref, dv_ref):
        b0 = 2 * pl.program_id(0)
        hm = _head_masks()
        dq_ref[lo, :], dk_ref[lo, :], dv_ref[lo, :] = block(
            q_ref[lo, :], k_ref[lo, :], v_ref[lo, :], do_ref[lo, :], lse_ref[lo, :], e_ref[lo, :], kp_ref[...], vp_ref[...],
            q_ref[hi, :], do_ref[hi, :], lse_ref[hi, :], e_ref[hi, :],
            (b0 & (nb - 1)) != 0, ((b0 + 1) & (nb - 1)) != 0, hm)
        dq_ref[hi, :], dk_ref[hi, :], dv_ref[hi, :] = block(
            q_ref[hi, :], k_ref[hi, :], v_ref[hi, :], do_ref[hi, :], lse_ref[hi, :], e_ref[hi, :], k_ref[lo, :], v_ref[lo, :],
            qn_ref[...], don_ref[...], lsen_ref[...], en_ref[...],
            ((b0 + 1) & (nb - 1)) != 0, ((b0 + 2) & (nb - 1)) != 0, hm)

    cur, prev, nxt = _attn_specs(nbt)
    return _call(body, name, (nbt // 2,), [cur] * 6 + [prev, prev] + [nxt] * 4, [cur, cur, cur],
                 [_sds((T, GA), F32), _sds((T, GA), F32), _sds((T, GA), BF16)],
                 comm=comm)(q, k, v, do, lse, e, k, v, q, do, lse, e)


def _flat(t):
    return t.reshape(t.shape[0] * t.shape[1], t.shape[2])


def _by_residue(t, dil):
    return t.reshape(dil, t.shape[0] // dil, t.shape[1])


def _pool_consts(shape, row0):
    lane = lax.broadcasted_iota(jnp.int32, shape, 1)
    t = lax.broadcasted_iota(jnp.int32, shape, 0) + row0
    grp = lane // (PW // len(POOL_WINDOWS))
    win = jnp.where(grp == 0, POOL_WINDOWS[0], jnp.where(grp == 1, POOL_WINDOWS[1],
                    jnp.where(grp == 2, POOL_WINDOWS[2], POOL_WINDOWS[3])))
    cnt = jnp.minimum(t + 1, win).astype(F32)
    return grp, cnt


def _window_sums(ext_ref, base, step, tm):
    outs, run = [], None
    for j in range(POOL_WINDOWS[-1]):
        sl = ext_ref[pl.ds(base + step * j, tm), :]
        run = sl if run is None else run + sl
        if j + 1 in POOL_WINDOWS:
            outs.append(run)
    return outs


def _select_group(grp, vals):
    return jnp.where(grp == 0, vals[0], jnp.where(grp == 1, vals[1], jnp.where(grp == 2, vals[2], vals[3])))


def _pool_d(pc_ref, pp_ref, ext_ref, i, tm):
    ext_ref[0:HALO, :] = jnp.where(i > 0, pp_ref[tm - HALO:tm, :], 0.0)
    ext_ref[HALO:HALO + tm, :] = pc_ref[...]
    grp, cnt = _pool_consts((tm, PW), i * tm)
    sums = _window_sums(ext_ref, HALO, -1, tm)
    return _select_group(grp, sums) / cnt - pc_ref[...]


def _group_weights(ls):
    mx = jnp.maximum(jnp.maximum(ls[0], ls[1]), ls[2])
    es = [jnp.exp(l - mx) for l in ls]
    inv = 1.0 / (es[0] + es[1] + es[2])
    return [e * inv for e in es]


def _mix_merge(h, vec, p, os, lses, gates, wp_bd, pscale, wpb, wab, wout):
    T = h.shape[0]

    def body(h_ref, vec_ref, pc_ref, pp_ref, o0, o1, o2, l0, l1, l2, gates_ref, wp_ref, ps_ref, wpb_ref, wab_ref, wout_ref,
             ho_ref, yp_ref, ya_ref, mg_ref, mo_ref, d_ref, ext_ref, scr_ref):
        i = pl.program_id(0)
        gt = vec_ref[3:4, :]
        d = _pool_d(pc_ref, pp_ref, ext_ref, i, TM).astype(BF16)
        d_ref[...] = d
        ypool = (_dot(d, wp_ref[...]) * ps_ref[0:1, :]).astype(BF16)
        yp_ref[...] = ypool
        w = _group_weights([_from_residues(r, scr_ref, dl) for r, dl in zip((l0, l1, l2), DIL)])
        yattn = None
        for wg, o_ref, dl in zip(w, (o0, o1, o2), DIL):
            part = wg * _from_residues(o_ref, scr_ref, dl)
            yattn = part if yattn is None else yattn + part
        yattn = yattn.astype(BF16)
        ya_ref[...] = yattn
        merged = (gates_ref[:, 0:D].astype(F32) * _dot(ypool, wpb_ref[...])
                  + gates_ref[:, D:GW].astype(F32) * _dot(yattn, wab_ref[...])).astype(BF16)
        mg_ref[...] = merged
        mo = _dot(merged, wout_ref[...])
        mo_ref[...] = mo.astype(BF16)
        ho_ref[...] = h_ref[...] + gt * mo

    prev = pl.BlockSpec((TM, PW), lambda i: (jnp.maximum(i - 1, 0), 0))
    return _call(
        body, "mix_merge", (T // TM,),
        [_rows(TM, D), _const((8, D)), _rows(TM, PW), prev] + [_rm_spec(dl) for dl in DIL] * 2 + [_rows(TM, GW), _const((PW, PW)),
         _const((8, PW)), _const((PW, D)), _const((GA, D)), _const((D, D))],
        [_rows(TM, D), _rows(TM, PW), _rows(TM, GA), _rows(TM, D), _rows(TM, D), _rows(TM, PW)],
        [_sds((T, D), F32), _sds((T, PW), BF16), _sds((T, GA), BF16), _sds((T, D), BF16), _sds((T, D), BF16), _sds((T, PW), BF16)],
        scratch=[pltpu.VMEM((TM + HALO, PW), F32), pltpu.VMEM((GA // LANES, TM, LANES), F32)],
        vmem=VMEM_BIG,
    )(h, vec, p, p, *os, *lses, gates, wp_bd, pscale, wpb, wab, wout)[0]


def _mix_bwd_a(dh, vec, mixout, gates, ypool, yattn, dpool, os, lses, wp_bd, pscale, wpb, wab, wout, ones_bd, comm=None):
    T = dh.shape[0]

    def body(dh_ref, vec_ref, mo_ref, gates_ref, yp_ref, ya_ref, d_ref, o0, o1, o2, l0, l1, l2,
             wp_ref, ps_ref, wpb_ref, wab_ref, wout_ref, ones_ref,
             dmo_ref, dp_ref, da_ref, dgates_ref, do0, do1, do2, e0, e1, e2, dd_ref, dyp_ref, acc_ref, acc2_ref, scr_ref):
        _zero_first(acc_ref)
        _zero_first(acc2_ref)
        gt = vec_ref[3:4, :]
        dho = dh_ref[...]
        acc_ref[3:4, :] += _colsum(dho * mo_ref[...].astype(F32))
        dmo = (gt * dho).astype(BF16)
        dmo_ref[...] = dmo
        dmerged = _dot_nt(dmo, wout_ref[...])
        gp = gates_ref[:, 0:D].astype(F32)
        ga = gates_ref[:, D:GW].astype(F32)
        bp = _dot(yp_ref[...], wpb_ref[...])
        ba = _dot(ya_ref[...], wab_ref[...])
        dgates_ref[:, 0:D] = (dmerged * bp * gp * (1.0 - gp)).astype(BF16)
        dgates_ref[:, D:GW] = (dmerged * ba * ga * (1.0 - ga)).astype(BF16)
        dbp = (dmerged * gp).astype(BF16)
        dba = (dmerged * ga).astype(BF16)
        dp_ref[...] = dbp
        da_ref[...] = dba
        dypool = _dot_nt(dbp, wpb_ref[...])
        ypre = _dot(d_ref[...], wp_ref[...])
        acc2_ref[0:1, :] += _colsum(dypool * ypre)
        dyp = (dypool * ps_ref[0:1, :]).astype(BF16)
        dyp_ref[...] = dyp
        dd_ref[...] = _dot_nt(dyp, wp_ref[...])
        dya = _dot_nt(dba, wab_ref[...])
        w = _group_weights([_from_residues(r, scr_ref, dl) for r, dl in zip((l0, l1, l2), DIL)])
        ya = None
        for wg, o_ref, dl in zip(w, (o0, o1, o2), DIL):
            part = wg * _from_residues(o_ref, scr_ref, dl)
            ya = part if ya is None else ya + part
        prod = dya * ya
        hi = prod.astype(BF16)
        lo = (prod - hi.astype(F32)).astype(BF16)
        tot = _dot(hi, ones_ref[...]) + _dot(lo, ones_ref[...])
        for wg, do_ref, e_ref, dl in zip(w, (do0, do1, do2), (e0, e1, e2), DIL):
            _to_residues(wg * dya, do_ref, scr_ref, dl)
            _to_residues(-wg * tot, e_ref, scr_ref, dl)

    return _call(
        body, "mix_bwd_a", (T // TM,),
        [_rows(TM, D), _const((8, D)), _rows(TM, D), _rows(TM, GW), _rows(TM, PW), _rows(TM, GA), _rows(TM, PW)]
        + [_rm_spec(dl) for dl in DIL] * 2
        + [_const((PW, PW)), _const((8, PW)), _const((PW, D)), _const((GA, D)), _const((D, D)), _const((GA, GA))],
        [_rows(TM, D)] * 3 + [_rows(TM, GW)] + [_rm_spec(dl) for dl in DIL] * 2
        + [_rows(TM, PW), _rows(TM, PW), _const((8, D)), _const((8, PW))],
        [_sds((T, D), BF16)] * 3 + [_sds((T, GW), BF16)] + [_sds((dl, T // dl, GA), BF16) for dl in DIL]
        + [_sds((dl, T // dl, GA), F32) for dl in DIL]
        + [_sds((T, PW), F32), _sds((T, PW), BF16), _sds((8, D), F32), _sds((8, PW), F32)],
        scratch=[pltpu.VMEM((GA // LANES, TM, LANES), F32)],
        vmem=VMEM_BIG, comm=comm,
    )(dh, vec, mixout, gates, ypool, yattn, dpool, *os, *lses, wp_bd, pscale, wpb, wab, wout, ones_bd)


def _mix_bwd_b(dh, h, vec, dd, dqs, dks, dvs, dgates, cos, sin, win):
    T = h.shape[0]
    nt = T // TM

    def body(dh_ref, h_ref, vec_ref, ddc_ref, ddn_ref, *rest):
        qk_refs, dv_refs = rest[:2 * NG], rest[2 * NG:3 * NG]
        dgates_ref, cos_ref, sin_ref, win_hbm, dhi_ref, dproj_ref, acc_ref, win_v, ext_ref, scr_ref, sems = rest[3 * NG:]
        i = pl.program_id(0)
        _load_once([(win_hbm, win_v)], sems)
        _zero_first(acc_ref)
        g, sh, sc = vec_ref[0:1, :], vec_ref[1:2, :], vec_ref[2:3, :]
        grp, cnt = _pool_consts((TM, PW), i * TM)
        _, cnt_n = _pool_consts((HALO, PW), (i + 1) * TM)
        ext_ref[0:TM, :] = ddc_ref[...] / cnt
        ext_ref[TM:TM + HALO, :] = jnp.where(i < nt - 1, ddn_ref[0:HALO, :] / cnt_n, 0.0)
        dp = _select_group(grp, _window_sums(ext_ref, 0, 1, TM)) - ddc_ref[...]
        dproj_ref[:, 0:PW] = dp.astype(BF16)
        cos_t, sin_t = cos_ref[...], sin_ref[...]
        for j in range(2 * NG):
            col = PW + j * GA
            dt = _from_residues(qk_refs[j], scr_ref, DIL[j % NG])
            dproj_ref[:, col:col + GA] = _rope_bwd(dt, cos_t, sin_t).astype(BF16)
        for j in range(NG):
            col = PW + (2 * NG + j) * GA
            dproj_ref[:, col:col + GA] = _from_residues(dv_refs[j], scr_ref, DIL[j]).astype(BF16)
        dproj_ref[:, PW + 3 * NG * GA:INW] = dgates_ref[...]
        du = None
        for j in range(INW // 512):
            part = _dot_nt(dproj_ref[:, j * 512:(j + 1) * 512], win_v[:, j * 512:(j + 1) * 512])
            du = part if du is None else du + part
        xh, r, n, _ = _norm_fwd(h_ref[...], g, sh, sc)
        dhn, dsh, dsc, dg = _norm_bwd(du, xh, r, n, g, sc)
        dhi_ref[...] = dh_ref[...] + dhn
        acc_ref[0:1, :] += dsh
        acc_ref[1:2, :] += dsc
        acc_ref[2:3, :] += dg

    nxt = pl.BlockSpec((TM, PW), lambda i: (jnp.minimum(i + 1, nt - 1), 0))
    return _call(
        body, "mix_bwd_b", (nt,),
        [_rows(TM, D), _rows(TM, D), _const((8, D)), _rows(TM, PW), nxt] + [_rm_spec(dl) for dl in DIL] * 3
        + [_rows(TM, GW), _rows(TM, 128), _rows(TM, 128), ANY],
        [_rows(TM, D), _rows(TM, INW), _const((8, D))],
        [_sds((T, D), F32), _sds((T, INW), BF16), _sds((8, D), F32)],
        scratch=[pltpu.VMEM((D, INW), BF16), pltpu.VMEM((TM + HALO, PW), F32), pltpu.VMEM((GA // LANES, TM, LANES), F32),
                 pltpu.SemaphoreType.DMA((1,))],
        vmem=VMEM_BIG,
    )(dh, h, vec, dd, dd, *dqs, *dks, *dvs, dgates, cos, sin, win)[0]


def _ada_fwd(c_all, w_shard, b_shard):
    n = w_shard.shape[1]

    def body(c_ref, w_ref, b_ref, o_ref):
        cv = c_ref[...]
        cond = (cv * jax.nn.sigmoid(cv)).astype(BF16)
        o_ref[...] = _dot(cond, w_ref[...].astype(BF16)) + b_ref[...]

    tn = n // 3
    return pl.pallas_call(
        body, name="ada_fwd", grid=(3,),
        in_specs=[pl.BlockSpec((8, D), lambda j: (0, 0)), pl.BlockSpec((D, tn), lambda j: (0, j)), pl.BlockSpec((1, tn), lambda j: (0, j))],
        out_specs=pl.BlockSpec((8, tn), lambda j: (0, j)), out_shape=_sds((8, n), F32),
        compiler_params=pltpu.CompilerParams(dimension_semantics=("arbitrary",)),
    )(c_all, w_shard, b_shard)


def _ada_bwd(c_all, dmod_shard):
    n = dmod_shard.shape[1]

    def body(c_ref, d_ref, o_ref):
        cv = c_ref[...]
        cond = (cv * jax.nn.sigmoid(cv)).astype(BF16)
        o_ref[...] = _dot_tn(cond, d_ref[...].astype(BF16))

    tn = n // 3
    return pl.pallas_call(
        body, name="ada_bwd", grid=(3,),
        in_specs=[pl.BlockSpec((8, D), lambda j: (0, 0)), pl.BlockSpec((8, tn), lambda j: (0, j))],
        out_specs=pl.BlockSpec((D, tn), lambda j: (0, j)), out_shape=_sds((D, n), F32),
        compiler_params=pltpu.CompilerParams(dimension_semantics=("arbitrary",)),
    )(c_all, dmod_shard)


def _adam_math(w, g, m, v):
    m2 = B1 * m + (1.0 - B1) * g
    v2 = B2 * v + (1.0 - B2) * (g * g)
    m_hat = m2 / (1.0 - B1 ** STEP)
    v_hat = v2 / (1.0 - B2 ** STEP)
    delta = -LR * (m_hat / (jnp.sqrt(v_hat) + AEPS) + WD * w)
    return delta, m2, v2


def _adam(w, m, v, parts, name):
    R, C = w.shape
    tr = R
    for cand in (128, 64, 32, 16, 8):
        if R % cand == 0:
            tr = cand
            break
    np_ = len(parts)

    def body(w_ref, m_ref, v_ref, *rest):
        p_refs, (g_ref, d_ref, m2_ref, v2_ref) = rest[:np_], rest[np_:]
        g = p_refs[0][...]
        for pr in p_refs[1:]:
            g = g + pr[...]
        delta, m2, v2 = _adam_math(w_ref[...], g, m_ref[...], v_ref[...])
        g_ref[...] = g
        d_ref[...] = delta
        m2_ref[...] = m2
        v2_ref[...] = v2

    spec = pl.BlockSpec((tr, C), lambda i: (i, 0))
    return pl.pallas_call(
        body, name=name, grid=(R // tr,), in_specs=[spec] * (3 + np_), out_specs=[spec] * 4,
        out_shape=[_sds((R, C), F32)] * 4,
        compiler_params=pltpu.CompilerParams(dimension_semantics=("arbitrary",), vmem_limit_bytes=VMEM_BIG),
    )(w, m, v, *parts)


def _adam_small(w, m, v, gathered):
    P = w.shape[1]

    def body(w_ref, m_ref, v_ref, ga_ref, g_ref, d_ref, m2_ref, v2_ref):
        g = ga_ref[0]
        for dev in range(1, 8):
            g = g + ga_ref[dev]
        delta, m2, v2 = _adam_math(w_ref[...], g, m_ref[...], v_ref[...])
        g_ref[...] = g
        d_ref[...] = delta
        m2_ref[...] = m2
        v2_ref[...] = v2

    return pl.pallas_call(body, name="adam_small", out_shape=[_sds((1, P), F32)] * 4)(w, m, v, gathered)


def _sum4(blocks, name):
    _, R, C = blocks.shape
    tr = R
    for cand in (256, 128, 64, 32, 16):
        if R % cand == 0:
            tr = cand
            break

    def body(r_ref, out_ref):
        out_ref[...] = ((r_ref[0].astype(F32) + r_ref[1].astype(F32)) + r_ref[2].astype(F32)) + r_ref[3].astype(F32)

    return pl.pallas_call(
        body, name=name, grid=(R // tr,),
        in_specs=[pl.BlockSpec((4, tr, C), lambda i: (0, i, 0))],
        out_specs=pl.BlockSpec((tr, C), lambda i: (i, 0)), out_shape=_sds((R, C), F32),
        compiler_params=pltpu.CompilerParams(dimension_semantics=("arbitrary",)),
    )(blocks)


def _place():
    return lax.axis_index("x"), lax.axis_index("y"), lax.axis_index("c")


def _gather_small(v):
    R, P = v.shape

    def body(v_ref, out_ref, send_sems, recv_sems):
        x, y, c = _place()
        me = 4 * x + 2 * y + c
        out_ref[me] = v_ref[...]
        copies = []
        for m in range(1, 8):
            peer = (x ^ (m >> 2), y ^ ((m >> 1) & 1), c ^ (m & 1))
            copies.append(pltpu.make_async_remote_copy(
                src_ref=v_ref, dst_ref=out_ref.at[me], send_sem=send_sems.at[m - 1], recv_sem=recv_sems.at[m - 1],
                device_id=peer, device_id_type=MESH))
        for cp in copies:
            cp.start()
        for m in range(1, 8):
            src = 4 * (x ^ (m >> 2)) + 2 * (y ^ ((m >> 1) & 1)) + (c ^ (m & 1))
            pltpu.make_async_remote_copy(
                src_ref=v_ref, dst_ref=out_ref.at[src], send_sem=send_sems.at[m - 1], recv_sem=recv_sems.at[m - 1],
                device_id=(x, y, c), device_id_type=MESH).wait_recv()
        for cp in copies:
            cp.wait_send()

    vm = pl.BlockSpec(memory_space=pltpu.VMEM)
    return pl.pallas_call(
        body, name="gather_small", in_specs=[vm], out_specs=vm, out_shape=_sds((8, R, P), F32),
        scratch_shapes=[pltpu.SemaphoreType.DMA((7,)), pltpu.SemaphoreType.DMA((7,))],
    )(v)


def _chip_peer(x, y, c, m):
    return (x ^ (m >> 1), y ^ (m & 1), c)


def _shard_ref(ref, axis, k, n):
    start = pl.multiple_of(k * n, 128 if axis == 1 else 16)
    return ref.at[:, pl.ds(start, n)] if axis == 1 else ref.at[pl.ds(start, n), :]


def _half_rows(ref, axis, k, n, hc):
    if axis == 1:
        half = ref.shape[0] // 2
        return ref.at[pl.ds(pl.multiple_of(hc * half, 16), half), pl.ds(pl.multiple_of(k * n, 128), n)]
    half = n // 2
    return ref.at[pl.ds(pl.multiple_of(k * n + hc * half, 16), half), :]


class _GatherPlan:
    def __init__(self, shards, axes):
        self.inputs, self.axes, nw = list(shards), list(axes), len(shards)
        self.out_shapes = [_sds((s.shape[0] * (4 if ax == 0 else 1), s.shape[1] * (4 if ax == 1 else 1)), BF16)
                           for s, ax in zip(shards, axes)]
        self.sem_shapes = [pltpu.SemaphoreType.DMA((nw,))] + [pltpu.SemaphoreType.DMA((nw, 3))] * 4

    def _copies(self, ins, outs, sems):
        local_sems, send_sems, recv_sems, pass_sems, got_sems = sems
        x, y, c = _place()
        k = 2 * x + y
        local, sends, arrivals, passes, handed = [], [], [], [], []
        for j, ax in enumerate(self.axes):
            n = ins[j].shape[ax]
            half = ins[j].shape[0] // 2
            local.append(pltpu.make_async_copy(ins[j], _shard_ref(outs[j], ax, k, n), local_sems.at[j]))
            my_half = ins[j].at[pl.ds(pl.multiple_of(c * half, 16), half), :]
            for m in range(1, 4):
                sends.append(pltpu.make_async_remote_copy(
                    src_ref=my_half, dst_ref=_half_rows(outs[j], ax, k, n, c), send_sem=send_sems.at[j, m - 1],
                    recv_sem=recv_sems.at[j, m - 1], device_id=_chip_peer(x, y, c, m), device_id_type=MESH))
                theirs = _half_rows(outs[j], ax, k ^ m, n, c)
                arrivals.append(pltpu.make_async_remote_copy(
                    src_ref=my_half, dst_ref=theirs, send_sem=send_sems.at[j, m - 1], recv_sem=recv_sems.at[j, m - 1],
                    device_id=(x, y, c), device_id_type=MESH))
                passes.append(pltpu.make_async_remote_copy(
                    src_ref=theirs, dst_ref=theirs, send_sem=pass_sems.at[j, m - 1], recv_sem=got_sems.at[j, m - 1],
                    device_id=(x, y, 1 - c), device_id_type=MESH))
                other = _half_rows(outs[j], ax, k ^ m, n, 1 - c)
                handed.append(pltpu.make_async_remote_copy(
                    src_ref=other, dst_ref=other, send_sem=pass_sems.at[j, m - 1], recv_sem=got_sems.at[j, m - 1],
                    device_id=(x, y, c), device_id_type=MESH))
        return local, sends, arrivals, passes, handed

    def start(self, ins, outs, sems):
        local, sends, _, _, _ = self._copies(ins, outs, sems)
        for cp in local + sends:
            cp.start()

    def wait(self, ins, outs, sems):
        local, sends, arrivals, passes, handed = self._copies(ins, outs, sems)
        for arrived, onward in zip(arrivals, passes):
            arrived.wait_recv()
            onward.start()
        for cp in handed:
            cp.wait_recv()
        for cp in sends + passes:
            cp.wait_send()
        for cp in local:
            cp.wait()


class _ScatterPlan:
    def __init__(self, grads, axes):
        self.inputs, self.axes, nw = list(grads), list(axes), len(grads)
        self.shard_shapes = [(g.shape[0] // (4 if ax == 0 else 1), g.shape[1] // (4 if ax == 1 else 1))
                             for g, ax in zip(grads, axes)]
        self.out_shapes = [_sds((4,) + s, BF16) for s in self.shard_shapes]
        self.sem_shapes = [pltpu.SemaphoreType.DMA((nw,)), pltpu.SemaphoreType.DMA((nw, 3)), pltpu.SemaphoreType.DMA((nw, 3))]

    def _copies(self, ins, outs, sems):
        local_sems, send_sems, recv_sems = sems
        x, y, c = _place()
        k = 2 * x + y
        local, remote, arrivals = [], [], []
        for j, ax in enumerate(self.axes):
            n = self.shard_shapes[j][ax]
            local.append(pltpu.make_async_copy(_shard_ref(ins[j], ax, k, n), outs[j].at[0], local_sems.at[j]))
            for m in range(1, 4):
                remote.append(pltpu.make_async_remote_copy(
                    src_ref=_shard_ref(ins[j], ax, k ^ m, n), dst_ref=outs[j].at[m],
                    send_sem=send_sems.at[j, m - 1], recv_sem=recv_sems.at[j, m - 1],
                    device_id=_chip_peer(x, y, c, m), device_id_type=MESH))
                arrivals.append(pltpu.make_async_remote_copy(
                    src_ref=_shard_ref(ins[j], ax, k, n), dst_ref=outs[j].at[m],
                    send_sem=send_sems.at[j, m - 1], recv_sem=recv_sems.at[j, m - 1],
                    device_id=(x, y, c), device_id_type=MESH))
        return local, remote, arrivals

    def start(self, ins, outs, sems):
        local, remote, _ = self._copies(ins, outs, sems)
        for cp in local + remote:
            cp.start()

    def wait(self, ins, outs, sems):
        local, remote, arrivals = self._copies(ins, outs, sems)
        for cp in arrivals:
            cp.wait_recv()
        for cp in remote:
            cp.wait_send()
        for cp in local:
            cp.wait()


def _run_plan(plan, name):
    nc = len(plan.inputs)

    def body(*refs):
        ins, outs, sems = refs[:nc], refs[nc:2 * nc], refs[2 * nc:]
        plan.start(ins, outs, sems)
        plan.wait(ins, outs, sems)

    return pl.pallas_call(body, name=name, in_specs=[ANY] * nc, out_specs=[ANY] * nc, out_shape=list(plan.out_shapes),
                          scratch_shapes=list(plan.sem_shapes))(*plan.inputs)


def _swap_sibling(parts):
    nw = len(parts)

    def body(*refs):
        ins, outs = refs[:nw], refs[nw:2 * nw]
        send_sems, recv_sems = refs[2 * nw:]
        x, y, c = _place()
        copies = [pltpu.make_async_remote_copy(
            src_ref=ins[j], dst_ref=outs[j], send_sem=send_sems.at[j], recv_sem=recv_sems.at[j],
            device_id=(x, y, 1 - c), device_id_type=MESH) for j in range(nw)]
        for cp in copies:
            cp.start()
        for cp in copies:
            cp.wait()

    return pl.pallas_call(
        body, name="swap_sibling", in_specs=[ANY] * nw, out_specs=[ANY] * nw,
        out_shape=[_sds(p.shape, p.dtype) for p in parts],
        scratch_shapes=[pltpu.SemaphoreType.DMA((nw,)), pltpu.SemaphoreType.DMA((nw,))],
    )(*parts)


BIG = ("w_ffn1_in", "w_ffn1_out", "w_in", "w_pool_branch", "w_attn_branch", "w_out", "w_ffn2_in", "w_ffn2_out")
BIG_AXIS = {"w_ffn1_in": 1, "w_ffn1_out": 0, "w_in": 1, "w_pool_branch": 1, "w_attn_branch": 1, "w_out": 0,
            "w_ffn2_in": 1, "w_ffn2_out": 0}


class _Sharded:
    fused_scatter = True

    def __init__(self, shards):
        self.shards, self.full, self.recv = shards, {}, {}

    def gather_plan(self, names):
        return _GatherPlan([self.shards[n] for n in names], [BIG_AXIS[n.split("/")[0]] for n in names])

    def gather_now(self, names):
        self.gathered(names, _run_plan(self.gather_plan(names), "gather_" + names[0]))

    def gathered(self, names, outs):
        self.full.update(zip(names, outs))

    def scatter_plan(self, names, grads):
        return _ScatterPlan([grads[n] for n in names], [BIG_AXIS[n] for n in names])

    def scatter_now(self, names, grads):
        self.scattered(names, _run_plan(self.scatter_plan(names, grads), "scatter_" + names[0]))

    def scattered(self, names, outs):
        self.recv.update(zip(names, outs))


class _Whole:
    fused_scatter = False

    def __init__(self, full):
        self.full, self.recv = dict(full), {}

    def gather_plan(self, names):
        return None

    def gather_now(self, names):
        pass

    def gathered(self, names, outs):
        pass

    def scatter_plan(self, names, grads):
        return None

    def scatter_now(self, names, grads):
        pass

    def scattered(self, names, outs):
        pass


def _vec(rows):
    pad = [jnp.zeros((1, D), F32)] * (8 - len(rows))
    return jnp.concatenate([r.reshape(1, D) for r in rows] + pad, axis=0)


def _block_diag(w_pool):
    n, c = w_pool.shape[0], w_pool.shape[1]
    eye = jnp.eye(n, dtype=w_pool.dtype)
    return (eye[:, None, :, None] * w_pool[:, :, None, :]).reshape(n * c, n * c)


def _example_step(x, tgt, positions, mod, gains, w_pool, pool_scale, ws):
    T = x.shape[0]
    assert (T // BLK // DIL[-1]) & (T // BLK // DIL[-1] - 1) == 0, "blocks per sequence must be a power of two"
    sh1, sc1, gt1, sh2, sc2, gt2, sh3, sc3, gt3 = [mod[j * D:(j + 1) * D] for j in range(NMOD)]
    g1, g2, g3, gf = gains
    vec1, vec2, vec3 = _vec([g1, sh1, sc1, gt1]), _vec([g2, sh2, sc2, gt2]), _vec([g3, sh3, sc3, gt3])
    inv_freq = 10000.0 ** (-jnp.arange(0, HD, 2, dtype=F32) / HD)
    ang = positions.astype(F32)[:, None] * inv_freq
    cos = jnp.tile(jnp.cos(ang), (1, 4))
    sin = jnp.tile(jnp.concatenate([-jnp.sin(ang), jnp.sin(ang)], axis=1), (1, 2))
    wp_bd = _block_diag(w_pool).astype(BF16)
    ones_bd = _block_diag(jnp.ones((NH, HD, HD), F32)).astype(BF16)
    ps = jnp.concatenate([pool_scale.reshape(1, PW), jnp.zeros((7, PW), F32)], axis=0)
    wb = ws.full

    ws.gather_now(["w_ffn1_in", "w_ffn1_out"])
    mixw = ["w_in", "w_pool_branch", "w_attn_branch", "w_out"]
    (h1, u1, a1, b1, f1), got = _ffn_fwd(x, vec1, [wb["w_ffn1_in"]], wb["w_ffn1_out"], "ffn1_fwd", ws.gather_plan(mixw))
    ws.gathered(mixw, got)
    (u2, p, qs, ks, vs, gates), got = _mix_proj(h1, vec2, wb["w_in"], cos, sin, ws.gather_plan(["w_ffn2_in/0"]))
    ws.gathered(["w_ffn2_in/0"], got)
    qs, ks, vs = [_flat(t) for t in qs], [_flat(t) for t in ks], [_flat(t) for t in vs]
    nbs = [T // d // BLK for d in DIL]
    os, lses = [], []
    for gi, riders in enumerate((["w_ffn2_out"], ["w_ffn2_in/1"], None)):
        (o, lse), got = _attn_fwd(qs[gi], ks[gi], vs[gi], nbs[gi], f"attn_fwd{gi}", riders and ws.gather_plan(riders))
        ws.gathered(riders or [], got)
        os.append(o)
        lses.append(lse)
    win3 = [wb["w_ffn2_in/0"], wb["w_ffn2_in/1"]] if "w_ffn2_in/0" in wb else [wb["w_ffn2_in"]]
    os_r = [_by_residue(t, d) for t, d in zip(os, DIL)]
    lses_r = [_by_residue(t, d) for t, d in zip(lses, DIL)]
    h2, ypool, yattn, merged, mixout, dpool = _mix_merge(
        h1, vec2, p, os_r, lses_r, gates, wp_bd, ps, wb["w_pool_branch"], wb["w_attn_branch"], wb["w_out"])
    (h3, u3, a3, b3, f3), _ = _ffn_fwd(h2, vec3, win3, wb["w_ffn2_out"], "ffn2_fwd")
    dh3, lacc = _final_loss(h3, tgt, _vec([gf]))
    loss = 0.5 * jnp.sum(lacc[0]) / D

    grads = {}

    def wgrad_cols(name, xx, yy, riders):
        plan = ws.scatter_plan(riders, grads) if riders else None
        if ws.fused_scatter:
            blocks, got = _wgrad_scatter(xx, yy, "wg_" + name, 1024, comm=plan)
            ws.scattered([name], [blocks])
        else:
            grads[name], got = _wgrad(xx, yy, "wg_" + name, D, 512, 1024, comm=plan)
        ws.scattered(riders, got)

    (dh2, dab3, s3, df3, acc3), _ = _ffn_bwd(dh3, h2, a3, b3, f3, vec3, win3, wb["w_ffn2_out"], "ffn2_bwd")
    grads["w_ffn2_out"], _ = _wgrad(s3, df3, "wg_ffn2_out", FC, 512, 1024)
    wgrad_cols("w_ffn2_in", u3, dab3, ["w_ffn2_out"])
    (dmo, dbp, dba, dgates, do0, do1, do2, e0, e1, e2, dd, dyp, acc2a, accps), _ = _mix_bwd_a(
        dh2, vec2, mixout, gates, ypool, yattn, dpool, os_r, lses_r, wp_bd, ps,
        wb["w_pool_branch"], wb["w_attn_branch"], wb["w_out"], ones_bd)
    grads["w_out"], _ = _wgrad(merged, dmo, "wg_out", D, 512, 1024)
    grads["w_pool_branch"], _ = _wgrad(ypool, dbp, "wg_pool_branch", PW, 512, 1024)
    grads["w_attn_branch"], _ = _wgrad(yattn, dba, "wg_attn_branch", GA, 512, 1024)
    gwp, _ = _wgrad(dpool, dyp, "wg_pool", PW, PW, 1024, out_dtype=F32)
    n = len(POOL_WINDOWS)
    c = PW // n
    grad_w_pool = jnp.stack([gwp[j * c:(j + 1) * c, j * c:(j + 1) * c] for j in range(n)], axis=0)
    small3 = ["w_out", "w_pool_branch", "w_attn_branch"]
    dqs, dks, dvs = [], [], []
    for gi, (do, e) in enumerate(((do0, e0), (do1, e1), (do2, e2))):
        plan = ws.scatter_plan(small3, grads) if gi == 0 else None
        (dq, dk, dv), got = _attn_bwd(qs[gi], ks[gi], vs[gi], _flat(do), lses[gi], _flat(e), nbs[gi], f"attn_bwd{gi}", plan)
        if gi == 0:
            ws.scattered(small3, got)
        dqs.append(_by_residue(dq, DIL[gi]))
        dks.append(_by_residue(dk, DIL[gi]))
        dvs.append(_by_residue(dv, DIL[gi]))
    dh1, dproj, acc2b = _mix_bwd_b(dh2, h1, vec2, dd, dqs, dks, dvs, dgates, cos, sin, wb["w_in"])
    wgrad_cols("w_in", u2, dproj, [])
    (dx, dab1, s1, df1, acc1), _ = _ffn_bwd(dh1, x, a1, b1, f1, vec1, [wb["w_ffn1_in"]], wb["w_ffn1_out"], "ffn1_bwd")
    grads["w_ffn1_out"], _ = _wgrad(s1, df1, "wg_ffn1_out", FC, 512, 1024)
    wgrad_cols("w_ffn1_in", u1, dab1, ["w_ffn1_out"])

    dmod = jnp.concatenate([acc1[0], acc1[1], acc1[3], acc2b[0], acc2b[1], acc2a[3], acc3[0], acc3[1], acc3[3]])
    dgains = jnp.stack([acc1[2], acc2b[2], acc3[2], lacc[1]], axis=0)
    return loss, dx, dmod, dgains, grad_w_pool, accps[0], grads


SMALL = ("b_ada", "g_norm_ffn1", "g_norm_mix", "g_norm_ffn2", "g_final", "pool_scale", "w_pool")
WEIGHTS = ("w_ada", "b_ada", "g_norm_ffn1", "w_ffn1_in", "w_ffn1_out", "g_norm_mix", "w_in", "w_pool", "pool_scale",
           "w_pool_branch", "w_attn_branch", "w_out", "g_norm_ffn2", "w_ffn2_in", "w_ffn2_out", "g_final")


def _pack_small(t):
    return jnp.concatenate([t[n].reshape(-1) for n in SMALL]).reshape(1, -1)


def _unpack_small(flat, like):
    out, off = {}, 0
    for n in SMALL:
        size = like[n].size
        out[n] = flat[0, off:off + size].reshape(like[n].shape)
        off += size
    return out


def kernel(x, c, positions, w_ada, b_ada, g_norm_ffn1, w_ffn1_in, w_ffn1_out, g_norm_mix, w_in, w_pool, pool_scale, w_pool_branch, w_attn_branch, w_out, g_norm_ffn2, w_ffn2_in, w_ffn2_out, g_final, loss_target, m_w_ada, m_b_ada, m_g_norm_ffn1, m_w_ffn1_in, m_w_ffn1_out, m_g_norm_mix, m_w_in, m_w_pool, m_pool_scale, m_w_pool_branch, m_w_attn_branch, m_w_out, m_g_norm_ffn2, m_w_ffn2_in, m_w_ffn2_out, m_g_final, v_w_ada, v_b_ada, v_g_norm_ffn1, v_w_ffn1_in, v_w_ffn1_out, v_g_norm_mix, v_w_in, v_w_pool, v_pool_scale, v_w_pool_branch, v_w_attn_branch, v_w_out, v_g_norm_ffn2, v_w_ffn2_in, v_w_ffn2_out, v_g_final):
    w = dict(w_ada=w_ada, b_ada=b_ada, g_norm_ffn1=g_norm_ffn1, w_ffn1_in=w_ffn1_in, w_ffn1_out=w_ffn1_out,
             g_norm_mix=g_norm_mix, w_in=w_in, w_pool=w_pool, pool_scale=pool_scale, w_pool_branch=w_pool_branch,
             w_attn_branch=w_attn_branch, w_out=w_out, g_norm_ffn2=g_norm_ffn2, w_ffn2_in=w_ffn2_in,
             w_ffn2_out=w_ffn2_out, g_final=g_final)
    mom = dict(w_ada=m_w_ada, b_ada=m_b_ada, g_norm_ffn1=m_g_norm_ffn1, w_ffn1_in=m_w_ffn1_in, w_ffn1_out=m_w_ffn1_out,
               g_norm_mix=m_g_norm_mix, w_in=m_w_in, w_pool=m_w_pool, pool_scale=m_pool_scale,
               w_pool_branch=m_w_pool_branch, w_attn_branch=m_w_attn_branch, w_out=m_w_out, g_norm_ffn2=m_g_norm_ffn2,
               w_ffn2_in=m_w_ffn2_in, w_ffn2_out=m_w_ffn2_out, g_final=m_g_final)
    var = dict(w_ada=v_w_ada, b_ada=v_b_ada, g_norm_ffn1=v_g_norm_ffn1, w_ffn1_in=v_w_ffn1_in, w_ffn1_out=v_w_ffn1_out,
               g_norm_mix=v_g_norm_mix, w_in=v_w_in, w_pool=v_w_pool, pool_scale=v_pool_scale,
               w_pool_branch=v_w_pool_branch, w_attn_branch=v_w_attn_branch, w_out=v_w_out, g_norm_ffn2=v_g_norm_ffn2,
               w_ffn2_in=v_w_ffn2_in, w_ffn2_out=v_w_ffn2_out, g_final=v_g_final)
    ix, iy, ic = _place()
    chip = 2 * ix + iy
    me = 4 * ix + 2 * iy + ic
    nada = w_ada.shape[2]

    c_all = _gather_small(c)[:, 0, :]
    b_shard = lax.dynamic_slice_in_dim(b_ada, chip * nada, nada, axis=1)
    mod_cols = _ada_fwd(c_all, w_ada[0], b_shard)
    mod_all = _gather_small(mod_cols)
    mod = jnp.concatenate([lax.dynamic_index_in_dim(mod_all[4 * (kk >> 1) + 2 * (kk & 1)], me, axis=0, keepdims=False)
                           for kk in range(4)])

    shards = {n: w[n][0].astype(BF16) for n in BIG}
    half = D // 2
    shards["w_ffn2_in/0"], shards["w_ffn2_in/1"] = shards["w_ffn2_in"][:half], shards["w_ffn2_in"][half:]
    ws = _Sharded(shards)
    loss, dx, dmod, dgains, g_w_pool, g_pool_scale, grads = _example_step(
        x[0], loss_target[0], positions[0], mod, (g_norm_ffn1[0], g_norm_mix[0], g_norm_ffn2[0], g_final),
        w_pool[0], pool_scale[0], ws)

    small_g = dict(b_ada=dmod, g_norm_ffn1=dgains[0], g_norm_mix=dgains[1], g_norm_ffn2=dgains[2], g_final=dgains[3],
                   pool_scale=g_pool_scale, w_pool=g_w_pool)
    tail = jnp.zeros((1, 128), F32)
    gathered = _gather_small(jnp.concatenate([_pack_small(small_g), jnp.pad(loss.reshape(1, 1), ((0, 0), (0, 127)))], axis=1))
    sg, sd, sm, sv = _adam_small(*[jnp.concatenate([_pack_small(t), tail], axis=1) for t in (w, mom, var)], gathered)
    small_out = [_unpack_small(t, w) for t in (sg, sd, sm, sv)]
    loss = sg[0, sg.shape[1] - 128]

    dmod_all = gathered[:, 0, :NMOD * D]
    dmod_cols = lax.dynamic_slice_in_dim(dmod_all, chip * nada, nada, axis=1)
    g_ada = _ada_bwd(c_all, dmod_cols)
    ada_out = _adam(w_ada[0], m_w_ada[0], v_w_ada[0], [g_ada], "adam_w_ada")

    partial = [_sum4(ws.recv[n], "sum_" + n) for n in BIG]
    other = _swap_sibling(partial)
    big_out = {n: _adam(w[n][0], mom[n][0], var[n][0], [pa, pb], "adam_" + n) for n, pa, pb in zip(BIG, partial, other)}

    def leaf(kind, n):
        if n == "w_ada":
            return ada_out[kind][None]
        if n in big_out:
            return big_out[n][kind][None]
        return small_out[kind][n]

    return (loss, dx[None], *[leaf(kind, n) for kind in range(4) for n in WEIGHTS])
```

```python
import jax
import jax.numpy as jnp
from jax import lax
from jax.experimental import pallas as pl
from jax.experimental.pallas import tpu as pltpu

F32 = jnp.float32
BF16 = jnp.bfloat16

D = 1024
FF = 2816
FC = 1408
PW = 256
GA = 256
HD = 64
LANES = 128
NH = GA // HD
NG = 3
DIL = (1, 4, 16)
BLK = 128
GW = 2 * D
INW = PW + 3 * NG * GA + GW
NMOD = 9
POOL_WINDOWS = (2, 4, 8, 16)
HALO = 16
EPS = 1e-6
SCALE = HD ** -0.5
NEG = -1e30

LR, B1, B2, AEPS, WD, STEP = 0.001, 0.9, 0.999, 1e-08, 0.01, 10

VMEM_BIG = 56 * 1024 * 1024
TM = 256

MESH = pl.DeviceIdType.MESH
ANY = pl.BlockSpec(memory_space=pl.ANY)


def _call(body, name, grid, in_specs, out_specs, out_shape, scratch=(), vmem=None, comm=None):
    params = pltpu.CompilerParams(dimension_semantics=("arbitrary",) * len(grid), vmem_limit_bytes=vmem)
    n_in, n_out, n_scr = len(in_specs), len(out_shape), len(scratch)
    if comm is None:
        call = pl.pallas_call(body, name=name, grid=grid, in_specs=list(in_specs), out_specs=list(out_specs),
                              out_shape=list(out_shape), scratch_shapes=list(scratch), compiler_params=params)
        return lambda *args: (call(*args), ())
    nc = len(comm.inputs)

    def body_with_comm(*refs):
        ins, refs = refs[:n_in], refs[n_in:]
        c_ins, refs = refs[:nc], refs[nc:]
        outs, refs = refs[:n_out], refs[n_out:]
        c_outs, refs = refs[:nc], refs[nc:]
        scr, sems = refs[:n_scr], refs[n_scr:]
        first = pl.program_id(0) == 0
        last = pl.program_id(0) == grid[0] - 1
        for ax in range(1, len(grid)):
            first = jnp.logical_and(first, pl.program_id(ax) == 0)
            last = jnp.logical_and(last, pl.program_id(ax) == grid[ax] - 1)

        @pl.when(first)
        def _():
            comm.start(c_ins, c_outs, sems)

        body(*ins, *outs, *scr)

        @pl.when(last)
        def _():
            comm.wait(c_ins, c_outs, sems)

    call = pl.pallas_call(
        body_with_comm, name=name, grid=grid, in_specs=list(in_specs) + [ANY] * nc,
        out_specs=list(out_specs) + [ANY] * nc, out_shape=list(out_shape) + list(comm.out_shapes),
        scratch_shapes=list(scratch) + list(comm.sem_shapes), compiler_params=params)

    def run(*args):
        res = call(*args, *comm.inputs)
        return res[:n_out], res[n_out:]

    return run


def _rows(tm, n):
    return pl.BlockSpec((tm, n), lambda i: (i, 0))


def _const(shape):
    return pl.BlockSpec(shape, lambda i: (0,) * len(shape))


def _sds(shape, dtype):
    return jax.ShapeDtypeStruct(shape, dtype)


def _dot(a, b):
    return jnp.dot(a, b, preferred_element_type=F32)


def _dot_nt(a, b):
    return lax.dot_general(a, b, (((1,), (1,)), ((), ())), preferred_element_type=F32)


def _dot_tn(a, b):
    return lax.dot_general(a, b, (((0,), (0,)), ((), ())), preferred_element_type=F32)


def _colsum(v):
    return jnp.sum(v, axis=0, keepdims=True)


def _norm_fwd(h, g, sh, sc):
    r = lax.rsqrt(jnp.mean(h * h, axis=-1, keepdims=True) + EPS)
    xh = h * r
    n = xh * g
    return xh, r, n, n * (1.0 + sc) + sh


def _norm_bwd(du, xh, r, n, g, sc):
    dn = du * (1.0 + sc)
    dxh = dn * g
    dh = r * (dxh - xh * jnp.mean(dxh * xh, axis=-1, keepdims=True))
    return dh, _colsum(du), _colsum(du * n), _colsum(dn * xh)


def _load_once(pairs, sems):
    @pl.when(pl.program_id(0) == 0)
    def _():
        cps = [pltpu.make_async_copy(src, dst, sems.at[j]) for j, (src, dst) in enumerate(pairs)]
        for cp in cps:
            cp.start()
        for cp in cps:
            cp.wait()


def _zero_first(ref):
    @pl.when(pl.program_id(0) == 0)
    def _():
        ref[...] = jnp.zeros(ref.shape, ref.dtype)


def _row_chunks(hbm_refs, vmem_ref):
    pairs, row = [], 0
    for ref in hbm_refs:
        pairs.append((ref, vmem_ref.at[pl.ds(row, ref.shape[0]), :]))
        row += ref.shape[0]
    return pairs


def _ffn_fwd(h, vec, wins, wout, name, comm=None):
    T = h.shape[0]
    nwin = len(wins)

    def body(h_ref, vec_ref, *rest):
        win_hbms, (wout_hbm, ho_ref, u_ref, a_ref, b_ref, f_ref, win_v, wout_v, sems) = rest[:nwin], rest[nwin:]
        _load_once(_row_chunks(win_hbms, win_v) + [(wout_hbm, wout_v)], sems)
        hh = h_ref[...]
        g, sh, sc, gt = vec_ref[0:1, :], vec_ref[1:2, :], vec_ref[2:3, :], vec_ref[3:4, :]
        _, _, _, u = _norm_fwd(hh, g, sh, sc)
        ub = u.astype(BF16)
        u_ref[...] = ub
        acc = None
        for j in range(FF // FC):
            lo, hi = j * FC, (j + 1) * FC
            a = _dot(ub, win_v[:, lo:hi])
            b = _dot(ub, win_v[:, FF + lo:FF + hi])
            a_ref[:, lo:hi] = a.astype(BF16)
            b_ref[:, lo:hi] = b.astype(BF16)
            s = (a * jax.nn.sigmoid(a) * b).astype(BF16)
            part = _dot(s, wout_v[lo:hi, :])
            acc = part if acc is None else acc + part
        f_ref[...] = acc.astype(BF16)
        ho_ref[...] = hh + 0.5 * gt * acc

    return _call(
        body, name, (T // TM,),
        [_rows(TM, D), _const((8, D))] + [ANY] * (nwin + 1),
        [_rows(TM, D), _rows(TM, D), _rows(TM, FF), _rows(TM, FF), _rows(TM, D)],
        [_sds((T, D), F32), _sds((T, D), BF16), _sds((T, FF), BF16), _sds((T, FF), BF16), _sds((T, D), BF16)],
        scratch=[pltpu.VMEM((D, 2 * FF), BF16), pltpu.VMEM((FF, D), BF16), pltpu.SemaphoreType.DMA((nwin + 1,))],
        vmem=VMEM_BIG, comm=comm,
    )(h, vec, *wins, wout)


def _ffn_bwd(dh, h, a, b, f, vec, wins, wout, name, comm=None):
    T = h.shape[0]
    nwin = len(wins)

    def body(dh_ref, h_ref, a_ref, b_ref, f_ref, vec_ref, *rest):
        win_hbms, (wout_hbm, dhi_ref, dab_ref, s_ref, df_ref, acc_ref, win_v, wout_v, sems) = rest[:nwin], rest[nwin:]
        _load_once(_row_chunks(win_hbms, win_v) + [(wout_hbm, wout_v)], sems)
        _zero_first(acc_ref)
        g, sh, sc, gt = vec_ref[0:1, :], vec_ref[1:2, :], vec_ref[2:3, :], vec_ref[3:4, :]
        dho = dh_ref[...]
        df = (0.5 * gt * dho).astype(BF16)
        df_ref[...] = df
        dgt = _colsum(0.5 * dho * f_ref[...].astype(F32))
        du = None
        for j in range(FF // FC):
            lo, hi = j * FC, (j + 1) * FC
            av = a_ref[:, lo:hi].astype(F32)
            bv = b_ref[:, lo:hi].astype(F32)
            ds = _dot_nt(df, wout_v[lo:hi, :])
            sig = jax.nn.sigmoid(av)
            sa = av * sig
            s_ref[:, lo:hi] = (sa * bv).astype(BF16)
            da = (ds * bv * (sig * (1.0 + av * (1.0 - sig)))).astype(BF16)
            db = (ds * sa).astype(BF16)
            dab_ref[:, lo:hi] = da
            dab_ref[:, FF + lo:FF + hi] = db
            part = _dot_nt(da, win_v[:, lo:hi]) + _dot_nt(db, win_v[:, FF + lo:FF + hi])
            du = part if du is None else du + part
        xh, r, n, _ = _norm_fwd(h_ref[...], g, sh, sc)
        dhn, dsh, dsc, dg = _norm_bwd(du, xh, r, n, g, sc)
        dhi_ref[...] = dho + dhn
        acc_ref[0:1, :] += dsh
        acc_ref[1:2, :] += dsc
        acc_ref[2:3, :] += dg
        acc_ref[3:4, :] += dgt

    return _call(
        body, name, (T // TM,),
        [_rows(TM, D), _rows(TM, D), _rows(TM, FF), _rows(TM, FF), _rows(TM, D), _const((8, D))] + [ANY] * (nwin + 1),
        [_rows(TM, D), _rows(TM, 2 * FF), _rows(TM, FF), _rows(TM, D), _const((8, D))],
        [_sds((T, D), F32), _sds((T, 2 * FF), BF16), _sds((T, FF), BF16), _sds((T, D), BF16), _sds((8, D), F32)],
        scratch=[pltpu.VMEM((D, 2 * FF), BF16), pltpu.VMEM((FF, D), BF16), pltpu.SemaphoreType.DMA((nwin + 1,))],
        vmem=VMEM_BIG, comm=comm,
    )(dh, h, a, b, f, vec, *wins, wout)


def _wgrad(x, y, name, tk, tn, tt, out_dtype=BF16, comm=None):
    T, K = x.shape
    N = y.shape[1]
    nt = T // tt

    def body(x_ref, y_ref, o_ref, acc_ref):
        t = pl.program_id(2)
        part = _dot_tn(x_ref[...], y_ref[...])

        @pl.when(t == 0)
        def _():
            acc_ref[...] = part

        @pl.when(t > 0)
        def _():
            acc_ref[...] += part

        @pl.when(t == nt - 1)
        def _():
            o_ref[...] = acc_ref[...].astype(out_dtype)

    (out,), c_outs = _call(
        body, name, (K // tk, N // tn, nt),
        [pl.BlockSpec((tt, tk), lambda i, j, t: (t, i)), pl.BlockSpec((tt, tn), lambda i, j, t: (t, j))],
        [pl.BlockSpec((tk, tn), lambda i, j, t: (i, j))], [_sds((K, N), out_dtype)],
        scratch=[pltpu.VMEM((tk, tn), F32)], vmem=VMEM_BIG, comm=comm,
    )(x, y)
    return out, c_outs


def _wgrad_scatter(x, y, name, tt, comm=None):
    T, K = x.shape
    n = y.shape[1] // 4
    nt = T // tt
    half = K // 2
    nc = 0 if comm is None else len(comm.inputs)

    def body(chip_ref, x_ref, y_ref, *refs):
        c_ins, refs = refs[:nc], refs[nc:]
        recv_ref, refs = refs[0], refs[1:]
        c_outs, refs = refs[:nc], refs[nc:]
        acc_ref, keep_ref, give_ref, take_ref, local_sem, give_sems, take_sems, send_sems, recv_sems = refs[:9]
        j, t = pl.program_id(0), pl.program_id(1)
        px, py, pc = _place()

        def hand_over(jj):
            return pltpu.make_async_remote_copy(
                src_ref=give_ref.at[jj], dst_ref=take_ref.at[jj], send_sem=give_sems.at[jj], recv_sem=take_sems.at[jj],
                device_id=(px, py, 1 - pc), device_id_type=MESH)

        def send(jj):
            m = jj + 1
            return pltpu.make_async_remote_copy(
                src_ref=keep_ref.at[jj], dst_ref=recv_ref.at[m], send_sem=send_sems.at[jj], recv_sem=recv_sems.at[jj],
                device_id=_chip_peer(px, py, pc, m), device_id_type=MESH)

        def add_sibling(jj):
            hand_over(jj).wait_recv()
            keep_ref[jj] = (keep_ref[jj].astype(F32) + take_ref[jj].astype(F32)).astype(BF16)

        if comm is not None:
            @pl.when(jnp.logical_and(j == 0, t == 0))
            def _():
                comm.start(c_ins, c_outs, refs[9:])

        part = _dot_tn(x_ref[...], y_ref[...])

        @pl.when(t == 0)
        def _():
            acc_ref[...] = part

        @pl.when(t > 0)
        def _():
            acc_ref[...] += part

        for jj in range(3):
            @pl.when(jnp.logical_and(j == jj + 1, t == nt // 2))
            def _():
                add_sibling(jj)
                send(jj).start()

        for jj in range(4):
            @pl.when(jnp.logical_and(j == jj, t == nt - 1))
            def _():
                keep_ref[jj] = acc_ref[pl.ds(pl.multiple_of(pc * half, 16), half), :].astype(BF16)
                give_ref[jj] = acc_ref[pl.ds(pl.multiple_of((1 - pc) * half, 16), half), :].astype(BF16)
                hand_over(jj).start()

        @pl.when(jnp.logical_and(j == 3, t == nt - 1))
        def _():
            add_sibling(3)
            own = pltpu.make_async_copy(keep_ref.at[3], recv_ref.at[0], local_sem.at[0])
            own.start()
            for jj in range(3):
                send(jj).wait_recv()
            for jj in range(3):
                send(jj).wait_send()
            for jj in range(4):
                hand_over(jj).wait_send()
            own.wait()
            if comm is not None:
                comm.wait(c_ins, c_outs, refs[9:])

    grid_spec = pltpu.PrefetchScalarGridSpec(
        num_scalar_prefetch=1, grid=(4, nt),
        in_specs=[pl.BlockSpec((tt, K), lambda j, t, chip: (t, 0)),
                  pl.BlockSpec((tt, n), lambda j, t, chip: (t, chip[0] ^ ((j + 1) & 3)))] + [ANY] * nc,
        out_specs=[ANY] * (1 + nc),
        scratch_shapes=[pltpu.VMEM((K, n), F32)] + [pltpu.VMEM((4, half, n), BF16)] * 3
        + [pltpu.SemaphoreType.DMA((1,))] + [pltpu.SemaphoreType.DMA((4,))] * 2 + [pltpu.SemaphoreType.DMA((3,))] * 2
        + ([] if comm is None else list(comm.sem_shapes)))
    px, py, _ = _place()
    res = pl.pallas_call(
        body, name=name, grid_spec=grid_spec,
        out_shape=[_sds((4, half, n), BF16)] + ([] if comm is None else list(comm.out_shapes)),
        compiler_params=pltpu.CompilerParams(dimension_semantics=("arbitrary", "arbitrary"), vmem_limit_bytes=VMEM_BIG),
    )((2 * px + py).astype(jnp.int32).reshape(1), x, y, *([] if comm is None else comm.inputs))
    return res[0], res[1:]


def _final_loss(h, tgt, gvec):
    T = h.shape[0]

    def body(h_ref, t_ref, g_ref, dh_ref, acc_ref):
        _zero_first(acc_ref)
        hh = h_ref[...]
        g = g_ref[0:1, :]
        r = lax.rsqrt(jnp.mean(hh * hh, axis=-1, keepdims=True) + EPS)
        xh = hh * r
        err = xh * g - t_ref[...]
        dy = err * (1.0 / D)
        dxh = dy * g
        dh_ref[...] = r * (dxh - xh * jnp.mean(dxh * xh, axis=-1, keepdims=True))
        acc_ref[0:1, :] += _colsum(err * err)
        acc_ref[1:2, :] += _colsum(dy * xh)

    return _call(
        body, "final_loss", (T // TM,),
        [_rows(TM, D), _rows(TM, D), _const((8, D))],
        [_rows(TM, D), _const((8, D))],
        [_sds((T, D), F32), _sds((8, D), F32)],
    )(h, tgt, gvec)[0]


def _swap_halves(t):
    w = t.shape[1]
    lane = lax.broadcasted_iota(jnp.int32, t.shape, 1)
    return jnp.where(lane % HD < HD // 2, pltpu.roll(t, w - HD // 2, 1), pltpu.roll(t, HD // 2, 1))


def _rope(t, cos, sin_signed):
    c = jnp.tile(cos, (1, t.shape[1] // cos.shape[1]))
    s = jnp.tile(sin_signed, (1, t.shape[1] // sin_signed.shape[1]))
    return t * c + _swap_halves(t) * s


def _rope_bwd(dt, cos, sin_signed):
    c = jnp.tile(cos, (1, dt.shape[1] // cos.shape[1]))
    s = jnp.tile(sin_signed, (1, dt.shape[1] // sin_signed.shape[1]))
    return dt * c + _swap_halves(dt * s)


def _rm_spec(dil):
    return pl.BlockSpec((dil, TM // dil, GA), lambda i: (0, i, 0))


def _to_residues(t, dst_ref, scr_ref, dil):
    if dil == 1:
        dst_ref[0] = t.astype(dst_ref.dtype)
        return
    for j in range(GA // LANES):
        scr_ref[j] = t[:, j * LANES:(j + 1) * LANES]
    for r in range(dil):
        for j in range(GA // LANES):
            rows = scr_ref.at[j][pl.ds(r, TM // dil, stride=dil), :]
            dst_ref[r, :, j * LANES:(j + 1) * LANES] = rows.astype(dst_ref.dtype)


def _from_residues(src_ref, scr_ref, dil):
    if dil == 1:
        return src_ref[0].astype(F32)
    for r in range(dil):
        for j in range(GA // LANES):
            scr_ref.at[j][pl.ds(r, TM // dil, stride=dil), :] = src_ref[r, :, j * LANES:(j + 1) * LANES].astype(F32)
    return jnp.concatenate([scr_ref[j] for j in range(GA // LANES)], axis=1)


def _mix_proj(h, vec, win, cos, sin, comm=None):
    T = h.shape[0]

    def body(h_ref, vec_ref, win_hbm, cos_ref, sin_ref, u_ref, p_ref, *rest):
        qkv_refs, gates_ref, win_v, scr_ref, sems = rest[:3 * NG], rest[3 * NG], rest[3 * NG + 1], rest[3 * NG + 2], rest[3 * NG + 3]
        _load_once([(win_hbm, win_v)], sems)
        g, sh, sc = vec_ref[0:1, :], vec_ref[1:2, :], vec_ref[2:3, :]
        _, _, _, u = _norm_fwd(h_ref[...], g, sh, sc)
        ub = u.astype(BF16)
        u_ref[...] = ub
        p_ref[...] = _dot(ub, win_v[:, 0:PW])
        cos_t, sin_t = cos_ref[...], sin_ref[...]
        for j in range(3 * NG):
            col = PW + j * GA
            t = _dot(ub, win_v[:, col:col + GA])
            if j < 2 * NG:
                t = _rope(t, cos_t, sin_t)
            _to_residues(t, qkv_refs[j], scr_ref, DIL[j % NG])
        for j in range(GW // 512):
            col = PW + 3 * NG * GA + j * 512
            gates_ref[:, j * 512:(j + 1) * 512] = jax.nn.sigmoid(_dot(ub, win_v[:, col:col + 512])).astype(BF16)

    outs, c_outs = _call(
        body, "mix_proj", (T // TM,),
        [_rows(TM, D), _const((8, D)), ANY, _rows(TM, 128), _rows(TM, 128)],
        [_rows(TM, D), _rows(TM, PW)] + [_rm_spec(d) for d in DIL] * 3 + [_rows(TM, GW)],
        [_sds((T, D), BF16), _sds((T, PW), F32)] + [_sds((d, T // d, GA), BF16) for d in DIL] * 3 + [_sds((T, GW), BF16)],
        scratch=[pltpu.VMEM((D, INW), BF16), pltpu.VMEM((GA // LANES, TM, LANES), F32), pltpu.SemaphoreType.DMA((1,))],
        vmem=VMEM_BIG, comm=comm,
    )(h, vec, win, cos, sin)
    return (outs[0], outs[1], outs[2:2 + NG], outs[2 + NG:2 + 2 * NG], outs[2 + 2 * NG:2 + 3 * NG], outs[2 + 3 * NG]), c_outs


def _head_masks():
    lane_head = lax.broadcasted_iota(jnp.int32, (BLK, GA), 1) // HD
    return [lane_head == hd for hd in range(NH)]


def _expand_heads(t, hm):
    return jnp.concatenate([jnp.where(m, t, jnp.zeros_like(t)) for m in hm], axis=0)


def _collapse_heads(tb, hm):
    out = None
    for hd, m in enumerate(hm):
        part = jnp.where(m, tb[hd * BLK:(hd + 1) * BLK, :], 0.0)
        out = part if out is None else out + part
    return out


def _head_rows(t):
    return jnp.concatenate([t[:, hd * HD:hd * HD + 1] for hd in range(NH)], axis=0)


def _band_masks():
    a = lax.broadcasted_iota(jnp.int32, (NH * BLK, BLK), 0) & (BLK - 1)
    c = lax.broadcasted_iota(jnp.int32, (NH * BLK, BLK), 1)
    return c <= a, c >= a


def _attn_specs(nbt):
    cur = pl.BlockSpec((2 * BLK, GA), lambda i: (i, 0))
    prev = pl.BlockSpec((BLK, GA), lambda i: (jnp.maximum(2 * i - 1, 0), 0))
    nxt = pl.BlockSpec((BLK, GA), lambda i: (jnp.minimum(2 * i + 2, nbt - 1), 0))
    return cur, prev, nxt


def _attn_fwd(q, k, v, nb, name, comm=None):
    T = q.shape[0]
    nbt = T // BLK
    lo, hi = slice(0, BLK), slice(BLK, 2 * BLK)

    def block(qv, kc, kp, vc, vp, has_prev, hm):
        m_cur, m_prev = _band_masks()
        m_prev = jnp.logical_and(m_prev, has_prev)
        qb = _expand_heads(qv, hm)
        s_c = jnp.where(m_cur, _dot_nt(qb, kc) * SCALE, NEG)
        s_p = jnp.where(m_prev, _dot_nt(qb, kp) * SCALE, NEG)
        mx = jnp.maximum(jnp.max(s_c, axis=-1, keepdims=True), jnp.max(s_p, axis=-1, keepdims=True))
        e_c = jnp.exp(s_c - mx)
        e_p = jnp.exp(s_p - mx)
        l = jnp.sum(e_c, axis=-1, keepdims=True) + jnp.sum(e_p, axis=-1, keepdims=True)
        inv = 1.0 / l
        ob = _dot((e_c * inv).astype(BF16), vc) + _dot((e_p * inv).astype(BF16), vp)
        return _collapse_heads(ob, hm), _collapse_heads(jnp.broadcast_to(mx + jnp.log(l), (NH * BLK, GA)), hm)

    def body(q_ref, k_ref, kp_ref, v_ref, vp_ref, o_ref, lse_ref):
        b0 = 2 * pl.program_id(0)
        hm = _head_masks()
        o_ref[lo, :], lse_ref[lo, :] = block(q_ref[lo, :], k_ref[lo, :], kp_ref[...], v_ref[lo, :], vp_ref[...],
                                             (b0 & (nb - 1)) != 0, hm)
        o_ref[hi, :], lse_ref[hi, :] = block(q_ref[hi, :], k_ref[hi, :], k_ref[lo, :], v_ref[hi, :], v_ref[lo, :],
                                             ((b0 + 1) & (nb - 1)) != 0, hm)

    cur, prev, _ = _attn_specs(nbt)
    return _call(body, name, (nbt // 2,), [cur, cur, prev, cur, prev], [cur, cur],
                 [_sds((T, GA), F32), _sds((T, GA), F32)], comm=comm)(q, k, k, v, v)


def _attn_bwd(q, k, v, do, lse, e, nb, name, comm=None):
    T = q.shape[0]
    nbt = T // BLK

    lo, hi = slice(0, BLK), slice(BLK, 2 * BLK)

    def block(qv, kc, vc, dov, lsev, ev, kp, vp, qn, don, lsen, en, has_prev, has_next, hm):
        m_cur, m_band = _band_masks()
        m_prev = jnp.logical_and(m_band, has_prev)
        m_next = jnp.logical_and(m_band, has_next)
        qb, dob = _expand_heads(qv, hm), _expand_heads(dov, hm)
        lse_r, e_r = _head_rows(lsev), _head_rows(ev)
        p_c = jnp.where(m_cur, jnp.exp(_dot_nt(qb, kc) * SCALE - lse_r), 0.0)
        p_p = jnp.where(m_prev, jnp.exp(_dot_nt(qb, kp) * SCALE - lse_r), 0.0)
        ds_c = (p_c * (_dot_nt(dob, vc) + e_r)).astype(BF16)
        ds_p = (p_p * (_dot_nt(dob, vp) + e_r)).astype(BF16)
        dq = _collapse_heads((_dot(ds_c, kc) + _dot(ds_p, kp)) * SCALE, hm)
        qnb, donb = _expand_heads(qn, hm), _expand_heads(don, hm)
        p_n = jnp.where(m_next, jnp.exp(_dot_nt(qnb, kc) * SCALE - _head_rows(lsen)), 0.0)
        ds_n = (p_n * (_dot_nt(donb, vc) + _head_rows(en))).astype(BF16)
        dk = (_dot_tn(ds_c, qb) + _dot_tn(ds_n, qnb)) * SCALE
        dv = (_dot_tn(p_c.astype(BF16), dob) + _dot_tn(p_n.astype(BF16), donb)).astype(BF16)
        return dq, dk, dv

    def body(q_ref, k_ref, v_ref, do_ref, lse_ref, e_ref, kp_ref, vp_ref, qn_ref, don_ref, lsen_ref, en_ref,
             dq_ref, dk_ref, dv_ref):
        b0 = 2 * pl.program_id(0)
        hm = _head_masks()
        dq_ref[lo, :], dk_ref[lo, :], dv_ref[lo, :] = block(
            q_ref[lo, :], k_ref[lo, :], v_ref[lo, :], do_ref[lo, :], lse_ref[lo, :], e_ref[lo, :], kp_ref[...], vp_ref[...],
            q_ref[hi, :], do_ref[hi, :], lse_ref[hi, :], e_ref[hi, :],
            (b0 & (nb - 1)) != 0, ((b0 + 1) & (nb - 1)) != 0, hm)
        dq_ref[hi, :], dk_ref[hi, :], dv_ref[hi, :] = block(
            q_ref[hi, :], k_ref[hi, :], v_ref[hi, :], do_ref[hi, :], lse_ref[hi, :], e_ref[hi, :], k_ref[lo, :], v_ref[lo, :],
            qn_ref[...], don_ref[...], lsen_ref[...], en_ref[...],
            ((b0 + 1) & (nb - 1)) != 0, ((b0 + 2) & (nb - 1)) != 0, hm)

    cur, prev, nxt = _attn_specs(nbt)
    return _call(body, name, (nbt // 2,), [cur] * 6 + [prev, prev] + [nxt] * 4, [cur, cur, cur],
                 [_sds((T, GA), F32), _sds((T, GA), F32), _sds((T, GA), BF16)],
                 comm=comm)(q, k, v, do, lse, e, k, v, q, do, lse, e)


def _flat(t):
    return t.reshape(t.shape[0] * t.shape[1], t.shape[2])


def _by_residue(t, dil):
    return t.reshape(dil, t.shape[0] // dil, t.shape[1])


def _pool_consts(shape, row0):
    lane = lax.broadcasted_iota(jnp.int32, shape, 1)
    t = lax.broadcasted_iota(jnp.int32, shape, 0) + row0
    grp = lane // (PW // len(POOL_WINDOWS))
    win = jnp.where(grp == 0, POOL_WINDOWS[0], jnp.where(grp == 1, POOL_WINDOWS[1],
                    jnp.where(grp == 2, POOL_WINDOWS[2], POOL_WINDOWS[3])))
    cnt = jnp.minimum(t + 1, win).astype(F32)
    return grp, cnt


def _window_sums(ext_ref, base, step, tm):
    outs, run = [], None
    for j in range(POOL_WINDOWS[-1]):
        sl = ext_ref[pl.ds(base + step * j, tm), :]
        run = sl if run is None else run + sl
        if j + 1 in POOL_WINDOWS:
            outs.append(run)
    return outs


def _select_group(grp, vals):
    return jnp.where(grp == 0, vals[0], jnp.where(grp == 1, vals[1], jnp.where(grp == 2, vals[2], vals[3])))


def _pool_d(pc_ref, pp_ref, ext_ref, i, tm):
    ext_ref[0:HALO, :] = jnp.where(i > 0, pp_ref[tm - HALO:tm, :], 0.0)
    ext_ref[HALO:HALO + tm, :] = pc_ref[...]
    grp, cnt = _pool_consts((tm, PW), i * tm)
    sums = _window_sums(ext_ref, HALO, -1, tm)
    return _select_group(grp, sums) / cnt - pc_ref[...]


def _group_weights(ls):
    mx = jnp.maximum(jnp.maximum(ls[0], ls[1]), ls[2])
    es = [jnp.exp(l - mx) for l in ls]
    inv = 1.0 / (es[0] + es[1] + es[2])
    return [e * inv for e in es]


def _mix_merge(h, vec, p, os, lses, gates, wp_bd, pscale, wpb, wab, wout):
    T = h.shape[0]

    def body(h_ref, vec_ref, pc_ref, pp_ref, o0, o1, o2, l0, l1, l2, gates_ref, wp_ref, ps_ref, wpb_ref, wab_ref, wout_ref,
             ho_ref, yp_ref, ya_ref, mg_ref, mo_ref, d_ref, ext_ref, scr_ref):
        i = pl.program_id(0)
        gt = vec_ref[3:4, :]
        d = _pool_d(pc_ref, pp_ref, ext_ref, i, TM).astype(BF16)
        d_ref[...] = d
        ypool = (_dot(d, wp_ref[...]) * ps_ref[0:1, :]).astype(BF16)
        yp_ref[...] = ypool
        w = _group_weights([_from_residues(r, scr_ref, dl) for r, dl in zip((l0, l1, l2), DIL)])
        yattn = None
        for wg, o_ref, dl in zip(w, (o0, o1, o2), DIL):
            part = wg * _from_residues(o_ref, scr_ref, dl)
            yattn = part if yattn is None else yattn + part
        yattn = yattn.astype(BF16)
        ya_ref[...] = yattn
        merged = (gates_ref[:, 0:D].astype(F32) * _dot(ypool, wpb_ref[...])
                  + gates_ref[:, D:GW].astype(F32) * _dot(yattn, wab_ref[...])).astype(BF16)
        mg_ref[...] = merged
        mo = _dot(merged, wout_ref[...])
        mo_ref[...] = mo.astype(BF16)
        ho_ref[...] = h_ref[...] + gt * mo

    prev = pl.BlockSpec((TM, PW), lambda i: (jnp.maximum(i - 1, 0), 0))
    return _call(
        body, "mix_merge", (T // TM,),
        [_rows(TM, D), _const((8, D)), _rows(TM, PW), prev] + [_rm_spec(dl) for dl in DIL] * 2 + [_rows(TM, GW), _const((PW, PW)),
         _const((8, PW)), _const((PW, D)), _const((GA, D)), _const((D, D))],
        [_rows(TM, D), _rows(TM, PW), _rows(TM, GA), _rows(TM, D), _rows(TM, D), _rows(TM, PW)],
        [_sds((T, D), F32), _sds((T, PW), BF16), _sds((T, GA), BF16), _sds((T, D), BF16), _sds((T, D), BF16), _sds((T, PW), BF16)],
        scratch=[pltpu.VMEM((TM + HALO, PW), F32), pltpu.VMEM((GA // LANES, TM, LANES), F32)],
        vmem=VMEM_BIG,
    )(h, vec, p, p, *os, *lses, gates, wp_bd, pscale, wpb, wab, wout)[0]


def _mix_bwd_a(dh, vec, mixout, gates, ypool, yattn, dpool, os, lses, wp_bd, pscale, wpb, wab, wout, ones_bd, comm=None):
    T = dh.shape[0]

    def body(dh_ref, vec_ref, mo_ref, gates_ref, yp_ref, ya_ref, d_ref, o0, o1, o2, l0, l1, l2,
             wp_ref, ps_ref, wpb_ref, wab_ref, wout_ref, ones_ref,
             dmo_ref, dp_ref, da_ref, dgates_ref, do0, do1, do2, e0, e1, e2, dd_ref, dyp_ref, acc_ref, acc2_ref, scr_ref):
        _zero_first(acc_ref)
        _zero_first(acc2_ref)
        gt = vec_ref[3:4, :]
        dho = dh_ref[...]
        acc_ref[3:4, :] += _colsum(dho * mo_ref[...].astype(F32))
        dmo = (gt * dho).astype(BF16)
        dmo_ref[...] = dmo
        dmerged = _dot_nt(dmo, wout_ref[...])
        gp = gates_ref[:, 0:D].astype(F32)
        ga = gates_ref[:, D:GW].astype(F32)
        bp = _dot(yp_ref[...], wpb_ref[...])
        ba = _dot(ya_ref[...], wab_ref[...])
        dgates_ref[:, 0:D] = (dmerged * bp * gp * (1.0 - gp)).astype(BF16)
        dgates_ref[:, D:GW] = (dmerged * ba * ga * (1.0 - ga)).astype(BF16)
        dbp = (dmerged * gp).astype(BF16)
        dba = (dmerged * ga).astype(BF16)
        dp_ref[...] = dbp
        da_ref[...] = dba
        dypool = _dot_nt(dbp, wpb_ref[...])
        ypre = _dot(d_ref[...], wp_ref[...])
        acc2_ref[0:1, :] += _colsum(dypool * ypre)
        dyp = (dypool * ps_ref[0:1, :]).astype(BF16)
        dyp_ref[...] = dyp
        dd_ref[...] = _dot_nt(dyp, wp_ref[...])
        dya = _dot_nt(dba, wab_ref[...])
        w = _group_weights([_from_residues(r, scr_ref, dl) for r, dl in zip((l0, l1, l2), DIL)])
        ya = None
        for wg, o_ref, dl in zip(w, (o0, o1, o2), DIL):
            part = wg * _from_residues(o_ref, scr_ref, dl)
            ya = part if ya is None else ya + part
        prod = dya * ya
        hi = prod.astype(BF16)
        lo = (prod - hi.astype(F32)).astype(BF16)
        tot = _dot(hi, ones_ref[...]) + _dot(lo, ones_ref[...])
        for wg, do_ref, e_ref, dl in zip(w, (do0, do1, do2), (e0, e1, e2), DIL):
            _to_residues(wg * dya, do_ref, scr_ref, dl)
            _to_residues(-wg * tot, e_ref, scr_ref, dl)

    return _call(
        body, "mix_bwd_a", (T // TM,),
        [_rows(TM, D), _const((8, D)), _rows(TM, D), _rows(TM, GW), _rows(TM, PW), _rows(TM, GA), _rows(TM, PW)]
        + [_rm_spec(dl) for dl in DIL] * 2
        + [_const((PW, PW)), _const((8, PW)), _const((PW, D)), _const((GA, D)), _const((D, D)), _const((GA, GA))],
        [_rows(TM, D)] * 3 + [_rows(TM, GW)] + [_rm_spec(dl) for dl in DIL] * 2
        + [_rows(TM, PW), _rows(TM, PW), _const((8, D)), _const((8, PW))],
        [_sds((T, D), BF16)] * 3 + [_sds((T, GW), BF16)] + [_sds((dl, T // dl, GA), BF16) for dl in DIL]
        + [_sds((dl, T // dl, GA), F32) for dl in DIL]
        + [_sds((T, PW), F32), _sds((T, PW), BF16), _sds((8, D), F32), _sds((8, PW), F32)],
        scratch=[pltpu.VMEM((GA // LANES, TM, LANES), F32)],
        vmem=VMEM_BIG, comm=comm,
    )(dh, vec, mixout, gates, ypool, yattn, dpool, *os, *lses, wp_bd, pscale, wpb, wab, wout, ones_bd)


def _mix_bwd_b(dh, h, vec, dd, dqs, dks, dvs, dgates, cos, sin, win):
    T = h.shape[0]
    nt = T // TM

    def body(dh_ref, h_ref, vec_ref, ddc_ref, ddn_ref, *rest):
        qk_refs, dv_refs = rest[:2 * NG], rest[2 * NG:3 * NG]
        dgates_ref, cos_ref, sin_ref, win_hbm, dhi_ref, dproj_ref, acc_ref, win_v, ext_ref, scr_ref, sems = rest[3 * NG:]
        i = pl.program_id(0)
        _load_once([(win_hbm, win_v)], sems)
        _zero_first(acc_ref)
        g, sh, sc = vec_ref[0:1, :], vec_ref[1:2, :], vec_ref[2:3, :]
        grp, cnt = _pool_consts((TM, PW), i * TM)
        _, cnt_n = _pool_consts((HALO, PW), (i + 1) * TM)
        ext_ref[0:TM, :] = ddc_ref[...] / cnt
        ext_ref[TM:TM + HALO, :] = jnp.where(i < nt - 1, ddn_ref[0:HALO, :] / cnt_n, 0.0)
        dp = _select_group(grp, _window_sums(ext_ref, 0, 1, TM)) - ddc_ref[...]
        dproj_ref[:, 0:PW] = dp.astype(BF16)
        cos_t, sin_t = cos_ref[...], sin_ref[...]
        for j in range(2 * NG):
            col = PW + j * GA
            dt = _from_residues(qk_refs[j], scr_ref, DIL[j % NG])
            dproj_ref[:, col:col + GA] = _rope_bwd(dt, cos_t, sin_t).astype(BF16)
        for j in range(NG):
            col = PW + (2 * NG + j) * GA
            dproj_ref[:, col:col + GA] = _from_residues(dv_refs[j], scr_ref, DIL[j]).astype(BF16)
        dproj_ref[:, PW + 3 * NG * GA:INW] = dgates_ref[...]
        du = None
        for j in range(INW // 512):
            part = _dot_nt(dproj_ref[:, j * 512:(j + 1) * 512], win_v[:, j * 512:(j + 1) * 512])
            du = part if du is None else du + part
        xh, r, n, _ = _norm_fwd(h_ref[...], g, sh, sc)
        dhn, dsh, dsc, dg = _norm_bwd(du, xh, r, n, g, sc)
        dhi_ref[...] = dh_ref[...] + dhn
        acc_ref[0:1, :] += dsh
        acc_ref[1:2, :] += dsc
        acc_ref[2:3, :] += dg

    nxt = pl.BlockSpec((TM, PW), lambda i: (jnp.minimum(i + 1, nt - 1), 0))
    return _call(
        body, "mix_bwd_b", (nt,),
        [_rows(TM, D), _rows(TM, D), _const((8, D)), _rows(TM, PW), nxt] + [_rm_spec(dl) for dl in DIL] * 3
        + [_rows(TM, GW), _rows(TM, 128), _rows(TM, 128), ANY],
        [_rows(TM, D), _rows(TM, INW), _const((8, D))],
        [_sds((T, D), F32), _sds((T, INW), BF16), _sds((8, D), F32)],
        scratch=[pltpu.VMEM((D, INW), BF16), pltpu.VMEM((TM + HALO, PW), F32), pltpu.VMEM((GA // LANES, TM, LANES), F32),
                 pltpu.SemaphoreType.DMA((1,))],
        vmem=VMEM_BIG,
    )(dh, h, vec, dd, dd, *dqs, *dks, *dvs, dgates, cos, sin, win)[0]


def _ada_fwd(c_all, w_shard, b_shard):
    n = w_shard.shape[1]

    def body(c_ref, w_ref, b_ref, o_ref):
        cv = c_ref[...]
        cond = (cv * jax.nn.sigmoid(cv)).astype(BF16)
        o_ref[...] = _dot(cond, w_ref[...].astype(BF16)) + b_ref[...]

    tn = n // 3
    return pl.pallas_call(
        body, name="ada_fwd", grid=(3,),
        in_specs=[pl.BlockSpec((8, D), lambda j: (0, 0)), pl.BlockSpec((D, tn), lambda j: (0, j)), pl.BlockSpec((1, tn), lambda j: (0, j))],
        out_specs=pl.BlockSpec((8, tn), lambda j: (0, j)), out_shape=_sds((8, n), F32),
        compiler_params=pltpu.CompilerParams(dimension_semantics=("arbitrary",)),
    )(c_all, w_shard, b_shard)


def _ada_bwd(c_all, dmod_shard):
    n = dmod_shard.shape[1]

    def body(c_ref, d_ref, o_ref):
        cv = c_ref[...]
        cond = (cv * jax.nn.sigmoid(cv)).astype(BF16)
        o_ref[...] = _dot_tn(cond, d_ref[...].astype(BF16))

    tn = n // 3
    return pl.pallas_call(
        body, name="ada_bwd", grid=(3,),
        in_specs=[pl.BlockSpec((8, D), lambda j: (0, 0)), pl.BlockSpec((8, tn), lambda j: (0, j))],
        out_specs=pl.BlockSpec((D, tn), lambda j: (0, j)), out_shape=_sds((D, n), F32),
        compiler_params=pltpu.CompilerParams(dimension_semantics=("arbitrary",)),
    )(c_all, dmod_shard)


def _adam_math(w, g, m, v):
    m2 = B1 * m + (1.0 - B1) * g
    v2 = B2 * v + (1.0 - B2) * (g * g)
    m_hat = m2 / (1.0 - B1 ** STEP)
    v_hat = v2 / (1.0 - B2 ** STEP)
    delta = -LR * (m_hat / (jnp.sqrt(v_hat) + AEPS) + WD * w)
    return delta, m2, v2


def _adam(w, m, v, parts, name):
    R, C = w.shape
    tr = R
    for cand in (128, 64, 32, 16, 8):
        if R % cand == 0:
            tr = cand
            break
    np_ = len(parts)

    def body(w_ref, m_ref, v_ref, *rest):
        p_refs, (g_ref, d_ref, m2_ref, v2_ref) = rest[:np_], rest[np_:]
        g = p_refs[0][...]
        for pr in p_refs[1:]:
            g = g + pr[...]
        delta, m2, v2 = _adam_math(w_ref[...], g, m_ref[...], v_ref[...])
        g_ref[...] = g
        d_ref[...] = delta
        m2_ref[...] = m2
        v2_ref[...] = v2

    spec = pl.BlockSpec((tr, C), lambda i: (i, 0))
    return pl.pallas_call(
        body, name=name, grid=(R // tr,), in_specs=[spec] * (3 + np_), out_specs=[spec] * 4,
        out_shape=[_sds((R, C), F32)] * 4,
        compiler_params=pltpu.CompilerParams(dimension_semantics=("arbitrary",), vmem_limit_bytes=VMEM_BIG),
    )(w, m, v, *parts)


def _adam_small(w, m, v, gathered):
    P = w.shape[1]

    def body(w_ref, m_ref, v_ref, ga_ref, g_ref, d_ref, m2_ref, v2_ref):
        g = ga_ref[0]
        for dev in range(1, 8):
            g = g + ga_ref[dev]
        delta, m2, v2 = _adam_math(w_ref[...], g, m_ref[...], v_ref[...])
        g_ref[...] = g
        d_ref[...] = delta
        m2_ref[...] = m2
        v2_ref[...] = v2

    return pl.pallas_call(body, name="adam_small", out_shape=[_sds((1, P), F32)] * 4)(w, m, v, gathered)


def _sum4(blocks, name):
    _, R, C = blocks.shape
    tr = R
    for cand in (256, 128, 64, 32, 16):
        if R % cand == 0:
            tr = cand
            break

    def body(r_ref, out_ref):
        out_ref[...] = ((r_ref[0].astype(F32) + r_ref[1].astype(F32)) + r_ref[2].astype(F32)) + r_ref[3].astype(F32)

    return pl.pallas_call(
        body, name=name, grid=(R // tr,),
        in_specs=[pl.BlockSpec((4, tr, C), lambda i: (0, i, 0))],
        out_specs=pl.BlockSpec((tr, C), lambda i: (i, 0)), out_shape=_sds((R, C), F32),
        compiler_params=pltpu.CompilerParams(dimension_semantics=("arbitrary",)),
    )(blocks)


def _place():
    return lax.axis_index("x"), lax.axis_index("y"), lax.axis_index("c")


def _gather_small(v):
    R, P = v.shape

    def body(v_ref, out_ref, send_sems, recv_sems):
        x, y, c = _place()
        me = 4 * x + 2 * y + c
        out_ref[me] = v_ref[...]
        copies = []
        for m in range(1, 8):
            peer = (x ^ (m >> 2), y ^ ((m >> 1) & 1), c ^ (m & 1))
            copies.append(pltpu.make_async_remote_copy(
                src_ref=v_ref, dst_ref=out_ref.at[me], send_sem=send_sems.at[m - 1], recv_sem=recv_sems.at[m - 1],
                device_id=peer, device_id_type=MESH))
        for cp in copies:
            cp.start()
        for m in range(1, 8):
            src = 4 * (x ^ (m >> 2)) + 2 * (y ^ ((m >> 1) & 1)) + (c ^ (m & 1))
            pltpu.make_async_remote_copy(
                src_ref=v_ref, dst_ref=out_ref.at[src], send_sem=send_sems.at[m - 1], recv_sem=recv_sems.at[m - 1],
                device_id=(x, y, c), device_id_type=MESH).wait_recv()
        for cp in copies:
            cp.wait_send()

    vm = pl.BlockSpec(memory_space=pltpu.VMEM)
    return pl.pallas_call(
        body, name="gather_small", in_specs=[vm], out_specs=vm, out_shape=_sds((8, R, P), F32),
        scratch_shapes=[pltpu.SemaphoreType.DMA((7,)), pltpu.SemaphoreType.DMA((7,))],
    )(v)


def _chip_peer(x, y, c, m):
    return (x ^ (m >> 1), y ^ (m & 1), c)


def _shard_ref(ref, axis, k, n):
    start = pl.multiple_of(k * n, 128 if axis == 1 else 16)
    return ref.at[:, pl.ds(start, n)] if axis == 1 else ref.at[pl.ds(start, n), :]


def _half_rows(ref, axis, k, n, hc):
    if axis == 1:
        half = ref.shape[0] // 2
        return ref.at[pl.ds(pl.multiple_of(hc * half, 16), half), pl.ds(pl.multiple_of(k * n, 128), n)]
    half = n // 2
    return ref.at[pl.ds(pl.multiple_of(k * n + hc * half, 16), half), :]


class _GatherPlan:
    def __init__(self, shards, axes):
        self.inputs, self.axes, nw = list(shards), list(axes), len(shards)
        self.out_shapes = [_sds((s.shape[0] * (4 if ax == 0 else 1), s.shape[1] * (4 if ax == 1 else 1)), BF16)
                           for s, ax in zip(shards, axes)]
        self.sem_shapes = [pltpu.SemaphoreType.DMA((nw,))] + [pltpu.SemaphoreType.DMA((nw, 3))] * 4

    def _copies(self, ins, outs, sems):
        local_sems, send_sems, recv_sems, pass_sems, got_sems = sems
        x, y, c = _place()
        k = 2 * x + y
        local, sends, arrivals, passes, handed = [], [], [], [], []
        for j, ax in enumerate(self.axes):
            n = ins[j].shape[ax]
            half = ins[j].shape[0] // 2
            local.append(pltpu.make_async_copy(ins[j], _shard_ref(outs[j], ax, k, n), local_sems.at[j]))
            my_half = ins[j].at[pl.ds(pl.multiple_of(c * half, 16), half), :]
            for m in range(1, 4):
                sends.append(pltpu.make_async_remote_copy(
                    src_ref=my_half, dst_ref=_half_rows(outs[j], ax, k, n, c), send_sem=send_sems.at[j, m - 1],
                    recv_sem=recv_sems.at[j, m - 1], device_id=_chip_peer(x, y, c, m), device_id_type=MESH))
                theirs = _half_rows(outs[j], ax, k ^ m, n, c)
                arrivals.append(pltpu.make_async_remote_copy(
                    src_ref=my_half, dst_ref=theirs, send_sem=send_sems.at[j, m - 1], recv_sem=recv_sems.at[j, m - 1],
                    device_id=(x, y, c), device_id_type=MESH))
                passes.append(pltpu.make_async_remote_copy(
                    src_ref=theirs, dst_ref=theirs, send_sem=pass_sems.at[j, m - 1], recv_sem=got_sems.at[j, m - 1],
                    device_id=(x, y, 1 - c), device_id_type=MESH))
                other = _half_rows(outs[j], ax, k ^ m, n, 1 - c)
                handed.append(pltpu.make_async_remote_copy(
                    src_ref=other, dst_ref=other, send_sem=pass_sems.at[j, m - 1], recv_sem=got_sems.at[j, m - 1],
                    device_id=(x, y, c), device_id_type=MESH))
        return local, sends, arrivals, passes, handed

    def start(self, ins, outs, sems):
        local, sends, _, _, _ = self._copies(ins, outs, sems)
        for cp in local + sends:
            cp.start()

    def wait(self, ins, outs, sems):
        local, sends, arrivals, passes, handed = self._copies(ins, outs, sems)
        for arrived, onward in zip(arrivals, passes):
            arrived.wait_recv()
            onward.start()
        for cp in handed:
            cp.wait_recv()
        for cp in sends + passes:
            cp.wait_send()
        for cp in local:
            cp.wait()


class _ScatterPlan:
    def __init__(self, grads, axes):
        self.inputs, self.axes, nw = list(grads), list(axes), len(grads)
        self.shard_shapes = [(g.shape[0] // (4 if ax == 0 else 1), g.shape[1] // (4 if ax == 1 else 1))
                             for g, ax in zip(grads, axes)]
        self.out_shapes = [_sds((4,) + s, BF16) for s in self.shard_shapes]
        self.sem_shapes = [pltpu.SemaphoreType.DMA((nw,)), pltpu.SemaphoreType.DMA((nw, 3)), pltpu.SemaphoreType.DMA((nw, 3))]

    def _copies(self, ins, outs, sems):
        local_sems, send_sems, recv_sems = sems
        x, y, c = _place()
        k = 2 * x + y
        local, remote, arrivals = [], [], []
        for j, ax in enumerate(self.axes):
            n = self.shard_shapes[j][ax]
            local.append(pltpu.make_async_copy(_shard_ref(ins[j], ax, k, n), outs[j].at[0], local_sems.at[j]))
            for m in range(1, 4):
                remote.append(pltpu.make_async_remote_copy(
                    src_ref=_shard_ref(ins[j], ax, k ^ m, n), dst_ref=outs[j].at[m],
                    send_sem=send_sems.at[j, m - 1], recv_sem=recv_sems.at[j, m - 1],
                    device_id=_chip_peer(x, y, c, m), device_id_type=MESH))
                arrivals.append(pltpu.make_async_remote_copy(
                    src_ref=_shard_ref(ins[j], ax, k, n), dst_ref=outs[j].at[m],
                    send_sem=send_sems.at[j, m - 1], recv_sem=recv_sems.at[j, m - 1],
                    device_id=(x, y, c), device_id_type=MESH))
        return local, remote, arrivals

    def start(self, ins, outs, sems):
        local, remote, _ = self._copies(ins, outs, sems)
        for cp in local + remote:
            cp.start()

    def wait(self, ins, outs, sems):
        local, remote, arrivals = self._copies(ins, outs, sems)
        for cp in arrivals:
            cp.wait_recv()
        for cp in remote:
            cp.wait_send()
        for cp in local:
            cp.wait()


def _run_plan(plan, name):
    nc = len(plan.inputs)

    def body(*refs):
        ins, outs, sems = refs[:nc], refs[nc:2 * nc], refs[2 * nc:]
        plan.start(ins, outs, sems)
        plan.wait(ins, outs, sems)

    return pl.pallas_call(body, name=name, in_specs=[ANY] * nc, out_specs=[ANY] * nc, out_shape=list(plan.out_shapes),
                          scratch_shapes=list(plan.sem_shapes))(*plan.inputs)


def _swap_sibling(parts):
    nw = len(parts)

    def body(*refs):
        ins, outs = refs[:nw], refs[nw:2 * nw]
        send_sems, recv_sems = refs[2 * nw:]
        x, y, c = _place()
        copies = [pltpu.make_async_remote_copy(
            src_ref=ins[j], dst_ref=outs[j], send_sem=send_sems.at[j], recv_sem=recv_sems.at[j],
            device_id=(x, y, 1 - c), device_id_type=MESH) for j in range(nw)]
        for cp in copies:
            cp.start()
        for cp in copies:
            cp.wait()

    return pl.pallas_call(
        body, name="swap_sibling", in_specs=[ANY] * nw, out_specs=[ANY] * nw,
        out_shape=[_sds(p.shape, p.dtype) for p in parts],
        scratch_shapes=[pltpu.SemaphoreType.DMA((nw,)), pltpu.SemaphoreType.DMA((nw,))],
    )(*parts)


def _join_halves(halves):
    nw = len(halves)

    def body(*refs):
        ins, outs = refs[:nw], refs[nw:2 * nw]
        local_sems, send_sems, recv_sems = refs[2 * nw:]
        x, y, c = _place()
        local, remote, arrivals = [], [], []
        for j in range(nw):
            h = ins[j].shape[0]
            mine = outs[j].at[pl.ds(pl.multiple_of(c * h, 8), h), :]
            theirs = outs[j].at[pl.ds(pl.multiple_of((1 - c) * h, 8), h), :]
            local.append(pltpu.make_async_copy(ins[j], mine, local_sems.at[j]))
            remote.append(pltpu.make_async_remote_copy(
                src_ref=ins[j], dst_ref=mine, send_sem=send_sems.at[j], recv_sem=recv_sems.at[j],
                device_id=(x, y, 1 - c), device_id_type=MESH))
            arrivals.append(pltpu.make_async_remote_copy(
                src_ref=ins[j], dst_ref=theirs, send_sem=send_sems.at[j], recv_sem=recv_sems.at[j],
                device_id=(x, y, c), device_id_type=MESH))
        for cp in local + remote:
            cp.start()
        for cp in arrivals:
            cp.wait_recv()
        for cp in remote:
            cp.wait_send()
        for cp in local:
            cp.wait()

    return pl.pallas_call(
        body, name="join_halves", in_specs=[ANY] * nw, out_specs=[ANY] * nw,
        out_shape=[_sds((2 * p.shape[0], p.shape[1]), p.dtype) for p in halves],
        scratch_shapes=[pltpu.SemaphoreType.DMA((nw,))] * 3,
    )(*halves)


BIG = ("w_ffn1_in", "w_ffn1_out", "w_in", "w_pool_branch", "w_attn_branch", "w_out", "w_ffn2_in", "w_ffn2_out")
BIG_AXIS = {"w_ffn1_in": 1, "w_ffn1_out": 0, "w_in": 1, "w_pool_branch": 1, "w_attn_branch": 1, "w_out": 0,
            "w_ffn2_in": 1, "w_ffn2_out": 0}


class _Sharded:
    fused_scatter = True

    def __init__(self, shards):
        self.shards, self.full, self.recv = shards, {}, {}

    def gather_plan(self, names):
        return _GatherPlan([self.shards[n] for n in names], [BIG_AXIS[n.split("/")[0]] for n in names])

    def gather_now(self, names):
        self.gathered(names, _run_plan(self.gather_plan(names), "gather_" + names[0]))

    def gathered(self, names, outs):
        self.full.update(zip(names, outs))

    def scatter_plan(self, names, grads):
        return _ScatterPlan([grads[n] for n in names], [BIG_AXIS[n] for n in names])

    def scatter_now(self, names, grads):
        self.scattered(names, _run_plan(self.scatter_plan(names, grads), "scatter_" + names[0]))

    def scattered(self, names, outs):
        self.recv.update(zip(names, outs))


class _Whole:
    fused_scatter = False

    def __init__(self, full):
        self.full, self.recv = dict(full), {}

    def gather_plan(self, names):
        return None

    def gather_now(self, names):
        pass

    def gathered(self, names, outs):
        pass

    def scatter_plan(self, names, grads):
        return None

    def scatter_now(self, names, grads):
        pass

    def scattered(self, names, outs):
        pass


def _vec(rows):
    pad = [jnp.zeros((1, D), F32)] * (8 - len(rows))
    return jnp.concatenate([r.reshape(1, D) for r in rows] + pad, axis=0)


def _block_diag(w_pool):
    n, c = w_pool.shape[0], w_pool.shape[1]
    eye = jnp.eye(n, dtype=w_pool.dtype)
    return (eye[:, None, :, None] * w_pool[:, :, None, :]).reshape(n * c, n * c)


def _example_step(x, tgt, positions, mod, gains, w_pool, pool_scale, ws):
    T = x.shape[0]
    assert (T // BLK // DIL[-1]) & (T // BLK // DIL[-1] - 1) == 0, "blocks per sequence must be a power of two"
    sh1, sc1, gt1, sh2, sc2, gt2, sh3, sc3, gt3 = [mod[j * D:(j + 1) * D] for j in range(NMOD)]
    g1, g2, g3, gf = gains
    vec1, vec2, vec3 = _vec([g1, sh1, sc1, gt1]), _vec([g2, sh2, sc2, gt2]), _vec([g3, sh3, sc3, gt3])
    inv_freq = 10000.0 ** (-jnp.arange(0, HD, 2, dtype=F32) / HD)
    ang = positions.astype(F32)[:, None] * inv_freq
    cos = jnp.tile(jnp.cos(ang), (1, 4))
    sin = jnp.tile(jnp.concatenate([-jnp.sin(ang), jnp.sin(ang)], axis=1), (1, 2))
    wp_bd = _block_diag(w_pool).astype(BF16)
    ones_bd = _block_diag(jnp.ones((NH, HD, HD), F32)).astype(BF16)
    ps = jnp.concatenate([pool_scale.reshape(1, PW), jnp.zeros((7, PW), F32)], axis=0)
    wb = ws.full

    ws.gather_now(["w_ffn1_in", "w_ffn1_out"])
    mixw = ["w_in", "w_pool_branch", "w_attn_branch", "w_out"]
    (h1, u1, a1, b1, f1), got = _ffn_fwd(x, vec1, [wb["w_ffn1_in"]], wb["w_ffn1_out"], "ffn1_fwd", ws.gather_plan(mixw))
    ws.gathered(mixw, got)
    (u2, p, qs, ks, vs, gates), got = _mix_proj(h1, vec2, wb["w_in"], cos, sin, ws.gather_plan(["w_ffn2_in/0"]))
    ws.gathered(["w_ffn2_in/0"], got)
    qs, ks, vs = [_flat(t) for t in qs], [_flat(t) for t in ks], [_flat(t) for t in vs]
    nbs = [T // d // BLK for d in DIL]
    os, lses = [], []
    for gi, riders in enumerate((["w_ffn2_out"], ["w_ffn2_in/1"], None)):
        (o, lse), got = _attn_fwd(qs[gi], ks[gi], vs[gi], nbs[gi], f"attn_fwd{gi}", riders and ws.gather_plan(riders))
        ws.gathered(riders or [], got)
        os.append(o)
        lses.append(lse)
    win3 = [wb["w_ffn2_in/0"], wb["w_ffn2_in/1"]] if "w_ffn2_in/0" in wb else [wb["w_ffn2_in"]]
    os_r = [_by_residue(t, d) for t, d in zip(os, DIL)]
    lses_r = [_by_residue(t, d) for t, d in zip(lses, DIL)]
    h2, ypool, yattn, merged, mixout, dpool = _mix_merge(
        h1, vec2, p, os_r, lses_r, gates, wp_bd, ps, wb["w_pool_branch"], wb["w_attn_branch"], wb["w_out"])
    (h3, u3, a3, b3, f3), _ = _ffn_fwd(h2, vec3, win3, wb["w_ffn2_out"], "ffn2_fwd")
    dh3, lacc = _final_loss(h3, tgt, _vec([gf]))
    loss = 0.5 * jnp.sum(lacc[0]) / D

    grads = {}

    def wgrad_cols(name, xx, yy, riders):
        plan = ws.scatter_plan(riders, grads) if riders else None
        if ws.fused_scatter:
            blocks, got = _wgrad_scatter(xx, yy, "wg_" + name, 1024, comm=plan)
            ws.scattered([name], [blocks])
        else:
            grads[name], got = _wgrad(xx, yy, "wg_" + name, D, 512, 1024, comm=plan)
        ws.scattered(riders, got)

    (dh2, dab3, s3, df3, acc3), _ = _ffn_bwd(dh3, h2, a3, b3, f3, vec3, win3, wb["w_ffn2_out"], "ffn2_bwd")
    grads["w_ffn2_out"], _ = _wgrad(s3, df3, "wg_ffn2_out", FC, 512, 1024)
    wgrad_cols("w_ffn2_in", u3, dab3, ["w_ffn2_out"])
    (dmo, dbp, dba, dgates, do0, do1, do2, e0, e1, e2, dd, dyp, acc2a, accps), _ = _mix_bwd_a(
        dh2, vec2, mixout, gates, ypool, yattn, dpool, os_r, lses_r, wp_bd, ps,
        wb["w_pool_branch"], wb["w_attn_branch"], wb["w_out"], ones_bd)
    grads["w_out"], _ = _wgrad(merged, dmo, "wg_out", D, 512, 1024)
    grads["w_pool_branch"], _ = _wgrad(ypool, dbp, "wg_pool_branch", PW, 512, 1024)
    grads["w_attn_branch"], _ = _wgrad(yattn, dba, "wg_attn_branch", GA, 512, 1024)
    gwp, _ = _wgrad(dpool, dyp, "wg_pool", PW, PW, 1024, out_dtype=F32)
    n = len(POOL_WINDOWS)
    c = PW // n
    grad_w_pool = jnp.stack([gwp[j * c:(j + 1) * c, j * c:(j + 1) * c] for j in range(n)], axis=0)
    small3 = ["w_out", "w_pool_branch", "w_attn_branch"]
    dqs, dks, dvs = [], [], []
    for gi, (do, e) in enumerate(((do0, e0), (do1, e1), (do2, e2))):
        plan = ws.scatter_plan(small3, grads) if gi == 0 else None
        (dq, dk, dv), got = _attn_bwd(qs[gi], ks[gi], vs[gi], _flat(do), lses[gi], _flat(e), nbs[gi], f"attn_bwd{gi}", plan)
        if gi == 0:
            ws.scattered(small3, got)
        dqs.append(_by_residue(dq, DIL[gi]))
        dks.append(_by_residue(dk, DIL[gi]))
        dvs.append(_by_residue(dv, DIL[gi]))
    dh1, dproj, acc2b = _mix_bwd_b(dh2, h1, vec2, dd, dqs, dks, dvs, dgates, cos, sin, wb["w_in"])
    wgrad_cols("w_in", u2, dproj, [])
    (dx, dab1, s1, df1, acc1), _ = _ffn_bwd(dh1, x, a1, b1, f1, vec1, [wb["w_ffn1_in"]], wb["w_ffn1_out"], "ffn1_bwd")
    grads["w_ffn1_out"], _ = _wgrad(s1, df1, "wg_ffn1_out", FC, 512, 1024)
    wgrad_cols("w_ffn1_in", u1, dab1, ["w_ffn1_out"])

    dmod = jnp.concatenate([acc1[0], acc1[1], acc1[3], acc2b[0], acc2b[1], acc2a[3], acc3[0], acc3[1], acc3[3]])
    dgains = jnp.stack([acc1[2], acc2b[2], acc3[2], lacc[1]], axis=0)
    return loss, dx, dmod, dgains, grad_w_pool, accps[0], grads


SMALL = ("b_ada", "g_norm_ffn1", "g_norm_mix", "g_norm_ffn2", "g_final", "pool_scale", "w_pool")
WEIGHTS = ("w_ada", "b_ada", "g_norm_ffn1", "w_ffn1_in", "w_ffn1_out", "g_norm_mix", "w_in", "w_pool", "pool_scale",
           "w_pool_branch", "w_attn_branch", "w_out", "g_norm_ffn2", "w_ffn2_in", "w_ffn2_out", "g_final")


def _pack_small(t):
    return jnp.concatenate([t[n].reshape(-1) for n in SMALL]).reshape(1, -1)


def _unpack_small(flat, like):
    out, off = {}, 0
    for n in SMALL:
        size = like[n].size
        out[n] = flat[0, off:off + size].reshape(like[n].shape)
        off += size
    return out


def kernel(x, c, positions, w_ada, b_ada, g_norm_ffn1, w_ffn1_in, w_ffn1_out, g_norm_mix, w_in, w_pool, pool_scale, w_pool_branch, w_attn_branch, w_out, g_norm_ffn2, w_ffn2_in, w_ffn2_out, g_final, loss_target, m_w_ada, m_b_ada, m_g_norm_ffn1, m_w_ffn1_in, m_w_ffn1_out, m_g_norm_mix, m_w_in, m_w_pool, m_pool_scale, m_w_pool_branch, m_w_attn_branch, m_w_out, m_g_norm_ffn2, m_w_ffn2_in, m_w_ffn2_out, m_g_final, v_w_ada, v_b_ada, v_g_norm_ffn1, v_w_ffn1_in, v_w_ffn1_out, v_g_norm_mix, v_w_in, v_w_pool, v_pool_scale, v_w_pool_branch, v_w_attn_branch, v_w_out, v_g_norm_ffn2, v_w_ffn2_in, v_w_ffn2_out, v_g_final):
    w = dict(w_ada=w_ada, b_ada=b_ada, g_norm_ffn1=g_norm_ffn1, w_ffn1_in=w_ffn1_in, w_ffn1_out=w_ffn1_out,
             g_norm_mix=g_norm_mix, w_in=w_in, w_pool=w_pool, pool_scale=pool_scale, w_pool_branch=w_pool_branch,
             w_attn_branch=w_attn_branch, w_out=w_out, g_norm_ffn2=g_norm_ffn2, w_ffn2_in=w_ffn2_in,
             w_ffn2_out=w_ffn2_out, g_final=g_final)
    mom = dict(w_ada=m_w_ada, b_ada=m_b_ada, g_norm_ffn1=m_g_norm_ffn1, w_ffn1_in=m_w_ffn1_in, w_ffn1_out=m_w_ffn1_out,
               g_norm_mix=m_g_norm_mix, w_in=m_w_in, w_pool=m_w_pool, pool_scale=m_pool_scale,
               w_pool_branch=m_w_pool_branch, w_attn_branch=m_w_attn_branch, w_out=m_w_out, g_norm_ffn2=m_g_norm_ffn2,
               w_ffn2_in=m_w_ffn2_in, w_ffn2_out=m_w_ffn2_out, g_final=m_g_final)
    var = dict(w_ada=v_w_ada, b_ada=v_b_ada, g_norm_ffn1=v_g_norm_ffn1, w_ffn1_in=v_w_ffn1_in, w_ffn1_out=v_w_ffn1_out,
               g_norm_mix=v_g_norm_mix, w_in=v_w_in, w_pool=v_w_pool, pool_scale=v_pool_scale,
               w_pool_branch=v_w_pool_branch, w_attn_branch=v_w_attn_branch, w_out=v_w_out, g_norm_ffn2=v_g_norm_ffn2,
               w_ffn2_in=v_w_ffn2_in, w_ffn2_out=v_w_ffn2_out, g_final=v_g_final)
    ix, iy, ic = _place()
    chip = 2 * ix + iy
    me = 4 * ix + 2 * iy + ic
    nada = w_ada.shape[2]

    c_all = _gather_small(c)[:, 0, :]
    b_shard = lax.dynamic_slice_in_dim(b_ada, chip * nada, nada, axis=1)
    mod_cols = _ada_fwd(c_all, w_ada[0], b_shard)
    mod_all = _gather_small(mod_cols)
    mod = jnp.concatenate([lax.dynamic_index_in_dim(mod_all[4 * (kk >> 1) + 2 * (kk & 1)], me, axis=0, keepdims=False)
                           for kk in range(4)])

    shards = {n: w[n][0].astype(BF16) for n in BIG}
    half = D // 2
    shards["w_ffn2_in/0"], shards["w_ffn2_in/1"] = shards["w_ffn2_in"][:half], shards["w_ffn2_in"][half:]
    ws = _Sharded(shards)
    loss, dx, dmod, dgains, g_w_pool, g_pool_scale, grads = _example_step(
        x[0], loss_target[0], positions[0], mod, (g_norm_ffn1[0], g_norm_mix[0], g_norm_ffn2[0], g_final),
        w_pool[0], pool_scale[0], ws)

    small_g = dict(b_ada=dmod, g_norm_ffn1=dgains[0], g_norm_mix=dgains[1], g_norm_ffn2=dgains[2], g_final=dgains[3],
                   pool_scale=g_pool_scale, w_pool=g_w_pool)
    tail = jnp.zeros((1, 128), F32)
    gathered = _gather_small(jnp.concatenate([_pack_small(small_g), jnp.pad(loss.reshape(1, 1), ((0, 0), (0, 127)))], axis=1))
    sg, sd, sm, sv = _adam_small(*[jnp.concatenate([_pack_small(t), tail], axis=1) for t in (w, mom, var)], gathered)
    small_out = [_unpack_small(t, w) for t in (sg, sd, sm, sv)]
    loss = sg[0, sg.shape[1] - 128]

    dmod_all = gathered[:, 0, :NMOD * D]
    dmod_cols = lax.dynamic_slice_in_dim(dmod_all, chip * nada, nada, axis=1)
    g_ada = _ada_bwd(c_all, dmod_cols)
    ada_out = _adam(w_ada[0], m_w_ada[0], v_w_ada[0], [g_ada], "adam_w_ada")

    sums = {n: _sum4(ws.recv[n], "sum_" + n) for n in BIG}
    halved = [n for n in BIG if sums[n].shape[0] < w[n].shape[1]]
    whole = [n for n in BIG if n not in halved]
    joined = dict(zip(halved, _join_halves([sums[n] for n in halved])))
    other = dict(zip(whole, _swap_sibling([sums[n] for n in whole])))
    big_out = {n: _adam(w[n][0], mom[n][0], var[n][0], [joined[n]] if n in joined else [sums[n], other[n]], "adam_" + n)
               for n in BIG}

    def leaf(kind, n):
        if n == "w_ada":
            return ada_out[kind][None]
        if n in big_out:
            return big_out[n][kind][None]
        return small_out[kind][n]

    return (loss, dx[None], *[leaf(kind, n) for kind in range(4) for n in WEIGHTS])
```

```python
import jax
import jax.numpy as jnp
from jax import lax
from jax.experimental import pallas as pl
from jax.experimental.pallas import tpu as pltpu

F32 = jnp.float32
BF16 = jnp.bfloat16

D = 1024
FF = 2816
FC = 1408
PW = 256
GA = 256
HD = 64
LANES = 128
NH = GA // HD
NG = 3
DIL = (1, 4, 16)
BLK = 128
GW = 2 * D
INW = PW + 3 * NG * GA + GW
NMOD = 9
POOL_WINDOWS = (2, 4, 8, 16)
HALO = 16
EPS = 1e-6
SCALE = HD ** -0.5
NEG = -1e30

LR, B1, B2, AEPS, WD, STEP = 0.001, 0.9, 0.999, 1e-08, 0.01, 10

VMEM_BIG = 56 * 1024 * 1024
TM = 256

MESH = pl.DeviceIdType.MESH
ANY = pl.BlockSpec(memory_space=pl.ANY)


def _call(body, name, grid, in_specs, out_specs, out_shape, scratch=(), vmem=None, comm=None):
    params = pltpu.CompilerParams(dimension_semantics=("arbitrary",) * len(grid), vmem_limit_bytes=vmem)
    n_in, n_out, n_scr = len(in_specs), len(out_shape), len(scratch)
    if comm is None:
        call = pl.pallas_call(body, name=name, grid=grid, in_specs=list(in_specs), out_specs=list(out_specs),
                              out_shape=list(out_shape), scratch_shapes=list(scratch), compiler_params=params)
        return lambda *args: (call(*args), ())
    nc = len(comm.inputs)

    def body_with_comm(*refs):
        ins, refs = refs[:n_in], refs[n_in:]
        c_ins, refs = refs[:nc], refs[nc:]
        outs, refs = refs[:n_out], refs[n_out:]
        c_outs, refs = refs[:nc], refs[nc:]
        scr, sems = refs[:n_scr], refs[n_scr:]
        first = pl.program_id(0) == 0
        last = pl.program_id(0) == grid[0] - 1
        for ax in range(1, len(grid)):
            first = jnp.logical_and(first, pl.program_id(ax) == 0)
            last = jnp.logical_and(last, pl.program_id(ax) == grid[ax] - 1)

        @pl.when(first)
        def _():
            comm.start(c_ins, c_outs, sems)

        body(*ins, *outs, *scr)

        @pl.when(last)
        def _():
            comm.wait(c_ins, c_outs, sems)

    call = pl.pallas_call(
        body_with_comm, name=name, grid=grid, in_specs=list(in_specs) + [ANY] * nc,
        out_specs=list(out_specs) + [ANY] * nc, out_shape=list(out_shape) + list(comm.out_shapes),
        scratch_shapes=list(scratch) + list(comm.sem_shapes), compiler_params=params)

    def run(*args):
        res = call(*args, *comm.inputs)
        return res[:n_out], res[n_out:]

    return run


def _rows(tm, n):
    return pl.BlockSpec((tm, n), lambda i: (i, 0))


def _const(shape):
    return pl.BlockSpec(shape, lambda i: (0,) * len(shape))


def _sds(shape, dtype):
    return jax.ShapeDtypeStruct(shape, dtype)


def _dot(a, b):
    return jnp.dot(a, b, preferred_element_type=F32)


def _dot_nt(a, b):
    return lax.dot_general(a, b, (((1,), (1,)), ((), ())), preferred_element_type=F32)


def _dot_tn(a, b):
    return lax.dot_general(a, b, (((0,), (0,)), ((), ())), preferred_element_type=F32)


def _colsum(v):
    return jnp.sum(v, axis=0, keepdims=True)


def _norm_fwd(h, g, sh, sc):
    r = lax.rsqrt(jnp.mean(h * h, axis=-1, keepdims=True) + EPS)
    xh = h * r
    n = xh * g
    return xh, r, n, n * (1.0 + sc) + sh


def _norm_bwd(du, xh, r, n, g, sc):
    dn = du * (1.0 + sc)
    dxh = dn * g
    dh = r * (dxh - xh * jnp.mean(dxh * xh, axis=-1, keepdims=True))
    return dh, _colsum(du), _colsum(du * n), _colsum(dn * xh)


def _load_once(pairs, sems):
    @pl.when(pl.program_id(0) == 0)
    def _():
        cps = [pltpu.make_async_copy(src, dst, sems.at[j]) for j, (src, dst) in enumerate(pairs)]
        for cp in cps:
            cp.start()
        for cp in cps:
            cp.wait()


def _zero_first(ref):
    @pl.when(pl.program_id(0) == 0)
    def _():
        ref[...] = jnp.zeros(ref.shape, ref.dtype)


def _row_chunks(hbm_refs, vmem_ref):
    pairs, row = [], 0
    for ref in hbm_refs:
        pairs.append((ref, vmem_ref.at[pl.ds(row, ref.shape[0]), :]))
        row += ref.shape[0]
    return pairs


def _loss_head(hh, tgt, g):
    r = lax.rsqrt(jnp.mean(hh * hh, axis=-1, keepdims=True) + EPS)
    xh = hh * r
    err = xh * g - tgt
    dy = err * (1.0 / D)
    dxh = dy * g
    dh = r * (dxh - xh * jnp.mean(dxh * xh, axis=-1, keepdims=True))
    return dh, _colsum(err * err), _colsum(dy * xh)


def _ffn_fwd(h, vec, wins, wout, name, comm=None, head=None):
    T = h.shape[0]
    nwin = len(wins)
    nhead = 0 if head is None else 2

    def body(h_ref, vec_ref, *rest):
        head_refs, rest = rest[:nhead], rest[nhead:]
        win_hbms, rest = rest[:nwin], rest[nwin:]
        (wout_hbm, ho_ref, u_ref, a_ref, b_ref, f_ref), rest = rest[:6], rest[6:]
        lacc_refs, (win_v, wout_v, sems) = rest[:nhead // 2], rest[nhead // 2:]
        _load_once(_row_chunks(win_hbms, win_v) + [(wout_hbm, wout_v)], sems)
        hh = h_ref[...]
        g, sh, sc, gt = vec_ref[0:1, :], vec_ref[1:2, :], vec_ref[2:3, :], vec_ref[3:4, :]
        _, _, _, u = _norm_fwd(hh, g, sh, sc)
        ub = u.astype(BF16)
        u_ref[...] = ub
        acc = None
        for j in range(FF // FC):
            lo, hi = j * FC, (j + 1) * FC
            a = _dot(ub, win_v[:, lo:hi])
            b = _dot(ub, win_v[:, FF + lo:FF + hi])
            a_ref[:, lo:hi] = a.astype(BF16)
            b_ref[:, lo:hi] = b.astype(BF16)
            s = (a * jax.nn.sigmoid(a) * b).astype(BF16)
            part = _dot(s, wout_v[lo:hi, :])
            acc = part if acc is None else acc + part
        f_ref[...] = acc.astype(BF16)
        ho = hh + 0.5 * gt * acc
        if head is None:
            ho_ref[...] = ho
        else:
            _zero_first(lacc_refs[0])
            dh, sq, dg = _loss_head(ho, head_refs[0][...], head_refs[1][0:1, :])
            ho_ref[...] = dh
            lacc_refs[0][0:1, :] += sq
            lacc_refs[0][1:2, :] += dg

    head_specs = [] if head is None else [_rows(TM, D), _const((8, D))]
    lacc_spec = [] if head is None else [_const((8, D))]
    lacc_shape = [] if head is None else [_sds((8, D), F32)]
    return _call(
        body, name, (T // TM,),
        [_rows(TM, D), _const((8, D))] + head_specs + [ANY] * (nwin + 1),
        [_rows(TM, D), _rows(TM, D), _rows(TM, FF), _rows(TM, FF), _rows(TM, D)] + lacc_spec,
        [_sds((T, D), F32), _sds((T, D), BF16), _sds((T, FF), BF16), _sds((T, FF), BF16), _sds((T, D), BF16)] + lacc_shape,
        scratch=[pltpu.VMEM((D, 2 * FF), BF16), pltpu.VMEM((FF, D), BF16), pltpu.SemaphoreType.DMA((nwin + 1,))],
        vmem=VMEM_BIG, comm=comm,
    )(h, vec, *([] if head is None else head), *wins, wout)


def _ffn_bwd(dh, h, a, b, f, vec, wins, wout, name, comm=None):
    T = h.shape[0]
    nwin = len(wins)

    def body(dh_ref, h_ref, a_ref, b_ref, f_ref, vec_ref, *rest):
        win_hbms, (wout_hbm, dhi_ref, dab_ref, s_ref, df_ref, acc_ref, win_v, wout_v, sems) = rest[:nwin], rest[nwin:]
        _load_once(_row_chunks(win_hbms, win_v) + [(wout_hbm, wout_v)], sems)
        _zero_first(acc_ref)
        g, sh, sc, gt = vec_ref[0:1, :], vec_ref[1:2, :], vec_ref[2:3, :], vec_ref[3:4, :]
        dho = dh_ref[...]
        df = (0.5 * gt * dho).astype(BF16)
        df_ref[...] = df
        dgt = _colsum(0.5 * dho * f_ref[...].astype(F32))
        du = None
        for j in range(FF // FC):
            lo, hi = j * FC, (j + 1) * FC
            av = a_ref[:, lo:hi].astype(F32)
            bv = b_ref[:, lo:hi].astype(F32)
            ds = _dot_nt(df, wout_v[lo:hi, :])
            sig = jax.nn.sigmoid(av)
            sa = av * sig
            s_ref[:, lo:hi] = (sa * bv).astype(BF16)
            da = (ds * bv * (sig * (1.0 + av * (1.0 - sig)))).astype(BF16)
            db = (ds * sa).astype(BF16)
            dab_ref[:, lo:hi] = da
            dab_ref[:, FF + lo:FF + hi] = db
            part = _dot_nt(da, win_v[:, lo:hi]) + _dot_nt(db, win_v[:, FF + lo:FF + hi])
            du = part if du is None else du + part
        xh, r, n, _ = _norm_fwd(h_ref[...], g, sh, sc)
        dhn, dsh, dsc, dg = _norm_bwd(du, xh, r, n, g, sc)
        dhi_ref[...] = dho + dhn
        acc_ref[0:1, :] += dsh
        acc_ref[1:2, :] += dsc
        acc_ref[2:3, :] += dg
        acc_ref[3:4, :] += dgt

    return _call(
        body, name, (T // TM,),
        [_rows(TM, D), _rows(TM, D), _rows(TM, FF), _rows(TM, FF), _rows(TM, D), _const((8, D))] + [ANY] * (nwin + 1),
        [_rows(TM, D), _rows(TM, 2 * FF), _rows(TM, FF), _rows(TM, D), _const((8, D))],
        [_sds((T, D), F32), _sds((T, 2 * FF), BF16), _sds((T, FF), BF16), _sds((T, D), BF16), _sds((8, D), F32)],
        scratch=[pltpu.VMEM((D, 2 * FF), BF16), pltpu.VMEM((FF, D), BF16), pltpu.SemaphoreType.DMA((nwin + 1,))],
        vmem=VMEM_BIG, comm=comm,
    )(dh, h, a, b, f, vec, *wins, wout)


def _wgrad(x, y, name, tk, tn, tt, out_dtype=BF16, comm=None):
    T, K = x.shape
    N = y.shape[1]
    nt = T // tt

    def body(x_ref, y_ref, o_ref, acc_ref):
        t = pl.program_id(2)
        part = _dot_tn(x_ref[...], y_ref[...])

        @pl.when(t == 0)
        def _():
            acc_ref[...] = part

        @pl.when(t > 0)
        def _():
            acc_ref[...] += part

        @pl.when(t == nt - 1)
        def _():
            o_ref[...] = acc_ref[...].astype(out_dtype)

    (out,), c_outs = _call(
        body, name, (K // tk, N // tn, nt),
        [pl.BlockSpec((tt, tk), lambda i, j, t: (t, i)), pl.BlockSpec((tt, tn), lambda i, j, t: (t, j))],
        [pl.BlockSpec((tk, tn), lambda i, j, t: (i, j))], [_sds((K, N), out_dtype)],
        scratch=[pltpu.VMEM((tk, tn), F32)], vmem=VMEM_BIG, comm=comm,
    )(x, y)
    return out, c_outs


def _wgrad_scatter(x, y, name, tt, comm=None):
    T, K = x.shape
    n = y.shape[1] // 4
    nt = T // tt
    half = K // 2
    nc = 0 if comm is None else len(comm.inputs)

    def body(chip_ref, x_ref, y_ref, *refs):
        c_ins, refs = refs[:nc], refs[nc:]
        recv_ref, refs = refs[0], refs[1:]
        c_outs, refs = refs[:nc], refs[nc:]
        acc_ref, keep_ref, give_ref, take_ref, local_sem, give_sems, take_sems, send_sems, recv_sems = refs[:9]
        j, t = pl.program_id(0), pl.program_id(1)
        px, py, pc = _place()

        def hand_over(jj):
            return pltpu.make_async_remote_copy(
                src_ref=give_ref.at[jj], dst_ref=take_ref.at[jj], send_sem=give_sems.at[jj], recv_sem=take_sems.at[jj],
                device_id=(px, py, 1 - pc), device_id_type=MESH)

        def send(jj):
            m = jj + 1
            return pltpu.make_async_remote_copy(
                src_ref=keep_ref.at[jj], dst_ref=recv_ref.at[m], send_sem=send_sems.at[jj], recv_sem=recv_sems.at[jj],
                device_id=_chip_peer(px, py, pc, m), device_id_type=MESH)

        def add_sibling(jj):
            hand_over(jj).wait_recv()
            keep_ref[jj] = (keep_ref[jj].astype(F32) + take_ref[jj].astype(F32)).astype(BF16)

        if comm is not None:
            @pl.when(jnp.logical_and(j == 0, t == 0))
            def _():
                comm.start(c_ins, c_outs, refs[9:])

        part = _dot_tn(x_ref[...], y_ref[...])

        @pl.when(t == 0)
        def _():
            acc_ref[...] = part

        @pl.when(t > 0)
        def _():
            acc_ref[...] += part

        for jj in range(3):
            @pl.when(jnp.logical_and(j == jj + 1, t == nt // 2))
            def _():
                add_sibling(jj)
                send(jj).start()

        for jj in range(4):
            @pl.when(jnp.logical_and(j == jj, t == nt - 1))
            def _():
                keep_ref[jj] = acc_ref[pl.ds(pl.multiple_of(pc * half, 16), half), :].astype(BF16)
                give_ref[jj] = acc_ref[pl.ds(pl.multiple_of((1 - pc) * half, 16), half), :].astype(BF16)
                hand_over(jj).start()

        @pl.when(jnp.logical_and(j == 3, t == nt - 1))
        def _():
            add_sibling(3)
            own = pltpu.make_async_copy(keep_ref.at[3], recv_ref.at[0], local_sem.at[0])
            own.start()
            for jj in range(3):
                send(jj).wait_recv()
            for jj in range(3):
                send(jj).wait_send()
            for jj in range(4):
                hand_over(jj).wait_send()
            own.wait()
            if comm is not None:
                comm.wait(c_ins, c_outs, refs[9:])

    grid_spec = pltpu.PrefetchScalarGridSpec(
        num_scalar_prefetch=1, grid=(4, nt),
        in_specs=[pl.BlockSpec((tt, K), lambda j, t, chip: (t, 0)),
                  pl.BlockSpec((tt, n), lambda j, t, chip: (t, chip[0] ^ ((j + 1) & 3)))] + [ANY] * nc,
        out_specs=[ANY] * (1 + nc),
        scratch_shapes=[pltpu.VMEM((K, n), F32)] + [pltpu.VMEM((4, half, n), BF16)] * 3
        + [pltpu.SemaphoreType.DMA((1,))] + [pltpu.SemaphoreType.DMA((4,))] * 2 + [pltpu.SemaphoreType.DMA((3,))] * 2
        + ([] if comm is None else list(comm.sem_shapes)))
    px, py, _ = _place()
    res = pl.pallas_call(
        body, name=name, grid_spec=grid_spec,
        out_shape=[_sds((4, half, n), BF16)] + ([] if comm is None else list(comm.out_shapes)),
        compiler_params=pltpu.CompilerParams(dimension_semantics=("arbitrary", "arbitrary"), vmem_limit_bytes=VMEM_BIG),
    )((2 * px + py).astype(jnp.int32).reshape(1), x, y, *([] if comm is None else comm.inputs))
    return res[0], res[1:]


def _swap_halves(t):
    w = t.shape[1]
    lane = lax.broadcasted_iota(jnp.int32, t.shape, 1)
    return jnp.where(lane % HD < HD // 2, pltpu.roll(t, w - HD // 2, 1), pltpu.roll(t, HD // 2, 1))


def _rope(t, cos, sin_signed):
    c = jnp.tile(cos, (1, t.shape[1] // cos.shape[1]))
    s = jnp.tile(sin_signed, (1, t.shape[1] // sin_signed.shape[1]))
    return t * c + _swap_halves(t) * s


def _rope_bwd(dt, cos, sin_signed):
    c = jnp.tile(cos, (1, dt.shape[1] // cos.shape[1]))
    s = jnp.tile(sin_signed, (1, dt.shape[1] // sin_signed.shape[1]))
    return dt * c + _swap_halves(dt * s)


def _rm_spec(dil):
    return pl.BlockSpec((dil, TM // dil, GA), lambda i: (0, i, 0))


def _to_residues(t, dst_ref, scr_ref, dil):
    if dil == 1:
        dst_ref[0] = t.astype(dst_ref.dtype)
        return
    for j in range(GA // LANES):
        scr_ref[j] = t[:, j * LANES:(j + 1) * LANES]
    for r in range(dil):
        for j in range(GA // LANES):
            rows = scr_ref.at[j][pl.ds(r, TM // dil, stride=dil), :]
            dst_ref[r, :, j * LANES:(j + 1) * LANES] = rows.astype(dst_ref.dtype)


def _from_residues(src_ref, scr_ref, dil):
    if dil == 1:
        return src_ref[0].astype(F32)
    for r in range(dil):
        for j in range(GA // LANES):
            scr_ref.at[j][pl.ds(r, TM // dil, stride=dil), :] = src_ref[r, :, j * LANES:(j + 1) * LANES].astype(F32)
    return jnp.concatenate([scr_ref[j] for j in range(GA // LANES)], axis=1)


def _mix_proj(h, vec, win, cos, sin, comm=None):
    T = h.shape[0]

    def body(h_ref, vec_ref, win_hbm, cos_ref, sin_ref, u_ref, p_ref, *rest):
        qkv_refs, gates_ref, win_v, scr_ref, sems = rest[:3 * NG], rest[3 * NG], rest[3 * NG + 1], rest[3 * NG + 2], rest[3 * NG + 3]
        _load_once([(win_hbm, win_v)], sems)
        g, sh, sc = vec_ref[0:1, :], vec_ref[1:2, :], vec_ref[2:3, :]
        _, _, _, u = _norm_fwd(h_ref[...], g, sh, sc)
        ub = u.astype(BF16)
        u_ref[...] = ub
        p_ref[...] = _dot(ub, win_v[:, 0:PW])
        cos_t, sin_t = cos_ref[...], sin_ref[...]
        for j in range(3 * NG):
            col = PW + j * GA
            t = _dot(ub, win_v[:, col:col + GA])
            if j < 2 * NG:
                t = _rope(t, cos_t, sin_t)
            _to_residues(t, qkv_refs[j], scr_ref, DIL[j % NG])
        for j in range(GW // 512):
            col = PW + 3 * NG * GA + j * 512
            gates_ref[:, j * 512:(j + 1) * 512] = jax.nn.sigmoid(_dot(ub, win_v[:, col:col + 512])).astype(BF16)

    outs, c_outs = _call(
        body, "mix_proj", (T // TM,),
        [_rows(TM, D), _const((8, D)), ANY, _rows(TM, 128), _rows(TM, 128)],
        [_rows(TM, D), _rows(TM, PW)] + [_rm_spec(d) for d in DIL] * 3 + [_rows(TM, GW)],
        [_sds((T, D), BF16), _sds((T, PW), F32)] + [_sds((d, T // d, GA), BF16) for d in DIL] * 3 + [_sds((T, GW), BF16)],
        scratch=[pltpu.VMEM((D, INW), BF16), pltpu.VMEM((GA // LANES, TM, LANES), F32), pltpu.SemaphoreType.DMA((1,))],
        vmem=VMEM_BIG, comm=comm,
    )(h, vec, win, cos, sin)
    return (outs[0], outs[1], outs[2:2 + NG], outs[2 + NG:2 + 2 * NG], outs[2 + 2 * NG:2 + 3 * NG], outs[2 + 3 * NG]), c_outs


def _head_masks():
    lane_head = lax.broadcasted_iota(jnp.int32, (BLK, GA), 1) // HD
    return [lane_head == hd for hd in range(NH)]


def _expand_heads(t, hm):
    return jnp.concatenate([jnp.where(m, t, jnp.zeros_like(t)) for m in hm], axis=0)


def _collapse_heads(tb, hm):
    out = None
    for hd, m in enumerate(hm):
        part = jnp.where(m, tb[hd * BLK:(hd + 1) * BLK, :], 0.0)
        out = part if out is None else out + part
    return out


def _head_rows(t):
    return jnp.concatenate([t[:, hd * HD:hd * HD + 1] for hd in range(NH)], axis=0)


def _band_masks():
    a = lax.broadcasted_iota(jnp.int32, (NH * BLK, BLK), 0) & (BLK - 1)
    c = lax.broadcasted_iota(jnp.int32, (NH * BLK, BLK), 1)
    return c <= a, c >= a


def _attn_specs(nbt):
    cur = pl.BlockSpec((2 * BLK, GA), lambda i: (i, 0))
    prev = pl.BlockSpec((BLK, GA), lambda i: (jnp.maximum(2 * i - 1, 0), 0))
    nxt = pl.BlockSpec((BLK, GA), lambda i: (jnp.minimum(2 * i + 2, nbt - 1), 0))
    return cur, prev, nxt


def _attn_fwd(q, k, v, nb, name, comm=None):
    T = q.shape[0]
    nbt = T // BLK
    lo, hi = slice(0, BLK), slice(BLK, 2 * BLK)

    def block(qv, kc, kp, vc, vp, has_prev, hm):
        m_cur, m_prev = _band_masks()
        m_prev = jnp.logical_and(m_prev, has_prev)
        qb = _expand_heads(qv, hm)
        s_c = jnp.where(m_cur, _dot_nt(qb, kc) * SCALE, NEG)
        s_p = jnp.where(m_prev, _dot_nt(qb, kp) * SCALE, NEG)
        mx = jnp.maximum(jnp.max(s_c, axis=-1, keepdims=True), jnp.max(s_p, axis=-1, keepdims=True))
        e_c = jnp.exp(s_c - mx)
        e_p = jnp.exp(s_p - mx)
        l = jnp.sum(e_c, axis=-1, keepdims=True) + jnp.sum(e_p, axis=-1, keepdims=True)
        inv = 1.0 / l
        ob = _dot((e_c * inv).astype(BF16), vc) + _dot((e_p * inv).astype(BF16), vp)
        return _collapse_heads(ob, hm), _collapse_heads(jnp.broadcast_to(mx + jnp.log(l), (NH * BLK, GA)), hm)

    def body(q_ref, k_ref, kp_ref, v_ref, vp_ref, o_ref, lse_ref):
        b0 = 2 * pl.program_id(0)
        hm = _head_masks()
        o_ref[lo, :], lse_ref[lo, :] = block(q_ref[lo, :], k_ref[lo, :], kp_ref[...], v_ref[lo, :], vp_ref[...],
                                             (b0 & (nb - 1)) != 0, hm)
        o_ref[hi, :], lse_ref[hi, :] = block(q_ref[hi, :], k_ref[hi, :], k_ref[lo, :], v_ref[hi, :], v_ref[lo, :],
                                             ((b0 + 1) & (nb - 1)) != 0, hm)

    cur, prev, _ = _attn_specs(nbt)
    return _call(body, name, (nbt // 2,), [cur, cur, prev, cur, prev], [cur, cur],
                 [_sds((T, GA), F32), _sds((T, GA), F32)], comm=comm)(q, k, k, v, v)


def _attn_bwd(q, k, v, do, lse, e, nb, name, comm=None):
    T = q.shape[0]
    nbt = T // BLK

    lo, hi = slice(0, BLK), slice(BLK, 2 * BLK)

    def block(qv, kc, vc, dov, lsev, ev, kp, vp, qn, don, lsen, en, has_prev, has_next, hm):
        m_cur, m_band = _band_masks()
        m_prev = jnp.logical_and(m_band, has_prev)
        m_next = jnp.logical_and(m_band, has_next)
        qb, dob = _expand_heads(qv, hm), _expand_heads(dov, hm)
        lse_r, e_r = _head_rows(lsev), _head_rows(ev)
        p_c = jnp.where(m_cur, jnp.exp(_dot_nt(qb, kc) * SCALE - lse_r), 0.0)
        p_p = jnp.where(m_prev, jnp.exp(_dot_nt(qb, kp) * SCALE - lse_r), 0.0)
        ds_c = (p_c * (_dot_nt(dob, vc) + e_r)).astype(BF16)
        ds_p = (p_p * (_dot_nt(dob, vp) + e_r)).astype(BF16)
        dq = _collapse_heads((_dot(ds_c, kc) + _dot(ds_p, kp)) * SCALE, hm)
        qnb, donb = _expand_heads(qn, hm), _expand_heads(don, hm)
        p_n = jnp.where(m_next, jnp.exp(_dot_nt(qnb, kc) * SCALE - _head_rows(lsen)), 0.0)
        ds_n = (p_n * (_dot_nt(donb, vc) + _head_rows(en))).astype(BF16)
        dk = (_dot_tn(ds_c, qb) + _dot_tn(ds_n, qnb)) * SCALE
        dv = (_dot_tn(p_c.astype(BF16), dob) + _dot_tn(p_n.astype(BF16), donb)).astype(BF16)
        return dq, dk, dv

    def body(q_ref, k_ref, v_ref, do_ref, lse_ref, e_ref, kp_ref, vp_ref, qn_ref, don_ref, lsen_ref, en_ref,
             dq_ref, dk_ref, dv_ref):
        b0 = 2 * pl.program_id(0)
        hm = _head_masks()
        dq_ref[lo, :], dk_ref[lo, :], dv_ref[lo, :] = block(
            q_ref[lo, :], k_ref[lo, :], v_ref[lo, :], do_ref[lo, :], lse_ref[lo, :], e_ref[lo, :], kp_ref[...], vp_ref[...],
            q_ref[hi, :], do_ref[hi, :], lse_ref[hi, :], e_ref[hi, :],
            (b0 & (nb - 1)) != 0, ((b0 + 1) & (nb - 1)) != 0, hm)
        dq_ref[hi, :], dk_ref[hi, :], dv_ref[hi, :] = block(
            q_ref[hi, :], k_ref[hi, :], v_ref[hi, :], do_ref[hi, :], lse_ref[hi, :], e_ref[hi, :], k_ref[lo, :], v_ref[lo, :],
            qn_ref[...], don_ref[...], lsen_ref[...], en_ref[...],
            ((b0 + 1) & (nb - 1)) != 0, ((b0 + 2) & (nb - 1)) != 0, hm)

    cur, prev, nxt = _attn_specs(nbt)
    return _call(body, name, (nbt // 2,), [cur] * 6 + [prev, prev] + [nxt] * 4, [cur, cur, cur],
                 [_sds((T, GA), F32), _sds((T, GA), F32), _sds((T, GA), BF16)],
                 comm=comm)(q, k, v, do, lse, e, k, v, q, do, lse, e)


def _flat(t):
    return t.reshape(t.shape[0] * t.shape[1], t.shape[2])


def _by_residue(t, dil):
    return t.reshape(dil, t.shape[0] // dil, t.shape[1])


def _pool_consts(shape, row0):
    lane = lax.broadcasted_iota(jnp.int32, shape, 1)
    t = lax.broadcasted_iota(jnp.int32, shape, 0) + row0
    grp = lane // (PW // len(POOL_WINDOWS))
    win = jnp.where(grp == 0, POOL_WINDOWS[0], jnp.where(grp == 1, POOL_WINDOWS[1],
                    jnp.where(grp == 2, POOL_WINDOWS[2], POOL_WINDOWS[3])))
    cnt = jnp.minimum(t + 1, win).astype(F32)
    return grp, cnt


def _window_sums(ext_ref, base, step, tm):
    outs, run = [], None
    for j in range(POOL_WINDOWS[-1]):
        sl = ext_ref[pl.ds(base + step * j, tm), :]
        run = sl if run is None else run + sl
        if j + 1 in POOL_WINDOWS:
            outs.append(run)
    return outs


def _select_group(grp, vals):
    return jnp.where(grp == 0, vals[0], jnp.where(grp == 1, vals[1], jnp.where(grp == 2, vals[2], vals[3])))


def _pool_d(pc_ref, pp_ref, ext_ref, i, tm):
    ext_ref[0:HALO, :] = jnp.where(i > 0, pp_ref[tm - HALO:tm, :], 0.0)
    ext_ref[HALO:HALO + tm, :] = pc_ref[...]
    grp, cnt = _pool_consts((tm, PW), i * tm)
    sums = _window_sums(ext_ref, HALO, -1, tm)
    return _select_group(grp, sums) / cnt - pc_ref[...]


def _group_weights(ls):
    mx = jnp.maximum(jnp.maximum(ls[0], ls[1]), ls[2])
    es = [jnp.exp(l - mx) for l in ls]
    inv = 1.0 / (es[0] + es[1] + es[2])
    return [e * inv for e in es]


def _mix_merge(h, vec, p, os, lses, gates, wp_bd, pscale, wpb, wab, wout):
    T = h.shape[0]

    def body(h_ref, vec_ref, pc_ref, pp_ref, o0, o1, o2, l0, l1, l2, gates_ref, wp_ref, ps_ref, wpb_ref, wab_ref, wout_ref,
             ho_ref, yp_ref, ya_ref, mg_ref, mo_ref, d_ref, ext_ref, scr_ref):
        i = pl.program_id(0)
        gt = vec_ref[3:4, :]
        d = _pool_d(pc_ref, pp_ref, ext_ref, i, TM).astype(BF16)
        d_ref[...] = d
        ypool = (_dot(d, wp_ref[...]) * ps_ref[0:1, :]).astype(BF16)
        yp_ref[...] = ypool
        w = _group_weights([_from_residues(r, scr_ref, dl) for r, dl in zip((l0, l1, l2), DIL)])
        yattn = None
        for wg, o_ref, dl in zip(w, (o0, o1, o2), DIL):
            part = wg * _from_residues(o_ref, scr_ref, dl)
            yattn = part if yattn is None else yattn + part
        yattn = yattn.astype(BF16)
        ya_ref[...] = yattn
        merged = (gates_ref[:, 0:D].astype(F32) * _dot(ypool, wpb_ref[...])
                  + gates_ref[:, D:GW].astype(F32) * _dot(yattn, wab_ref[...])).astype(BF16)
        mg_ref[...] = merged
        mo = _dot(merged, wout_ref[...])
        mo_ref[...] = mo.astype(BF16)
        ho_ref[...] = h_ref[...] + gt * mo

    prev = pl.BlockSpec((TM, PW), lambda i: (jnp.maximum(i - 1, 0), 0))
    return _call(
        body, "mix_merge", (T // TM,),
        [_rows(TM, D), _const((8, D)), _rows(TM, PW), prev] + [_rm_spec(dl) for dl in DIL] * 2 + [_rows(TM, GW), _const((PW, PW)),
         _const((8, PW)), _const((PW, D)), _const((GA, D)), _const((D, D))],
        [_rows(TM, D), _rows(TM, PW), _rows(TM, GA), _rows(TM, D), _rows(TM, D), _rows(TM, PW)],
        [_sds((T, D), F32), _sds((T, PW), BF16), _sds((T, GA), BF16), _sds((T, D), BF16), _sds((T, D), BF16), _sds((T, PW), BF16)],
        scratch=[pltpu.VMEM((TM + HALO, PW), F32), pltpu.VMEM((GA // LANES, TM, LANES), F32)],
        vmem=VMEM_BIG,
    )(h, vec, p, p, *os, *lses, gates, wp_bd, pscale, wpb, wab, wout)[0]


def _mix_bwd_a(dh, vec, mixout, gates, ypool, yattn, dpool, os, lses, wp_bd, pscale, wpb, wab, wout, ones_bd, comm=None):
    T = dh.shape[0]

    def body(dh_ref, vec_ref, mo_ref, gates_ref, yp_ref, ya_ref, d_ref, o0, o1, o2, l0, l1, l2,
             wp_ref, ps_ref, wpb_ref, wab_ref, wout_ref, ones_ref,
             dmo_ref, dp_ref, da_ref, dgates_ref, do0, do1, do2, e0, e1, e2, dd_ref, dyp_ref, acc_ref, acc2_ref, scr_ref):
        _zero_first(acc_ref)
        _zero_first(acc2_ref)
        gt = vec_ref[3:4, :]
        dho = dh_ref[...]
        acc_ref[3:4, :] += _colsum(dho * mo_ref[...].astype(F32))
        dmo = (gt * dho).astype(BF16)
        dmo_ref[...] = dmo
        dmerged = _dot_nt(dmo, wout_ref[...])
        gp = gates_ref[:, 0:D].astype(F32)
        ga = gates_ref[:, D:GW].astype(F32)
        bp = _dot(yp_ref[...], wpb_ref[...])
        ba = _dot(ya_ref[...], wab_ref[...])
        dgates_ref[:, 0:D] = (dmerged * bp * gp * (1.0 - gp)).astype(BF16)
        dgates_ref[:, D:GW] = (dmerged * ba * ga * (1.0 - ga)).astype(BF16)
        dbp = (dmerged * gp).astype(BF16)
        dba = (dmerged * ga).astype(BF16)
        dp_ref[...] = dbp
        da_ref[...] = dba
        dypool = _dot_nt(dbp, wpb_ref[...])
        ypre = _dot(d_ref[...], wp_ref[...])
        acc2_ref[0:1, :] += _colsum(dypool * ypre)
        dyp = (dypool * ps_ref[0:1, :]).astype(BF16)
        dyp_ref[...] = dyp
        dd_ref[...] = _dot_nt(dyp, wp_ref[...])
        dya = _dot_nt(dba, wab_ref[...])
        w = _group_weights([_from_residues(r, scr_ref, dl) for r, dl in zip((l0, l1, l2), DIL)])
        ya = None
        for wg, o_ref, dl in zip(w, (o0, o1, o2), DIL):
            part = wg * _from_residues(o_ref, scr_ref, dl)
            ya = part if ya is None else ya + part
        prod = dya * ya
        hi = prod.astype(BF16)
        lo = (prod - hi.astype(F32)).astype(BF16)
        tot = _dot(hi, ones_ref[...]) + _dot(lo, ones_ref[...])
        for wg, do_ref, e_ref, dl in zip(w, (do0, do1, do2), (e0, e1, e2), DIL):
            _to_residues(wg * dya, do_ref, scr_ref, dl)
            _to_residues(-wg * tot, e_ref, scr_ref, dl)

    return _call(
        body, "mix_bwd_a", (T // TM,),
        [_rows(TM, D), _const((8, D)), _rows(TM, D), _rows(TM, GW), _rows(TM, PW), _rows(TM, GA), _rows(TM, PW)]
        + [_rm_spec(dl) for dl in DIL] * 2
        + [_const((PW, PW)), _const((8, PW)), _const((PW, D)), _const((GA, D)), _const((D, D)), _const((GA, GA))],
        [_rows(TM, D)] * 3 + [_rows(TM, GW)] + [_rm_spec(dl) for dl in DIL] * 2
        + [_rows(TM, PW), _rows(TM, PW), _const((8, D)), _const((8, PW))],
        [_sds((T, D), BF16)] * 3 + [_sds((T, GW), BF16)] + [_sds((dl, T // dl, GA), BF16) for dl in DIL]
        + [_sds((dl, T // dl, GA), F32) for dl in DIL]
        + [_sds((T, PW), F32), _sds((T, PW), BF16), _sds((8, D), F32), _sds((8, PW), F32)],
        scratch=[pltpu.VMEM((GA // LANES, TM, LANES), F32)],
        vmem=VMEM_BIG, comm=comm,
    )(dh, vec, mixout, gates, ypool, yattn, dpool, *os, *lses, wp_bd, pscale, wpb, wab, wout, ones_bd)


def _mix_bwd_b(dh, h, vec, dd, dqs, dks, dvs, dgates, cos, sin, win):
    T = h.shape[0]
    nt = T // TM

    def body(dh_ref, h_ref, vec_ref, ddc_ref, ddn_ref, *rest):
        qk_refs, dv_refs = rest[:2 * NG], rest[2 * NG:3 * NG]
        dgates_ref, cos_ref, sin_ref, win_hbm, dhi_ref, dproj_ref, acc_ref, win_v, ext_ref, scr_ref, sems = rest[3 * NG:]
        i = pl.program_id(0)
        _load_once([(win_hbm, win_v)], sems)
        _zero_first(acc_ref)
        g, sh, sc = vec_ref[0:1, :], vec_ref[1:2, :], vec_ref[2:3, :]
        grp, cnt = _pool_consts((TM, PW), i * TM)
        _, cnt_n = _pool_consts((HALO, PW), (i + 1) * TM)
        ext_ref[0:TM, :] = ddc_ref[...] / cnt
        ext_ref[TM:TM + HALO, :] = jnp.where(i < nt - 1, ddn_ref[0:HALO, :] / cnt_n, 0.0)
        dp = _select_group(grp, _window_sums(ext_ref, 0, 1, TM)) - ddc_ref[...]
        dproj_ref[:, 0:PW] = dp.astype(BF16)
        cos_t, sin_t = cos_ref[...], sin_ref[...]
        for j in range(2 * NG):
            col = PW + j * GA
            dt = _from_residues(qk_refs[j], scr_ref, DIL[j % NG])
            dproj_ref[:, col:col + GA] = _rope_bwd(dt, cos_t, sin_t).astype(BF16)
        for j in range(NG):
            col = PW + (2 * NG + j) * GA
            dproj_ref[:, col:col + GA] = _from_residues(dv_refs[j], scr_ref, DIL[j]).astype(BF16)
        dproj_ref[:, PW + 3 * NG * GA:INW] = dgates_ref[...]
        du = None
        for j in range(INW // 512):
            part = _dot_nt(dproj_ref[:, j * 512:(j + 1) * 512], win_v[:, j * 512:(j + 1) * 512])
            du = part if du is None else du + part
        xh, r, n, _ = _norm_fwd(h_ref[...], g, sh, sc)
        dhn, dsh, dsc, dg = _norm_bwd(du, xh, r, n, g, sc)
        dhi_ref[...] = dh_ref[...] + dhn
        acc_ref[0:1, :] += dsh
        acc_ref[1:2, :] += dsc
        acc_ref[2:3, :] += dg

    nxt = pl.BlockSpec((TM, PW), lambda i: (jnp.minimum(i + 1, nt - 1), 0))
    return _call(
        body, "mix_bwd_b", (nt,),
        [_rows(TM, D), _rows(TM, D), _const((8, D)), _rows(TM, PW), nxt] + [_rm_spec(dl) for dl in DIL] * 3
        + [_rows(TM, GW), _rows(TM, 128), _rows(TM, 128), ANY],
        [_rows(TM, D), _rows(TM, INW), _const((8, D))],
        [_sds((T, D), F32), _sds((T, INW), BF16), _sds((8, D), F32)],
        scratch=[pltpu.VMEM((D, INW), BF16), pltpu.VMEM((TM + HALO, PW), F32), pltpu.VMEM((GA // LANES, TM, LANES), F32),
                 pltpu.SemaphoreType.DMA((1,))],
        vmem=VMEM_BIG,
    )(dh, h, vec, dd, dd, *dqs, *dks, *dvs, dgates, cos, sin, win)[0]


def _ada_fwd(c_all, w_shard, b_shard):
    n = w_shard.shape[1]

    def body(c_ref, w_ref, b_ref, o_ref):
        cv = c_ref[...]
        cond = (cv * jax.nn.sigmoid(cv)).astype(BF16)
        o_ref[...] = _dot(cond, w_ref[...].astype(BF16)) + b_ref[...]

    tn = n // 3
    return pl.pallas_call(
        body, name="ada_fwd", grid=(3,),
        in_specs=[pl.BlockSpec((8, D), lambda j: (0, 0)), pl.BlockSpec((D, tn), lambda j: (0, j)), pl.BlockSpec((1, tn), lambda j: (0, j))],
        out_specs=pl.BlockSpec((8, tn), lambda j: (0, j)), out_shape=_sds((8, n), F32),
        compiler_params=pltpu.CompilerParams(dimension_semantics=("arbitrary",)),
    )(c_all, w_shard, b_shard)


def _ada_bwd(c_all, dmod_shard):
    n = dmod_shard.shape[1]

    def body(c_ref, d_ref, o_ref):
        cv = c_ref[...]
        cond = (cv * jax.nn.sigmoid(cv)).astype(BF16)
        o_ref[...] = _dot_tn(cond, d_ref[...].astype(BF16))

    tn = n // 3
    return pl.pallas_call(
        body, name="ada_bwd", grid=(3,),
        in_specs=[pl.BlockSpec((8, D), lambda j: (0, 0)), pl.BlockSpec((8, tn), lambda j: (0, j))],
        out_specs=pl.BlockSpec((D, tn), lambda j: (0, j)), out_shape=_sds((D, n), F32),
        compiler_params=pltpu.CompilerParams(dimension_semantics=("arbitrary",)),
    )(c_all, dmod_shard)


def _adam_math(w, g, m, v):
    m2 = B1 * m + (1.0 - B1) * g
    v2 = B2 * v + (1.0 - B2) * (g * g)
    m_hat = m2 / (1.0 - B1 ** STEP)
    v_hat = v2 / (1.0 - B2 ** STEP)
    delta = -LR * (m_hat / (jnp.sqrt(v_hat) + AEPS) + WD * w)
    return delta, m2, v2


def _adam(w, m, v, parts, name):
    R, C = w.shape
    tr = R
    for cand in (128, 64, 32, 16, 8):
        if R % cand == 0:
            tr = cand
            break
    np_ = len(parts)

    def body(w_ref, m_ref, v_ref, *rest):
        p_refs, (g_ref, d_ref, m2_ref, v2_ref) = rest[:np_], rest[np_:]
        g = p_refs[0][...]
        for pr in p_refs[1:]:
            g = g + pr[...]
        delta, m2, v2 = _adam_math(w_ref[...], g, m_ref[...], v_ref[...])
        g_ref[...] = g
        d_ref[...] = delta
        m2_ref[...] = m2
        v2_ref[...] = v2

    spec = pl.BlockSpec((tr, C), lambda i: (i, 0))
    return pl.pallas_call(
        body, name=name, grid=(R // tr,), in_specs=[spec] * (3 + np_), out_specs=[spec] * 4,
        out_shape=[_sds((R, C), F32)] * 4,
        compiler_params=pltpu.CompilerParams(dimension_semantics=("arbitrary",), vmem_limit_bytes=VMEM_BIG),
    )(w, m, v, *parts)


def _adam_halves(w, m, v, mine, other, name):
    R, C = w.shape
    tr = 128
    nh = R // 2 // tr

    def body(c_ref, w_ref, m_ref, v_ref, mine_ref, other_ref, g_ref, d_ref, m2_ref, v2_ref):
        i = pl.program_id(0)
        in_mine = jnp.logical_and(i >= c_ref[0] * nh, i < (c_ref[0] + 1) * nh)
        g = jnp.where(in_mine, mine_ref[...], other_ref[...])
        delta, m2, v2 = _adam_math(w_ref[...], g, m_ref[...], v_ref[...])
        g_ref[...] = g
        d_ref[...] = delta
        m2_ref[...] = m2
        v2_ref[...] = v2

    spec = pl.BlockSpec((tr, C), lambda i, c: (i, 0))
    grid_spec = pltpu.PrefetchScalarGridSpec(
        num_scalar_prefetch=1, grid=(R // tr,),
        in_specs=[spec] * 3 + [pl.BlockSpec((tr, C), lambda i, c: (jnp.clip(i - c[0] * nh, 0, nh - 1), 0)),
                               pl.BlockSpec((tr, C), lambda i, c: (jnp.clip(i - (1 - c[0]) * nh, 0, nh - 1), 0))],
        out_specs=[spec] * 4)
    return pl.pallas_call(
        body, name=name, grid_spec=grid_spec, out_shape=[_sds((R, C), F32)] * 4,
        compiler_params=pltpu.CompilerParams(dimension_semantics=("arbitrary",), vmem_limit_bytes=VMEM_BIG),
    )(lax.axis_index("c").astype(jnp.int32).reshape(1), w, m, v, mine, other)


def _adam_small(w, m, v, gathered):
    P = w.shape[1]

    def body(w_ref, m_ref, v_ref, ga_ref, g_ref, d_ref, m2_ref, v2_ref):
        g = ga_ref[0]
        for dev in range(1, 8):
            g = g + ga_ref[dev]
        delta, m2, v2 = _adam_math(w_ref[...], g, m_ref[...], v_ref[...])
        g_ref[...] = g
        d_ref[...] = delta
        m2_ref[...] = m2
        v2_ref[...] = v2

    return pl.pallas_call(body, name="adam_small", out_shape=[_sds((1, P), F32)] * 4)(w, m, v, gathered)


def _sum4(blocks, name):
    _, R, C = blocks.shape
    tr = R
    for cand in (256, 128, 64, 32, 16):
        if R % cand == 0:
            tr = cand
            break

    def body(r_ref, out_ref):
        out_ref[...] = ((r_ref[0].astype(F32) + r_ref[1].astype(F32)) + r_ref[2].astype(F32)) + r_ref[3].astype(F32)

    return pl.pallas_call(
        body, name=name, grid=(R // tr,),
        in_specs=[pl.BlockSpec((4, tr, C), lambda i: (0, i, 0))],
        out_specs=pl.BlockSpec((tr, C), lambda i: (i, 0)), out_shape=_sds((R, C), F32),
        compiler_params=pltpu.CompilerParams(dimension_semantics=("arbitrary",)),
    )(blocks)


def _place():
    return lax.axis_index("x"), lax.axis_index("y"), lax.axis_index("c")


def _gather_small(v):
    R, P = v.shape

    def body(v_ref, out_ref, send_sems, recv_sems):
        x, y, c = _place()
        me = 4 * x + 2 * y + c
        out_ref[me] = v_ref[...]
        copies = []
        for m in range(1, 8):
            peer = (x ^ (m >> 2), y ^ ((m >> 1) & 1), c ^ (m & 1))
            copies.append(pltpu.make_async_remote_copy(
                src_ref=v_ref, dst_ref=out_ref.at[me], send_sem=send_sems.at[m - 1], recv_sem=recv_sems.at[m - 1],
                device_id=peer, device_id_type=MESH))
        for cp in copies:
            cp.start()
        for m in range(1, 8):
            src = 4 * (x ^ (m >> 2)) + 2 * (y ^ ((m >> 1) & 1)) + (c ^ (m & 1))
            pltpu.make_async_remote_copy(
                src_ref=v_ref, dst_ref=out_ref.at[src], send_sem=send_sems.at[m - 1], recv_sem=recv_sems.at[m - 1],
                device_id=(x, y, c), device_id_type=MESH).wait_recv()
        for cp in copies:
            cp.wait_send()

    vm = pl.BlockSpec(memory_space=pltpu.VMEM)
    return pl.pallas_call(
        body, name="gather_small", in_specs=[vm], out_specs=vm, out_shape=_sds((8, R, P), F32),
        scratch_shapes=[pltpu.SemaphoreType.DMA((7,)), pltpu.SemaphoreType.DMA((7,))],
    )(v)


def _chip_peer(x, y, c, m):
    return (x ^ (m >> 1), y ^ (m & 1), c)


def _shard_ref(ref, axis, k, n):
    start = pl.multiple_of(k * n, 128 if axis == 1 else 16)
    return ref.at[:, pl.ds(start, n)] if axis == 1 else ref.at[pl.ds(start, n), :]


def _half_rows(ref, axis, k, n, hc):
    if axis == 1:
        half = ref.shape[0] // 2
        return ref.at[pl.ds(pl.multiple_of(hc * half, 16), half), pl.ds(pl.multiple_of(k * n, 128), n)]
    half = n // 2
    return ref.at[pl.ds(pl.multiple_of(k * n + hc * half, 16), half), :]


class _GatherPlan:
    def __init__(self, shards, axes):
        self.inputs, self.axes, nw = list(shards), list(axes), len(shards)
        self.out_shapes = [_sds((s.shape[0] * (4 if ax == 0 else 1), s.shape[1] * (4 if ax == 1 else 1)), BF16)
                           for s, ax in zip(shards, axes)]
        self.sem_shapes = [pltpu.SemaphoreType.DMA((nw,))] + [pltpu.SemaphoreType.DMA((nw, 3))] * 4

    def _copies(self, ins, outs, sems):
        local_sems, send_sems, recv_sems, pass_sems, got_sems = sems
        x, y, c = _place()
        k = 2 * x + y
        local, sends, arrivals, passes, handed = [], [], [], [], []
        for j, ax in enumerate(self.axes):
            n = ins[j].shape[ax]
            half = ins[j].shape[0] // 2
            local.append(pltpu.make_async_copy(ins[j], _shard_ref(outs[j], ax, k, n), local_sems.at[j]))
            my_half = ins[j].at[pl.ds(pl.multiple_of(c * half, 16), half), :]
            for m in range(1, 4):
                sends.append(pltpu.make_async_remote_copy(
                    src_ref=my_half, dst_ref=_half_rows(outs[j], ax, k, n, c), send_sem=send_sems.at[j, m - 1],
                    recv_sem=recv_sems.at[j, m - 1], device_id=_chip_peer(x, y, c, m), device_id_type=MESH))
                theirs = _half_rows(outs[j], ax, k ^ m, n, c)
                arrivals.append(pltpu.make_async_remote_copy(
                    src_ref=my_half, dst_ref=theirs, send_sem=send_sems.at[j, m - 1], recv_sem=recv_sems.at[j, m - 1],
                    device_id=(x, y, c), device_id_type=MESH))
                passes.append(pltpu.make_async_remote_copy(
                    src_ref=theirs, dst_ref=theirs, send_sem=pass_sems.at[j, m - 1], recv_sem=got_sems.at[j, m - 1],
                    device_id=(x, y, 1 - c), device_id_type=MESH))
                other = _half_rows(outs[j], ax, k ^ m, n, 1 - c)
                handed.append(pltpu.make_async_remote_copy(
                    src_ref=other, dst_ref=other, send_sem=pass_sems.at[j, m - 1], recv_sem=got_sems.at[j, m - 1],
                    device_id=(x, y, c), device_id_type=MESH))
        return local, sends, arrivals, passes, handed

    def start(self, ins, outs, sems):
        local, sends, _, _, _ = self._copies(ins, outs, sems)
        for cp in local + sends:
            cp.start()

    def wait(self, ins, outs, sems):
        local, sends, arrivals, passes, handed = self._copies(ins, outs, sems)
        for arrived, onward in zip(arrivals, passes):
            arrived.wait_recv()
            onward.start()
        for cp in handed:
            cp.wait_recv()
        for cp in sends + passes:
            cp.wait_send()
        for cp in local:
            cp.wait()


class _ScatterPlan:
    def __init__(self, grads, axes):
        self.inputs, self.axes, nw = list(grads), list(axes), len(grads)
        self.shard_shapes = [(g.shape[0] // (4 if ax == 0 else 1), g.shape[1] // (4 if ax == 1 else 1))
                             for g, ax in zip(grads, axes)]
        self.out_shapes = [_sds((4,) + s, BF16) for s in self.shard_shapes]
        self.sem_shapes = [pltpu.SemaphoreType.DMA((nw,)), pltpu.SemaphoreType.DMA((nw, 3)), pltpu.SemaphoreType.DMA((nw, 3))]

    def _copies(self, ins, outs, sems):
        local_sems, send_sems, recv_sems = sems
        x, y, c = _place()
        k = 2 * x + y
        local, remote, arrivals = [], [], []
        for j, ax in enumerate(self.axes):
            n = self.shard_shapes[j][ax]
            local.append(pltpu.make_async_copy(_shard_ref(ins[j], ax, k, n), outs[j].at[0], local_sems.at[j]))
            for m in range(1, 4):
                remote.append(pltpu.make_async_remote_copy(
                    src_ref=_shard_ref(ins[j], ax, k ^ m, n), dst_ref=outs[j].at[m],
                    send_sem=send_sems.at[j, m - 1], recv_sem=recv_sems.at[j, m - 1],
                    device_id=_chip_peer(x, y, c, m), device_id_type=MESH))
                arrivals.append(pltpu.make_async_remote_copy(
                    src_ref=_shard_ref(ins[j], ax, k, n), dst_ref=outs[j].at[m],
                    send_sem=send_sems.at[j, m - 1], recv_sem=recv_sems.at[j, m - 1],
                    device_id=(x, y, c), device_id_type=MESH))
        return local, remote, arrivals

    def start(self, ins, outs, sems):
        local, remote, _ = self._copies(ins, outs, sems)
        for cp in local + remote:
            cp.start()

    def wait(self, ins, outs, sems):
        local, remote, arrivals = self._copies(ins, outs, sems)
        for cp in arrivals:
            cp.wait_recv()
        for cp in remote:
            cp.wait_send()
        for cp in local:
            cp.wait()


def _run_plan(plan, name):
    nc = len(plan.inputs)

    def body(*refs):
        ins, outs, sems = refs[:nc], refs[nc:2 * nc], refs[2 * nc:]
        plan.start(ins, outs, sems)
        plan.wait(ins, outs, sems)

    return pl.pallas_call(body, name=name, in_specs=[ANY] * nc, out_specs=[ANY] * nc, out_shape=list(plan.out_shapes),
                          scratch_shapes=list(plan.sem_shapes))(*plan.inputs)


def _swap_sibling(parts):
    nw = len(parts)

    def body(*refs):
        ins, outs = refs[:nw], refs[nw:2 * nw]
        send_sems, recv_sems = refs[2 * nw:]
        x, y, c = _place()
        copies = [pltpu.make_async_remote_copy(
            src_ref=ins[j], dst_ref=outs[j], send_sem=send_sems.at[j], recv_sem=recv_sems.at[j],
            device_id=(x, y, 1 - c), device_id_type=MESH) for j in range(nw)]
        for cp in copies:
            cp.start()
        for cp in copies:
            cp.wait()

    return pl.pallas_call(
        body, name="swap_sibling", in_specs=[ANY] * nw, out_specs=[ANY] * nw,
        out_shape=[_sds(p.shape, p.dtype) for p in parts],
        scratch_shapes=[pltpu.SemaphoreType.DMA((nw,)), pltpu.SemaphoreType.DMA((nw,))],
    )(*parts)


BIG = ("w_ffn1_in", "w_ffn1_out", "w_in", "w_pool_branch", "w_attn_branch", "w_out", "w_ffn2_in", "w_ffn2_out")
BIG_AXIS = {"w_ffn1_in": 1, "w_ffn1_out": 0, "w_in": 1, "w_pool_branch": 1, "w_attn_branch": 1, "w_out": 0,
            "w_ffn2_in": 1, "w_ffn2_out": 0}


class _Sharded:
    fused_scatter = True

    def __init__(self, shards):
        self.shards, self.full, self.recv = shards, {}, {}

    def gather_plan(self, names):
        return _GatherPlan([self.shards[n] for n in names], [BIG_AXIS[n.split("/")[0]] for n in names])

    def gather_now(self, names):
        self.gathered(names, _run_plan(self.gather_plan(names), "gather_" + names[0]))

    def gathered(self, names, outs):
        self.full.update(zip(names, outs))

    def scatter_plan(self, names, grads):
        return _ScatterPlan([grads[n] for n in names], [BIG_AXIS[n] for n in names])

    def scatter_now(self, names, grads):
        self.scattered(names, _run_plan(self.scatter_plan(names, grads), "scatter_" + names[0]))

    def scattered(self, names, outs):
        self.recv.update(zip(names, outs))


class _Whole:
    fused_scatter = False

    def __init__(self, full):
        self.full, self.recv = dict(full), {}

    def gather_plan(self, names):
        return None

    def gather_now(self, names):
        pass

    def gathered(self, names, outs):
        pass

    def scatter_plan(self, names, grads):
        return None

    def scatter_now(self, names, grads):
        pass

    def scattered(self, names, outs):
        pass


def _vec(rows):
    pad = [jnp.zeros((1, D), F32)] * (8 - len(rows))
    return jnp.concatenate([r.reshape(1, D) for r in rows] + pad, axis=0)


def _block_diag(w_pool):
    n, c = w_pool.shape[0], w_pool.shape[1]
    eye = jnp.eye(n, dtype=w_pool.dtype)
    return (eye[:, None, :, None] * w_pool[:, :, None, :]).reshape(n * c, n * c)


def _example_step(x, tgt, positions, mod, gains, w_pool, pool_scale, ws):
    T = x.shape[0]
    assert (T // BLK // DIL[-1]) & (T // BLK // DIL[-1] - 1) == 0, "blocks per sequence must be a power of two"
    sh1, sc1, gt1, sh2, sc2, gt2, sh3, sc3, gt3 = [mod[j * D:(j + 1) * D] for j in range(NMOD)]
    g1, g2, g3, gf = gains
    vec1, vec2, vec3 = _vec([g1, sh1, sc1, gt1]), _vec([g2, sh2, sc2, gt2]), _vec([g3, sh3, sc3, gt3])
    inv_freq = 10000.0 ** (-jnp.arange(0, HD, 2, dtype=F32) / HD)
    ang = positions.astype(F32)[:, None] * inv_freq
    cos = jnp.tile(jnp.cos(ang), (1, 4))
    sin = jnp.tile(jnp.concatenate([-jnp.sin(ang), jnp.sin(ang)], axis=1), (1, 2))
    wp_bd = _block_diag(w_pool).astype(BF16)
    ones_bd = _block_diag(jnp.ones((NH, HD, HD), F32)).astype(BF16)
    ps = jnp.concatenate([pool_scale.reshape(1, PW), jnp.zeros((7, PW), F32)], axis=0)
    wb = ws.full

    ws.gather_now(["w_ffn1_in", "w_ffn1_out"])
    mixw = ["w_in", "w_pool_branch", "w_attn_branch", "w_out"]
    (h1, u1, a1, b1, f1), got = _ffn_fwd(x, vec1, [wb["w_ffn1_in"]], wb["w_ffn1_out"], "ffn1_fwd", ws.gather_plan(mixw))
    ws.gathered(mixw, got)
    (u2, p, qs, ks, vs, gates), got = _mix_proj(h1, vec2, wb["w_in"], cos, sin, ws.gather_plan(["w_ffn2_in/0"]))
    ws.gathered(["w_ffn2_in/0"], got)
    qs, ks, vs = [_flat(t) for t in qs], [_flat(t) for t in ks], [_flat(t) for t in vs]
    nbs = [T // d // BLK for d in DIL]
    os, lses = [], []
    for gi, riders in enumerate((["w_ffn2_out"], ["w_ffn2_in/1"], None)):
        (o, lse), got = _attn_fwd(qs[gi], ks[gi], vs[gi], nbs[gi], f"attn_fwd{gi}", riders and ws.gather_plan(riders))
        ws.gathered(riders or [], got)
        os.append(o)
        lses.append(lse)
    win3 = [wb["w_ffn2_in/0"], wb["w_ffn2_in/1"]] if "w_ffn2_in/0" in wb else [wb["w_ffn2_in"]]
    os_r = [_by_residue(t, d) for t, d in zip(os, DIL)]
    lses_r = [_by_residue(t, d) for t, d in zip(lses, DIL)]
    h2, ypool, yattn, merged, mixout, dpool = _mix_merge(
        h1, vec2, p, os_r, lses_r, gates, wp_bd, ps, wb["w_pool_branch"], wb["w_attn_branch"], wb["w_out"])
    (dh3, u3, a3, b3, f3, lacc), _ = _ffn_fwd(h2, vec3, win3, wb["w_ffn2_out"], "ffn2_fwd", head=(tgt, _vec([gf])))
    loss = 0.5 * jnp.sum(lacc[0]) / D

    grads = {}

    def wgrad_cols(name, xx, yy, riders):
        plan = ws.scatter_plan(riders, grads) if riders else None
        if ws.fused_scatter:
            blocks, got = _wgrad_scatter(xx, yy, "wg_" + name, min(2048, T // 2), comm=plan)
            ws.scattered([name], [blocks])
        else:
            grads[name], got = _wgrad(xx, yy, "wg_" + name, D, 512, 1024, comm=plan)
        ws.scattered(riders, got)

    (dh2, dab3, s3, df3, acc3), _ = _ffn_bwd(dh3, h2, a3, b3, f3, vec3, win3, wb["w_ffn2_out"], "ffn2_bwd")
    grads["w_ffn2_out"], _ = _wgrad(s3, df3, "wg_ffn2_out", FC, 512, 1024)
    wgrad_cols("w_ffn2_in", u3, dab3, ["w_ffn2_out"])
    (dmo, dbp, dba, dgates, do0, do1, do2, e0, e1, e2, dd, dyp, acc2a, accps), _ = _mix_bwd_a(
        dh2, vec2, mixout, gates, ypool, yattn, dpool, os_r, lses_r, wp_bd, ps,
        wb["w_pool_branch"], wb["w_attn_branch"], wb["w_out"], ones_bd)
    grads["w_out"], _ = _wgrad(merged, dmo, "wg_out", D, 512, 1024)
    grads["w_pool_branch"], _ = _wgrad(ypool, dbp, "wg_pool_branch", PW, 512, 1024)
    grads["w_attn_branch"], _ = _wgrad(yattn, dba, "wg_attn_branch", GA, 512, 1024)
    gwp, _ = _wgrad(dpool, dyp, "wg_pool", PW, PW, 1024, out_dtype=F32)
    n = len(POOL_WINDOWS)
    c = PW // n
    grad_w_pool = jnp.stack([gwp[j * c:(j + 1) * c, j * c:(j + 1) * c] for j in range(n)], axis=0)
    small3 = ["w_out", "w_pool_branch", "w_attn_branch"]
    dqs, dks, dvs = [], [], []
    for gi, (do, e) in enumerate(((do0, e0), (do1, e1), (do2, e2))):
        plan = ws.scatter_plan(small3, grads) if gi == 0 else None
        (dq, dk, dv), got = _attn_bwd(qs[gi], ks[gi], vs[gi], _flat(do), lses[gi], _flat(e), nbs[gi], f"attn_bwd{gi}", plan)
        if gi == 0:
            ws.scattered(small3, got)
        dqs.append(_by_residue(dq, DIL[gi]))
        dks.append(_by_residue(dk, DIL[gi]))
        dvs.append(_by_residue(dv, DIL[gi]))
    dh1, dproj, acc2b = _mix_bwd_b(dh2, h1, vec2, dd, dqs, dks, dvs, dgates, cos, sin, wb["w_in"])
    wgrad_cols("w_in", u2, dproj, [])
    (dx, dab1, s1, df1, acc1), _ = _ffn_bwd(dh1, x, a1, b1, f1, vec1, [wb["w_ffn1_in"]], wb["w_ffn1_out"], "ffn1_bwd")
    grads["w_ffn1_out"], _ = _wgrad(s1, df1, "wg_ffn1_out", FC, 512, 1024)
    wgrad_cols("w_ffn1_in", u1, dab1, ["w_ffn1_out"])

    dmod = jnp.concatenate([acc1[0], acc1[1], acc1[3], acc2b[0], acc2b[1], acc2a[3], acc3[0], acc3[1], acc3[3]])
    dgains = jnp.stack([acc1[2], acc2b[2], acc3[2], lacc[1]], axis=0)
    return loss, dx, dmod, dgains, grad_w_pool, accps[0], grads


SMALL = ("b_ada", "g_norm_ffn1", "g_norm_mix", "g_norm_ffn2", "g_final", "pool_scale", "w_pool")
WEIGHTS = ("w_ada", "b_ada", "g_norm_ffn1", "w_ffn1_in", "w_ffn1_out", "g_norm_mix", "w_in", "w_pool", "pool_scale",
           "w_pool_branch", "w_attn_branch", "w_out", "g_norm_ffn2", "w_ffn2_in", "w_ffn2_out", "g_final")


def _pack_small(t):
    return jnp.concatenate([t[n].reshape(-1) for n in SMALL]).reshape(1, -1)


def _unpack_small(flat, like):
    out, off = {}, 0
    for n in SMALL:
        size = like[n].size
        out[n] = flat[0, off:off + size].reshape(like[n].shape)
        off += size
    return out


def kernel(x, c, positions, w_ada, b_ada, g_norm_ffn1, w_ffn1_in, w_ffn1_out, g_norm_mix, w_in, w_pool, pool_scale, w_pool_branch, w_attn_branch, w_out, g_norm_ffn2, w_ffn2_in, w_ffn2_out, g_final, loss_target, m_w_ada, m_b_ada, m_g_norm_ffn1, m_w_ffn1_in, m_w_ffn1_out, m_g_norm_mix, m_w_in, m_w_pool, m_pool_scale, m_w_pool_branch, m_w_attn_branch, m_w_out, m_g_norm_ffn2, m_w_ffn2_in, m_w_ffn2_out, m_g_final, v_w_ada, v_b_ada, v_g_norm_ffn1, v_w_ffn1_in, v_w_ffn1_out, v_g_norm_mix, v_w_in, v_w_pool, v_pool_scale, v_w_pool_branch, v_w_attn_branch, v_w_out, v_g_norm_ffn2, v_w_ffn2_in, v_w_ffn2_out, v_g_final):
    w = dict(w_ada=w_ada, b_ada=b_ada, g_norm_ffn1=g_norm_ffn1, w_ffn1_in=w_ffn1_in, w_ffn1_out=w_ffn1_out,
             g_norm_mix=g_norm_mix, w_in=w_in, w_pool=w_pool, pool_scale=pool_scale, w_pool_branch=w_pool_branch,
             w_attn_branch=w_attn_branch, w_out=w_out, g_norm_ffn2=g_norm_ffn2, w_ffn2_in=w_ffn2_in,
             w_ffn2_out=w_ffn2_out, g_final=g_final)
    mom = dict(w_ada=m_w_ada, b_ada=m_b_ada, g_norm_ffn1=m_g_norm_ffn1, w_ffn1_in=m_w_ffn1_in, w_ffn1_out=m_w_ffn1_out,
               g_norm_mix=m_g_norm_mix, w_in=m_w_in, w_pool=m_w_pool, pool_scale=m_pool_scale,
               w_pool_branch=m_w_pool_branch, w_attn_branch=m_w_attn_branch, w_out=m_w_out, g_norm_ffn2=m_g_norm_ffn2,
               w_ffn2_in=m_w_ffn2_in, w_ffn2_out=m_w_ffn2_out, g_final=m_g_final)
    var = dict(w_ada=v_w_ada, b_ada=v_b_ada, g_norm_ffn1=v_g_norm_ffn1, w_ffn1_in=v_w_ffn1_in, w_ffn1_out=v_w_ffn1_out,
               g_norm_mix=v_g_norm_mix, w_in=v_w_in, w_pool=v_w_pool, pool_scale=v_pool_scale,
               w_pool_branch=v_w_pool_branch, w_attn_branch=v_w_attn_branch, w_out=v_w_out, g_norm_ffn2=v_g_norm_ffn2,
               w_ffn2_in=v_w_ffn2_in, w_ffn2_out=v_w_ffn2_out, g_final=v_g_final)
    ix, iy, ic = _place()
    chip = 2 * ix + iy
    me = 4 * ix + 2 * iy + ic
    nada = w_ada.shape[2]

    c_all = _gather_small(c)[:, 0, :]
    b_shard = lax.dynamic_slice_in_dim(b_ada, chip * nada, nada, axis=1)
    mod_cols = _ada_fwd(c_all, w_ada[0], b_shard)
    mod_all = _gather_small(mod_cols)
    mod = jnp.concatenate([lax.dynamic_index_in_dim(mod_all[4 * (kk >> 1) + 2 * (kk & 1)], me, axis=0, keepdims=False)
                           for kk in range(4)])

    shards = {n: w[n][0].astype(BF16) for n in BIG}
    half = D // 2
    shards["w_ffn2_in/0"], shards["w_ffn2_in/1"] = shards["w_ffn2_in"][:half], shards["w_ffn2_in"][half:]
    ws = _Sharded(shards)
    loss, dx, dmod, dgains, g_w_pool, g_pool_scale, grads = _example_step(
        x[0], loss_target[0], positions[0], mod, (g_norm_ffn1[0], g_norm_mix[0], g_norm_ffn2[0], g_final),
        w_pool[0], pool_scale[0], ws)

    small_g = dict(b_ada=dmod, g_norm_ffn1=dgains[0], g_norm_mix=dgains[1], g_norm_ffn2=dgains[2], g_final=dgains[3],
                   pool_scale=g_pool_scale, w_pool=g_w_pool)
    tail = jnp.zeros((1, 128), F32)
    gathered = _gather_small(jnp.concatenate([_pack_small(small_g), jnp.pad(loss.reshape(1, 1), ((0, 0), (0, 127)))], axis=1))
    sg, sd, sm, sv = _adam_small(*[jnp.concatenate([_pack_small(t), tail], axis=1) for t in (w, mom, var)], gathered)
    small_out = [_unpack_small(t, w) for t in (sg, sd, sm, sv)]
    loss = sg[0, sg.shape[1] - 128]

    dmod_all = gathered[:, 0, :NMOD * D]
    dmod_cols = lax.dynamic_slice_in_dim(dmod_all, chip * nada, nada, axis=1)
    g_ada = _ada_bwd(c_all, dmod_cols)
    ada_out = _adam(w_ada[0], m_w_ada[0], v_w_ada[0], [g_ada], "adam_w_ada")

    sums = {n: _sum4(ws.recv[n], "sum_" + n) for n in BIG}
    other = dict(zip(BIG, _swap_sibling([sums[n] for n in BIG])))
    big_out = {}
    for n in BIG:
        if sums[n].shape[0] < w[n].shape[1]:
            big_out[n] = _adam_halves(w[n][0], mom[n][0], var[n][0], sums[n], other[n], "adam_" + n)
        else:
            big_out[n] = _adam(w[n][0], mom[n][0], var[n][0], [sums[n], other[n]], "adam_" + n)

    def leaf(kind, n):
        if n == "w_ada":
            return ada_out[kind][None]
        if n in big_out:
            return big_out[n][kind][None]
        return small_out[kind][n]

    return (loss, dx[None], *[leaf(kind, n) for kind in range(4) for n in WEIGHTS])
```

```python
import jax
import jax.numpy as jnp
from jax import lax
from jax.experimental import pallas as pl
from jax.experimental.pallas import tpu as pltpu

F32 = jnp.float32
BF16 = jnp.bfloat16

D = 1024
FF = 2816
FC = 1408
PW = 256
GA = 256
HD = 64
LANES = 128
NH = GA // HD
NG = 3
DIL = (1, 4, 16)
BLK = 128
GW = 2 * D
INW = PW + 3 * NG * GA + GW
NMOD = 9
POOL_WINDOWS = (2, 4, 8, 16)
HALO = 16
EPS = 1e-6
SCALE = HD ** -0.5
NEG = -1e30

LR, B1, B2, AEPS, WD, STEP = 0.001, 0.9, 0.999, 1e-08, 0.01, 10

VMEM_BIG = 56 * 1024 * 1024
TM = 256

MESH = pl.DeviceIdType.MESH
ANY = pl.BlockSpec(memory_space=pl.ANY)


def _call(body, name, grid, in_specs, out_specs, out_shape, scratch=(), vmem=None, comm=None):
    params = pltpu.CompilerParams(dimension_semantics=("arbitrary",) * len(grid), vmem_limit_bytes=vmem)
    n_in, n_out, n_scr = len(in_specs), len(out_shape), len(scratch)
    if comm is None:
        call = pl.pallas_call(body, name=name, grid=grid, in_specs=list(in_specs), out_specs=list(out_specs),
                              out_shape=list(out_shape), scratch_shapes=list(scratch), compiler_params=params)
        return lambda *args: (call(*args), ())
    nc = len(comm.inputs)

    def body_with_comm(*refs):
        ins, refs = refs[:n_in], refs[n_in:]
        c_ins, refs = refs[:nc], refs[nc:]
        outs, refs = refs[:n_out], refs[n_out:]
        c_outs, refs = refs[:nc], refs[nc:]
        scr, sems = refs[:n_scr], refs[n_scr:]
        first = pl.program_id(0) == 0
        last = pl.program_id(0) == grid[0] - 1
        for ax in range(1, len(grid)):
            first = jnp.logical_and(first, pl.program_id(ax) == 0)
            last = jnp.logical_and(last, pl.program_id(ax) == grid[ax] - 1)

        @pl.when(first)
        def _():
            comm.start(c_ins, c_outs, sems)

        body(*ins, *outs, *scr)

        @pl.when(last)
        def _():
            comm.wait(c_ins, c_outs, sems)

    call = pl.pallas_call(
        body_with_comm, name=name, grid=grid, in_specs=list(in_specs) + [ANY] * nc,
        out_specs=list(out_specs) + [ANY] * nc, out_shape=list(out_shape) + list(comm.out_shapes),
        scratch_shapes=list(scratch) + list(comm.sem_shapes), compiler_params=params)

    def run(*args):
        res = call(*args, *comm.inputs)
        return res[:n_out], res[n_out:]

    return run


def _rows(tm, n):
    return pl.BlockSpec((tm, n), lambda i: (i, 0))


def _const(shape):
    return pl.BlockSpec(shape, lambda i: (0,) * len(shape))


def _sds(shape, dtype):
    return jax.ShapeDtypeStruct(shape, dtype)


def _dot(a, b):
    return jnp.dot(a, b, preferred_element_type=F32)


def _dot_nt(a, b):
    return lax.dot_general(a, b, (((1,), (1,)), ((), ())), preferred_element_type=F32)


def _dot_tn(a, b):
    return lax.dot_general(a, b, (((0,), (0,)), ((), ())), preferred_element_type=F32)


def _colsum(v):
    return jnp.sum(v, axis=0, keepdims=True)


def _norm_fwd(h, g, sh, sc):
    r = lax.rsqrt(jnp.mean(h * h, axis=-1, keepdims=True) + EPS)
    xh = h * r
    n = xh * g
    return xh, r, n, n * (1.0 + sc) + sh


def _norm_bwd(du, xh, r, n, g, sc):
    dn = du * (1.0 + sc)
    dxh = dn * g
    dh = r * (dxh - xh * jnp.mean(dxh * xh, axis=-1, keepdims=True))
    return dh, _colsum(du), _colsum(du * n), _colsum(dn * xh)


def _load_once(pairs, sems):
    @pl.when(pl.program_id(0) == 0)
    def _():
        cps = [pltpu.make_async_copy(src, dst, sems.at[j]) for j, (src, dst) in enumerate(pairs)]
        for cp in cps:
            cp.start()
        for cp in cps:
            cp.wait()


def _zero_first(ref):
    @pl.when(pl.program_id(0) == 0)
    def _():
        ref[...] = jnp.zeros(ref.shape, ref.dtype)


def _row_chunks(hbm_refs, vmem_ref):
    pairs, row = [], 0
    for ref in hbm_refs:
        pairs.append((ref, vmem_ref.at[pl.ds(row, ref.shape[0]), :]))
        row += ref.shape[0]
    return pairs


def _loss_head(hh, tgt, g):
    r = lax.rsqrt(jnp.mean(hh * hh, axis=-1, keepdims=True) + EPS)
    xh = hh * r
    err = xh * g - tgt
    dy = err * (1.0 / D)
    dxh = dy * g
    dh = r * (dxh - xh * jnp.mean(dxh * xh, axis=-1, keepdims=True))
    return dh, _colsum(err * err), _colsum(dy * xh)


def _ffn_fwd(h, vec, wins, wout, name, comm=None, head=None):
    T = h.shape[0]
    nwin = len(wins)
    nhead = 0 if head is None else 2

    def body(h_ref, vec_ref, *rest):
        head_refs, rest = rest[:nhead], rest[nhead:]
        win_hbms, rest = rest[:nwin], rest[nwin:]
        (wout_hbm, ho_ref, u_ref, a_ref, b_ref, f_ref), rest = rest[:6], rest[6:]
        lacc_refs, (win_v, wout_v, sems) = rest[:nhead // 2], rest[nhead // 2:]
        _load_once(_row_chunks(win_hbms, win_v) + [(wout_hbm, wout_v)], sems)
        hh = h_ref[...]
        g, sh, sc, gt = vec_ref[0:1, :], vec_ref[1:2, :], vec_ref[2:3, :], vec_ref[3:4, :]
        _, _, _, u = _norm_fwd(hh, g, sh, sc)
        ub = u.astype(BF16)
        u_ref[...] = ub
        acc = None
        for j in range(FF // FC):
            lo, hi = j * FC, (j + 1) * FC
            a = _dot(ub, win_v[:, lo:hi])
            b = _dot(ub, win_v[:, FF + lo:FF + hi])
            a_ref[:, lo:hi] = a.astype(BF16)
            b_ref[:, lo:hi] = b.astype(BF16)
            s = (a * jax.nn.sigmoid(a) * b).astype(BF16)
            part = _dot(s, wout_v[lo:hi, :])
            acc = part if acc is None else acc + part
        f_ref[...] = acc.astype(BF16)
        ho = hh + 0.5 * gt * acc
        if head is None:
            ho_ref[...] = ho
        else:
            _zero_first(lacc_refs[0])
            dh, sq, dg = _loss_head(ho, head_refs[0][...], head_refs[1][0:1, :])
            ho_ref[...] = dh
            lacc_refs[0][0:1, :] += sq
            lacc_refs[0][1:2, :] += dg

    head_specs = [] if head is None else [_rows(TM, D), _const((8, D))]
    lacc_spec = [] if head is None else [_const((8, D))]
    lacc_shape = [] if head is None else [_sds((8, D), F32)]
    return _call(
        body, name, (T // TM,),
        [_rows(TM, D), _const((8, D))] + head_specs + [ANY] * (nwin + 1),
        [_rows(TM, D), _rows(TM, D), _rows(TM, FF), _rows(TM, FF), _rows(TM, D)] + lacc_spec,
        [_sds((T, D), F32), _sds((T, D), BF16), _sds((T, FF), BF16), _sds((T, FF), BF16), _sds((T, D), BF16)] + lacc_shape,
        scratch=[pltpu.VMEM((D, 2 * FF), BF16), pltpu.VMEM((FF, D), BF16), pltpu.SemaphoreType.DMA((nwin + 1,))],
        vmem=VMEM_BIG, comm=comm,
    )(h, vec, *([] if head is None else head), *wins, wout)


def _ffn_bwd(dh, h, a, b, f, vec, wins, wout, name, comm=None):
    T = h.shape[0]
    nwin = len(wins)

    def body(dh_ref, h_ref, a_ref, b_ref, f_ref, vec_ref, *rest):
        win_hbms, (wout_hbm, dhi_ref, dab_ref, s_ref, df_ref, acc_ref, win_v, wout_v, sems) = rest[:nwin], rest[nwin:]
        _load_once(_row_chunks(win_hbms, win_v) + [(wout_hbm, wout_v)], sems)
        _zero_first(acc_ref)
        g, sh, sc, gt = vec_ref[0:1, :], vec_ref[1:2, :], vec_ref[2:3, :], vec_ref[3:4, :]
        dho = dh_ref[...]
        df = (0.5 * gt * dho).astype(BF16)
        df_ref[...] = df
        dgt = _colsum(0.5 * dho * f_ref[...].astype(F32))
        du = None
        for j in range(FF // FC):
            lo, hi = j * FC, (j + 1) * FC
            av = a_ref[:, lo:hi].astype(F32)
            bv = b_ref[:, lo:hi].astype(F32)
            ds = _dot_nt(df, wout_v[lo:hi, :])
            sig = jax.nn.sigmoid(av)
            sa = av * sig
            s_ref[:, lo:hi] = (sa * bv).astype(BF16)
            da = (ds * bv * (sig * (1.0 + av * (1.0 - sig)))).astype(BF16)
            db = (ds * sa).astype(BF16)
            dab_ref[:, lo:hi] = da
            dab_ref[:, FF + lo:FF + hi] = db
            part = _dot_nt(da, win_v[:, lo:hi]) + _dot_nt(db, win_v[:, FF + lo:FF + hi])
            du = part if du is None else du + part
        xh, r, n, _ = _norm_fwd(h_ref[...], g, sh, sc)
        dhn, dsh, dsc, dg = _norm_bwd(du, xh, r, n, g, sc)
        dhi_ref[...] = dho + dhn
        acc_ref[0:1, :] += dsh
        acc_ref[1:2, :] += dsc
        acc_ref[2:3, :] += dg
        acc_ref[3:4, :] += dgt

    return _call(
        body, name, (T // TM,),
        [_rows(TM, D), _rows(TM, D), _rows(TM, FF), _rows(TM, FF), _rows(TM, D), _const((8, D))] + [ANY] * (nwin + 1),
        [_rows(TM, D), _rows(TM, 2 * FF), _rows(TM, FF), _rows(TM, D), _const((8, D))],
        [_sds((T, D), F32), _sds((T, 2 * FF), BF16), _sds((T, FF), BF16), _sds((T, D), BF16), _sds((8, D), F32)],
        scratch=[pltpu.VMEM((D, 2 * FF), BF16), pltpu.VMEM((FF, D), BF16), pltpu.SemaphoreType.DMA((nwin + 1,))],
        vmem=VMEM_BIG, comm=comm,
    )(dh, h, a, b, f, vec, *wins, wout)


def _wgrad(x, y, name, tk, tn, tt, out_dtype=BF16, comm=None):
    T, K = x.shape
    N = y.shape[1]
    nt = T // tt

    def body(x_ref, y_ref, o_ref, acc_ref):
        t = pl.program_id(2)
        part = _dot_tn(x_ref[...], y_ref[...])

        @pl.when(t == 0)
        def _():
            acc_ref[...] = part

        @pl.when(t > 0)
        def _():
            acc_ref[...] += part

        @pl.when(t == nt - 1)
        def _():
            o_ref[...] = acc_ref[...].astype(out_dtype)

    (out,), c_outs = _call(
        body, name, (K // tk, N // tn, nt),
        [pl.BlockSpec((tt, tk), lambda i, j, t: (t, i)), pl.BlockSpec((tt, tn), lambda i, j, t: (t, j))],
        [pl.BlockSpec((tk, tn), lambda i, j, t: (i, j))], [_sds((K, N), out_dtype)],
        scratch=[pltpu.VMEM((tk, tn), F32)], vmem=VMEM_BIG, comm=comm,
    )(x, y)
    return out, c_outs


def _wgrad_scatter(x, y, name, tt, comm=None):
    T, K = x.shape
    n = y.shape[1] // 4
    nt = T // tt
    half = K // 2
    nc = 0 if comm is None else len(comm.inputs)

    def body(chip_ref, x_ref, y_ref, *refs):
        c_ins, refs = refs[:nc], refs[nc:]
        recv_ref, refs = refs[0], refs[1:]
        c_outs, refs = refs[:nc], refs[nc:]
        acc_ref, keep_ref, give_ref, take_ref, local_sem, give_sems, take_sems, send_sems, recv_sems = refs[:9]
        j, t = pl.program_id(0), pl.program_id(1)
        px, py, pc = _place()

        def hand_over(jj):
            return pltpu.make_async_remote_copy(
                src_ref=give_ref.at[jj], dst_ref=take_ref.at[jj], send_sem=give_sems.at[jj], recv_sem=take_sems.at[jj],
                device_id=(px, py, 1 - pc), device_id_type=MESH)

        def send(jj):
            m = jj + 1
            return pltpu.make_async_remote_copy(
                src_ref=keep_ref.at[jj], dst_ref=recv_ref.at[m], send_sem=send_sems.at[jj], recv_sem=recv_sems.at[jj],
                device_id=_chip_peer(px, py, pc, m), device_id_type=MESH)

        def add_sibling(jj):
            hand_over(jj).wait_recv()
            keep_ref[jj] = (keep_ref[jj].astype(F32) + take_ref[jj].astype(F32)).astype(BF16)

        if comm is not None:
            @pl.when(jnp.logical_and(j == 0, t == 0))
            def _():
                comm.start(c_ins, c_outs, refs[9:])

        part = _dot_tn(x_ref[...], y_ref[...])

        @pl.when(t == 0)
        def _():
            acc_ref[...] = part

        @pl.when(t > 0)
        def _():
            acc_ref[...] += part

        for jj in range(3):
            @pl.when(jnp.logical_and(j == jj + 1, t == nt // 2))
            def _():
                add_sibling(jj)
                send(jj).start()

        for jj in range(4):
            @pl.when(jnp.logical_and(j == jj, t == nt - 1))
            def _():
                keep_ref[jj] = acc_ref[pl.ds(pl.multiple_of(pc * half, 16), half), :].astype(BF16)
                give_ref[jj] = acc_ref[pl.ds(pl.multiple_of((1 - pc) * half, 16), half), :].astype(BF16)
                hand_over(jj).start()

        @pl.when(jnp.logical_and(j == 3, t == nt - 1))
        def _():
            add_sibling(3)
            own = pltpu.make_async_copy(keep_ref.at[3], recv_ref.at[0], local_sem.at[0])
            own.start()
            for jj in range(3):
                send(jj).wait_recv()
            for jj in range(3):
                send(jj).wait_send()
            for jj in range(4):
                hand_over(jj).wait_send()
            own.wait()
            if comm is not None:
                comm.wait(c_ins, c_outs, refs[9:])

    grid_spec = pltpu.PrefetchScalarGridSpec(
        num_scalar_prefetch=1, grid=(4, nt),
        in_specs=[pl.BlockSpec((tt, K), lambda j, t, chip: (t, 0)),
                  pl.BlockSpec((tt, n), lambda j, t, chip: (t, chip[0] ^ ((j + 1) & 3)))] + [ANY] * nc,
        out_specs=[ANY] * (1 + nc),
        scratch_shapes=[pltpu.VMEM((K, n), F32)] + [pltpu.VMEM((4, half, n), BF16)] * 3
        + [pltpu.SemaphoreType.DMA((1,))] + [pltpu.SemaphoreType.DMA((4,))] * 2 + [pltpu.SemaphoreType.DMA((3,))] * 2
        + ([] if comm is None else list(comm.sem_shapes)))
    px, py, _ = _place()
    res = pl.pallas_call(
        body, name=name, grid_spec=grid_spec,
        out_shape=[_sds((4, half, n), BF16)] + ([] if comm is None else list(comm.out_shapes)),
        compiler_params=pltpu.CompilerParams(dimension_semantics=("arbitrary", "arbitrary"), vmem_limit_bytes=VMEM_BIG),
    )((2 * px + py).astype(jnp.int32).reshape(1), x, y, *([] if comm is None else comm.inputs))
    return res[0], res[1:]


def _swap_halves(t):
    w = t.shape[1]
    lane = lax.broadcasted_iota(jnp.int32, t.shape, 1)
    return jnp.where(lane % HD < HD // 2, pltpu.roll(t, w - HD // 2, 1), pltpu.roll(t, HD // 2, 1))


def _rope(t, cos, sin_signed):
    c = jnp.tile(cos, (1, t.shape[1] // cos.shape[1]))
    s = jnp.tile(sin_signed, (1, t.shape[1] // sin_signed.shape[1]))
    return t * c + _swap_halves(t) * s


def _rope_bwd(dt, cos, sin_signed):
    c = jnp.tile(cos, (1, dt.shape[1] // cos.shape[1]))
    s = jnp.tile(sin_signed, (1, dt.shape[1] // sin_signed.shape[1]))
    return dt * c + _swap_halves(dt * s)


def _rm_spec(dil):
    return pl.BlockSpec((dil, TM // dil, GA), lambda i: (0, i, 0))


def _to_residues(t, dst_ref, scr_ref, dil):
    if dil == 1:
        dst_ref[0] = t.astype(dst_ref.dtype)
        return
    for j in range(GA // LANES):
        scr_ref[j] = t[:, j * LANES:(j + 1) * LANES]
    for r in range(dil):
        for j in range(GA // LANES):
            rows = scr_ref.at[j][pl.ds(r, TM // dil, stride=dil), :]
            dst_ref[r, :, j * LANES:(j + 1) * LANES] = rows.astype(dst_ref.dtype)


def _from_residues(src_ref, scr_ref, dil):
    if dil == 1:
        return src_ref[0].astype(F32)
    for r in range(dil):
        for j in range(GA // LANES):
            scr_ref.at[j][pl.ds(r, TM // dil, stride=dil), :] = src_ref[r, :, j * LANES:(j + 1) * LANES].astype(F32)
    return jnp.concatenate([scr_ref[j] for j in range(GA // LANES)], axis=1)


def _mix_proj(h, vec, win, cos, sin, comm=None):
    T = h.shape[0]

    def body(h_ref, vec_ref, win_hbm, cos_ref, sin_ref, u_ref, p_ref, *rest):
        qkv_refs, gates_ref, win_v, scr_ref, sems = rest[:3 * NG], rest[3 * NG], rest[3 * NG + 1], rest[3 * NG + 2], rest[3 * NG + 3]
        _load_once([(win_hbm, win_v)], sems)
        g, sh, sc = vec_ref[0:1, :], vec_ref[1:2, :], vec_ref[2:3, :]
        _, _, _, u = _norm_fwd(h_ref[...], g, sh, sc)
        ub = u.astype(BF16)
        u_ref[...] = ub
        p_ref[...] = _dot(ub, win_v[:, 0:PW])
        cos_t, sin_t = cos_ref[...], sin_ref[...]
        for j in range(3 * NG):
            col = PW + j * GA
            t = _dot(ub, win_v[:, col:col + GA])
            if j < 2 * NG:
                t = _rope(t, cos_t, sin_t)
            _to_residues(t, qkv_refs[j], scr_ref, DIL[j % NG])
        for j in range(GW // 512):
            col = PW + 3 * NG * GA + j * 512
            gates_ref[:, j * 512:(j + 1) * 512] = jax.nn.sigmoid(_dot(ub, win_v[:, col:col + 512])).astype(BF16)

    outs, c_outs = _call(
        body, "mix_proj", (T // TM,),
        [_rows(TM, D), _const((8, D)), ANY, _rows(TM, 128), _rows(TM, 128)],
        [_rows(TM, D), _rows(TM, PW)] + [_rm_spec(d) for d in DIL] * 3 + [_rows(TM, GW)],
        [_sds((T, D), BF16), _sds((T, PW), F32)] + [_sds((d, T // d, GA), BF16) for d in DIL] * 3 + [_sds((T, GW), BF16)],
        scratch=[pltpu.VMEM((D, INW), BF16), pltpu.VMEM((GA // LANES, TM, LANES), F32), pltpu.SemaphoreType.DMA((1,))],
        vmem=VMEM_BIG, comm=comm,
    )(h, vec, win, cos, sin)
    return (outs[0], outs[1], outs[2:2 + NG], outs[2 + NG:2 + 2 * NG], outs[2 + 2 * NG:2 + 3 * NG], outs[2 + 3 * NG]), c_outs


def _head_masks():
    lane_head = lax.broadcasted_iota(jnp.int32, (BLK, GA), 1) // HD
    return [lane_head == hd for hd in range(NH)]


def _expand_heads(t, hm):
    return jnp.concatenate([jnp.where(m, t, jnp.zeros_like(t)) for m in hm], axis=0)


def _collapse_heads(tb, hm):
    out = None
    for hd, m in enumerate(hm):
        part = jnp.where(m, tb[hd * BLK:(hd + 1) * BLK, :], 0.0)
        out = part if out is None else out + part
    return out


def _head_rows(t):
    return jnp.concatenate([t[:, hd * HD:hd * HD + 1] for hd in range(NH)], axis=0)


def _band(has_prev):
    a = lax.broadcasted_iota(jnp.int32, (NH * BLK, 2 * BLK), 0) & (BLK - 1)
    c = lax.broadcasted_iota(jnp.int32, (NH * BLK, 2 * BLK), 1)
    return jnp.logical_and(c >= jnp.where(has_prev, a, BLK), c <= a + BLK)


def _attn_specs(nbt):
    cur = pl.BlockSpec((2 * BLK, GA), lambda i: (i, 0))
    prev = pl.BlockSpec((BLK, GA), lambda i: (jnp.maximum(2 * i - 1, 0), 0))
    nxt = pl.BlockSpec((BLK, GA), lambda i: (jnp.minimum(2 * i + 2, nbt - 1), 0))
    return cur, prev, nxt


def _attn_fwd(q, k, v, nb, name, comm=None):
    T = q.shape[0]
    nbt = T // BLK
    lo, hi = slice(0, BLK), slice(BLK, 2 * BLK)

    def block(qv, kcat, vcat, has_prev, hm):
        s = jnp.where(_band(has_prev), _dot_nt(_expand_heads(qv, hm), kcat) * SCALE, NEG)
        mx = jnp.max(s, axis=-1, keepdims=True)
        e = jnp.exp(s - mx)
        l = jnp.sum(e, axis=-1, keepdims=True)
        ob = _dot((e * (1.0 / l)).astype(BF16), vcat)
        return _collapse_heads(ob, hm), _collapse_heads(jnp.broadcast_to(mx + jnp.log(l), (NH * BLK, GA)), hm)

    def body(q_ref, k_ref, kp_ref, v_ref, vp_ref, o_ref, lse_ref):
        b0 = 2 * pl.program_id(0)
        hm = _head_masks()
        k_first = jnp.concatenate([kp_ref[...], k_ref[lo, :]], axis=0)
        v_first = jnp.concatenate([vp_ref[...], v_ref[lo, :]], axis=0)
        o_ref[lo, :], lse_ref[lo, :] = block(q_ref[lo, :], k_first, v_first, (b0 & (nb - 1)) != 0, hm)
        o_ref[hi, :], lse_ref[hi, :] = block(q_ref[hi, :], k_ref[...], v_ref[...], ((b0 + 1) & (nb - 1)) != 0, hm)

    cur, prev, _ = _attn_specs(nbt)
    return _call(body, name, (nbt // 2,), [cur, cur, prev, cur, prev], [cur, cur],
                 [_sds((T, GA), F32), _sds((T, GA), F32)], comm=comm)(q, k, k, v, v)


def _attn_bwd(q, k, v, do, lse, e, nb, name, comm=None):
    T = q.shape[0]
    nbt = T // BLK

    lo, hi = slice(0, BLK), slice(BLK, 2 * BLK)

    def probs_and_ds(qb, dob, kcat, vcat, lsev, ev, valid):
        p = jnp.where(valid, jnp.exp(_dot_nt(qb, kcat) * SCALE - _head_rows(lsev)), 0.0)
        return p, (p * (_dot_nt(dob, vcat) + _head_rows(ev))).astype(BF16)

    def body(q_ref, k_ref, v_ref, do_ref, lse_ref, e_ref, kp_ref, vp_ref, qn_ref, don_ref, lsen_ref, en_ref,
             dq_ref, dk_ref, dv_ref):
        b0 = 2 * pl.program_id(0)
        hm = _head_masks()
        q1, q2, q3 = _expand_heads(q_ref[lo, :], hm), _expand_heads(q_ref[hi, :], hm), _expand_heads(qn_ref[...], hm)
        do1, do2, do3 = (_expand_heads(do_ref[lo, :], hm), _expand_heads(do_ref[hi, :], hm),
                         _expand_heads(don_ref[...], hm))
        k1 = jnp.concatenate([kp_ref[...], k_ref[lo, :]], axis=0)
        v1 = jnp.concatenate([vp_ref[...], v_ref[lo, :]], axis=0)
        k2, v2 = k_ref[...], v_ref[...]
        p1, ds1 = probs_and_ds(q1, do1, k1, v1, lse_ref[lo, :], e_ref[lo, :], _band((b0 & (nb - 1)) != 0))
        p2, ds2 = probs_and_ds(q2, do2, k2, v2, lse_ref[hi, :], e_ref[hi, :], _band(((b0 + 1) & (nb - 1)) != 0))
        dq_ref[lo, :] = _collapse_heads(_dot(ds1, k1) * SCALE, hm)
        dq_ref[hi, :] = _collapse_heads(_dot(ds2, k2) * SCALE, hm)
        a = lax.broadcasted_iota(jnp.int32, (NH * BLK, BLK), 0) & (BLK - 1)
        c = lax.broadcasted_iota(jnp.int32, (NH * BLK, BLK), 1)
        valid3 = jnp.logical_and(c >= a, ((b0 + 2) & (nb - 1)) != 0)
        p3, ds3 = probs_and_ds(q3, do3, k_ref[hi, :], v_ref[hi, :], lsen_ref[...], en_ref[...], valid3)
        q12, q23 = jnp.concatenate([q1, q2], axis=0), jnp.concatenate([q2, q3], axis=0)
        do12, do23 = jnp.concatenate([do1, do2], axis=0), jnp.concatenate([do2, do3], axis=0)
        dk_ref[lo, :] = _dot_tn(jnp.concatenate([ds1[:, BLK:], ds2[:, :BLK]], axis=0), q12) * SCALE
        dk_ref[hi, :] = _dot_tn(jnp.concatenate([ds2[:, BLK:], ds3], axis=0), q23) * SCALE
        pb1, pb2, pb3 = p1.astype(BF16), p2.astype(BF16), p3.astype(BF16)
        dv_ref[lo, :] = _dot_tn(jnp.concatenate([pb1[:, BLK:], pb2[:, :BLK]], axis=0), do12).astype(BF16)
        dv_ref[hi, :] = _dot_tn(jnp.concatenate([pb2[:, BLK:], pb3], axis=0), do23).astype(BF16)

    cur, prev, nxt = _attn_specs(nbt)
    return _call(body, name, (nbt // 2,), [cur] * 6 + [prev, prev] + [nxt] * 4, [cur, cur, cur],
                 [_sds((T, GA), F32), _sds((T, GA), F32), _sds((T, GA), BF16)],
                 comm=comm)(q, k, v, do, lse, e, k, v, q, do, lse, e)


def _flat(t):
    return t.reshape(t.shape[0] * t.shape[1], t.shape[2])


def _by_residue(t, dil):
    return t.reshape(dil, t.shape[0] // dil, t.shape[1])


def _pool_consts(shape, row0):
    lane = lax.broadcasted_iota(jnp.int32, shape, 1)
    t = lax.broadcasted_iota(jnp.int32, shape, 0) + row0
    grp = lane // (PW // len(POOL_WINDOWS))
    win = jnp.where(grp == 0, POOL_WINDOWS[0], jnp.where(grp == 1, POOL_WINDOWS[1],
                    jnp.where(grp == 2, POOL_WINDOWS[2], POOL_WINDOWS[3])))
    cnt = jnp.minimum(t + 1, win).astype(F32)
    return grp, cnt


def _window_sums(ext_ref, base, step, tm):
    outs, run = [], None
    for j in range(POOL_WINDOWS[-1]):
        sl = ext_ref[pl.ds(base + step * j, tm), :]
        run = sl if run is None else run + sl
        if j + 1 in POOL_WINDOWS:
            outs.append(run)
    return outs


def _select_group(grp, vals):
    return jnp.where(grp == 0, vals[0], jnp.where(grp == 1, vals[1], jnp.where(grp == 2, vals[2], vals[3])))


def _pool_d(pc_ref, pp_ref, ext_ref, i, tm):
    ext_ref[0:HALO, :] = jnp.where(i > 0, pp_ref[tm - HALO:tm, :], 0.0)
    ext_ref[HALO:HALO + tm, :] = pc_ref[...]
    grp, cnt = _pool_consts((tm, PW), i * tm)
    sums = _window_sums(ext_ref, HALO, -1, tm)
    return _select_group(grp, sums) / cnt - pc_ref[...]


def _group_weights(ls):
    mx = jnp.maximum(jnp.maximum(ls[0], ls[1]), ls[2])
    es = [jnp.exp(l - mx) for l in ls]
    inv = 1.0 / (es[0] + es[1] + es[2])
    return [e * inv for e in es]


def _mix_merge(h, vec, p, os, lses, gates, wp_bd, pscale, wpb, wab, wout):
    T = h.shape[0]

    def body(h_ref, vec_ref, pc_ref, pp_ref, o0, o1, o2, l0, l1, l2, gates_ref, wp_ref, ps_ref, wpb_ref, wab_ref, wout_ref,
             ho_ref, yp_ref, ya_ref, mg_ref, mo_ref, d_ref, ext_ref, scr_ref):
        i = pl.program_id(0)
        gt = vec_ref[3:4, :]
        d = _pool_d(pc_ref, pp_ref, ext_ref, i, TM).astype(BF16)
        d_ref[...] = d
        ypool = (_dot(d, wp_ref[...]) * ps_ref[0:1, :]).astype(BF16)
        yp_ref[...] = ypool
        w = _group_weights([_from_residues(r, scr_ref, dl) for r, dl in zip((l0, l1, l2), DIL)])
        yattn = None
        for wg, o_ref, dl in zip(w, (o0, o1, o2), DIL):
            part = wg * _from_residues(o_ref, scr_ref, dl)
            yattn = part if yattn is None else yattn + part
        yattn = yattn.astype(BF16)
        ya_ref[...] = yattn
        merged = (gates_ref[:, 0:D].astype(F32) * _dot(ypool, wpb_ref[...])
                  + gates_ref[:, D:GW].astype(F32) * _dot(yattn, wab_ref[...])).astype(BF16)
        mg_ref[...] = merged
        mo = _dot(merged, wout_ref[...])
        mo_ref[...] = mo.astype(BF16)
        ho_ref[...] = h_ref[...] + gt * mo

    prev = pl.BlockSpec((TM, PW), lambda i: (jnp.maximum(i - 1, 0), 0))
    return _call(
        body, "mix_merge", (T // TM,),
        [_rows(TM, D), _const((8, D)), _rows(TM, PW), prev] + [_rm_spec(dl) for dl in DIL] * 2 + [_rows(TM, GW), _const((PW, PW)),
         _const((8, PW)), _const((PW, D)), _const((GA, D)), _const((D, D))],
        [_rows(TM, D), _rows(TM, PW), _rows(TM, GA), _rows(TM, D), _rows(TM, D), _rows(TM, PW)],
        [_sds((T, D), F32), _sds((T, PW), BF16), _sds((T, GA), BF16), _sds((T, D), BF16), _sds((T, D), BF16), _sds((T, PW), BF16)],
        scratch=[pltpu.VMEM((TM + HALO, PW), F32), pltpu.VMEM((GA // LANES, TM, LANES), F32)],
        vmem=VMEM_BIG,
    )(h, vec, p, p, *os, *lses, gates, wp_bd, pscale, wpb, wab, wout)[0]


def _mix_bwd_a(dh, vec, mixout, gates, ypool, yattn, dpool, os, lses, wp_bd, pscale, wpb, wab, wout, ones_bd, comm=None):
    T = dh.shape[0]

    def body(dh_ref, vec_ref, mo_ref, gates_ref, yp_ref, ya_ref, d_ref, o0, o1, o2, l0, l1, l2,
             wp_ref, ps_ref, wpb_ref, wab_ref, wout_ref, ones_ref,
             dmo_ref, dp_ref, da_ref, dgates_ref, do0, do1, do2, e0, e1, e2, dd_ref, dyp_ref, acc_ref, acc2_ref, scr_ref):
        _zero_first(acc_ref)
        _zero_first(acc2_ref)
        gt = vec_ref[3:4, :]
        dho = dh_ref[...]
        acc_ref[3:4, :] += _colsum(dho * mo_ref[...].astype(F32))
        dmo = (gt * dho).astype(BF16)
        dmo_ref[...] = dmo
        dmerged = _dot_nt(dmo, wout_ref[...])
        gp = gates_ref[:, 0:D].astype(F32)
        ga = gates_ref[:, D:GW].astype(F32)
        bp = _dot(yp_ref[...], wpb_ref[...])
        ba = _dot(ya_ref[...], wab_ref[...])
        dgates_ref[:, 0:D] = (dmerged * bp * gp * (1.0 - gp)).astype(BF16)
        dgates_ref[:, D:GW] = (dmerged * ba * ga * (1.0 - ga)).astype(BF16)
        dbp = (dmerged * gp).astype(BF16)
        dba = (dmerged * ga).astype(BF16)
        dp_ref[...] = dbp
        da_ref[...] = dba
        dypool = _dot_nt(dbp, wpb_ref[...])
        ypre = _dot(d_ref[...], wp_ref[...])
        acc2_ref[0:1, :] += _colsum(dypool * ypre)
        dyp = (dypool * ps_ref[0:1, :]).astype(BF16)
        dyp_ref[...] = dyp
        dd_ref[...] = _dot_nt(dyp, wp_ref[...])
        dya = _dot_nt(dba, wab_ref[...])
        w = _group_weights([_from_residues(r, scr_ref, dl) for r, dl in zip((l0, l1, l2), DIL)])
        ya = None
        for wg, o_ref, dl in zip(w, (o0, o1, o2), DIL):
            part = wg * _from_residues(o_ref, scr_ref, dl)
            ya = part if ya is None else ya + part
        prod = dya * ya
        hi = prod.astype(BF16)
        lo = (prod - hi.astype(F32)).astype(BF16)
        tot = _dot(hi, ones_ref[...]) + _dot(lo, ones_ref[...])
        for wg, do_ref, e_ref, dl in zip(w, (do0, do1, do2), (e0, e1, e2), DIL):
            _to_residues(wg * dya, do_ref, scr_ref, dl)
            _to_residues(-wg * tot, e_ref, scr_ref, dl)

    return _call(
        body, "mix_bwd_a", (T // TM,),
        [_rows(TM, D), _const((8, D)), _rows(TM, D), _rows(TM, GW), _rows(TM, PW), _rows(TM, GA), _rows(TM, PW)]
        + [_rm_spec(dl) for dl in DIL] * 2
        + [_const((PW, PW)), _const((8, PW)), _const((PW, D)), _const((GA, D)), _const((D, D)), _const((GA, GA))],
        [_rows(TM, D)] * 3 + [_rows(TM, GW)] + [_rm_spec(dl) for dl in DIL] * 2
        + [_rows(TM, PW), _rows(TM, PW), _const((8, D)), _const((8, PW))],
        [_sds((T, D), BF16)] * 3 + [_sds((T, GW), BF16)] + [_sds((dl, T // dl, GA), BF16) for dl in DIL]
        + [_sds((dl, T // dl, GA), F32) for dl in DIL]
        + [_sds((T, PW), F32), _sds((T, PW), BF16), _sds((8, D), F32), _sds((8, PW), F32)],
        scratch=[pltpu.VMEM((GA // LANES, TM, LANES), F32)],
        vmem=VMEM_BIG, comm=comm,
    )(dh, vec, mixout, gates, ypool, yattn, dpool, *os, *lses, wp_bd, pscale, wpb, wab, wout, ones_bd)


def _mix_bwd_b(dh, h, vec, dd, dqs, dks, dvs, dgates, cos, sin, win):
    T = h.shape[0]
    nt = T // TM

    def body(dh_ref, h_ref, vec_ref, ddc_ref, ddn_ref, *rest):
        qk_refs, dv_refs = rest[:2 * NG], rest[2 * NG:3 * NG]
        dgates_ref, cos_ref, sin_ref, win_hbm, dhi_ref, dproj_ref, acc_ref, win_v, ext_ref, scr_ref, sems = rest[3 * NG:]
        i = pl.program_id(0)
        _load_once([(win_hbm, win_v)], sems)
        _zero_first(acc_ref)
        g, sh, sc = vec_ref[0:1, :], vec_ref[1:2, :], vec_ref[2:3, :]
        grp, cnt = _pool_consts((TM, PW), i * TM)
        _, cnt_n = _pool_consts((HALO, PW), (i + 1) * TM)
        ext_ref[0:TM, :] = ddc_ref[...] / cnt
        ext_ref[TM:TM + HALO, :] = jnp.where(i < nt - 1, ddn_ref[0:HALO, :] / cnt_n, 0.0)
        dp = _select_group(grp, _window_sums(ext_ref, 0, 1, TM)) - ddc_ref[...]
        dproj_ref[:, 0:PW] = dp.astype(BF16)
        cos_t, sin_t = cos_ref[...], sin_ref[...]
        for j in range(2 * NG):
            col = PW + j * GA
            dt = _from_residues(qk_refs[j], scr_ref, DIL[j % NG])
            dproj_ref[:, col:col + GA] = _rope_bwd(dt, cos_t, sin_t).astype(BF16)
        for j in range(NG):
            col = PW + (2 * NG + j) * GA
            dproj_ref[:, col:col + GA] = _from_residues(dv_refs[j], scr_ref, DIL[j]).astype(BF16)
        dproj_ref[:, PW + 3 * NG * GA:INW] = dgates_ref[...]
        du = None
        for j in range(INW // 512):
            part = _dot_nt(dproj_ref[:, j * 512:(j + 1) * 512], win_v[:, j * 512:(j + 1) * 512])
            du = part if du is None else du + part
        xh, r, n, _ = _norm_fwd(h_ref[...], g, sh, sc)
        dhn, dsh, dsc, dg = _norm_bwd(du, xh, r, n, g, sc)
        dhi_ref[...] = dh_ref[...] + dhn
        acc_ref[0:1, :] += dsh
        acc_ref[1:2, :] += dsc
        acc_ref[2:3, :] += dg

    nxt = pl.BlockSpec((TM, PW), lambda i: (jnp.minimum(i + 1, nt - 1), 0))
    return _call(
        body, "mix_bwd_b", (nt,),
        [_rows(TM, D), _rows(TM, D), _const((8, D)), _rows(TM, PW), nxt] + [_rm_spec(dl) for dl in DIL] * 3
        + [_rows(TM, GW), _rows(TM, 128), _rows(TM, 128), ANY],
        [_rows(TM, D), _rows(TM, INW), _const((8, D))],
        [_sds((T, D), F32), _sds((T, INW), BF16), _sds((8, D), F32)],
        scratch=[pltpu.VMEM((D, INW), BF16), pltpu.VMEM((TM + HALO, PW), F32), pltpu.VMEM((GA // LANES, TM, LANES), F32),
                 pltpu.SemaphoreType.DMA((1,))],
        vmem=VMEM_BIG,
    )(dh, h, vec, dd, dd, *dqs, *dks, *dvs, dgates, cos, sin, win)[0]


def _ada_fwd(c_all, w_shard, b_shard):
    n = w_shard.shape[1]

    def body(c_ref, w_ref, b_ref, o_ref):
        cv = c_ref[...]
        cond = (cv * jax.nn.sigmoid(cv)).astype(BF16)
        o_ref[...] = _dot(cond, w_ref[...].astype(BF16)) + b_ref[...]

    tn = n // 3
    return pl.pallas_call(
        body, name="ada_fwd", grid=(3,),
        in_specs=[pl.BlockSpec((8, D), lambda j: (0, 0)), pl.BlockSpec((D, tn), lambda j: (0, j)), pl.BlockSpec((1, tn), lambda j: (0, j))],
        out_specs=pl.BlockSpec((8, tn), lambda j: (0, j)), out_shape=_sds((8, n), F32),
        compiler_params=pltpu.CompilerParams(dimension_semantics=("arbitrary",)),
    )(c_all, w_shard, b_shard)


def _ada_bwd(c_all, dmod_shard):
    n = dmod_shard.shape[1]

    def body(c_ref, d_ref, o_ref):
        cv = c_ref[...]
        cond = (cv * jax.nn.sigmoid(cv)).astype(BF16)
        o_ref[...] = _dot_tn(cond, d_ref[...].astype(BF16))

    tn = n // 3
    return pl.pallas_call(
        body, name="ada_bwd", grid=(3,),
        in_specs=[pl.BlockSpec((8, D), lambda j: (0, 0)), pl.BlockSpec((8, tn), lambda j: (0, j))],
        out_specs=pl.BlockSpec((D, tn), lambda j: (0, j)), out_shape=_sds((D, n), F32),
        compiler_params=pltpu.CompilerParams(dimension_semantics=("arbitrary",)),
    )(c_all, dmod_shard)


def _adam_math(w, g, m, v):
    m2 = B1 * m + (1.0 - B1) * g
    v2 = B2 * v + (1.0 - B2) * (g * g)
    m_hat = m2 / (1.0 - B1 ** STEP)
    v_hat = v2 / (1.0 - B2 ** STEP)
    delta = -LR * (m_hat / (jnp.sqrt(v_hat) + AEPS) + WD * w)
    return delta, m2, v2


def _adam(w, m, v, parts, name):
    R, C = w.shape
    tr = R
    for cand in (128, 64, 32, 16, 8):
        if R % cand == 0:
            tr = cand
            break
    np_ = len(parts)

    def body(w_ref, m_ref, v_ref, *rest):
        p_refs, (g_ref, d_ref, m2_ref, v2_ref) = rest[:np_], rest[np_:]
        g = p_refs[0][...]
        for pr in p_refs[1:]:
            g = g + pr[...]
        delta, m2, v2 = _adam_math(w_ref[...], g, m_ref[...], v_ref[...])
        g_ref[...] = g
        d_ref[...] = delta
        m2_ref[...] = m2
        v2_ref[...] = v2

    spec = pl.BlockSpec((tr, C), lambda i: (i, 0))
    return pl.pallas_call(
        body, name=name, grid=(R // tr,), in_specs=[spec] * (3 + np_), out_specs=[spec] * 4,
        out_shape=[_sds((R, C), F32)] * 4,
        compiler_params=pltpu.CompilerParams(dimension_semantics=("arbitrary",), vmem_limit_bytes=VMEM_BIG),
    )(w, m, v, *parts)


def _adam_halves(w, m, v, mine, other, name):
    R, C = w.shape
    tr = 128
    nh = R // 2 // tr

    def body(c_ref, w_ref, m_ref, v_ref, mine_ref, other_ref, g_ref, d_ref, m2_ref, v2_ref):
        i = pl.program_id(0)
        in_mine = jnp.logical_and(i >= c_ref[0] * nh, i < (c_ref[0] + 1) * nh)
        g = jnp.where(in_mine, mine_ref[...], other_ref[...])
        delta, m2, v2 = _adam_math(w_ref[...], g, m_ref[...], v_ref[...])
        g_ref[...] = g
        d_ref[...] = delta
        m2_ref[...] = m2
        v2_ref[...] = v2

    spec = pl.BlockSpec((tr, C), lambda i, c: (i, 0))
    grid_spec = pltpu.PrefetchScalarGridSpec(
        num_scalar_prefetch=1, grid=(R // tr,),
        in_specs=[spec] * 3 + [pl.BlockSpec((tr, C), lambda i, c: (jnp.clip(i - c[0] * nh, 0, nh - 1), 0)),
                               pl.BlockSpec((tr, C), lambda i, c: (jnp.clip(i - (1 - c[0]) * nh, 0, nh - 1), 0))],
        out_specs=[spec] * 4)
    return pl.pallas_call(
        body, name=name, grid_spec=grid_spec, out_shape=[_sds((R, C), F32)] * 4,
        compiler_params=pltpu.CompilerParams(dimension_semantics=("arbitrary",), vmem_limit_bytes=VMEM_BIG),
    )(lax.axis_index("c").astype(jnp.int32).reshape(1), w, m, v, mine, other)


def _adam_small(ws, ms, vs, gathered):
    n = len(ws)
    sizes = [a.shape[1] for a in ws]

    def total(ga_ref, off, size):
        g = ga_ref[0, :, off:off + size]
        for dev in range(1, 8):
            g = g + ga_ref[dev, :, off:off + size]
        return g

    def body(*refs):
        w_refs, m_refs, v_refs, ga_ref, outs = refs[:n], refs[n:2 * n], refs[2 * n:3 * n], refs[3 * n], refs[3 * n + 1:]
        off = 0
        for j, size in enumerate(sizes):
            g = total(ga_ref, off, size)
            delta, m2, v2 = _adam_math(w_refs[j][...], g, m_refs[j][...], v_refs[j][...])
            for ref, val in zip(outs[4 * j:4 * j + 4], (g, delta, m2, v2)):
                ref[...] = val
            off += size
        outs[4 * n][...] = total(ga_ref, off, 128)

    res = pl.pallas_call(
        body, name="adam_small",
        out_shape=[_sds((1, size), F32) for size in sizes for _ in range(4)] + [_sds((1, 128), F32)],
    )(*ws, *ms, *vs, gathered)
    return [res[4 * j:4 * j + 4] for j in range(n)], res[4 * n]


def _sum4(blocks, name):
    _, R, C = blocks.shape
    tr = R
    for cand in (256, 128, 64, 32, 16):
        if R % cand == 0:
            tr = cand
            break

    def body(r_ref, out_ref):
        out_ref[...] = ((r_ref[0].astype(F32) + r_ref[1].astype(F32)) + r_ref[2].astype(F32)) + r_ref[3].astype(F32)

    return pl.pallas_call(
        body, name=name, grid=(R // tr,),
        in_specs=[pl.BlockSpec((4, tr, C), lambda i: (0, i, 0))],
        out_specs=pl.BlockSpec((tr, C), lambda i: (i, 0)), out_shape=_sds((R, C), F32),
        compiler_params=pltpu.CompilerParams(dimension_semantics=("arbitrary",)),
    )(blocks)


def _place():
    return lax.axis_index("x"), lax.axis_index("y"), lax.axis_index("c")


def _gather_small(v):
    R, P = v.shape

    def body(v_ref, out_ref, send_sems, recv_sems):
        x, y, c = _place()
        me = 4 * x + 2 * y + c
        out_ref[me] = v_ref[...]
        copies = []
        for m in range(1, 8):
            peer = (x ^ (m >> 2), y ^ ((m >> 1) & 1), c ^ (m & 1))
            copies.append(pltpu.make_async_remote_copy(
                src_ref=v_ref, dst_ref=out_ref.at[me], send_sem=send_sems.at[m - 1], recv_sem=recv_sems.at[m - 1],
                device_id=peer, device_id_type=MESH))
        for cp in copies:
            cp.start()
        for m in range(1, 8):
            src = 4 * (x ^ (m >> 2)) + 2 * (y ^ ((m >> 1) & 1)) + (c ^ (m & 1))
            pltpu.make_async_remote_copy(
                src_ref=v_ref, dst_ref=out_ref.at[src], send_sem=send_sems.at[m - 1], recv_sem=recv_sems.at[m - 1],
                device_id=(x, y, c), device_id_type=MESH).wait_recv()
        for cp in copies:
            cp.wait_send()

    vm = pl.BlockSpec(memory_space=pltpu.VMEM)
    return pl.pallas_call(
        body, name="gather_small", in_specs=[vm], out_specs=vm, out_shape=_sds((8, R, P), F32),
        scratch_shapes=[pltpu.SemaphoreType.DMA((7,)), pltpu.SemaphoreType.DMA((7,))],
    )(v)


def _chip_peer(x, y, c, m):
    return (x ^ (m >> 1), y ^ (m & 1), c)


def _shard_ref(ref, axis, k, n):
    start = pl.multiple_of(k * n, 128 if axis == 1 else 16)
    return ref.at[:, pl.ds(start, n)] if axis == 1 else ref.at[pl.ds(start, n), :]


def _half_rows(ref, axis, k, n, hc):
    if axis == 1:
        half = ref.shape[0] // 2
        return ref.at[pl.ds(pl.multiple_of(hc * half, 16), half), pl.ds(pl.multiple_of(k * n, 128), n)]
    half = n // 2
    return ref.at[pl.ds(pl.multiple_of(k * n + hc * half, 16), half), :]


class _GatherPlan:
    def __init__(self, shards, axes):
        self.inputs, self.axes, nw = list(shards), list(axes), len(shards)
        self.out_shapes = [_sds((s.shape[0] * (4 if ax == 0 else 1), s.shape[1] * (4 if ax == 1 else 1)), BF16)
                           for s, ax in zip(shards, axes)]
        self.sem_shapes = [pltpu.SemaphoreType.DMA((nw,))] + [pltpu.SemaphoreType.DMA((nw, 3))] * 4

    def _copies(self, ins, outs, sems):
        local_sems, send_sems, recv_sems, pass_sems, got_sems = sems
        x, y, c = _place()
        k = 2 * x + y
        local, sends, arrivals, passes, handed = [], [], [], [], []
        for j, ax in enumerate(self.axes):
            n = ins[j].shape[ax]
            half = ins[j].shape[0] // 2
            local.append(pltpu.make_async_copy(ins[j], _shard_ref(outs[j], ax, k, n), local_sems.at[j]))
            my_half = ins[j].at[pl.ds(pl.multiple_of(c * half, 16), half), :]
            for m in range(1, 4):
                sends.append(pltpu.make_async_remote_copy(
                    src_ref=my_half, dst_ref=_half_rows(outs[j], ax, k, n, c), send_sem=send_sems.at[j, m - 1],
                    recv_sem=recv_sems.at[j, m - 1], device_id=_chip_peer(x, y, c, m), device_id_type=MESH))
                theirs = _half_rows(outs[j], ax, k ^ m, n, c)
                arrivals.append(pltpu.make_async_remote_copy(
                    src_ref=my_half, dst_ref=theirs, send_sem=send_sems.at[j, m - 1], recv_sem=recv_sems.at[j, m - 1],
                    device_id=(x, y, c), device_id_type=MESH))
                passes.append(pltpu.make_async_remote_copy(
                    src_ref=theirs, dst_ref=theirs, send_sem=pass_sems.at[j, m - 1], recv_sem=got_sems.at[j, m - 1],
                    device_id=(x, y, 1 - c), device_id_type=MESH))
                other = _half_rows(outs[j], ax, k ^ m, n, 1 - c)
                handed.append(pltpu.make_async_remote_copy(
                    src_ref=other, dst_ref=other, send_sem=pass_sems.at[j, m - 1], recv_sem=got_sems.at[j, m - 1],
                    device_id=(x, y, c), device_id_type=MESH))
        return local, sends, arrivals, passes, handed

    def start(self, ins, outs, sems):
        local, sends, _, _, _ = self._copies(ins, outs, sems)
        for cp in local + sends:
            cp.start()

    def wait(self, ins, outs, sems):
        local, sends, arrivals, passes, handed = self._copies(ins, outs, sems)
        for arrived, onward in zip(arrivals, passes):
            arrived.wait_recv()
            onward.start()
        for cp in handed:
            cp.wait_recv()
        for cp in sends + passes:
            cp.wait_send()
        for cp in local:
            cp.wait()


class _ScatterPlan:
    def __init__(self, grads, axes):
        self.inputs, self.axes, nw = list(grads), list(axes), len(grads)
        self.shard_shapes = [(g.shape[0] // (4 if ax == 0 else 1), g.shape[1] // (4 if ax == 1 else 1))
                             for g, ax in zip(grads, axes)]
        self.out_shapes = [_sds((4,) + s, BF16) for s in self.shard_shapes]
        self.sem_shapes = [pltpu.SemaphoreType.DMA((nw,)), pltpu.SemaphoreType.DMA((nw, 3)), pltpu.SemaphoreType.DMA((nw, 3))]

    def _copies(self, ins, outs, sems):
        local_sems, send_sems, recv_sems = sems
        x, y, c = _place()
        k = 2 * x + y
        local, remote, arrivals = [], [], []
        for j, ax in enumerate(self.axes):
            n = self.shard_shapes[j][ax]
            local.append(pltpu.make_async_copy(_shard_ref(ins[j], ax, k, n), outs[j].at[0], local_sems.at[j]))
            for m in range(1, 4):
                remote.append(pltpu.make_async_remote_copy(
                    src_ref=_shard_ref(ins[j], ax, k ^ m, n), dst_ref=outs[j].at[m],
                    send_sem=send_sems.at[j, m - 1], recv_sem=recv_sems.at[j, m - 1],
                    device_id=_chip_peer(x, y, c, m), device_id_type=MESH))
                arrivals.append(pltpu.make_async_remote_copy(
                    src_ref=_shard_ref(ins[j], ax, k, n), dst_ref=outs[j].at[m],
                    send_sem=send_sems.at[j, m - 1], recv_sem=recv_sems.at[j, m - 1],
                    device_id=(x, y, c), device_id_type=MESH))
        return local, remote, arrivals

    def start(self, ins, outs, sems):
        local, remote, _ = self._copies(ins, outs, sems)
        for cp in local + remote:
            cp.start()

    def wait(self, ins, outs, sems):
        local, remote, arrivals = self._copies(ins, outs, sems)
        for cp in arrivals:
            cp.wait_recv()
        for cp in remote:
            cp.wait_send()
        for cp in local:
            cp.wait()


def _run_plan(plan, name):
    nc = len(plan.inputs)

    def body(*refs):
        ins, outs, sems = refs[:nc], refs[nc:2 * nc], refs[2 * nc:]
        plan.start(ins, outs, sems)
        plan.wait(ins, outs, sems)

    return pl.pallas_call(body, name=name, in_specs=[ANY] * nc, out_specs=[ANY] * nc, out_shape=list(plan.out_shapes),
                          scratch_shapes=list(plan.sem_shapes))(*plan.inputs)


def _swap_sibling(parts):
    nw = len(parts)

    def body(*refs):
        ins, outs = refs[:nw], refs[nw:2 * nw]
        send_sems, recv_sems = refs[2 * nw:]
        x, y, c = _place()
        copies = [pltpu.make_async_remote_copy(
            src_ref=ins[j], dst_ref=outs[j], send_sem=send_sems.at[j], recv_sem=recv_sems.at[j],
            device_id=(x, y, 1 - c), device_id_type=MESH) for j in range(nw)]
        for cp in copies:
            cp.start()
        for cp in copies:
            cp.wait()

    return pl.pallas_call(
        body, name="swap_sibling", in_specs=[ANY] * nw, out_specs=[ANY] * nw,
        out_shape=[_sds(p.shape, p.dtype) for p in parts],
        scratch_shapes=[pltpu.SemaphoreType.DMA((nw,)), pltpu.SemaphoreType.DMA((nw,))],
    )(*parts)


BIG = ("w_ffn1_in", "w_ffn1_out", "w_in", "w_pool_branch", "w_attn_branch", "w_out", "w_ffn2_in", "w_ffn2_out")
BIG_AXIS = {"w_ffn1_in": 1, "w_ffn1_out": 0, "w_in": 1, "w_pool_branch": 1, "w_attn_branch": 1, "w_out": 0,
            "w_ffn2_in": 1, "w_ffn2_out": 0}


class _Sharded:
    fused_scatter = True

    def __init__(self, shards):
        self.shards, self.full, self.recv = shards, {}, {}

    def gather_plan(self, names):
        return _GatherPlan([self.shards[n] for n in names], [BIG_AXIS[n.split("/")[0]] for n in names])

    def gather_now(self, names):
        self.gathered(names, _run_plan(self.gather_plan(names), "gather_" + names[0]))

    def gathered(self, names, outs):
        self.full.update(zip(names, outs))

    def scatter_plan(self, names, grads):
        return _ScatterPlan([grads[n] for n in names], [BIG_AXIS[n] for n in names])

    def scatter_now(self, names, grads):
        self.scattered(names, _run_plan(self.scatter_plan(names, grads), "scatter_" + names[0]))

    def scattered(self, names, outs):
        self.recv.update(zip(names, outs))


class _Whole:
    fused_scatter = False

    def __init__(self, full):
        self.full, self.recv = dict(full), {}

    def gather_plan(self, names):
        return None

    def gather_now(self, names):
        pass

    def gathered(self, names, outs):
        pass

    def scatter_plan(self, names, grads):
        return None

    def scatter_now(self, names, grads):
        pass

    def scattered(self, names, outs):
        pass


def _vec(rows):
    pad = [jnp.zeros((1, D), F32)] * (8 - len(rows))
    return jnp.concatenate([r.reshape(1, D) for r in rows] + pad, axis=0)


def _block_diag(w_pool):
    n, c = w_pool.shape[0], w_pool.shape[1]
    eye = jnp.eye(n, dtype=w_pool.dtype)
    return (eye[:, None, :, None] * w_pool[:, :, None, :]).reshape(n * c, n * c)


def _example_step(x, tgt, positions, mod, gains, w_pool, pool_scale, ws):
    T = x.shape[0]
    assert (T // BLK // DIL[-1]) & (T // BLK // DIL[-1] - 1) == 0, "blocks per sequence must be a power of two"
    sh1, sc1, gt1, sh2, sc2, gt2, sh3, sc3, gt3 = [mod[j * D:(j + 1) * D] for j in range(NMOD)]
    g1, g2, g3, gf = gains
    vec1, vec2, vec3 = _vec([g1, sh1, sc1, gt1]), _vec([g2, sh2, sc2, gt2]), _vec([g3, sh3, sc3, gt3])
    inv_freq = 10000.0 ** (-jnp.arange(0, HD, 2, dtype=F32) / HD)
    ang = positions.astype(F32)[:, None] * inv_freq
    cos = jnp.tile(jnp.cos(ang), (1, 4))
    sin = jnp.tile(jnp.concatenate([-jnp.sin(ang), jnp.sin(ang)], axis=1), (1, 2))
    wp_bd = _block_diag(w_pool).astype(BF16)
    ones_bd = _block_diag(jnp.ones((NH, HD, HD), F32)).astype(BF16)
    ps = jnp.concatenate([pool_scale.reshape(1, PW), jnp.zeros((7, PW), F32)], axis=0)
    wb = ws.full

    ws.gather_now(["w_ffn1_in", "w_ffn1_out"])
    mixw = ["w_in", "w_pool_branch", "w_attn_branch", "w_out"]
    (h1, u1, a1, b1, f1), got = _ffn_fwd(x, vec1, [wb["w_ffn1_in"]], wb["w_ffn1_out"], "ffn1_fwd", ws.gather_plan(mixw))
    ws.gathered(mixw, got)
    (u2, p, qs, ks, vs, gates), got = _mix_proj(h1, vec2, wb["w_in"], cos, sin, ws.gather_plan(["w_ffn2_in/0"]))
    ws.gathered(["w_ffn2_in/0"], got)
    qs, ks, vs = [_flat(t) for t in qs], [_flat(t) for t in ks], [_flat(t) for t in vs]
    nbs = [T // d // BLK for d in DIL]
    os, lses = [], []
    for gi, riders in enumerate((["w_ffn2_out"], ["w_ffn2_in/1"], None)):
        (o, lse), got = _attn_fwd(qs[gi], ks[gi], vs[gi], nbs[gi], f"attn_fwd{gi}", riders and ws.gather_plan(riders))
        ws.gathered(riders or [], got)
        os.append(o)
        lses.append(lse)
    win3 = [wb["w_ffn2_in/0"], wb["w_ffn2_in/1"]] if "w_ffn2_in/0" in wb else [wb["w_ffn2_in"]]
    os_r = [_by_residue(t, d) for t, d in zip(os, DIL)]
    lses_r = [_by_residue(t, d) for t, d in zip(lses, DIL)]
    h2, ypool, yattn, merged, mixout, dpool = _mix_merge(
        h1, vec2, p, os_r, lses_r, gates, wp_bd, ps, wb["w_pool_branch"], wb["w_attn_branch"], wb["w_out"])
    (dh3, u3, a3, b3, f3, lacc), _ = _ffn_fwd(h2, vec3, win3, wb["w_ffn2_out"], "ffn2_fwd", head=(tgt, _vec([gf])))
    loss = 0.5 * jnp.sum(lacc[0]) / D

    grads = {}

    def wgrad_cols(name, xx, yy, riders):
        plan = ws.scatter_plan(riders, grads) if riders else None
        if ws.fused_scatter:
            blocks, got = _wgrad_scatter(xx, yy, "wg_" + name, min(2048, T // 2), comm=plan)
            ws.scattered([name], [blocks])
        else:
            grads[name], got = _wgrad(xx, yy, "wg_" + name, D, 512, 1024, comm=plan)
        ws.scattered(riders, got)

    (dh2, dab3, s3, df3, acc3), _ = _ffn_bwd(dh3, h2, a3, b3, f3, vec3, win3, wb["w_ffn2_out"], "ffn2_bwd")
    grads["w_ffn2_out"], _ = _wgrad(s3, df3, "wg_ffn2_out", FC, 512, 1024)
    wgrad_cols("w_ffn2_in", u3, dab3, ["w_ffn2_out"])
    (dmo, dbp, dba, dgates, do0, do1, do2, e0, e1, e2, dd, dyp, acc2a, accps), _ = _mix_bwd_a(
        dh2, vec2, mixout, gates, ypool, yattn, dpool, os_r, lses_r, wp_bd, ps,
        wb["w_pool_branch"], wb["w_attn_branch"], wb["w_out"], ones_bd)
    grads["w_out"], _ = _wgrad(merged, dmo, "wg_out", D, 512, 1024)
    grads["w_pool_branch"], _ = _wgrad(ypool, dbp, "wg_pool_branch", PW, 512, 1024)
    grads["w_attn_branch"], _ = _wgrad(yattn, dba, "wg_attn_branch", GA, 512, 1024)
    gwp, _ = _wgrad(dpool, dyp, "wg_pool", PW, PW, 1024, out_dtype=F32)
    n = len(POOL_WINDOWS)
    c = PW // n
    grad_w_pool = jnp.stack([gwp[j * c:(j + 1) * c, j * c:(j + 1) * c] for j in range(n)], axis=0)
    small3 = ["w_out", "w_pool_branch", "w_attn_branch"]
    dqs, dks, dvs = [], [], []
    for gi, (do, e) in enumerate(((do0, e0), (do1, e1), (do2, e2))):
        plan = ws.scatter_plan(small3, grads) if gi == 0 else None
        (dq, dk, dv), got = _attn_bwd(qs[gi], ks[gi], vs[gi], _flat(do), lses[gi], _flat(e), nbs[gi], f"attn_bwd{gi}", plan)
        if gi == 0:
            ws.scattered(small3, got)
        dqs.append(_by_residue(dq, DIL[gi]))
        dks.append(_by_residue(dk, DIL[gi]))
        dvs.append(_by_residue(dv, DIL[gi]))
    dh1, dproj, acc2b = _mix_bwd_b(dh2, h1, vec2, dd, dqs, dks, dvs, dgates, cos, sin, wb["w_in"])
    wgrad_cols("w_in", u2, dproj, [])
    (dx, dab1, s1, df1, acc1), _ = _ffn_bwd(dh1, x, a1, b1, f1, vec1, [wb["w_ffn1_in"]], wb["w_ffn1_out"], "ffn1_bwd")
    grads["w_ffn1_out"], _ = _wgrad(s1, df1, "wg_ffn1_out", FC, 512, 1024)
    wgrad_cols("w_ffn1_in", u1, dab1, ["w_ffn1_out"])

    dmod = jnp.concatenate([acc1[0], acc1[1], acc1[3], acc2b[0], acc2b[1], acc2a[3], acc3[0], acc3[1], acc3[3]])
    dgains = jnp.stack([acc1[2], acc2b[2], acc3[2], lacc[1]], axis=0)
    return loss, dx, dmod, dgains, grad_w_pool, accps[0], grads


SMALL = ("b_ada", "g_norm_ffn1", "g_norm_mix", "g_norm_ffn2", "g_final", "pool_scale", "w_pool")
WEIGHTS = ("w_ada", "b_ada", "g_norm_ffn1", "w_ffn1_in", "w_ffn1_out", "g_norm_mix", "w_in", "w_pool", "pool_scale",
           "w_pool_branch", "w_attn_branch", "w_out", "g_norm_ffn2", "w_ffn2_in", "w_ffn2_out", "g_final")


def _pack_small(t):
    return jnp.concatenate([t[n].reshape(-1) for n in SMALL]).reshape(1, -1)


def kernel(x, c, positions, w_ada, b_ada, g_norm_ffn1, w_ffn1_in, w_ffn1_out, g_norm_mix, w_in, w_pool, pool_scale, w_pool_branch, w_attn_branch, w_out, g_norm_ffn2, w_ffn2_in, w_ffn2_out, g_final, loss_target, m_w_ada, m_b_ada, m_g_norm_ffn1, m_w_ffn1_in, m_w_ffn1_out, m_g_norm_mix, m_w_in, m_w_pool, m_pool_scale, m_w_pool_branch, m_w_attn_branch, m_w_out, m_g_norm_ffn2, m_w_ffn2_in, m_w_ffn2_out, m_g_final, v_w_ada, v_b_ada, v_g_norm_ffn1, v_w_ffn1_in, v_w_ffn1_out, v_g_norm_mix, v_w_in, v_w_pool, v_pool_scale, v_w_pool_branch, v_w_attn_branch, v_w_out, v_g_norm_ffn2, v_w_ffn2_in, v_w_ffn2_out, v_g_final):
    w = dict(w_ada=w_ada, b_ada=b_ada, g_norm_ffn1=g_norm_ffn1, w_ffn1_in=w_ffn1_in, w_ffn1_out=w_ffn1_out,
             g_norm_mix=g_norm_mix, w_in=w_in, w_pool=w_pool, pool_scale=pool_scale, w_pool_branch=w_pool_branch,
             w_attn_branch=w_attn_branch, w_out=w_out, g_norm_ffn2=g_norm_ffn2, w_ffn2_in=w_ffn2_in,
             w_ffn2_out=w_ffn2_out, g_final=g_final)
    mom = dict(w_ada=m_w_ada, b_ada=m_b_ada, g_norm_ffn1=m_g_norm_ffn1, w_ffn1_in=m_w_ffn1_in, w_ffn1_out=m_w_ffn1_out,
               g_norm_mix=m_g_norm_mix, w_in=m_w_in, w_pool=m_w_pool, pool_scale=m_pool_scale,
               w_pool_branch=m_w_pool_branch, w_attn_branch=m_w_attn_branch, w_out=m_w_out, g_norm_ffn2=m_g_norm_ffn2,
               w_ffn2_in=m_w_ffn2_in, w_ffn2_out=m_w_ffn2_out, g_final=m_g_final)
    var = dict(w_ada=v_w_ada, b_ada=v_b_ada, g_norm_ffn1=v_g_norm_ffn1, w_ffn1_in=v_w_ffn1_in, w_ffn1_out=v_w_ffn1_out,
               g_norm_mix=v_g_norm_mix, w_in=v_w_in, w_pool=v_w_pool, pool_scale=v_pool_scale,
               w_pool_branch=v_w_pool_branch, w_attn_branch=v_w_attn_branch, w_out=v_w_out, g_norm_ffn2=v_g_norm_ffn2,
               w_ffn2_in=v_w_ffn2_in, w_ffn2_out=v_w_ffn2_out, g_final=v_g_final)
    ix, iy, ic = _place()
    chip = 2 * ix + iy
    me = 4 * ix + 2 * iy + ic
    nada = w_ada.shape[2]

    c_all = _gather_small(c)[:, 0, :]
    b_shard = lax.dynamic_slice_in_dim(b_ada, chip * nada, nada, axis=1)
    mod_cols = _ada_fwd(c_all, w_ada[0], b_shard)
    mod_all = _gather_small(mod_cols)
    mod = jnp.concatenate([lax.dynamic_index_in_dim(mod_all[4 * (kk >> 1) + 2 * (kk & 1)], me, axis=0, keepdims=False)
                           for kk in range(4)])

    shards = {n: w[n][0].astype(BF16) for n in BIG}
    half = D // 2
    shards["w_ffn2_in/0"], shards["w_ffn2_in/1"] = shards["w_ffn2_in"][:half], shards["w_ffn2_in"][half:]
    ws = _Sharded(shards)
    loss, dx, dmod, dgains, g_w_pool, g_pool_scale, grads = _example_step(
        x[0], loss_target[0], positions[0], mod, (g_norm_ffn1[0], g_norm_mix[0], g_norm_ffn2[0], g_final),
        w_pool[0], pool_scale[0], ws)

    small_g = dict(b_ada=dmod, g_norm_ffn1=dgains[0], g_norm_mix=dgains[1], g_norm_ffn2=dgains[2], g_final=dgains[3],
                   pool_scale=g_pool_scale, w_pool=g_w_pool)
    gathered = _gather_small(jnp.concatenate([_pack_small(small_g), jnp.pad(loss.reshape(1, 1), ((0, 0), (0, 127)))], axis=1))
    per_weight, loss_tile = _adam_small(*[[t[n].reshape(1, -1) for n in SMALL] for t in (w, mom, var)], gathered)
    small_out = [{n: per_weight[j][kind].reshape(w[n].shape) for j, n in enumerate(SMALL)} for kind in range(4)]
    loss = loss_tile[0, 0]

    dmod_all = gathered[:, 0, :NMOD * D]
    dmod_cols = lax.dynamic_slice_in_dim(dmod_all, chip * nada, nada, axis=1)
    g_ada = _ada_bwd(c_all, dmod_cols)
    ada_out = _adam(w_ada[0], m_w_ada[0], v_w_ada[0], [g_ada], "adam_w_ada")

    sums = {n: _sum4(ws.recv[n], "sum_" + n) for n in BIG}
    other = dict(zip(BIG, _swap_sibling([sums[n] for n in BIG])))
    big_out = {}
    for n in BIG:
        if sums[n].shape[0] < w[n].shape[1]:
            big_out[n] = _adam_halves(w[n][0], mom[n][0], var[n][0], sums[n], other[n], "adam_" + n)
        else:
            big_out[n] = _adam(w[n][0], mom[n][0], var[n][0], [sums[n], other[n]], "adam_" + n)

    def leaf(kind, n):
        if n == "w_ada":
            return ada_out[kind][None]
        if n in big_out:
            return big_out[n][kind][None]
        return small_out[kind][n]

    return (loss, dx[None], *[leaf(kind, n) for kind in range(4) for n in WEIGHTS])
```

```python
import jax
import jax.numpy as jnp
from jax import lax
from jax.experimental import pallas as pl
from jax.experimental.pallas import tpu as pltpu

F32 = jnp.float32
BF16 = jnp.bfloat16

D = 1024
FF = 2816
FC = 1408
PW = 256
GA = 256
HD = 64
LANES = 128
NH = GA // HD
NG = 3
DIL = (1, 4, 16)
BLK = 128
GW = 2 * D
INW = PW + 3 * NG * GA + GW
NMOD = 9
POOL_WINDOWS = (2, 4, 8, 16)
HALO = 16
EPS = 1e-6
SCALE = HD ** -0.5
NEG = -1e30

LR, B1, B2, AEPS, WD, STEP = 0.001, 0.9, 0.999, 1e-08, 0.01, 10

VMEM_BIG = 56 * 1024 * 1024
TM = 256

MESH = pl.DeviceIdType.MESH
ANY = pl.BlockSpec(memory_space=pl.ANY)


def _call(body, name, grid, in_specs, out_specs, out_shape, scratch=(), vmem=None, comm=None):
    params = pltpu.CompilerParams(dimension_semantics=("arbitrary",) * len(grid), vmem_limit_bytes=vmem)
    n_in, n_out, n_scr = len(in_specs), len(out_shape), len(scratch)
    if comm is None:
        call = pl.pallas_call(body, name=name, grid=grid, in_specs=list(in_specs), out_specs=list(out_specs),
                              out_shape=list(out_shape), scratch_shapes=list(scratch), compiler_params=params)
        return lambda *args: (call(*args), ())
    nc = len(comm.inputs)

    def body_with_comm(*refs):
        ins, refs = refs[:n_in], refs[n_in:]
        c_ins, refs = refs[:nc], refs[nc:]
        outs, refs = refs[:n_out], refs[n_out:]
        c_outs, refs = refs[:nc], refs[nc:]
        scr, sems = refs[:n_scr], refs[n_scr:]
        first = pl.program_id(0) == 0
        last = pl.program_id(0) == grid[0] - 1
        for ax in range(1, len(grid)):
            first = jnp.logical_and(first, pl.program_id(ax) == 0)
            last = jnp.logical_and(last, pl.program_id(ax) == grid[ax] - 1)

        @pl.when(first)
        def _():
            comm.start(c_ins, c_outs, sems)

        body(*ins, *outs, *scr)

        @pl.when(last)
        def _():
            comm.wait(c_ins, c_outs, sems)

    call = pl.pallas_call(
        body_with_comm, name=name, grid=grid, in_specs=list(in_specs) + [ANY] * nc,
        out_specs=list(out_specs) + [ANY] * nc, out_shape=list(out_shape) + list(comm.out_shapes),
        scratch_shapes=list(scratch) + list(comm.sem_shapes), compiler_params=params)

    def run(*args):
        res = call(*args, *comm.inputs)
        return res[:n_out], res[n_out:]

    return run


def _rows(tm, n):
    return pl.BlockSpec((tm, n), lambda i: (i, 0))


def _const(shape):
    return pl.BlockSpec(shape, lambda i: (0,) * len(shape))


def _sds(shape, dtype):
    return jax.ShapeDtypeStruct(shape, dtype)


def _dot(a, b):
    return jnp.dot(a, b, preferred_element_type=F32)


def _dot_nt(a, b):
    return lax.dot_general(a, b, (((1,), (1,)), ((), ())), preferred_element_type=F32)


def _dot_tn(a, b):
    return lax.dot_general(a, b, (((0,), (0,)), ((), ())), preferred_element_type=F32)


def _colsum(v):
    return jnp.sum(v, axis=0, keepdims=True)


def _norm_fwd(h, g, sh, sc):
    r = lax.rsqrt(jnp.mean(h * h, axis=-1, keepdims=True) + EPS)
    xh = h * r
    n = xh * g
    return xh, r, n, n * (1.0 + sc) + sh


def _norm_bwd(du, xh, r, n, g, sc):
    dn = du * (1.0 + sc)
    dxh = dn * g
    dh = r * (dxh - xh * jnp.mean(dxh * xh, axis=-1, keepdims=True))
    return dh, _colsum(du), _colsum(du * n), _colsum(dn * xh)


def _load_once(pairs, sems):
    @pl.when(pl.program_id(0) == 0)
    def _():
        cps = [pltpu.make_async_copy(src, dst, sems.at[j]) for j, (src, dst) in enumerate(pairs)]
        for cp in cps:
            cp.start()
        for cp in cps:
            cp.wait()


def _zero_first(ref):
    @pl.when(pl.program_id(0) == 0)
    def _():
        ref[...] = jnp.zeros(ref.shape, ref.dtype)


def _row_chunks(hbm_refs, vmem_ref):
    pairs, row = [], 0
    for ref in hbm_refs:
        pairs.append((ref, vmem_ref.at[pl.ds(row, ref.shape[0]), :]))
        row += ref.shape[0]
    return pairs


def _loss_head(hh, tgt, g):
    r = lax.rsqrt(jnp.mean(hh * hh, axis=-1, keepdims=True) + EPS)
    xh = hh * r
    err = xh * g - tgt
    dy = err * (1.0 / D)
    dxh = dy * g
    dh = r * (dxh - xh * jnp.mean(dxh * xh, axis=-1, keepdims=True))
    return dh, _colsum(err * err), _colsum(dy * xh)


def _ffn_fwd(h, vec, wins, wout, name, comm=None, head=None):
    T = h.shape[0]
    nwin = len(wins)
    nhead = 0 if head is None else 2

    def body(h_ref, vec_ref, *rest):
        head_refs, rest = rest[:nhead], rest[nhead:]
        win_hbms, rest = rest[:nwin], rest[nwin:]
        (wout_hbm, ho_ref, u_ref, a_ref, b_ref, f_ref), rest = rest[:6], rest[6:]
        lacc_refs, (win_v, wout_v, sems) = rest[:nhead // 2], rest[nhead // 2:]
        _load_once(_row_chunks(win_hbms, win_v) + [(wout_hbm, wout_v)], sems)
        hh = h_ref[...]
        g, sh, sc, gt = vec_ref[0:1, :], vec_ref[1:2, :], vec_ref[2:3, :], vec_ref[3:4, :]
        _, _, _, u = _norm_fwd(hh, g, sh, sc)
        ub = u.astype(BF16)
        u_ref[...] = ub
        acc = None
        for j in range(FF // FC):
            lo, hi = j * FC, (j + 1) * FC
            a = _dot(ub, win_v[:, lo:hi])
            b = _dot(ub, win_v[:, FF + lo:FF + hi])
            a_ref[:, lo:hi] = a.astype(BF16)
            b_ref[:, lo:hi] = b.astype(BF16)
            s = (a * jax.nn.sigmoid(a) * b).astype(BF16)
            part = _dot(s, wout_v[lo:hi, :])
            acc = part if acc is None else acc + part
        f_ref[...] = acc.astype(BF16)
        ho = hh + 0.5 * gt * acc
        if head is None:
            ho_ref[...] = ho
        else:
            _zero_first(lacc_refs[0])
            dh, sq, dg = _loss_head(ho, head_refs[0][...], head_refs[1][0:1, :])
            ho_ref[...] = dh
            lacc_refs[0][0:1, :] += sq
            lacc_refs[0][1:2, :] += dg

    head_specs = [] if head is None else [_rows(TM, D), _const((8, D))]
    lacc_spec = [] if head is None else [_const((8, D))]
    lacc_shape = [] if head is None else [_sds((8, D), F32)]
    return _call(
        body, name, (T // TM,),
        [_rows(TM, D), _const((8, D))] + head_specs + [ANY] * (nwin + 1),
        [_rows(TM, D), _rows(TM, D), _rows(TM, FF), _rows(TM, FF), _rows(TM, D)] + lacc_spec,
        [_sds((T, D), F32), _sds((T, D), BF16), _sds((T, FF), BF16), _sds((T, FF), BF16), _sds((T, D), BF16)] + lacc_shape,
        scratch=[pltpu.VMEM((D, 2 * FF), BF16), pltpu.VMEM((FF, D), BF16), pltpu.SemaphoreType.DMA((nwin + 1,))],
        vmem=VMEM_BIG, comm=comm,
    )(h, vec, *([] if head is None else head), *wins, wout)


def _ffn_bwd(dh, h, a, b, f, vec, wins, wout, name, comm=None):
    T = h.shape[0]
    nwin = len(wins)

    def body(dh_ref, h_ref, a_ref, b_ref, f_ref, vec_ref, *rest):
        win_hbms, (wout_hbm, dhi_ref, dab_ref, s_ref, df_ref, acc_ref, win_v, wout_v, sems) = rest[:nwin], rest[nwin:]
        _load_once(_row_chunks(win_hbms, win_v) + [(wout_hbm, wout_v)], sems)
        _zero_first(acc_ref)
        g, sh, sc, gt = vec_ref[0:1, :], vec_ref[1:2, :], vec_ref[2:3, :], vec_ref[3:4, :]
        dho = dh_ref[...]
        df = (0.5 * gt * dho).astype(BF16)
        df_ref[...] = df
        dgt = _colsum(0.5 * dho * f_ref[...].astype(F32))
        du = None
        for j in range(FF // FC):
            lo, hi = j * FC, (j + 1) * FC
            av = a_ref[:, lo:hi].astype(F32)
            bv = b_ref[:, lo:hi].astype(F32)
            ds = _dot_nt(df, wout_v[lo:hi, :])
            sig = jax.nn.sigmoid(av)
            sa = av * sig
            s_ref[:, lo:hi] = (sa * bv).astype(BF16)
            da = (ds * bv * (sig * (1.0 + av * (1.0 - sig)))).astype(BF16)
            db = (ds * sa).astype(BF16)
            dab_ref[:, lo:hi] = da
            dab_ref[:, FF + lo:FF + hi] = db
            part = _dot_nt(da, win_v[:, lo:hi]) + _dot_nt(db, win_v[:, FF + lo:FF + hi])
            du = part if du is None else du + part
        xh, r, n, _ = _norm_fwd(h_ref[...], g, sh, sc)
        dhn, dsh, dsc, dg = _norm_bwd(du, xh, r, n, g, sc)
        dhi_ref[...] = dho + dhn
        acc_ref[0:1, :] += dsh
        acc_ref[1:2, :] += dsc
        acc_ref[2:3, :] += dg
        acc_ref[3:4, :] += dgt

    return _call(
        body, name, (T // TM,),
        [_rows(TM, D), _rows(TM, D), _rows(TM, FF), _rows(TM, FF), _rows(TM, D), _const((8, D))] + [ANY] * (nwin + 1),
        [_rows(TM, D), _rows(TM, 2 * FF), _rows(TM, FF), _rows(TM, D), _const((8, D))],
        [_sds((T, D), F32), _sds((T, 2 * FF), BF16), _sds((T, FF), BF16), _sds((T, D), BF16), _sds((8, D), F32)],
        scratch=[pltpu.VMEM((D, 2 * FF), BF16), pltpu.VMEM((FF, D), BF16), pltpu.SemaphoreType.DMA((nwin + 1,))],
        vmem=VMEM_BIG, comm=comm,
    )(dh, h, a, b, f, vec, *wins, wout)


def _wgrad(x, y, name, tk, tn, tt, out_dtype=BF16, comm=None):
    T, K = x.shape
    N = y.shape[1]
    nt = T // tt

    def body(x_ref, y_ref, o_ref, acc_ref):
        t = pl.program_id(2)
        part = _dot_tn(x_ref[...], y_ref[...])

        @pl.when(t == 0)
        def _():
            acc_ref[...] = part

        @pl.when(t > 0)
        def _():
            acc_ref[...] += part

        @pl.when(t == nt - 1)
        def _():
            o_ref[...] = acc_ref[...].astype(out_dtype)

    (out,), c_outs = _call(
        body, name, (K // tk, N // tn, nt),
        [pl.BlockSpec((tt, tk), lambda i, j, t: (t, i)), pl.BlockSpec((tt, tn), lambda i, j, t: (t, j))],
        [pl.BlockSpec((tk, tn), lambda i, j, t: (i, j))], [_sds((K, N), out_dtype)],
        scratch=[pltpu.VMEM((tk, tn), F32)], vmem=VMEM_BIG, comm=comm,
    )(x, y)
    return out, c_outs


def _wgrad_scatter(x, y, name, tt, comm=None):
    T, K = x.shape
    n = y.shape[1] // 4
    nt = T // tt
    half = K // 2
    nc = 0 if comm is None else len(comm.inputs)

    def body(chip_ref, x_ref, y_ref, *refs):
        c_ins, refs = refs[:nc], refs[nc:]
        recv_ref, refs = refs[0], refs[1:]
        c_outs, refs = refs[:nc], refs[nc:]
        acc_ref, keep_ref, give_ref, take_ref, local_sem, give_sems, take_sems, send_sems, recv_sems = refs[:9]
        j, t = pl.program_id(0), pl.program_id(1)
        px, py, pc = _place()

        def hand_over(jj):
            return pltpu.make_async_remote_copy(
                src_ref=give_ref.at[jj], dst_ref=take_ref.at[jj], send_sem=give_sems.at[jj], recv_sem=take_sems.at[jj],
                device_id=(px, py, 1 - pc), device_id_type=MESH)

        def send(jj):
            m = jj + 1
            return pltpu.make_async_remote_copy(
                src_ref=keep_ref.at[jj], dst_ref=recv_ref.at[m], send_sem=send_sems.at[jj], recv_sem=recv_sems.at[jj],
                device_id=_chip_peer(px, py, pc, m), device_id_type=MESH)

        def add_sibling(jj):
            hand_over(jj).wait_recv()
            keep_ref[jj] = (keep_ref[jj].astype(F32) + take_ref[jj].astype(F32)).astype(BF16)

        if comm is not None:
            @pl.when(jnp.logical_and(j == 0, t == 0))
            def _():
                comm.start(c_ins, c_outs, refs[9:])

        part = _dot_tn(x_ref[...], y_ref[...])

        @pl.when(t == 0)
        def _():
            acc_ref[...] = part

        @pl.when(t > 0)
        def _():
            acc_ref[...] += part

        for jj in range(3):
            @pl.when(jnp.logical_and(j == jj + 1, t == nt // 2))
            def _():
                add_sibling(jj)
                send(jj).start()

        for jj in range(4):
            @pl.when(jnp.logical_and(j == jj, t == nt - 1))
            def _():
                keep_ref[jj] = acc_ref[pl.ds(pl.multiple_of(pc * half, 16), half), :].astype(BF16)
                give_ref[jj] = acc_ref[pl.ds(pl.multiple_of((1 - pc) * half, 16), half), :].astype(BF16)
                hand_over(jj).start()

        @pl.when(jnp.logical_and(j == 3, t == nt - 1))
        def _():
            add_sibling(3)
            own = pltpu.make_async_copy(keep_ref.at[3], recv_ref.at[0], local_sem.at[0])
            own.start()
            for jj in range(3):
                send(jj).wait_recv()
            for jj in range(3):
                send(jj).wait_send()
            for jj in range(4):
                hand_over(jj).wait_send()
            own.wait()
            if comm is not None:
                comm.wait(c_ins, c_outs, refs[9:])

    grid_spec = pltpu.PrefetchScalarGridSpec(
        num_scalar_prefetch=1, grid=(4, nt),
        in_specs=[pl.BlockSpec((tt, K), lambda j, t, chip: (t, 0)),
                  pl.BlockSpec((tt, n), lambda j, t, chip: (t, chip[0] ^ ((j + 1) & 3)))] + [ANY] * nc,
        out_specs=[ANY] * (1 + nc),
        scratch_shapes=[pltpu.VMEM((K, n), F32)] + [pltpu.VMEM((4, half, n), BF16)] * 3
        + [pltpu.SemaphoreType.DMA((1,))] + [pltpu.SemaphoreType.DMA((4,))] * 2 + [pltpu.SemaphoreType.DMA((3,))] * 2
        + ([] if comm is None else list(comm.sem_shapes)))
    px, py, _ = _place()
    res = pl.pallas_call(
        body, name=name, grid_spec=grid_spec,
        out_shape=[_sds((4, half, n), BF16)] + ([] if comm is None else list(comm.out_shapes)),
        compiler_params=pltpu.CompilerParams(dimension_semantics=("arbitrary", "arbitrary"), vmem_limit_bytes=VMEM_BIG),
    )((2 * px + py).astype(jnp.int32).reshape(1), x, y, *([] if comm is None else comm.inputs))
    return res[0], res[1:]


def _swap_halves(t):
    w = t.shape[1]
    lane = lax.broadcasted_iota(jnp.int32, t.shape, 1)
    return jnp.where(lane % HD < HD // 2, pltpu.roll(t, w - HD // 2, 1), pltpu.roll(t, HD // 2, 1))


def _rope(t, cos, sin_signed):
    c = jnp.tile(cos, (1, t.shape[1] // cos.shape[1]))
    s = jnp.tile(sin_signed, (1, t.shape[1] // sin_signed.shape[1]))
    return t * c + _swap_halves(t) * s


def _rope_bwd(dt, cos, sin_signed):
    c = jnp.tile(cos, (1, dt.shape[1] // cos.shape[1]))
    s = jnp.tile(sin_signed, (1, dt.shape[1] // sin_signed.shape[1]))
    return dt * c + _swap_halves(dt * s)


def _rm_spec(dil):
    return pl.BlockSpec((dil, TM // dil, GA), lambda i: (0, i, 0))


def _to_residues(t, dst_ref, scr_ref, dil):
    if dil == 1:
        dst_ref[0] = t.astype(dst_ref.dtype)
        return
    for j in range(GA // LANES):
        scr_ref[j] = t[:, j * LANES:(j + 1) * LANES]
    for r in range(dil):
        for j in range(GA // LANES):
            rows = scr_ref.at[j][pl.ds(r, TM // dil, stride=dil), :]
            dst_ref[r, :, j * LANES:(j + 1) * LANES] = rows.astype(dst_ref.dtype)


def _from_residues(src_ref, scr_ref, dil):
    if dil == 1:
        return src_ref[0].astype(F32)
    for r in range(dil):
        for j in range(GA // LANES):
            scr_ref.at[j][pl.ds(r, TM // dil, stride=dil), :] = src_ref[r, :, j * LANES:(j + 1) * LANES].astype(F32)
    return jnp.concatenate([scr_ref[j] for j in range(GA // LANES)], axis=1)


def _mix_proj(h, vec, win, cos, sin, comm=None):
    T = h.shape[0]

    def body(h_ref, vec_ref, win_hbm, cos_ref, sin_ref, u_ref, p_ref, *rest):
        qkv_refs, gates_ref, win_v, scr_ref, sems = rest[:3 * NG], rest[3 * NG], rest[3 * NG + 1], rest[3 * NG + 2], rest[3 * NG + 3]
        _load_once([(win_hbm, win_v)], sems)
        g, sh, sc = vec_ref[0:1, :], vec_ref[1:2, :], vec_ref[2:3, :]
        _, _, _, u = _norm_fwd(h_ref[...], g, sh, sc)
        ub = u.astype(BF16)
        u_ref[...] = ub
        p_ref[...] = _dot(ub, win_v[:, 0:PW])
        cos_t, sin_t = cos_ref[...], sin_ref[...]
        for j in range(3 * NG):
            col = PW + j * GA
            t = _dot(ub, win_v[:, col:col + GA])
            if j < 2 * NG:
                t = _rope(t, cos_t, sin_t)
            _to_residues(t, qkv_refs[j], scr_ref, DIL[j % NG])
        for j in range(GW // 512):
            col = PW + 3 * NG * GA + j * 512
            gates_ref[:, j * 512:(j + 1) * 512] = jax.nn.sigmoid(_dot(ub, win_v[:, col:col + 512])).astype(BF16)

    outs, c_outs = _call(
        body, "mix_proj", (T // TM,),
        [_rows(TM, D), _const((8, D)), ANY, _rows(TM, 128), _rows(TM, 128)],
        [_rows(TM, D), _rows(TM, PW)] + [_rm_spec(d) for d in DIL] * 3 + [_rows(TM, GW)],
        [_sds((T, D), BF16), _sds((T, PW), F32)] + [_sds((d, T // d, GA), BF16) for d in DIL] * 3 + [_sds((T, GW), BF16)],
        scratch=[pltpu.VMEM((D, INW), BF16), pltpu.VMEM((GA // LANES, TM, LANES), F32), pltpu.SemaphoreType.DMA((1,))],
        vmem=VMEM_BIG, comm=comm,
    )(h, vec, win, cos, sin)
    return (outs[0], outs[1], outs[2:2 + NG], outs[2 + NG:2 + 2 * NG], outs[2 + 2 * NG:2 + 3 * NG], outs[2 + 3 * NG]), c_outs


def _head_masks():
    lane_head = lax.broadcasted_iota(jnp.int32, (BLK, GA), 1) // HD
    return [lane_head == hd for hd in range(NH)]


def _expand_heads(t, hm):
    return jnp.concatenate([jnp.where(m, t, jnp.zeros_like(t)) for m in hm], axis=0)


def _collapse_heads(tb, hm):
    out = None
    for hd, m in enumerate(hm):
        part = jnp.where(m, tb[hd * BLK:(hd + 1) * BLK, :], 0.0)
        out = part if out is None else out + part
    return out


def _head_rows(t):
    return jnp.concatenate([t[:, hd * HD:hd * HD + 1] for hd in range(NH)], axis=0)


def _band(has_prev):
    a = lax.broadcasted_iota(jnp.int32, (NH * BLK, 2 * BLK), 0) & (BLK - 1)
    c = lax.broadcasted_iota(jnp.int32, (NH * BLK, 2 * BLK), 1)
    return jnp.logical_and(c >= jnp.where(has_prev, a, BLK), c <= a + BLK)


def _attn_specs(nbt):
    cur = pl.BlockSpec((2 * BLK, GA), lambda i: (i, 0))
    prev = pl.BlockSpec((BLK, GA), lambda i: (jnp.maximum(2 * i - 1, 0), 0))
    nxt = pl.BlockSpec((BLK, GA), lambda i: (jnp.minimum(2 * i + 2, nbt - 1), 0))
    return cur, prev, nxt


def _attn_fwd(q, k, v, nb, name, comm=None):
    T = q.shape[0]
    nbt = T // BLK
    lo, hi = slice(0, BLK), slice(BLK, 2 * BLK)

    def block(qv, kcat, vcat, has_prev, hm):
        s = jnp.where(_band(has_prev), _dot_nt(_expand_heads(qv, hm), kcat) * SCALE, NEG)
        mx = jnp.max(s, axis=-1, keepdims=True)
        e = jnp.exp(s - mx)
        l = jnp.sum(e, axis=-1, keepdims=True)
        ob = _dot((e * (1.0 / l)).astype(BF16), vcat)
        return _collapse_heads(ob, hm), _collapse_heads(jnp.broadcast_to(mx + jnp.log(l), (NH * BLK, GA)), hm)

    def body(q_ref, k_ref, kp_ref, v_ref, vp_ref, o_ref, lse_ref):
        b0 = 2 * pl.program_id(0)
        hm = _head_masks()
        k_first = jnp.concatenate([kp_ref[...], k_ref[lo, :]], axis=0)
        v_first = jnp.concatenate([vp_ref[...], v_ref[lo, :]], axis=0)
        o_ref[lo, :], lse_ref[lo, :] = block(q_ref[lo, :], k_first, v_first, (b0 & (nb - 1)) != 0, hm)
        o_ref[hi, :], lse_ref[hi, :] = block(q_ref[hi, :], k_ref[...], v_ref[...], ((b0 + 1) & (nb - 1)) != 0, hm)

    cur, prev, _ = _attn_specs(nbt)
    return _call(body, name, (nbt // 2,), [cur, cur, prev, cur, prev], [cur, cur],
                 [_sds((T, GA), F32), _sds((T, GA), F32)], comm=comm)(q, k, k, v, v)


def _attn_bwd(q, k, v, do, lse, e, nb, name, comm=None):
    T = q.shape[0]
    nbt = T // BLK

    lo, hi = slice(0, BLK), slice(BLK, 2 * BLK)

    def probs_and_ds(qb, dob, kcat, vcat, lsev, ev, valid):
        p = jnp.where(valid, jnp.exp(_dot_nt(qb, kcat) * SCALE - _head_rows(lsev)), 0.0)
        return p, (p * (_dot_nt(dob, vcat) + _head_rows(ev))).astype(BF16)

    def body(q_ref, k_ref, v_ref, do_ref, lse_ref, e_ref, kp_ref, vp_ref, qn_ref, don_ref, lsen_ref, en_ref,
             dq_ref, dk_ref, dv_ref):
        b0 = 2 * pl.program_id(0)
        hm = _head_masks()
        q1, q2, q3 = _expand_heads(q_ref[lo, :], hm), _expand_heads(q_ref[hi, :], hm), _expand_heads(qn_ref[...], hm)
        do1, do2, do3 = (_expand_heads(do_ref[lo, :], hm), _expand_heads(do_ref[hi, :], hm),
                         _expand_heads(don_ref[...], hm))
        k1 = jnp.concatenate([kp_ref[...], k_ref[lo, :]], axis=0)
        v1 = jnp.concatenate([vp_ref[...], v_ref[lo, :]], axis=0)
        k2, v2 = k_ref[...], v_ref[...]
        p1, ds1 = probs_and_ds(q1, do1, k1, v1, lse_ref[lo, :], e_ref[lo, :], _band((b0 & (nb - 1)) != 0))
        p2, ds2 = probs_and_ds(q2, do2, k2, v2, lse_ref[hi, :], e_ref[hi, :], _band(((b0 + 1) & (nb - 1)) != 0))
        dq_ref[lo, :] = _collapse_heads(_dot(ds1, k1) * SCALE, hm)
        dq_ref[hi, :] = _collapse_heads(_dot(ds2, k2) * SCALE, hm)
        a = lax.broadcasted_iota(jnp.int32, (NH * BLK, BLK), 0) & (BLK - 1)
        c = lax.broadcasted_iota(jnp.int32, (NH * BLK, BLK), 1)
        valid3 = jnp.logical_and(c >= a, ((b0 + 2) & (nb - 1)) != 0)
        p3, ds3 = probs_and_ds(q3, do3, k_ref[hi, :], v_ref[hi, :], lsen_ref[...], en_ref[...], valid3)
        q12, q23 = jnp.concatenate([q1, q2], axis=0), jnp.concatenate([q2, q3], axis=0)
        do12, do23 = jnp.concatenate([do1, do2], axis=0), jnp.concatenate([do2, do3], axis=0)
        dk_ref[lo, :] = _dot_tn(jnp.concatenate([ds1[:, BLK:], ds2[:, :BLK]], axis=0), q12) * SCALE
        dk_ref[hi, :] = _dot_tn(jnp.concatenate([ds2[:, BLK:], ds3], axis=0), q23) * SCALE
        pb1, pb2, pb3 = p1.astype(BF16), p2.astype(BF16), p3.astype(BF16)
        dv_ref[lo, :] = _dot_tn(jnp.concatenate([pb1[:, BLK:], pb2[:, :BLK]], axis=0), do12).astype(BF16)
        dv_ref[hi, :] = _dot_tn(jnp.concatenate([pb2[:, BLK:], pb3], axis=0), do23).astype(BF16)

    cur, prev, nxt = _attn_specs(nbt)
    return _call(body, name, (nbt // 2,), [cur] * 6 + [prev, prev] + [nxt] * 4, [cur, cur, cur],
                 [_sds((T, GA), F32), _sds((T, GA), F32), _sds((T, GA), BF16)],
                 comm=comm)(q, k, v, do, lse, e, k, v, q, do, lse, e)


def _flat(t):
    return t.reshape(t.shape[0] * t.shape[1], t.shape[2])


def _by_residue(t, dil):
    return t.reshape(dil, t.shape[0] // dil, t.shape[1])


def _pool_consts(shape, row0):
    lane = lax.broadcasted_iota(jnp.int32, shape, 1)
    t = lax.broadcasted_iota(jnp.int32, shape, 0) + row0
    grp = lane // (PW // len(POOL_WINDOWS))
    win = jnp.where(grp == 0, POOL_WINDOWS[0], jnp.where(grp == 1, POOL_WINDOWS[1],
                    jnp.where(grp == 2, POOL_WINDOWS[2], POOL_WINDOWS[3])))
    cnt = jnp.minimum(t + 1, win).astype(F32)
    return grp, cnt


def _window_sums(ext_ref, base, step, tm):
    outs, run = [], None
    for j in range(POOL_WINDOWS[-1]):
        sl = ext_ref[pl.ds(base + step * j, tm), :]
        run = sl if run is None else run + sl
        if j + 1 in POOL_WINDOWS:
            outs.append(run)
    return outs


def _select_group(grp, vals):
    return jnp.where(grp == 0, vals[0], jnp.where(grp == 1, vals[1], jnp.where(grp == 2, vals[2], vals[3])))


def _pool_d(pc_ref, pp_ref, ext_ref, i, tm):
    ext_ref[0:HALO, :] = jnp.where(i > 0, pp_ref[tm - HALO:tm, :], 0.0)
    ext_ref[HALO:HALO + tm, :] = pc_ref[...]
    grp, cnt = _pool_consts((tm, PW), i * tm)
    sums = _window_sums(ext_ref, HALO, -1, tm)
    return _select_group(grp, sums) / cnt - pc_ref[...]


def _group_weights(ls):
    mx = jnp.maximum(jnp.maximum(ls[0], ls[1]), ls[2])
    es = [jnp.exp(l - mx) for l in ls]
    inv = 1.0 / (es[0] + es[1] + es[2])
    return [e * inv for e in es]


def _mix_merge(h, vec, p, os, lses, gates, wp_bd, pscale, wpb, wab, wout):
    T = h.shape[0]

    def body(h_ref, vec_ref, pc_ref, pp_ref, o0, o1, o2, l0, l1, l2, gates_ref, wp_ref, ps_ref, wpb_ref, wab_ref, wout_ref,
             ho_ref, yp_ref, ya_ref, mg_ref, mo_ref, d_ref, ext_ref, scr_ref):
        i = pl.program_id(0)
        gt = vec_ref[3:4, :]
        d = _pool_d(pc_ref, pp_ref, ext_ref, i, TM).astype(BF16)
        d_ref[...] = d
        ypool = (_dot(d, wp_ref[...]) * ps_ref[0:1, :]).astype(BF16)
        yp_ref[...] = ypool
        w = _group_weights([_from_residues(r, scr_ref, dl) for r, dl in zip((l0, l1, l2), DIL)])
        yattn = None
        for wg, o_ref, dl in zip(w, (o0, o1, o2), DIL):
            part = wg * _from_residues(o_ref, scr_ref, dl)
            yattn = part if yattn is None else yattn + part
        yattn = yattn.astype(BF16)
        ya_ref[...] = yattn
        merged = (gates_ref[:, 0:D].astype(F32) * _dot(ypool, wpb_ref[...])
                  + gates_ref[:, D:GW].astype(F32) * _dot(yattn, wab_ref[...])).astype(BF16)
        mg_ref[...] = merged
        mo = _dot(merged, wout_ref[...])
        mo_ref[...] = mo.astype(BF16)
        ho_ref[...] = h_ref[...] + gt * mo

    prev = pl.BlockSpec((TM, PW), lambda i: (jnp.maximum(i - 1, 0), 0))
    return _call(
        body, "mix_merge", (T // TM,),
        [_rows(TM, D), _const((8, D)), _rows(TM, PW), prev] + [_rm_spec(dl) for dl in DIL] * 2 + [_rows(TM, GW), _const((PW, PW)),
         _const((8, PW)), _const((PW, D)), _const((GA, D)), _const((D, D))],
        [_rows(TM, D), _rows(TM, PW), _rows(TM, GA), _rows(TM, D), _rows(TM, D), _rows(TM, PW)],
        [_sds((T, D), F32), _sds((T, PW), BF16), _sds((T, GA), BF16), _sds((T, D), BF16), _sds((T, D), BF16), _sds((T, PW), BF16)],
        scratch=[pltpu.VMEM((TM + HALO, PW), F32), pltpu.VMEM((GA // LANES, TM, LANES), F32)],
        vmem=VMEM_BIG,
    )(h, vec, p, p, *os, *lses, gates, wp_bd, pscale, wpb, wab, wout)[0]


def _mix_bwd_a(dh, vec, mixout, gates, ypool, yattn, dpool, os, lses, wp_bd, pscale, wpb, wab, wout, ones_bd, comm=None):
    T = dh.shape[0]

    def body(dh_ref, vec_ref, mo_ref, gates_ref, yp_ref, ya_ref, d_ref, o0, o1, o2, l0, l1, l2,
             wp_ref, ps_ref, wpb_ref, wab_ref, wout_ref, ones_ref,
             dmo_ref, dp_ref, da_ref, dgates_ref, do0, do1, do2, e0, e1, e2, dd_ref, dyp_ref, acc_ref, acc2_ref, scr_ref):
        _zero_first(acc_ref)
        _zero_first(acc2_ref)
        gt = vec_ref[3:4, :]
        dho = dh_ref[...]
        acc_ref[3:4, :] += _colsum(dho * mo_ref[...].astype(F32))
        dmo = (gt * dho).astype(BF16)
        dmo_ref[...] = dmo
        dmerged = _dot_nt(dmo, wout_ref[...])
        gp = gates_ref[:, 0:D].astype(F32)
        ga = gates_ref[:, D:GW].astype(F32)
        bp = _dot(yp_ref[...], wpb_ref[...])
        ba = _dot(ya_ref[...], wab_ref[...])
        dgates_ref[:, 0:D] = (dmerged * bp * gp * (1.0 - gp)).astype(BF16)
        dgates_ref[:, D:GW] = (dmerged * ba * ga * (1.0 - ga)).astype(BF16)
        dbp = (dmerged * gp).astype(BF16)
        dba = (dmerged * ga).astype(BF16)
        dp_ref[...] = dbp
        da_ref[...] = dba
        dypool = _dot_nt(dbp, wpb_ref[...])
        ypre = _dot(d_ref[...], wp_ref[...])
        acc2_ref[0:1, :] += _colsum(dypool * ypre)
        dyp = (dypool * ps_ref[0:1, :]).astype(BF16)
        dyp_ref[...] = dyp
        dd_ref[...] = _dot_nt(dyp, wp_ref[...])
        dya = _dot_nt(dba, wab_ref[...])
        w = _group_weights([_from_residues(r, scr_ref, dl) for r, dl in zip((l0, l1, l2), DIL)])
        ya = None
        for wg, o_ref, dl in zip(w, (o0, o1, o2), DIL):
            part = wg * _from_residues(o_ref, scr_ref, dl)
            ya = part if ya is None else ya + part
        prod = dya * ya
        hi = prod.astype(BF16)
        lo = (prod - hi.astype(F32)).astype(BF16)
        tot = _dot(hi, ones_ref[...]) + _dot(lo, ones_ref[...])
        for wg, do_ref, e_ref, dl in zip(w, (do0, do1, do2), (e0, e1, e2), DIL):
            _to_residues(wg * dya, do_ref, scr_ref, dl)
            _to_residues(-wg * tot, e_ref, scr_ref, dl)

    return _call(
        body, "mix_bwd_a", (T // TM,),
        [_rows(TM, D), _const((8, D)), _rows(TM, D), _rows(TM, GW), _rows(TM, PW), _rows(TM, GA), _rows(TM, PW)]
        + [_rm_spec(dl) for dl in DIL] * 2
        + [_const((PW, PW)), _const((8, PW)), _const((PW, D)), _const((GA, D)), _const((D, D)), _const((GA, GA))],
        [_rows(TM, D)] * 3 + [_rows(TM, GW)] + [_rm_spec(dl) for dl in DIL] * 2
        + [_rows(TM, PW), _rows(TM, PW), _const((8, D)), _const((8, PW))],
        [_sds((T, D), BF16)] * 3 + [_sds((T, GW), BF16)] + [_sds((dl, T // dl, GA), BF16) for dl in DIL]
        + [_sds((dl, T // dl, GA), F32) for dl in DIL]
        + [_sds((T, PW), F32), _sds((T, PW), BF16), _sds((8, D), F32), _sds((8, PW), F32)],
        scratch=[pltpu.VMEM((GA // LANES, TM, LANES), F32)],
        vmem=VMEM_BIG, comm=comm,
    )(dh, vec, mixout, gates, ypool, yattn, dpool, *os, *lses, wp_bd, pscale, wpb, wab, wout, ones_bd)


def _mix_bwd_b(dh, h, vec, dd, dqs, dks, dvs, dgates, cos, sin, win):
    T = h.shape[0]
    nt = T // TM

    def body(dh_ref, h_ref, vec_ref, ddc_ref, ddn_ref, *rest):
        qk_refs, dv_refs = rest[:2 * NG], rest[2 * NG:3 * NG]
        dgates_ref, cos_ref, sin_ref, win_hbm, dhi_ref, dproj_ref, acc_ref, win_v, ext_ref, scr_ref, sems = rest[3 * NG:]
        i = pl.program_id(0)
        _load_once([(win_hbm, win_v)], sems)
        _zero_first(acc_ref)
        g, sh, sc = vec_ref[0:1, :], vec_ref[1:2, :], vec_ref[2:3, :]
        grp, cnt = _pool_consts((TM, PW), i * TM)
        _, cnt_n = _pool_consts((HALO, PW), (i + 1) * TM)
        ext_ref[0:TM, :] = ddc_ref[...] / cnt
        ext_ref[TM:TM + HALO, :] = jnp.where(i < nt - 1, ddn_ref[0:HALO, :] / cnt_n, 0.0)
        dp = _select_group(grp, _window_sums(ext_ref, 0, 1, TM)) - ddc_ref[...]
        dproj_ref[:, 0:PW] = dp.astype(BF16)
        cos_t, sin_t = cos_ref[...], sin_ref[...]
        for j in range(2 * NG):
            col = PW + j * GA
            dt = _from_residues(qk_refs[j], scr_ref, DIL[j % NG])
            dproj_ref[:, col:col + GA] = _rope_bwd(dt, cos_t, sin_t).astype(BF16)
        for j in range(NG):
            col = PW + (2 * NG + j) * GA
            dproj_ref[:, col:col + GA] = _from_residues(dv_refs[j], scr_ref, DIL[j]).astype(BF16)
        dproj_ref[:, PW + 3 * NG * GA:INW] = dgates_ref[...]
        du = None
        for j in range(INW // 512):
            part = _dot_nt(dproj_ref[:, j * 512:(j + 1) * 512], win_v[:, j * 512:(j + 1) * 512])
            du = part if du is None else du + part
        xh, r, n, _ = _norm_fwd(h_ref[...], g, sh, sc)
        dhn, dsh, dsc, dg = _norm_bwd(du, xh, r, n, g, sc)
        dhi_ref[...] = dh_ref[...] + dhn
        acc_ref[0:1, :] += dsh
        acc_ref[1:2, :] += dsc
        acc_ref[2:3, :] += dg

    nxt = pl.BlockSpec((TM, PW), lambda i: (jnp.minimum(i + 1, nt - 1), 0))
    return _call(
        body, "mix_bwd_b", (nt,),
        [_rows(TM, D), _rows(TM, D), _const((8, D)), _rows(TM, PW), nxt] + [_rm_spec(dl) for dl in DIL] * 3
        + [_rows(TM, GW), _rows(TM, 128), _rows(TM, 128), ANY],
        [_rows(TM, D), _rows(TM, INW), _const((8, D))],
        [_sds((T, D), F32), _sds((T, INW), BF16), _sds((8, D), F32)],
        scratch=[pltpu.VMEM((D, INW), BF16), pltpu.VMEM((TM + HALO, PW), F32), pltpu.VMEM((GA // LANES, TM, LANES), F32),
                 pltpu.SemaphoreType.DMA((1,))],
        vmem=VMEM_BIG,
    )(dh, h, vec, dd, dd, *dqs, *dks, *dvs, dgates, cos, sin, win)[0]


def _ada_fwd(c_all, w_shard, b_shard):
    n = w_shard.shape[1]

    def body(c_ref, w_ref, b_ref, o_ref):
        cv = c_ref[...]
        cond = (cv * jax.nn.sigmoid(cv)).astype(BF16)
        o_ref[...] = _dot(cond, w_ref[...].astype(BF16)) + b_ref[...]

    tn = n // 3
    return pl.pallas_call(
        body, name="ada_fwd", grid=(3,),
        in_specs=[pl.BlockSpec((8, D), lambda j: (0, 0)), pl.BlockSpec((D, tn), lambda j: (0, j)), pl.BlockSpec((1, tn), lambda j: (0, j))],
        out_specs=pl.BlockSpec((8, tn), lambda j: (0, j)), out_shape=_sds((8, n), F32),
        compiler_params=pltpu.CompilerParams(dimension_semantics=("arbitrary",)),
    )(c_all, w_shard, b_shard)


def _ada_bwd(c_all, dmod_shard):
    n = dmod_shard.shape[1]

    def body(c_ref, d_ref, o_ref):
        cv = c_ref[...]
        cond = (cv * jax.nn.sigmoid(cv)).astype(BF16)
        o_ref[...] = _dot_tn(cond, d_ref[...].astype(BF16))

    tn = n // 3
    return pl.pallas_call(
        body, name="ada_bwd", grid=(3,),
        in_specs=[pl.BlockSpec((8, D), lambda j: (0, 0)), pl.BlockSpec((8, tn), lambda j: (0, j))],
        out_specs=pl.BlockSpec((D, tn), lambda j: (0, j)), out_shape=_sds((D, n), F32),
        compiler_params=pltpu.CompilerParams(dimension_semantics=("arbitrary",)),
    )(c_all, dmod_shard)


def _adam_math(w, g, m, v):
    m2 = B1 * m + (1.0 - B1) * g
    v2 = B2 * v + (1.0 - B2) * (g * g)
    m_hat = m2 / (1.0 - B1 ** STEP)
    v_hat = v2 / (1.0 - B2 ** STEP)
    delta = -LR * (m_hat / (jnp.sqrt(v_hat) + AEPS) + WD * w)
    return delta, m2, v2


def _adam(w, m, v, parts, name, comm=None):
    R, C = w.shape
    tr = R
    for cand in (128, 64, 32, 16, 8):
        if R % cand == 0:
            tr = cand
            break
    np_ = len(parts)

    def body(w_ref, m_ref, v_ref, *rest):
        p_refs, (g_ref, d_ref, m2_ref, v2_ref) = rest[:np_], rest[np_:]
        g = p_refs[0][...]
        for pr in p_refs[1:]:
            g = g + pr[...]
        delta, m2, v2 = _adam_math(w_ref[...], g, m_ref[...], v_ref[...])
        g_ref[...] = g
        d_ref[...] = delta
        m2_ref[...] = m2
        v2_ref[...] = v2

    spec = pl.BlockSpec((tr, C), lambda i: (i, 0))
    return _call(body, name, (R // tr,), [spec] * (3 + np_), [spec] * 4, [_sds((R, C), F32)] * 4,
                 vmem=VMEM_BIG, comm=comm)(w, m, v, *parts)


def _adam_halves(w, m, v, mine, other, name):
    R, C = w.shape
    tr = 128
    nh = R // 2 // tr

    def body(c_ref, w_ref, m_ref, v_ref, mine_ref, other_ref, g_ref, d_ref, m2_ref, v2_ref):
        i = pl.program_id(0)
        in_mine = jnp.logical_and(i >= c_ref[0] * nh, i < (c_ref[0] + 1) * nh)
        g = jnp.where(in_mine, mine_ref[...], other_ref[...])
        delta, m2, v2 = _adam_math(w_ref[...], g, m_ref[...], v_ref[...])
        g_ref[...] = g
        d_ref[...] = delta
        m2_ref[...] = m2
        v2_ref[...] = v2

    spec = pl.BlockSpec((tr, C), lambda i, c: (i, 0))
    grid_spec = pltpu.PrefetchScalarGridSpec(
        num_scalar_prefetch=1, grid=(R // tr,),
        in_specs=[spec] * 3 + [pl.BlockSpec((tr, C), lambda i, c: (jnp.clip(i - c[0] * nh, 0, nh - 1), 0)),
                               pl.BlockSpec((tr, C), lambda i, c: (jnp.clip(i - (1 - c[0]) * nh, 0, nh - 1), 0))],
        out_specs=[spec] * 4)
    return pl.pallas_call(
        body, name=name, grid_spec=grid_spec, out_shape=[_sds((R, C), F32)] * 4,
        compiler_params=pltpu.CompilerParams(dimension_semantics=("arbitrary",), vmem_limit_bytes=VMEM_BIG),
    )(lax.axis_index("c").astype(jnp.int32).reshape(1), w, m, v, mine, other)


def _adam_small(ws, ms, vs, gathered):
    n = len(ws)
    sizes = [a.shape[1] for a in ws]

    def total(ga_ref, off, size):
        g = ga_ref[0, :, off:off + size]
        for dev in range(1, 8):
            g = g + ga_ref[dev, :, off:off + size]
        return g

    def body(*refs):
        w_refs, m_refs, v_refs, ga_ref, outs = refs[:n], refs[n:2 * n], refs[2 * n:3 * n], refs[3 * n], refs[3 * n + 1:]
        off = 0
        for j, size in enumerate(sizes):
            g = total(ga_ref, off, size)
            delta, m2, v2 = _adam_math(w_refs[j][...], g, m_refs[j][...], v_refs[j][...])
            for ref, val in zip(outs[4 * j:4 * j + 4], (g, delta, m2, v2)):
                ref[...] = val
            off += size
        outs[4 * n][...] = total(ga_ref, off, 128)

    res = pl.pallas_call(
        body, name="adam_small",
        out_shape=[_sds((1, size), F32) for size in sizes for _ in range(4)] + [_sds((1, 128), F32)],
    )(*ws, *ms, *vs, gathered)
    return [res[4 * j:4 * j + 4] for j in range(n)], res[4 * n]


def _sum4(blocks, name):
    _, R, C = blocks.shape
    tr = R
    for cand in (256, 128, 64, 32, 16):
        if R % cand == 0:
            tr = cand
            break

    def body(r_ref, out_ref):
        out_ref[...] = ((r_ref[0].astype(F32) + r_ref[1].astype(F32)) + r_ref[2].astype(F32)) + r_ref[3].astype(F32)

    return pl.pallas_call(
        body, name=name, grid=(R // tr,),
        in_specs=[pl.BlockSpec((4, tr, C), lambda i: (0, i, 0))],
        out_specs=pl.BlockSpec((tr, C), lambda i: (i, 0)), out_shape=_sds((R, C), F32),
        compiler_params=pltpu.CompilerParams(dimension_semantics=("arbitrary",)),
    )(blocks)


def _place():
    return lax.axis_index("x"), lax.axis_index("y"), lax.axis_index("c")


def _chip_peer(x, y, c, m):
    return (x ^ (m >> 1), y ^ (m & 1), c)


def _shard_ref(ref, axis, k, n):
    start = pl.multiple_of(k * n, 128 if axis == 1 else 16)
    return ref.at[:, pl.ds(start, n)] if axis == 1 else ref.at[pl.ds(start, n), :]


def _half_rows(ref, axis, k, n, hc):
    if axis == 1:
        half = ref.shape[0] // 2
        return ref.at[pl.ds(pl.multiple_of(hc * half, 16), half), pl.ds(pl.multiple_of(k * n, 128), n)]
    half = n // 2
    return ref.at[pl.ds(pl.multiple_of(k * n + hc * half, 16), half), :]


class _GatherPlan:
    def __init__(self, shards, axes):
        self.inputs, self.axes, nw = list(shards), list(axes), len(shards)
        self.out_shapes = [_sds((s.shape[0] * (4 if ax == 0 else 1), s.shape[1] * (4 if ax == 1 else 1)), BF16)
                           for s, ax in zip(shards, axes)]
        self.sem_shapes = [pltpu.SemaphoreType.DMA((nw,))] + [pltpu.SemaphoreType.DMA((nw, 3))] * 4

    def _copies(self, ins, outs, sems):
        local_sems, send_sems, recv_sems, pass_sems, got_sems = sems
        x, y, c = _place()
        k = 2 * x + y
        local, sends, arrivals, passes, handed = [], [], [], [], []
        for j, ax in enumerate(self.axes):
            n = ins[j].shape[ax]
            half = ins[j].shape[0] // 2
            local.append(pltpu.make_async_copy(ins[j], _shard_ref(outs[j], ax, k, n), local_sems.at[j]))
            my_half = ins[j].at[pl.ds(pl.multiple_of(c * half, 16), half), :]
            for m in range(1, 4):
                sends.append(pltpu.make_async_remote_copy(
                    src_ref=my_half, dst_ref=_half_rows(outs[j], ax, k, n, c), send_sem=send_sems.at[j, m - 1],
                    recv_sem=recv_sems.at[j, m - 1], device_id=_chip_peer(x, y, c, m), device_id_type=MESH))
                theirs = _half_rows(outs[j], ax, k ^ m, n, c)
                arrivals.append(pltpu.make_async_remote_copy(
                    src_ref=my_half, dst_ref=theirs, send_sem=send_sems.at[j, m - 1], recv_sem=recv_sems.at[j, m - 1],
                    device_id=(x, y, c), device_id_type=MESH))
                passes.append(pltpu.make_async_remote_copy(
                    src_ref=theirs, dst_ref=theirs, send_sem=pass_sems.at[j, m - 1], recv_sem=got_sems.at[j, m - 1],
                    device_id=(x, y, 1 - c), device_id_type=MESH))
                other = _half_rows(outs[j], ax, k ^ m, n, 1 - c)
                handed.append(pltpu.make_async_remote_copy(
                    src_ref=other, dst_ref=other, send_sem=pass_sems.at[j, m - 1], recv_sem=got_sems.at[j, m - 1],
                    device_id=(x, y, c), device_id_type=MESH))
        return local, sends, arrivals, passes, handed

    def start(self, ins, outs, sems):
        local, sends, _, _, _ = self._copies(ins, outs, sems)
        for cp in local + sends:
            cp.start()

    def wait(self, ins, outs, sems):
        local, sends, arrivals, passes, handed = self._copies(ins, outs, sems)
        for arrived, onward in zip(arrivals, passes):
            arrived.wait_recv()
            onward.start()
        for cp in handed:
            cp.wait_recv()
        for cp in sends + passes:
            cp.wait_send()
        for cp in local:
            cp.wait()


class _ScatterPlan:
    def __init__(self, grads, axes):
        self.inputs, self.axes, nw = list(grads), list(axes), len(grads)
        self.shard_shapes = [(g.shape[0] // (4 if ax == 0 else 1), g.shape[1] // (4 if ax == 1 else 1))
                             for g, ax in zip(grads, axes)]
        self.out_shapes = [_sds((4,) + s, BF16) for s in self.shard_shapes]
        self.sem_shapes = [pltpu.SemaphoreType.DMA((nw,)), pltpu.SemaphoreType.DMA((nw, 3)), pltpu.SemaphoreType.DMA((nw, 3))]

    def _copies(self, ins, outs, sems):
        local_sems, send_sems, recv_sems = sems
        x, y, c = _place()
        k = 2 * x + y
        local, remote, arrivals = [], [], []
        for j, ax in enumerate(self.axes):
            n = self.shard_shapes[j][ax]
            local.append(pltpu.make_async_copy(_shard_ref(ins[j], ax, k, n), outs[j].at[0], local_sems.at[j]))
            for m in range(1, 4):
                remote.append(pltpu.make_async_remote_copy(
                    src_ref=_shard_ref(ins[j], ax, k ^ m, n), dst_ref=outs[j].at[m],
                    send_sem=send_sems.at[j, m - 1], recv_sem=recv_sems.at[j, m - 1],
                    device_id=_chip_peer(x, y, c, m), device_id_type=MESH))
                arrivals.append(pltpu.make_async_remote_copy(
                    src_ref=_shard_ref(ins[j], ax, k, n), dst_ref=outs[j].at[m],
                    send_sem=send_sems.at[j, m - 1], recv_sem=recv_sems.at[j, m - 1],
                    device_id=(x, y, c), device_id_type=MESH))
        return local, remote, arrivals

    def start(self, ins, outs, sems):
        local, remote, _ = self._copies(ins, outs, sems)
        for cp in local + remote:
            cp.start()

    def wait(self, ins, outs, sems):
        local, remote, arrivals = self._copies(ins, outs, sems)
        for cp in arrivals:
            cp.wait_recv()
        for cp in remote:
            cp.wait_send()
        for cp in local:
            cp.wait()


def _run_plan(plan, name):
    nc = len(plan.inputs)

    def body(*refs):
        ins, outs, sems = refs[:nc], refs[nc:2 * nc], refs[2 * nc:]
        plan.start(ins, outs, sems)
        plan.wait(ins, outs, sems)

    return pl.pallas_call(body, name=name, in_specs=[ANY] * nc, out_specs=[ANY] * nc, out_shape=list(plan.out_shapes),
                          scratch_shapes=list(plan.sem_shapes))(*plan.inputs)


class _SwapPlan:
    def __init__(self, parts):
        self.inputs, nw = list(parts), len(parts)
        self.out_shapes = [_sds(p.shape, p.dtype) for p in parts]
        self.sem_shapes = [pltpu.SemaphoreType.DMA((nw,)), pltpu.SemaphoreType.DMA((nw,))]

    def _copies(self, ins, outs, sems):
        send_sems, recv_sems = sems
        x, y, c = _place()
        return [pltpu.make_async_remote_copy(
            src_ref=ins[j], dst_ref=outs[j], send_sem=send_sems.at[j], recv_sem=recv_sems.at[j],
            device_id=(x, y, 1 - c), device_id_type=MESH) for j in range(len(ins))]

    def start(self, ins, outs, sems):
        for cp in self._copies(ins, outs, sems):
            cp.start()

    def wait(self, ins, outs, sems):
        for cp in self._copies(ins, outs, sems):
            cp.wait()


class _SmallGatherPlan:
    def __init__(self, v):
        self.inputs = [v]
        self.out_shapes = [_sds((8,) + v.shape, v.dtype)]
        self.sem_shapes = [pltpu.SemaphoreType.DMA((1,)), pltpu.SemaphoreType.DMA((7,)), pltpu.SemaphoreType.DMA((7,))]

    def _copies(self, ins, outs, sems):
        (v_ref,), (out_ref,), (local_sem, send_sems, recv_sems) = ins, outs, sems
        x, y, c = _place()
        me = 4 * x + 2 * y + c
        local = pltpu.make_async_copy(v_ref, out_ref.at[me], local_sem.at[0])
        sends, arrivals = [], []
        for m in range(1, 8):
            px, py, pc = x ^ (m >> 2), y ^ ((m >> 1) & 1), c ^ (m & 1)
            sends.append(pltpu.make_async_remote_copy(
                src_ref=v_ref, dst_ref=out_ref.at[me], send_sem=send_sems.at[m - 1], recv_sem=recv_sems.at[m - 1],
                device_id=(px, py, pc), device_id_type=MESH))
            arrivals.append(pltpu.make_async_remote_copy(
                src_ref=v_ref, dst_ref=out_ref.at[4 * px + 2 * py + pc], send_sem=send_sems.at[m - 1],
                recv_sem=recv_sems.at[m - 1], device_id=(x, y, c), device_id_type=MESH))
        return local, sends, arrivals

    def start(self, ins, outs, sems):
        local, sends, _ = self._copies(ins, outs, sems)
        for cp in [local] + sends:
            cp.start()

    def wait(self, ins, outs, sems):
        local, sends, arrivals = self._copies(ins, outs, sems)
        for cp in arrivals:
            cp.wait_recv()
        for cp in sends:
            cp.wait_send()
        local.wait()


class _PlanGroup:
    def __init__(self, plans):
        self.plans = [p for p in plans if p is not None]
        self.inputs = [a for p in self.plans for a in p.inputs]
        self.out_shapes = [s for p in self.plans for s in p.out_shapes]
        self.sem_shapes = [s for p in self.plans for s in p.sem_shapes]

    def _each(self, ins, outs, sems):
        i = s = 0
        for p in self.plans:
            n, ns = len(p.inputs), len(p.sem_shapes)
            yield p, ins[i:i + n], outs[i:i + n], sems[s:s + ns]
            i, s = i + n, s + ns

    def start(self, ins, outs, sems):
        for p, pi, po, ps in self._each(ins, outs, sems):
            p.start(pi, po, ps)

    def wait(self, ins, outs, sems):
        for p, pi, po, ps in self._each(ins, outs, sems):
            p.wait(pi, po, ps)

    def split(self, outs):
        res, i = [], 0
        for p in self.plans:
            res.append(outs[i:i + len(p.inputs)])
            i += len(p.inputs)
        return res


BIG = ("w_ffn1_in", "w_ffn1_out", "w_in", "w_pool_branch", "w_attn_branch", "w_out", "w_ffn2_in", "w_ffn2_out")
BIG_AXIS = {"w_ffn1_in": 1, "w_ffn1_out": 0, "w_in": 1, "w_pool_branch": 1, "w_attn_branch": 1, "w_out": 0,
            "w_ffn2_in": 1, "w_ffn2_out": 0}


class _Sharded:
    fused_scatter = True

    def __init__(self, shards):
        self.shards, self.full, self.recv = shards, {}, {}

    def gather_plan(self, names):
        return _GatherPlan([self.shards[n] for n in names], [BIG_AXIS[n.split("/")[0]] for n in names])

    def gather_now(self, names):
        self.gathered(names, _run_plan(self.gather_plan(names), "gather_" + names[0]))

    def gathered(self, names, outs):
        self.full.update(zip(names, outs))

    def scatter_plan(self, names, grads):
        return _ScatterPlan([grads[n] for n in names], [BIG_AXIS[n] for n in names])

    def scatter_now(self, names, grads):
        self.scattered(names, _run_plan(self.scatter_plan(names, grads), "scatter_" + names[0]))

    def scattered(self, names, outs):
        self.recv.update(zip(names, outs))


class _Whole:
    fused_scatter = False

    def __init__(self, full):
        self.full, self.recv = dict(full), {}

    def gather_plan(self, names):
        return None

    def gather_now(self, names):
        pass

    def gathered(self, names, outs):
        pass

    def scatter_plan(self, names, grads):
        return None

    def scatter_now(self, names, grads):
        pass

    def scattered(self, names, outs):
        pass


def _vec(rows):
    pad = [jnp.zeros((1, D), F32)] * (8 - len(rows))
    return jnp.concatenate([r.reshape(1, D) for r in rows] + pad, axis=0)


def _block_diag(w_pool):
    n, c = w_pool.shape[0], w_pool.shape[1]
    eye = jnp.eye(n, dtype=w_pool.dtype)
    return (eye[:, None, :, None] * w_pool[:, :, None, :]).reshape(n * c, n * c)


FFN1 = ["w_ffn1_in", "w_ffn1_out"]


def _example_step(x, tgt, positions, mod, gains, w_pool, pool_scale, ws, pack=None):
    T = x.shape[0]
    assert (T // BLK // DIL[-1]) & (T // BLK // DIL[-1] - 1) == 0, "blocks per sequence must be a power of two"
    sh1, sc1, gt1, sh2, sc2, gt2, sh3, sc3, gt3 = [mod[j * D:(j + 1) * D] for j in range(NMOD)]
    g1, g2, g3, gf = gains
    vec1, vec2, vec3 = _vec([g1, sh1, sc1, gt1]), _vec([g2, sh2, sc2, gt2]), _vec([g3, sh3, sc3, gt3])
    inv_freq = 10000.0 ** (-jnp.arange(0, HD, 2, dtype=F32) / HD)
    ang = positions.astype(F32)[:, None] * inv_freq
    cos = jnp.tile(jnp.cos(ang), (1, 4))
    sin = jnp.tile(jnp.concatenate([-jnp.sin(ang), jnp.sin(ang)], axis=1), (1, 2))
    wp_bd = _block_diag(w_pool).astype(BF16)
    ones_bd = _block_diag(jnp.ones((NH, HD, HD), F32)).astype(BF16)
    ps = jnp.concatenate([pool_scale.reshape(1, PW), jnp.zeros((7, PW), F32)], axis=0)
    wb = ws.full

    if "w_ffn1_in" not in wb:
        ws.gather_now(FFN1)
    mixw =["w_in", "w_pool_branch", "w_attn_branch", "w_out"]
    (h1, u1, a1, b1, f1), got = _ffn_fwd(x, vec1, [wb["w_ffn1_in"]], wb["w_ffn1_out"], "ffn1_fwd", ws.gather_plan(mixw))
    ws.gathered(mixw, got)
    (u2, p, qs, ks, vs, gates), got = _mix_proj(h1, vec2, wb["w_in"], cos, sin, ws.gather_plan(["w_ffn2_in/0"]))
    ws.gathered(["w_ffn2_in/0"], got)
    qs, ks, vs = [_flat(t) for t in qs], [_flat(t) for t in ks], [_flat(t) for t in vs]
    nbs = [T // d // BLK for d in DIL]
    os, lses = [], []
    for gi, riders in enumerate((["w_ffn2_out"], ["w_ffn2_in/1"], None)):
        (o, lse), got = _attn_fwd(qs[gi], ks[gi], vs[gi], nbs[gi], f"attn_fwd{gi}", riders and ws.gather_plan(riders))
        ws.gathered(riders or [], got)
        os.append(o)
        lses.append(lse)
    win3 = [wb["w_ffn2_in/0"], wb["w_ffn2_in/1"]] if "w_ffn2_in/0" in wb else [wb["w_ffn2_in"]]
    os_r = [_by_residue(t, d) for t, d in zip(os, DIL)]
    lses_r = [_by_residue(t, d) for t, d in zip(lses, DIL)]
    h2, ypool, yattn, merged, mixout, dpool = _mix_merge(
        h1, vec2, p, os_r, lses_r, gates, wp_bd, ps, wb["w_pool_branch"], wb["w_attn_branch"], wb["w_out"])
    (dh3, u3, a3, b3, f3, lacc), _ = _ffn_fwd(h2, vec3, win3, wb["w_ffn2_out"], "ffn2_fwd", head=(tgt, _vec([gf])))
    loss = 0.5 * jnp.sum(lacc[0]) / D

    grads = {}

    def wgrad_cols(name, xx, yy, riders, extra=None):
        group = _PlanGroup([ws.scatter_plan(riders, grads) if riders else None, extra])
        plan = group if group.plans else None
        if ws.fused_scatter:
            blocks, got = _wgrad_scatter(xx, yy, "wg_" + name, min(2048, T // 2), comm=plan)
            ws.scattered([name], [blocks])
        else:
            grads[name], got = _wgrad(xx, yy, "wg_" + name, D, 512, 1024, comm=plan)
        parts = group.split(got)
        if len(parts) > (extra is not None):
            ws.scattered(riders, parts[0])
        return parts[-1] if extra is not None else None

    (dh2, dab3, s3, df3, acc3), _ = _ffn_bwd(dh3, h2, a3, b3, f3, vec3, win3, wb["w_ffn2_out"], "ffn2_bwd")
    grads["w_ffn2_out"], _ = _wgrad(s3, df3, "wg_ffn2_out", FC, 512, 1024)
    wgrad_cols("w_ffn2_in", u3, dab3, ["w_ffn2_out"])
    (dmo, dbp, dba, dgates, do0, do1, do2, e0, e1, e2, dd, dyp, acc2a, accps), _ = _mix_bwd_a(
        dh2, vec2, mixout, gates, ypool, yattn, dpool, os_r, lses_r, wp_bd, ps,
        wb["w_pool_branch"], wb["w_attn_branch"], wb["w_out"], ones_bd)
    grads["w_out"], _ = _wgrad(merged, dmo, "wg_out", D, 512, 1024)
    grads["w_pool_branch"], _ = _wgrad(ypool, dbp, "wg_pool_branch", PW, 512, 1024)
    grads["w_attn_branch"], _ = _wgrad(yattn, dba, "wg_attn_branch", GA, 512, 1024)
    gwp, _ = _wgrad(dpool, dyp, "wg_pool", PW, PW, 1024, out_dtype=F32)
    n = len(POOL_WINDOWS)
    c = PW // n
    grad_w_pool = jnp.stack([gwp[j * c:(j + 1) * c, j * c:(j + 1) * c] for j in range(n)], axis=0)
    small3 = ["w_out", "w_pool_branch", "w_attn_branch"]
    dqs, dks, dvs = [], [], []
    for gi, (do, e) in enumerate(((do0, e0), (do1, e1), (do2, e2))):
        plan = ws.scatter_plan(small3, grads) if gi == 0 else None
        (dq, dk, dv), got = _attn_bwd(qs[gi], ks[gi], vs[gi], _flat(do), lses[gi], _flat(e), nbs[gi], f"attn_bwd{gi}", plan)
        if gi == 0:
            ws.scattered(small3, got)
        dqs.append(_by_residue(dq, DIL[gi]))
        dks.append(_by_residue(dk, DIL[gi]))
        dvs.append(_by_residue(dv, DIL[gi]))
    dh1, dproj, acc2b = _mix_bwd_b(dh2, h1, vec2, dd, dqs, dks, dvs, dgates, cos, sin, wb["w_in"])
    wgrad_cols("w_in", u2, dproj, [])
    (dx, dab1, s1, df1, acc1), _ = _ffn_bwd(dh1, x, a1, b1, f1, vec1, [wb["w_ffn1_in"]], wb["w_ffn1_out"], "ffn1_bwd")
    grads["w_ffn1_out"], _ = _wgrad(s1, df1, "wg_ffn1_out", FC, 512, 1024)
    dmod = jnp.concatenate([acc1[0], acc1[1], acc1[3], acc2b[0], acc2b[1], acc2a[3], acc3[0], acc3[1], acc3[3]])
    dgains = jnp.stack([acc1[2], acc2b[2], acc3[2], lacc[1]], axis=0)
    row = None if pack is None else _SmallGatherPlan(pack(loss, dmod, dgains, grad_w_pool, accps[0]))
    rows = wgrad_cols("w_ffn1_in", u1, dab1, ["w_ffn1_out"], row)
    return loss, dx, dmod, dgains, grad_w_pool, accps[0], grads, None if rows is None else rows[0]


SMALL = ("b_ada", "g_norm_ffn1", "g_norm_mix", "g_norm_ffn2", "g_final", "pool_scale", "w_pool")
WEIGHTS = ("w_ada", "b_ada", "g_norm_ffn1", "w_ffn1_in", "w_ffn1_out", "g_norm_mix", "w_in", "w_pool", "pool_scale",
           "w_pool_branch", "w_attn_branch", "w_out", "g_norm_ffn2", "w_ffn2_in", "w_ffn2_out", "g_final")


def _pack_small(t):
    return jnp.concatenate([t[n].reshape(-1) for n in SMALL]).reshape(1, -1)


def kernel(x, c, positions, w_ada, b_ada, g_norm_ffn1, w_ffn1_in, w_ffn1_out, g_norm_mix, w_in, w_pool, pool_scale, w_pool_branch, w_attn_branch, w_out, g_norm_ffn2, w_ffn2_in, w_ffn2_out, g_final, loss_target, m_w_ada, m_b_ada, m_g_norm_ffn1, m_w_ffn1_in, m_w_ffn1_out, m_g_norm_mix, m_w_in, m_w_pool, m_pool_scale, m_w_pool_branch, m_w_attn_branch, m_w_out, m_g_norm_ffn2, m_w_ffn2_in, m_w_ffn2_out, m_g_final, v_w_ada, v_b_ada, v_g_norm_ffn1, v_w_ffn1_in, v_w_ffn1_out, v_g_norm_mix, v_w_in, v_w_pool, v_pool_scale, v_w_pool_branch, v_w_attn_branch, v_w_out, v_g_norm_ffn2, v_w_ffn2_in, v_w_ffn2_out, v_g_final):
    w = dict(w_ada=w_ada, b_ada=b_ada, g_norm_ffn1=g_norm_ffn1, w_ffn1_in=w_ffn1_in, w_ffn1_out=w_ffn1_out,
             g_norm_mix=g_norm_mix, w_in=w_in, w_pool=w_pool, pool_scale=pool_scale, w_pool_branch=w_pool_branch,
             w_attn_branch=w_attn_branch, w_out=w_out, g_norm_ffn2=g_norm_ffn2, w_ffn2_in=w_ffn2_in,
             w_ffn2_out=w_ffn2_out, g_final=g_final)
    mom = dict(w_ada=m_w_ada, b_ada=m_b_ada, g_norm_ffn1=m_g_norm_ffn1, w_ffn1_in=m_w_ffn1_in, w_ffn1_out=m_w_ffn1_out,
               g_norm_mix=m_g_norm_mix, w_in=m_w_in, w_pool=m_w_pool, pool_scale=m_pool_scale,
               w_pool_branch=m_w_pool_branch, w_attn_branch=m_w_attn_branch, w_out=m_w_out, g_norm_ffn2=m_g_norm_ffn2,
               w_ffn2_in=m_w_ffn2_in, w_ffn2_out=m_w_ffn2_out, g_final=m_g_final)
    var = dict(w_ada=v_w_ada, b_ada=v_b_ada, g_norm_ffn1=v_g_norm_ffn1, w_ffn1_in=v_w_ffn1_in, w_ffn1_out=v_w_ffn1_out,
               g_norm_mix=v_g_norm_mix, w_in=v_w_in, w_pool=v_w_pool, pool_scale=v_pool_scale,
               w_pool_branch=v_w_pool_branch, w_attn_branch=v_w_attn_branch, w_out=v_w_out, g_norm_ffn2=v_g_norm_ffn2,
               w_ffn2_in=v_w_ffn2_in, w_ffn2_out=v_w_ffn2_out, g_final=v_g_final)
    ix, iy, ic = _place()
    chip = 2 * ix + iy
    me = 4 * ix + 2 * iy + ic
    nada = w_ada.shape[2]

    shards = {n: w[n][0].astype(BF16) for n in BIG}
    half = D // 2
    shards["w_ffn2_in/0"], shards["w_ffn2_in/1"] = shards["w_ffn2_in"][:half], shards["w_ffn2_in"][half:]
    ws = _Sharded(shards)
    c_all = _run_plan(_SmallGatherPlan(c), "gather_c")[0][:, 0, :]
    b_shard = lax.dynamic_slice_in_dim(b_ada, chip * nada, nada, axis=1)
    mod_cols = _ada_fwd(c_all, w_ada[0], b_shard)
    first = _PlanGroup([_SmallGatherPlan(mod_cols), ws.gather_plan(FFN1)])
    (mod_all,), ffn1 = first.split(_run_plan(first, "gather_first"))
    ws.gathered(FFN1, ffn1)
    mod = jnp.concatenate([lax.dynamic_index_in_dim(mod_all[4 * (kk >> 1) + 2 * (kk & 1)], me, axis=0, keepdims=False)
                           for kk in range(4)])

    def pack(loss, dmod, dgains, g_w_pool, g_pool_scale):
        small_g = dict(b_ada=dmod, g_norm_ffn1=dgains[0], g_norm_mix=dgains[1], g_norm_ffn2=dgains[2],
                       g_final=dgains[3], pool_scale=g_pool_scale, w_pool=g_w_pool)
        return jnp.concatenate([_pack_small(small_g), jnp.pad(loss.reshape(1, 1), ((0, 0), (0, 127)))], axis=1)

    _, dx, _, _, _, _, _, gathered = _example_step(
        x[0], loss_target[0], positions[0], mod, (g_norm_ffn1[0], g_norm_mix[0], g_norm_ffn2[0], g_final),
        w_pool[0], pool_scale[0], ws, pack)

    per_weight, loss_tile = _adam_small(*[[t[n].reshape(1, -1) for n in SMALL] for t in (w, mom, var)], gathered)
    small_out = [{n: per_weight[j][kind].reshape(w[n].shape) for j, n in enumerate(SMALL)} for kind in range(4)]
    loss = loss_tile[0, 0]

    dmod_all = gathered[:, 0, :NMOD * D]
    dmod_cols = lax.dynamic_slice_in_dim(dmod_all, chip * nada, nada, axis=1)
    g_ada = _ada_bwd(c_all, dmod_cols)

    sums = {n: _sum4(ws.recv[n], "sum_" + n) for n in BIG}
    ada_out, swapped = _adam(w_ada[0], m_w_ada[0], v_w_ada[0], [g_ada], "adam_w_ada", _SwapPlan([sums[n] for n in BIG]))
    other = dict(zip(BIG, swapped))
    big_out = {}
    for n in BIG:
        if sums[n].shape[0] < w[n].shape[1]:
            big_out[n] = _adam_halves(w[n][0], mom[n][0], var[n][0], sums[n], other[n], "adam_" + n)
        else:
            big_out[n] = _adam(w[n][0], mom[n][0], var[n][0], [sums[n], other[n]], "adam_" + n)[0]

    def leaf(kind, n):
        if n == "w_ada":
            return ada_out[kind][None]
        if n in big_out:
            return big_out[n][kind][None]
        return small_out[kind][n]

    return (loss, dx[None], *[leaf(kind, n) for kind in range(4) for n in WEIGHTS])
```

```python
import jax
import jax.numpy as jnp
from jax import lax
from jax.experimental import pallas as pl
from jax.experimental.pallas import tpu as pltpu

F32 = jnp.float32
BF16 = jnp.bfloat16

D = 1024
FF = 2816
FC = 1408
PW = 256
GA = 256
HD = 64
LANES = 128
NH = GA // HD
NG = 3
DIL = (1, 4, 16)
BLK = 128
GW = 2 * D
INW = PW + 3 * NG * GA + GW
NMOD = 9
POOL_WINDOWS = (2, 4, 8, 16)
HALO = 16
EPS = 1e-6
SCALE = HD ** -0.5
NEG = -1e30

LR, B1, B2, AEPS, WD, STEP = 0.001, 0.9, 0.999, 1e-08, 0.01, 10

VMEM_BIG = 56 * 1024 * 1024
TM = 256

MESH = pl.DeviceIdType.MESH
ANY = pl.BlockSpec(memory_space=pl.ANY)


def _call(body, name, grid, in_specs, out_specs, out_shape, scratch=(), vmem=None, comm=None):
    params = pltpu.CompilerParams(dimension_semantics=("arbitrary",) * len(grid), vmem_limit_bytes=vmem)
    n_in, n_out, n_scr = len(in_specs), len(out_shape), len(scratch)
    if comm is None:
        call = pl.pallas_call(body, name=name, grid=grid, in_specs=list(in_specs), out_specs=list(out_specs),
                              out_shape=list(out_shape), scratch_shapes=list(scratch), compiler_params=params)
        return lambda *args: (call(*args), ())
    nc = len(comm.inputs)

    def body_with_comm(*refs):
        ins, refs = refs[:n_in], refs[n_in:]
        c_ins, refs = refs[:nc], refs[nc:]
        outs, refs = refs[:n_out], refs[n_out:]
        c_outs, refs = refs[:nc], refs[nc:]
        scr, sems = refs[:n_scr], refs[n_scr:]
        first = pl.program_id(0) == 0
        last = pl.program_id(0) == grid[0] - 1
        for ax in range(1, len(grid)):
            first = jnp.logical_and(first, pl.program_id(ax) == 0)
            last = jnp.logical_and(last, pl.program_id(ax) == grid[ax] - 1)

        @pl.when(first)
        def _():
            comm.start(c_ins, c_outs, sems)

        body(*ins, *outs, *scr)

        @pl.when(last)
        def _():
            comm.wait(c_ins, c_outs, sems)

    call = pl.pallas_call(
        body_with_comm, name=name, grid=grid, in_specs=list(in_specs) + [ANY] * nc,
        out_specs=list(out_specs) + [ANY] * nc, out_shape=list(out_shape) + list(comm.out_shapes),
        scratch_shapes=list(scratch) + list(comm.sem_shapes), compiler_params=params)

    def run(*args):
        res = call(*args, *comm.inputs)
        return res[:n_out], res[n_out:]

    return run


def _rows(tm, n):
    return pl.BlockSpec((tm, n), lambda i: (i, 0))


def _const(shape):
    return pl.BlockSpec(shape, lambda i: (0,) * len(shape))


def _sds(shape, dtype):
    return jax.ShapeDtypeStruct(shape, dtype)


def _dot(a, b):
    return jnp.dot(a, b, preferred_element_type=F32)


def _dot_nt(a, b):
    return lax.dot_general(a, b, (((1,), (1,)), ((), ())), preferred_element_type=F32)


def _dot_tn(a, b):
    return lax.dot_general(a, b, (((0,), (0,)), ((), ())), preferred_element_type=F32)


def _colsum(v):
    return jnp.sum(v, axis=0, keepdims=True)


def _norm_fwd(h, g, sh, sc):
    r = lax.rsqrt(jnp.mean(h * h, axis=-1, keepdims=True) + EPS)
    xh = h * r
    n = xh * g
    return xh, r, n, n * (1.0 + sc) + sh


def _norm_bwd(du, xh, r, n, g, sc):
    dn = du * (1.0 + sc)
    dxh = dn * g
    dh = r * (dxh - xh * jnp.mean(dxh * xh, axis=-1, keepdims=True))
    return dh, _colsum(du), _colsum(du * n), _colsum(dn * xh)


def _load_once(pairs, sems):
    @pl.when(pl.program_id(0) == 0)
    def _():
        cps = [pltpu.make_async_copy(src, dst, sems.at[j]) for j, (src, dst) in enumerate(pairs)]
        for cp in cps:
            cp.start()
        for cp in cps:
            cp.wait()


def _zero_first(ref):
    @pl.when(pl.program_id(0) == 0)
    def _():
        ref[...] = jnp.zeros(ref.shape, ref.dtype)


def _row_chunks(hbm_refs, vmem_ref):
    pairs, row = [], 0
    for ref in hbm_refs:
        pairs.append((ref, vmem_ref.at[pl.ds(row, ref.shape[0]), :]))
        row += ref.shape[0]
    return pairs


def _loss_head(hh, tgt, g):
    r = lax.rsqrt(jnp.mean(hh * hh, axis=-1, keepdims=True) + EPS)
    xh = hh * r
    err = xh * g - tgt
    dy = err * (1.0 / D)
    dxh = dy * g
    dh = r * (dxh - xh * jnp.mean(dxh * xh, axis=-1, keepdims=True))
    return dh, _colsum(err * err), _colsum(dy * xh)


def _ffn_fwd(h, vec, wins, wout, name, comm=None, head=None):
    T = h.shape[0]
    nwin = len(wins)
    nhead = 0 if head is None else 2

    def body(h_ref, vec_ref, *rest):
        head_refs, rest = rest[:nhead], rest[nhead:]
        win_hbms, rest = rest[:nwin], rest[nwin:]
        (wout_hbm, ho_ref, u_ref, a_ref, b_ref, f_ref), rest = rest[:6], rest[6:]
        lacc_refs, (win_v, wout_v, sems) = rest[:nhead // 2], rest[nhead // 2:]
        _load_once(_row_chunks(win_hbms, win_v) + [(wout_hbm, wout_v)], sems)
        hh = h_ref[...]
        g, sh, sc, gt = vec_ref[0:1, :], vec_ref[1:2, :], vec_ref[2:3, :], vec_ref[3:4, :]
        _, _, _, u = _norm_fwd(hh, g, sh, sc)
        ub = u.astype(BF16)
        u_ref[...] = ub
        acc = None
        for j in range(FF // FC):
            lo, hi = j * FC, (j + 1) * FC
            a = _dot(ub, win_v[:, lo:hi])
            b = _dot(ub, win_v[:, FF + lo:FF + hi])
            a_ref[:, lo:hi] = a.astype(BF16)
            b_ref[:, lo:hi] = b.astype(BF16)
            s = (a * jax.nn.sigmoid(a) * b).astype(BF16)
            part = _dot(s, wout_v[lo:hi, :])
            acc = part if acc is None else acc + part
        f_ref[...] = acc.astype(BF16)
        ho = hh + 0.5 * gt * acc
        if head is None:
            ho_ref[...] = ho
        else:
            _zero_first(lacc_refs[0])
            dh, sq, dg = _loss_head(ho, head_refs[0][...], head_refs[1][0:1, :])
            ho_ref[...] = dh
            lacc_refs[0][0:1, :] += sq
            lacc_refs[0][1:2, :] += dg

    head_specs = [] if head is None else [_rows(TM, D), _const((8, D))]
    lacc_spec = [] if head is None else [_const((8, D))]
    lacc_shape = [] if head is None else [_sds((8, D), F32)]
    return _call(
        body, name, (T // TM,),
        [_rows(TM, D), _const((8, D))] + head_specs + [ANY] * (nwin + 1),
        [_rows(TM, D), _rows(TM, D), _rows(TM, FF), _rows(TM, FF), _rows(TM, D)] + lacc_spec,
        [_sds((T, D), F32), _sds((T, D), BF16), _sds((T, FF), BF16), _sds((T, FF), BF16), _sds((T, D), BF16)] + lacc_shape,
        scratch=[pltpu.VMEM((D, 2 * FF), BF16), pltpu.VMEM((FF, D), BF16), pltpu.SemaphoreType.DMA((nwin + 1,))],
        vmem=VMEM_BIG, comm=comm,
    )(h, vec, *([] if head is None else head), *wins, wout)


def _ffn_ab(h, vec, wins, name, comm=None):
    T = h.shape[0]
    nwin = len(wins)

    def body(h_ref, vec_ref, *rest):
        win_hbms, (u_ref, a_ref, b_ref, win_v, sems) = rest[:nwin], rest[nwin:]
        _load_once(_row_chunks(win_hbms, win_v), sems)
        g, sh, sc = vec_ref[0:1, :], vec_ref[1:2, :], vec_ref[2:3, :]
        _, _, _, u = _norm_fwd(h_ref[...], g, sh, sc)
        ub = u.astype(BF16)
        u_ref[...] = ub
        for j in range(FF // FC):
            lo, hi = j * FC, (j + 1) * FC
            a_ref[:, lo:hi] = _dot(ub, win_v[:, lo:hi]).astype(BF16)
            b_ref[:, lo:hi] = _dot(ub, win_v[:, FF + lo:FF + hi]).astype(BF16)

    return _call(
        body, name, (T // TM,),
        [_rows(TM, D), _const((8, D))] + [ANY] * nwin,
        [_rows(TM, D), _rows(TM, FF), _rows(TM, FF)],
        [_sds((T, D), BF16), _sds((T, FF), BF16), _sds((T, FF), BF16)],
        scratch=[pltpu.VMEM((D, 2 * FF), BF16), pltpu.SemaphoreType.DMA((nwin,))],
        vmem=VMEM_BIG, comm=comm,
    )(h, vec, *wins)


def _ffn_out(h, a, b, vec, wout, name, comm=None):
    T = h.shape[0]

    def body(h_ref, a_ref, b_ref, vec_ref, wout_hbm, ho_ref, f_ref, wout_v, sems):
        _load_once([(wout_hbm, wout_v)], sems)
        gt = vec_ref[3:4, :]
        acc = None
        for j in range(FF // FC):
            lo, hi = j * FC, (j + 1) * FC
            av = a_ref[:, lo:hi].astype(F32)
            s = (av * jax.nn.sigmoid(av) * b_ref[:, lo:hi].astype(F32)).astype(BF16)
            part = _dot(s, wout_v[lo:hi, :])
            acc = part if acc is None else acc + part
        f_ref[...] = acc.astype(BF16)
        ho_ref[...] = h_ref[...] + 0.5 * gt * acc

    return _call(
        body, name, (T // TM,),
        [_rows(TM, D), _rows(TM, FF), _rows(TM, FF), _const((8, D)), ANY],
        [_rows(TM, D), _rows(TM, D)],
        [_sds((T, D), F32), _sds((T, D), BF16)],
        scratch=[pltpu.VMEM((FF, D), BF16), pltpu.SemaphoreType.DMA((1,))],
        vmem=VMEM_BIG, comm=comm,
    )(h, a, b, vec, wout)


def _ffn_bwd(dh, h, a, b, f, vec, wins, wout, name, comm=None):
    T = h.shape[0]
    nwin = len(wins)

    def body(dh_ref, h_ref, a_ref, b_ref, f_ref, vec_ref, *rest):
        win_hbms, (wout_hbm, dhi_ref, dab_ref, s_ref, df_ref, acc_ref, win_v, wout_v, sems) = rest[:nwin], rest[nwin:]
        _load_once(_row_chunks(win_hbms, win_v) + [(wout_hbm, wout_v)], sems)
        _zero_first(acc_ref)
        g, sh, sc, gt = vec_ref[0:1, :], vec_ref[1:2, :], vec_ref[2:3, :], vec_ref[3:4, :]
        dho = dh_ref[...]
        df = (0.5 * gt * dho).astype(BF16)
        df_ref[...] = df
        dgt = _colsum(0.5 * dho * f_ref[...].astype(F32))
        du = None
        for j in range(FF // FC):
            lo, hi = j * FC, (j + 1) * FC
            av = a_ref[:, lo:hi].astype(F32)
            bv = b_ref[:, lo:hi].astype(F32)
            ds = _dot_nt(df, wout_v[lo:hi, :])
            sig = jax.nn.sigmoid(av)
            sa = av * sig
            s_ref[:, lo:hi] = (sa * bv).astype(BF16)
            da = (ds * bv * (sig * (1.0 + av * (1.0 - sig)))).astype(BF16)
            db = (ds * sa).astype(BF16)
            dab_ref[:, lo:hi] = da
            dab_ref[:, FF + lo:FF + hi] = db
            part = _dot_nt(da, win_v[:, lo:hi]) + _dot_nt(db, win_v[:, FF + lo:FF + hi])
            du = part if du is None else du + part
        xh, r, n, _ = _norm_fwd(h_ref[...], g, sh, sc)
        dhn, dsh, dsc, dg = _norm_bwd(du, xh, r, n, g, sc)
        dhi_ref[...] = dho + dhn
        acc_ref[0:1, :] += dsh
        acc_ref[1:2, :] += dsc
        acc_ref[2:3, :] += dg
        acc_ref[3:4, :] += dgt

    return _call(
        body, name, (T // TM,),
        [_rows(TM, D), _rows(TM, D), _rows(TM, FF), _rows(TM, FF), _rows(TM, D), _const((8, D))] + [ANY] * (nwin + 1),
        [_rows(TM, D), _rows(TM, 2 * FF), _rows(TM, FF), _rows(TM, D), _const((8, D))],
        [_sds((T, D), F32), _sds((T, 2 * FF), BF16), _sds((T, FF), BF16), _sds((T, D), BF16), _sds((8, D), F32)],
        scratch=[pltpu.VMEM((D, 2 * FF), BF16), pltpu.VMEM((FF, D), BF16), pltpu.SemaphoreType.DMA((nwin + 1,))],
        vmem=VMEM_BIG, comm=comm,
    )(dh, h, a, b, f, vec, *wins, wout)


def _wgrad(x, y, name, tk, tn, tt, out_dtype=BF16, comm=None):
    T, K = x.shape
    N = y.shape[1]
    nt = T // tt

    def body(x_ref, y_ref, o_ref, acc_ref):
        t = pl.program_id(2)
        part = _dot_tn(x_ref[...], y_ref[...])

        @pl.when(t == 0)
        def _():
            acc_ref[...] = part

        @pl.when(t > 0)
        def _():
            acc_ref[...] += part

        @pl.when(t == nt - 1)
        def _():
            o_ref[...] = acc_ref[...].astype(out_dtype)

    (out,), c_outs = _call(
        body, name, (K // tk, N // tn, nt),
        [pl.BlockSpec((tt, tk), lambda i, j, t: (t, i)), pl.BlockSpec((tt, tn), lambda i, j, t: (t, j))],
        [pl.BlockSpec((tk, tn), lambda i, j, t: (i, j))], [_sds((K, N), out_dtype)],
        scratch=[pltpu.VMEM((tk, tn), F32)], vmem=VMEM_BIG, comm=comm,
    )(x, y)
    return out, c_outs


def _wgrad_scatter(x, y, name, tt, comm=None):
    T, K = x.shape
    n = y.shape[1] // 4
    nt = T // tt
    assert nt >= 2, "a block's hand-over is added one grid step into the next block"
    half = K // 2
    nc = 0 if comm is None else len(comm.inputs)

    def body(chip_ref, x_ref, y_ref, *refs):
        c_ins, refs = refs[:nc], refs[nc:]
        recv_ref, refs = refs[0], refs[1:]
        c_outs, refs = refs[:nc], refs[nc:]
        acc_ref, keep_ref, give_ref, take_ref, local_sem, give_sems, take_sems, send_sems, recv_sems = refs[:9]
        j, t = pl.program_id(0), pl.program_id(1)
        px, py, pc = _place()

        def hand_over(jj):
            return pltpu.make_async_remote_copy(
                src_ref=give_ref.at[jj], dst_ref=take_ref.at[jj], send_sem=give_sems.at[jj], recv_sem=take_sems.at[jj],
                device_id=(px, py, 1 - pc), device_id_type=MESH)

        def send(jj):
            m = jj + 1
            return pltpu.make_async_remote_copy(
                src_ref=keep_ref.at[jj], dst_ref=recv_ref.at[m], send_sem=send_sems.at[jj], recv_sem=recv_sems.at[jj],
                device_id=_chip_peer(px, py, pc, m), device_id_type=MESH)

        def add_sibling(jj):
            hand_over(jj).wait_recv()
            keep_ref[jj] = (keep_ref[jj].astype(F32) + take_ref[jj].astype(F32)).astype(BF16)

        if comm is not None:
            @pl.when(jnp.logical_and(j == 0, t == 0))
            def _():
                comm.start(c_ins, c_outs, refs[9:])

        part = _dot_tn(x_ref[...], y_ref[...])

        @pl.when(t == 0)
        def _():
            acc_ref[...] = part

        @pl.when(t > 0)
        def _():
            acc_ref[...] += part

        for jj in range(3):
            @pl.when(jnp.logical_and(j == jj + 1, t == 0))
            def _():
                add_sibling(jj)
                send(jj).start()

        for jj in range(4):
            @pl.when(jnp.logical_and(j == jj, t == nt - 1))
            def _():
                keep_ref[jj] = acc_ref[pl.ds(pl.multiple_of(pc * half, 16), half), :].astype(BF16)
                give_ref[jj] = acc_ref[pl.ds(pl.multiple_of((1 - pc) * half, 16), half), :].astype(BF16)
                hand_over(jj).start()

        @pl.when(jnp.logical_and(j == 3, t == nt - 1))
        def _():
            add_sibling(3)
            own = pltpu.make_async_copy(keep_ref.at[3], recv_ref.at[0], local_sem.at[0])
            own.start()
            for jj in range(3):
                send(jj).wait_recv()
            for jj in range(3):
                send(jj).wait_send()
            for jj in range(4):
                hand_over(jj).wait_send()
            own.wait()
            if comm is not None:
                comm.wait(c_ins, c_outs, refs[9:])

    grid_spec = pltpu.PrefetchScalarGridSpec(
        num_scalar_prefetch=1, grid=(4, nt),
        in_specs=[pl.BlockSpec((tt, K), lambda j, t, chip: (t, 0)),
                  pl.BlockSpec((tt, n), lambda j, t, chip: (t, chip[0] ^ ((j + 1) & 3)))] + [ANY] * nc,
        out_specs=[ANY] * (1 + nc),
        scratch_shapes=[pltpu.VMEM((K, n), F32)] + [pltpu.VMEM((4, half, n), BF16)] * 3
        + [pltpu.SemaphoreType.DMA((1,))] + [pltpu.SemaphoreType.DMA((4,))] * 2 + [pltpu.SemaphoreType.DMA((3,))] * 2
        + ([] if comm is None else list(comm.sem_shapes)))
    px, py, _ = _place()
    res = pl.pallas_call(
        body, name=name, grid_spec=grid_spec,
        out_shape=[_sds((4, half, n), BF16)] + ([] if comm is None else list(comm.out_shapes)),
        compiler_params=pltpu.CompilerParams(dimension_semantics=("arbitrary", "arbitrary"), vmem_limit_bytes=VMEM_BIG),
    )((2 * px + py).astype(jnp.int32).reshape(1), x, y, *([] if comm is None else comm.inputs))
    return res[0], res[1:]


def _swap_halves(t):
    w = t.shape[1]
    lane = lax.broadcasted_iota(jnp.int32, t.shape, 1)
    return jnp.where(lane % HD < HD // 2, pltpu.roll(t, w - HD // 2, 1), pltpu.roll(t, HD // 2, 1))


def _rope(t, cos, sin_signed):
    c = jnp.tile(cos, (1, t.shape[1] // cos.shape[1]))
    s = jnp.tile(sin_signed, (1, t.shape[1] // sin_signed.shape[1]))
    return t * c + _swap_halves(t) * s


def _rope_bwd(dt, cos, sin_signed):
    c = jnp.tile(cos, (1, dt.shape[1] // cos.shape[1]))
    s = jnp.tile(sin_signed, (1, dt.shape[1] // sin_signed.shape[1]))
    return dt * c + _swap_halves(dt * s)


def _rm_spec(dil):
    return pl.BlockSpec((dil, TM // dil, GA), lambda i: (0, i, 0))


def _to_residues(t, dst_ref, scr_ref, dil):
    if dil == 1:
        dst_ref[0] = t.astype(dst_ref.dtype)
        return
    for j in range(GA // LANES):
        scr_ref[j] = t[:, j * LANES:(j + 1) * LANES]
    for r in range(dil):
        for j in range(GA // LANES):
            rows = scr_ref.at[j][pl.ds(r, TM // dil, stride=dil), :]
            dst_ref[r, :, j * LANES:(j + 1) * LANES] = rows.astype(dst_ref.dtype)


def _from_residues(src_ref, scr_ref, dil):
    if dil == 1:
        return src_ref[0].astype(F32)
    for r in range(dil):
        for j in range(GA // LANES):
            scr_ref.at[j][pl.ds(r, TM // dil, stride=dil), :] = src_ref[r, :, j * LANES:(j + 1) * LANES].astype(F32)
    return jnp.concatenate([scr_ref[j] for j in range(GA // LANES)], axis=1)


def _mix_proj(h, vec, win, cos, sin, comm=None):
    T = h.shape[0]

    def body(h_ref, vec_ref, win_hbm, cos_ref, sin_ref, u_ref, p_ref, *rest):
        qkv_refs, gates_ref, win_v, scr_ref, sems = rest[:3 * NG], rest[3 * NG], rest[3 * NG + 1], rest[3 * NG + 2], rest[3 * NG + 3]
        _load_once([(win_hbm, win_v)], sems)
        g, sh, sc = vec_ref[0:1, :], vec_ref[1:2, :], vec_ref[2:3, :]
        _, _, _, u = _norm_fwd(h_ref[...], g, sh, sc)
        ub = u.astype(BF16)
        u_ref[...] = ub
        p_ref[...] = _dot(ub, win_v[:, 0:PW])
        cos_t, sin_t = cos_ref[...], sin_ref[...]
        for j in range(3 * NG):
            col = PW + j * GA
            t = _dot(ub, win_v[:, col:col + GA])
            if j < 2 * NG:
                t = _rope(t, cos_t, sin_t)
            _to_residues(t, qkv_refs[j], scr_ref, DIL[j % NG])
        for j in range(GW // 512):
            col = PW + 3 * NG * GA + j * 512
            gates_ref[:, j * 512:(j + 1) * 512] = jax.nn.sigmoid(_dot(ub, win_v[:, col:col + 512])).astype(BF16)

    outs, c_outs = _call(
        body, "mix_proj", (T // TM,),
        [_rows(TM, D), _const((8, D)), ANY, _rows(TM, 128), _rows(TM, 128)],
        [_rows(TM, D), _rows(TM, PW)] + [_rm_spec(d) for d in DIL] * 3 + [_rows(TM, GW)],
        [_sds((T, D), BF16), _sds((T, PW), F32)] + [_sds((d, T // d, GA), BF16) for d in DIL] * 3 + [_sds((T, GW), BF16)],
        scratch=[pltpu.VMEM((D, INW), BF16), pltpu.VMEM((GA // LANES, TM, LANES), F32), pltpu.SemaphoreType.DMA((1,))],
        vmem=VMEM_BIG, comm=comm,
    )(h, vec, win, cos, sin)
    return (outs[0], outs[1], outs[2:2 + NG], outs[2 + NG:2 + 2 * NG], outs[2 + 2 * NG:2 + 3 * NG], outs[2 + 3 * NG]), c_outs


def _head_masks():
    lane_head = lax.broadcasted_iota(jnp.int32, (BLK, GA), 1) // HD
    return [lane_head == hd for hd in range(NH)]


def _expand_heads(t, hm):
    return jnp.concatenate([jnp.where(m, t, jnp.zeros_like(t)) for m in hm], axis=0)


def _collapse_heads(tb, hm):
    out = None
    for hd, m in enumerate(hm):
        part = jnp.where(m, tb[hd * BLK:(hd + 1) * BLK, :], 0.0)
        out = part if out is None else out + part
    return out


def _head_rows(t):
    return jnp.concatenate([t[:, hd * HD:hd * HD + 1] for hd in range(NH)], axis=0)


def _band(has_prev):
    a = lax.broadcasted_iota(jnp.int32, (NH * BLK, 2 * BLK), 0) & (BLK - 1)
    c = lax.broadcasted_iota(jnp.int32, (NH * BLK, 2 * BLK), 1)
    return jnp.logical_and(c >= jnp.where(has_prev, a, BLK), c <= a + BLK)


def _attn_specs(nbt):
    cur = pl.BlockSpec((2 * BLK, GA), lambda i: (i, 0))
    prev = pl.BlockSpec((BLK, GA), lambda i: (jnp.maximum(2 * i - 1, 0), 0))
    nxt = pl.BlockSpec((BLK, GA), lambda i: (jnp.minimum(2 * i + 2, nbt - 1), 0))
    return cur, prev, nxt


def _attn_fwd(q, k, v, nb, name, comm=None):
    T = q.shape[0]
    nbt = T // BLK
    lo, hi = slice(0, BLK), slice(BLK, 2 * BLK)

    def block(qv, kcat, vcat, has_prev, hm):
        s = jnp.where(_band(has_prev), _dot_nt(_expand_heads(qv, hm), kcat) * SCALE, NEG)
        mx = jnp.max(s, axis=-1, keepdims=True)
        e = jnp.exp(s - mx)
        l = jnp.sum(e, axis=-1, keepdims=True)
        ob = _dot((e * (1.0 / l)).astype(BF16), vcat)
        return _collapse_heads(ob, hm), _collapse_heads(jnp.broadcast_to(mx + jnp.log(l), (NH * BLK, GA)), hm)

    def body(q_ref, k_ref, kp_ref, v_ref, vp_ref, o_ref, lse_ref):
        b0 = 2 * pl.program_id(0)
        hm = _head_masks()
        k_first = jnp.concatenate([kp_ref[...], k_ref[lo, :]], axis=0)
        v_first = jnp.concatenate([vp_ref[...], v_ref[lo, :]], axis=0)
        o_ref[lo, :], lse_ref[lo, :] = block(q_ref[lo, :], k_first, v_first, (b0 & (nb - 1)) != 0, hm)
        o_ref[hi, :], lse_ref[hi, :] = block(q_ref[hi, :], k_ref[...], v_ref[...], ((b0 + 1) & (nb - 1)) != 0, hm)

    cur, prev, _ = _attn_specs(nbt)
    return _call(body, name, (nbt // 2,), [cur, cur, prev, cur, prev], [cur, cur],
                 [_sds((T, GA), F32), _sds((T, GA), F32)], comm=comm)(q, k, k, v, v)


def _attn_bwd(q, k, v, do, lse, e, nb, name, comm=None):
    T = q.shape[0]
    nbt = T // BLK

    lo, hi = slice(0, BLK), slice(BLK, 2 * BLK)

    def probs_and_ds(qb, dob, kcat, vcat, lsev, ev, valid):
        p = jnp.where(valid, jnp.exp(_dot_nt(qb, kcat) * SCALE - _head_rows(lsev)), 0.0)
        return p, (p * (_dot_nt(dob, vcat) + _head_rows(ev))).astype(BF16)

    def body(q_ref, k_ref, v_ref, do_ref, lse_ref, e_ref, kp_ref, vp_ref, qn_ref, don_ref, lsen_ref, en_ref,
             dq_ref, dk_ref, dv_ref):
        b0 = 2 * pl.program_id(0)
        hm = _head_masks()
        q1, q2, q3 = _expand_heads(q_ref[lo, :], hm), _expand_heads(q_ref[hi, :], hm), _expand_heads(qn_ref[...], hm)
        do1, do2, do3 = (_expand_heads(do_ref[lo, :], hm), _expand_heads(do_ref[hi, :], hm),
                         _expand_heads(don_ref[...], hm))
        k1 = jnp.concatenate([kp_ref[...], k_ref[lo, :]], axis=0)
        v1 = jnp.concatenate([vp_ref[...], v_ref[lo, :]], axis=0)
        k2, v2 = k_ref[...], v_ref[...]
        p1, ds1 = probs_and_ds(q1, do1, k1, v1, lse_ref[lo, :], e_ref[lo, :], _band((b0 & (nb - 1)) != 0))
        p2, ds2 = probs_and_ds(q2, do2, k2, v2, lse_ref[hi, :], e_ref[hi, :], _band(((b0 + 1) & (nb - 1)) != 0))
        dq_ref[lo, :] = _collapse_heads(_dot(ds1, k1) * SCALE, hm)
        dq_ref[hi, :] = _collapse_heads(_dot(ds2, k2) * SCALE, hm)
        a = lax.broadcasted_iota(jnp.int32, (NH * BLK, BLK), 0) & (BLK - 1)
        c = lax.broadcasted_iota(jnp.int32, (NH * BLK, BLK), 1)
        valid3 = jnp.logical_and(c >= a, ((b0 + 2) & (nb - 1)) != 0)
        p3, ds3 = probs_and_ds(q3, do3, k_ref[hi, :], v_ref[hi, :], lsen_ref[...], en_ref[...], valid3)
        q12, q23 = jnp.concatenate([q1, q2], axis=0), jnp.concatenate([q2, q3], axis=0)
        do12, do23 = jnp.concatenate([do1, do2], axis=0), jnp.concatenate([do2, do3], axis=0)
        dk_ref[lo, :] = _dot_tn(jnp.concatenate([ds1[:, BLK:], ds2[:, :BLK]], axis=0), q12) * SCALE
        dk_ref[hi, :] = _dot_tn(jnp.concatenate([ds2[:, BLK:], ds3], axis=0), q23) * SCALE
        pb1, pb2, pb3 = p1.astype(BF16), p2.astype(BF16), p3.astype(BF16)
        dv_ref[lo, :] = _dot_tn(jnp.concatenate([pb1[:, BLK:], pb2[:, :BLK]], axis=0), do12).astype(BF16)
        dv_ref[hi, :] = _dot_tn(jnp.concatenate([pb2[:, BLK:], pb3], axis=0), do23).astype(BF16)

    cur, prev, nxt = _attn_specs(nbt)
    return _call(body, name, (nbt // 2,), [cur] * 6 + [prev, prev] + [nxt] * 4, [cur, cur, cur],
                 [_sds((T, GA), F32), _sds((T, GA), F32), _sds((T, GA), BF16)],
                 comm=comm)(q, k, v, do, lse, e, k, v, q, do, lse, e)


def _flat(t):
    return t.reshape(t.shape[0] * t.shape[1], t.shape[2])


def _by_residue(t, dil):
    return t.reshape(dil, t.shape[0] // dil, t.shape[1])


def _pool_consts(shape, row0):
    lane = lax.broadcasted_iota(jnp.int32, shape, 1)
    t = lax.broadcasted_iota(jnp.int32, shape, 0) + row0
    grp = lane // (PW // len(POOL_WINDOWS))
    win = jnp.where(grp == 0, POOL_WINDOWS[0], jnp.where(grp == 1, POOL_WINDOWS[1],
                    jnp.where(grp == 2, POOL_WINDOWS[2], POOL_WINDOWS[3])))
    cnt = jnp.minimum(t + 1, win).astype(F32)
    return grp, cnt


def _window_sums(ext_ref, base, step, tm):
    outs, run = [], None
    for j in range(POOL_WINDOWS[-1]):
        sl = ext_ref[pl.ds(base + step * j, tm), :]
        run = sl if run is None else run + sl
        if j + 1 in POOL_WINDOWS:
            outs.append(run)
    return outs


def _select_group(grp, vals):
    return jnp.where(grp == 0, vals[0], jnp.where(grp == 1, vals[1], jnp.where(grp == 2, vals[2], vals[3])))


def _pool_d(pc_ref, pp_ref, ext_ref, i, tm):
    ext_ref[0:HALO, :] = jnp.where(i > 0, pp_ref[tm - HALO:tm, :], 0.0)
    ext_ref[HALO:HALO + tm, :] = pc_ref[...]
    grp, cnt = _pool_consts((tm, PW), i * tm)
    sums = _window_sums(ext_ref, HALO, -1, tm)
    return _select_group(grp, sums) / cnt - pc_ref[...]


def _group_weights(ls):
    mx = jnp.maximum(jnp.maximum(ls[0], ls[1]), ls[2])
    es = [jnp.exp(l - mx) for l in ls]
    inv = 1.0 / (es[0] + es[1] + es[2])
    return [e * inv for e in es]


def _mix_merge(h, vec, p, os, lses, gates, wp_bd, pscale, wpb, wab, wout):
    T = h.shape[0]

    def body(h_ref, vec_ref, pc_ref, pp_ref, o0, o1, o2, l0, l1, l2, gates_ref, wp_ref, ps_ref, wpb_ref, wab_ref, wout_ref,
             ho_ref, yp_ref, ya_ref, mg_ref, mo_ref, d_ref, ext_ref, scr_ref):
        i = pl.program_id(0)
        gt = vec_ref[3:4, :]
        d = _pool_d(pc_ref, pp_ref, ext_ref, i, TM).astype(BF16)
        d_ref[...] = d
        ypool = (_dot(d, wp_ref[...]) * ps_ref[0:1, :]).astype(BF16)
        yp_ref[...] = ypool
        w = _group_weights([_from_residues(r, scr_ref, dl) for r, dl in zip((l0, l1, l2), DIL)])
        yattn = None
        for wg, o_ref, dl in zip(w, (o0, o1, o2), DIL):
            part = wg * _from_residues(o_ref, scr_ref, dl)
            yattn = part if yattn is None else yattn + part
        yattn = yattn.astype(BF16)
        ya_ref[...] = yattn
        merged = (gates_ref[:, 0:D].astype(F32) * _dot(ypool, wpb_ref[...])
                  + gates_ref[:, D:GW].astype(F32) * _dot(yattn, wab_ref[...])).astype(BF16)
        mg_ref[...] = merged
        mo = _dot(merged, wout_ref[...])
        mo_ref[...] = mo.astype(BF16)
        ho_ref[...] = h_ref[...] + gt * mo

    prev = pl.BlockSpec((TM, PW), lambda i: (jnp.maximum(i - 1, 0), 0))
    return _call(
        body, "mix_merge", (T // TM,),
        [_rows(TM, D), _const((8, D)), _rows(TM, PW), prev] + [_rm_spec(dl) for dl in DIL] * 2 + [_rows(TM, GW), _const((PW, PW)),
         _const((8, PW)), _const((PW, D)), _const((GA, D)), _const((D, D))],
        [_rows(TM, D), _rows(TM, PW), _rows(TM, GA), _rows(TM, D), _rows(TM, D), _rows(TM, PW)],
        [_sds((T, D), F32), _sds((T, PW), BF16), _sds((T, GA), BF16), _sds((T, D), BF16), _sds((T, D), BF16), _sds((T, PW), BF16)],
        scratch=[pltpu.VMEM((TM + HALO, PW), F32), pltpu.VMEM((GA // LANES, TM, LANES), F32)],
        vmem=VMEM_BIG,
    )(h, vec, p, p, *os, *lses, gates, wp_bd, pscale, wpb, wab, wout)[0]


def _mix_bwd_a(dh, vec, mixout, gates, ypool, yattn, dpool, os, lses, wp_bd, pscale, wpb, wab, wout, ones_bd, comm=None):
    T = dh.shape[0]

    def body(dh_ref, vec_ref, mo_ref, gates_ref, yp_ref, ya_ref, d_ref, o0, o1, o2, l0, l1, l2,
             wp_ref, ps_ref, wpb_ref, wab_ref, wout_ref, ones_ref,
             dmo_ref, dp_ref, da_ref, dgates_ref, do0, do1, do2, e0, e1, e2, dd_ref, dyp_ref, acc_ref, acc2_ref, scr_ref):
        _zero_first(acc_ref)
        _zero_first(acc2_ref)
        gt = vec_ref[3:4, :]
        dho = dh_ref[...]
        acc_ref[3:4, :] += _colsum(dho * mo_ref[...].astype(F32))
        dmo = (gt * dho).astype(BF16)
        dmo_ref[...] = dmo
        dmerged = _dot_nt(dmo, wout_ref[...])
        gp = gates_ref[:, 0:D].astype(F32)
        ga = gates_ref[:, D:GW].astype(F32)
        bp = _dot(yp_ref[...], wpb_ref[...])
        ba = _dot(ya_ref[...], wab_ref[...])
        dgates_ref[:, 0:D] = (dmerged * bp * gp * (1.0 - gp)).astype(BF16)
        dgates_ref[:, D:GW] = (dmerged * ba * ga * (1.0 - ga)).astype(BF16)
        dbp = (dmerged * gp).astype(BF16)
        dba = (dmerged * ga).astype(BF16)
        dp_ref[...] = dbp
        da_ref[...] = dba
        dypool = _dot_nt(dbp, wpb_ref[...])
        ypre = _dot(d_ref[...], wp_ref[...])
        acc2_ref[0:1, :] += _colsum(dypool * ypre)
        dyp = (dypool * ps_ref[0:1, :]).astype(BF16)
        dyp_ref[...] = dyp
        dd_ref[...] = _dot_nt(dyp, wp_ref[...])
        dya = _dot_nt(dba, wab_ref[...])
        w = _group_weights([_from_residues(r, scr_ref, dl) for r, dl in zip((l0, l1, l2), DIL)])
        ya = None
        for wg, o_ref, dl in zip(w, (o0, o1, o2), DIL):
            part = wg * _from_residues(o_ref, scr_ref, dl)
            ya = part if ya is None else ya + part
        prod = dya * ya
        hi = prod.astype(BF16)
        lo = (prod - hi.astype(F32)).astype(BF16)
        tot = _dot(hi, ones_ref[...]) + _dot(lo, ones_ref[...])
        for wg, do_ref, e_ref, dl in zip(w, (do0, do1, do2), (e0, e1, e2), DIL):
            _to_residues(wg * dya, do_ref, scr_ref, dl)
            _to_residues(-wg * tot, e_ref, scr_ref, dl)

    return _call(
        body, "mix_bwd_a", (T // TM,),
        [_rows(TM, D), _const((8, D)), _rows(TM, D), _rows(TM, GW), _rows(TM, PW), _rows(TM, GA), _rows(TM, PW)]
        + [_rm_spec(dl) for dl in DIL] * 2
        + [_const((PW, PW)), _const((8, PW)), _const((PW, D)), _const((GA, D)), _const((D, D)), _const((GA, GA))],
        [_rows(TM, D)] * 3 + [_rows(TM, GW)] + [_rm_spec(dl) for dl in DIL] * 2
        + [_rows(TM, PW), _rows(TM, PW), _const((8, D)), _const((8, PW))],
        [_sds((T, D), BF16)] * 3 + [_sds((T, GW), BF16)] + [_sds((dl, T // dl, GA), BF16) for dl in DIL]
        + [_sds((dl, T // dl, GA), F32) for dl in DIL]
        + [_sds((T, PW), F32), _sds((T, PW), BF16), _sds((8, D), F32), _sds((8, PW), F32)],
        scratch=[pltpu.VMEM((GA // LANES, TM, LANES), F32)],
        vmem=VMEM_BIG, comm=comm,
    )(dh, vec, mixout, gates, ypool, yattn, dpool, *os, *lses, wp_bd, pscale, wpb, wab, wout, ones_bd)


def _mix_bwd_b(dh, h, vec, dd, dqs, dks, dvs, dgates, cos, sin, win):
    T = h.shape[0]
    nt = T // TM

    def body(dh_ref, h_ref, vec_ref, ddc_ref, ddn_ref, *rest):
        qk_refs, dv_refs = rest[:2 * NG], rest[2 * NG:3 * NG]
        dgates_ref, cos_ref, sin_ref, win_hbm, dhi_ref, dproj_ref, acc_ref, win_v, ext_ref, scr_ref, sems = rest[3 * NG:]
        i = pl.program_id(0)
        _load_once([(win_hbm, win_v)], sems)
        _zero_first(acc_ref)
        g, sh, sc = vec_ref[0:1, :], vec_ref[1:2, :], vec_ref[2:3, :]
        grp, cnt = _pool_consts((TM, PW), i * TM)
        _, cnt_n = _pool_consts((HALO, PW), (i + 1) * TM)
        ext_ref[0:TM, :] = ddc_ref[...] / cnt
        ext_ref[TM:TM + HALO, :] = jnp.where(i < nt - 1, ddn_ref[0:HALO, :] / cnt_n, 0.0)
        dp = _select_group(grp, _window_sums(ext_ref, 0, 1, TM)) - ddc_ref[...]
        dproj_ref[:, 0:PW] = dp.astype(BF16)
        cos_t, sin_t = cos_ref[...], sin_ref[...]
        for j in range(2 * NG):
            col = PW + j * GA
            dt = _from_residues(qk_refs[j], scr_ref, DIL[j % NG])
            dproj_ref[:, col:col + GA] = _rope_bwd(dt, cos_t, sin_t).astype(BF16)
        for j in range(NG):
            col = PW + (2 * NG + j) * GA
            dproj_ref[:, col:col + GA] = _from_residues(dv_refs[j], scr_ref, DIL[j]).astype(BF16)
        dproj_ref[:, PW + 3 * NG * GA:INW] = dgates_ref[...]
        du = None
        for j in range(INW // 512):
            part = _dot_nt(dproj_ref[:, j * 512:(j + 1) * 512], win_v[:, j * 512:(j + 1) * 512])
            du = part if du is None else du + part
        xh, r, n, _ = _norm_fwd(h_ref[...], g, sh, sc)
        dhn, dsh, dsc, dg = _norm_bwd(du, xh, r, n, g, sc)
        dhi_ref[...] = dh_ref[...] + dhn
        acc_ref[0:1, :] += dsh
        acc_ref[1:2, :] += dsc
        acc_ref[2:3, :] += dg

    nxt = pl.BlockSpec((TM, PW), lambda i: (jnp.minimum(i + 1, nt - 1), 0))
    return _call(
        body, "mix_bwd_b", (nt,),
        [_rows(TM, D), _rows(TM, D), _const((8, D)), _rows(TM, PW), nxt] + [_rm_spec(dl) for dl in DIL] * 3
        + [_rows(TM, GW), _rows(TM, 128), _rows(TM, 128), ANY],
        [_rows(TM, D), _rows(TM, INW), _const((8, D))],
        [_sds((T, D), F32), _sds((T, INW), BF16), _sds((8, D), F32)],
        scratch=[pltpu.VMEM((D, INW), BF16), pltpu.VMEM((TM + HALO, PW), F32), pltpu.VMEM((GA // LANES, TM, LANES), F32),
                 pltpu.SemaphoreType.DMA((1,))],
        vmem=VMEM_BIG,
    )(dh, h, vec, dd, dd, *dqs, *dks, *dvs, dgates, cos, sin, win)[0]


def _ada_fwd(c_all, w_shard, b_shard):
    n = w_shard.shape[1]

    def body(c_ref, w_ref, b_ref, o_ref):
        cv = c_ref[...]
        cond = (cv * jax.nn.sigmoid(cv)).astype(BF16)
        o_ref[...] = _dot(cond, w_ref[...].astype(BF16)) + b_ref[...]

    tn = n // 3
    return pl.pallas_call(
        body, name="ada_fwd", grid=(3,),
        in_specs=[pl.BlockSpec((8, D), lambda j: (0, 0)), pl.BlockSpec((D, tn), lambda j: (0, j)), pl.BlockSpec((1, tn), lambda j: (0, j))],
        out_specs=pl.BlockSpec((8, tn), lambda j: (0, j)), out_shape=_sds((8, n), F32),
        compiler_params=pltpu.CompilerParams(dimension_semantics=("arbitrary",)),
    )(c_all, w_shard, b_shard)


def _ada_bwd(c_all, dmod_shard):
    n = dmod_shard.shape[1]

    def body(c_ref, d_ref, o_ref):
        cv = c_ref[...]
        cond = (cv * jax.nn.sigmoid(cv)).astype(BF16)
        o_ref[...] = _dot_tn(cond, d_ref[...].astype(BF16))

    tn = n // 3
    return pl.pallas_call(
        body, name="ada_bwd", grid=(3,),
        in_specs=[pl.BlockSpec((8, D), lambda j: (0, 0)), pl.BlockSpec((8, tn), lambda j: (0, j))],
        out_specs=pl.BlockSpec((D, tn), lambda j: (0, j)), out_shape=_sds((D, n), F32),
        compiler_params=pltpu.CompilerParams(dimension_semantics=("arbitrary",)),
    )(c_all, dmod_shard)


def _adam_math(w, g, m, v):
    m2 = B1 * m + (1.0 - B1) * g
    v2 = B2 * v + (1.0 - B2) * (g * g)
    m_hat = m2 / (1.0 - B1 ** STEP)
    v_hat = v2 / (1.0 - B2 ** STEP)
    delta = -LR * (m_hat / (jnp.sqrt(v_hat) + AEPS) + WD * w)
    return delta, m2, v2


def _adam(w, m, v, parts, name, comm=None):
    R, C = w.shape
    tr = R
    for cand in (128, 64, 32, 16, 8):
        if R % cand == 0:
            tr = cand
            break
    np_ = len(parts)

    def body(w_ref, m_ref, v_ref, *rest):
        p_refs, (g_ref, d_ref, m2_ref, v2_ref) = rest[:np_], rest[np_:]
        g = p_refs[0][...]
        for pr in p_refs[1:]:
            g = g + pr[...]
        delta, m2, v2 = _adam_math(w_ref[...], g, m_ref[...], v_ref[...])
        g_ref[...] = g
        d_ref[...] = delta
        m2_ref[...] = m2
        v2_ref[...] = v2

    spec = pl.BlockSpec((tr, C), lambda i: (i, 0))
    return _call(body, name, (R // tr,), [spec] * (3 + np_), [spec] * 4, [_sds((R, C), F32)] * 4,
                 vmem=VMEM_BIG, comm=comm)(w, m, v, *parts)


def _adam_halves(w, m, v, mine, other, name):
    R, C = w.shape
    tr = 128
    nh = R // 2 // tr

    def body(c_ref, w_ref, m_ref, v_ref, mine_ref, other_ref, g_ref, d_ref, m2_ref, v2_ref):
        i = pl.program_id(0)
        in_mine = jnp.logical_and(i >= c_ref[0] * nh, i < (c_ref[0] + 1) * nh)
        g = jnp.where(in_mine, mine_ref[...], other_ref[...])
        delta, m2, v2 = _adam_math(w_ref[...], g, m_ref[...], v_ref[...])
        g_ref[...] = g
        d_ref[...] = delta
        m2_ref[...] = m2
        v2_ref[...] = v2

    spec = pl.BlockSpec((tr, C), lambda i, c: (i, 0))
    grid_spec = pltpu.PrefetchScalarGridSpec(
        num_scalar_prefetch=1, grid=(R // tr,),
        in_specs=[spec] * 3 + [pl.BlockSpec((tr, C), lambda i, c: (jnp.clip(i - c[0] * nh, 0, nh - 1), 0)),
                               pl.BlockSpec((tr, C), lambda i, c: (jnp.clip(i - (1 - c[0]) * nh, 0, nh - 1), 0))],
        out_specs=[spec] * 4)
    return pl.pallas_call(
        body, name=name, grid_spec=grid_spec, out_shape=[_sds((R, C), F32)] * 4,
        compiler_params=pltpu.CompilerParams(dimension_semantics=("arbitrary",), vmem_limit_bytes=VMEM_BIG),
    )(lax.axis_index("c").astype(jnp.int32).reshape(1), w, m, v, mine, other)


def _adam_small(ws, ms, vs, gathered):
    n = len(ws)
    sizes = [a.shape[1] for a in ws]

    def total(ga_ref, off, size):
        g = ga_ref[0, :, off:off + size]
        for dev in range(1, 8):
            g = g + ga_ref[dev, :, off:off + size]
        return g

    def body(*refs):
        w_refs, m_refs, v_refs, ga_ref, outs = refs[:n], refs[n:2 * n], refs[2 * n:3 * n], refs[3 * n], refs[3 * n + 1:]
        off = 0
        for j, size in enumerate(sizes):
            g = total(ga_ref, off, size)
            delta, m2, v2 = _adam_math(w_refs[j][...], g, m_refs[j][...], v_refs[j][...])
            for ref, val in zip(outs[4 * j:4 * j + 4], (g, delta, m2, v2)):
                ref[...] = val
            off += size
        outs[4 * n][...] = total(ga_ref, off, 128)

    res = pl.pallas_call(
        body, name="adam_small",
        out_shape=[_sds((1, size), F32) for size in sizes for _ in range(4)] + [_sds((1, 128), F32)],
    )(*ws, *ms, *vs, gathered)
    return [res[4 * j:4 * j + 4] for j in range(n)], res[4 * n]


def _sum4(blocks, name):
    _, R, C = blocks.shape
    tr = R
    for cand in (256, 128, 64, 32, 16):
        if R % cand == 0:
            tr = cand
            break

    def body(r_ref, out_ref):
        out_ref[...] = ((r_ref[0].astype(F32) + r_ref[1].astype(F32)) + r_ref[2].astype(F32)) + r_ref[3].astype(F32)

    return pl.pallas_call(
        body, name=name, grid=(R // tr,),
        in_specs=[pl.BlockSpec((4, tr, C), lambda i: (0, i, 0))],
        out_specs=pl.BlockSpec((tr, C), lambda i: (i, 0)), out_shape=_sds((R, C), F32),
        compiler_params=pltpu.CompilerParams(dimension_semantics=("arbitrary",)),
    )(blocks)


def _place():
    return lax.axis_index("x"), lax.axis_index("y"), lax.axis_index("c")


def _chip_peer(x, y, c, m):
    return (x ^ (m >> 1), y ^ (m & 1), c)


def _shard_ref(ref, axis, k, n):
    start = pl.multiple_of(k * n, 128 if axis == 1 else 16)
    return ref.at[:, pl.ds(start, n)] if axis == 1 else ref.at[pl.ds(start, n), :]


def _half_rows(ref, axis, k, n, hc):
    if axis == 1:
        half = ref.shape[0] // 2
        return ref.at[pl.ds(pl.multiple_of(hc * half, 16), half), pl.ds(pl.multiple_of(k * n, 128), n)]
    half = n // 2
    return ref.at[pl.ds(pl.multiple_of(k * n + hc * half, 16), half), :]


class _GatherPlan:
    def __init__(self, shards, axes):
        self.inputs, self.axes, nw = list(shards), list(axes), len(shards)
        self.out_shapes = [_sds((s.shape[0] * (4 if ax == 0 else 1), s.shape[1] * (4 if ax == 1 else 1)), BF16)
                           for s, ax in zip(shards, axes)]
        self.sem_shapes = [pltpu.SemaphoreType.DMA((nw,))] + [pltpu.SemaphoreType.DMA((nw, 3))] * 4

    def _copies(self, ins, outs, sems):
        local_sems, send_sems, recv_sems, pass_sems, got_sems = sems
        x, y, c = _place()
        k = 2 * x + y
        local, sends, arrivals, passes, handed = [], [], [], [], []
        for j, ax in enumerate(self.axes):
            n = ins[j].shape[ax]
            half = ins[j].shape[0] // 2
            local.append(pltpu.make_async_copy(ins[j], _shard_ref(outs[j], ax, k, n), local_sems.at[j]))
            my_half = ins[j].at[pl.ds(pl.multiple_of(c * half, 16), half), :]
            for m in range(1, 4):
                sends.append(pltpu.make_async_remote_copy(
                    src_ref=my_half, dst_ref=_half_rows(outs[j], ax, k, n, c), send_sem=send_sems.at[j, m - 1],
                    recv_sem=recv_sems.at[j, m - 1], device_id=_chip_peer(x, y, c, m), device_id_type=MESH))
                theirs = _half_rows(outs[j], ax, k ^ m, n, c)
                arrivals.append(pltpu.make_async_remote_copy(
                    src_ref=my_half, dst_ref=theirs, send_sem=send_sems.at[j, m - 1], recv_sem=recv_sems.at[j, m - 1],
                    device_id=(x, y, c), device_id_type=MESH))
                passes.append(pltpu.make_async_remote_copy(
                    src_ref=theirs, dst_ref=theirs, send_sem=pass_sems.at[j, m - 1], recv_sem=got_sems.at[j, m - 1],
                    device_id=(x, y, 1 - c), device_id_type=MESH))
                other = _half_rows(outs[j], ax, k ^ m, n, 1 - c)
                handed.append(pltpu.make_async_remote_copy(
                    src_ref=other, dst_ref=other, send_sem=pass_sems.at[j, m - 1], recv_sem=got_sems.at[j, m - 1],
                    device_id=(x, y, c), device_id_type=MESH))
        return local, sends, arrivals, passes, handed

    def start(self, ins, outs, sems):
        local, sends, _, _, _ = self._copies(ins, outs, sems)
        for cp in local + sends:
            cp.start()

    def wait(self, ins, outs, sems):
        local, sends, arrivals, passes, handed = self._copies(ins, outs, sems)
        for arrived, onward in zip(arrivals, passes):
            arrived.wait_recv()
            onward.start()
        for cp in handed:
            cp.wait_recv()
        for cp in sends + passes:
            cp.wait_send()
        for cp in local:
            cp.wait()


class _ScatterPlan:
    def __init__(self, grads, axes):
        self.inputs, self.axes, nw = list(grads), list(axes), len(grads)
        self.shard_shapes = [(g.shape[0] // (4 if ax == 0 else 1), g.shape[1] // (4 if ax == 1 else 1))
                             for g, ax in zip(grads, axes)]
        self.out_shapes = [_sds((4,) + s, BF16) for s in self.shard_shapes]
        self.sem_shapes = [pltpu.SemaphoreType.DMA((nw,)), pltpu.SemaphoreType.DMA((nw, 3)), pltpu.SemaphoreType.DMA((nw, 3))]

    def _copies(self, ins, outs, sems):
        local_sems, send_sems, recv_sems = sems
        x, y, c = _place()
        k = 2 * x + y
        local, remote, arrivals = [], [], []
        for j, ax in enumerate(self.axes):
            n = self.shard_shapes[j][ax]
            local.append(pltpu.make_async_copy(_shard_ref(ins[j], ax, k, n), outs[j].at[0], local_sems.at[j]))
            for m in range(1, 4):
                remote.append(pltpu.make_async_remote_copy(
                    src_ref=_shard_ref(ins[j], ax, k ^ m, n), dst_ref=outs[j].at[m],
                    send_sem=send_sems.at[j, m - 1], recv_sem=recv_sems.at[j, m - 1],
                    device_id=_chip_peer(x, y, c, m), device_id_type=MESH))
                arrivals.append(pltpu.make_async_remote_copy(
                    src_ref=_shard_ref(ins[j], ax, k, n), dst_ref=outs[j].at[m],
                    send_sem=send_sems.at[j, m - 1], recv_sem=recv_sems.at[j, m - 1],
                    device_id=(x, y, c), device_id_type=MESH))
        return local, remote, arrivals

    def start(self, ins, outs, sems):
        local, remote, _ = self._copies(ins, outs, sems)
        for cp in local + remote:
            cp.start()

    def wait(self, ins, outs, sems):
        local, remote, arrivals = self._copies(ins, outs, sems)
        for cp in arrivals:
            cp.wait_recv()
        for cp in remote:
            cp.wait_send()
        for cp in local:
            cp.wait()


def _run_plan(plan, name):
    nc = len(plan.inputs)

    def body(*refs):
        ins, outs, sems = refs[:nc], refs[nc:2 * nc], refs[2 * nc:]
        plan.start(ins, outs, sems)
        plan.wait(ins, outs, sems)

    return pl.pallas_call(body, name=name, in_specs=[ANY] * nc, out_specs=[ANY] * nc, out_shape=list(plan.out_shapes),
                          scratch_shapes=list(plan.sem_shapes))(*plan.inputs)


class _SwapPlan:
    def __init__(self, parts):
        self.inputs, nw = list(parts), len(parts)
        self.out_shapes = [_sds(p.shape, p.dtype) for p in parts]
        self.sem_shapes = [pltpu.SemaphoreType.DMA((nw,)), pltpu.SemaphoreType.DMA((nw,))]

    def _copies(self, ins, outs, sems):
        send_sems, recv_sems = sems
        x, y, c = _place()
        return [pltpu.make_async_remote_copy(
            src_ref=ins[j], dst_ref=outs[j], send_sem=send_sems.at[j], recv_sem=recv_sems.at[j],
            device_id=(x, y, 1 - c), device_id_type=MESH) for j in range(len(ins))]

    def start(self, ins, outs, sems):
        for cp in self._copies(ins, outs, sems):
            cp.start()

    def wait(self, ins, outs, sems):
        for cp in self._copies(ins, outs, sems):
            cp.wait()


class _SmallGatherPlan:
    def __init__(self, v):
        self.inputs = [v]
        self.out_shapes = [_sds((8,) + v.shape, v.dtype)]
        self.sem_shapes = [pltpu.SemaphoreType.DMA((1,)), pltpu.SemaphoreType.DMA((7,)), pltpu.SemaphoreType.DMA((7,))]

    def _copies(self, ins, outs, sems):
        (v_ref,), (out_ref,), (local_sem, send_sems, recv_sems) = ins, outs, sems
        x, y, c = _place()
        me = 4 * x + 2 * y + c
        local = pltpu.make_async_copy(v_ref, out_ref.at[me], local_sem.at[0])
        sends, arrivals = [], []
        for m in range(1, 8):
            px, py, pc = x ^ (m >> 2), y ^ ((m >> 1) & 1), c ^ (m & 1)
            sends.append(pltpu.make_async_remote_copy(
                src_ref=v_ref, dst_ref=out_ref.at[me], send_sem=send_sems.at[m - 1], recv_sem=recv_sems.at[m - 1],
                device_id=(px, py, pc), device_id_type=MESH))
            arrivals.append(pltpu.make_async_remote_copy(
                src_ref=v_ref, dst_ref=out_ref.at[4 * px + 2 * py + pc], send_sem=send_sems.at[m - 1],
                recv_sem=recv_sems.at[m - 1], device_id=(x, y, c), device_id_type=MESH))
        return local, sends, arrivals

    def start(self, ins, outs, sems):
        local, sends, _ = self._copies(ins, outs, sems)
        for cp in [local] + sends:
            cp.start()

    def wait(self, ins, outs, sems):
        local, sends, arrivals = self._copies(ins, outs, sems)
        for cp in arrivals:
            cp.wait_recv()
        for cp in sends:
            cp.wait_send()
        local.wait()


class _PlanGroup:
    def __init__(self, plans):
        self.plans = [p for p in plans if p is not None]
        self.inputs = [a for p in self.plans for a in p.inputs]
        self.out_shapes = [s for p in self.plans for s in p.out_shapes]
        self.sem_shapes = [s for p in self.plans for s in p.sem_shapes]

    def _each(self, ins, outs, sems):
        i = s = 0
        for p in self.plans:
            n, ns = len(p.inputs), len(p.sem_shapes)
            yield p, ins[i:i + n], outs[i:i + n], sems[s:s + ns]
            i, s = i + n, s + ns

    def start(self, ins, outs, sems):
        for p, pi, po, ps in self._each(ins, outs, sems):
            p.start(pi, po, ps)

    def wait(self, ins, outs, sems):
        for p, pi, po, ps in self._each(ins, outs, sems):
            p.wait(pi, po, ps)

    def split(self, outs):
        res, i = [], 0
        for p in self.plans:
            res.append(outs[i:i + len(p.inputs)])
            i += len(p.inputs)
        return res


BIG = ("w_ffn1_in", "w_ffn1_out", "w_in", "w_pool_branch", "w_attn_branch", "w_out", "w_ffn2_in", "w_ffn2_out")
BIG_AXIS = {"w_ffn1_in": 1, "w_ffn1_out": 0, "w_in": 1, "w_pool_branch": 1, "w_attn_branch": 1, "w_out": 0,
            "w_ffn2_in": 1, "w_ffn2_out": 0}


class _Sharded:
    fused_scatter = True

    def __init__(self, shards):
        self.shards, self.full, self.recv = shards, {}, {}

    def gather_plan(self, names):
        return _GatherPlan([self.shards[n] for n in names], [BIG_AXIS[n.split("/")[0]] for n in names])

    def gather_now(self, names):
        self.gathered(names, _run_plan(self.gather_plan(names), "gather_" + names[0]))

    def gathered(self, names, outs):
        self.full.update(zip(names, outs))

    def scatter_plan(self, names, grads):
        return _ScatterPlan([grads[n] for n in names], [BIG_AXIS[n] for n in names])

    def scatter_now(self, names, grads):
        self.scattered(names, _run_plan(self.scatter_plan(names, grads), "scatter_" + names[0]))

    def scattered(self, names, outs):
        self.recv.update(zip(names, outs))


class _Whole:
    fused_scatter = False

    def __init__(self, full):
        self.full, self.recv = dict(full), {}

    def gather_plan(self, names):
        return None

    def gather_now(self, names):
        pass

    def gathered(self, names, outs):
        pass

    def scatter_plan(self, names, grads):
        return None

    def scatter_now(self, names, grads):
        pass

    def scattered(self, names, outs):
        pass


def _vec(rows):
    pad = [jnp.zeros((1, D), F32)] * (8 - len(rows))
    return jnp.concatenate([r.reshape(1, D) for r in rows] + pad, axis=0)


def _block_diag(w_pool):
    n, c = w_pool.shape[0], w_pool.shape[1]
    eye = jnp.eye(n, dtype=w_pool.dtype)
    return (eye[:, None, :, None] * w_pool[:, :, None, :]).reshape(n * c, n * c)


def _example_step(x, tgt, positions, mod, gains, w_pool, pool_scale, ws, pack=None):
    T = x.shape[0]
    assert (T // BLK // DIL[-1]) & (T // BLK // DIL[-1] - 1) == 0, "blocks per sequence must be a power of two"
    sh1, sc1, gt1, sh2, sc2, gt2, sh3, sc3, gt3 = [mod[j * D:(j + 1) * D] for j in range(NMOD)]
    g1, g2, g3, gf = gains
    vec1, vec2, vec3 = _vec([g1, sh1, sc1, gt1]), _vec([g2, sh2, sc2, gt2]), _vec([g3, sh3, sc3, gt3])
    inv_freq = 10000.0 ** (-jnp.arange(0, HD, 2, dtype=F32) / HD)
    ang = positions.astype(F32)[:, None] * inv_freq
    cos = jnp.tile(jnp.cos(ang), (1, 4))
    sin = jnp.tile(jnp.concatenate([-jnp.sin(ang), jnp.sin(ang)], axis=1), (1, 2))
    wp_bd = _block_diag(w_pool).astype(BF16)
    ones_bd = _block_diag(jnp.ones((NH, HD, HD), F32)).astype(BF16)
    ps = jnp.concatenate([pool_scale.reshape(1, PW), jnp.zeros((7, PW), F32)], axis=0)
    wb = ws.full

    if "w_ffn1_in" not in wb:
        ws.gather_now(["w_ffn1_in"])
    mixw = ["w_ffn1_out", "w_in", "w_pool_branch", "w_attn_branch", "w_out"]
    (u1, a1, b1), got = _ffn_ab(x, vec1, [wb["w_ffn1_in"]], "ffn1_ab", ws.gather_plan(mixw))
    ws.gathered(mixw, got)
    (h1, f1), _ = _ffn_out(x, a1, b1, vec1, wb["w_ffn1_out"], "ffn1_out")
    (u2, p, qs, ks, vs, gates), got = _mix_proj(h1, vec2, wb["w_in"], cos, sin, ws.gather_plan(["w_ffn2_in/0"]))
    ws.gathered(["w_ffn2_in/0"], got)
    qs, ks, vs = [_flat(t) for t in qs], [_flat(t) for t in ks], [_flat(t) for t in vs]
    nbs = [T // d // BLK for d in DIL]
    os, lses = [], []
    for gi, riders in enumerate((["w_ffn2_out"], ["w_ffn2_in/1"], None)):
        (o, lse), got = _attn_fwd(qs[gi], ks[gi], vs[gi], nbs[gi], f"attn_fwd{gi}", riders and ws.gather_plan(riders))
        ws.gathered(riders or [], got)
        os.append(o)
        lses.append(lse)
    win3 = [wb["w_ffn2_in/0"], wb["w_ffn2_in/1"]] if "w_ffn2_in/0" in wb else [wb["w_ffn2_in"]]
    os_r = [_by_residue(t, d) for t, d in zip(os, DIL)]
    lses_r = [_by_residue(t, d) for t, d in zip(lses, DIL)]
    h2, ypool, yattn, merged, mixout, dpool = _mix_merge(
        h1, vec2, p, os_r, lses_r, gates, wp_bd, ps, wb["w_pool_branch"], wb["w_attn_branch"], wb["w_out"])
    (dh3, u3, a3, b3, f3, lacc), _ = _ffn_fwd(h2, vec3, win3, wb["w_ffn2_out"], "ffn2_fwd", head=(tgt, _vec([gf])))
    loss = 0.5 * jnp.sum(lacc[0]) / D

    grads = {}

    def wgrad_cols(name, xx, yy, riders, extra=None):
        group = _PlanGroup([ws.scatter_plan(riders, grads) if riders else None, extra])
        plan = group if group.plans else None
        if ws.fused_scatter:
            blocks, got = _wgrad_scatter(xx, yy, "wg_" + name, min(2048, T // 2), comm=plan)
            ws.scattered([name], [blocks])
        else:
            grads[name], got = _wgrad(xx, yy, "wg_" + name, D, 512, 1024, comm=plan)
        parts = group.split(got)
        if len(parts) > (extra is not None):
            ws.scattered(riders, parts[0])
        return parts[-1] if extra is not None else None

    (dh2, dab3, s3, df3, acc3), _ = _ffn_bwd(dh3, h2, a3, b3, f3, vec3, win3, wb["w_ffn2_out"], "ffn2_bwd")
    grads["w_ffn2_out"], _ = _wgrad(s3, df3, "wg_ffn2_out", FC, 512, 1024)
    wgrad_cols("w_ffn2_in", u3, dab3, ["w_ffn2_out"])
    (dmo, dbp, dba, dgates, do0, do1, do2, e0, e1, e2, dd, dyp, acc2a, accps), _ = _mix_bwd_a(
        dh2, vec2, mixout, gates, ypool, yattn, dpool, os_r, lses_r, wp_bd, ps,
        wb["w_pool_branch"], wb["w_attn_branch"], wb["w_out"], ones_bd)
    grads["w_out"], _ = _wgrad(merged, dmo, "wg_out", D, 512, 1024)
    grads["w_pool_branch"], _ = _wgrad(ypool, dbp, "wg_pool_branch", PW, 512, 1024)
    grads["w_attn_branch"], _ = _wgrad(yattn, dba, "wg_attn_branch", GA, 512, 1024)
    gwp, _ = _wgrad(dpool, dyp, "wg_pool", PW, PW, 1024, out_dtype=F32)
    n = len(POOL_WINDOWS)
    c = PW // n
    grad_w_pool = jnp.stack([gwp[j * c:(j + 1) * c, j * c:(j + 1) * c] for j in range(n)], axis=0)
    small3 = ["w_out", "w_pool_branch", "w_attn_branch"]
    dqs, dks, dvs = [], [], []
    for gi, (do, e) in enumerate(((do0, e0), (do1, e1), (do2, e2))):
        plan = ws.scatter_plan(small3, grads) if gi == 0 else None
        (dq, dk, dv), got = _attn_bwd(qs[gi], ks[gi], vs[gi], _flat(do), lses[gi], _flat(e), nbs[gi], f"attn_bwd{gi}", plan)
        if gi == 0:
            ws.scattered(small3, got)
        dqs.append(_by_residue(dq, DIL[gi]))
        dks.append(_by_residue(dk, DIL[gi]))
        dvs.append(_by_residue(dv, DIL[gi]))
    dh1, dproj, acc2b = _mix_bwd_b(dh2, h1, vec2, dd, dqs, dks, dvs, dgates, cos, sin, wb["w_in"])
    wgrad_cols("w_in", u2, dproj, [])
    (dx, dab1, s1, df1, acc1), _ = _ffn_bwd(dh1, x, a1, b1, f1, vec1, [wb["w_ffn1_in"]], wb["w_ffn1_out"], "ffn1_bwd")
    grads["w_ffn1_out"], _ = _wgrad(s1, df1, "wg_ffn1_out", FC, 512, 1024)
    dmod = jnp.concatenate([acc1[0], acc1[1], acc1[3], acc2b[0], acc2b[1], acc2a[3], acc3[0], acc3[1], acc3[3]])
    dgains = jnp.stack([acc1[2], acc2b[2], acc3[2], lacc[1]], axis=0)
    row = None if pack is None else _SmallGatherPlan(pack(loss, dmod, dgains, grad_w_pool, accps[0]))
    rows = wgrad_cols("w_ffn1_in", u1, dab1, ["w_ffn1_out"], row)
    return loss, dx, dmod, dgains, grad_w_pool, accps[0], grads, None if rows is None else rows[0]


SMALL = ("b_ada", "g_norm_ffn1", "g_norm_mix", "g_norm_ffn2", "g_final", "pool_scale", "w_pool")
WEIGHTS = ("w_ada", "b_ada", "g_norm_ffn1", "w_ffn1_in", "w_ffn1_out", "g_norm_mix", "w_in", "w_pool", "pool_scale",
           "w_pool_branch", "w_attn_branch", "w_out", "g_norm_ffn2", "w_ffn2_in", "w_ffn2_out", "g_final")


def _pack_small(t):
    return jnp.concatenate([t[n].reshape(-1) for n in SMALL]).reshape(1, -1)


def kernel(x, c, positions, w_ada, b_ada, g_norm_ffn1, w_ffn1_in, w_ffn1_out, g_norm_mix, w_in, w_pool, pool_scale, w_pool_branch, w_attn_branch, w_out, g_norm_ffn2, w_ffn2_in, w_ffn2_out, g_final, loss_target, m_w_ada, m_b_ada, m_g_norm_ffn1, m_w_ffn1_in, m_w_ffn1_out, m_g_norm_mix, m_w_in, m_w_pool, m_pool_scale, m_w_pool_branch, m_w_attn_branch, m_w_out, m_g_norm_ffn2, m_w_ffn2_in, m_w_ffn2_out, m_g_final, v_w_ada, v_b_ada, v_g_norm_ffn1, v_w_ffn1_in, v_w_ffn1_out, v_g_norm_mix, v_w_in, v_w_pool, v_pool_scale, v_w_pool_branch, v_w_attn_branch, v_w_out, v_g_norm_ffn2, v_w_ffn2_in, v_w_ffn2_out, v_g_final):
    w = dict(w_ada=w_ada, b_ada=b_ada, g_norm_ffn1=g_norm_ffn1, w_ffn1_in=w_ffn1_in, w_ffn1_out=w_ffn1_out,
             g_norm_mix=g_norm_mix, w_in=w_in, w_pool=w_pool, pool_scale=pool_scale, w_pool_branch=w_pool_branch,
             w_attn_branch=w_attn_branch, w_out=w_out, g_norm_ffn2=g_norm_ffn2, w_ffn2_in=w_ffn2_in,
             w_ffn2_out=w_ffn2_out, g_final=g_final)
    mom = dict(w_ada=m_w_ada, b_ada=m_b_ada, g_norm_ffn1=m_g_norm_ffn1, w_ffn1_in=m_w_ffn1_in, w_ffn1_out=m_w_ffn1_out,
               g_norm_mix=m_g_norm_mix, w_in=m_w_in, w_pool=m_w_pool, pool_scale=m_pool_scale,
               w_pool_branch=m_w_pool_branch, w_attn_branch=m_w_attn_branch, w_out=m_w_out, g_norm_ffn2=m_g_norm_ffn2,
               w_ffn2_in=m_w_ffn2_in, w_ffn2_out=m_w_ffn2_out, g_final=m_g_final)
    var = dict(w_ada=v_w_ada, b_ada=v_b_ada, g_norm_ffn1=v_g_norm_ffn1, w_ffn1_in=v_w_ffn1_in, w_ffn1_out=v_w_ffn1_out,
               g_norm_mix=v_g_norm_mix, w_in=v_w_in, w_pool=v_w_pool, pool_scale=v_pool_scale,
               w_pool_branch=v_w_pool_branch, w_attn_branch=v_w_attn_branch, w_out=v_w_out, g_norm_ffn2=v_g_norm_ffn2,
               w_ffn2_in=v_w_ffn2_in, w_ffn2_out=v_w_ffn2_out, g_final=v_g_final)
    ix, iy, ic = _place()
    chip = 2 * ix + iy
    me = 4 * ix + 2 * iy + ic
    nada = w_ada.shape[2]

    shards = {n: w[n][0].astype(BF16) for n in BIG}
    half = D // 2
    shards["w_ffn2_in/0"], shards["w_ffn2_in/1"] = shards["w_ffn2_in"][:half], shards["w_ffn2_in"][half:]
    ws = _Sharded(shards)
    c_all = _run_plan(_SmallGatherPlan(c), "gather_c")[0][:, 0, :]
    b_shard = lax.dynamic_slice_in_dim(b_ada, chip * nada, nada, axis=1)
    mod_cols = _ada_fwd(c_all, w_ada[0], b_shard)
    first = _PlanGroup([_SmallGatherPlan(mod_cols), ws.gather_plan(["w_ffn1_in"])])
    (mod_all,), ffn1 = first.split(_run_plan(first, "gather_first"))
    ws.gathered(["w_ffn1_in"], ffn1)
    mod = jnp.concatenate([lax.dynamic_index_in_dim(mod_all[4 * (kk >> 1) + 2 * (kk & 1)], me, axis=0, keepdims=False)
                           for kk in range(4)])

    def pack(loss, dmod, dgains, g_w_pool, g_pool_scale):
        small_g = dict(b_ada=dmod, g_norm_ffn1=dgains[0], g_norm_mix=dgains[1], g_norm_ffn2=dgains[2],
                       g_final=dgains[3], pool_scale=g_pool_scale, w_pool=g_w_pool)
        return jnp.concatenate([_pack_small(small_g), jnp.pad(loss.reshape(1, 1), ((0, 0), (0, 127)))], axis=1)

    _, dx, _, _, _, _, _, gathered = _example_step(
        x[0], loss_target[0], positions[0], mod, (g_norm_ffn1[0], g_norm_mix[0], g_norm_ffn2[0], g_final),
        w_pool[0], pool_scale[0], ws, pack)

    per_weight, loss_tile = _adam_small(*[[t[n].reshape(1, -1) for n in SMALL] for t in (w, mom, var)], gathered)
    small_out = [{n: per_weight[j][kind].reshape(w[n].shape) for j, n in enumerate(SMALL)} for kind in range(4)]
    loss = loss_tile[0, 0]

    dmod_all = gathered[:, 0, :NMOD * D]
    dmod_cols = lax.dynamic_slice_in_dim(dmod_all, chip * nada, nada, axis=1)
    g_ada = _ada_bwd(c_all, dmod_cols)

    ada_out = _adam(w_ada[0], m_w_ada[0], v_w_ada[0], [g_ada], "adam_w_ada")[0]

    sums = {n: _sum4(ws.recv[n], "sum_" + n) for n in BIG}
    other = dict(zip(BIG, _run_plan(_SwapPlan([sums[n] for n in BIG]), "swap_sibling")))
    big_out = {}
    for n in BIG:
        if sums[n].shape[0] < w[n].shape[1]:
            big_out[n] = _adam_halves(w[n][0], mom[n][0], var[n][0], sums[n], other[n], "adam_" + n)
        else:
            big_out[n] = _adam(w[n][0], mom[n][0], var[n][0], [sums[n], other[n]], "adam_" + n)[0]

    def leaf(kind, n):
        if n == "w_ada":
            return ada_out[kind][None]
        if n in big_out:
            return big_out[n][kind][None]
        return small_out[kind][n]

    return (loss, dx[None], *[leaf(kind, n) for kind in range(4) for n in WEIGHTS])
```

```python
import jax
import jax.numpy as jnp
from jax import lax
from jax.experimental import pallas as pl
from jax.experimental.pallas import tpu as pltpu

F32 = jnp.float32
BF16 = jnp.bfloat16

D = 1024
FF = 2816
FC = 1408
PW = 256
GA = 256
HD = 64
LANES = 128
NH = GA // HD
NG = 3
DIL = (1, 4, 16)
BLK = 128
GW = 2 * D
INW = PW + 3 * NG * GA + GW
NMOD = 9
POOL_WINDOWS = (2, 4, 8, 16)
HALO = 16
EPS = 1e-6
SCALE = HD ** -0.5
NEG = -1e30

LR, B1, B2, AEPS, WD, STEP = 0.001, 0.9, 0.999, 1e-08, 0.01, 10

VMEM_BIG = 56 * 1024 * 1024
TM = 256

MESH = pl.DeviceIdType.MESH
ANY = pl.BlockSpec(memory_space=pl.ANY)


def _call(body, name, grid, in_specs, out_specs, out_shape, scratch=(), vmem=None, comm=None):
    params = pltpu.CompilerParams(dimension_semantics=("arbitrary",) * len(grid), vmem_limit_bytes=vmem)
    n_in, n_out, n_scr = len(in_specs), len(out_shape), len(scratch)
    if comm is None:
        call = pl.pallas_call(body, name=name, grid=grid, in_specs=list(in_specs), out_specs=list(out_specs),
                              out_shape=list(out_shape), scratch_shapes=list(scratch), compiler_params=params)
        return lambda *args: (call(*args), ())
    nc = len(comm.inputs)

    def body_with_comm(*refs):
        ins, refs = refs[:n_in], refs[n_in:]
        c_ins, refs = refs[:nc], refs[nc:]
        outs, refs = refs[:n_out], refs[n_out:]
        c_outs, refs = refs[:nc], refs[nc:]
        scr, sems = refs[:n_scr], refs[n_scr:]
        first = pl.program_id(0) == 0
        last = pl.program_id(0) == grid[0] - 1
        for ax in range(1, len(grid)):
            first = jnp.logical_and(first, pl.program_id(ax) == 0)
            last = jnp.logical_and(last, pl.program_id(ax) == grid[ax] - 1)

        @pl.when(first)
        def _():
            comm.start(c_ins, c_outs, sems)

        body(*ins, *outs, *scr)
        early_relay = len(grid) == 1 and grid[0] >= 4
        if early_relay:
            @pl.when(pl.program_id(0) == (3 * grid[0]) // 4)
            def _():
                comm.relay(c_ins, c_outs, sems)

        @pl.when(last)
        def _():
            if not early_relay:
                comm.relay(c_ins, c_outs, sems)
            comm.wait(c_ins, c_outs, sems)

    call = pl.pallas_call(
        body_with_comm, name=name, grid=grid, in_specs=list(in_specs) + [ANY] * nc,
        out_specs=list(out_specs) + [ANY] * nc, out_shape=list(out_shape) + list(comm.out_shapes),
        scratch_shapes=list(scratch) + list(comm.sem_shapes), compiler_params=params)

    def run(*args):
        res = call(*args, *comm.inputs)
        return res[:n_out], res[n_out:]

    return run


def _rows(tm, n):
    return pl.BlockSpec((tm, n), lambda i: (i, 0))


def _const(shape):
    return pl.BlockSpec(shape, lambda i: (0,) * len(shape))


def _sds(shape, dtype):
    return jax.ShapeDtypeStruct(shape, dtype)


def _dot(a, b):
    return jnp.dot(a, b, preferred_element_type=F32)


def _dot_nt(a, b):
    return lax.dot_general(a, b, (((1,), (1,)), ((), ())), preferred_element_type=F32)


def _dot_tn(a, b):
    return lax.dot_general(a, b, (((0,), (0,)), ((), ())), preferred_element_type=F32)


def _colsum(v):
    return jnp.sum(v, axis=0, keepdims=True)


def _norm_fwd(h, g, sh, sc):
    r = lax.rsqrt(jnp.mean(h * h, axis=-1, keepdims=True) + EPS)
    xh = h * r
    n = xh * g
    return xh, r, n, n * (1.0 + sc) + sh


def _norm_bwd(du, xh, r, n, g, sc):
    dn = du * (1.0 + sc)
    dxh = dn * g
    dh = r * (dxh - xh * jnp.mean(dxh * xh, axis=-1, keepdims=True))
    return dh, _colsum(du), _colsum(du * n), _colsum(dn * xh)


def _load_once(pairs, sems):
    @pl.when(pl.program_id(0) == 0)
    def _():
        cps = [pltpu.make_async_copy(src, dst, sems.at[j]) for j, (src, dst) in enumerate(pairs)]
        for cp in cps:
            cp.start()
        for cp in cps:
            cp.wait()


def _zero_first(ref):
    @pl.when(pl.program_id(0) == 0)
    def _():
        ref[...] = jnp.zeros(ref.shape, ref.dtype)


def _row_chunks(hbm_refs, vmem_ref):
    pairs, row = [], 0
    for ref in hbm_refs:
        pairs.append((ref, vmem_ref.at[pl.ds(row, ref.shape[0]), :]))
        row += ref.shape[0]
    return pairs


def _loss_head(hh, tgt, g):
    r = lax.rsqrt(jnp.mean(hh * hh, axis=-1, keepdims=True) + EPS)
    xh = hh * r
    err = xh * g - tgt
    dy = err * (1.0 / D)
    dxh = dy * g
    dh = r * (dxh - xh * jnp.mean(dxh * xh, axis=-1, keepdims=True))
    return dh, _colsum(err * err), _colsum(dy * xh)


def _ffn_fwd(h, vec, wins, wout, name, comm=None, head=None):
    T = h.shape[0]
    nwin = len(wins)
    nhead = 0 if head is None else 2

    def body(h_ref, vec_ref, *rest):
        head_refs, rest = rest[:nhead], rest[nhead:]
        win_hbms, rest = rest[:nwin], rest[nwin:]
        (wout_hbm, ho_ref, u_ref, a_ref, b_ref, f_ref), rest = rest[:6], rest[6:]
        lacc_refs, (win_v, wout_v, sems) = rest[:nhead // 2], rest[nhead // 2:]
        _load_once(_row_chunks(win_hbms, win_v) + [(wout_hbm, wout_v)], sems)
        hh = h_ref[...]
        g, sh, sc, gt = vec_ref[0:1, :], vec_ref[1:2, :], vec_ref[2:3, :], vec_ref[3:4, :]
        _, _, _, u = _norm_fwd(hh, g, sh, sc)
        ub = u.astype(BF16)
        u_ref[...] = ub
        acc = None
        for j in range(FF // FC):
            lo, hi = j * FC, (j + 1) * FC
            a = _dot(ub, win_v[:, lo:hi])
            b = _dot(ub, win_v[:, FF + lo:FF + hi])
            a_ref[:, lo:hi] = a.astype(BF16)
            b_ref[:, lo:hi] = b.astype(BF16)
            s = (a * jax.nn.sigmoid(a) * b).astype(BF16)
            part = _dot(s, wout_v[lo:hi, :])
            acc = part if acc is None else acc + part
        f_ref[...] = acc.astype(BF16)
        ho = hh + 0.5 * gt * acc
        if head is None:
            ho_ref[...] = ho
        else:
            _zero_first(lacc_refs[0])
            dh, sq, dg = _loss_head(ho, head_refs[0][...], head_refs[1][0:1, :])
            ho_ref[...] = dh
            lacc_refs[0][0:1, :] += sq
            lacc_refs[0][1:2, :] += dg

    head_specs = [] if head is None else [_rows(TM, D), _const((8, D))]
    lacc_spec = [] if head is None else [_const((8, D))]
    lacc_shape = [] if head is None else [_sds((8, D), F32)]
    return _call(
        body, name, (T // TM,),
        [_rows(TM, D), _const((8, D))] + head_specs + [ANY] * (nwin + 1),
        [_rows(TM, D), _rows(TM, D), _rows(TM, FF), _rows(TM, FF), _rows(TM, D)] + lacc_spec,
        [_sds((T, D), F32), _sds((T, D), BF16), _sds((T, FF), BF16), _sds((T, FF), BF16), _sds((T, D), BF16)] + lacc_shape,
        scratch=[pltpu.VMEM((D, 2 * FF), BF16), pltpu.VMEM((FF, D), BF16), pltpu.SemaphoreType.DMA((nwin + 1,))],
        vmem=VMEM_BIG, comm=comm,
    )(h, vec, *([] if head is None else head), *wins, wout)


def _ffn_ab(h, vec, wins, name, comm=None):
    T = h.shape[0]
    nwin = len(wins)

    def body(h_ref, vec_ref, *rest):
        win_hbms, (u_ref, a_ref, b_ref, win_v, sems) = rest[:nwin], rest[nwin:]
        _load_once(_row_chunks(win_hbms, win_v), sems)
        g, sh, sc = vec_ref[0:1, :], vec_ref[1:2, :], vec_ref[2:3, :]
        _, _, _, u = _norm_fwd(h_ref[...], g, sh, sc)
        ub = u.astype(BF16)
        u_ref[...] = ub
        for j in range(FF // FC):
            lo, hi = j * FC, (j + 1) * FC
            a_ref[:, lo:hi] = _dot(ub, win_v[:, lo:hi]).astype(BF16)
            b_ref[:, lo:hi] = _dot(ub, win_v[:, FF + lo:FF + hi]).astype(BF16)

    return _call(
        body, name, (T // TM,),
        [_rows(TM, D), _const((8, D))] + [ANY] * nwin,
        [_rows(TM, D), _rows(TM, FF), _rows(TM, FF)],
        [_sds((T, D), BF16), _sds((T, FF), BF16), _sds((T, FF), BF16)],
        scratch=[pltpu.VMEM((D, 2 * FF), BF16), pltpu.SemaphoreType.DMA((nwin,))],
        vmem=VMEM_BIG, comm=comm,
    )(h, vec, *wins)


def _ffn_out(h, a, b, vec, wout, name, comm=None):
    T = h.shape[0]

    def body(h_ref, a_ref, b_ref, vec_ref, wout_hbm, ho_ref, f_ref, wout_v, sems):
        _load_once([(wout_hbm, wout_v)], sems)
        gt = vec_ref[3:4, :]
        acc = None
        for j in range(FF // FC):
            lo, hi = j * FC, (j + 1) * FC
            av = a_ref[:, lo:hi].astype(F32)
            s = (av * jax.nn.sigmoid(av) * b_ref[:, lo:hi].astype(F32)).astype(BF16)
            part = _dot(s, wout_v[lo:hi, :])
            acc = part if acc is None else acc + part
        f_ref[...] = acc.astype(BF16)
        ho_ref[...] = h_ref[...] + 0.5 * gt * acc

    return _call(
        body, name, (T // TM,),
        [_rows(TM, D), _rows(TM, FF), _rows(TM, FF), _const((8, D)), ANY],
        [_rows(TM, D), _rows(TM, D)],
        [_sds((T, D), F32), _sds((T, D), BF16)],
        scratch=[pltpu.VMEM((FF, D), BF16), pltpu.SemaphoreType.DMA((1,))],
        vmem=VMEM_BIG, comm=comm,
    )(h, a, b, vec, wout)


def _ffn_bwd(dh, h, a, b, f, vec, wins, wout, name, comm=None):
    T = h.shape[0]
    nwin = len(wins)

    def body(dh_ref, h_ref, a_ref, b_ref, f_ref, vec_ref, *rest):
        win_hbms, (wout_hbm, dhi_ref, dab_ref, s_ref, df_ref, acc_ref, win_v, wout_v, sems) = rest[:nwin], rest[nwin:]
        _load_once(_row_chunks(win_hbms, win_v) + [(wout_hbm, wout_v)], sems)
        _zero_first(acc_ref)
        g, sh, sc, gt = vec_ref[0:1, :], vec_ref[1:2, :], vec_ref[2:3, :], vec_ref[3:4, :]
        dho = dh_ref[...]
        df = (0.5 * gt * dho).astype(BF16)
        df_ref[...] = df
        dgt = _colsum(0.5 * dho * f_ref[...].astype(F32))
        du = None
        for j in range(FF // FC):
            lo, hi = j * FC, (j + 1) * FC
            av = a_ref[:, lo:hi].astype(F32)
            bv = b_ref[:, lo:hi].astype(F32)
            ds = _dot_nt(df, wout_v[lo:hi, :])
            sig = jax.nn.sigmoid(av)
            sa = av * sig
            s_ref[:, lo:hi] = (sa * bv).astype(BF16)
            da = (ds * bv * (sig * (1.0 + av * (1.0 - sig)))).astype(BF16)
            db = (ds * sa).astype(BF16)
            dab_ref[:, lo:hi] = da
            dab_ref[:, FF + lo:FF + hi] = db
            part = _dot_nt(da, win_v[:, lo:hi]) + _dot_nt(db, win_v[:, FF + lo:FF + hi])
            du = part if du is None else du + part
        xh, r, n, _ = _norm_fwd(h_ref[...], g, sh, sc)
        dhn, dsh, dsc, dg = _norm_bwd(du, xh, r, n, g, sc)
        dhi_ref[...] = dho + dhn
        acc_ref[0:1, :] += dsh
        acc_ref[1:2, :] += dsc
        acc_ref[2:3, :] += dg
        acc_ref[3:4, :] += dgt

    return _call(
        body, name, (T // TM,),
        [_rows(TM, D), _rows(TM, D), _rows(TM, FF), _rows(TM, FF), _rows(TM, D), _const((8, D))] + [ANY] * (nwin + 1),
        [_rows(TM, D), _rows(TM, 2 * FF), _rows(TM, FF), _rows(TM, D), _const((8, D))],
        [_sds((T, D), F32), _sds((T, 2 * FF), BF16), _sds((T, FF), BF16), _sds((T, D), BF16), _sds((8, D), F32)],
        scratch=[pltpu.VMEM((D, 2 * FF), BF16), pltpu.VMEM((FF, D), BF16), pltpu.SemaphoreType.DMA((nwin + 1,))],
        vmem=VMEM_BIG, comm=comm,
    )(dh, h, a, b, f, vec, *wins, wout)


def _wgrad(x, y, name, tk, tn, tt, out_dtype=BF16, comm=None):
    T, K = x.shape
    N = y.shape[1]
    nt = T // tt

    def body(x_ref, y_ref, o_ref, acc_ref):
        t = pl.program_id(2)
        part = _dot_tn(x_ref[...], y_ref[...])

        @pl.when(t == 0)
        def _():
            acc_ref[...] = part

        @pl.when(t > 0)
        def _():
            acc_ref[...] += part

        @pl.when(t == nt - 1)
        def _():
            o_ref[...] = acc_ref[...].astype(out_dtype)

    (out,), c_outs = _call(
        body, name, (K // tk, N // tn, nt),
        [pl.BlockSpec((tt, tk), lambda i, j, t: (t, i)), pl.BlockSpec((tt, tn), lambda i, j, t: (t, j))],
        [pl.BlockSpec((tk, tn), lambda i, j, t: (i, j))], [_sds((K, N), out_dtype)],
        scratch=[pltpu.VMEM((tk, tn), F32)], vmem=VMEM_BIG, comm=comm,
    )(x, y)
    return out, c_outs


def _wgrad_scatter(x, y, name, tt, comm=None):
    T, K = x.shape
    n = y.shape[1] // 4
    nt = T // tt
    assert nt >= 2, "a block's hand-over is added one grid step into the next block"
    half = K // 2
    nc = 0 if comm is None else len(comm.inputs)

    def body(chip_ref, x_ref, y_ref, *refs):
        c_ins, refs = refs[:nc], refs[nc:]
        recv_ref, refs = refs[0], refs[1:]
        c_outs, refs = refs[:nc], refs[nc:]
        acc_ref, keep_ref, give_ref, take_ref, local_sem, give_sems, take_sems, send_sems, recv_sems = refs[:9]
        j, t = pl.program_id(0), pl.program_id(1)
        px, py, pc = _place()

        def hand_over(jj):
            return pltpu.make_async_remote_copy(
                src_ref=give_ref.at[jj], dst_ref=take_ref.at[jj], send_sem=give_sems.at[jj], recv_sem=take_sems.at[jj],
                device_id=(px, py, 1 - pc), device_id_type=MESH)

        def send(jj):
            m = jj + 1
            return pltpu.make_async_remote_copy(
                src_ref=keep_ref.at[jj], dst_ref=recv_ref.at[m], send_sem=send_sems.at[jj], recv_sem=recv_sems.at[jj],
                device_id=_chip_peer(px, py, pc, m), device_id_type=MESH)

        def add_sibling(jj):
            hand_over(jj).wait_recv()
            keep_ref[jj] = (keep_ref[jj].astype(F32) + take_ref[jj].astype(F32)).astype(BF16)

        if comm is not None:
            @pl.when(jnp.logical_and(j == 0, t == 0))
            def _():
                comm.start(c_ins, c_outs, refs[9:])

        part = _dot_tn(x_ref[...], y_ref[...])

        @pl.when(t == 0)
        def _():
            acc_ref[...] = part

        @pl.when(t > 0)
        def _():
            acc_ref[...] += part

        for jj in range(3):
            @pl.when(jnp.logical_and(j == jj + 1, t == 0))
            def _():
                add_sibling(jj)
                send(jj).start()

        for jj in range(4):
            @pl.when(jnp.logical_and(j == jj, t == nt - 1))
            def _():
                keep_ref[jj] = acc_ref[pl.ds(pl.multiple_of(pc * half, 16), half), :].astype(BF16)
                give_ref[jj] = acc_ref[pl.ds(pl.multiple_of((1 - pc) * half, 16), half), :].astype(BF16)
                hand_over(jj).start()

        @pl.when(jnp.logical_and(j == 3, t == nt - 1))
        def _():
            add_sibling(3)
            own = pltpu.make_async_copy(keep_ref.at[3], recv_ref.at[0], local_sem.at[0])
            own.start()
            for jj in range(3):
                send(jj).wait_recv()
            for jj in range(3):
                send(jj).wait_send()
            for jj in range(4):
                hand_over(jj).wait_send()
            own.wait()
            if comm is not None:
                comm.relay(c_ins, c_outs, refs[9:])
                comm.wait(c_ins, c_outs, refs[9:])

    grid_spec = pltpu.PrefetchScalarGridSpec(
        num_scalar_prefetch=1, grid=(4, nt),
        in_specs=[pl.BlockSpec((tt, K), lambda j, t, chip: (t, 0)),
                  pl.BlockSpec((tt, n), lambda j, t, chip: (t, chip[0] ^ ((j + 1) & 3)))] + [ANY] * nc,
        out_specs=[ANY] * (1 + nc),
        scratch_shapes=[pltpu.VMEM((K, n), F32)] + [pltpu.VMEM((4, half, n), BF16)] * 3
        + [pltpu.SemaphoreType.DMA((1,))] + [pltpu.SemaphoreType.DMA((4,))] * 2 + [pltpu.SemaphoreType.DMA((3,))] * 2
        + ([] if comm is None else list(comm.sem_shapes)))
    px, py, _ = _place()
    res = pl.pallas_call(
        body, name=name, grid_spec=grid_spec,
        out_shape=[_sds((4, half, n), BF16)] + ([] if comm is None else list(comm.out_shapes)),
        compiler_params=pltpu.CompilerParams(dimension_semantics=("arbitrary", "arbitrary"), vmem_limit_bytes=VMEM_BIG),
    )((2 * px + py).astype(jnp.int32).reshape(1), x, y, *([] if comm is None else comm.inputs))
    return res[0], res[1:]


def _swap_halves(t):
    w = t.shape[1]
    lane = lax.broadcasted_iota(jnp.int32, t.shape, 1)
    return jnp.where(lane % HD < HD // 2, pltpu.roll(t, w - HD // 2, 1), pltpu.roll(t, HD // 2, 1))


def _rope(t, cos, sin_signed):
    c = jnp.tile(cos, (1, t.shape[1] // cos.shape[1]))
    s = jnp.tile(sin_signed, (1, t.shape[1] // sin_signed.shape[1]))
    return t * c + _swap_halves(t) * s


def _rope_bwd(dt, cos, sin_signed):
    c = jnp.tile(cos, (1, dt.shape[1] // cos.shape[1]))
    s = jnp.tile(sin_signed, (1, dt.shape[1] // sin_signed.shape[1]))
    return dt * c + _swap_halves(dt * s)


def _rm_spec(dil):
    return pl.BlockSpec((dil, TM // dil, GA), lambda i: (0, i, 0))


def _to_residues(t, dst_ref, scr_ref, dil):
    if dil == 1:
        dst_ref[0] = t.astype(dst_ref.dtype)
        return
    for j in range(GA // LANES):
        scr_ref[j] = t[:, j * LANES:(j + 1) * LANES]
    for r in range(dil):
        for j in range(GA // LANES):
            rows = scr_ref.at[j][pl.ds(r, TM // dil, stride=dil), :]
            dst_ref[r, :, j * LANES:(j + 1) * LANES] = rows.astype(dst_ref.dtype)


def _from_residues(src_ref, scr_ref, dil):
    if dil == 1:
        return src_ref[0].astype(F32)
    for r in range(dil):
        for j in range(GA // LANES):
            scr_ref.at[j][pl.ds(r, TM // dil, stride=dil), :] = src_ref[r, :, j * LANES:(j + 1) * LANES].astype(F32)
    return jnp.concatenate([scr_ref[j] for j in range(GA // LANES)], axis=1)


def _mix_proj(h, vec, win, cos, sin, comm=None):
    T = h.shape[0]

    def body(h_ref, vec_ref, win_hbm, cos_ref, sin_ref, u_ref, p_ref, *rest):
        qkv_refs, gates_ref, win_v, scr_ref, sems = rest[:3 * NG], rest[3 * NG], rest[3 * NG + 1], rest[3 * NG + 2], rest[3 * NG + 3]
        _load_once([(win_hbm, win_v)], sems)
        g, sh, sc = vec_ref[0:1, :], vec_ref[1:2, :], vec_ref[2:3, :]
        _, _, _, u = _norm_fwd(h_ref[...], g, sh, sc)
        ub = u.astype(BF16)
        u_ref[...] = ub
        p_ref[...] = _dot(ub, win_v[:, 0:PW])
        cos_t, sin_t = cos_ref[...], sin_ref[...]
        for j in range(3 * NG):
            col = PW + j * GA
            t = _dot(ub, win_v[:, col:col + GA])
            if j < 2 * NG:
                t = _rope(t, cos_t, sin_t)
            _to_residues(t, qkv_refs[j], scr_ref, DIL[j % NG])
        for j in range(GW // 512):
            col = PW + 3 * NG * GA + j * 512
            gates_ref[:, j * 512:(j + 1) * 512] = jax.nn.sigmoid(_dot(ub, win_v[:, col:col + 512])).astype(BF16)

    outs, c_outs = _call(
        body, "mix_proj", (T // TM,),
        [_rows(TM, D), _const((8, D)), ANY, _rows(TM, 128), _rows(TM, 128)],
        [_rows(TM, D), _rows(TM, PW)] + [_rm_spec(d) for d in DIL] * 3 + [_rows(TM, GW)],
        [_sds((T, D), BF16), _sds((T, PW), F32)] + [_sds((d, T // d, GA), BF16) for d in DIL] * 3 + [_sds((T, GW), BF16)],
        scratch=[pltpu.VMEM((D, INW), BF16), pltpu.VMEM((GA // LANES, TM, LANES), F32), pltpu.SemaphoreType.DMA((1,))],
        vmem=VMEM_BIG, comm=comm,
    )(h, vec, win, cos, sin)
    return (outs[0], outs[1], outs[2:2 + NG], outs[2 + NG:2 + 2 * NG], outs[2 + 2 * NG:2 + 3 * NG], outs[2 + 3 * NG]), c_outs


def _head_masks():
    lane_head = lax.broadcasted_iota(jnp.int32, (BLK, GA), 1) // HD
    return [lane_head == hd for hd in range(NH)]


def _expand_heads(t, hm):
    return jnp.concatenate([jnp.where(m, t, jnp.zeros_like(t)) for m in hm], axis=0)


def _collapse_heads(tb, hm):
    out = None
    for hd, m in enumerate(hm):
        part = jnp.where(m, tb[hd * BLK:(hd + 1) * BLK, :], 0.0)
        out = part if out is None else out + part
    return out


def _head_rows(t):
    return jnp.concatenate([t[:, hd * HD:hd * HD + 1] for hd in range(NH)], axis=0)


def _band(has_prev):
    a = lax.broadcasted_iota(jnp.int32, (NH * BLK, 2 * BLK), 0) & (BLK - 1)
    c = lax.broadcasted_iota(jnp.int32, (NH * BLK, 2 * BLK), 1)
    return jnp.logical_and(c >= jnp.where(has_prev, a, BLK), c <= a + BLK)


def _attn_specs(nbt):
    cur = pl.BlockSpec((2 * BLK, GA), lambda i: (i, 0))
    prev = pl.BlockSpec((BLK, GA), lambda i: (jnp.maximum(2 * i - 1, 0), 0))
    nxt = pl.BlockSpec((BLK, GA), lambda i: (jnp.minimum(2 * i + 2, nbt - 1), 0))
    return cur, prev, nxt


def _attn_fwd(q, k, v, nb, name, comm=None):
    T = q.shape[0]
    nbt = T // BLK
    lo, hi = slice(0, BLK), slice(BLK, 2 * BLK)

    def block(qv, kcat, vcat, has_prev, hm):
        s = jnp.where(_band(has_prev), _dot_nt(_expand_heads(qv, hm), kcat) * SCALE, NEG)
        mx = jnp.max(s, axis=-1, keepdims=True)
        e = jnp.exp(s - mx)
        l = jnp.sum(e, axis=-1, keepdims=True)
        ob = _dot((e * (1.0 / l)).astype(BF16), vcat)
        return _collapse_heads(ob, hm), _collapse_heads(jnp.broadcast_to(mx + jnp.log(l), (NH * BLK, GA)), hm)

    def body(q_ref, k_ref, kp_ref, v_ref, vp_ref, o_ref, lse_ref):
        b0 = 2 * pl.program_id(0)
        hm = _head_masks()
        k_first = jnp.concatenate([kp_ref[...], k_ref[lo, :]], axis=0)
        v_first = jnp.concatenate([vp_ref[...], v_ref[lo, :]], axis=0)
        o_ref[lo, :], lse_ref[lo, :] = block(q_ref[lo, :], k_first, v_first, (b0 & (nb - 1)) != 0, hm)
        o_ref[hi, :], lse_ref[hi, :] = block(q_ref[hi, :], k_ref[...], v_ref[...], ((b0 + 1) & (nb - 1)) != 0, hm)

    cur, prev, _ = _attn_specs(nbt)
    return _call(body, name, (nbt // 2,), [cur, cur, prev, cur, prev], [cur, cur],
                 [_sds((T, GA), F32), _sds((T, GA), F32)], comm=comm)(q, k, k, v, v)


def _attn_bwd(q, k, v, do, lse, e, nb, name, comm=None):
    T = q.shape[0]
    nbt = T // BLK

    lo, hi = slice(0, BLK), slice(BLK, 2 * BLK)

    def probs_and_ds(qb, dob, kcat, vcat, lsev, ev, valid):
        p = jnp.where(valid, jnp.exp(_dot_nt(qb, kcat) * SCALE - _head_rows(lsev)), 0.0)
        return p, (p * (_dot_nt(dob, vcat) + _head_rows(ev))).astype(BF16)

    def body(q_ref, k_ref, v_ref, do_ref, lse_ref, e_ref, kp_ref, vp_ref, qn_ref, don_ref, lsen_ref, en_ref,
             dq_ref, dk_ref, dv_ref):
        b0 = 2 * pl.program_id(0)
        hm = _head_masks()
        q1, q2, q3 = _expand_heads(q_ref[lo, :], hm), _expand_heads(q_ref[hi, :], hm), _expand_heads(qn_ref[...], hm)
        do1, do2, do3 = (_expand_heads(do_ref[lo, :], hm), _expand_heads(do_ref[hi, :], hm),
                         _expand_heads(don_ref[...], hm))
        k1 = jnp.concatenate([kp_ref[...], k_ref[lo, :]], axis=0)
        v1 = jnp.concatenate([vp_ref[...], v_ref[lo, :]], axis=0)
        k2, v2 = k_ref[...], v_ref[...]
        p1, ds1 = probs_and_ds(q1, do1, k1, v1, lse_ref[lo, :], e_ref[lo, :], _band((b0 & (nb - 1)) != 0))
        p2, ds2 = probs_and_ds(q2, do2, k2, v2, lse_ref[hi, :], e_ref[hi, :], _band(((b0 + 1) & (nb - 1)) != 0))
        dq_ref[lo, :] = _collapse_heads(_dot(ds1, k1) * SCALE, hm)
        dq_ref[hi, :] = _collapse_heads(_dot(ds2, k2) * SCALE, hm)
        a = lax.broadcasted_iota(jnp.int32, (NH * BLK, BLK), 0) & (BLK - 1)
        c = lax.broadcasted_iota(jnp.int32, (NH * BLK, BLK), 1)
        valid3 = jnp.logical_and(c >= a, ((b0 + 2) & (nb - 1)) != 0)
        p3, ds3 = probs_and_ds(q3, do3, k_ref[hi, :], v_ref[hi, :], lsen_ref[...], en_ref[...], valid3)
        q12, q23 = jnp.concatenate([q1, q2], axis=0), jnp.concatenate([q2, q3], axis=0)
        do12, do23 = jnp.concatenate([do1, do2], axis=0), jnp.concatenate([do2, do3], axis=0)
        dk_ref[lo, :] = _dot_tn(jnp.concatenate([ds1[:, BLK:], ds2[:, :BLK]], axis=0), q12) * SCALE
        dk_ref[hi, :] = _dot_tn(jnp.concatenate([ds2[:, BLK:], ds3], axis=0), q23) * SCALE
        pb1, pb2, pb3 = p1.astype(BF16), p2.astype(BF16), p3.astype(BF16)
        dv_ref[lo, :] = _dot_tn(jnp.concatenate([pb1[:, BLK:], pb2[:, :BLK]], axis=0), do12).astype(BF16)
        dv_ref[hi, :] = _dot_tn(jnp.concatenate([pb2[:, BLK:], pb3], axis=0), do23).astype(BF16)

    cur, prev, nxt = _attn_specs(nbt)
    return _call(body, name, (nbt // 2,), [cur] * 6 + [prev, prev] + [nxt] * 4, [cur, cur, cur],
                 [_sds((T, GA), F32), _sds((T, GA), F32), _sds((T, GA), BF16)],
                 comm=comm)(q, k, v, do, lse, e, k, v, q, do, lse, e)


def _flat(t):
    return t.reshape(t.shape[0] * t.shape[1], t.shape[2])


def _by_residue(t, dil):
    return t.reshape(dil, t.shape[0] // dil, t.shape[1])


def _pool_consts(shape, row0):
    lane = lax.broadcasted_iota(jnp.int32, shape, 1)
    t = lax.broadcasted_iota(jnp.int32, shape, 0) + row0
    grp = lane // (PW // len(POOL_WINDOWS))
    win = jnp.where(grp == 0, POOL_WINDOWS[0], jnp.where(grp == 1, POOL_WINDOWS[1],
                    jnp.where(grp == 2, POOL_WINDOWS[2], POOL_WINDOWS[3])))
    cnt = jnp.minimum(t + 1, win).astype(F32)
    return grp, cnt


def _window_sums(ext_ref, base, step, tm):
    outs, run = [], None
    for j in range(POOL_WINDOWS[-1]):
        sl = ext_ref[pl.ds(base + step * j, tm), :]
        run = sl if run is None else run + sl
        if j + 1 in POOL_WINDOWS:
            outs.append(run)
    return outs


def _select_group(grp, vals):
    return jnp.where(grp == 0, vals[0], jnp.where(grp == 1, vals[1], jnp.where(grp == 2, vals[2], vals[3])))


def _pool_d(pc_ref, pp_ref, ext_ref, i, tm):
    ext_ref[0:HALO, :] = jnp.where(i > 0, pp_ref[tm - HALO:tm, :], 0.0)
    ext_ref[HALO:HALO + tm, :] = pc_ref[...]
    grp, cnt = _pool_consts((tm, PW), i * tm)
    sums = _window_sums(ext_ref, HALO, -1, tm)
    return _select_group(grp, sums) / cnt - pc_ref[...]


def _group_weights(ls):
    mx = jnp.maximum(jnp.maximum(ls[0], ls[1]), ls[2])
    es = [jnp.exp(l - mx) for l in ls]
    inv = 1.0 / (es[0] + es[1] + es[2])
    return [e * inv for e in es]


def _mix_merge(h, vec, p, os, lses, gates, wp_bd, pscale, wpb, wab, wout):
    T = h.shape[0]

    def body(h_ref, vec_ref, pc_ref, pp_ref, o0, o1, o2, l0, l1, l2, gates_ref, wp_ref, ps_ref, wpb_ref, wab_ref, wout_ref,
             ho_ref, yp_ref, ya_ref, mg_ref, mo_ref, d_ref, ext_ref, scr_ref):
        i = pl.program_id(0)
        gt = vec_ref[3:4, :]
        d = _pool_d(pc_ref, pp_ref, ext_ref, i, TM).astype(BF16)
        d_ref[...] = d
        ypool = (_dot(d, wp_ref[...]) * ps_ref[0:1, :]).astype(BF16)
        yp_ref[...] = ypool
        w = _group_weights([_from_residues(r, scr_ref, dl) for r, dl in zip((l0, l1, l2), DIL)])
        yattn = None
        for wg, o_ref, dl in zip(w, (o0, o1, o2), DIL):
            part = wg * _from_residues(o_ref, scr_ref, dl)
            yattn = part if yattn is None else yattn + part
        yattn = yattn.astype(BF16)
        ya_ref[...] = yattn
        merged = (gates_ref[:, 0:D].astype(F32) * _dot(ypool, wpb_ref[...])
                  + gates_ref[:, D:GW].astype(F32) * _dot(yattn, wab_ref[...])).astype(BF16)
        mg_ref[...] = merged
        mo = _dot(merged, wout_ref[...])
        mo_ref[...] = mo.astype(BF16)
        ho_ref[...] = h_ref[...] + gt * mo

    prev = pl.BlockSpec((TM, PW), lambda i: (jnp.maximum(i - 1, 0), 0))
    return _call(
        body, "mix_merge", (T // TM,),
        [_rows(TM, D), _const((8, D)), _rows(TM, PW), prev] + [_rm_spec(dl) for dl in DIL] * 2 + [_rows(TM, GW), _const((PW, PW)),
         _const((8, PW)), _const((PW, D)), _const((GA, D)), _const((D, D))],
        [_rows(TM, D), _rows(TM, PW), _rows(TM, GA), _rows(TM, D), _rows(TM, D), _rows(TM, PW)],
        [_sds((T, D), F32), _sds((T, PW), BF16), _sds((T, GA), BF16), _sds((T, D), BF16), _sds((T, D), BF16), _sds((T, PW), BF16)],
        scratch=[pltpu.VMEM((TM + HALO, PW), F32), pltpu.VMEM((GA // LANES, TM, LANES), F32)],
        vmem=VMEM_BIG,
    )(h, vec, p, p, *os, *lses, gates, wp_bd, pscale, wpb, wab, wout)[0]


def _mix_bwd_a(dh, vec, mixout, gates, ypool, yattn, dpool, os, lses, wp_bd, pscale, wpb, wab, wout, ones_bd, comm=None):
    T = dh.shape[0]

    def body(dh_ref, vec_ref, mo_ref, gates_ref, yp_ref, ya_ref, d_ref, o0, o1, o2, l0, l1, l2,
             wp_ref, ps_ref, wpb_ref, wab_ref, wout_ref, ones_ref,
             dmo_ref, dp_ref, da_ref, dgates_ref, do0, do1, do2, e0, e1, e2, dd_ref, dyp_ref, acc_ref, acc2_ref, scr_ref):
        _zero_first(acc_ref)
        _zero_first(acc2_ref)
        gt = vec_ref[3:4, :]
        dho = dh_ref[...]
        acc_ref[3:4, :] += _colsum(dho * mo_ref[...].astype(F32))
        dmo = (gt * dho).astype(BF16)
        dmo_ref[...] = dmo
        dmerged = _dot_nt(dmo, wout_ref[...])
        gp = gates_ref[:, 0:D].astype(F32)
        ga = gates_ref[:, D:GW].astype(F32)
        bp = _dot(yp_ref[...], wpb_ref[...])
        ba = _dot(ya_ref[...], wab_ref[...])
        dgates_ref[:, 0:D] = (dmerged * bp * gp * (1.0 - gp)).astype(BF16)
        dgates_ref[:, D:GW] = (dmerged * ba * ga * (1.0 - ga)).astype(BF16)
        dbp = (dmerged * gp).astype(BF16)
        dba = (dmerged * ga).astype(BF16)
        dp_ref[...] = dbp
        da_ref[...] = dba
        dypool = _dot_nt(dbp, wpb_ref[...])
        ypre = _dot(d_ref[...], wp_ref[...])
        acc2_ref[0:1, :] += _colsum(dypool * ypre)
        dyp = (dypool * ps_ref[0:1, :]).astype(BF16)
        dyp_ref[...] = dyp
        dd_ref[...] = _dot_nt(dyp, wp_ref[...])
        dya = _dot_nt(dba, wab_ref[...])
        w = _group_weights([_from_residues(r, scr_ref, dl) for r, dl in zip((l0, l1, l2), DIL)])
        ya = None
        for wg, o_ref, dl in zip(w, (o0, o1, o2), DIL):
            part = wg * _from_residues(o_ref, scr_ref, dl)
            ya = part if ya is None else ya + part
        prod = dya * ya
        hi = prod.astype(BF16)
        lo = (prod - hi.astype(F32)).astype(BF16)
        tot = _dot(hi, ones_ref[...]) + _dot(lo, ones_ref[...])
        for wg, do_ref, e_ref, dl in zip(w, (do0, do1, do2), (e0, e1, e2), DIL):
            _to_residues(wg * dya, do_ref, scr_ref, dl)
            _to_residues(-wg * tot, e_ref, scr_ref, dl)

    return _call(
        body, "mix_bwd_a", (T // TM,),
        [_rows(TM, D), _const((8, D)), _rows(TM, D), _rows(TM, GW), _rows(TM, PW), _rows(TM, GA), _rows(TM, PW)]
        + [_rm_spec(dl) for dl in DIL] * 2
        + [_const((PW, PW)), _const((8, PW)), _const((PW, D)), _const((GA, D)), _const((D, D)), _const((GA, GA))],
        [_rows(TM, D)] * 3 + [_rows(TM, GW)] + [_rm_spec(dl) for dl in DIL] * 2
        + [_rows(TM, PW), _rows(TM, PW), _const((8, D)), _const((8, PW))],
        [_sds((T, D), BF16)] * 3 + [_sds((T, GW), BF16)] + [_sds((dl, T // dl, GA), BF16) for dl in DIL]
        + [_sds((dl, T // dl, GA), F32) for dl in DIL]
        + [_sds((T, PW), F32), _sds((T, PW), BF16), _sds((8, D), F32), _sds((8, PW), F32)],
        scratch=[pltpu.VMEM((GA // LANES, TM, LANES), F32)],
        vmem=VMEM_BIG, comm=comm,
    )(dh, vec, mixout, gates, ypool, yattn, dpool, *os, *lses, wp_bd, pscale, wpb, wab, wout, ones_bd)


def _mix_bwd_b(dh, h, vec, dd, dqs, dks, dvs, dgates, cos, sin, win):
    T = h.shape[0]
    nt = T // TM

    def body(dh_ref, h_ref, vec_ref, ddc_ref, ddn_ref, *rest):
        qk_refs, dv_refs = rest[:2 * NG], rest[2 * NG:3 * NG]
        dgates_ref, cos_ref, sin_ref, win_hbm, dhi_ref, dproj_ref, acc_ref, win_v, ext_ref, scr_ref, sems = rest[3 * NG:]
        i = pl.program_id(0)
        _load_once([(win_hbm, win_v)], sems)
        _zero_first(acc_ref)
        g, sh, sc = vec_ref[0:1, :], vec_ref[1:2, :], vec_ref[2:3, :]
        grp, cnt = _pool_consts((TM, PW), i * TM)
        _, cnt_n = _pool_consts((HALO, PW), (i + 1) * TM)
        ext_ref[0:TM, :] = ddc_ref[...] / cnt
        ext_ref[TM:TM + HALO, :] = jnp.where(i < nt - 1, ddn_ref[0:HALO, :] / cnt_n, 0.0)
        dp = _select_group(grp, _window_sums(ext_ref, 0, 1, TM)) - ddc_ref[...]
        dproj_ref[:, 0:PW] = dp.astype(BF16)
        cos_t, sin_t = cos_ref[...], sin_ref[...]
        for j in range(2 * NG):
            col = PW + j * GA
            dt = _from_residues(qk_refs[j], scr_ref, DIL[j % NG])
            dproj_ref[:, col:col + GA] = _rope_bwd(dt, cos_t, sin_t).astype(BF16)
        for j in range(NG):
            col = PW + (2 * NG + j) * GA
            dproj_ref[:, col:col + GA] = _from_residues(dv_refs[j], scr_ref, DIL[j]).astype(BF16)
        dproj_ref[:, PW + 3 * NG * GA:INW] = dgates_ref[...]
        du = None
        for j in range(INW // 512):
            part = _dot_nt(dproj_ref[:, j * 512:(j + 1) * 512], win_v[:, j * 512:(j + 1) * 512])
            du = part if du is None else du + part
        xh, r, n, _ = _norm_fwd(h_ref[...], g, sh, sc)
        dhn, dsh, dsc, dg = _norm_bwd(du, xh, r, n, g, sc)
        dhi_ref[...] = dh_ref[...] + dhn
        acc_ref[0:1, :] += dsh
        acc_ref[1:2, :] += dsc
        acc_ref[2:3, :] += dg

    nxt = pl.BlockSpec((TM, PW), lambda i: (jnp.minimum(i + 1, nt - 1), 0))
    return _call(
        body, "mix_bwd_b", (nt,),
        [_rows(TM, D), _rows(TM, D), _const((8, D)), _rows(TM, PW), nxt] + [_rm_spec(dl) for dl in DIL] * 3
        + [_rows(TM, GW), _rows(TM, 128), _rows(TM, 128), ANY],
        [_rows(TM, D), _rows(TM, INW), _const((8, D))],
        [_sds((T, D), F32), _sds((T, INW), BF16), _sds((8, D), F32)],
        scratch=[pltpu.VMEM((D, INW), BF16), pltpu.VMEM((TM + HALO, PW), F32), pltpu.VMEM((GA // LANES, TM, LANES), F32),
                 pltpu.SemaphoreType.DMA((1,))],
        vmem=VMEM_BIG,
    )(dh, h, vec, dd, dd, *dqs, *dks, *dvs, dgates, cos, sin, win)[0]


def _ada_fwd(c_all, w_shard, b_shard):
    n = w_shard.shape[1]

    def body(c_ref, w_ref, b_ref, o_ref):
        cv = c_ref[...]
        cond = (cv * jax.nn.sigmoid(cv)).astype(BF16)
        o_ref[...] = _dot(cond, w_ref[...].astype(BF16)) + b_ref[...]

    tn = n // 3
    return pl.pallas_call(
        body, name="ada_fwd", grid=(3,),
        in_specs=[pl.BlockSpec((8, D), lambda j: (0, 0)), pl.BlockSpec((D, tn), lambda j: (0, j)), pl.BlockSpec((1, tn), lambda j: (0, j))],
        out_specs=pl.BlockSpec((8, tn), lambda j: (0, j)), out_shape=_sds((8, n), F32),
        compiler_params=pltpu.CompilerParams(dimension_semantics=("arbitrary",)),
    )(c_all, w_shard, b_shard)


def _ada_bwd(c_all, dmod_shard):
    n = dmod_shard.shape[1]

    def body(c_ref, d_ref, o_ref):
        cv = c_ref[...]
        cond = (cv * jax.nn.sigmoid(cv)).astype(BF16)
        o_ref[...] = _dot_tn(cond, d_ref[...].astype(BF16))

    tn = n // 3
    return pl.pallas_call(
        body, name="ada_bwd", grid=(3,),
        in_specs=[pl.BlockSpec((8, D), lambda j: (0, 0)), pl.BlockSpec((8, tn), lambda j: (0, j))],
        out_specs=pl.BlockSpec((D, tn), lambda j: (0, j)), out_shape=_sds((D, n), F32),
        compiler_params=pltpu.CompilerParams(dimension_semantics=("arbitrary",)),
    )(c_all, dmod_shard)


def _adam_math(w, g, m, v):
    m2 = B1 * m + (1.0 - B1) * g
    v2 = B2 * v + (1.0 - B2) * (g * g)
    m_hat = m2 / (1.0 - B1 ** STEP)
    v_hat = v2 / (1.0 - B2 ** STEP)
    delta = -LR * (m_hat / (jnp.sqrt(v_hat) + AEPS) + WD * w)
    return delta, m2, v2


def _adam(w, m, v, parts, name, comm=None):
    R, C = w.shape
    tr = R
    for cand in (128, 64, 32, 16, 8):
        if R % cand == 0:
            tr = cand
            break
    np_ = len(parts)

    def body(w_ref, m_ref, v_ref, *rest):
        p_refs, (g_ref, d_ref, m2_ref, v2_ref) = rest[:np_], rest[np_:]
        g = p_refs[0][...]
        for pr in p_refs[1:]:
            g = g + pr[...]
        delta, m2, v2 = _adam_math(w_ref[...], g, m_ref[...], v_ref[...])
        g_ref[...] = g
        d_ref[...] = delta
        m2_ref[...] = m2
        v2_ref[...] = v2

    spec = pl.BlockSpec((tr, C), lambda i: (i, 0))
    return _call(body, name, (R // tr,), [spec] * (3 + np_), [spec] * 4, [_sds((R, C), F32)] * 4,
                 vmem=VMEM_BIG, comm=comm)(w, m, v, *parts)


def _adam_halves(w, m, v, mine, other, name):
    R, C = w.shape
    tr = 128
    nh = R // 2 // tr

    def body(c_ref, w_ref, m_ref, v_ref, mine_ref, other_ref, g_ref, d_ref, m2_ref, v2_ref):
        i = pl.program_id(0)
        in_mine = jnp.logical_and(i >= c_ref[0] * nh, i < (c_ref[0] + 1) * nh)
        g = jnp.where(in_mine, mine_ref[...], other_ref[...])
        delta, m2, v2 = _adam_math(w_ref[...], g, m_ref[...], v_ref[...])
        g_ref[...] = g
        d_ref[...] = delta
        m2_ref[...] = m2
        v2_ref[...] = v2

    spec = pl.BlockSpec((tr, C), lambda i, c: (i, 0))
    grid_spec = pltpu.PrefetchScalarGridSpec(
        num_scalar_prefetch=1, grid=(R // tr,),
        in_specs=[spec] * 3 + [pl.BlockSpec((tr, C), lambda i, c: (jnp.clip(i - c[0] * nh, 0, nh - 1), 0)),
                               pl.BlockSpec((tr, C), lambda i, c: (jnp.clip(i - (1 - c[0]) * nh, 0, nh - 1), 0))],
        out_specs=[spec] * 4)
    return pl.pallas_call(
        body, name=name, grid_spec=grid_spec, out_shape=[_sds((R, C), F32)] * 4,
        compiler_params=pltpu.CompilerParams(dimension_semantics=("arbitrary",), vmem_limit_bytes=VMEM_BIG),
    )(lax.axis_index("c").astype(jnp.int32).reshape(1), w, m, v, mine, other)


def _adam_small(ws, ms, vs, gathered):
    n = len(ws)
    sizes = [a.shape[1] for a in ws]

    def total(ga_ref, off, size):
        g = ga_ref[0, :, off:off + size]
        for dev in range(1, 8):
            g = g + ga_ref[dev, :, off:off + size]
        return g

    def body(*refs):
        w_refs, m_refs, v_refs, ga_ref, outs = refs[:n], refs[n:2 * n], refs[2 * n:3 * n], refs[3 * n], refs[3 * n + 1:]
        off = 0
        for j, size in enumerate(sizes):
            g = total(ga_ref, off, size)
            delta, m2, v2 = _adam_math(w_refs[j][...], g, m_refs[j][...], v_refs[j][...])
            for ref, val in zip(outs[4 * j:4 * j + 4], (g, delta, m2, v2)):
                ref[...] = val
            off += size
        outs[4 * n][...] = total(ga_ref, off, 128)

    res = pl.pallas_call(
        body, name="adam_small",
        out_shape=[_sds((1, size), F32) for size in sizes for _ in range(4)] + [_sds((1, 128), F32)],
    )(*ws, *ms, *vs, gathered)
    return [res[4 * j:4 * j + 4] for j in range(n)], res[4 * n]


def _sum4(blocks, name):
    _, R, C = blocks.shape
    tr = R
    for cand in (256, 128, 64, 32, 16):
        if R % cand == 0:
            tr = cand
            break

    def body(r_ref, out_ref):
        out_ref[...] = ((r_ref[0].astype(F32) + r_ref[1].astype(F32)) + r_ref[2].astype(F32)) + r_ref[3].astype(F32)

    return pl.pallas_call(
        body, name=name, grid=(R // tr,),
        in_specs=[pl.BlockSpec((4, tr, C), lambda i: (0, i, 0))],
        out_specs=pl.BlockSpec((tr, C), lambda i: (i, 0)), out_shape=_sds((R, C), F32),
        compiler_params=pltpu.CompilerParams(dimension_semantics=("arbitrary",)),
    )(blocks)


def _place():
    return lax.axis_index("x"), lax.axis_index("y"), lax.axis_index("c")


def _chip_peer(x, y, c, m):
    return (x ^ (m >> 1), y ^ (m & 1), c)


def _shard_ref(ref, axis, k, n):
    start = pl.multiple_of(k * n, 128 if axis == 1 else 16)
    return ref.at[:, pl.ds(start, n)] if axis == 1 else ref.at[pl.ds(start, n), :]


def _half_rows(ref, axis, k, n, hc):
    if axis == 1:
        half = ref.shape[0] // 2
        return ref.at[pl.ds(pl.multiple_of(hc * half, 16), half), pl.ds(pl.multiple_of(k * n, 128), n)]
    half = n // 2
    return ref.at[pl.ds(pl.multiple_of(k * n + hc * half, 16), half), :]


class _GatherPlan:
    def __init__(self, shards, axes):
        self.inputs, self.axes, nw = list(shards), list(axes), len(shards)
        self.out_shapes = [_sds((s.shape[0] * (4 if ax == 0 else 1), s.shape[1] * (4 if ax == 1 else 1)), BF16)
                           for s, ax in zip(shards, axes)]
        self.sem_shapes = [pltpu.SemaphoreType.DMA((nw,))] + [pltpu.SemaphoreType.DMA((nw, 3))] * 4

    def _copies(self, ins, outs, sems):
        local_sems, send_sems, recv_sems, pass_sems, got_sems = sems
        x, y, c = _place()
        k = 2 * x + y
        local, sends, arrivals, passes, handed = [], [], [], [], []
        for j, ax in enumerate(self.axes):
            n = ins[j].shape[ax]
            half = ins[j].shape[0] // 2
            local.append(pltpu.make_async_copy(ins[j], _shard_ref(outs[j], ax, k, n), local_sems.at[j]))
            my_half = ins[j].at[pl.ds(pl.multiple_of(c * half, 16), half), :]
            for m in range(1, 4):
                sends.append(pltpu.make_async_remote_copy(
                    src_ref=my_half, dst_ref=_half_rows(outs[j], ax, k, n, c), send_sem=send_sems.at[j, m - 1],
                    recv_sem=recv_sems.at[j, m - 1], device_id=_chip_peer(x, y, c, m), device_id_type=MESH))
                theirs = _half_rows(outs[j], ax, k ^ m, n, c)
                arrivals.append(pltpu.make_async_remote_copy(
                    src_ref=my_half, dst_ref=theirs, send_sem=send_sems.at[j, m - 1], recv_sem=recv_sems.at[j, m - 1],
                    device_id=(x, y, c), device_id_type=MESH))
                passes.append(pltpu.make_async_remote_copy(
                    src_ref=theirs, dst_ref=theirs, send_sem=pass_sems.at[j, m - 1], recv_sem=got_sems.at[j, m - 1],
                    device_id=(x, y, 1 - c), device_id_type=MESH))
                other = _half_rows(outs[j], ax, k ^ m, n, 1 - c)
                handed.append(pltpu.make_async_remote_copy(
                    src_ref=other, dst_ref=other, send_sem=pass_sems.at[j, m - 1], recv_sem=got_sems.at[j, m - 1],
                    device_id=(x, y, c), device_id_type=MESH))
        return local, sends, arrivals, passes, handed

    def start(self, ins, outs, sems):
        local, sends, _, _, _ = self._copies(ins, outs, sems)
        for cp in local + sends:
            cp.start()

    def relay(self, ins, outs, sems):
        _, _, arrivals, passes, _ = self._copies(ins, outs, sems)
        for arrived, onward in zip(arrivals, passes):
            arrived.wait_recv()
            onward.start()

    def wait(self, ins, outs, sems):
        local, sends, _, passes, handed = self._copies(ins, outs, sems)
        for cp in handed:
            cp.wait_recv()
        for cp in sends + passes:
            cp.wait_send()
        for cp in local:
            cp.wait()


class _ScatterPlan:
    def __init__(self, grads, axes):
        self.inputs, self.axes, nw = list(grads), list(axes), len(grads)
        self.shard_shapes = [(g.shape[0] // (4 if ax == 0 else 1), g.shape[1] // (4 if ax == 1 else 1))
                             for g, ax in zip(grads, axes)]
        self.out_shapes = [_sds((4,) + s, BF16) for s in self.shard_shapes]
        self.sem_shapes = [pltpu.SemaphoreType.DMA((nw,)), pltpu.SemaphoreType.DMA((nw, 3)), pltpu.SemaphoreType.DMA((nw, 3))]

    def _copies(self, ins, outs, sems):
        local_sems, send_sems, recv_sems = sems
        x, y, c = _place()
        k = 2 * x + y
        local, remote, arrivals = [], [], []
        for j, ax in enumerate(self.axes):
            n = self.shard_shapes[j][ax]
            local.append(pltpu.make_async_copy(_shard_ref(ins[j], ax, k, n), outs[j].at[0], local_sems.at[j]))
            for m in range(1, 4):
                remote.append(pltpu.make_async_remote_copy(
                    src_ref=_shard_ref(ins[j], ax, k ^ m, n), dst_ref=outs[j].at[m],
                    send_sem=send_sems.at[j, m - 1], recv_sem=recv_sems.at[j, m - 1],
                    device_id=_chip_peer(x, y, c, m), device_id_type=MESH))
                arrivals.append(pltpu.make_async_remote_copy(
                    src_ref=_shard_ref(ins[j], ax, k, n), dst_ref=outs[j].at[m],
                    send_sem=send_sems.at[j, m - 1], recv_sem=recv_sems.at[j, m - 1],
                    device_id=(x, y, c), device_id_type=MESH))
        return local, remote, arrivals

    def start(self, ins, outs, sems):
        local, remote, _ = self._copies(ins, outs, sems)
        for cp in local + remote:
            cp.start()

    def relay(self, ins, outs, sems):
        pass

    def wait(self, ins, outs, sems):
        local, remote, arrivals = self._copies(ins, outs, sems)
        for cp in arrivals:
            cp.wait_recv()
        for cp in remote:
            cp.wait_send()
        for cp in local:
            cp.wait()


def _run_plan(plan, name):
    nc = len(plan.inputs)

    def body(*refs):
        ins, outs, sems = refs[:nc], refs[nc:2 * nc], refs[2 * nc:]
        plan.start(ins, outs, sems)
        plan.relay(ins, outs, sems)
        plan.wait(ins, outs, sems)

    return pl.pallas_call(body, name=name, in_specs=[ANY] * nc, out_specs=[ANY] * nc, out_shape=list(plan.out_shapes),
                          scratch_shapes=list(plan.sem_shapes))(*plan.inputs)


class _SwapPlan:
    def __init__(self, parts):
        self.inputs, nw = list(parts), len(parts)
        self.out_shapes = [_sds(p.shape, p.dtype) for p in parts]
        self.sem_shapes = [pltpu.SemaphoreType.DMA((nw,)), pltpu.SemaphoreType.DMA((nw,))]

    def _copies(self, ins, outs, sems):
        send_sems, recv_sems = sems
        x, y, c = _place()
        return [pltpu.make_async_remote_copy(
            src_ref=ins[j], dst_ref=outs[j], send_sem=send_sems.at[j], recv_sem=recv_sems.at[j],
            device_id=(x, y, 1 - c), device_id_type=MESH) for j in range(len(ins))]

    def start(self, ins, outs, sems):
        for cp in self._copies(ins, outs, sems):
            cp.start()

    def relay(self, ins, outs, sems):
        pass

    def wait(self, ins, outs, sems):
        for cp in self._copies(ins, outs, sems):
            cp.wait()


class _SmallGatherPlan:
    def __init__(self, v):
        self.inputs = [v]
        self.out_shapes = [_sds((8,) + v.shape, v.dtype)]
        self.sem_shapes = [pltpu.SemaphoreType.DMA((1,)), pltpu.SemaphoreType.DMA((7,)), pltpu.SemaphoreType.DMA((7,))]

    def _copies(self, ins, outs, sems):
        (v_ref,), (out_ref,), (local_sem, send_sems, recv_sems) = ins, outs, sems
        x, y, c = _place()
        me = 4 * x + 2 * y + c
        local = pltpu.make_async_copy(v_ref, out_ref.at[me], local_sem.at[0])
        sends, arrivals = [], []
        for m in range(1, 8):
            px, py, pc = x ^ (m >> 2), y ^ ((m >> 1) & 1), c ^ (m & 1)
            sends.append(pltpu.make_async_remote_copy(
                src_ref=v_ref, dst_ref=out_ref.at[me], send_sem=send_sems.at[m - 1], recv_sem=recv_sems.at[m - 1],
                device_id=(px, py, pc), device_id_type=MESH))
            arrivals.append(pltpu.make_async_remote_copy(
                src_ref=v_ref, dst_ref=out_ref.at[4 * px + 2 * py + pc], send_sem=send_sems.at[m - 1],
                recv_sem=recv_sems.at[m - 1], device_id=(x, y, c), device_id_type=MESH))
        return local, sends, arrivals

    def start(self, ins, outs, sems):
        local, sends, _ = self._copies(ins, outs, sems)
        for cp in [local] + sends:
            cp.start()

    def relay(self, ins, outs, sems):
        pass

    def wait(self, ins, outs, sems):
        local, sends, arrivals = self._copies(ins, outs, sems)
        for cp in arrivals:
            cp.wait_recv()
        for cp in sends:
            cp.wait_send()
        local.wait()


class _PlanGroup:
    def __init__(self, plans):
        self.plans = [p for p in plans if p is not None]
        self.inputs = [a for p in self.plans for a in p.inputs]
        self.out_shapes = [s for p in self.plans for s in p.out_shapes]
        self.sem_shapes = [s for p in self.plans for s in p.sem_shapes]

    def _each(self, ins, outs, sems):
        i = s = 0
        for p in self.plans:
            n, ns = len(p.inputs), len(p.sem_shapes)
            yield p, ins[i:i + n], outs[i:i + n], sems[s:s + ns]
            i, s = i + n, s + ns

    def start(self, ins, outs, sems):
        for p, pi, po, ps in self._each(ins, outs, sems):
            p.start(pi, po, ps)

    def relay(self, ins, outs, sems):
        for p, pi, po, ps in self._each(ins, outs, sems):
            p.relay(pi, po, ps)

    def wait(self, ins, outs, sems):
        for p, pi, po, ps in self._each(ins, outs, sems):
            p.wait(pi, po, ps)

    def split(self, outs):
        res, i = [], 0
        for p in self.plans:
            res.append(outs[i:i + len(p.inputs)])
            i += len(p.inputs)
        return res


BIG = ("w_ffn1_in", "w_ffn1_out", "w_in", "w_pool_branch", "w_attn_branch", "w_out", "w_ffn2_in", "w_ffn2_out")
BIG_AXIS = {"w_ffn1_in": 1, "w_ffn1_out": 0, "w_in": 1, "w_pool_branch": 1, "w_attn_branch": 1, "w_out": 0,
            "w_ffn2_in": 1, "w_ffn2_out": 0}


class _Sharded:
    fused_scatter = True

    def __init__(self, shards):
        self.shards, self.full, self.recv = shards, {}, {}

    def gather_plan(self, names):
        return _GatherPlan([self.shards[n] for n in names], [BIG_AXIS[n.split("/")[0]] for n in names])

    def gather_now(self, names):
        self.gathered(names, _run_plan(self.gather_plan(names), "gather_" + names[0]))

    def gathered(self, names, outs):
        self.full.update(zip(names, outs))

    def scatter_plan(self, names, grads):
        return _ScatterPlan([grads[n] for n in names], [BIG_AXIS[n] for n in names])

    def scatter_now(self, names, grads):
        self.scattered(names, _run_plan(self.scatter_plan(names, grads), "scatter_" + names[0]))

    def scattered(self, names, outs):
        self.recv.update(zip(names, outs))


class _Whole:
    fused_scatter = False

    def __init__(self, full):
        self.full, self.recv = dict(full), {}

    def gather_plan(self, names):
        return None

    def gather_now(self, names):
        pass

    def gathered(self, names, outs):
        pass

    def scatter_plan(self, names, grads):
        return None

    def scatter_now(self, names, grads):
        pass

    def scattered(self, names, outs):
        pass


def _vec(rows):
    pad = [jnp.zeros((1, D), F32)] * (8 - len(rows))
    return jnp.concatenate([r.reshape(1, D) for r in rows] + pad, axis=0)


def _block_diag(w_pool):
    n, c = w_pool.shape[0], w_pool.shape[1]
    eye = jnp.eye(n, dtype=w_pool.dtype)
    return (eye[:, None, :, None] * w_pool[:, :, None, :]).reshape(n * c, n * c)


def _example_step(x, tgt, positions, mod, gains, w_pool, pool_scale, ws, pack=None):
    T = x.shape[0]
    assert (T // BLK // DIL[-1]) & (T // BLK // DIL[-1] - 1) == 0, "blocks per sequence must be a power of two"
    sh1, sc1, gt1, sh2, sc2, gt2, sh3, sc3, gt3 = [mod[j * D:(j + 1) * D] for j in range(NMOD)]
    g1, g2, g3, gf = gains
    vec1, vec2, vec3 = _vec([g1, sh1, sc1, gt1]), _vec([g2, sh2, sc2, gt2]), _vec([g3, sh3, sc3, gt3])
    inv_freq = 10000.0 ** (-jnp.arange(0, HD, 2, dtype=F32) / HD)
    ang = positions.astype(F32)[:, None] * inv_freq
    cos = jnp.tile(jnp.cos(ang), (1, 4))
    sin = jnp.tile(jnp.concatenate([-jnp.sin(ang), jnp.sin(ang)], axis=1), (1, 2))
    wp_bd = _block_diag(w_pool).astype(BF16)
    ones_bd = _block_diag(jnp.ones((NH, HD, HD), F32)).astype(BF16)
    ps = jnp.concatenate([pool_scale.reshape(1, PW), jnp.zeros((7, PW), F32)], axis=0)
    wb = ws.full

    if "w_ffn1_in" not in wb:
        ws.gather_now(["w_ffn1_in"])
    (u1, a1, b1), got = _ffn_ab(x, vec1, [wb["w_ffn1_in"]], "ffn1_ab", ws.gather_plan(["w_ffn1_out"]))
    ws.gathered(["w_ffn1_out"], got)
    mixw = ["w_in", "w_pool_branch", "w_attn_branch", "w_out"]
    (h1, f1), got = _ffn_out(x, a1, b1, vec1, wb["w_ffn1_out"], "ffn1_out", ws.gather_plan(mixw))
    ws.gathered(mixw, got)
    (u2, p, qs, ks, vs, gates), got = _mix_proj(h1, vec2, wb["w_in"], cos, sin, ws.gather_plan(["w_ffn2_in/0"]))
    ws.gathered(["w_ffn2_in/0"], got)
    qs, ks, vs = [_flat(t) for t in qs], [_flat(t) for t in ks], [_flat(t) for t in vs]
    nbs = [T // d // BLK for d in DIL]
    os, lses = [], []
    for gi, riders in enumerate((["w_ffn2_out"], ["w_ffn2_in/1"], None)):
        (o, lse), got = _attn_fwd(qs[gi], ks[gi], vs[gi], nbs[gi], f"attn_fwd{gi}", riders and ws.gather_plan(riders))
        ws.gathered(riders or [], got)
        os.append(o)
        lses.append(lse)
    win3 = [wb["w_ffn2_in/0"], wb["w_ffn2_in/1"]] if "w_ffn2_in/0" in wb else [wb["w_ffn2_in"]]
    os_r = [_by_residue(t, d) for t, d in zip(os, DIL)]
    lses_r = [_by_residue(t, d) for t, d in zip(lses, DIL)]
    h2, ypool, yattn, merged, mixout, dpool = _mix_merge(
        h1, vec2, p, os_r, lses_r, gates, wp_bd, ps, wb["w_pool_branch"], wb["w_attn_branch"], wb["w_out"])
    (dh3, u3, a3, b3, f3, lacc), _ = _ffn_fwd(h2, vec3, win3, wb["w_ffn2_out"], "ffn2_fwd", head=(tgt, _vec([gf])))
    loss = 0.5 * jnp.sum(lacc[0]) / D

    grads = {}

    def wgrad_cols(name, xx, yy, riders, extra=None):
        group = _PlanGroup([ws.scatter_plan(riders, grads) if riders else None, extra])
        plan = group if group.plans else None
        if ws.fused_scatter:
            blocks, got = _wgrad_scatter(xx, yy, "wg_" + name, min(2048, T // 2), comm=plan)
            ws.scattered([name], [blocks])
        else:
            grads[name], got = _wgrad(xx, yy, "wg_" + name, D, 512, 1024, comm=plan)
        parts = group.split(got)
        if len(parts) > (extra is not None):
            ws.scattered(riders, parts[0])
        return parts[-1] if extra is not None else None

    (dh2, dab3, s3, df3, acc3), _ = _ffn_bwd(dh3, h2, a3, b3, f3, vec3, win3, wb["w_ffn2_out"], "ffn2_bwd")
    grads["w_ffn2_out"], _ = _wgrad(s3, df3, "wg_ffn2_out", FC, 512, 1024)
    wgrad_cols("w_ffn2_in", u3, dab3, ["w_ffn2_out"])
    (dmo, dbp, dba, dgates, do0, do1, do2, e0, e1, e2, dd, dyp, acc2a, accps), _ = _mix_bwd_a(
        dh2, vec2, mixout, gates, ypool, yattn, dpool, os_r, lses_r, wp_bd, ps,
        wb["w_pool_branch"], wb["w_attn_branch"], wb["w_out"], ones_bd)
    grads["w_out"], _ = _wgrad(merged, dmo, "wg_out", D, 512, 1024)
    grads["w_pool_branch"], _ = _wgrad(ypool, dbp, "wg_pool_branch", PW, 512, 1024)
    grads["w_attn_branch"], _ = _wgrad(yattn, dba, "wg_attn_branch", GA, 512, 1024)
    gwp, _ = _wgrad(dpool, dyp, "wg_pool", PW, PW, 1024, out_dtype=F32)
    n = len(POOL_WINDOWS)
    c = PW // n
    grad_w_pool = jnp.stack([gwp[j * c:(j + 1) * c, j * c:(j + 1) * c] for j in range(n)], axis=0)
    small3 = ["w_out", "w_pool_branch", "w_attn_branch"]
    dqs, dks, dvs = [], [], []
    for gi, (do, e) in enumerate(((do0, e0), (do1, e1), (do2, e2))):
        plan = ws.scatter_plan(small3, grads) if gi == 0 else None
        (dq, dk, dv), got = _attn_bwd(qs[gi], ks[gi], vs[gi], _flat(do), lses[gi], _flat(e), nbs[gi], f"attn_bwd{gi}", plan)
        if gi == 0:
            ws.scattered(small3, got)
        dqs.append(_by_residue(dq, DIL[gi]))
        dks.append(_by_residue(dk, DIL[gi]))
        dvs.append(_by_residue(dv, DIL[gi]))
    dh1, dproj, acc2b = _mix_bwd_b(dh2, h1, vec2, dd, dqs, dks, dvs, dgates, cos, sin, wb["w_in"])
    wgrad_cols("w_in", u2, dproj, [])
    (dx, dab1, s1, df1, acc1), _ = _ffn_bwd(dh1, x, a1, b1, f1, vec1, [wb["w_ffn1_in"]], wb["w_ffn1_out"], "ffn1_bwd")
    grads["w_ffn1_out"], _ = _wgrad(s1, df1, "wg_ffn1_out", FC, 512, 1024)
    dmod = jnp.concatenate([acc1[0], acc1[1], acc1[3], acc2b[0], acc2b[1], acc2a[3], acc3[0], acc3[1], acc3[3]])
    dgains = jnp.stack([acc1[2], acc2b[2], acc3[2], lacc[1]], axis=0)
    row = None if pack is None else _SmallGatherPlan(pack(loss, dmod, dgains, grad_w_pool, accps[0]))
    rows = wgrad_cols("w_ffn1_in", u1, dab1, ["w_ffn1_out"], row)
    return loss, dx, dmod, dgains, grad_w_pool, accps[0], grads, None if rows is None else rows[0]


SMALL = ("b_ada", "g_norm_ffn1", "g_norm_mix", "g_norm_ffn2", "g_final", "pool_scale", "w_pool")
WEIGHTS = ("w_ada", "b_ada", "g_norm_ffn1", "w_ffn1_in", "w_ffn1_out", "g_norm_mix", "w_in", "w_pool", "pool_scale",
           "w_pool_branch", "w_attn_branch", "w_out", "g_norm_ffn2", "w_ffn2_in", "w_ffn2_out", "g_final")


def _pack_small(t):
    return jnp.concatenate([t[n].reshape(-1) for n in SMALL]).reshape(1, -1)


def kernel(x, c, positions, w_ada, b_ada, g_norm_ffn1, w_ffn1_in, w_ffn1_out, g_norm_mix, w_in, w_pool, pool_scale, w_pool_branch, w_attn_branch, w_out, g_norm_ffn2, w_ffn2_in, w_ffn2_out, g_final, loss_target, m_w_ada, m_b_ada, m_g_norm_ffn1, m_w_ffn1_in, m_w_ffn1_out, m_g_norm_mix, m_w_in, m_w_pool, m_pool_scale, m_w_pool_branch, m_w_attn_branch, m_w_out, m_g_norm_ffn2, m_w_ffn2_in, m_w_ffn2_out, m_g_final, v_w_ada, v_b_ada, v_g_norm_ffn1, v_w_ffn1_in, v_w_ffn1_out, v_g_norm_mix, v_w_in, v_w_pool, v_pool_scale, v_w_pool_branch, v_w_attn_branch, v_w_out, v_g_norm_ffn2, v_w_ffn2_in, v_w_ffn2_out, v_g_final):
    w = dict(w_ada=w_ada, b_ada=b_ada, g_norm_ffn1=g_norm_ffn1, w_ffn1_in=w_ffn1_in, w_ffn1_out=w_ffn1_out,
             g_norm_mix=g_norm_mix, w_in=w_in, w_pool=w_pool, pool_scale=pool_scale, w_pool_branch=w_pool_branch,
             w_attn_branch=w_attn_branch, w_out=w_out, g_norm_ffn2=g_norm_ffn2, w_ffn2_in=w_ffn2_in,
             w_ffn2_out=w_ffn2_out, g_final=g_final)
    mom = dict(w_ada=m_w_ada, b_ada=m_b_ada, g_norm_ffn1=m_g_norm_ffn1, w_ffn1_in=m_w_ffn1_in, w_ffn1_out=m_w_ffn1_out,
               g_norm_mix=m_g_norm_mix, w_in=m_w_in, w_pool=m_w_pool, pool_scale=m_pool_scale,
               w_pool_branch=m_w_pool_branch, w_attn_branch=m_w_attn_branch, w_out=m_w_out, g_norm_ffn2=m_g_norm_ffn2,
               w_ffn2_in=m_w_ffn2_in, w_ffn2_out=m_w_ffn2_out, g_final=m_g_final)
    var = dict(w_ada=v_w_ada, b_ada=v_b_ada, g_norm_ffn1=v_g_norm_ffn1, w_ffn1_in=v_w_ffn1_in, w_ffn1_out=v_w_ffn1_out,
               g_norm_mix=v_g_norm_mix, w_in=v_w_in, w_pool=v_w_pool, pool_scale=v_pool_scale,
               w_pool_branch=v_w_pool_branch, w_attn_branch=v_w_attn_branch, w_out=v_w_out, g_norm_ffn2=v_g_norm_ffn2,
               w_ffn2_in=v_w_ffn2_in, w_ffn2_out=v_w_ffn2_out, g_final=v_g_final)
    ix, iy, ic = _place()
    chip = 2 * ix + iy
    me = 4 * ix + 2 * iy + ic
    nada = w_ada.shape[2]

    shards = {n: w[n][0].astype(BF16) for n in BIG}
    half = D // 2
    shards["w_ffn2_in/0"], shards["w_ffn2_in/1"] = shards["w_ffn2_in"][:half], shards["w_ffn2_in"][half:]
    ws = _Sharded(shards)
    c_all = _run_plan(_SmallGatherPlan(c), "gather_c")[0][:, 0, :]
    b_shard = lax.dynamic_slice_in_dim(b_ada, chip * nada, nada, axis=1)
    mod_cols = _ada_fwd(c_all, w_ada[0], b_shard)
    first = _PlanGroup([_SmallGatherPlan(mod_cols), ws.gather_plan(["w_ffn1_in"])])
    (mod_all,), ffn1 = first.split(_run_plan(first, "gather_first"))
    ws.gathered(["w_ffn1_in"], ffn1)
    mod = jnp.concatenate([lax.dynamic_index_in_dim(mod_all[4 * (kk >> 1) + 2 * (kk & 1)], me, axis=0, keepdims=False)
                           for kk in range(4)])

    def pack(loss, dmod, dgains, g_w_pool, g_pool_scale):
        small_g = dict(b_ada=dmod, g_norm_ffn1=dgains[0], g_norm_mix=dgains[1], g_norm_ffn2=dgains[2],
                       g_final=dgains[3], pool_scale=g_pool_scale, w_pool=g_w_pool)
        return jnp.concatenate([_pack_small(small_g), jnp.pad(loss.reshape(1, 1), ((0, 0), (0, 127)))], axis=1)

    _, dx, _, _, _, _, _, gathered = _example_step(
        x[0], loss_target[0], positions[0], mod, (g_norm_ffn1[0], g_norm_mix[0], g_norm_ffn2[0], g_final),
        w_pool[0], pool_scale[0], ws, pack)

    per_weight, loss_tile = _adam_small(*[[t[n].reshape(1, -1) for n in SMALL] for t in (w, mom, var)], gathered)
    small_out = [{n: per_weight[j][kind].reshape(w[n].shape) for j, n in enumerate(SMALL)} for kind in range(4)]
    loss = loss_tile[0, 0]

    dmod_all = gathered[:, 0, :NMOD * D]
    dmod_cols = lax.dynamic_slice_in_dim(dmod_all, chip * nada, nada, axis=1)
    g_ada = _ada_bwd(c_all, dmod_cols)

    ada_out = _adam(w_ada[0], m_w_ada[0], v_w_ada[0], [g_ada], "adam_w_ada")[0]

    sums = {n: _sum4(ws.recv[n], "sum_" + n) for n in BIG}
    other = dict(zip(BIG, _run_plan(_SwapPlan([sums[n] for n in BIG]), "swap_sibling")))
    big_out = {}
    for n in BIG:
        if sums[n].shape[0] < w[n].shape[1]:
            big_out[n] = _adam_halves(w[n][0], mom[n][0], var[n][0], sums[n], other[n], "adam_" + n)
        else:
            big_out[n] = _adam(w[n][0], mom[n][0], var[n][0], [sums[n], other[n]], "adam_" + n)[0]

    def leaf(kind, n):
        if n == "w_ada":
            return ada_out[kind][None]
        if n in big_out:
            return big_out[n][kind][None]
        return small_out[kind][n]

    return (loss, dx[None], *[leaf(kind, n) for kind in range(4) for n in WEIGHTS])
```

```python
import jax
import jax.numpy as jnp
from jax import lax
from jax.experimental import pallas as pl
from jax.experimental.pallas import tpu as pltpu

F32 = jnp.float32
BF16 = jnp.bfloat16

D = 1024
FF = 2816
FC = 1408
PW = 256
GA = 256
HD = 64
LANES = 128
NH = GA // HD
NG = 3
DIL = (1, 4, 16)
BLK = 128
GW = 2 * D
INW = PW + 3 * NG * GA + GW
NMOD = 9
POOL_WINDOWS = (2, 4, 8, 16)
HALO = 16
EPS = 1e-6
SCALE = HD ** -0.5
NEG = -1e30

LR, B1, B2, AEPS, WD, STEP = 0.001, 0.9, 0.999, 1e-08, 0.01, 10

VMEM_BIG = 56 * 1024 * 1024
TM = 256

MESH = pl.DeviceIdType.MESH
ANY = pl.BlockSpec(memory_space=pl.ANY)


def _call(body, name, grid, in_specs, out_specs, out_shape, scratch=(), vmem=None, comm=None):
    params = pltpu.CompilerParams(dimension_semantics=("arbitrary",) * len(grid), vmem_limit_bytes=vmem)
    n_in, n_out, n_scr = len(in_specs), len(out_shape), len(scratch)
    if comm is None:
        call = pl.pallas_call(body, name=name, grid=grid, in_specs=list(in_specs), out_specs=list(out_specs),
                              out_shape=list(out_shape), scratch_shapes=list(scratch), compiler_params=params)
        return lambda *args: (call(*args), ())
    nc = len(comm.inputs)

    def body_with_comm(*refs):
        ins, refs = refs[:n_in], refs[n_in:]
        c_ins, refs = refs[:nc], refs[nc:]
        outs, refs = refs[:n_out], refs[n_out:]
        c_outs, refs = refs[:nc], refs[nc:]
        scr, sems = refs[:n_scr], refs[n_scr:]
        first = pl.program_id(0) == 0
        last = pl.program_id(0) == grid[0] - 1
        for ax in range(1, len(grid)):
            first = jnp.logical_and(first, pl.program_id(ax) == 0)
            last = jnp.logical_and(last, pl.program_id(ax) == grid[ax] - 1)

        @pl.when(first)
        def _():
            comm.start(c_ins, c_outs, sems)

        body(*ins, *outs, *scr)
        early_relay = len(grid) == 1 and grid[0] >= 4
        if early_relay:
            @pl.when(pl.program_id(0) == (3 * grid[0]) // 4)
            def _():
                comm.relay(c_ins, c_outs, sems)

        @pl.when(last)
        def _():
            if not early_relay:
                comm.relay(c_ins, c_outs, sems)
            comm.wait(c_ins, c_outs, sems)

    call = pl.pallas_call(
        body_with_comm, name=name, grid=grid, in_specs=list(in_specs) + [ANY] * nc,
        out_specs=list(out_specs) + [ANY] * nc, out_shape=list(out_shape) + list(comm.out_shapes),
        scratch_shapes=list(scratch) + list(comm.sem_shapes), compiler_params=params)

    def run(*args):
        res = call(*args, *comm.inputs)
        return res[:n_out], res[n_out:]

    return run


def _rows(tm, n):
    return pl.BlockSpec((tm, n), lambda i: (i, 0))


def _const(shape):
    return pl.BlockSpec(shape, lambda i: (0,) * len(shape))


def _sds(shape, dtype):
    return jax.ShapeDtypeStruct(shape, dtype)


def _dot(a, b):
    return jnp.dot(a, b, preferred_element_type=F32)


def _dot_nt(a, b):
    return lax.dot_general(a, b, (((1,), (1,)), ((), ())), preferred_element_type=F32)


def _dot_tn(a, b):
    return lax.dot_general(a, b, (((0,), (0,)), ((), ())), preferred_element_type=F32)


def _colsum(v):
    return jnp.sum(v, axis=0, keepdims=True)


def _norm_fwd(h, g, sh, sc):
    r = lax.rsqrt(jnp.mean(h * h, axis=-1, keepdims=True) + EPS)
    xh = h * r
    n = xh * g
    return xh, r, n, n * (1.0 + sc) + sh


def _norm_bwd(du, xh, r, n, g, sc):
    dn = du * (1.0 + sc)
    dxh = dn * g
    dh = r * (dxh - xh * jnp.mean(dxh * xh, axis=-1, keepdims=True))
    return dh, _colsum(du), _colsum(du * n), _colsum(dn * xh)


def _load_once(pairs, sems):
    @pl.when(pl.program_id(0) == 0)
    def _():
        cps = [pltpu.make_async_copy(src, dst, sems.at[j]) for j, (src, dst) in enumerate(pairs)]
        for cp in cps:
            cp.start()
        for cp in cps:
            cp.wait()


def _zero_first(ref):
    @pl.when(pl.program_id(0) == 0)
    def _():
        ref[...] = jnp.zeros(ref.shape, ref.dtype)


def _row_chunks(hbm_refs, vmem_ref):
    pairs, row = [], 0
    for ref in hbm_refs:
        pairs.append((ref, vmem_ref.at[pl.ds(row, ref.shape[0]), :]))
        row += ref.shape[0]
    return pairs


def _loss_head(hh, tgt, g):
    r = lax.rsqrt(jnp.mean(hh * hh, axis=-1, keepdims=True) + EPS)
    xh = hh * r
    err = xh * g - tgt
    dy = err * (1.0 / D)
    dxh = dy * g
    dh = r * (dxh - xh * jnp.mean(dxh * xh, axis=-1, keepdims=True))
    return dh, _colsum(err * err), _colsum(dy * xh)


def _ffn_fwd(h, vec, wins, wout, name, comm=None, head=None):
    T = h.shape[0]
    nwin = len(wins)
    nhead = 0 if head is None else 2

    def body(h_ref, vec_ref, *rest):
        head_refs, rest = rest[:nhead], rest[nhead:]
        win_hbms, rest = rest[:nwin], rest[nwin:]
        (wout_hbm, ho_ref, u_ref, a_ref, b_ref, f_ref), rest = rest[:6], rest[6:]
        lacc_refs, (win_v, wout_v, sems) = rest[:nhead // 2], rest[nhead // 2:]
        _load_once(_row_chunks(win_hbms, win_v) + [(wout_hbm, wout_v)], sems)
        hh = h_ref[...]
        g, sh, sc, gt = vec_ref[0:1, :], vec_ref[1:2, :], vec_ref[2:3, :], vec_ref[3:4, :]
        _, _, _, u = _norm_fwd(hh, g, sh, sc)
        ub = u.astype(BF16)
        u_ref[...] = ub
        acc = None
        for j in range(FF // FC):
            lo, hi = j * FC, (j + 1) * FC
            a = _dot(ub, win_v[:, lo:hi])
            b = _dot(ub, win_v[:, FF + lo:FF + hi])
            a_ref[:, lo:hi] = a.astype(BF16)
            b_ref[:, lo:hi] = b.astype(BF16)
            s = (a * jax.nn.sigmoid(a) * b).astype(BF16)
            part = _dot(s, wout_v[lo:hi, :])
            acc = part if acc is None else acc + part
        f_ref[...] = acc.astype(BF16)
        ho = hh + 0.5 * gt * acc
        if head is None:
            ho_ref[...] = ho
        else:
            _zero_first(lacc_refs[0])
            dh, sq, dg = _loss_head(ho, head_refs[0][...], head_refs[1][0:1, :])
            ho_ref[...] = dh
            lacc_refs[0][0:1, :] += sq
            lacc_refs[0][1:2, :] += dg

    head_specs = [] if head is None else [_rows(TM, D), _const((8, D))]
    lacc_spec = [] if head is None else [_const((8, D))]
    lacc_shape = [] if head is None else [_sds((8, D), F32)]
    return _call(
        body, name, (T // TM,),
        [_rows(TM, D), _const((8, D))] + head_specs + [ANY] * (nwin + 1),
        [_rows(TM, D), _rows(TM, D), _rows(TM, FF), _rows(TM, FF), _rows(TM, D)] + lacc_spec,
        [_sds((T, D), F32), _sds((T, D), BF16), _sds((T, FF), BF16), _sds((T, FF), BF16), _sds((T, D), BF16)] + lacc_shape,
        scratch=[pltpu.VMEM((D, 2 * FF), BF16), pltpu.VMEM((FF, D), BF16), pltpu.SemaphoreType.DMA((nwin + 1,))],
        vmem=VMEM_BIG, comm=comm,
    )(h, vec, *([] if head is None else head), *wins, wout)


def _ffn_ab(h, vec, wins, name, comm=None):
    T = h.shape[0]
    nwin = len(wins)

    def body(h_ref, vec_ref, *rest):
        win_hbms, (u_ref, a_ref, b_ref, win_v, sems) = rest[:nwin], rest[nwin:]
        _load_once(_row_chunks(win_hbms, win_v), sems)
        g, sh, sc = vec_ref[0:1, :], vec_ref[1:2, :], vec_ref[2:3, :]
        _, _, _, u = _norm_fwd(h_ref[...], g, sh, sc)
        ub = u.astype(BF16)
        u_ref[...] = ub
        for j in range(FF // FC):
            lo, hi = j * FC, (j + 1) * FC
            a_ref[:, lo:hi] = _dot(ub, win_v[:, lo:hi]).astype(BF16)
            b_ref[:, lo:hi] = _dot(ub, win_v[:, FF + lo:FF + hi]).astype(BF16)

    return _call(
        body, name, (T // TM,),
        [_rows(TM, D), _const((8, D))] + [ANY] * nwin,
        [_rows(TM, D), _rows(TM, FF), _rows(TM, FF)],
        [_sds((T, D), BF16), _sds((T, FF), BF16), _sds((T, FF), BF16)],
        scratch=[pltpu.VMEM((D, 2 * FF), BF16), pltpu.SemaphoreType.DMA((nwin,))],
        vmem=VMEM_BIG, comm=comm,
    )(h, vec, *wins)


def _ffn_out(h, a, b, vec, wout, name, comm=None):
    T = h.shape[0]

    def body(h_ref, a_ref, b_ref, vec_ref, wout_hbm, ho_ref, f_ref, wout_v, sems):
        _load_once([(wout_hbm, wout_v)], sems)
        gt = vec_ref[3:4, :]
        acc = None
        for j in range(FF // FC):
            lo, hi = j * FC, (j + 1) * FC
            av = a_ref[:, lo:hi].astype(F32)
            s = (av * jax.nn.sigmoid(av) * b_ref[:, lo:hi].astype(F32)).astype(BF16)
            part = _dot(s, wout_v[lo:hi, :])
            acc = part if acc is None else acc + part
        f_ref[...] = acc.astype(BF16)
        ho_ref[...] = h_ref[...] + 0.5 * gt * acc

    return _call(
        body, name, (T // TM,),
        [_rows(TM, D), _rows(TM, FF), _rows(TM, FF), _const((8, D)), ANY],
        [_rows(TM, D), _rows(TM, D)],
        [_sds((T, D), F32), _sds((T, D), BF16)],
        scratch=[pltpu.VMEM((FF, D), BF16), pltpu.SemaphoreType.DMA((1,))],
        vmem=VMEM_BIG, comm=comm,
    )(h, a, b, vec, wout)


def _ffn_bwd(dh, h, a, b, f, vec, wins, wout, name, comm=None):
    T = h.shape[0]
    nwin = len(wins)

    def body(dh_ref, h_ref, a_ref, b_ref, f_ref, vec_ref, *rest):
        win_hbms, (wout_hbm, dhi_ref, dab_ref, s_ref, df_ref, acc_ref, win_v, wout_v, sems) = rest[:nwin], rest[nwin:]
        _load_once(_row_chunks(win_hbms, win_v) + [(wout_hbm, wout_v)], sems)
        _zero_first(acc_ref)
        g, sh, sc, gt = vec_ref[0:1, :], vec_ref[1:2, :], vec_ref[2:3, :], vec_ref[3:4, :]
        dho = dh_ref[...]
        df = (0.5 * gt * dho).astype(BF16)
        df_ref[...] = df
        dgt = _colsum(0.5 * dho * f_ref[...].astype(F32))
        du = None
        for j in range(FF // FC):
            lo, hi = j * FC, (j + 1) * FC
            av = a_ref[:, lo:hi].astype(F32)
            bv = b_ref[:, lo:hi].astype(F32)
            ds = _dot_nt(df, wout_v[lo:hi, :])
            sig = jax.nn.sigmoid(av)
            sa = av * sig
            s_ref[:, lo:hi] = (sa * bv).astype(BF16)
            da = (ds * bv * (sig * (1.0 + av * (1.0 - sig)))).astype(BF16)
            db = (ds * sa).astype(BF16)
            dab_ref[:, lo:hi] = da
            dab_ref[:, FF + lo:FF + hi] = db
            part = _dot_nt(da, win_v[:, lo:hi]) + _dot_nt(db, win_v[:, FF + lo:FF + hi])
            du = part if du is None else du + part
        xh, r, n, _ = _norm_fwd(h_ref[...], g, sh, sc)
        dhn, dsh, dsc, dg = _norm_bwd(du, xh, r, n, g, sc)
        dhi_ref[...] = dho + dhn
        acc_ref[0:1, :] += dsh
        acc_ref[1:2, :] += dsc
        acc_ref[2:3, :] += dg
        acc_ref[3:4, :] += dgt

    return _call(
        body, name, (T // TM,),
        [_rows(TM, D), _rows(TM, D), _rows(TM, FF), _rows(TM, FF), _rows(TM, D), _const((8, D))] + [ANY] * (nwin + 1),
        [_rows(TM, D), _rows(TM, 2 * FF), _rows(TM, FF), _rows(TM, D), _const((8, D))],
        [_sds((T, D), F32), _sds((T, 2 * FF), BF16), _sds((T, FF), BF16), _sds((T, D), BF16), _sds((8, D), F32)],
        scratch=[pltpu.VMEM((D, 2 * FF), BF16), pltpu.VMEM((FF, D), BF16), pltpu.SemaphoreType.DMA((nwin + 1,))],
        vmem=VMEM_BIG, comm=comm,
    )(dh, h, a, b, f, vec, *wins, wout)


def _wgrad(x, y, name, tk, tn, tt, out_dtype=BF16, comm=None):
    T, K = x.shape
    N = y.shape[1]
    nt = T // tt

    def body(x_ref, y_ref, o_ref, acc_ref):
        t = pl.program_id(2)
        part = _dot_tn(x_ref[...], y_ref[...])

        @pl.when(t == 0)
        def _():
            acc_ref[...] = part

        @pl.when(t > 0)
        def _():
            acc_ref[...] += part

        @pl.when(t == nt - 1)
        def _():
            o_ref[...] = acc_ref[...].astype(out_dtype)

    (out,), c_outs = _call(
        body, name, (K // tk, N // tn, nt),
        [pl.BlockSpec((tt, tk), lambda i, j, t: (t, i)), pl.BlockSpec((tt, tn), lambda i, j, t: (t, j))],
        [pl.BlockSpec((tk, tn), lambda i, j, t: (i, j))], [_sds((K, N), out_dtype)],
        scratch=[pltpu.VMEM((tk, tn), F32)], vmem=VMEM_BIG, comm=comm,
    )(x, y)
    return out, c_outs


def _wgrad_scatter(x, y, name, tt, comm=None):
    T, K = x.shape
    n = y.shape[1] // 4
    nt = T // tt
    assert nt >= 2, "a block's hand-over is added one grid step into the next block"
    half = K // 2
    nc = 0 if comm is None else len(comm.inputs)

    def body(chip_ref, x_ref, y_ref, *refs):
        c_ins, refs = refs[:nc], refs[nc:]
        recv_ref, refs = refs[0], refs[1:]
        c_outs, refs = refs[:nc], refs[nc:]
        acc_ref, keep_ref, give_ref, take_ref, local_sem, give_sems, take_sems, send_sems, recv_sems = refs[:9]
        j, t = pl.program_id(0), pl.program_id(1)
        px, py, pc = _place()

        def hand_over(jj):
            return pltpu.make_async_remote_copy(
                src_ref=give_ref.at[jj], dst_ref=take_ref.at[jj], send_sem=give_sems.at[jj], recv_sem=take_sems.at[jj],
                device_id=(px, py, 1 - pc), device_id_type=MESH)

        def send(jj):
            m = jj + 1
            return pltpu.make_async_remote_copy(
                src_ref=keep_ref.at[jj], dst_ref=recv_ref.at[m], send_sem=send_sems.at[jj], recv_sem=recv_sems.at[jj],
                device_id=_chip_peer(px, py, pc, m), device_id_type=MESH)

        def add_sibling(jj):
            hand_over(jj).wait_recv()
            keep_ref[jj] = (keep_ref[jj].astype(F32) + take_ref[jj].astype(F32)).astype(BF16)

        if comm is not None:
            @pl.when(jnp.logical_and(j == 0, t == 0))
            def _():
                comm.start(c_ins, c_outs, refs[9:])

        part = _dot_tn(x_ref[...], y_ref[...])

        @pl.when(t == 0)
        def _():
            acc_ref[...] = part

        @pl.when(t > 0)
        def _():
            acc_ref[...] += part

        for jj in range(3):
            @pl.when(jnp.logical_and(j == jj + 1, t == 0))
            def _():
                add_sibling(jj)
                send(jj).start()

        for jj in range(4):
            @pl.when(jnp.logical_and(j == jj, t == nt - 1))
            def _():
                keep_ref[jj] = acc_ref[pl.ds(pl.multiple_of(pc * half, 16), half), :].astype(BF16)
                give_ref[jj] = acc_ref[pl.ds(pl.multiple_of((1 - pc) * half, 16), half), :].astype(BF16)
                hand_over(jj).start()

        @pl.when(jnp.logical_and(j == 3, t == nt - 1))
        def _():
            add_sibling(3)
            own = pltpu.make_async_copy(keep_ref.at[3], recv_ref.at[0], local_sem.at[0])
            own.start()
            for jj in range(3):
                send(jj).wait_recv()
            for jj in range(3):
                send(jj).wait_send()
            for jj in range(4):
                hand_over(jj).wait_send()
            own.wait()
            if comm is not None:
                comm.relay(c_ins, c_outs, refs[9:])
                comm.wait(c_ins, c_outs, refs[9:])

    grid_spec = pltpu.PrefetchScalarGridSpec(
        num_scalar_prefetch=1, grid=(4, nt),
        in_specs=[pl.BlockSpec((tt, K), lambda j, t, chip: (t, 0)),
                  pl.BlockSpec((tt, n), lambda j, t, chip: (t, chip[0] ^ ((j + 1) & 3)))] + [ANY] * nc,
        out_specs=[ANY] * (1 + nc),
        scratch_shapes=[pltpu.VMEM((K, n), F32)] + [pltpu.VMEM((4, half, n), BF16)] * 3
        + [pltpu.SemaphoreType.DMA((1,))] + [pltpu.SemaphoreType.DMA((4,))] * 2 + [pltpu.SemaphoreType.DMA((3,))] * 2
        + ([] if comm is None else list(comm.sem_shapes)))
    px, py, _ = _place()
    res = pl.pallas_call(
        body, name=name, grid_spec=grid_spec,
        out_shape=[_sds((4, half, n), BF16)] + ([] if comm is None else list(comm.out_shapes)),
        compiler_params=pltpu.CompilerParams(dimension_semantics=("arbitrary", "arbitrary"), vmem_limit_bytes=VMEM_BIG),
    )((2 * px + py).astype(jnp.int32).reshape(1), x, y, *([] if comm is None else comm.inputs))
    return res[0], res[1:]


def _swap_halves(t):
    w = t.shape[1]
    lane = lax.broadcasted_iota(jnp.int32, t.shape, 1)
    return jnp.where(lane % HD < HD // 2, pltpu.roll(t, w - HD // 2, 1), pltpu.roll(t, HD // 2, 1))


def _rope(t, cos, sin_signed):
    c = jnp.tile(cos, (1, t.shape[1] // cos.shape[1]))
    s = jnp.tile(sin_signed, (1, t.shape[1] // sin_signed.shape[1]))
    return t * c + _swap_halves(t) * s


def _rope_bwd(dt, cos, sin_signed):
    c = jnp.tile(cos, (1, dt.shape[1] // cos.shape[1]))
    s = jnp.tile(sin_signed, (1, dt.shape[1] // sin_signed.shape[1]))
    return dt * c + _swap_halves(dt * s)


def _rm_spec(dil):
    return pl.BlockSpec((dil, TM // dil, GA), lambda i: (0, i, 0))


def _to_residues(t, dst_ref, scr_ref, dil):
    if dil == 1:
        dst_ref[0] = t.astype(dst_ref.dtype)
        return
    for j in range(GA // LANES):
        scr_ref[j] = t[:, j * LANES:(j + 1) * LANES]
    for r in range(dil):
        for j in range(GA // LANES):
            rows = scr_ref.at[j][pl.ds(r, TM // dil, stride=dil), :]
            dst_ref[r, :, j * LANES:(j + 1) * LANES] = rows.astype(dst_ref.dtype)


def _from_residues(src_ref, scr_ref, dil):
    if dil == 1:
        return src_ref[0].astype(F32)
    for r in range(dil):
        for j in range(GA // LANES):
            scr_ref.at[j][pl.ds(r, TM // dil, stride=dil), :] = src_ref[r, :, j * LANES:(j + 1) * LANES].astype(F32)
    return jnp.concatenate([scr_ref[j] for j in range(GA // LANES)], axis=1)


def _mix_proj(h, vec, win, cos, sin, comm=None):
    T = h.shape[0]

    def body(h_ref, vec_ref, win_hbm, cos_ref, sin_ref, u_ref, p_ref, *rest):
        qkv_refs, gates_ref, win_v, scr_ref, sems = rest[:3 * NG], rest[3 * NG], rest[3 * NG + 1], rest[3 * NG + 2], rest[3 * NG + 3]
        _load_once([(win_hbm, win_v)], sems)
        g, sh, sc = vec_ref[0:1, :], vec_ref[1:2, :], vec_ref[2:3, :]
        _, _, _, u = _norm_fwd(h_ref[...], g, sh, sc)
        ub = u.astype(BF16)
        u_ref[...] = ub
        p_ref[...] = _dot(ub, win_v[:, 0:PW])
        cos_t, sin_t = cos_ref[...], sin_ref[...]
        for j in range(3 * NG):
            col = PW + j * GA
            t = _dot(ub, win_v[:, col:col + GA])
            if j < 2 * NG:
                t = _rope(t, cos_t, sin_t)
            _to_residues(t, qkv_refs[j], scr_ref, DIL[j % NG])
        for j in range(GW // 512):
            col = PW + 3 * NG * GA + j * 512
            gates_ref[:, j * 512:(j + 1) * 512] = jax.nn.sigmoid(_dot(ub, win_v[:, col:col + 512])).astype(BF16)

    outs, c_outs = _call(
        body, "mix_proj", (T // TM,),
        [_rows(TM, D), _const((8, D)), ANY, _rows(TM, 128), _rows(TM, 128)],
        [_rows(TM, D), _rows(TM, PW)] + [_rm_spec(d) for d in DIL] * 3 + [_rows(TM, GW)],
        [_sds((T, D), BF16), _sds((T, PW), F32)] + [_sds((d, T // d, GA), BF16) for d in DIL] * 3 + [_sds((T, GW), BF16)],
        scratch=[pltpu.VMEM((D, INW), BF16), pltpu.VMEM((GA // LANES, TM, LANES), F32), pltpu.SemaphoreType.DMA((1,))],
        vmem=VMEM_BIG, comm=comm,
    )(h, vec, win, cos, sin)
    return (outs[0], outs[1], outs[2:2 + NG], outs[2 + NG:2 + 2 * NG], outs[2 + 2 * NG:2 + 3 * NG], outs[2 + 3 * NG]), c_outs


def _head_masks():
    lane_head = lax.broadcasted_iota(jnp.int32, (BLK, GA), 1) // HD
    return [lane_head == hd for hd in range(NH)]


def _expand_heads(t, hm):
    return jnp.concatenate([jnp.where(m, t, jnp.zeros_like(t)) for m in hm], axis=0)


def _collapse_heads(tb, hm):
    out = None
    for hd, m in enumerate(hm):
        part = jnp.where(m, tb[hd * BLK:(hd + 1) * BLK, :], 0.0)
        out = part if out is None else out + part
    return out


def _head_rows(t):
    return jnp.concatenate([t[:, hd * HD:hd * HD + 1] for hd in range(NH)], axis=0)


def _band(has_prev):
    a = lax.broadcasted_iota(jnp.int32, (NH * BLK, 2 * BLK), 0) & (BLK - 1)
    c = lax.broadcasted_iota(jnp.int32, (NH * BLK, 2 * BLK), 1)
    return jnp.logical_and(c >= jnp.where(has_prev, a, BLK), c <= a + BLK)


def _attn_specs(nbt):
    cur = pl.BlockSpec((2 * BLK, GA), lambda i: (i, 0))
    prev = pl.BlockSpec((BLK, GA), lambda i: (jnp.maximum(2 * i - 1, 0), 0))
    nxt = pl.BlockSpec((BLK, GA), lambda i: (jnp.minimum(2 * i + 2, nbt - 1), 0))
    return cur, prev, nxt


def _attn_fwd(q, k, v, nb, name, comm=None):
    T = q.shape[0]
    nbt = T // BLK
    lo, hi = slice(0, BLK), slice(BLK, 2 * BLK)

    def block(qv, kcat, vcat, has_prev, hm):
        s = jnp.where(_band(has_prev), _dot_nt(_expand_heads(qv, hm), kcat) * SCALE, NEG)
        mx = jnp.max(s, axis=-1, keepdims=True)
        e = jnp.exp(s - mx)
        l = jnp.sum(e, axis=-1, keepdims=True)
        ob = _dot((e * (1.0 / l)).astype(BF16), vcat)
        return _collapse_heads(ob, hm), _collapse_heads(jnp.broadcast_to(mx + jnp.log(l), (NH * BLK, GA)), hm)

    def body(q_ref, k_ref, kp_ref, v_ref, vp_ref, o_ref, lse_ref):
        b0 = 2 * pl.program_id(0)
        hm = _head_masks()
        k_first = jnp.concatenate([kp_ref[...], k_ref[lo, :]], axis=0)
        v_first = jnp.concatenate([vp_ref[...], v_ref[lo, :]], axis=0)
        o_ref[lo, :], lse_ref[lo, :] = block(q_ref[lo, :], k_first, v_first, (b0 & (nb - 1)) != 0, hm)
        o_ref[hi, :], lse_ref[hi, :] = block(q_ref[hi, :], k_ref[...], v_ref[...], ((b0 + 1) & (nb - 1)) != 0, hm)

    cur, prev, _ = _attn_specs(nbt)
    return _call(body, name, (nbt // 2,), [cur, cur, prev, cur, prev], [cur, cur],
                 [_sds((T, GA), F32), _sds((T, GA), F32)], comm=comm)(q, k, k, v, v)


def _attn_bwd(q, k, v, do, lse, e, nb, name, comm=None):
    T = q.shape[0]
    nbt = T // BLK

    lo, hi = slice(0, BLK), slice(BLK, 2 * BLK)

    def probs_and_ds(qb, dob, kcat, vcat, lsev, ev, valid):
        p = jnp.where(valid, jnp.exp(_dot_nt(qb, kcat) * SCALE - _head_rows(lsev)), 0.0)
        return p, (p * (_dot_nt(dob, vcat) + _head_rows(ev))).astype(BF16)

    def body(q_ref, k_ref, v_ref, do_ref, lse_ref, e_ref, kp_ref, vp_ref, qn_ref, don_ref, lsen_ref, en_ref,
             dq_ref, dk_ref, dv_ref):
        b0 = 2 * pl.program_id(0)
        hm = _head_masks()
        q1, q2, q3 = _expand_heads(q_ref[lo, :], hm), _expand_heads(q_ref[hi, :], hm), _expand_heads(qn_ref[...], hm)
        do1, do2, do3 = (_expand_heads(do_ref[lo, :], hm), _expand_heads(do_ref[hi, :], hm),
                         _expand_heads(don_ref[...], hm))
        k1 = jnp.concatenate([kp_ref[...], k_ref[lo, :]], axis=0)
        v1 = jnp.concatenate([vp_ref[...], v_ref[lo, :]], axis=0)
        k2, v2 = k_ref[...], v_ref[...]
        p1, ds1 = probs_and_ds(q1, do1, k1, v1, lse_ref[lo, :], e_ref[lo, :], _band((b0 & (nb - 1)) != 0))
        p2, ds2 = probs_and_ds(q2, do2, k2, v2, lse_ref[hi, :], e_ref[hi, :], _band(((b0 + 1) & (nb - 1)) != 0))
        dq_ref[lo, :] = _collapse_heads(_dot(ds1, k1) * SCALE, hm)
        dq_ref[hi, :] = _collapse_heads(_dot(ds2, k2) * SCALE, hm)
        a = lax.broadcasted_iota(jnp.int32, (NH * BLK, BLK), 0) & (BLK - 1)
        c = lax.broadcasted_iota(jnp.int32, (NH * BLK, BLK), 1)
        valid3 = jnp.logical_and(c >= a, ((b0 + 2) & (nb - 1)) != 0)
        p3, ds3 = probs_and_ds(q3, do3, k_ref[hi, :], v_ref[hi, :], lsen_ref[...], en_ref[...], valid3)
        q12, q23 = jnp.concatenate([q1, q2], axis=0), jnp.concatenate([q2, q3], axis=0)
        do12, do23 = jnp.concatenate([do1, do2], axis=0), jnp.concatenate([do2, do3], axis=0)
        dk_ref[lo, :] = _dot_tn(jnp.concatenate([ds1[:, BLK:], ds2[:, :BLK]], axis=0), q12) * SCALE
        dk_ref[hi, :] = _dot_tn(jnp.concatenate([ds2[:, BLK:], ds3], axis=0), q23) * SCALE
        pb1, pb2, pb3 = p1.astype(BF16), p2.astype(BF16), p3.astype(BF16)
        dv_ref[lo, :] = _dot_tn(jnp.concatenate([pb1[:, BLK:], pb2[:, :BLK]], axis=0), do12).astype(BF16)
        dv_ref[hi, :] = _dot_tn(jnp.concatenate([pb2[:, BLK:], pb3], axis=0), do23).astype(BF16)

    cur, prev, nxt = _attn_specs(nbt)
    return _call(body, name, (nbt // 2,), [cur] * 6 + [prev, prev] + [nxt] * 4, [cur, cur, cur],
                 [_sds((T, GA), F32), _sds((T, GA), F32), _sds((T, GA), BF16)],
                 comm=comm)(q, k, v, do, lse, e, k, v, q, do, lse, e)


def _flat(t):
    return t.reshape(t.shape[0] * t.shape[1], t.shape[2])


def _by_residue(t, dil):
    return t.reshape(dil, t.shape[0] // dil, t.shape[1])


def _pool_consts(shape, row0):
    lane = lax.broadcasted_iota(jnp.int32, shape, 1)
    t = lax.broadcasted_iota(jnp.int32, shape, 0) + row0
    grp = lane // (PW // len(POOL_WINDOWS))
    win = jnp.where(grp == 0, POOL_WINDOWS[0], jnp.where(grp == 1, POOL_WINDOWS[1],
                    jnp.where(grp == 2, POOL_WINDOWS[2], POOL_WINDOWS[3])))
    cnt = jnp.minimum(t + 1, win).astype(F32)
    return grp, cnt


def _window_sums(ext_ref, base, step, tm):
    outs, run = [], None
    for j in range(POOL_WINDOWS[-1]):
        sl = ext_ref[pl.ds(base + step * j, tm), :]
        run = sl if run is None else run + sl
        if j + 1 in POOL_WINDOWS:
            outs.append(run)
    return outs


def _select_group(grp, vals):
    return jnp.where(grp == 0, vals[0], jnp.where(grp == 1, vals[1], jnp.where(grp == 2, vals[2], vals[3])))


def _pool_d(pc_ref, pp_ref, ext_ref, i, tm):
    ext_ref[0:HALO, :] = jnp.where(i > 0, pp_ref[tm - HALO:tm, :], 0.0)
    ext_ref[HALO:HALO + tm, :] = pc_ref[...]
    grp, cnt = _pool_consts((tm, PW), i * tm)
    sums = _window_sums(ext_ref, HALO, -1, tm)
    return _select_group(grp, sums) / cnt - pc_ref[...]


def _group_weights(ls):
    mx = jnp.maximum(jnp.maximum(ls[0], ls[1]), ls[2])
    es = [jnp.exp(l - mx) for l in ls]
    inv = 1.0 / (es[0] + es[1] + es[2])
    return [e * inv for e in es]


def _mix_merge(h, vec, p, os, lses, gates, wp_bd, pscale, wpb, wab, wout):
    T = h.shape[0]

    def body(h_ref, vec_ref, pc_ref, pp_ref, o0, o1, o2, l0, l1, l2, gates_ref, wp_ref, ps_ref, wpb_ref, wab_ref, wout_ref,
             ho_ref, yp_ref, ya_ref, mg_ref, mo_ref, d_ref, ext_ref, scr_ref):
        i = pl.program_id(0)
        gt = vec_ref[3:4, :]
        d = _pool_d(pc_ref, pp_ref, ext_ref, i, TM).astype(BF16)
        d_ref[...] = d
        ypool = (_dot(d, wp_ref[...]) * ps_ref[0:1, :]).astype(BF16)
        yp_ref[...] = ypool
        w = _group_weights([_from_residues(r, scr_ref, dl) for r, dl in zip((l0, l1, l2), DIL)])
        yattn = None
        for wg, o_ref, dl in zip(w, (o0, o1, o2), DIL):
            part = wg * _from_residues(o_ref, scr_ref, dl)
            yattn = part if yattn is None else yattn + part
        yattn = yattn.astype(BF16)
        ya_ref[...] = yattn
        merged = (gates_ref[:, 0:D].astype(F32) * _dot(ypool, wpb_ref[...])
                  + gates_ref[:, D:GW].astype(F32) * _dot(yattn, wab_ref[...])).astype(BF16)
        mg_ref[...] = merged
        mo = _dot(merged, wout_ref[...])
        mo_ref[...] = mo.astype(BF16)
        ho_ref[...] = h_ref[...] + gt * mo

    prev = pl.BlockSpec((TM, PW), lambda i: (jnp.maximum(i - 1, 0), 0))
    return _call(
        body, "mix_merge", (T // TM,),
        [_rows(TM, D), _const((8, D)), _rows(TM, PW), prev] + [_rm_spec(dl) for dl in DIL] * 2 + [_rows(TM, GW), _const((PW, PW)),
         _const((8, PW)), _const((PW, D)), _const((GA, D)), _const((D, D))],
        [_rows(TM, D), _rows(TM, PW), _rows(TM, GA), _rows(TM, D), _rows(TM, D), _rows(TM, PW)],
        [_sds((T, D), F32), _sds((T, PW), BF16), _sds((T, GA), BF16), _sds((T, D), BF16), _sds((T, D), BF16), _sds((T, PW), BF16)],
        scratch=[pltpu.VMEM((TM + HALO, PW), F32), pltpu.VMEM((GA // LANES, TM, LANES), F32)],
        vmem=VMEM_BIG,
    )(h, vec, p, p, *os, *lses, gates, wp_bd, pscale, wpb, wab, wout)[0]


def _mix_bwd_a(dh, vec, mixout, merged, gates, ypool, yattn, dpool, os, lses, wp_bd, pscale, wpb, wab, wout, ones_bd,
               comm=None):
    T = dh.shape[0]
    nt = T // TM

    def body(dh_ref, vec_ref, mo_ref, mg_ref, gates_ref, yp_ref, ya_ref, d_ref, o0, o1, o2, l0, l1, l2,
             wp_ref, ps_ref, wpb_ref, wab_ref, wout_ref, ones_ref,
             dgates_ref, do0, do1, do2, e0, e1, e2, dd_ref, acc_ref, acc2_ref, g_out_ref, g_pb_ref, g_ab_ref, g_pool_ref,
             scr_ref, a_out, a_pb, a_ab, a_pool):
        _zero_first(acc_ref)
        _zero_first(acc2_ref)
        for a_ref in (a_out, a_pb, a_ab, a_pool):
            _zero_first(a_ref)
        gt = vec_ref[3:4, :]
        dho = dh_ref[...]
        acc_ref[3:4, :] += _colsum(dho * mo_ref[...].astype(F32))
        dmo = (gt * dho).astype(BF16)
        a_out[...] += _dot_tn(mg_ref[...], dmo)
        dmerged = _dot_nt(dmo, wout_ref[...])
        gp = gates_ref[:, 0:D].astype(F32)
        ga = gates_ref[:, D:GW].astype(F32)
        bp = _dot(yp_ref[...], wpb_ref[...])
        ba = _dot(ya_ref[...], wab_ref[...])
        dgates_ref[:, 0:D] = (dmerged * bp * gp * (1.0 - gp)).astype(BF16)
        dgates_ref[:, D:GW] = (dmerged * ba * ga * (1.0 - ga)).astype(BF16)
        dbp = (dmerged * gp).astype(BF16)
        dba = (dmerged * ga).astype(BF16)
        a_pb[...] += _dot_tn(yp_ref[...], dbp)
        a_ab[...] += _dot_tn(ya_ref[...], dba)
        dypool = _dot_nt(dbp, wpb_ref[...])
        ypre = _dot(d_ref[...], wp_ref[...])
        acc2_ref[0:1, :] += _colsum(dypool * ypre)
        dyp = (dypool * ps_ref[0:1, :]).astype(BF16)
        a_pool[...] += _dot_tn(d_ref[...], dyp)
        dd_ref[...] = _dot_nt(dyp, wp_ref[...])
        dya = _dot_nt(dba, wab_ref[...])
        w = _group_weights([_from_residues(r, scr_ref, dl) for r, dl in zip((l0, l1, l2), DIL)])
        ya = None
        for wg, o_ref, dl in zip(w, (o0, o1, o2), DIL):
            part = wg * _from_residues(o_ref, scr_ref, dl)
            ya = part if ya is None else ya + part
        prod = dya * ya
        hi = prod.astype(BF16)
        lo = (prod - hi.astype(F32)).astype(BF16)
        tot = _dot(hi, ones_ref[...]) + _dot(lo, ones_ref[...])
        for wg, do_ref, e_ref, dl in zip(w, (do0, do1, do2), (e0, e1, e2), DIL):
            _to_residues(wg * dya, do_ref, scr_ref, dl)
            _to_residues(-wg * tot, e_ref, scr_ref, dl)

        @pl.when(pl.program_id(0) == nt - 1)
        def _():
            g_out_ref[...] = a_out[...].astype(BF16)
            g_pb_ref[...] = a_pb[...].astype(BF16)
            g_ab_ref[...] = a_ab[...].astype(BF16)
            g_pool_ref[...] = a_pool[...]

    return _call(
        body, "mix_bwd_a", (nt,),
        [_rows(TM, D), _const((8, D)), _rows(TM, D), _rows(TM, D), _rows(TM, GW), _rows(TM, PW), _rows(TM, GA), _rows(TM, PW)]
        + [_rm_spec(dl) for dl in DIL] * 2
        + [_const((PW, PW)), _const((8, PW)), _const((PW, D)), _const((GA, D)), _const((D, D)), _const((GA, GA))],
        [_rows(TM, GW)] + [_rm_spec(dl) for dl in DIL] * 2 + [_rows(TM, PW), _const((8, D)), _const((8, PW))]
        + [_const((D, D)), _const((PW, D)), _const((GA, D)), _const((PW, PW))],
        [_sds((T, GW), BF16)] + [_sds((dl, T // dl, GA), BF16) for dl in DIL]
        + [_sds((dl, T // dl, GA), F32) for dl in DIL] + [_sds((T, PW), F32), _sds((8, D), F32), _sds((8, PW), F32)]
        + [_sds((D, D), BF16), _sds((PW, D), BF16), _sds((GA, D), BF16), _sds((PW, PW), F32)],
        scratch=[pltpu.VMEM((GA // LANES, TM, LANES), F32), pltpu.VMEM((D, D), F32), pltpu.VMEM((PW, D), F32),
                 pltpu.VMEM((GA, D), F32), pltpu.VMEM((PW, PW), F32)],
        vmem=VMEM_BIG, comm=comm,
    )(dh, vec, mixout, merged, gates, ypool, yattn, dpool, *os, *lses, wp_bd, pscale, wpb, wab, wout, ones_bd)


def _mix_bwd_b(dh, h, vec, dd, dqs, dks, dvs, dgates, cos, sin, win):
    T = h.shape[0]
    nt = T // TM

    def body(dh_ref, h_ref, vec_ref, ddc_ref, ddn_ref, *rest):
        qk_refs, dv_refs = rest[:2 * NG], rest[2 * NG:3 * NG]
        dgates_ref, cos_ref, sin_ref, win_hbm, dhi_ref, dproj_ref, acc_ref, win_v, ext_ref, scr_ref, sems = rest[3 * NG:]
        i = pl.program_id(0)
        _load_once([(win_hbm, win_v)], sems)
        _zero_first(acc_ref)
        g, sh, sc = vec_ref[0:1, :], vec_ref[1:2, :], vec_ref[2:3, :]
        grp, cnt = _pool_consts((TM, PW), i * TM)
        _, cnt_n = _pool_consts((HALO, PW), (i + 1) * TM)
        ext_ref[0:TM, :] = ddc_ref[...] / cnt
        ext_ref[TM:TM + HALO, :] = jnp.where(i < nt - 1, ddn_ref[0:HALO, :] / cnt_n, 0.0)
        dp = _select_group(grp, _window_sums(ext_ref, 0, 1, TM)) - ddc_ref[...]
        dproj_ref[:, 0:PW] = dp.astype(BF16)
        cos_t, sin_t = cos_ref[...], sin_ref[...]
        for j in range(2 * NG):
            col = PW + j * GA
            dt = _from_residues(qk_refs[j], scr_ref, DIL[j % NG])
            dproj_ref[:, col:col + GA] = _rope_bwd(dt, cos_t, sin_t).astype(BF16)
        for j in range(NG):
            col = PW + (2 * NG + j) * GA
            dproj_ref[:, col:col + GA] = _from_residues(dv_refs[j], scr_ref, DIL[j]).astype(BF16)
        dproj_ref[:, PW + 3 * NG * GA:INW] = dgates_ref[...]
        du = None
        for j in range(INW // 512):
            part = _dot_nt(dproj_ref[:, j * 512:(j + 1) * 512], win_v[:, j * 512:(j + 1) * 512])
            du = part if du is None else du + part
        xh, r, n, _ = _norm_fwd(h_ref[...], g, sh, sc)
        dhn, dsh, dsc, dg = _norm_bwd(du, xh, r, n, g, sc)
        dhi_ref[...] = dh_ref[...] + dhn
        acc_ref[0:1, :] += dsh
        acc_ref[1:2, :] += dsc
        acc_ref[2:3, :] += dg

    nxt = pl.BlockSpec((TM, PW), lambda i: (jnp.minimum(i + 1, nt - 1), 0))
    return _call(
        body, "mix_bwd_b", (nt,),
        [_rows(TM, D), _rows(TM, D), _const((8, D)), _rows(TM, PW), nxt] + [_rm_spec(dl) for dl in DIL] * 3
        + [_rows(TM, GW), _rows(TM, 128), _rows(TM, 128), ANY],
        [_rows(TM, D), _rows(TM, INW), _const((8, D))],
        [_sds((T, D), F32), _sds((T, INW), BF16), _sds((8, D), F32)],
        scratch=[pltpu.VMEM((D, INW), BF16), pltpu.VMEM((TM + HALO, PW), F32), pltpu.VMEM((GA // LANES, TM, LANES), F32),
                 pltpu.SemaphoreType.DMA((1,))],
        vmem=VMEM_BIG,
    )(dh, h, vec, dd, dd, *dqs, *dks, *dvs, dgates, cos, sin, win)[0]


def _ada_fwd(c_all, w_shard, b_shard):
    n = w_shard.shape[1]

    def body(c_ref, w_ref, b_ref, o_ref):
        cv = c_ref[...]
        cond = (cv * jax.nn.sigmoid(cv)).astype(BF16)
        o_ref[...] = _dot(cond, w_ref[...].astype(BF16)) + b_ref[...]

    tn = n // 3
    return pl.pallas_call(
        body, name="ada_fwd", grid=(3,),
        in_specs=[pl.BlockSpec((8, D), lambda j: (0, 0)), pl.BlockSpec((D, tn), lambda j: (0, j)), pl.BlockSpec((1, tn), lambda j: (0, j))],
        out_specs=pl.BlockSpec((8, tn), lambda j: (0, j)), out_shape=_sds((8, n), F32),
        compiler_params=pltpu.CompilerParams(dimension_semantics=("arbitrary",)),
    )(c_all, w_shard, b_shard)


def _ada_bwd(c_all, dmod_shard):
    n = dmod_shard.shape[1]

    def body(c_ref, d_ref, o_ref):
        cv = c_ref[...]
        cond = (cv * jax.nn.sigmoid(cv)).astype(BF16)
        o_ref[...] = _dot_tn(cond, d_ref[...].astype(BF16))

    tn = n // 3
    return pl.pallas_call(
        body, name="ada_bwd", grid=(3,),
        in_specs=[pl.BlockSpec((8, D), lambda j: (0, 0)), pl.BlockSpec((8, tn), lambda j: (0, j))],
        out_specs=pl.BlockSpec((D, tn), lambda j: (0, j)), out_shape=_sds((D, n), F32),
        compiler_params=pltpu.CompilerParams(dimension_semantics=("arbitrary",)),
    )(c_all, dmod_shard)


def _adam_math(w, g, m, v):
    m2 = B1 * m + (1.0 - B1) * g
    v2 = B2 * v + (1.0 - B2) * (g * g)
    m_hat = m2 / (1.0 - B1 ** STEP)
    v_hat = v2 / (1.0 - B2 ** STEP)
    delta = -LR * (m_hat / (jnp.sqrt(v_hat) + AEPS) + WD * w)
    return delta, m2, v2


def _adam(w, m, v, parts, name, comm=None):
    R, C = w.shape
    tr = R
    for cand in (128, 64, 32, 16, 8):
        if R % cand == 0:
            tr = cand
            break
    np_ = len(parts)

    def body(w_ref, m_ref, v_ref, *rest):
        p_refs, (g_ref, d_ref, m2_ref, v2_ref) = rest[:np_], rest[np_:]
        g = p_refs[0][...]
        for pr in p_refs[1:]:
            g = g + pr[...]
        delta, m2, v2 = _adam_math(w_ref[...], g, m_ref[...], v_ref[...])
        g_ref[...] = g
        d_ref[...] = delta
        m2_ref[...] = m2
        v2_ref[...] = v2

    spec = pl.BlockSpec((tr, C), lambda i: (i, 0))
    return _call(body, name, (R // tr,), [spec] * (3 + np_), [spec] * 4, [_sds((R, C), F32)] * 4,
                 vmem=VMEM_BIG, comm=comm)(w, m, v, *parts)


def _adam_halves(w, m, v, mine, other, name):
    R, C = w.shape
    tr = 128
    nh = R // 2 // tr

    def body(c_ref, w_ref, m_ref, v_ref, mine_ref, other_ref, g_ref, d_ref, m2_ref, v2_ref):
        i = pl.program_id(0)
        in_mine = jnp.logical_and(i >= c_ref[0] * nh, i < (c_ref[0] + 1) * nh)
        g = jnp.where(in_mine, mine_ref[...], other_ref[...])
        delta, m2, v2 = _adam_math(w_ref[...], g, m_ref[...], v_ref[...])
        g_ref[...] = g
        d_ref[...] = delta
        m2_ref[...] = m2
        v2_ref[...] = v2

    spec = pl.BlockSpec((tr, C), lambda i, c: (i, 0))
    grid_spec = pltpu.PrefetchScalarGridSpec(
        num_scalar_prefetch=1, grid=(R // tr,),
        in_specs=[spec] * 3 + [pl.BlockSpec((tr, C), lambda i, c: (jnp.clip(i - c[0] * nh, 0, nh - 1), 0)),
                               pl.BlockSpec((tr, C), lambda i, c: (jnp.clip(i - (1 - c[0]) * nh, 0, nh - 1), 0))],
        out_specs=[spec] * 4)
    return pl.pallas_call(
        body, name=name, grid_spec=grid_spec, out_shape=[_sds((R, C), F32)] * 4,
        compiler_params=pltpu.CompilerParams(dimension_semantics=("arbitrary",), vmem_limit_bytes=VMEM_BIG),
    )(lax.axis_index("c").astype(jnp.int32).reshape(1), w, m, v, mine, other)


def _adam_small(ws, ms, vs, gathered):
    n = len(ws)
    sizes = [a.shape[1] for a in ws]

    def total(ga_ref, off, size):
        g = ga_ref[0, :, off:off + size]
        for dev in range(1, 8):
            g = g + ga_ref[dev, :, off:off + size]
        return g

    def body(*refs):
        w_refs, m_refs, v_refs, ga_ref, outs = refs[:n], refs[n:2 * n], refs[2 * n:3 * n], refs[3 * n], refs[3 * n + 1:]
        off = 0
        for j, size in enumerate(sizes):
            g = total(ga_ref, off, size)
            delta, m2, v2 = _adam_math(w_refs[j][...], g, m_refs[j][...], v_refs[j][...])
            for ref, val in zip(outs[4 * j:4 * j + 4], (g, delta, m2, v2)):
                ref[...] = val
            off += size
        outs[4 * n][...] = total(ga_ref, off, 128)

    res = pl.pallas_call(
        body, name="adam_small",
        out_shape=[_sds((1, size), F32) for size in sizes for _ in range(4)] + [_sds((1, 128), F32)],
    )(*ws, *ms, *vs, gathered)
    return [res[4 * j:4 * j + 4] for j in range(n)], res[4 * n]


def _sum4(blocks, name):
    _, R, C = blocks.shape
    tr = R
    for cand in (256, 128, 64, 32, 16):
        if R % cand == 0:
            tr = cand
            break

    def body(r_ref, out_ref):
        out_ref[...] = ((r_ref[0].astype(F32) + r_ref[1].astype(F32)) + r_ref[2].astype(F32)) + r_ref[3].astype(F32)

    return pl.pallas_call(
        body, name=name, grid=(R // tr,),
        in_specs=[pl.BlockSpec((4, tr, C), lambda i: (0, i, 0))],
        out_specs=pl.BlockSpec((tr, C), lambda i: (i, 0)), out_shape=_sds((R, C), F32),
        compiler_params=pltpu.CompilerParams(dimension_semantics=("arbitrary",)),
    )(blocks)


def _place():
    return lax.axis_index("x"), lax.axis_index("y"), lax.axis_index("c")


def _chip_peer(x, y, c, m):
    return (x ^ (m >> 1), y ^ (m & 1), c)


def _shard_ref(ref, axis, k, n):
    start = pl.multiple_of(k * n, 128 if axis == 1 else 16)
    return ref.at[:, pl.ds(start, n)] if axis == 1 else ref.at[pl.ds(start, n), :]


def _half_rows(ref, axis, k, n, hc):
    if axis == 1:
        half = ref.shape[0] // 2
        return ref.at[pl.ds(pl.multiple_of(hc * half, 16), half), pl.ds(pl.multiple_of(k * n, 128), n)]
    half = n // 2
    return ref.at[pl.ds(pl.multiple_of(k * n + hc * half, 16), half), :]


class _GatherPlan:
    def __init__(self, shards, axes):
        self.inputs, self.axes, nw = list(shards), list(axes), len(shards)
        self.out_shapes = [_sds((s.shape[0] * (4 if ax == 0 else 1), s.shape[1] * (4 if ax == 1 else 1)), BF16)
                           for s, ax in zip(shards, axes)]
        self.sem_shapes = [pltpu.SemaphoreType.DMA((nw,))] + [pltpu.SemaphoreType.DMA((nw, 3))] * 4

    def _copies(self, ins, outs, sems):
        local_sems, send_sems, recv_sems, pass_sems, got_sems = sems
        x, y, c = _place()
        k = 2 * x + y
        local, sends, arrivals, passes, handed = [], [], [], [], []
        for j, ax in enumerate(self.axes):
            n = ins[j].shape[ax]
            half = ins[j].shape[0] // 2
            local.append(pltpu.make_async_copy(ins[j], _shard_ref(outs[j], ax, k, n), local_sems.at[j]))
            my_half = ins[j].at[pl.ds(pl.multiple_of(c * half, 16), half), :]
            for m in range(1, 4):
                sends.append(pltpu.make_async_remote_copy(
                    src_ref=my_half, dst_ref=_half_rows(outs[j], ax, k, n, c), send_sem=send_sems.at[j, m - 1],
                    recv_sem=recv_sems.at[j, m - 1], device_id=_chip_peer(x, y, c, m), device_id_type=MESH))
                theirs = _half_rows(outs[j], ax, k ^ m, n, c)
                arrivals.append(pltpu.make_async_remote_copy(
                    src_ref=my_half, dst_ref=theirs, send_sem=send_sems.at[j, m - 1], recv_sem=recv_sems.at[j, m - 1],
                    device_id=(x, y, c), device_id_type=MESH))
                passes.append(pltpu.make_async_remote_copy(
                    src_ref=theirs, dst_ref=theirs, send_sem=pass_sems.at[j, m - 1], recv_sem=got_sems.at[j, m - 1],
                    device_id=(x, y, 1 - c), device_id_type=MESH))
                other = _half_rows(outs[j], ax, k ^ m, n, 1 - c)
                handed.append(pltpu.make_async_remote_copy(
                    src_ref=other, dst_ref=other, send_sem=pass_sems.at[j, m - 1], recv_sem=got_sems.at[j, m - 1],
                    device_id=(x, y, c), device_id_type=MESH))
        return local, sends, arrivals, passes, handed

    def start(self, ins, outs, sems):
        local, sends, _, _, _ = self._copies(ins, outs, sems)
        for cp in local + sends:
            cp.start()

    def relay(self, ins, outs, sems):
        _, _, arrivals, passes, _ = self._copies(ins, outs, sems)
        for arrived, onward in zip(arrivals, passes):
            arrived.wait_recv()
            onward.start()

    def wait(self, ins, outs, sems):
        local, sends, _, passes, handed = self._copies(ins, outs, sems)
        for cp in handed:
            cp.wait_recv()
        for cp in sends + passes:
            cp.wait_send()
        for cp in local:
            cp.wait()


class _ScatterPlan:
    def __init__(self, grads, axes):
        self.inputs, self.axes, nw = list(grads), list(axes), len(grads)
        self.shard_shapes = [(g.shape[0] // (4 if ax == 0 else 1), g.shape[1] // (4 if ax == 1 else 1))
                             for g, ax in zip(grads, axes)]
        self.out_shapes = [_sds((4,) + s, BF16) for s in self.shard_shapes]
        self.sem_shapes = [pltpu.SemaphoreType.DMA((nw,)), pltpu.SemaphoreType.DMA((nw, 3)), pltpu.SemaphoreType.DMA((nw, 3))]

    def _copies(self, ins, outs, sems):
        local_sems, send_sems, recv_sems = sems
        x, y, c = _place()
        k = 2 * x + y
        local, remote, arrivals = [], [], []
        for j, ax in enumerate(self.axes):
            n = self.shard_shapes[j][ax]
            local.append(pltpu.make_async_copy(_shard_ref(ins[j], ax, k, n), outs[j].at[0], local_sems.at[j]))
            for m in range(1, 4):
                remote.append(pltpu.make_async_remote_copy(
                    src_ref=_shard_ref(ins[j], ax, k ^ m, n), dst_ref=outs[j].at[m],
                    send_sem=send_sems.at[j, m - 1], recv_sem=recv_sems.at[j, m - 1],
                    device_id=_chip_peer(x, y, c, m), device_id_type=MESH))
                arrivals.append(pltpu.make_async_remote_copy(
                    src_ref=_shard_ref(ins[j], ax, k, n), dst_ref=outs[j].at[m],
                    send_sem=send_sems.at[j, m - 1], recv_sem=recv_sems.at[j, m - 1],
                    device_id=(x, y, c), device_id_type=MESH))
        return local, remote, arrivals

    def start(self, ins, outs, sems):
        local, remote, _ = self._copies(ins, outs, sems)
        for cp in local + remote:
            cp.start()

    def relay(self, ins, outs, sems):
        pass

    def wait(self, ins, outs, sems):
        local, remote, arrivals = self._copies(ins, outs, sems)
        for cp in arrivals:
            cp.wait_recv()
        for cp in remote:
            cp.wait_send()
        for cp in local:
            cp.wait()


def _run_plan(plan, name):
    nc = len(plan.inputs)

    def body(*refs):
        ins, outs, sems = refs[:nc], refs[nc:2 * nc], refs[2 * nc:]
        plan.start(ins, outs, sems)
        plan.relay(ins, outs, sems)
        plan.wait(ins, outs, sems)

    return pl.pallas_call(body, name=name, in_specs=[ANY] * nc, out_specs=[ANY] * nc, out_shape=list(plan.out_shapes),
                          scratch_shapes=list(plan.sem_shapes))(*plan.inputs)


class _SwapPlan:
    def __init__(self, parts):
        self.inputs, nw = list(parts), len(parts)
        self.out_shapes = [_sds(p.shape, p.dtype) for p in parts]
        self.sem_shapes = [pltpu.SemaphoreType.DMA((nw,)), pltpu.SemaphoreType.DMA((nw,))]

    def _copies(self, ins, outs, sems):
        send_sems, recv_sems = sems
        x, y, c = _place()
        return [pltpu.make_async_remote_copy(
            src_ref=ins[j], dst_ref=outs[j], send_sem=send_sems.at[j], recv_sem=recv_sems.at[j],
            device_id=(x, y, 1 - c), device_id_type=MESH) for j in range(len(ins))]

    def start(self, ins, outs, sems):
        for cp in self._copies(ins, outs, sems):
            cp.start()

    def relay(self, ins, outs, sems):
        pass

    def wait(self, ins, outs, sems):
        for cp in self._copies(ins, outs, sems):
            cp.wait()


class _SmallGatherPlan:
    def __init__(self, v):
        self.inputs = [v]
        self.out_shapes = [_sds((8,) + v.shape, v.dtype)]
        self.sem_shapes = [pltpu.SemaphoreType.DMA((1,)), pltpu.SemaphoreType.DMA((7,)), pltpu.SemaphoreType.DMA((7,))]

    def _copies(self, ins, outs, sems):
        (v_ref,), (out_ref,), (local_sem, send_sems, recv_sems) = ins, outs, sems
        x, y, c = _place()
        me = 4 * x + 2 * y + c
        local = pltpu.make_async_copy(v_ref, out_ref.at[me], local_sem.at[0])
        sends, arrivals = [], []
        for m in range(1, 8):
            px, py, pc = x ^ (m >> 2), y ^ ((m >> 1) & 1), c ^ (m & 1)
            sends.append(pltpu.make_async_remote_copy(
                src_ref=v_ref, dst_ref=out_ref.at[me], send_sem=send_sems.at[m - 1], recv_sem=recv_sems.at[m - 1],
                device_id=(px, py, pc), device_id_type=MESH))
            arrivals.append(pltpu.make_async_remote_copy(
                src_ref=v_ref, dst_ref=out_ref.at[4 * px + 2 * py + pc], send_sem=send_sems.at[m - 1],
                recv_sem=recv_sems.at[m - 1], device_id=(x, y, c), device_id_type=MESH))
        return local, sends, arrivals

    def start(self, ins, outs, sems):
        local, sends, _ = self._copies(ins, outs, sems)
        for cp in [local] + sends:
            cp.start()

    def relay(self, ins, outs, sems):
        pass

    def wait(self, ins, outs, sems):
        local, sends, arrivals = self._copies(ins, outs, sems)
        for cp in arrivals:
            cp.wait_recv()
        for cp in sends:
            cp.wait_send()
        local.wait()


class _PlanGroup:
    def __init__(self, plans):
        self.plans = [p for p in plans if p is not None]
        self.inputs = [a for p in self.plans for a in p.inputs]
        self.out_shapes = [s for p in self.plans for s in p.out_shapes]
        self.sem_shapes = [s for p in self.plans for s in p.sem_shapes]

    def _each(self, ins, outs, sems):
        i = s = 0
        for p in self.plans:
            n, ns = len(p.inputs), len(p.sem_shapes)
            yield p, ins[i:i + n], outs[i:i + n], sems[s:s + ns]
            i, s = i + n, s + ns

    def start(self, ins, outs, sems):
        for p, pi, po, ps in self._each(ins, outs, sems):
            p.start(pi, po, ps)

    def relay(self, ins, outs, sems):
        for p, pi, po, ps in self._each(ins, outs, sems):
            p.relay(pi, po, ps)

    def wait(self, ins, outs, sems):
        for p, pi, po, ps in self._each(ins, outs, sems):
            p.wait(pi, po, ps)

    def split(self, outs):
        res, i = [], 0
        for p in self.plans:
            res.append(outs[i:i + len(p.inputs)])
            i += len(p.inputs)
        return res


BIG = ("w_ffn1_in", "w_ffn1_out", "w_in", "w_pool_branch", "w_attn_branch", "w_out", "w_ffn2_in", "w_ffn2_out")
BIG_AXIS = {"w_ffn1_in": 1, "w_ffn1_out": 0, "w_in": 1, "w_pool_branch": 1, "w_attn_branch": 1, "w_out": 0,
            "w_ffn2_in": 1, "w_ffn2_out": 0}


class _Sharded:
    fused_scatter = True

    def __init__(self, shards):
        self.shards, self.full, self.recv = shards, {}, {}

    def gather_plan(self, names):
        return _GatherPlan([self.shards[n] for n in names], [BIG_AXIS[n.split("/")[0]] for n in names])

    def gather_now(self, names):
        self.gathered(names, _run_plan(self.gather_plan(names), "gather_" + names[0]))

    def gathered(self, names, outs):
        self.full.update(zip(names, outs))

    def scatter_plan(self, names, grads):
        return _ScatterPlan([grads[n] for n in names], [BIG_AXIS[n] for n in names])

    def scatter_now(self, names, grads):
        self.scattered(names, _run_plan(self.scatter_plan(names, grads), "scatter_" + names[0]))

    def scattered(self, names, outs):
        self.recv.update(zip(names, outs))


class _Whole:
    fused_scatter = False

    def __init__(self, full):
        self.full, self.recv = dict(full), {}

    def gather_plan(self, names):
        return None

    def gather_now(self, names):
        pass

    def gathered(self, names, outs):
        pass

    def scatter_plan(self, names, grads):
        return None

    def scatter_now(self, names, grads):
        pass

    def scattered(self, names, outs):
        pass


def _vec(rows):
    pad = [jnp.zeros((1, D), F32)] * (8 - len(rows))
    return jnp.concatenate([r.reshape(1, D) for r in rows] + pad, axis=0)


def _block_diag(w_pool):
    n, c = w_pool.shape[0], w_pool.shape[1]
    eye = jnp.eye(n, dtype=w_pool.dtype)
    return (eye[:, None, :, None] * w_pool[:, :, None, :]).reshape(n * c, n * c)


def _example_step(x, tgt, positions, mod, gains, w_pool, pool_scale, ws, pack=None):
    T = x.shape[0]
    assert (T // BLK // DIL[-1]) & (T // BLK // DIL[-1] - 1) == 0, "blocks per sequence must be a power of two"
    sh1, sc1, gt1, sh2, sc2, gt2, sh3, sc3, gt3 = [mod[j * D:(j + 1) * D] for j in range(NMOD)]
    g1, g2, g3, gf = gains
    vec1, vec2, vec3 = _vec([g1, sh1, sc1, gt1]), _vec([g2, sh2, sc2, gt2]), _vec([g3, sh3, sc3, gt3])
    inv_freq = 10000.0 ** (-jnp.arange(0, HD, 2, dtype=F32) / HD)
    ang = positions.astype(F32)[:, None] * inv_freq
    cos = jnp.tile(jnp.cos(ang), (1, 4))
    sin = jnp.tile(jnp.concatenate([-jnp.sin(ang), jnp.sin(ang)], axis=1), (1, 2))
    wp_bd = _block_diag(w_pool).astype(BF16)
    ones_bd = _block_diag(jnp.ones((NH, HD, HD), F32)).astype(BF16)
    ps = jnp.concatenate([pool_scale.reshape(1, PW), jnp.zeros((7, PW), F32)], axis=0)
    wb = ws.full

    if "w_ffn1_in" not in wb:
        ws.gather_now(["w_ffn1_in"])
    (u1, a1, b1), got = _ffn_ab(x, vec1, [wb["w_ffn1_in"]], "ffn1_ab", ws.gather_plan(["w_ffn1_out"]))
    ws.gathered(["w_ffn1_out"], got)
    mixw = ["w_in", "w_pool_branch", "w_attn_branch", "w_out"]
    (h1, f1), got = _ffn_out(x, a1, b1, vec1, wb["w_ffn1_out"], "ffn1_out", ws.gather_plan(mixw))
    ws.gathered(mixw, got)
    (u2, p, qs, ks, vs, gates), got = _mix_proj(h1, vec2, wb["w_in"], cos, sin, ws.gather_plan(["w_ffn2_in/0"]))
    ws.gathered(["w_ffn2_in/0"], got)
    qs, ks, vs = [_flat(t) for t in qs], [_flat(t) for t in ks], [_flat(t) for t in vs]
    nbs = [T // d // BLK for d in DIL]
    os, lses = [], []
    for gi, riders in enumerate((["w_ffn2_out"], ["w_ffn2_in/1"], None)):
        (o, lse), got = _attn_fwd(qs[gi], ks[gi], vs[gi], nbs[gi], f"attn_fwd{gi}", riders and ws.gather_plan(riders))
        ws.gathered(riders or [], got)
        os.append(o)
        lses.append(lse)
    win3 = [wb["w_ffn2_in/0"], wb["w_ffn2_in/1"]] if "w_ffn2_in/0" in wb else [wb["w_ffn2_in"]]
    os_r = [_by_residue(t, d) for t, d in zip(os, DIL)]
    lses_r = [_by_residue(t, d) for t, d in zip(lses, DIL)]
    h2, ypool, yattn, merged, mixout, dpool = _mix_merge(
        h1, vec2, p, os_r, lses_r, gates, wp_bd, ps, wb["w_pool_branch"], wb["w_attn_branch"], wb["w_out"])
    (dh3, u3, a3, b3, f3, lacc), _ = _ffn_fwd(h2, vec3, win3, wb["w_ffn2_out"], "ffn2_fwd", head=(tgt, _vec([gf])))
    loss = 0.5 * jnp.sum(lacc[0]) / D

    grads = {}

    def wgrad_cols(name, xx, yy, riders, extra=None):
        group = _PlanGroup([ws.scatter_plan(riders, grads) if riders else None, extra])
        plan = group if group.plans else None
        if ws.fused_scatter:
            blocks, got = _wgrad_scatter(xx, yy, "wg_" + name, min(2048, T // 2), comm=plan)
            ws.scattered([name], [blocks])
        else:
            grads[name], got = _wgrad(xx, yy, "wg_" + name, D, 512, 1024, comm=plan)
        parts = group.split(got)
        if len(parts) > (extra is not None):
            ws.scattered(riders, parts[0])
        return parts[-1] if extra is not None else None

    (dh2, dab3, s3, df3, acc3), _ = _ffn_bwd(dh3, h2, a3, b3, f3, vec3, win3, wb["w_ffn2_out"], "ffn2_bwd")
    grads["w_ffn2_out"], _ = _wgrad(s3, df3, "wg_ffn2_out", FC, 512, 1024)
    wgrad_cols("w_ffn2_in", u3, dab3, ["w_ffn2_out"])
    (dgates, do0, do1, do2, e0, e1, e2, dd, acc2a, accps,
     grads["w_out"], grads["w_pool_branch"], grads["w_attn_branch"], gwp), _ = _mix_bwd_a(
        dh2, vec2, mixout, merged, gates, ypool, yattn, dpool, os_r, lses_r, wp_bd, ps,
        wb["w_pool_branch"], wb["w_attn_branch"], wb["w_out"], ones_bd)
    n = len(POOL_WINDOWS)
    c = PW // n
    grad_w_pool = jnp.stack([gwp[j * c:(j + 1) * c, j * c:(j + 1) * c] for j in range(n)], axis=0)
    small3 = ["w_out", "w_pool_branch", "w_attn_branch"]
    dqs, dks, dvs = [], [], []
    for gi, (do, e) in enumerate(((do0, e0), (do1, e1), (do2, e2))):
        plan = ws.scatter_plan(small3, grads) if gi == 0 else None
        (dq, dk, dv), got = _attn_bwd(qs[gi], ks[gi], vs[gi], _flat(do), lses[gi], _flat(e), nbs[gi], f"attn_bwd{gi}", plan)
        if gi == 0:
            ws.scattered(small3, got)
        dqs.append(_by_residue(dq, DIL[gi]))
        dks.append(_by_residue(dk, DIL[gi]))
        dvs.append(_by_residue(dv, DIL[gi]))
    dh1, dproj, acc2b = _mix_bwd_b(dh2, h1, vec2, dd, dqs, dks, dvs, dgates, cos, sin, wb["w_in"])
    wgrad_cols("w_in", u2, dproj, [])
    (dx, dab1, s1, df1, acc1), _ = _ffn_bwd(dh1, x, a1, b1, f1, vec1, [wb["w_ffn1_in"]], wb["w_ffn1_out"], "ffn1_bwd")
    grads["w_ffn1_out"], _ = _wgrad(s1, df1, "wg_ffn1_out", FC, 512, 1024)
    dmod = jnp.concatenate([acc1[0], acc1[1], acc1[3], acc2b[0], acc2b[1], acc2a[3], acc3[0], acc3[1], acc3[3]])
    dgains = jnp.stack([acc1[2], acc2b[2], acc3[2], lacc[1]], axis=0)
    row = None if pack is None else _SmallGatherPlan(pack(loss, dmod, dgains, grad_w_pool, accps[0]))
    rows = wgrad_cols("w_ffn1_in", u1, dab1, ["w_ffn1_out"], row)
    return loss, dx, dmod, dgains, grad_w_pool, accps[0], grads, None if rows is None else rows[0]


SMALL = ("b_ada", "g_norm_ffn1", "g_norm_mix", "g_norm_ffn2", "g_final", "pool_scale", "w_pool")
WEIGHTS = ("w_ada", "b_ada", "g_norm_ffn1", "w_ffn1_in", "w_ffn1_out", "g_norm_mix", "w_in", "w_pool", "pool_scale",
           "w_pool_branch", "w_attn_branch", "w_out", "g_norm_ffn2", "w_ffn2_in", "w_ffn2_out", "g_final")


def _pack_small(t):
    return jnp.concatenate([t[n].reshape(-1) for n in SMALL]).reshape(1, -1)


def kernel(x, c, positions, w_ada, b_ada, g_norm_ffn1, w_ffn1_in, w_ffn1_out, g_norm_mix, w_in, w_pool, pool_scale, w_pool_branch, w_attn_branch, w_out, g_norm_ffn2, w_ffn2_in, w_ffn2_out, g_final, loss_target, m_w_ada, m_b_ada, m_g_norm_ffn1, m_w_ffn1_in, m_w_ffn1_out, m_g_norm_mix, m_w_in, m_w_pool, m_pool_scale, m_w_pool_branch, m_w_attn_branch, m_w_out, m_g_norm_ffn2, m_w_ffn2_in, m_w_ffn2_out, m_g_final, v_w_ada, v_b_ada, v_g_norm_ffn1, v_w_ffn1_in, v_w_ffn1_out, v_g_norm_mix, v_w_in, v_w_pool, v_pool_scale, v_w_pool_branch, v_w_attn_branch, v_w_out, v_g_norm_ffn2, v_w_ffn2_in, v_w_ffn2_out, v_g_final):
    w = dict(w_ada=w_ada, b_ada=b_ada, g_norm_ffn1=g_norm_ffn1, w_ffn1_in=w_ffn1_in, w_ffn1_out=w_ffn1_out,
             g_norm_mix=g_norm_mix, w_in=w_in, w_pool=w_pool, pool_scale=pool_scale, w_pool_branch=w_pool_branch,
             w_attn_branch=w_attn_branch, w_out=w_out, g_norm_ffn2=g_norm_ffn2, w_ffn2_in=w_ffn2_in,
             w_ffn2_out=w_ffn2_out, g_final=g_final)
    mom = dict(w_ada=m_w_ada, b_ada=m_b_ada, g_norm_ffn1=m_g_norm_ffn1, w_ffn1_in=m_w_ffn1_in, w_ffn1_out=m_w_ffn1_out,
               g_norm_mix=m_g_norm_mix, w_in=m_w_in, w_pool=m_w_pool, pool_scale=m_pool_scale,
               w_pool_branch=m_w_pool_branch, w_attn_branch=m_w_attn_branch, w_out=m_w_out, g_norm_ffn2=m_g_norm_ffn2,
               w_ffn2_in=m_w_ffn2_in, w_ffn2_out=m_w_ffn2_out, g_final=m_g_final)
    var = dict(w_ada=v_w_ada, b_ada=v_b_ada, g_norm_ffn1=v_g_norm_ffn1, w_ffn1_in=v_w_ffn1_in, w_ffn1_out=v_w_ffn1_out,
               g_norm_mix=v_g_norm_mix, w_in=v_w_in, w_pool=v_w_pool, pool_scale=v_pool_scale,
               w_pool_branch=v_w_pool_branch, w_attn_branch=v_w_attn_branch, w_out=v_w_out, g_norm_ffn2=v_g_norm_ffn2,
               w_ffn2_in=v_w_ffn2_in, w_ffn2_out=v_w_ffn2_out, g_final=v_g_final)
    ix, iy, ic = _place()
    chip = 2 * ix + iy
    me = 4 * ix + 2 * iy + ic
    nada = w_ada.shape[2]

    shards = {n: w[n][0].astype(BF16) for n in BIG}
    half = D // 2
    shards["w_ffn2_in/0"], shards["w_ffn2_in/1"] = shards["w_ffn2_in"][:half], shards["w_ffn2_in"][half:]
    ws = _Sharded(shards)
    c_all = _run_plan(_SmallGatherPlan(c), "gather_c")[0][:, 0, :]
    b_shard = lax.dynamic_slice_in_dim(b_ada, chip * nada, nada, axis=1)
    mod_cols = _ada_fwd(c_all, w_ada[0], b_shard)
    first = _PlanGroup([_SmallGatherPlan(mod_cols), ws.gather_plan(["w_ffn1_in"])])
    (mod_all,), ffn1 = first.split(_run_plan(first, "gather_first"))
    ws.gathered(["w_ffn1_in"], ffn1)
    mod = jnp.concatenate([lax.dynamic_index_in_dim(mod_all[4 * (kk >> 1) + 2 * (kk & 1)], me, axis=0, keepdims=False)
                           for kk in range(4)])

    def pack(loss, dmod, dgains, g_w_pool, g_pool_scale):
        small_g = dict(b_ada=dmod, g_norm_ffn1=dgains[0], g_norm_mix=dgains[1], g_norm_ffn2=dgains[2],
                       g_final=dgains[3], pool_scale=g_pool_scale, w_pool=g_w_pool)
        return jnp.concatenate([_pack_small(small_g), jnp.pad(loss.reshape(1, 1), ((0, 0), (0, 127)))], axis=1)

    _, dx, _, _, _, _, _, gathered = _example_step(
        x[0], loss_target[0], positions[0], mod, (g_norm_ffn1[0], g_norm_mix[0], g_norm_ffn2[0], g_final),
        w_pool[0], pool_scale[0], ws, pack)

    per_weight, loss_tile = _adam_small(*[[t[n].reshape(1, -1) for n in SMALL] for t in (w, mom, var)], gathered)
    small_out = [{n: per_weight[j][kind].reshape(w[n].shape) for j, n in enumerate(SMALL)} for kind in range(4)]
    loss = loss_tile[0, 0]

    dmod_all = gathered[:, 0, :NMOD * D]
    dmod_cols = lax.dynamic_slice_in_dim(dmod_all, chip * nada, nada, axis=1)
    g_ada = _ada_bwd(c_all, dmod_cols)

    ada_out = _adam(w_ada[0], m_w_ada[0], v_w_ada[0], [g_ada], "adam_w_ada")[0]

    sums = {n: _sum4(ws.recv[n], "sum_" + n) for n in BIG}
    other = dict(zip(BIG, _run_plan(_SwapPlan([sums[n] for n in BIG]), "swap_sibling")))
    big_out = {}
    for n in BIG:
        if sums[n].shape[0] < w[n].shape[1]:
            big_out[n] = _adam_halves(w[n][0], mom[n][0], var[n][0], sums[n], other[n], "adam_" + n)
        else:
            big_out[n] = _adam(w[n][0], mom[n][0], var[n][0], [sums[n], other[n]], "adam_" + n)[0]

    def leaf(kind, n):
        if n == "w_ada":
            return ada_out[kind][None]
        if n in big_out:
            return big_out[n][kind][None]
        return small_out[kind][n]

    return (loss, dx[None], *[leaf(kind, n) for kind in range(4) for n in WEIGHTS])
```

```python
import jax
import jax.numpy as jnp
from jax import lax
from jax.experimental import pallas as pl
from jax.experimental.pallas import tpu as pltpu

F32 = jnp.float32
BF16 = jnp.bfloat16

D = 1024
FF = 2816
FC = FF
PW = 256
GA = 256
HD = 64
LANES = 128
NH = GA // HD
NG = 3
DIL = (1, 4, 16)
BLK = 128
GW = 2 * D
INW = PW + 3 * NG * GA + GW
NMOD = 9
POOL_WINDOWS = (2, 4, 8, 16)
HALO = 16
EPS = 1e-6
SCALE = HD ** -0.5
NEG = -1e30

LR, B1, B2, AEPS, WD, STEP = 0.001, 0.9, 0.999, 1e-08, 0.01, 10

VMEM_BIG = 56 * 1024 * 1024
TM = 256

MESH = pl.DeviceIdType.MESH
ANY = pl.BlockSpec(memory_space=pl.ANY)


def _call(body, name, grid, in_specs, out_specs, out_shape, scratch=(), vmem=None, comm=None):
    params = pltpu.CompilerParams(dimension_semantics=("arbitrary",) * len(grid), vmem_limit_bytes=vmem)
    n_in, n_out, n_scr = len(in_specs), len(out_shape), len(scratch)
    if comm is None:
        call = pl.pallas_call(body, name=name, grid=grid, in_specs=list(in_specs), out_specs=list(out_specs),
                              out_shape=list(out_shape), scratch_shapes=list(scratch), compiler_params=params)
        return lambda *args: (call(*args), ())
    nc = len(comm.inputs)

    def body_with_comm(*refs):
        ins, refs = refs[:n_in], refs[n_in:]
        c_ins, refs = refs[:nc], refs[nc:]
        outs, refs = refs[:n_out], refs[n_out:]
        c_outs, refs = refs[:nc], refs[nc:]
        scr, sems = refs[:n_scr], refs[n_scr:]
        first = pl.program_id(0) == 0
        last = pl.program_id(0) == grid[0] - 1
        for ax in range(1, len(grid)):
            first = jnp.logical_and(first, pl.program_id(ax) == 0)
            last = jnp.logical_and(last, pl.program_id(ax) == grid[ax] - 1)

        @pl.when(first)
        def _():
            comm.start(c_ins, c_outs, sems)

        body(*ins, *outs, *scr)
        early_relay = len(grid) == 1 and grid[0] >= 4
        if early_relay:
            @pl.when(pl.program_id(0) == (3 * grid[0]) // 4)
            def _():
                comm.relay(c_ins, c_outs, sems)

        @pl.when(last)
        def _():
            if not early_relay:
                comm.relay(c_ins, c_outs, sems)
            comm.wait(c_ins, c_outs, sems)

    call = pl.pallas_call(
        body_with_comm, name=name, grid=grid, in_specs=list(in_specs) + [ANY] * nc,
        out_specs=list(out_specs) + [ANY] * nc, out_shape=list(out_shape) + list(comm.out_shapes),
        scratch_shapes=list(scratch) + list(comm.sem_shapes), compiler_params=params)

    def run(*args):
        res = call(*args, *comm.inputs)
        return res[:n_out], res[n_out:]

    return run


def _rows(tm, n):
    return pl.BlockSpec((tm, n), lambda i: (i, 0))


def _const(shape):
    return pl.BlockSpec(shape, lambda i: (0,) * len(shape))


def _sds(shape, dtype):
    return jax.ShapeDtypeStruct(shape, dtype)


def _dot(a, b):
    return jnp.dot(a, b, preferred_element_type=F32)


def _dot_nt(a, b):
    return lax.dot_general(a, b, (((1,), (1,)), ((), ())), preferred_element_type=F32)


def _dot_tn(a, b):
    return lax.dot_general(a, b, (((0,), (0,)), ((), ())), preferred_element_type=F32)


def _colsum(v):
    return jnp.sum(v, axis=0, keepdims=True)


def _norm_fwd(h, g, sh, sc):
    r = lax.rsqrt(jnp.mean(h * h, axis=-1, keepdims=True) + EPS)
    xh = h * r
    n = xh * g
    return xh, r, n, n * (1.0 + sc) + sh


def _norm_bwd(du, xh, r, n, g, sc):
    dn = du * (1.0 + sc)
    dxh = dn * g
    dh = r * (dxh - xh * jnp.mean(dxh * xh, axis=-1, keepdims=True))
    return dh, _colsum(du), _colsum(du * n), _colsum(dn * xh)


def _load_once(pairs, sems):
    @pl.when(pl.program_id(0) == 0)
    def _():
        cps = [pltpu.make_async_copy(src, dst, sems.at[j]) for j, (src, dst) in enumerate(pairs)]
        for cp in cps:
            cp.start()
        for cp in cps:
            cp.wait()


def _zero_first(ref):
    @pl.when(pl.program_id(0) == 0)
    def _():
        ref[...] = jnp.zeros(ref.shape, ref.dtype)


def _row_chunks(hbm_refs, vmem_ref):
    pairs, row = [], 0
    for ref in hbm_refs:
        pairs.append((ref, vmem_ref.at[pl.ds(row, ref.shape[0]), :]))
        row += ref.shape[0]
    return pairs


def _loss_head(hh, tgt, g):
    r = lax.rsqrt(jnp.mean(hh * hh, axis=-1, keepdims=True) + EPS)
    xh = hh * r
    err = xh * g - tgt
    dy = err * (1.0 / D)
    dxh = dy * g
    dh = r * (dxh - xh * jnp.mean(dxh * xh, axis=-1, keepdims=True))
    return dh, _colsum(err * err), _colsum(dy * xh)


def _ffn_fwd(h, vec, wins, wout, name, comm=None, head=None):
    T = h.shape[0]
    nwin = len(wins)
    nhead = 0 if head is None else 2

    def body(h_ref, vec_ref, *rest):
        head_refs, rest = rest[:nhead], rest[nhead:]
        win_hbms, rest = rest[:nwin], rest[nwin:]
        (wout_hbm, ho_ref, u_ref, a_ref, b_ref, f_ref), rest = rest[:6], rest[6:]
        lacc_refs, (win_v, wout_v, sems) = rest[:nhead // 2], rest[nhead // 2:]
        _load_once(_row_chunks(win_hbms, win_v) + [(wout_hbm, wout_v)], sems)
        hh = h_ref[...]
        g, sh, sc, gt = vec_ref[0:1, :], vec_ref[1:2, :], vec_ref[2:3, :], vec_ref[3:4, :]
        _, _, _, u = _norm_fwd(hh, g, sh, sc)
        ub = u.astype(BF16)
        u_ref[...] = ub
        acc = None
        for j in range(FF // FC):
            lo, hi = j * FC, (j + 1) * FC
            a = _dot(ub, win_v[:, lo:hi])
            b = _dot(ub, win_v[:, FF + lo:FF + hi])
            a_ref[:, lo:hi] = a.astype(BF16)
            b_ref[:, lo:hi] = b.astype(BF16)
            s = (a * jax.nn.sigmoid(a) * b).astype(BF16)
            part = _dot(s, wout_v[lo:hi, :])
            acc = part if acc is None else acc + part
        f_ref[...] = acc.astype(BF16)
        ho = hh + 0.5 * gt * acc
        if head is None:
            ho_ref[...] = ho
        else:
            _zero_first(lacc_refs[0])
            dh, sq, dg = _loss_head(ho, head_refs[0][...], head_refs[1][0:1, :])
            ho_ref[...] = dh
            lacc_refs[0][0:1, :] += sq
            lacc_refs[0][1:2, :] += dg

    head_specs = [] if head is None else [_rows(TM, D), _const((8, D))]
    lacc_spec = [] if head is None else [_const((8, D))]
    lacc_shape = [] if head is None else [_sds((8, D), F32)]
    return _call(
        body, name, (T // TM,),
        [_rows(TM, D), _const((8, D))] + head_specs + [ANY] * (nwin + 1),
        [_rows(TM, D), _rows(TM, D), _rows(TM, FF), _rows(TM, FF), _rows(TM, D)] + lacc_spec,
        [_sds((T, D), F32), _sds((T, D), BF16), _sds((T, FF), BF16), _sds((T, FF), BF16), _sds((T, D), BF16)] + lacc_shape,
        scratch=[pltpu.VMEM((D, 2 * FF), BF16), pltpu.VMEM((FF, D), BF16), pltpu.SemaphoreType.DMA((nwin + 1,))],
        vmem=VMEM_BIG, comm=comm,
    )(h, vec, *([] if head is None else head), *wins, wout)


def _ffn_ab(h, vec, wins, name, comm=None):
    T = h.shape[0]
    nwin = len(wins)

    def body(h_ref, vec_ref, *rest):
        win_hbms, (u_ref, a_ref, b_ref, win_v, sems) = rest[:nwin], rest[nwin:]
        _load_once(_row_chunks(win_hbms, win_v), sems)
        g, sh, sc = vec_ref[0:1, :], vec_ref[1:2, :], vec_ref[2:3, :]
        _, _, _, u = _norm_fwd(h_ref[...], g, sh, sc)
        ub = u.astype(BF16)
        u_ref[...] = ub
        for j in range(FF // FC):
            lo, hi = j * FC, (j + 1) * FC
            a_ref[:, lo:hi] = _dot(ub, win_v[:, lo:hi]).astype(BF16)
            b_ref[:, lo:hi] = _dot(ub, win_v[:, FF + lo:FF + hi]).astype(BF16)

    return _call(
        body, name, (T // TM,),
        [_rows(TM, D), _const((8, D))] + [ANY] * nwin,
        [_rows(TM, D), _rows(TM, FF), _rows(TM, FF)],
        [_sds((T, D), BF16), _sds((T, FF), BF16), _sds((T, FF), BF16)],
        scratch=[pltpu.VMEM((D, 2 * FF), BF16), pltpu.SemaphoreType.DMA((nwin,))],
        vmem=VMEM_BIG, comm=comm,
    )(h, vec, *wins)


def _ffn_out(h, a, b, vec, wout, name, comm=None):
    T = h.shape[0]

    def body(h_ref, a_ref, b_ref, vec_ref, wout_hbm, ho_ref, f_ref, wout_v, sems):
        _load_once([(wout_hbm, wout_v)], sems)
        gt = vec_ref[3:4, :]
        acc = None
        for j in range(FF // FC):
            lo, hi = j * FC, (j + 1) * FC
            av = a_ref[:, lo:hi].astype(F32)
            s = (av * jax.nn.sigmoid(av) * b_ref[:, lo:hi].astype(F32)).astype(BF16)
            part = _dot(s, wout_v[lo:hi, :])
            acc = part if acc is None else acc + part
        f_ref[...] = acc.astype(BF16)
        ho_ref[...] = h_ref[...] + 0.5 * gt * acc

    return _call(
        body, name, (T // TM,),
        [_rows(TM, D), _rows(TM, FF), _rows(TM, FF), _const((8, D)), ANY],
        [_rows(TM, D), _rows(TM, D)],
        [_sds((T, D), F32), _sds((T, D), BF16)],
        scratch=[pltpu.VMEM((FF, D), BF16), pltpu.SemaphoreType.DMA((1,))],
        vmem=VMEM_BIG, comm=comm,
    )(h, a, b, vec, wout)


def _ffn_bwd(dh, h, a, b, f, vec, wins, wout, name, comm=None):
    T = h.shape[0]
    nwin = len(wins)

    def body(dh_ref, h_ref, a_ref, b_ref, f_ref, vec_ref, *rest):
        win_hbms, (wout_hbm, dhi_ref, dab_ref, s_ref, df_ref, acc_ref, win_v, wout_v, sems) = rest[:nwin], rest[nwin:]
        _load_once(_row_chunks(win_hbms, win_v) + [(wout_hbm, wout_v)], sems)
        _zero_first(acc_ref)
        g, sh, sc, gt = vec_ref[0:1, :], vec_ref[1:2, :], vec_ref[2:3, :], vec_ref[3:4, :]
        dho = dh_ref[...]
        df = (0.5 * gt * dho).astype(BF16)
        df_ref[...] = df
        dgt = _colsum(0.5 * dho * f_ref[...].astype(F32))
        du = None
        for j in range(FF // FC):
            lo, hi = j * FC, (j + 1) * FC
            av = a_ref[:, lo:hi].astype(F32)
            bv = b_ref[:, lo:hi].astype(F32)
            ds = _dot_nt(df, wout_v[lo:hi, :])
            sig = jax.nn.sigmoid(av)
            sa = av * sig
            s_ref[:, lo:hi] = (sa * bv).astype(BF16)
            da = (ds * bv * (sig * (1.0 + av * (1.0 - sig)))).astype(BF16)
            db = (ds * sa).astype(BF16)
            dab_ref[:, lo:hi] = da
            dab_ref[:, FF + lo:FF + hi] = db
            part = _dot_nt(da, win_v[:, lo:hi]) + _dot_nt(db, win_v[:, FF + lo:FF + hi])
            du = part if du is None else du + part
        xh, r, n, _ = _norm_fwd(h_ref[...], g, sh, sc)
        dhn, dsh, dsc, dg = _norm_bwd(du, xh, r, n, g, sc)
        dhi_ref[...] = dho + dhn
        acc_ref[0:1, :] += dsh
        acc_ref[1:2, :] += dsc
        acc_ref[2:3, :] += dg
        acc_ref[3:4, :] += dgt

    return _call(
        body, name, (T // TM,),
        [_rows(TM, D), _rows(TM, D), _rows(TM, FF), _rows(TM, FF), _rows(TM, D), _const((8, D))] + [ANY] * (nwin + 1),
        [_rows(TM, D), _rows(TM, 2 * FF), _rows(TM, FF), _rows(TM, D), _const((8, D))],
        [_sds((T, D), F32), _sds((T, 2 * FF), BF16), _sds((T, FF), BF16), _sds((T, D), BF16), _sds((8, D), F32)],
        scratch=[pltpu.VMEM((D, 2 * FF), BF16), pltpu.VMEM((FF, D), BF16), pltpu.SemaphoreType.DMA((nwin + 1,))],
        vmem=VMEM_BIG, comm=comm,
    )(dh, h, a, b, f, vec, *wins, wout)


def _wgrad(x, y, name, tk, tn, tt, out_dtype=BF16, comm=None):
    T, K = x.shape
    N = y.shape[1]
    nt = T // tt

    def body(x_ref, y_ref, o_ref, acc_ref):
        t = pl.program_id(2)
        part = _dot_tn(x_ref[...], y_ref[...])

        @pl.when(t == 0)
        def _():
            acc_ref[...] = part

        @pl.when(t > 0)
        def _():
            acc_ref[...] += part

        @pl.when(t == nt - 1)
        def _():
            o_ref[...] = acc_ref[...].astype(out_dtype)

    (out,), c_outs = _call(
        body, name, (K // tk, N // tn, nt),
        [pl.BlockSpec((tt, tk), lambda i, j, t: (t, i)), pl.BlockSpec((tt, tn), lambda i, j, t: (t, j))],
        [pl.BlockSpec((tk, tn), lambda i, j, t: (i, j))], [_sds((K, N), out_dtype)],
        scratch=[pltpu.VMEM((tk, tn), F32)], vmem=VMEM_BIG, comm=comm,
    )(x, y)
    return out, c_outs


def _wgrad_scatter(x, y, name, tt, comm=None):
    T, K = x.shape
    n = y.shape[1] // 4
    nt = T // tt
    assert nt >= 2, "a block's hand-over is added one grid step into the next block"
    half = K // 2
    nc = 0 if comm is None else len(comm.inputs)

    def body(chip_ref, x_ref, y_ref, *refs):
        c_ins, refs = refs[:nc], refs[nc:]
        recv_ref, refs = refs[0], refs[1:]
        c_outs, refs = refs[:nc], refs[nc:]
        acc_ref, keep_ref, give_ref, take_ref, local_sem, give_sems, take_sems, send_sems, recv_sems = refs[:9]
        j, t = pl.program_id(0), pl.program_id(1)
        px, py, pc = _place()

        def hand_over(jj):
            return pltpu.make_async_remote_copy(
                src_ref=give_ref.at[jj], dst_ref=take_ref.at[jj], send_sem=give_sems.at[jj], recv_sem=take_sems.at[jj],
                device_id=(px, py, 1 - pc), device_id_type=MESH)

        def send(jj):
            m = jj + 1
            return pltpu.make_async_remote_copy(
                src_ref=keep_ref.at[jj], dst_ref=recv_ref.at[m], send_sem=send_sems.at[jj], recv_sem=recv_sems.at[jj],
                device_id=_chip_peer(px, py, pc, m), device_id_type=MESH)

        def add_sibling(jj):
            hand_over(jj).wait_recv()
            keep_ref[jj] = (keep_ref[jj].astype(F32) + take_ref[jj].astype(F32)).astype(BF16)

        if comm is not None:
            @pl.when(jnp.logical_and(j == 0, t == 0))
            def _():
                comm.start(c_ins, c_outs, refs[9:])

        part = _dot_tn(x_ref[...], y_ref[...])

        @pl.when(t == 0)
        def _():
            acc_ref[...] = part

        @pl.when(t > 0)
        def _():
            acc_ref[...] += part

        for jj in range(3):
            @pl.when(jnp.logical_and(j == jj + 1, t == 0))
            def _():
                add_sibling(jj)
                send(jj).start()

        for jj in range(4):
            @pl.when(jnp.logical_and(j == jj, t == nt - 1))
            def _():
                keep_ref[jj] = acc_ref[pl.ds(pl.multiple_of(pc * half, 16), half), :].astype(BF16)
                give_ref[jj] = acc_ref[pl.ds(pl.multiple_of((1 - pc) * half, 16), half), :].astype(BF16)
                hand_over(jj).start()

        @pl.when(jnp.logical_and(j == 3, t == nt - 1))
        def _():
            add_sibling(3)
            own = pltpu.make_async_copy(keep_ref.at[3], recv_ref.at[0], local_sem.at[0])
            own.start()
            for jj in range(3):
                send(jj).wait_recv()
            for jj in range(3):
                send(jj).wait_send()
            for jj in range(4):
                hand_over(jj).wait_send()
            own.wait()
            if comm is not None:
                comm.relay(c_ins, c_outs, refs[9:])
                comm.wait(c_ins, c_outs, refs[9:])

    grid_spec = pltpu.PrefetchScalarGridSpec(
        num_scalar_prefetch=1, grid=(4, nt),
        in_specs=[pl.BlockSpec((tt, K), lambda j, t, chip: (t, 0)),
                  pl.BlockSpec((tt, n), lambda j, t, chip: (t, chip[0] ^ ((j + 1) & 3)))] + [ANY] * nc,
        out_specs=[ANY] * (1 + nc),
        scratch_shapes=[pltpu.VMEM((K, n), F32)] + [pltpu.VMEM((4, half, n), BF16)] * 3
        + [pltpu.SemaphoreType.DMA((1,))] + [pltpu.SemaphoreType.DMA((4,))] * 2 + [pltpu.SemaphoreType.DMA((3,))] * 2
        + ([] if comm is None else list(comm.sem_shapes)))
    px, py, _ = _place()
    res = pl.pallas_call(
        body, name=name, grid_spec=grid_spec,
        out_shape=[_sds((4, half, n), BF16)] + ([] if comm is None else list(comm.out_shapes)),
        compiler_params=pltpu.CompilerParams(dimension_semantics=("arbitrary", "arbitrary"), vmem_limit_bytes=VMEM_BIG),
    )((2 * px + py).astype(jnp.int32).reshape(1), x, y, *([] if comm is None else comm.inputs))
    return res[0], res[1:]


def _swap_halves(t):
    w = t.shape[1]
    lane = lax.broadcasted_iota(jnp.int32, t.shape, 1)
    return jnp.where(lane % HD < HD // 2, pltpu.roll(t, w - HD // 2, 1), pltpu.roll(t, HD // 2, 1))


def _rope(t, cos, sin_signed):
    c = jnp.tile(cos, (1, t.shape[1] // cos.shape[1]))
    s = jnp.tile(sin_signed, (1, t.shape[1] // sin_signed.shape[1]))
    return t * c + _swap_halves(t) * s


def _rope_bwd(dt, cos, sin_signed):
    c = jnp.tile(cos, (1, dt.shape[1] // cos.shape[1]))
    s = jnp.tile(sin_signed, (1, dt.shape[1] // sin_signed.shape[1]))
    return dt * c + _swap_halves(dt * s)


def _rm_spec(dil):
    return pl.BlockSpec((dil, TM // dil, GA), lambda i: (0, i, 0))


def _to_residues(t, dst_ref, scr_ref, dil):
    if dil == 1:
        dst_ref[0] = t.astype(dst_ref.dtype)
        return
    for j in range(GA // LANES):
        scr_ref[j] = t[:, j * LANES:(j + 1) * LANES]
    for r in range(dil):
        for j in range(GA // LANES):
            rows = scr_ref.at[j][pl.ds(r, TM // dil, stride=dil), :]
            dst_ref[r, :, j * LANES:(j + 1) * LANES] = rows.astype(dst_ref.dtype)


def _from_residues(src_ref, scr_ref, dil):
    if dil == 1:
        return src_ref[0].astype(F32)
    for r in range(dil):
        for j in range(GA // LANES):
            scr_ref.at[j][pl.ds(r, TM // dil, stride=dil), :] = src_ref[r, :, j * LANES:(j + 1) * LANES].astype(F32)
    return jnp.concatenate([scr_ref[j] for j in range(GA // LANES)], axis=1)


def _mix_proj(h, vec, win, cos, sin, comm=None):
    T = h.shape[0]

    def body(h_ref, vec_ref, win_hbm, cos_ref, sin_ref, u_ref, p_ref, *rest):
        qkv_refs, gates_ref, win_v, scr_ref, sems = rest[:3 * NG], rest[3 * NG], rest[3 * NG + 1], rest[3 * NG + 2], rest[3 * NG + 3]
        _load_once([(win_hbm, win_v)], sems)
        g, sh, sc = vec_ref[0:1, :], vec_ref[1:2, :], vec_ref[2:3, :]
        _, _, _, u = _norm_fwd(h_ref[...], g, sh, sc)
        ub = u.astype(BF16)
        u_ref[...] = ub
        mixer_cols = PW + 3 * NG * GA
        proj = _dot(ub, win_v[:, 0:mixer_cols])
        p_ref[...] = proj[:, 0:PW]
        cos_t, sin_t = cos_ref[...], sin_ref[...]
        for j in range(3 * NG):
            col = PW + j * GA
            t = proj[:, col:col + GA]
            if j < 2 * NG:
                t = _rope(t, cos_t, sin_t)
            _to_residues(t, qkv_refs[j], scr_ref, DIL[j % NG])
        gates_ref[...] = jax.nn.sigmoid(_dot(ub, win_v[:, mixer_cols:INW])).astype(BF16)

    outs, c_outs = _call(
        body, "mix_proj", (T // TM,),
        [_rows(TM, D), _const((8, D)), ANY, _rows(TM, 128), _rows(TM, 128)],
        [_rows(TM, D), _rows(TM, PW)] + [_rm_spec(d) for d in DIL] * 3 + [_rows(TM, GW)],
        [_sds((T, D), BF16), _sds((T, PW), F32)] + [_sds((d, T // d, GA), BF16) for d in DIL] * 3 + [_sds((T, GW), BF16)],
        scratch=[pltpu.VMEM((D, INW), BF16), pltpu.VMEM((GA // LANES, TM, LANES), F32), pltpu.SemaphoreType.DMA((1,))],
        vmem=VMEM_BIG, comm=comm,
    )(h, vec, win, cos, sin)
    return (outs[0], outs[1], outs[2:2 + NG], outs[2 + NG:2 + 2 * NG], outs[2 + 2 * NG:2 + 3 * NG], outs[2 + 3 * NG]), c_outs


def _head_masks():
    lane_head = lax.broadcasted_iota(jnp.int32, (BLK, GA), 1) // HD
    return [lane_head == hd for hd in range(NH)]


def _expand_heads(t, hm):
    return jnp.concatenate([jnp.where(m, t, jnp.zeros_like(t)) for m in hm], axis=0)


def _collapse_heads(tb, hm):
    out = None
    for hd, m in enumerate(hm):
        part = jnp.where(m, tb[hd * BLK:(hd + 1) * BLK, :], 0.0)
        out = part if out is None else out + part
    return out


def _head_rows(t):
    return jnp.concatenate([t[:, hd * HD:hd * HD + 1] for hd in range(NH)], axis=0)


def _band(has_prev):
    a = lax.broadcasted_iota(jnp.int32, (NH * BLK, 2 * BLK), 0) & (BLK - 1)
    c = lax.broadcasted_iota(jnp.int32, (NH * BLK, 2 * BLK), 1)
    return jnp.logical_and(c >= jnp.where(has_prev, a, BLK), c <= a + BLK)


def _attn_specs(nbt):
    cur = pl.BlockSpec((2 * BLK, GA), lambda i: (i, 0))
    prev = pl.BlockSpec((BLK, GA), lambda i: (jnp.maximum(2 * i - 1, 0), 0))
    nxt = pl.BlockSpec((BLK, GA), lambda i: (jnp.minimum(2 * i + 2, nbt - 1), 0))
    return cur, prev, nxt


def _attn_fwd(q, k, v, nb, name, comm=None):
    T = q.shape[0]
    nbt = T // BLK
    lo, hi = slice(0, BLK), slice(BLK, 2 * BLK)

    def block(qv, kcat, vcat, has_prev, hm):
        s = jnp.where(_band(has_prev), _dot_nt(_expand_heads(qv, hm), kcat) * SCALE, NEG)
        mx = jnp.max(s, axis=-1, keepdims=True)
        e = jnp.exp(s - mx)
        l = jnp.sum(e, axis=-1, keepdims=True)
        ob = _dot((e * (1.0 / l)).astype(BF16), vcat)
        return _collapse_heads(ob, hm), _collapse_heads(jnp.broadcast_to(mx + jnp.log(l), (NH * BLK, GA)), hm)

    def body(q_ref, k_ref, kp_ref, v_ref, vp_ref, o_ref, lse_ref):
        b0 = 2 * pl.program_id(0)
        hm = _head_masks()
        k_first = jnp.concatenate([kp_ref[...], k_ref[lo, :]], axis=0)
        v_first = jnp.concatenate([vp_ref[...], v_ref[lo, :]], axis=0)
        o_ref[lo, :], lse_ref[lo, :] = block(q_ref[lo, :], k_first, v_first, (b0 & (nb - 1)) != 0, hm)
        o_ref[hi, :], lse_ref[hi, :] = block(q_ref[hi, :], k_ref[...], v_ref[...], ((b0 + 1) & (nb - 1)) != 0, hm)

    cur, prev, _ = _attn_specs(nbt)
    return _call(body, name, (nbt // 2,), [cur, cur, prev, cur, prev], [cur, cur],
                 [_sds((T, GA), F32), _sds((T, GA), F32)], comm=comm)(q, k, k, v, v)


def _attn_bwd(q, k, v, do, lse, e, nb, name, comm=None):
    T = q.shape[0]
    nbt = T // BLK

    lo, hi = slice(0, BLK), slice(BLK, 2 * BLK)

    def probs_and_ds(qb, dob, kcat, vcat, lsev, ev, valid):
        p = jnp.where(valid, jnp.exp(_dot_nt(qb, kcat) * SCALE - _head_rows(lsev)), 0.0)
        return p, (p * (_dot_nt(dob, vcat) + _head_rows(ev))).astype(BF16)

    def body(q_ref, k_ref, v_ref, do_ref, lse_ref, e_ref, kp_ref, vp_ref, qn_ref, don_ref, lsen_ref, en_ref,
             dq_ref, dk_ref, dv_ref):
        b0 = 2 * pl.program_id(0)
        hm = _head_masks()
        q1, q2, q3 = _expand_heads(q_ref[lo, :], hm), _expand_heads(q_ref[hi, :], hm), _expand_heads(qn_ref[...], hm)
        do1, do2, do3 = (_expand_heads(do_ref[lo, :], hm), _expand_heads(do_ref[hi, :], hm),
                         _expand_heads(don_ref[...], hm))
        k1 = jnp.concatenate([kp_ref[...], k_ref[lo, :]], axis=0)
        v1 = jnp.concatenate([vp_ref[...], v_ref[lo, :]], axis=0)
        k2, v2 = k_ref[...], v_ref[...]
        p1, ds1 = probs_and_ds(q1, do1, k1, v1, lse_ref[lo, :], e_ref[lo, :], _band((b0 & (nb - 1)) != 0))
        p2, ds2 = probs_and_ds(q2, do2, k2, v2, lse_ref[hi, :], e_ref[hi, :], _band(((b0 + 1) & (nb - 1)) != 0))
        dq_ref[lo, :] = _collapse_heads(_dot(ds1, k1) * SCALE, hm)
        dq_ref[hi, :] = _collapse_heads(_dot(ds2, k2) * SCALE, hm)
        a = lax.broadcasted_iota(jnp.int32, (NH * BLK, BLK), 0) & (BLK - 1)
        c = lax.broadcasted_iota(jnp.int32, (NH * BLK, BLK), 1)
        valid3 = jnp.logical_and(c >= a, ((b0 + 2) & (nb - 1)) != 0)
        p3, ds3 = probs_and_ds(q3, do3, k_ref[hi, :], v_ref[hi, :], lsen_ref[...], en_ref[...], valid3)
        q12, q23 = jnp.concatenate([q1, q2], axis=0), jnp.concatenate([q2, q3], axis=0)
        do12, do23 = jnp.concatenate([do1, do2], axis=0), jnp.concatenate([do2, do3], axis=0)
        dk_ref[lo, :] = _dot_tn(jnp.concatenate([ds1[:, BLK:], ds2[:, :BLK]], axis=0), q12) * SCALE
        dk_ref[hi, :] = _dot_tn(jnp.concatenate([ds2[:, BLK:], ds3], axis=0), q23) * SCALE
        pb1, pb2, pb3 = p1.astype(BF16), p2.astype(BF16), p3.astype(BF16)
        dv_ref[lo, :] = _dot_tn(jnp.concatenate([pb1[:, BLK:], pb2[:, :BLK]], axis=0), do12).astype(BF16)
        dv_ref[hi, :] = _dot_tn(jnp.concatenate([pb2[:, BLK:], pb3], axis=0), do23).astype(BF16)

    cur, prev, nxt = _attn_specs(nbt)
    return _call(body, name, (nbt // 2,), [cur] * 6 + [prev, prev] + [nxt] * 4, [cur, cur, cur],
                 [_sds((T, GA), F32), _sds((T, GA), F32), _sds((T, GA), BF16)],
                 comm=comm)(q, k, v, do, lse, e, k, v, q, do, lse, e)


def _flat(t):
    return t.reshape(t.shape[0] * t.shape[1], t.shape[2])


def _by_residue(t, dil):
    return t.reshape(dil, t.shape[0] // dil, t.shape[1])


def _pool_consts(shape, row0):
    lane = lax.broadcasted_iota(jnp.int32, shape, 1)
    t = lax.broadcasted_iota(jnp.int32, shape, 0) + row0
    grp = lane // (PW // len(POOL_WINDOWS))
    win = jnp.where(grp == 0, POOL_WINDOWS[0], jnp.where(grp == 1, POOL_WINDOWS[1],
                    jnp.where(grp == 2, POOL_WINDOWS[2], POOL_WINDOWS[3])))
    cnt = jnp.minimum(t + 1, win).astype(F32)
    return grp, cnt


def _window_sums(ext_ref, base, step, tm):
    outs, run = [], None
    for j in range(POOL_WINDOWS[-1]):
        sl = ext_ref[pl.ds(base + step * j, tm), :]
        run = sl if run is None else run + sl
        if j + 1 in POOL_WINDOWS:
            outs.append(run)
    return outs


def _select_group(grp, vals):
    return jnp.where(grp == 0, vals[0], jnp.where(grp == 1, vals[1], jnp.where(grp == 2, vals[2], vals[3])))


def _pool_d(pc_ref, pp_ref, ext_ref, i, tm):
    ext_ref[0:HALO, :] = jnp.where(i > 0, pp_ref[tm - HALO:tm, :], 0.0)
    ext_ref[HALO:HALO + tm, :] = pc_ref[...]
    grp, cnt = _pool_consts((tm, PW), i * tm)
    sums = _window_sums(ext_ref, HALO, -1, tm)
    return _select_group(grp, sums) / cnt - pc_ref[...]


def _group_weights(ls):
    mx = jnp.maximum(jnp.maximum(ls[0], ls[1]), ls[2])
    es = [jnp.exp(l - mx) for l in ls]
    inv = 1.0 / (es[0] + es[1] + es[2])
    return [e * inv for e in es]


def _mix_merge(h, vec, p, os, lses, gates, wp_bd, pscale, wpb, wab, wout):
    T = h.shape[0]

    def body(h_ref, vec_ref, pc_ref, pp_ref, o0, o1, o2, l0, l1, l2, gates_ref, wp_ref, ps_ref, wpb_ref, wab_ref, wout_ref,
             ho_ref, yp_ref, ya_ref, mg_ref, mo_ref, d_ref, ext_ref, scr_ref):
        i = pl.program_id(0)
        gt = vec_ref[3:4, :]
        d = _pool_d(pc_ref, pp_ref, ext_ref, i, TM).astype(BF16)
        d_ref[...] = d
        ypool = (_dot(d, wp_ref[...]) * ps_ref[0:1, :]).astype(BF16)
        yp_ref[...] = ypool
        w = _group_weights([_from_residues(r, scr_ref, dl) for r, dl in zip((l0, l1, l2), DIL)])
        yattn = None
        for wg, o_ref, dl in zip(w, (o0, o1, o2), DIL):
            part = wg * _from_residues(o_ref, scr_ref, dl)
            yattn = part if yattn is None else yattn + part
        yattn = yattn.astype(BF16)
        ya_ref[...] = yattn
        merged = (gates_ref[:, 0:D].astype(F32) * _dot(ypool, wpb_ref[...])
                  + gates_ref[:, D:GW].astype(F32) * _dot(yattn, wab_ref[...])).astype(BF16)
        mg_ref[...] = merged
        mo = _dot(merged, wout_ref[...])
        mo_ref[...] = mo.astype(BF16)
        ho_ref[...] = h_ref[...] + gt * mo

    prev = pl.BlockSpec((TM, PW), lambda i: (jnp.maximum(i - 1, 0), 0))
    return _call(
        body, "mix_merge", (T // TM,),
        [_rows(TM, D), _const((8, D)), _rows(TM, PW), prev] + [_rm_spec(dl) for dl in DIL] * 2 + [_rows(TM, GW), _const((PW, PW)),
         _const((8, PW)), _const((PW, D)), _const((GA, D)), _const((D, D))],
        [_rows(TM, D), _rows(TM, PW), _rows(TM, GA), _rows(TM, D), _rows(TM, D), _rows(TM, PW)],
        [_sds((T, D), F32), _sds((T, PW), BF16), _sds((T, GA), BF16), _sds((T, D), BF16), _sds((T, D), BF16), _sds((T, PW), BF16)],
        scratch=[pltpu.VMEM((TM + HALO, PW), F32), pltpu.VMEM((GA // LANES, TM, LANES), F32)],
        vmem=VMEM_BIG,
    )(h, vec, p, p, *os, *lses, gates, wp_bd, pscale, wpb, wab, wout)[0]


def _mix_bwd_a(dh, vec, mixout, merged, gates, ypool, yattn, dpool, os, lses, wp_bd, pscale, wpb, wab, wout, ones_bd,
               comm=None):
    T = dh.shape[0]
    nt = T // TM

    def body(dh_ref, vec_ref, mo_ref, mg_ref, gates_ref, yp_ref, ya_ref, d_ref, o0, o1, o2, l0, l1, l2,
             wp_ref, ps_ref, wpb_ref, wab_ref, wout_ref, ones_ref,
             dgates_ref, do0, do1, do2, e0, e1, e2, dd_ref, acc_ref, acc2_ref, g_out_ref, g_pb_ref, g_ab_ref, g_pool_ref,
             scr_ref, a_out, a_pb, a_ab, a_pool):
        _zero_first(acc_ref)
        _zero_first(acc2_ref)
        for a_ref in (a_out, a_pb, a_ab, a_pool):
            _zero_first(a_ref)
        gt = vec_ref[3:4, :]
        dho = dh_ref[...]
        acc_ref[3:4, :] += _colsum(dho * mo_ref[...].astype(F32))
        dmo = (gt * dho).astype(BF16)
        a_out[...] += _dot_tn(mg_ref[...], dmo)
        dmerged = _dot_nt(dmo, wout_ref[...])
        gp = gates_ref[:, 0:D].astype(F32)
        ga = gates_ref[:, D:GW].astype(F32)
        bp = _dot(yp_ref[...], wpb_ref[...])
        ba = _dot(ya_ref[...], wab_ref[...])
        dgates_ref[:, 0:D] = (dmerged * bp * gp * (1.0 - gp)).astype(BF16)
        dgates_ref[:, D:GW] = (dmerged * ba * ga * (1.0 - ga)).astype(BF16)
        dbp = (dmerged * gp).astype(BF16)
        dba = (dmerged * ga).astype(BF16)
        a_pb[...] += _dot_tn(yp_ref[...], dbp)
        a_ab[...] += _dot_tn(ya_ref[...], dba)
        dypool = _dot_nt(dbp, wpb_ref[...])
        ypre = _dot(d_ref[...], wp_ref[...])
        acc2_ref[0:1, :] += _colsum(dypool * ypre)
        dyp = (dypool * ps_ref[0:1, :]).astype(BF16)
        a_pool[...] += _dot_tn(d_ref[...], dyp)
        dd_ref[...] = _dot_nt(dyp, wp_ref[...])
        dya = _dot_nt(dba, wab_ref[...])
        w = _group_weights([_from_residues(r, scr_ref, dl) for r, dl in zip((l0, l1, l2), DIL)])
        ya = None
        for wg, o_ref, dl in zip(w, (o0, o1, o2), DIL):
            part = wg * _from_residues(o_ref, scr_ref, dl)
            ya = part if ya is None else ya + part
        prod = dya * ya
        hi = prod.astype(BF16)
        lo = (prod - hi.astype(F32)).astype(BF16)
        tot = _dot(hi, ones_ref[...]) + _dot(lo, ones_ref[...])
        for wg, do_ref, e_ref, dl in zip(w, (do0, do1, do2), (e0, e1, e2), DIL):
            _to_residues(wg * dya, do_ref, scr_ref, dl)
            _to_residues(-wg * tot, e_ref, scr_ref, dl)

        @pl.when(pl.program_id(0) == nt - 1)
        def _():
            g_out_ref[...] = a_out[...].astype(BF16)
            g_pb_ref[...] = a_pb[...].astype(BF16)
            g_ab_ref[...] = a_ab[...].astype(BF16)
            g_pool_ref[...] = a_pool[...]

    return _call(
        body, "mix_bwd_a", (nt,),
        [_rows(TM, D), _const((8, D)), _rows(TM, D), _rows(TM, D), _rows(TM, GW), _rows(TM, PW), _rows(TM, GA), _rows(TM, PW)]
        + [_rm_spec(dl) for dl in DIL] * 2
        + [_const((PW, PW)), _const((8, PW)), _const((PW, D)), _const((GA, D)), _const((D, D)), _const((GA, GA))],
        [_rows(TM, GW)] + [_rm_spec(dl) for dl in DIL] * 2 + [_rows(TM, PW), _const((8, D)), _const((8, PW))]
        + [_const((D, D)), _const((PW, D)), _const((GA, D)), _const((PW, PW))],
        [_sds((T, GW), BF16)] + [_sds((dl, T // dl, GA), BF16) for dl in DIL]
        + [_sds((dl, T // dl, GA), F32) for dl in DIL] + [_sds((T, PW), F32), _sds((8, D), F32), _sds((8, PW), F32)]
        + [_sds((D, D), BF16), _sds((PW, D), BF16), _sds((GA, D), BF16), _sds((PW, PW), F32)],
        scratch=[pltpu.VMEM((GA // LANES, TM, LANES), F32), pltpu.VMEM((D, D), F32), pltpu.VMEM((PW, D), F32),
                 pltpu.VMEM((GA, D), F32), pltpu.VMEM((PW, PW), F32)],
        vmem=VMEM_BIG, comm=comm,
    )(dh, vec, mixout, merged, gates, ypool, yattn, dpool, *os, *lses, wp_bd, pscale, wpb, wab, wout, ones_bd)


def _mix_bwd_b(dh, h, vec, dd, dqs, dks, dvs, dgates, cos, sin, win):
    T = h.shape[0]
    nt = T // TM

    def body(dh_ref, h_ref, vec_ref, ddc_ref, ddn_ref, *rest):
        qk_refs, dv_refs = rest[:2 * NG], rest[2 * NG:3 * NG]
        dgates_ref, cos_ref, sin_ref, win_hbm, dhi_ref, dproj_ref, acc_ref, win_v, ext_ref, scr_ref, sems = rest[3 * NG:]
        i = pl.program_id(0)
        _load_once([(win_hbm, win_v)], sems)
        _zero_first(acc_ref)
        g, sh, sc = vec_ref[0:1, :], vec_ref[1:2, :], vec_ref[2:3, :]
        grp, cnt = _pool_consts((TM, PW), i * TM)
        _, cnt_n = _pool_consts((HALO, PW), (i + 1) * TM)
        ext_ref[0:TM, :] = ddc_ref[...] / cnt
        ext_ref[TM:TM + HALO, :] = jnp.where(i < nt - 1, ddn_ref[0:HALO, :] / cnt_n, 0.0)
        dp = _select_group(grp, _window_sums(ext_ref, 0, 1, TM)) - ddc_ref[...]
        dproj_ref[:, 0:PW] = dp.astype(BF16)
        cos_t, sin_t = cos_ref[...], sin_ref[...]
        for j in range(2 * NG):
            col = PW + j * GA
            dt = _from_residues(qk_refs[j], scr_ref, DIL[j % NG])
            dproj_ref[:, col:col + GA] = _rope_bwd(dt, cos_t, sin_t).astype(BF16)
        for j in range(NG):
            col = PW + (2 * NG + j) * GA
            dproj_ref[:, col:col + GA] = _from_residues(dv_refs[j], scr_ref, DIL[j]).astype(BF16)
        dproj_ref[:, PW + 3 * NG * GA:INW] = dgates_ref[...]
        du = None
        for j in range(INW // 512):
            part = _dot_nt(dproj_ref[:, j * 512:(j + 1) * 512], win_v[:, j * 512:(j + 1) * 512])
            du = part if du is None else du + part
        xh, r, n, _ = _norm_fwd(h_ref[...], g, sh, sc)
        dhn, dsh, dsc, dg = _norm_bwd(du, xh, r, n, g, sc)
        dhi_ref[...] = dh_ref[...] + dhn
        acc_ref[0:1, :] += dsh
        acc_ref[1:2, :] += dsc
        acc_ref[2:3, :] += dg

    nxt = pl.BlockSpec((TM, PW), lambda i: (jnp.minimum(i + 1, nt - 1), 0))
    return _call(
        body, "mix_bwd_b", (nt,),
        [_rows(TM, D), _rows(TM, D), _const((8, D)), _rows(TM, PW), nxt] + [_rm_spec(dl) for dl in DIL] * 3
        + [_rows(TM, GW), _rows(TM, 128), _rows(TM, 128), ANY],
        [_rows(TM, D), _rows(TM, INW), _const((8, D))],
        [_sds((T, D), F32), _sds((T, INW), BF16), _sds((8, D), F32)],
        scratch=[pltpu.VMEM((D, INW), BF16), pltpu.VMEM((TM + HALO, PW), F32), pltpu.VMEM((GA // LANES, TM, LANES), F32),
                 pltpu.SemaphoreType.DMA((1,))],
        vmem=VMEM_BIG,
    )(dh, h, vec, dd, dd, *dqs, *dks, *dvs, dgates, cos, sin, win)[0]


def _ada_fwd(c_all, w_shard, b_shard):
    n = w_shard.shape[1]

    def body(c_ref, w_ref, b_ref, o_ref):
        cv = c_ref[...]
        cond = (cv * jax.nn.sigmoid(cv)).astype(BF16)
        o_ref[...] = _dot(cond, w_ref[...].astype(BF16)) + b_ref[...]

    tn = n // 3
    return pl.pallas_call(
        body, name="ada_fwd", grid=(3,),
        in_specs=[pl.BlockSpec((8, D), lambda j: (0, 0)), pl.BlockSpec((D, tn), lambda j: (0, j)), pl.BlockSpec((1, tn), lambda j: (0, j))],
        out_specs=pl.BlockSpec((8, tn), lambda j: (0, j)), out_shape=_sds((8, n), F32),
        compiler_params=pltpu.CompilerParams(dimension_semantics=("arbitrary",)),
    )(c_all, w_shard, b_shard)


def _ada_bwd(c_all, dmod_shard):
    n = dmod_shard.shape[1]

    def body(c_ref, d_ref, o_ref):
        cv = c_ref[...]
        cond = (cv * jax.nn.sigmoid(cv)).astype(BF16)
        o_ref[...] = _dot_tn(cond, d_ref[...].astype(BF16))

    tn = n // 3
    return pl.pallas_call(
        body, name="ada_bwd", grid=(3,),
        in_specs=[pl.BlockSpec((8, D), lambda j: (0, 0)), pl.BlockSpec((8, tn), lambda j: (0, j))],
        out_specs=pl.BlockSpec((D, tn), lambda j: (0, j)), out_shape=_sds((D, n), F32),
        compiler_params=pltpu.CompilerParams(dimension_semantics=("arbitrary",)),
    )(c_all, dmod_shard)


def _adam_math(w, g, m, v):
    m2 = B1 * m + (1.0 - B1) * g
    v2 = B2 * v + (1.0 - B2) * (g * g)
    m_hat = m2 / (1.0 - B1 ** STEP)
    v_hat = v2 / (1.0 - B2 ** STEP)
    delta = -LR * (m_hat / (jnp.sqrt(v_hat) + AEPS) + WD * w)
    return delta, m2, v2


def _adam(w, m, v, parts, name, comm=None):
    R, C = w.shape
    tr = R
    for cand in (128, 64, 32, 16, 8):
        if R % cand == 0:
            tr = cand
            break
    np_ = len(parts)

    def body(w_ref, m_ref, v_ref, *rest):
        p_refs, (g_ref, d_ref, m2_ref, v2_ref) = rest[:np_], rest[np_:]
        g = p_refs[0][...]
        for pr in p_refs[1:]:
            g = g + pr[...]
        delta, m2, v2 = _adam_math(w_ref[...], g, m_ref[...], v_ref[...])
        g_ref[...] = g
        d_ref[...] = delta
        m2_ref[...] = m2
        v2_ref[...] = v2

    spec = pl.BlockSpec((tr, C), lambda i: (i, 0))
    return _call(body, name, (R // tr,), [spec] * (3 + np_), [spec] * 4, [_sds((R, C), F32)] * 4,
                 vmem=VMEM_BIG, comm=comm)(w, m, v, *parts)


def _adam_halves(w, m, v, mine, other, name):
    R, C = w.shape
    tr = 128
    nh = R // 2 // tr

    def body(c_ref, w_ref, m_ref, v_ref, mine_ref, other_ref, g_ref, d_ref, m2_ref, v2_ref):
        i = pl.program_id(0)
        in_mine = jnp.logical_and(i >= c_ref[0] * nh, i < (c_ref[0] + 1) * nh)
        g = jnp.where(in_mine, mine_ref[...], other_ref[...])
        delta, m2, v2 = _adam_math(w_ref[...], g, m_ref[...], v_ref[...])
        g_ref[...] = g
        d_ref[...] = delta
        m2_ref[...] = m2
        v2_ref[...] = v2

    spec = pl.BlockSpec((tr, C), lambda i, c: (i, 0))
    grid_spec = pltpu.PrefetchScalarGridSpec(
        num_scalar_prefetch=1, grid=(R // tr,),
        in_specs=[spec] * 3 + [pl.BlockSpec((tr, C), lambda i, c: (jnp.clip(i - c[0] * nh, 0, nh - 1), 0)),
                               pl.BlockSpec((tr, C), lambda i, c: (jnp.clip(i - (1 - c[0]) * nh, 0, nh - 1), 0))],
        out_specs=[spec] * 4)
    return pl.pallas_call(
        body, name=name, grid_spec=grid_spec, out_shape=[_sds((R, C), F32)] * 4,
        compiler_params=pltpu.CompilerParams(dimension_semantics=("arbitrary",), vmem_limit_bytes=VMEM_BIG),
    )(lax.axis_index("c").astype(jnp.int32).reshape(1), w, m, v, mine, other)


def _adam_small(ws, ms, vs, gathered):
    n = len(ws)
    sizes = [a.shape[1] for a in ws]

    def total(ga_ref, off, size):
        g = ga_ref[0, :, off:off + size]
        for dev in range(1, 8):
            g = g + ga_ref[dev, :, off:off + size]
        return g

    def body(*refs):
        w_refs, m_refs, v_refs, ga_ref, outs = refs[:n], refs[n:2 * n], refs[2 * n:3 * n], refs[3 * n], refs[3 * n + 1:]
        off = 0
        for j, size in enumerate(sizes):
            g = total(ga_ref, off, size)
            delta, m2, v2 = _adam_math(w_refs[j][...], g, m_refs[j][...], v_refs[j][...])
            for ref, val in zip(outs[4 * j:4 * j + 4], (g, delta, m2, v2)):
                ref[...] = val
            off += size
        outs[4 * n][...] = total(ga_ref, off, 128)

    res = pl.pallas_call(
        body, name="adam_small",
        out_shape=[_sds((1, size), F32) for size in sizes for _ in range(4)] + [_sds((1, 128), F32)],
    )(*ws, *ms, *vs, gathered)
    return [res[4 * j:4 * j + 4] for j in range(n)], res[4 * n]


def _sum4(blocks, name):
    _, R, C = blocks.shape
    tr = R
    for cand in (256, 128, 64, 32, 16):
        if R % cand == 0:
            tr = cand
            break

    def body(r_ref, out_ref):
        out_ref[...] = ((r_ref[0].astype(F32) + r_ref[1].astype(F32)) + r_ref[2].astype(F32)) + r_ref[3].astype(F32)

    return pl.pallas_call(
        body, name=name, grid=(R // tr,),
        in_specs=[pl.BlockSpec((4, tr, C), lambda i: (0, i, 0))],
        out_specs=pl.BlockSpec((tr, C), lambda i: (i, 0)), out_shape=_sds((R, C), F32),
        compiler_params=pltpu.CompilerParams(dimension_semantics=("arbitrary",)),
    )(blocks)


def _place():
    return lax.axis_index("x"), lax.axis_index("y"), lax.axis_index("c")


def _chip_peer(x, y, c, m):
    return (x ^ (m >> 1), y ^ (m & 1), c)


def _shard_ref(ref, axis, k, n):
    start = pl.multiple_of(k * n, 128 if axis == 1 else 16)
    return ref.at[:, pl.ds(start, n)] if axis == 1 else ref.at[pl.ds(start, n), :]


def _half_rows(ref, axis, k, n, hc):
    if axis == 1:
        half = ref.shape[0] // 2
        return ref.at[pl.ds(pl.multiple_of(hc * half, 16), half), pl.ds(pl.multiple_of(k * n, 128), n)]
    half = n // 2
    return ref.at[pl.ds(pl.multiple_of(k * n + hc * half, 16), half), :]


class _GatherPlan:
    def __init__(self, shards, axes):
        self.inputs, self.axes, nw = list(shards), list(axes), len(shards)
        self.out_shapes = [_sds((s.shape[0] * (4 if ax == 0 else 1), s.shape[1] * (4 if ax == 1 else 1)), BF16)
                           for s, ax in zip(shards, axes)]
        self.sem_shapes = [pltpu.SemaphoreType.DMA((nw,))] + [pltpu.SemaphoreType.DMA((nw, 3))] * 4

    def _copies(self, ins, outs, sems):
        local_sems, send_sems, recv_sems, pass_sems, got_sems = sems
        x, y, c = _place()
        k = 2 * x + y
        local, sends, arrivals, passes, handed = [], [], [], [], []
        for j, ax in enumerate(self.axes):
            n = ins[j].shape[ax]
            half = ins[j].shape[0] // 2
            local.append(pltpu.make_async_copy(ins[j], _shard_ref(outs[j], ax, k, n), local_sems.at[j]))
            my_half = ins[j].at[pl.ds(pl.multiple_of(c * half, 16), half), :]
            for m in range(1, 4):
                sends.append(pltpu.make_async_remote_copy(
                    src_ref=my_half, dst_ref=_half_rows(outs[j], ax, k, n, c), send_sem=send_sems.at[j, m - 1],
                    recv_sem=recv_sems.at[j, m - 1], device_id=_chip_peer(x, y, c, m), device_id_type=MESH))
                theirs = _half_rows(outs[j], ax, k ^ m, n, c)
                arrivals.append(pltpu.make_async_remote_copy(
                    src_ref=my_half, dst_ref=theirs, send_sem=send_sems.at[j, m - 1], recv_sem=recv_sems.at[j, m - 1],
                    device_id=(x, y, c), device_id_type=MESH))
                passes.append(pltpu.make_async_remote_copy(
                    src_ref=theirs, dst_ref=theirs, send_sem=pass_sems.at[j, m - 1], recv_sem=got_sems.at[j, m - 1],
                    device_id=(x, y, 1 - c), device_id_type=MESH))
                other = _half_rows(outs[j], ax, k ^ m, n, 1 - c)
                handed.append(pltpu.make_async_remote_copy(
                    src_ref=other, dst_ref=other, send_sem=pass_sems.at[j, m - 1], recv_sem=got_sems.at[j, m - 1],
                    device_id=(x, y, c), device_id_type=MESH))
        return local, sends, arrivals, passes, handed

    def start(self, ins, outs, sems):
        local, sends, _, _, _ = self._copies(ins, outs, sems)
        for cp in local + sends:
            cp.start()

    def relay(self, ins, outs, sems):
        _, _, arrivals, passes, _ = self._copies(ins, outs, sems)
        for arrived, onward in zip(arrivals, passes):
            arrived.wait_recv()
            onward.start()

    def wait(self, ins, outs, sems):
        local, sends, _, passes, handed = self._copies(ins, outs, sems)
        for cp in handed:
            cp.wait_recv()
        for cp in sends + passes:
            cp.wait_send()
        for cp in local:
            cp.wait()


class _ScatterPlan:
    def __init__(self, grads, axes):
        self.inputs, self.axes, nw = list(grads), list(axes), len(grads)
        self.shard_shapes = [(g.shape[0] // (4 if ax == 0 else 1), g.shape[1] // (4 if ax == 1 else 1))
                             for g, ax in zip(grads, axes)]
        self.out_shapes = [_sds((4,) + s, BF16) for s in self.shard_shapes]
        self.sem_shapes = [pltpu.SemaphoreType.DMA((nw,)), pltpu.SemaphoreType.DMA((nw, 3)), pltpu.SemaphoreType.DMA((nw, 3))]

    def _copies(self, ins, outs, sems):
        local_sems, send_sems, recv_sems = sems
        x, y, c = _place()
        k = 2 * x + y
        local, remote, arrivals = [], [], []
        for j, ax in enumerate(self.axes):
            n = self.shard_shapes[j][ax]
            local.append(pltpu.make_async_copy(_shard_ref(ins[j], ax, k, n), outs[j].at[0], local_sems.at[j]))
            for m in range(1, 4):
                remote.append(pltpu.make_async_remote_copy(
                    src_ref=_shard_ref(ins[j], ax, k ^ m, n), dst_ref=outs[j].at[m],
                    send_sem=send_sems.at[j, m - 1], recv_sem=recv_sems.at[j, m - 1],
                    device_id=_chip_peer(x, y, c, m), device_id_type=MESH))
                arrivals.append(pltpu.make_async_remote_copy(
                    src_ref=_shard_ref(ins[j], ax, k, n), dst_ref=outs[j].at[m],
                    send_sem=send_sems.at[j, m - 1], recv_sem=recv_sems.at[j, m - 1],
                    device_id=(x, y, c), device_id_type=MESH))
        return local, remote, arrivals

    def start(self, ins, outs, sems):
        local, remote, _ = self._copies(ins, outs, sems)
        for cp in local + remote:
            cp.start()

    def relay(self, ins, outs, sems):
        pass

    def wait(self, ins, outs, sems):
        local, remote, arrivals = self._copies(ins, outs, sems)
        for cp in arrivals:
            cp.wait_recv()
        for cp in remote:
            cp.wait_send()
        for cp in local:
            cp.wait()


def _run_plan(plan, name):
    nc = len(plan.inputs)

    def body(*refs):
        ins, outs, sems = refs[:nc], refs[nc:2 * nc], refs[2 * nc:]
        plan.start(ins, outs, sems)
        plan.relay(ins, outs, sems)
        plan.wait(ins, outs, sems)

    return pl.pallas_call(body, name=name, in_specs=[ANY] * nc, out_specs=[ANY] * nc, out_shape=list(plan.out_shapes),
                          scratch_shapes=list(plan.sem_shapes))(*plan.inputs)


class _SwapPlan:
    def __init__(self, parts):
        self.inputs, nw = list(parts), len(parts)
        self.out_shapes = [_sds(p.shape, p.dtype) for p in parts]
        self.sem_shapes = [pltpu.SemaphoreType.DMA((nw,)), pltpu.SemaphoreType.DMA((nw,))]

    def _copies(self, ins, outs, sems):
        send_sems, recv_sems = sems
        x, y, c = _place()
        return [pltpu.make_async_remote_copy(
            src_ref=ins[j], dst_ref=outs[j], send_sem=send_sems.at[j], recv_sem=recv_sems.at[j],
            device_id=(x, y, 1 - c), device_id_type=MESH) for j in range(len(ins))]

    def start(self, ins, outs, sems):
        for cp in self._copies(ins, outs, sems):
            cp.start()

    def relay(self, ins, outs, sems):
        pass

    def wait(self, ins, outs, sems):
        for cp in self._copies(ins, outs, sems):
            cp.wait()


class _SmallGatherPlan:
    def __init__(self, v):
        self.inputs = [v]
        self.out_shapes = [_sds((8,) + v.shape, v.dtype)]
        self.sem_shapes = [pltpu.SemaphoreType.DMA((1,)), pltpu.SemaphoreType.DMA((7,)), pltpu.SemaphoreType.DMA((7,))]

    def _copies(self, ins, outs, sems):
        (v_ref,), (out_ref,), (local_sem, send_sems, recv_sems) = ins, outs, sems
        x, y, c = _place()
        me = 4 * x + 2 * y + c
        local = pltpu.make_async_copy(v_ref, out_ref.at[me], local_sem.at[0])
        sends, arrivals = [], []
        for m in range(1, 8):
            px, py, pc = x ^ (m >> 2), y ^ ((m >> 1) & 1), c ^ (m & 1)
            sends.append(pltpu.make_async_remote_copy(
                src_ref=v_ref, dst_ref=out_ref.at[me], send_sem=send_sems.at[m - 1], recv_sem=recv_sems.at[m - 1],
                device_id=(px, py, pc), device_id_type=MESH))
            arrivals.append(pltpu.make_async_remote_copy(
                src_ref=v_ref, dst_ref=out_ref.at[4 * px + 2 * py + pc], send_sem=send_sems.at[m - 1],
                recv_sem=recv_sems.at[m - 1], device_id=(x, y, c), device_id_type=MESH))
        return local, sends, arrivals

    def start(self, ins, outs, sems):
        local, sends, _ = self._copies(ins, outs, sems)
        for cp in [local] + sends:
            cp.start()

    def relay(self, ins, outs, sems):
        pass

    def wait(self, ins, outs, sems):
        local, sends, arrivals = self._copies(ins, outs, sems)
        for cp in arrivals:
            cp.wait_recv()
        for cp in sends:
            cp.wait_send()
        local.wait()


class _PlanGroup:
    def __init__(self, plans):
        self.plans = [p for p in plans if p is not None]
        self.inputs = [a for p in self.plans for a in p.inputs]
        self.out_shapes = [s for p in self.plans for s in p.out_shapes]
        self.sem_shapes = [s for p in self.plans for s in p.sem_shapes]

    def _each(self, ins, outs, sems):
        i = s = 0
        for p in self.plans:
            n, ns = len(p.inputs), len(p.sem_shapes)
            yield p, ins[i:i + n], outs[i:i + n], sems[s:s + ns]
            i, s = i + n, s + ns

    def start(self, ins, outs, sems):
        for p, pi, po, ps in self._each(ins, outs, sems):
            p.start(pi, po, ps)

    def relay(self, ins, outs, sems):
        for p, pi, po, ps in self._each(ins, outs, sems):
            p.relay(pi, po, ps)

    def wait(self, ins, outs, sems):
        for p, pi, po, ps in self._each(ins, outs, sems):
            p.wait(pi, po, ps)

    def split(self, outs):
        res, i = [], 0
        for p in self.plans:
            res.append(outs[i:i + len(p.inputs)])
            i += len(p.inputs)
        return res


BIG = ("w_ffn1_in", "w_ffn1_out", "w_in", "w_pool_branch", "w_attn_branch", "w_out", "w_ffn2_in", "w_ffn2_out")
BIG_AXIS = {"w_ffn1_in": 1, "w_ffn1_out": 0, "w_in": 1, "w_pool_branch": 1, "w_attn_branch": 1, "w_out": 0,
            "w_ffn2_in": 1, "w_ffn2_out": 0}


class _Sharded:
    fused_scatter = True

    def __init__(self, shards):
        self.shards, self.full, self.recv = shards, {}, {}

    def gather_plan(self, names):
        return _GatherPlan([self.shards[n] for n in names], [BIG_AXIS[n.split("/")[0]] for n in names])

    def gather_now(self, names):
        self.gathered(names, _run_plan(self.gather_plan(names), "gather_" + names[0]))

    def gathered(self, names, outs):
        self.full.update(zip(names, outs))

    def scatter_plan(self, names, grads):
        return _ScatterPlan([grads[n] for n in names], [BIG_AXIS[n] for n in names])

    def scatter_now(self, names, grads):
        self.scattered(names, _run_plan(self.scatter_plan(names, grads), "scatter_" + names[0]))

    def scattered(self, names, outs):
        self.recv.update(zip(names, outs))


class _Whole:
    fused_scatter = False

    def __init__(self, full):
        self.full, self.recv = dict(full), {}

    def gather_plan(self, names):
        return None

    def gather_now(self, names):
        pass

    def gathered(self, names, outs):
        pass

    def scatter_plan(self, names, grads):
        return None

    def scatter_now(self, names, grads):
        pass

    def scattered(self, names, outs):
        pass


def _vec(rows):
    pad = [jnp.zeros((1, D), F32)] * (8 - len(rows))
    return jnp.concatenate([r.reshape(1, D) for r in rows] + pad, axis=0)


def _block_diag(w_pool):
    n, c = w_pool.shape[0], w_pool.shape[1]
    eye = jnp.eye(n, dtype=w_pool.dtype)
    return (eye[:, None, :, None] * w_pool[:, :, None, :]).reshape(n * c, n * c)


def _example_step(x, tgt, positions, mod, gains, w_pool, pool_scale, ws, pack=None):
    T = x.shape[0]
    assert (T // BLK // DIL[-1]) & (T // BLK // DIL[-1] - 1) == 0, "blocks per sequence must be a power of two"
    sh1, sc1, gt1, sh2, sc2, gt2, sh3, sc3, gt3 = [mod[j * D:(j + 1) * D] for j in range(NMOD)]
    g1, g2, g3, gf = gains
    vec1, vec2, vec3 = _vec([g1, sh1, sc1, gt1]), _vec([g2, sh2, sc2, gt2]), _vec([g3, sh3, sc3, gt3])
    inv_freq = 10000.0 ** (-jnp.arange(0, HD, 2, dtype=F32) / HD)
    ang = positions.astype(F32)[:, None] * inv_freq
    cos = jnp.tile(jnp.cos(ang), (1, 4))
    sin = jnp.tile(jnp.concatenate([-jnp.sin(ang), jnp.sin(ang)], axis=1), (1, 2))
    wp_bd = _block_diag(w_pool).astype(BF16)
    ones_bd = _block_diag(jnp.ones((NH, HD, HD), F32)).astype(BF16)
    ps = jnp.concatenate([pool_scale.reshape(1, PW), jnp.zeros((7, PW), F32)], axis=0)
    wb = ws.full

    if "w_ffn1_in" not in wb:
        ws.gather_now(["w_ffn1_in"])
    (u1, a1, b1), got = _ffn_ab(x, vec1, [wb["w_ffn1_in"]], "ffn1_ab", ws.gather_plan(["w_ffn1_out"]))
    ws.gathered(["w_ffn1_out"], got)
    mixw = ["w_in", "w_pool_branch", "w_attn_branch", "w_out"]
    (h1, f1), got = _ffn_out(x, a1, b1, vec1, wb["w_ffn1_out"], "ffn1_out", ws.gather_plan(mixw))
    ws.gathered(mixw, got)
    (u2, p, qs, ks, vs, gates), got = _mix_proj(h1, vec2, wb["w_in"], cos, sin, ws.gather_plan(["w_ffn2_in/0"]))
    ws.gathered(["w_ffn2_in/0"], got)
    qs, ks, vs = [_flat(t) for t in qs], [_flat(t) for t in ks], [_flat(t) for t in vs]
    nbs = [T // d // BLK for d in DIL]
    os, lses = [], []
    for gi, riders in enumerate((["w_ffn2_out"], ["w_ffn2_in/1"], None)):
        (o, lse), got = _attn_fwd(qs[gi], ks[gi], vs[gi], nbs[gi], f"attn_fwd{gi}", riders and ws.gather_plan(riders))
        ws.gathered(riders or [], got)
        os.append(o)
        lses.append(lse)
    win3 = [wb["w_ffn2_in/0"], wb["w_ffn2_in/1"]] if "w_ffn2_in/0" in wb else [wb["w_ffn2_in"]]
    os_r = [_by_residue(t, d) for t, d in zip(os, DIL)]
    lses_r = [_by_residue(t, d) for t, d in zip(lses, DIL)]
    h2, ypool, yattn, merged, mixout, dpool = _mix_merge(
        h1, vec2, p, os_r, lses_r, gates, wp_bd, ps, wb["w_pool_branch"], wb["w_attn_branch"], wb["w_out"])
    (dh3, u3, a3, b3, f3, lacc), _ = _ffn_fwd(h2, vec3, win3, wb["w_ffn2_out"], "ffn2_fwd", head=(tgt, _vec([gf])))
    loss = 0.5 * jnp.sum(lacc[0]) / D

    grads = {}

    def wgrad_cols(name, xx, yy, riders, extra=None):
        group = _PlanGroup([ws.scatter_plan(riders, grads) if riders else None, extra])
        plan = group if group.plans else None
        if ws.fused_scatter:
            blocks, got = _wgrad_scatter(xx, yy, "wg_" + name, min(2048, T // 2), comm=plan)
            ws.scattered([name], [blocks])
        else:
            grads[name], got = _wgrad(xx, yy, "wg_" + name, D, 512, 1024, comm=plan)
        parts = group.split(got)
        if len(parts) > (extra is not None):
            ws.scattered(riders, parts[0])
        return parts[-1] if extra is not None else None

    (dh2, dab3, s3, df3, acc3), _ = _ffn_bwd(dh3, h2, a3, b3, f3, vec3, win3, wb["w_ffn2_out"], "ffn2_bwd")
    grads["w_ffn2_out"], _ = _wgrad(s3, df3, "wg_ffn2_out", FF // 2, 512, 1024)
    wgrad_cols("w_ffn2_in", u3, dab3, ["w_ffn2_out"])
    (dgates, do0, do1, do2, e0, e1, e2, dd, acc2a, accps,
     grads["w_out"], grads["w_pool_branch"], grads["w_attn_branch"], gwp), _ = _mix_bwd_a(
        dh2, vec2, mixout, merged, gates, ypool, yattn, dpool, os_r, lses_r, wp_bd, ps,
        wb["w_pool_branch"], wb["w_attn_branch"], wb["w_out"], ones_bd)
    n = len(POOL_WINDOWS)
    c = PW // n
    grad_w_pool = jnp.stack([gwp[j * c:(j + 1) * c, j * c:(j + 1) * c] for j in range(n)], axis=0)
    small3 = ["w_out", "w_pool_branch", "w_attn_branch"]
    dqs, dks, dvs = [], [], []
    for gi, (do, e) in enumerate(((do0, e0), (do1, e1), (do2, e2))):
        plan = ws.scatter_plan(small3, grads) if gi == 0 else None
        (dq, dk, dv), got = _attn_bwd(qs[gi], ks[gi], vs[gi], _flat(do), lses[gi], _flat(e), nbs[gi], f"attn_bwd{gi}", plan)
        if gi == 0:
            ws.scattered(small3, got)
        dqs.append(_by_residue(dq, DIL[gi]))
        dks.append(_by_residue(dk, DIL[gi]))
        dvs.append(_by_residue(dv, DIL[gi]))
    dh1, dproj, acc2b = _mix_bwd_b(dh2, h1, vec2, dd, dqs, dks, dvs, dgates, cos, sin, wb["w_in"])
    wgrad_cols("w_in", u2, dproj, [])
    (dx, dab1, s1, df1, acc1), _ = _ffn_bwd(dh1, x, a1, b1, f1, vec1, [wb["w_ffn1_in"]], wb["w_ffn1_out"], "ffn1_bwd")
    grads["w_ffn1_out"], _ = _wgrad(s1, df1, "wg_ffn1_out", FF // 2, 512, 1024)
    dmod = jnp.concatenate([acc1[0], acc1[1], acc1[3], acc2b[0], acc2b[1], acc2a[3], acc3[0], acc3[1], acc3[3]])
    dgains = jnp.stack([acc1[2], acc2b[2], acc3[2], lacc[1]], axis=0)
    row = None if pack is None else _SmallGatherPlan(pack(loss, dmod, dgains, grad_w_pool, accps[0]))
    rows = wgrad_cols("w_ffn1_in", u1, dab1, ["w_ffn1_out"], row)
    return loss, dx, dmod, dgains, grad_w_pool, accps[0], grads, None if rows is None else rows[0]


SMALL = ("b_ada", "g_norm_ffn1", "g_norm_mix", "g_norm_ffn2", "g_final", "pool_scale", "w_pool")
WEIGHTS = ("w_ada", "b_ada", "g_norm_ffn1", "w_ffn1_in", "w_ffn1_out", "g_norm_mix", "w_in", "w_pool", "pool_scale",
           "w_pool_branch", "w_attn_branch", "w_out", "g_norm_ffn2", "w_ffn2_in", "w_ffn2_out", "g_final")


def _pack_small(t):
    return jnp.concatenate([t[n].reshape(-1) for n in SMALL]).reshape(1, -1)


def kernel(x, c, positions, w_ada, b_ada, g_norm_ffn1, w_ffn1_in, w_ffn1_out, g_norm_mix, w_in, w_pool, pool_scale, w_pool_branch, w_attn_branch, w_out, g_norm_ffn2, w_ffn2_in, w_ffn2_out, g_final, loss_target, m_w_ada, m_b_ada, m_g_norm_ffn1, m_w_ffn1_in, m_w_ffn1_out, m_g_norm_mix, m_w_in, m_w_pool, m_pool_scale, m_w_pool_branch, m_w_attn_branch, m_w_out, m_g_norm_ffn2, m_w_ffn2_in, m_w_ffn2_out, m_g_final, v_w_ada, v_b_ada, v_g_norm_ffn1, v_w_ffn1_in, v_w_ffn1_out, v_g_norm_mix, v_w_in, v_w_pool, v_pool_scale, v_w_pool_branch, v_w_attn_branch, v_w_out, v_g_norm_ffn2, v_w_ffn2_in, v_w_ffn2_out, v_g_final):
    w = dict(w_ada=w_ada, b_ada=b_ada, g_norm_ffn1=g_norm_ffn1, w_ffn1_in=w_ffn1_in, w_ffn1_out=w_ffn1_out,
             g_norm_mix=g_norm_mix, w_in=w_in, w_pool=w_pool, pool_scale=pool_scale, w_pool_branch=w_pool_branch,
             w_attn_branch=w_attn_branch, w_out=w_out, g_norm_ffn2=g_norm_ffn2, w_ffn2_in=w_ffn2_in,
             w_ffn2_out=w_ffn2_out, g_final=g_final)
    mom = dict(w_ada=m_w_ada, b_ada=m_b_ada, g_norm_ffn1=m_g_norm_ffn1, w_ffn1_in=m_w_ffn1_in, w_ffn1_out=m_w_ffn1_out,
               g_norm_mix=m_g_norm_mix, w_in=m_w_in, w_pool=m_w_pool, pool_scale=m_pool_scale,
               w_pool_branch=m_w_pool_branch, w_attn_branch=m_w_attn_branch, w_out=m_w_out, g_norm_ffn2=m_g_norm_ffn2,
               w_ffn2_in=m_w_ffn2_in, w_ffn2_out=m_w_ffn2_out, g_final=m_g_final)
    var = dict(w_ada=v_w_ada, b_ada=v_b_ada, g_norm_ffn1=v_g_norm_ffn1, w_ffn1_in=v_w_ffn1_in, w_ffn1_out=v_w_ffn1_out,
               g_norm_mix=v_g_norm_mix, w_in=v_w_in, w_pool=v_w_pool, pool_scale=v_pool_scale,
               w_pool_branch=v_w_pool_branch, w_attn_branch=v_w_attn_branch, w_out=v_w_out, g_norm_ffn2=v_g_norm_ffn2,
               w_ffn2_in=v_w_ffn2_in, w_ffn2_out=v_w_ffn2_out, g_final=v_g_final)
    ix, iy, ic = _place()
    chip = 2 * ix + iy
    me = 4 * ix + 2 * iy + ic
    nada = w_ada.shape[2]

    shards = {n: w[n][0].astype(BF16) for n in BIG}
    half = D // 2
    shards["w_ffn2_in/0"], shards["w_ffn2_in/1"] = shards["w_ffn2_in"][:half], shards["w_ffn2_in"][half:]
    ws = _Sharded(shards)
    c_all = _run_plan(_SmallGatherPlan(c), "gather_c")[0][:, 0, :]
    b_shard = lax.dynamic_slice_in_dim(b_ada, chip * nada, nada, axis=1)
    mod_cols = _ada_fwd(c_all, w_ada[0], b_shard)
    first = _PlanGroup([_SmallGatherPlan(mod_cols), ws.gather_plan(["w_ffn1_in"])])
    (mod_all,), ffn1 = first.split(_run_plan(first, "gather_first"))
    ws.gathered(["w_ffn1_in"], ffn1)
    mod = jnp.concatenate([lax.dynamic_index_in_dim(mod_all[4 * (kk >> 1) + 2 * (kk & 1)], me, axis=0, keepdims=False)
                           for kk in range(4)])

    def pack(loss, dmod, dgains, g_w_pool, g_pool_scale):
        small_g = dict(b_ada=dmod, g_norm_ffn1=dgains[0], g_norm_mix=dgains[1], g_norm_ffn2=dgains[2],
                       g_final=dgains[3], pool_scale=g_pool_scale, w_pool=g_w_pool)
        return jnp.concatenate([_pack_small(small_g), jnp.pad(loss.reshape(1, 1), ((0, 0), (0, 127)))], axis=1)

    _, dx, _, _, _, _, _, gathered = _example_step(
        x[0], loss_target[0], positions[0], mod, (g_norm_ffn1[0], g_norm_mix[0], g_norm_ffn2[0], g_final),
        w_pool[0], pool_scale[0], ws, pack)

    per_weight, loss_tile = _adam_small(*[[t[n].reshape(1, -1) for n in SMALL] for t in (w, mom, var)], gathered)
    small_out = [{n: per_weight[j][kind].reshape(w[n].shape) for j, n in enumerate(SMALL)} for kind in range(4)]
    loss = loss_tile[0, 0]

    dmod_all = gathered[:, 0, :NMOD * D]
    dmod_cols = lax.dynamic_slice_in_dim(dmod_all, chip * nada, nada, axis=1)
    g_ada = _ada_bwd(c_all, dmod_cols)

    ada_out = _adam(w_ada[0], m_w_ada[0], v_w_ada[0], [g_ada], "adam_w_ada")[0]

    sums = {n: _sum4(ws.recv[n], "sum_" + n) for n in BIG}
    other = dict(zip(BIG, _run_plan(_SwapPlan([sums[n] for n in BIG]), "swap_sibling")))
    big_out = {}
    for n in BIG:
        if sums[n].shape[0] < w[n].shape[1]:
            big_out[n] = _adam_halves(w[n][0], mom[n][0], var[n][0], sums[n], other[n], "adam_" + n)
        else:
            big_out[n] = _adam(w[n][0], mom[n][0], var[n][0], [sums[n], other[n]], "adam_" + n)[0]

    def leaf(kind, n):
        if n == "w_ada":
            return ada_out[kind][None]
        if n in big_out:
            return big_out[n][kind][None]
        return small_out[kind][n]

    return (loss, dx[None], *[leaf(kind, n) for kind in range(4) for n in WEIGHTS])
```

```python
import jax
import jax.numpy as jnp
from jax import lax
from jax.experimental import pallas as pl
from jax.experimental.pallas import tpu as pltpu

F32 = jnp.float32
BF16 = jnp.bfloat16

D = 1024
FF = 2816
FC = FF
PW = 256
GA = 256
HD = 64
LANES = 128
NH = GA // HD
NG = 3
DIL = (1, 4, 16)
BLK = 128
GW = 2 * D
INW = PW + 3 * NG * GA + GW
NMOD = 9
POOL_WINDOWS = (2, 4, 8, 16)
HALO = 16
EPS = 1e-6
SCALE = HD ** -0.5
NEG = -1e30

LR, B1, B2, AEPS, WD, STEP = 0.001, 0.9, 0.999, 1e-08, 0.01, 10

VMEM_BIG = 56 * 1024 * 1024
TM = 256

MESH = pl.DeviceIdType.MESH
ANY = pl.BlockSpec(memory_space=pl.ANY)


def _call(body, name, grid, in_specs, out_specs, out_shape, scratch=(), vmem=None, comm=None):
    params = pltpu.CompilerParams(dimension_semantics=("arbitrary",) * len(grid), vmem_limit_bytes=vmem)
    n_in, n_out, n_scr = len(in_specs), len(out_shape), len(scratch)
    if comm is None:
        call = pl.pallas_call(body, name=name, grid=grid, in_specs=list(in_specs), out_specs=list(out_specs),
                              out_shape=list(out_shape), scratch_shapes=list(scratch), compiler_params=params)
        return lambda *args: (call(*args), ())
    nc = len(comm.inputs)

    def body_with_comm(*refs):
        ins, refs = refs[:n_in], refs[n_in:]
        c_ins, refs = refs[:nc], refs[nc:]
        outs, refs = refs[:n_out], refs[n_out:]
        c_outs, refs = refs[:nc], refs[nc:]
        scr, sems = refs[:n_scr], refs[n_scr:]
        first = pl.program_id(0) == 0
        last = pl.program_id(0) == grid[0] - 1
        for ax in range(1, len(grid)):
            first = jnp.logical_and(first, pl.program_id(ax) == 0)
            last = jnp.logical_and(last, pl.program_id(ax) == grid[ax] - 1)

        @pl.when(first)
        def _():
            comm.start(c_ins, c_outs, sems)

        body(*ins, *outs, *scr)
        early_relay = len(grid) == 1 and grid[0] >= 4
        if early_relay:
            @pl.when(pl.program_id(0) == (3 * grid[0]) // 4)
            def _():
                comm.relay(c_ins, c_outs, sems)

        @pl.when(last)
        def _():
            if not early_relay:
                comm.relay(c_ins, c_outs, sems)
            comm.wait(c_ins, c_outs, sems)

    call = pl.pallas_call(
        body_with_comm, name=name, grid=grid, in_specs=list(in_specs) + [ANY] * nc,
        out_specs=list(out_specs) + [ANY] * nc, out_shape=list(out_shape) + list(comm.out_shapes),
        scratch_shapes=list(scratch) + list(comm.sem_shapes), compiler_params=params)

    def run(*args):
        res = call(*args, *comm.inputs)
        return res[:n_out], res[n_out:]

    return run


def _rows(tm, n):
    return pl.BlockSpec((tm, n), lambda i: (i, 0))


def _const(shape):
    return pl.BlockSpec(shape, lambda i: (0,) * len(shape))


def _sds(shape, dtype):
    return jax.ShapeDtypeStruct(shape, dtype)


def _dot(a, b):
    return jnp.dot(a, b, preferred_element_type=F32)


def _dot_nt(a, b):
    return lax.dot_general(a, b, (((1,), (1,)), ((), ())), preferred_element_type=F32)


def _dot_tn(a, b):
    return lax.dot_general(a, b, (((0,), (0,)), ((), ())), preferred_element_type=F32)


def _colsum(v):
    return jnp.sum(v, axis=0, keepdims=True)


def _norm_fwd(h, g, sh, sc):
    r = lax.rsqrt(jnp.mean(h * h, axis=-1, keepdims=True) + EPS)
    xh = h * r
    n = xh * g
    return xh, r, n, n * (1.0 + sc) + sh


def _norm_bwd(du, xh, r, n, g, sc):
    dn = du * (1.0 + sc)
    dxh = dn * g
    dh = r * (dxh - xh * jnp.mean(dxh * xh, axis=-1, keepdims=True))
    return dh, _colsum(du), _colsum(du * n), _colsum(dn * xh)


def _load_once(pairs, sems):
    @pl.when(pl.program_id(0) == 0)
    def _():
        cps = [pltpu.make_async_copy(src, dst, sems.at[j]) for j, (src, dst) in enumerate(pairs)]
        for cp in cps:
            cp.start()
        for cp in cps:
            cp.wait()


def _zero_first(ref):
    @pl.when(pl.program_id(0) == 0)
    def _():
        ref[...] = jnp.zeros(ref.shape, ref.dtype)


def _row_chunks(hbm_refs, vmem_ref):
    pairs, row = [], 0
    for ref in hbm_refs:
        pairs.append((ref, vmem_ref.at[pl.ds(row, ref.shape[0]), :]))
        row += ref.shape[0]
    return pairs


def _loss_head(hh, tgt, g):
    r = lax.rsqrt(jnp.mean(hh * hh, axis=-1, keepdims=True) + EPS)
    xh = hh * r
    err = xh * g - tgt
    dy = err * (1.0 / D)
    dxh = dy * g
    dh = r * (dxh - xh * jnp.mean(dxh * xh, axis=-1, keepdims=True))
    return dh, _colsum(err * err), _colsum(dy * xh)


def _ffn_fwd(h, vec, wins, wout, name, comm=None, head=None):
    T = h.shape[0]
    nwin = len(wins)
    nhead = 0 if head is None else 2

    def body(h_ref, vec_ref, *rest):
        head_refs, rest = rest[:nhead], rest[nhead:]
        win_hbms, rest = rest[:nwin], rest[nwin:]
        (wout_hbm, ho_ref, u_ref, a_ref, b_ref, f_ref), rest = rest[:6], rest[6:]
        lacc_refs, (win_v, wout_v, sems) = rest[:nhead // 2], rest[nhead // 2:]
        _load_once(_row_chunks(win_hbms, win_v) + [(wout_hbm, wout_v)], sems)
        hh = h_ref[...]
        g, sh, sc, gt = vec_ref[0:1, :], vec_ref[1:2, :], vec_ref[2:3, :], vec_ref[3:4, :]
        _, _, _, u = _norm_fwd(hh, g, sh, sc)
        ub = u.astype(BF16)
        u_ref[...] = ub
        acc = None
        for j in range(FF // FC):
            lo, hi = j * FC, (j + 1) * FC
            a = _dot(ub, win_v[:, lo:hi])
            b = _dot(ub, win_v[:, FF + lo:FF + hi])
            a_ref[:, lo:hi] = a.astype(BF16)
            b_ref[:, lo:hi] = b.astype(BF16)
            s = (a * jax.nn.sigmoid(a) * b).astype(BF16)
            part = _dot(s, wout_v[lo:hi, :])
            acc = part if acc is None else acc + part
        f_ref[...] = acc.astype(BF16)
        ho = hh + 0.5 * gt * acc
        if head is None:
            ho_ref[...] = ho
        else:
            _zero_first(lacc_refs[0])
            dh, sq, dg = _loss_head(ho, head_refs[0][...], head_refs[1][0:1, :])
            ho_ref[...] = dh
            lacc_refs[0][0:1, :] += sq
            lacc_refs[0][1:2, :] += dg

    head_specs = [] if head is None else [_rows(TM, D), _const((8, D))]
    lacc_spec = [] if head is None else [_const((8, D))]
    lacc_shape = [] if head is None else [_sds((8, D), F32)]
    return _call(
        body, name, (T // TM,),
        [_rows(TM, D), _const((8, D))] + head_specs + [ANY] * (nwin + 1),
        [_rows(TM, D), _rows(TM, D), _rows(TM, FF), _rows(TM, FF), _rows(TM, D)] + lacc_spec,
        [_sds((T, D), F32), _sds((T, D), BF16), _sds((T, FF), BF16), _sds((T, FF), BF16), _sds((T, D), BF16)] + lacc_shape,
        scratch=[pltpu.VMEM((D, 2 * FF), BF16), pltpu.VMEM((FF, D), BF16), pltpu.SemaphoreType.DMA((nwin + 1,))],
        vmem=VMEM_BIG, comm=comm,
    )(h, vec, *([] if head is None else head), *wins, wout)


def _ffn_ab(h, vec, wins, name, comm=None):
    T = h.shape[0]
    nwin = len(wins)

    def body(h_ref, vec_ref, *rest):
        win_hbms, (u_ref, a_ref, b_ref, win_v, sems) = rest[:nwin], rest[nwin:]
        _load_once(_row_chunks(win_hbms, win_v), sems)
        g, sh, sc = vec_ref[0:1, :], vec_ref[1:2, :], vec_ref[2:3, :]
        _, _, _, u = _norm_fwd(h_ref[...], g, sh, sc)
        ub = u.astype(BF16)
        u_ref[...] = ub
        for j in range(FF // FC):
            lo, hi = j * FC, (j + 1) * FC
            a_ref[:, lo:hi] = _dot(ub, win_v[:, lo:hi]).astype(BF16)
            b_ref[:, lo:hi] = _dot(ub, win_v[:, FF + lo:FF + hi]).astype(BF16)

    return _call(
        body, name, (T // TM,),
        [_rows(TM, D), _const((8, D))] + [ANY] * nwin,
        [_rows(TM, D), _rows(TM, FF), _rows(TM, FF)],
        [_sds((T, D), BF16), _sds((T, FF), BF16), _sds((T, FF), BF16)],
        scratch=[pltpu.VMEM((D, 2 * FF), BF16), pltpu.SemaphoreType.DMA((nwin,))],
        vmem=VMEM_BIG, comm=comm,
    )(h, vec, *wins)


def _ffn_out(h, a, b, vec, wout, name, comm=None):
    T = h.shape[0]

    def body(h_ref, a_ref, b_ref, vec_ref, wout_hbm, ho_ref, f_ref, wout_v, sems):
        _load_once([(wout_hbm, wout_v)], sems)
        gt = vec_ref[3:4, :]
        acc = None
        for j in range(FF // FC):
            lo, hi = j * FC, (j + 1) * FC
            av = a_ref[:, lo:hi].astype(F32)
            s = (av * jax.nn.sigmoid(av) * b_ref[:, lo:hi].astype(F32)).astype(BF16)
            part = _dot(s, wout_v[lo:hi, :])
            acc = part if acc is None else acc + part
        f_ref[...] = acc.astype(BF16)
        ho_ref[...] = h_ref[...] + 0.5 * gt * acc

    return _call(
        body, name, (T // TM,),
        [_rows(TM, D), _rows(TM, FF), _rows(TM, FF), _const((8, D)), ANY],
        [_rows(TM, D), _rows(TM, D)],
        [_sds((T, D), F32), _sds((T, D), BF16)],
        scratch=[pltpu.VMEM((FF, D), BF16), pltpu.SemaphoreType.DMA((1,))],
        vmem=VMEM_BIG, comm=comm,
    )(h, a, b, vec, wout)


def _ffn_bwd(dh, h, a, b, f, vec, wins, wout, name, comm=None):
    T = h.shape[0]
    nwin = len(wins)

    def body(dh_ref, h_ref, a_ref, b_ref, f_ref, vec_ref, *rest):
        win_hbms, (wout_hbm, dhi_ref, dab_ref, s_ref, df_ref, acc_ref, win_v, wout_v, sems) = rest[:nwin], rest[nwin:]
        _load_once(_row_chunks(win_hbms, win_v) + [(wout_hbm, wout_v)], sems)
        _zero_first(acc_ref)
        g, sh, sc, gt = vec_ref[0:1, :], vec_ref[1:2, :], vec_ref[2:3, :], vec_ref[3:4, :]
        dho = dh_ref[...]
        df = (0.5 * gt * dho).astype(BF16)
        df_ref[...] = df
        dgt = _colsum(0.5 * dho * f_ref[...].astype(F32))
        du = None
        for j in range(FF // FC):
            lo, hi = j * FC, (j + 1) * FC
            av = a_ref[:, lo:hi].astype(F32)
            bv = b_ref[:, lo:hi].astype(F32)
            ds = _dot_nt(df, wout_v[lo:hi, :])
            sig = jax.nn.sigmoid(av)
            sa = av * sig
            s_ref[:, lo:hi] = (sa * bv).astype(BF16)
            da = (ds * bv * (sig * (1.0 + av * (1.0 - sig)))).astype(BF16)
            db = (ds * sa).astype(BF16)
            dab_ref[:, lo:hi] = da
            dab_ref[:, FF + lo:FF + hi] = db
            part = _dot_nt(da, win_v[:, lo:hi]) + _dot_nt(db, win_v[:, FF + lo:FF + hi])
            du = part if du is None else du + part
        xh, r, n, _ = _norm_fwd(h_ref[...], g, sh, sc)
        dhn, dsh, dsc, dg = _norm_bwd(du, xh, r, n, g, sc)
        dhi_ref[...] = dho + dhn
        acc_ref[0:1, :] += dsh
        acc_ref[1:2, :] += dsc
        acc_ref[2:3, :] += dg
        acc_ref[3:4, :] += dgt

    return _call(
        body, name, (T // TM,),
        [_rows(TM, D), _rows(TM, D), _rows(TM, FF), _rows(TM, FF), _rows(TM, D), _const((8, D))] + [ANY] * (nwin + 1),
        [_rows(TM, D), _rows(TM, 2 * FF), _rows(TM, FF), _rows(TM, D), _const((8, D))],
        [_sds((T, D), F32), _sds((T, 2 * FF), BF16), _sds((T, FF), BF16), _sds((T, D), BF16), _sds((8, D), F32)],
        scratch=[pltpu.VMEM((D, 2 * FF), BF16), pltpu.VMEM((FF, D), BF16), pltpu.SemaphoreType.DMA((nwin + 1,))],
        vmem=VMEM_BIG, comm=comm,
    )(dh, h, a, b, f, vec, *wins, wout)


def _wgrad(x, y, name, tk, tn, tt, out_dtype=BF16, comm=None):
    T, K = x.shape
    N = y.shape[1]
    nt = T // tt

    def body(x_ref, y_ref, o_ref, acc_ref):
        t = pl.program_id(2)
        part = _dot_tn(x_ref[...], y_ref[...])

        @pl.when(t == 0)
        def _():
            acc_ref[...] = part

        @pl.when(t > 0)
        def _():
            acc_ref[...] += part

        @pl.when(t == nt - 1)
        def _():
            o_ref[...] = acc_ref[...].astype(out_dtype)

    (out,), c_outs = _call(
        body, name, (K // tk, N // tn, nt),
        [pl.BlockSpec((tt, tk), lambda i, j, t: (t, i)), pl.BlockSpec((tt, tn), lambda i, j, t: (t, j))],
        [pl.BlockSpec((tk, tn), lambda i, j, t: (i, j))], [_sds((K, N), out_dtype)],
        scratch=[pltpu.VMEM((tk, tn), F32)], vmem=VMEM_BIG, comm=comm,
    )(x, y)
    return out, c_outs


def _wgrad_scatter(x, y, name, tt, comm=None):
    T, K = x.shape
    n = y.shape[1] // 4
    nt = T // tt
    assert nt >= 2, "a block's hand-over is added one grid step into the next block"
    half = K // 2
    nc = 0 if comm is None else len(comm.inputs)

    def body(chip_ref, x_ref, y_ref, *refs):
        c_ins, refs = refs[:nc], refs[nc:]
        recv_ref, refs = refs[0], refs[1:]
        c_outs, refs = refs[:nc], refs[nc:]
        acc_ref, keep_ref, give_ref, take_ref, local_sem, give_sems, take_sems, send_sems, recv_sems = refs[:9]
        j, t = pl.program_id(0), pl.program_id(1)
        px, py, pc = _place()

        def hand_over(jj):
            return pltpu.make_async_remote_copy(
                src_ref=give_ref.at[jj], dst_ref=take_ref.at[jj], send_sem=give_sems.at[jj], recv_sem=take_sems.at[jj],
                device_id=(px, py, 1 - pc), device_id_type=MESH)

        def send(jj):
            m = jj + 1
            return pltpu.make_async_remote_copy(
                src_ref=keep_ref.at[jj], dst_ref=recv_ref.at[m], send_sem=send_sems.at[jj], recv_sem=recv_sems.at[jj],
                device_id=_chip_peer(px, py, pc, m), device_id_type=MESH)

        def add_sibling(jj):
            hand_over(jj).wait_recv()
            keep_ref[jj] = (keep_ref[jj].astype(F32) + take_ref[jj].astype(F32)).astype(BF16)

        if comm is not None:
            @pl.when(jnp.logical_and(j == 0, t == 0))
            def _():
                comm.start(c_ins, c_outs, refs[9:])

        part = _dot_tn(x_ref[...], y_ref[...])

        @pl.when(t == 0)
        def _():
            acc_ref[...] = part

        @pl.when(t > 0)
        def _():
            acc_ref[...] += part

        for jj in range(3):
            @pl.when(jnp.logical_and(j == jj + 1, t == 0))
            def _():
                add_sibling(jj)
                send(jj).start()

        for jj in range(4):
            @pl.when(jnp.logical_and(j == jj, t == nt - 1))
            def _():
                keep_ref[jj] = acc_ref[pl.ds(pl.multiple_of(pc * half, 16), half), :].astype(BF16)
                give_ref[jj] = acc_ref[pl.ds(pl.multiple_of((1 - pc) * half, 16), half), :].astype(BF16)
                hand_over(jj).start()

        @pl.when(jnp.logical_and(j == 3, t == nt - 1))
        def _():
            add_sibling(3)
            own = pltpu.make_async_copy(keep_ref.at[3], recv_ref.at[0], local_sem.at[0])
            own.start()
            for jj in range(3):
                send(jj).wait_recv()
            for jj in range(3):
                send(jj).wait_send()
            for jj in range(4):
                hand_over(jj).wait_send()
            own.wait()
            if comm is not None:
                comm.relay(c_ins, c_outs, refs[9:])
                comm.wait(c_ins, c_outs, refs[9:])

    grid_spec = pltpu.PrefetchScalarGridSpec(
        num_scalar_prefetch=1, grid=(4, nt),
        in_specs=[pl.BlockSpec((tt, K), lambda j, t, chip: (t, 0)),
                  pl.BlockSpec((tt, n), lambda j, t, chip: (t, chip[0] ^ ((j + 1) & 3)))] + [ANY] * nc,
        out_specs=[ANY] * (1 + nc),
        scratch_shapes=[pltpu.VMEM((K, n), F32)] + [pltpu.VMEM((4, half, n), BF16)] * 3
        + [pltpu.SemaphoreType.DMA((1,))] + [pltpu.SemaphoreType.DMA((4,))] * 2 + [pltpu.SemaphoreType.DMA((3,))] * 2
        + ([] if comm is None else list(comm.sem_shapes)))
    px, py, _ = _place()
    res = pl.pallas_call(
        body, name=name, grid_spec=grid_spec,
        out_shape=[_sds((4, half, n), BF16)] + ([] if comm is None else list(comm.out_shapes)),
        compiler_params=pltpu.CompilerParams(dimension_semantics=("arbitrary", "arbitrary"), vmem_limit_bytes=VMEM_BIG),
    )((2 * px + py).astype(jnp.int32).reshape(1), x, y, *([] if comm is None else comm.inputs))
    return res[0], res[1:]


def _swap_halves(t):
    w = t.shape[1]
    lane = lax.broadcasted_iota(jnp.int32, t.shape, 1)
    return jnp.where(lane % HD < HD // 2, pltpu.roll(t, w - HD // 2, 1), pltpu.roll(t, HD // 2, 1))


def _rope(t, cos, sin_signed):
    c = jnp.tile(cos, (1, t.shape[1] // cos.shape[1]))
    s = jnp.tile(sin_signed, (1, t.shape[1] // sin_signed.shape[1]))
    return t * c + _swap_halves(t) * s


def _rope_bwd(dt, cos, sin_signed):
    c = jnp.tile(cos, (1, dt.shape[1] // cos.shape[1]))
    s = jnp.tile(sin_signed, (1, dt.shape[1] // sin_signed.shape[1]))
    return dt * c + _swap_halves(dt * s)


def _rm_spec(dil):
    return pl.BlockSpec((dil, TM // dil, GA), lambda i: (0, i, 0))


def _to_residues(t, dst_ref, scr_ref, dil):
    if dil == 1:
        dst_ref[0] = t.astype(dst_ref.dtype)
        return
    for j in range(GA // LANES):
        scr_ref[j] = t[:, j * LANES:(j + 1) * LANES]
    for r in range(dil):
        for j in range(GA // LANES):
            rows = scr_ref.at[j][pl.ds(r, TM // dil, stride=dil), :]
            dst_ref[r, :, j * LANES:(j + 1) * LANES] = rows.astype(dst_ref.dtype)


def _from_residues(src_ref, scr_ref, dil):
    if dil == 1:
        return src_ref[0].astype(F32)
    for r in range(dil):
        for j in range(GA // LANES):
            scr_ref.at[j][pl.ds(r, TM // dil, stride=dil), :] = src_ref[r, :, j * LANES:(j + 1) * LANES].astype(F32)
    return jnp.concatenate([scr_ref[j] for j in range(GA // LANES)], axis=1)


def _mix_proj(h, vec, win, cos, sin, comm=None):
    T = h.shape[0]

    def body(h_ref, vec_ref, win_hbm, cos_ref, sin_ref, u_ref, p_ref, *rest):
        qkv_refs, gates_ref, win_v, scr_ref, sems = rest[:3 * NG], rest[3 * NG], rest[3 * NG + 1], rest[3 * NG + 2], rest[3 * NG + 3]
        _load_once([(win_hbm, win_v)], sems)
        g, sh, sc = vec_ref[0:1, :], vec_ref[1:2, :], vec_ref[2:3, :]
        _, _, _, u = _norm_fwd(h_ref[...], g, sh, sc)
        ub = u.astype(BF16)
        u_ref[...] = ub
        mixer_cols = PW + 3 * NG * GA
        proj = _dot(ub, win_v[:, 0:mixer_cols])
        p_ref[...] = proj[:, 0:PW]
        cos_t, sin_t = cos_ref[...], sin_ref[...]
        for j in range(3 * NG):
            col = PW + j * GA
            t = proj[:, col:col + GA]
            if j < 2 * NG:
                t = _rope(t, cos_t, sin_t)
            _to_residues(t, qkv_refs[j], scr_ref, DIL[j % NG])
        gates_ref[...] = jax.nn.sigmoid(_dot(ub, win_v[:, mixer_cols:INW])).astype(BF16)

    outs, c_outs = _call(
        body, "mix_proj", (T // TM,),
        [_rows(TM, D), _const((8, D)), ANY, _rows(TM, 128), _rows(TM, 128)],
        [_rows(TM, D), _rows(TM, PW)] + [_rm_spec(d) for d in DIL] * 3 + [_rows(TM, GW)],
        [_sds((T, D), BF16), _sds((T, PW), F32)] + [_sds((d, T // d, GA), BF16) for d in DIL] * 3 + [_sds((T, GW), BF16)],
        scratch=[pltpu.VMEM((D, INW), BF16), pltpu.VMEM((GA // LANES, TM, LANES), F32), pltpu.SemaphoreType.DMA((1,))],
        vmem=VMEM_BIG, comm=comm,
    )(h, vec, win, cos, sin)
    return (outs[0], outs[1], outs[2:2 + NG], outs[2 + NG:2 + 2 * NG], outs[2 + 2 * NG:2 + 3 * NG], outs[2 + 3 * NG]), c_outs


def _head_masks():
    lane_head = lax.broadcasted_iota(jnp.int32, (BLK, GA), 1) // HD
    return [lane_head == hd for hd in range(NH)]


def _expand_heads(t, hm):
    return jnp.concatenate([jnp.where(m, t, jnp.zeros_like(t)) for m in hm], axis=0)


def _collapse_heads(tb, hm):
    out = None
    for hd, m in enumerate(hm):
        part = jnp.where(m, tb[hd * BLK:(hd + 1) * BLK, :], 0.0)
        out = part if out is None else out + part
    return out


def _head_rows(t):
    return jnp.concatenate([t[:, hd * HD:hd * HD + 1] for hd in range(NH)], axis=0)


def _band(has_prev):
    a = lax.broadcasted_iota(jnp.int32, (NH * BLK, 2 * BLK), 0) & (BLK - 1)
    c = lax.broadcasted_iota(jnp.int32, (NH * BLK, 2 * BLK), 1)
    return jnp.logical_and(c >= jnp.where(has_prev, a, BLK), c <= a + BLK)


def _attn_specs(nbt):
    cur = pl.BlockSpec((2 * BLK, GA), lambda i: (i, 0))
    prev = pl.BlockSpec((BLK, GA), lambda i: (jnp.maximum(2 * i - 1, 0), 0))
    nxt = pl.BlockSpec((BLK, GA), lambda i: (jnp.minimum(2 * i + 2, nbt - 1), 0))
    return cur, prev, nxt


def _attn_fwd(q, k, v, nb, name, comm=None):
    T = q.shape[0]
    nbt = T // BLK
    lo, hi = slice(0, BLK), slice(BLK, 2 * BLK)

    def block(qv, kcat, vcat, has_prev, hm):
        s = jnp.where(_band(has_prev), _dot_nt(_expand_heads(qv, hm), kcat) * SCALE, NEG)
        mx = jnp.max(s, axis=-1, keepdims=True)
        e = jnp.exp(s - mx)
        l = jnp.sum(e, axis=-1, keepdims=True)
        ob = _dot((e * (1.0 / l)).astype(BF16), vcat)
        return _collapse_heads(ob, hm), _collapse_heads(jnp.broadcast_to(mx + jnp.log(l), (NH * BLK, GA)), hm)

    def body(q_ref, k_ref, kp_ref, v_ref, vp_ref, o_ref, lse_ref):
        b0 = 2 * pl.program_id(0)
        hm = _head_masks()
        k_first = jnp.concatenate([kp_ref[...], k_ref[lo, :]], axis=0)
        v_first = jnp.concatenate([vp_ref[...], v_ref[lo, :]], axis=0)
        o_ref[lo, :], lse_ref[lo, :] = block(q_ref[lo, :], k_first, v_first, (b0 & (nb - 1)) != 0, hm)
        o_ref[hi, :], lse_ref[hi, :] = block(q_ref[hi, :], k_ref[...], v_ref[...], ((b0 + 1) & (nb - 1)) != 0, hm)

    cur, prev, _ = _attn_specs(nbt)
    return _call(body, name, (nbt // 2,), [cur, cur, prev, cur, prev], [cur, cur],
                 [_sds((T, GA), F32), _sds((T, GA), F32)], comm=comm)(q, k, k, v, v)


def _attn_bwd(q, k, v, do, lse, e, nb, name, comm=None):
    T = q.shape[0]
    nbt = T // BLK

    lo, hi = slice(0, BLK), slice(BLK, 2 * BLK)

    def probs_and_ds(qb, dob, kcat, vcat, lsev, ev, valid):
        p = jnp.where(valid, jnp.exp(_dot_nt(qb, kcat) * SCALE - _head_rows(lsev)), 0.0)
        return p, (p * (_dot_nt(dob, vcat) + _head_rows(ev))).astype(BF16)

    def body(q_ref, k_ref, v_ref, do_ref, lse_ref, e_ref, kp_ref, vp_ref, qn_ref, don_ref, lsen_ref, en_ref,
             dq_ref, dk_ref, dv_ref):
        b0 = 2 * pl.program_id(0)
        hm = _head_masks()
        q1, q2, q3 = _expand_heads(q_ref[lo, :], hm), _expand_heads(q_ref[hi, :], hm), _expand_heads(qn_ref[...], hm)
        do1, do2, do3 = (_expand_heads(do_ref[lo, :], hm), _expand_heads(do_ref[hi, :], hm),
                         _expand_heads(don_ref[...], hm))
        k1 = jnp.concatenate([kp_ref[...], k_ref[lo, :]], axis=0)
        v1 = jnp.concatenate([vp_ref[...], v_ref[lo, :]], axis=0)
        k2, v2 = k_ref[...], v_ref[...]
        p1, ds1 = probs_and_ds(q1, do1, k1, v1, lse_ref[lo, :], e_ref[lo, :], _band((b0 & (nb - 1)) != 0))
        p2, ds2 = probs_and_ds(q2, do2, k2, v2, lse_ref[hi, :], e_ref[hi, :], _band(((b0 + 1) & (nb - 1)) != 0))
        dq_ref[lo, :] = _collapse_heads(_dot(ds1, k1) * SCALE, hm)
        dq_ref[hi, :] = _collapse_heads(_dot(ds2, k2) * SCALE, hm)
        a = lax.broadcasted_iota(jnp.int32, (NH * BLK, BLK), 0) & (BLK - 1)
        c = lax.broadcasted_iota(jnp.int32, (NH * BLK, BLK), 1)
        valid3 = jnp.logical_and(c >= a, ((b0 + 2) & (nb - 1)) != 0)
        p3, ds3 = probs_and_ds(q3, do3, k_ref[hi, :], v_ref[hi, :], lsen_ref[...], en_ref[...], valid3)
        q12, q23 = jnp.concatenate([q1, q2], axis=0), jnp.concatenate([q2, q3], axis=0)
        do12, do23 = jnp.concatenate([do1, do2], axis=0), jnp.concatenate([do2, do3], axis=0)
        dk_ref[lo, :] = _dot_tn(jnp.concatenate([ds1[:, BLK:], ds2[:, :BLK]], axis=0), q12) * SCALE
        dk_ref[hi, :] = _dot_tn(jnp.concatenate([ds2[:, BLK:], ds3], axis=0), q23) * SCALE
        pb1, pb2, pb3 = p1.astype(BF16), p2.astype(BF16), p3.astype(BF16)
        dv_ref[lo, :] = _dot_tn(jnp.concatenate([pb1[:, BLK:], pb2[:, :BLK]], axis=0), do12).astype(BF16)
        dv_ref[hi, :] = _dot_tn(jnp.concatenate([pb2[:, BLK:], pb3], axis=0), do23).astype(BF16)

    cur, prev, nxt = _attn_specs(nbt)
    return _call(body, name, (nbt // 2,), [cur] * 6 + [prev, prev] + [nxt] * 4, [cur, cur, cur],
                 [_sds((T, GA), F32), _sds((T, GA), F32), _sds((T, GA), BF16)],
                 comm=comm)(q, k, v, do, lse, e, k, v, q, do, lse, e)


def _flat(t):
    return t.reshape(t.shape[0] * t.shape[1], t.shape[2])


def _by_residue(t, dil):
    return t.reshape(dil, t.shape[0] // dil, t.shape[1])


def _pool_consts(shape, row0):
    lane = lax.broadcasted_iota(jnp.int32, shape, 1)
    t = lax.broadcasted_iota(jnp.int32, shape, 0) + row0
    grp = lane // (PW // len(POOL_WINDOWS))
    win = jnp.where(grp == 0, POOL_WINDOWS[0], jnp.where(grp == 1, POOL_WINDOWS[1],
                    jnp.where(grp == 2, POOL_WINDOWS[2], POOL_WINDOWS[3])))
    cnt = jnp.minimum(t + 1, win).astype(F32)
    return grp, cnt


def _window_sums(ext_ref, base, step, tm):
    outs, run = [], None
    for j in range(POOL_WINDOWS[-1]):
        sl = ext_ref[pl.ds(base + step * j, tm), :]
        run = sl if run is None else run + sl
        if j + 1 in POOL_WINDOWS:
            outs.append(run)
    return outs


def _select_group(grp, vals):
    return jnp.where(grp == 0, vals[0], jnp.where(grp == 1, vals[1], jnp.where(grp == 2, vals[2], vals[3])))


def _pool_d(pc_ref, pp_ref, ext_ref, i, tm):
    ext_ref[0:HALO, :] = jnp.where(i > 0, pp_ref[tm - HALO:tm, :], 0.0)
    ext_ref[HALO:HALO + tm, :] = pc_ref[...]
    grp, cnt = _pool_consts((tm, PW), i * tm)
    sums = _window_sums(ext_ref, HALO, -1, tm)
    return _select_group(grp, sums) / cnt - pc_ref[...]


def _group_weights(ls):
    mx = jnp.maximum(jnp.maximum(ls[0], ls[1]), ls[2])
    es = [jnp.exp(l - mx) for l in ls]
    inv = 1.0 / (es[0] + es[1] + es[2])
    return [e * inv for e in es]


def _mix_merge(h, vec, p, os, lses, gates, wp_bd, pscale, wpb, wab, wout):
    T = h.shape[0]

    def body(h_ref, vec_ref, pc_ref, pp_ref, o0, o1, o2, l0, l1, l2, gates_ref, wp_ref, ps_ref, wpb_ref, wab_ref, wout_ref,
             ho_ref, yp_ref, ya_ref, mg_ref, mo_ref, d_ref, ext_ref, scr_ref):
        i = pl.program_id(0)
        gt = vec_ref[3:4, :]
        d = _pool_d(pc_ref, pp_ref, ext_ref, i, TM).astype(BF16)
        d_ref[...] = d
        ypool = (_dot(d, wp_ref[...]) * ps_ref[0:1, :]).astype(BF16)
        yp_ref[...] = ypool
        w = _group_weights([_from_residues(r, scr_ref, dl) for r, dl in zip((l0, l1, l2), DIL)])
        yattn = None
        for wg, o_ref, dl in zip(w, (o0, o1, o2), DIL):
            part = wg * _from_residues(o_ref, scr_ref, dl)
            yattn = part if yattn is None else yattn + part
        yattn = yattn.astype(BF16)
        ya_ref[...] = yattn
        merged = (gates_ref[:, 0:D].astype(F32) * _dot(ypool, wpb_ref[...])
                  + gates_ref[:, D:GW].astype(F32) * _dot(yattn, wab_ref[...])).astype(BF16)
        mg_ref[...] = merged
        mo = _dot(merged, wout_ref[...])
        mo_ref[...] = mo.astype(BF16)
        ho_ref[...] = h_ref[...] + gt * mo

    prev = pl.BlockSpec((TM, PW), lambda i: (jnp.maximum(i - 1, 0), 0))
    return _call(
        body, "mix_merge", (T // TM,),
        [_rows(TM, D), _const((8, D)), _rows(TM, PW), prev] + [_rm_spec(dl) for dl in DIL] * 2 + [_rows(TM, GW), _const((PW, PW)),
         _const((8, PW)), _const((PW, D)), _const((GA, D)), _const((D, D))],
        [_rows(TM, D), _rows(TM, PW), _rows(TM, GA), _rows(TM, D), _rows(TM, D), _rows(TM, PW)],
        [_sds((T, D), F32), _sds((T, PW), BF16), _sds((T, GA), BF16), _sds((T, D), BF16), _sds((T, D), BF16), _sds((T, PW), BF16)],
        scratch=[pltpu.VMEM((TM + HALO, PW), F32), pltpu.VMEM((GA // LANES, TM, LANES), F32)],
        vmem=VMEM_BIG,
    )(h, vec, p, p, *os, *lses, gates, wp_bd, pscale, wpb, wab, wout)[0]


def _mix_bwd_a(dh, vec, mixout, merged, gates, ypool, yattn, dpool, os, lses, wp_bd, pscale, wpb, wab, wout, ones_bd,
               comm=None):
    T = dh.shape[0]
    nt = T // TM

    def body(dh_ref, vec_ref, mo_ref, mg_ref, gates_ref, yp_ref, ya_ref, d_ref, o0, o1, o2, l0, l1, l2,
             wp_ref, ps_ref, wpb_ref, wab_ref, wout_ref, ones_ref,
             dgates_ref, do0, do1, do2, e0, e1, e2, dd_ref, acc_ref, acc2_ref, g_out_ref, g_pb_ref, g_ab_ref, g_pool_ref,
             scr_ref, a_out, a_pb, a_ab, a_pool):
        _zero_first(acc_ref)
        _zero_first(acc2_ref)
        for a_ref in (a_out, a_pb, a_ab, a_pool):
            _zero_first(a_ref)
        gt = vec_ref[3:4, :]
        dho = dh_ref[...]
        acc_ref[3:4, :] += _colsum(dho * mo_ref[...].astype(F32))
        dmo = (gt * dho).astype(BF16)
        a_out[...] += _dot_tn(mg_ref[...], dmo)
        dmerged = _dot_nt(dmo, wout_ref[...])
        gp = gates_ref[:, 0:D].astype(F32)
        ga = gates_ref[:, D:GW].astype(F32)
        bp = _dot(yp_ref[...], wpb_ref[...])
        ba = _dot(ya_ref[...], wab_ref[...])
        dgates_ref[:, 0:D] = (dmerged * bp * gp * (1.0 - gp)).astype(BF16)
        dgates_ref[:, D:GW] = (dmerged * ba * ga * (1.0 - ga)).astype(BF16)
        dbp = (dmerged * gp).astype(BF16)
        dba = (dmerged * ga).astype(BF16)
        a_pb[...] += _dot_tn(yp_ref[...], dbp)
        a_ab[...] += _dot_tn(ya_ref[...], dba)
        dypool = _dot_nt(dbp, wpb_ref[...])
        ypre = _dot(d_ref[...], wp_ref[...])
        acc2_ref[0:1, :] += _colsum(dypool * ypre)
        dyp = (dypool * ps_ref[0:1, :]).astype(BF16)
        a_pool[...] += _dot_tn(d_ref[...], dyp)
        dd_ref[...] = _dot_nt(dyp, wp_ref[...])
        dya = _dot_nt(dba, wab_ref[...])
        w = _group_weights([_from_residues(r, scr_ref, dl) for r, dl in zip((l0, l1, l2), DIL)])
        ya = None
        for wg, o_ref, dl in zip(w, (o0, o1, o2), DIL):
            part = wg * _from_residues(o_ref, scr_ref, dl)
            ya = part if ya is None else ya + part
        prod = dya * ya
        hi = prod.astype(BF16)
        lo = (prod - hi.astype(F32)).astype(BF16)
        tot = _dot(hi, ones_ref[...]) + _dot(lo, ones_ref[...])
        for wg, do_ref, e_ref, dl in zip(w, (do0, do1, do2), (e0, e1, e2), DIL):
            _to_residues(wg * dya, do_ref, scr_ref, dl)
            _to_residues(-wg * tot, e_ref, scr_ref, dl)

        @pl.when(pl.program_id(0) == nt - 1)
        def _():
            g_out_ref[...] = a_out[...].astype(BF16)
            g_pb_ref[...] = a_pb[...].astype(BF16)
            g_ab_ref[...] = a_ab[...].astype(BF16)
            g_pool_ref[...] = a_pool[...]

    return _call(
        body, "mix_bwd_a", (nt,),
        [_rows(TM, D), _const((8, D)), _rows(TM, D), _rows(TM, D), _rows(TM, GW), _rows(TM, PW), _rows(TM, GA), _rows(TM, PW)]
        + [_rm_spec(dl) for dl in DIL] * 2
        + [_const((PW, PW)), _const((8, PW)), _const((PW, D)), _const((GA, D)), _const((D, D)), _const((GA, GA))],
        [_rows(TM, GW)] + [_rm_spec(dl) for dl in DIL] * 2 + [_rows(TM, PW), _const((8, D)), _const((8, PW))]
        + [_const((D, D)), _const((PW, D)), _const((GA, D)), _const((PW, PW))],
        [_sds((T, GW), BF16)] + [_sds((dl, T // dl, GA), BF16) for dl in DIL]
        + [_sds((dl, T // dl, GA), F32) for dl in DIL] + [_sds((T, PW), F32), _sds((8, D), F32), _sds((8, PW), F32)]
        + [_sds((D, D), BF16), _sds((PW, D), BF16), _sds((GA, D), BF16), _sds((PW, PW), F32)],
        scratch=[pltpu.VMEM((GA // LANES, TM, LANES), F32), pltpu.VMEM((D, D), F32), pltpu.VMEM((PW, D), F32),
                 pltpu.VMEM((GA, D), F32), pltpu.VMEM((PW, PW), F32)],
        vmem=VMEM_BIG, comm=comm,
    )(dh, vec, mixout, merged, gates, ypool, yattn, dpool, *os, *lses, wp_bd, pscale, wpb, wab, wout, ones_bd)


def _mix_bwd_b(dh, h, vec, dd, dqs, dks, dvs, dgates, cos, sin, win):
    T = h.shape[0]
    nt = T // TM

    def body(dh_ref, h_ref, vec_ref, ddc_ref, ddn_ref, *rest):
        qk_refs, dv_refs = rest[:2 * NG], rest[2 * NG:3 * NG]
        dgates_ref, cos_ref, sin_ref, win_hbm, dhi_ref, dproj_ref, acc_ref, win_v, ext_ref, scr_ref, sems = rest[3 * NG:]
        i = pl.program_id(0)
        _load_once([(win_hbm, win_v)], sems)
        _zero_first(acc_ref)
        g, sh, sc = vec_ref[0:1, :], vec_ref[1:2, :], vec_ref[2:3, :]
        grp, cnt = _pool_consts((TM, PW), i * TM)
        _, cnt_n = _pool_consts((HALO, PW), (i + 1) * TM)
        ext_ref[0:TM, :] = ddc_ref[...] / cnt
        ext_ref[TM:TM + HALO, :] = jnp.where(i < nt - 1, ddn_ref[0:HALO, :] / cnt_n, 0.0)
        dp = _select_group(grp, _window_sums(ext_ref, 0, 1, TM)) - ddc_ref[...]
        dproj_ref[:, 0:PW] = dp.astype(BF16)
        cos_t, sin_t = cos_ref[...], sin_ref[...]
        for j in range(2 * NG):
            col = PW + j * GA
            dt = _from_residues(qk_refs[j], scr_ref, DIL[j % NG])
            dproj_ref[:, col:col + GA] = _rope_bwd(dt, cos_t, sin_t).astype(BF16)
        for j in range(NG):
            col = PW + (2 * NG + j) * GA
            dproj_ref[:, col:col + GA] = _from_residues(dv_refs[j], scr_ref, DIL[j]).astype(BF16)
        dproj_ref[:, PW + 3 * NG * GA:INW] = dgates_ref[...]
        du = None
        for j in range(INW // 512):
            part = _dot_nt(dproj_ref[:, j * 512:(j + 1) * 512], win_v[:, j * 512:(j + 1) * 512])
            du = part if du is None else du + part
        xh, r, n, _ = _norm_fwd(h_ref[...], g, sh, sc)
        dhn, dsh, dsc, dg = _norm_bwd(du, xh, r, n, g, sc)
        dhi_ref[...] = dh_ref[...] + dhn
        acc_ref[0:1, :] += dsh
        acc_ref[1:2, :] += dsc
        acc_ref[2:3, :] += dg

    nxt = pl.BlockSpec((TM, PW), lambda i: (jnp.minimum(i + 1, nt - 1), 0))
    return _call(
        body, "mix_bwd_b", (nt,),
        [_rows(TM, D), _rows(TM, D), _const((8, D)), _rows(TM, PW), nxt] + [_rm_spec(dl) for dl in DIL] * 3
        + [_rows(TM, GW), _rows(TM, 128), _rows(TM, 128), ANY],
        [_rows(TM, D), _rows(TM, INW), _const((8, D))],
        [_sds((T, D), F32), _sds((T, INW), BF16), _sds((8, D), F32)],
        scratch=[pltpu.VMEM((D, INW), BF16), pltpu.VMEM((TM + HALO, PW), F32), pltpu.VMEM((GA // LANES, TM, LANES), F32),
                 pltpu.SemaphoreType.DMA((1,))],
        vmem=VMEM_BIG,
    )(dh, h, vec, dd, dd, *dqs, *dks, *dvs, dgates, cos, sin, win)[0]


def _ada_fwd(c_all, w_shard, b_shard):
    n = w_shard.shape[1]

    def body(c_ref, w_ref, b_ref, o_ref):
        cv = c_ref[...]
        cond = (cv * jax.nn.sigmoid(cv)).astype(BF16)
        o_ref[...] = _dot(cond, w_ref[...].astype(BF16)) + b_ref[...]

    tn = n // 3
    return pl.pallas_call(
        body, name="ada_fwd", grid=(3,),
        in_specs=[pl.BlockSpec((8, D), lambda j: (0, 0)), pl.BlockSpec((D, tn), lambda j: (0, j)), pl.BlockSpec((1, tn), lambda j: (0, j))],
        out_specs=pl.BlockSpec((8, tn), lambda j: (0, j)), out_shape=_sds((8, n), F32),
        compiler_params=pltpu.CompilerParams(dimension_semantics=("arbitrary",)),
    )(c_all, w_shard, b_shard)


def _ada_bwd(c_all, dmod_shard):
    n = dmod_shard.shape[1]

    def body(c_ref, d_ref, o_ref):
        cv = c_ref[...]
        cond = (cv * jax.nn.sigmoid(cv)).astype(BF16)
        o_ref[...] = _dot_tn(cond, d_ref[...].astype(BF16))

    tn = n // 3
    return pl.pallas_call(
        body, name="ada_bwd", grid=(3,),
        in_specs=[pl.BlockSpec((8, D), lambda j: (0, 0)), pl.BlockSpec((8, tn), lambda j: (0, j))],
        out_specs=pl.BlockSpec((D, tn), lambda j: (0, j)), out_shape=_sds((D, n), F32),
        compiler_params=pltpu.CompilerParams(dimension_semantics=("arbitrary",)),
    )(c_all, dmod_shard)


def _adam_math(w, g, m, v):
    m2 = B1 * m + (1.0 - B1) * g
    v2 = B2 * v + (1.0 - B2) * (g * g)
    m_hat = m2 / (1.0 - B1 ** STEP)
    v_hat = v2 / (1.0 - B2 ** STEP)
    delta = -LR * (m_hat / (jnp.sqrt(v_hat) + AEPS) + WD * w)
    return delta, m2, v2


def _adam(w, m, v, parts, name, comm=None):
    R, C = w.shape
    tr = R
    for cand in (128, 64, 32, 16, 8):
        if R % cand == 0:
            tr = cand
            break
    np_ = len(parts)

    def body(w_ref, m_ref, v_ref, *rest):
        p_refs, (g_ref, d_ref, m2_ref, v2_ref) = rest[:np_], rest[np_:]
        g = p_refs[0][...]
        for pr in p_refs[1:]:
            g = g + pr[...]
        delta, m2, v2 = _adam_math(w_ref[...], g, m_ref[...], v_ref[...])
        g_ref[...] = g
        d_ref[...] = delta
        m2_ref[...] = m2
        v2_ref[...] = v2

    spec = pl.BlockSpec((tr, C), lambda i: (i, 0))
    return _call(body, name, (R // tr,), [spec] * (3 + np_), [spec] * 4, [_sds((R, C), F32)] * 4,
                 vmem=VMEM_BIG, comm=comm)(w, m, v, *parts)


def _adam_halves(w, m, v, mine, other, name):
    R, C = w.shape
    tr = 128
    nh = R // 2 // tr

    def body(c_ref, w_ref, m_ref, v_ref, mine_ref, other_ref, g_ref, d_ref, m2_ref, v2_ref):
        i = pl.program_id(0)
        in_mine = jnp.logical_and(i >= c_ref[0] * nh, i < (c_ref[0] + 1) * nh)
        g = jnp.where(in_mine, mine_ref[...], other_ref[...])
        delta, m2, v2 = _adam_math(w_ref[...], g, m_ref[...], v_ref[...])
        g_ref[...] = g
        d_ref[...] = delta
        m2_ref[...] = m2
        v2_ref[...] = v2

    spec = pl.BlockSpec((tr, C), lambda i, c: (i, 0))
    grid_spec = pltpu.PrefetchScalarGridSpec(
        num_scalar_prefetch=1, grid=(R // tr,),
        in_specs=[spec] * 3 + [pl.BlockSpec((tr, C), lambda i, c: (jnp.clip(i - c[0] * nh, 0, nh - 1), 0)),
                               pl.BlockSpec((tr, C), lambda i, c: (jnp.clip(i - (1 - c[0]) * nh, 0, nh - 1), 0))],
        out_specs=[spec] * 4)
    return pl.pallas_call(
        body, name=name, grid_spec=grid_spec, out_shape=[_sds((R, C), F32)] * 4,
        compiler_params=pltpu.CompilerParams(dimension_semantics=("arbitrary",), vmem_limit_bytes=VMEM_BIG),
    )(lax.axis_index("c").astype(jnp.int32).reshape(1), w, m, v, mine, other)


def _adam_small(ws, ms, vs, gathered):
    n = len(ws)
    sizes = [a.shape[1] for a in ws]

    def total(ga_ref, off, size):
        g = ga_ref[0, :, off:off + size]
        for dev in range(1, 8):
            g = g + ga_ref[dev, :, off:off + size]
        return g

    def body(*refs):
        w_refs, m_refs, v_refs, ga_ref, outs = refs[:n], refs[n:2 * n], refs[2 * n:3 * n], refs[3 * n], refs[3 * n + 1:]
        off = 0
        for j, size in enumerate(sizes):
            g = total(ga_ref, off, size)
            delta, m2, v2 = _adam_math(w_refs[j][...], g, m_refs[j][...], v_refs[j][...])
            for ref, val in zip(outs[4 * j:4 * j + 4], (g, delta, m2, v2)):
                ref[...] = val
            off += size
        outs[4 * n][...] = total(ga_ref, off, 128)

    res = pl.pallas_call(
        body, name="adam_small",
        out_shape=[_sds((1, size), F32) for size in sizes for _ in range(4)] + [_sds((1, 128), F32)],
    )(*ws, *ms, *vs, gathered)
    return [res[4 * j:4 * j + 4] for j in range(n)], res[4 * n]


def _sum4(blocks, name):
    _, R, C = blocks.shape
    tr = R
    for cand in (256, 128, 64, 32, 16):
        if R % cand == 0:
            tr = cand
            break

    def body(r_ref, out_ref):
        out_ref[...] = ((r_ref[0].astype(F32) + r_ref[1].astype(F32)) + r_ref[2].astype(F32)) + r_ref[3].astype(F32)

    return pl.pallas_call(
        body, name=name, grid=(R // tr,),
        in_specs=[pl.BlockSpec((4, tr, C), lambda i: (0, i, 0))],
        out_specs=pl.BlockSpec((tr, C), lambda i: (i, 0)), out_shape=_sds((R, C), F32),
        compiler_params=pltpu.CompilerParams(dimension_semantics=("arbitrary",)),
    )(blocks)


def _place():
    return lax.axis_index("x"), lax.axis_index("y"), lax.axis_index("c")


def _chip_peer(x, y, c, m):
    return (x ^ (m >> 1), y ^ (m & 1), c)


def _shard_ref(ref, axis, k, n):
    start = pl.multiple_of(k * n, 128 if axis == 1 else 16)
    return ref.at[:, pl.ds(start, n)] if axis == 1 else ref.at[pl.ds(start, n), :]


def _half_rows(ref, axis, k, n, hc):
    if axis == 1:
        half = ref.shape[0] // 2
        return ref.at[pl.ds(pl.multiple_of(hc * half, 16), half), pl.ds(pl.multiple_of(k * n, 128), n)]
    half = n // 2
    return ref.at[pl.ds(pl.multiple_of(k * n + hc * half, 16), half), :]


class _GatherPlan:
    def __init__(self, shards, axes):
        self.inputs, self.axes, nw = list(shards), list(axes), len(shards)
        self.out_shapes = [_sds((s.shape[0] * (4 if ax == 0 else 1), s.shape[1] * (4 if ax == 1 else 1)), BF16)
                           for s, ax in zip(shards, axes)]
        self.sem_shapes = [pltpu.SemaphoreType.DMA((nw,))] + [pltpu.SemaphoreType.DMA((nw, 3))] * 4

    def _copies(self, ins, outs, sems):
        local_sems, send_sems, recv_sems, pass_sems, got_sems = sems
        x, y, c = _place()
        k = 2 * x + y
        local, sends, arrivals, passes, handed = [], [], [], [], []
        for j, ax in enumerate(self.axes):
            n = ins[j].shape[ax]
            half = ins[j].shape[0] // 2
            local.append(pltpu.make_async_copy(ins[j], _shard_ref(outs[j], ax, k, n), local_sems.at[j]))
            my_half = ins[j].at[pl.ds(pl.multiple_of(c * half, 16), half), :]
            for m in range(1, 4):
                sends.append(pltpu.make_async_remote_copy(
                    src_ref=my_half, dst_ref=_half_rows(outs[j], ax, k, n, c), send_sem=send_sems.at[j, m - 1],
                    recv_sem=recv_sems.at[j, m - 1], device_id=_chip_peer(x, y, c, m), device_id_type=MESH))
                theirs = _half_rows(outs[j], ax, k ^ m, n, c)
                arrivals.append(pltpu.make_async_remote_copy(
                    src_ref=my_half, dst_ref=theirs, send_sem=send_sems.at[j, m - 1], recv_sem=recv_sems.at[j, m - 1],
                    device_id=(x, y, c), device_id_type=MESH))
                passes.append(pltpu.make_async_remote_copy(
                    src_ref=theirs, dst_ref=theirs, send_sem=pass_sems.at[j, m - 1], recv_sem=got_sems.at[j, m - 1],
                    device_id=(x, y, 1 - c), device_id_type=MESH))
                other = _half_rows(outs[j], ax, k ^ m, n, 1 - c)
                handed.append(pltpu.make_async_remote_copy(
                    src_ref=other, dst_ref=other, send_sem=pass_sems.at[j, m - 1], recv_sem=got_sems.at[j, m - 1],
                    device_id=(x, y, c), device_id_type=MESH))
        return local, sends, arrivals, passes, handed

    def start(self, ins, outs, sems):
        local, sends, _, _, _ = self._copies(ins, outs, sems)
        for cp in local + sends:
            cp.start()

    def relay(self, ins, outs, sems):
        _, _, arrivals, passes, _ = self._copies(ins, outs, sems)
        for arrived, onward in zip(arrivals, passes):
            arrived.wait_recv()
            onward.start()

    def wait(self, ins, outs, sems):
        local, sends, _, passes, handed = self._copies(ins, outs, sems)
        for cp in handed:
            cp.wait_recv()
        for cp in sends + passes:
            cp.wait_send()
        for cp in local:
            cp.wait()


class _ScatterPlan:
    def __init__(self, grads, axes):
        self.inputs, self.axes, nw = list(grads), list(axes), len(grads)
        self.shard_shapes = [(g.shape[0] // (4 if ax == 0 else 1), g.shape[1] // (4 if ax == 1 else 1))
                             for g, ax in zip(grads, axes)]
        self.out_shapes = [_sds((4,) + s, BF16) for s in self.shard_shapes]
        self.sem_shapes = [pltpu.SemaphoreType.DMA((nw,)), pltpu.SemaphoreType.DMA((nw, 3)), pltpu.SemaphoreType.DMA((nw, 3))]

    def _copies(self, ins, outs, sems):
        local_sems, send_sems, recv_sems = sems
        x, y, c = _place()
        k = 2 * x + y
        local, remote, arrivals = [], [], []
        for j, ax in enumerate(self.axes):
            n = self.shard_shapes[j][ax]
            local.append(pltpu.make_async_copy(_shard_ref(ins[j], ax, k, n), outs[j].at[0], local_sems.at[j]))
            for m in range(1, 4):
                remote.append(pltpu.make_async_remote_copy(
                    src_ref=_shard_ref(ins[j], ax, k ^ m, n), dst_ref=outs[j].at[m],
                    send_sem=send_sems.at[j, m - 1], recv_sem=recv_sems.at[j, m - 1],
                    device_id=_chip_peer(x, y, c, m), device_id_type=MESH))
                arrivals.append(pltpu.make_async_remote_copy(
                    src_ref=_shard_ref(ins[j], ax, k, n), dst_ref=outs[j].at[m],
                    send_sem=send_sems.at[j, m - 1], recv_sem=recv_sems.at[j, m - 1],
                    device_id=(x, y, c), device_id_type=MESH))
        return local, remote, arrivals

    def start(self, ins, outs, sems):
        local, remote, _ = self._copies(ins, outs, sems)
        for cp in local + remote:
            cp.start()

    def relay(self, ins, outs, sems):
        pass

    def wait(self, ins, outs, sems):
        local, remote, arrivals = self._copies(ins, outs, sems)
        for cp in arrivals:
            cp.wait_recv()
        for cp in remote:
            cp.wait_send()
        for cp in local:
            cp.wait()


def _run_plan(plan, name):
    nc = len(plan.inputs)

    def body(*refs):
        ins, outs, sems = refs[:nc], refs[nc:2 * nc], refs[2 * nc:]
        plan.start(ins, outs, sems)
        plan.relay(ins, outs, sems)
        plan.wait(ins, outs, sems)

    return pl.pallas_call(body, name=name, in_specs=[ANY] * nc, out_specs=[ANY] * nc, out_shape=list(plan.out_shapes),
                          scratch_shapes=list(plan.sem_shapes))(*plan.inputs)


class _SwapPlan:
    def __init__(self, parts):
        self.inputs, nw = list(parts), len(parts)
        self.out_shapes = [_sds(p.shape, p.dtype) for p in parts]
        self.sem_shapes = [pltpu.SemaphoreType.DMA((nw,)), pltpu.SemaphoreType.DMA((nw,))]

    def _copies(self, ins, outs, sems):
        send_sems, recv_sems = sems
        x, y, c = _place()
        return [pltpu.make_async_remote_copy(
            src_ref=ins[j], dst_ref=outs[j], send_sem=send_sems.at[j], recv_sem=recv_sems.at[j],
            device_id=(x, y, 1 - c), device_id_type=MESH) for j in range(len(ins))]

    def start(self, ins, outs, sems):
        for cp in self._copies(ins, outs, sems):
            cp.start()

    def relay(self, ins, outs, sems):
        pass

    def wait(self, ins, outs, sems):
        for cp in self._copies(ins, outs, sems):
            cp.wait()


class _SmallGatherPlan:
    def __init__(self, v):
        self.inputs = [v]
        self.out_shapes = [_sds((8,) + v.shape, v.dtype)]
        self.sem_shapes = [pltpu.SemaphoreType.DMA((1,)), pltpu.SemaphoreType.DMA((7,)), pltpu.SemaphoreType.DMA((7,))]

    def _copies(self, ins, outs, sems):
        (v_ref,), (out_ref,), (local_sem, send_sems, recv_sems) = ins, outs, sems
        x, y, c = _place()
        me = 4 * x + 2 * y + c
        local = pltpu.make_async_copy(v_ref, out_ref.at[me], local_sem.at[0])
        sends, arrivals = [], []
        for m in range(1, 8):
            px, py, pc = x ^ (m >> 2), y ^ ((m >> 1) & 1), c ^ (m & 1)
            sends.append(pltpu.make_async_remote_copy(
                src_ref=v_ref, dst_ref=out_ref.at[me], send_sem=send_sems.at[m - 1], recv_sem=recv_sems.at[m - 1],
                device_id=(px, py, pc), device_id_type=MESH))
            arrivals.append(pltpu.make_async_remote_copy(
                src_ref=v_ref, dst_ref=out_ref.at[4 * px + 2 * py + pc], send_sem=send_sems.at[m - 1],
                recv_sem=recv_sems.at[m - 1], device_id=(x, y, c), device_id_type=MESH))
        return local, sends, arrivals

    def start(self, ins, outs, sems):
        local, sends, _ = self._copies(ins, outs, sems)
        for cp in [local] + sends:
            cp.start()

    def relay(self, ins, outs, sems):
        pass

    def wait(self, ins, outs, sems):
        local, sends, arrivals = self._copies(ins, outs, sems)
        for cp in arrivals:
            cp.wait_recv()
        for cp in sends:
            cp.wait_send()
        local.wait()


class _PlanGroup:
    def __init__(self, plans):
        self.plans = [p for p in plans if p is not None]
        self.inputs = [a for p in self.plans for a in p.inputs]
        self.out_shapes = [s for p in self.plans for s in p.out_shapes]
        self.sem_shapes = [s for p in self.plans for s in p.sem_shapes]

    def _each(self, ins, outs, sems):
        i = s = 0
        for p in self.plans:
            n, ns = len(p.inputs), len(p.sem_shapes)
            yield p, ins[i:i + n], outs[i:i + n], sems[s:s + ns]
            i, s = i + n, s + ns

    def start(self, ins, outs, sems):
        for p, pi, po, ps in self._each(ins, outs, sems):
            p.start(pi, po, ps)

    def relay(self, ins, outs, sems):
        for p, pi, po, ps in self._each(ins, outs, sems):
            p.relay(pi, po, ps)

    def wait(self, ins, outs, sems):
        for p, pi, po, ps in self._each(ins, outs, sems):
            p.wait(pi, po, ps)

    def split(self, outs):
        res, i = [], 0
        for p in self.plans:
            res.append(outs[i:i + len(p.inputs)])
            i += len(p.inputs)
        return res


BIG = ("w_ffn1_in", "w_ffn1_out", "w_in", "w_pool_branch", "w_attn_branch", "w_out", "w_ffn2_in", "w_ffn2_out")
BIG_AXIS = {"w_ffn1_in": 1, "w_ffn1_out": 0, "w_in": 1, "w_pool_branch": 1, "w_attn_branch": 1, "w_out": 0,
            "w_ffn2_in": 1, "w_ffn2_out": 0}


class _Sharded:
    fused_scatter = True

    def __init__(self, shards):
        self.shards, self.full, self.recv = shards, {}, {}

    def gather_plan(self, names):
        return _GatherPlan([self.shards[n] for n in names], [BIG_AXIS[n.split("/")[0]] for n in names])

    def gather_now(self, names):
        self.gathered(names, _run_plan(self.gather_plan(names), "gather_" + names[0]))

    def gathered(self, names, outs):
        self.full.update(zip(names, outs))

    def scatter_plan(self, names, grads):
        return _ScatterPlan([grads[n] for n in names], [BIG_AXIS[n] for n in names])

    def scatter_now(self, names, grads):
        self.scattered(names, _run_plan(self.scatter_plan(names, grads), "scatter_" + names[0]))

    def scattered(self, names, outs):
        self.recv.update(zip(names, outs))


class _Whole:
    fused_scatter = False

    def __init__(self, full):
        self.full, self.recv = dict(full), {}

    def gather_plan(self, names):
        return None

    def gather_now(self, names):
        pass

    def gathered(self, names, outs):
        pass

    def scatter_plan(self, names, grads):
        return None

    def scatter_now(self, names, grads):
        pass

    def scattered(self, names, outs):
        pass


def _vec(rows):
    pad = [jnp.zeros((1, D), F32)] * (8 - len(rows))
    return jnp.concatenate([r.reshape(1, D) for r in rows] + pad, axis=0)


def _block_diag(w_pool):
    n, c = w_pool.shape[0], w_pool.shape[1]
    eye = jnp.eye(n, dtype=w_pool.dtype)
    return (eye[:, None, :, None] * w_pool[:, :, None, :]).reshape(n * c, n * c)


def _example_step(x, tgt, positions, mod, gains, w_pool, pool_scale, ws, pack=None):
    T = x.shape[0]
    assert (T // BLK // DIL[-1]) & (T // BLK // DIL[-1] - 1) == 0, "blocks per sequence must be a power of two"
    sh1, sc1, gt1, sh2, sc2, gt2, sh3, sc3, gt3 = [mod[j * D:(j + 1) * D] for j in range(NMOD)]
    g1, g2, g3, gf = gains
    vec1, vec2, vec3 = _vec([g1, sh1, sc1, gt1]), _vec([g2, sh2, sc2, gt2]), _vec([g3, sh3, sc3, gt3])
    inv_freq = 10000.0 ** (-jnp.arange(0, HD, 2, dtype=F32) / HD)
    ang = positions.astype(F32)[:, None] * inv_freq
    cos = jnp.tile(jnp.cos(ang), (1, 4))
    sin = jnp.tile(jnp.concatenate([-jnp.sin(ang), jnp.sin(ang)], axis=1), (1, 2))
    wp_bd = _block_diag(w_pool).astype(BF16)
    ones_bd = _block_diag(jnp.ones((NH, HD, HD), F32)).astype(BF16)
    ps = jnp.concatenate([pool_scale.reshape(1, PW), jnp.zeros((7, PW), F32)], axis=0)
    wb = ws.full

    if "w_ffn1_in" not in wb:
        ws.gather_now(["w_ffn1_in"])
    (u1, a1, b1), got = _ffn_ab(x, vec1, [wb["w_ffn1_in"]], "ffn1_ab", ws.gather_plan(["w_ffn1_out"]))
    ws.gathered(["w_ffn1_out"], got)
    mixw = ["w_in", "w_pool_branch", "w_attn_branch", "w_out"]
    (h1, f1), got = _ffn_out(x, a1, b1, vec1, wb["w_ffn1_out"], "ffn1_out", ws.gather_plan(mixw))
    ws.gathered(mixw, got)
    (u2, p, qs, ks, vs, gates), got = _mix_proj(h1, vec2, wb["w_in"], cos, sin, ws.gather_plan(["w_ffn2_in/0"]))
    ws.gathered(["w_ffn2_in/0"], got)
    qs, ks, vs = [_flat(t) for t in qs], [_flat(t) for t in ks], [_flat(t) for t in vs]
    nbs = [T // d // BLK for d in DIL]
    os, lses = [], []
    for gi, riders in enumerate((["w_ffn2_out"], ["w_ffn2_in/1"], None)):
        (o, lse), got = _attn_fwd(qs[gi], ks[gi], vs[gi], nbs[gi], f"attn_fwd{gi}", riders and ws.gather_plan(riders))
        ws.gathered(riders or [], got)
        os.append(o)
        lses.append(lse)
    win3 = [wb["w_ffn2_in/0"], wb["w_ffn2_in/1"]] if "w_ffn2_in/0" in wb else [wb["w_ffn2_in"]]
    os_r = [_by_residue(t, d) for t, d in zip(os, DIL)]
    lses_r = [_by_residue(t, d) for t, d in zip(lses, DIL)]
    h2, ypool, yattn, merged, mixout, dpool = _mix_merge(
        h1, vec2, p, os_r, lses_r, gates, wp_bd, ps, wb["w_pool_branch"], wb["w_attn_branch"], wb["w_out"])
    (dh3, u3, a3, b3, f3, lacc), _ = _ffn_fwd(h2, vec3, win3, wb["w_ffn2_out"], "ffn2_fwd", head=(tgt, _vec([gf])))
    loss = 0.5 * jnp.sum(lacc[0]) / D

    grads = {}

    def wgrad_cols(name, xx, yy, riders, extra=None):
        group = _PlanGroup([ws.scatter_plan(riders, grads) if riders else None, extra])
        plan = group if group.plans else None
        if ws.fused_scatter:
            blocks, got = _wgrad_scatter(xx, yy, "wg_" + name, min(2048, T // 2), comm=plan)
            ws.scattered([name], [blocks])
        else:
            grads[name], got = _wgrad(xx, yy, "wg_" + name, D, 512, 1024, comm=plan)
        parts = group.split(got)
        if len(parts) > (extra is not None):
            ws.scattered(riders, parts[0])
        return parts[-1] if extra is not None else None

    (dh2, dab3, s3, df3, acc3), _ = _ffn_bwd(dh3, h2, a3, b3, f3, vec3, win3, wb["w_ffn2_out"], "ffn2_bwd")
    grads["w_ffn2_out"], _ = _wgrad(s3, df3, "wg_ffn2_out", FF // 2, 512, min(4096, T // 2))
    wgrad_cols("w_ffn2_in", u3, dab3, ["w_ffn2_out"])
    (dgates, do0, do1, do2, e0, e1, e2, dd, acc2a, accps,
     grads["w_out"], grads["w_pool_branch"], grads["w_attn_branch"], gwp), _ = _mix_bwd_a(
        dh2, vec2, mixout, merged, gates, ypool, yattn, dpool, os_r, lses_r, wp_bd, ps,
        wb["w_pool_branch"], wb["w_attn_branch"], wb["w_out"], ones_bd)
    n = len(POOL_WINDOWS)
    c = PW // n
    grad_w_pool = jnp.stack([gwp[j * c:(j + 1) * c, j * c:(j + 1) * c] for j in range(n)], axis=0)
    small3 = ["w_out", "w_pool_branch", "w_attn_branch"]
    dqs, dks, dvs = [], [], []
    for gi, (do, e) in enumerate(((do0, e0), (do1, e1), (do2, e2))):
        plan = ws.scatter_plan(small3, grads) if gi == 0 else None
        (dq, dk, dv), got = _attn_bwd(qs[gi], ks[gi], vs[gi], _flat(do), lses[gi], _flat(e), nbs[gi], f"attn_bwd{gi}", plan)
        if gi == 0:
            ws.scattered(small3, got)
        dqs.append(_by_residue(dq, DIL[gi]))
        dks.append(_by_residue(dk, DIL[gi]))
        dvs.append(_by_residue(dv, DIL[gi]))
    dh1, dproj, acc2b = _mix_bwd_b(dh2, h1, vec2, dd, dqs, dks, dvs, dgates, cos, sin, wb["w_in"])
    wgrad_cols("w_in", u2, dproj, [])
    (dx, dab1, s1, df1, acc1), _ = _ffn_bwd(dh1, x, a1, b1, f1, vec1, [wb["w_ffn1_in"]], wb["w_ffn1_out"], "ffn1_bwd")
    grads["w_ffn1_out"], _ = _wgrad(s1, df1, "wg_ffn1_out", FF // 2, 512, min(4096, T // 2))
    dmod = jnp.concatenate([acc1[0], acc1[1], acc1[3], acc2b[0], acc2b[1], acc2a[3], acc3[0], acc3[1], acc3[3]])
    dgains = jnp.stack([acc1[2], acc2b[2], acc3[2], lacc[1]], axis=0)
    row = None if pack is None else _SmallGatherPlan(pack(loss, dmod, dgains, grad_w_pool, accps[0]))
    rows = wgrad_cols("w_ffn1_in", u1, dab1, ["w_ffn1_out"], row)
    return loss, dx, dmod, dgains, grad_w_pool, accps[0], grads, None if rows is None else rows[0]


SMALL = ("b_ada", "g_norm_ffn1", "g_norm_mix", "g_norm_ffn2", "g_final", "pool_scale", "w_pool")
WEIGHTS = ("w_ada", "b_ada", "g_norm_ffn1", "w_ffn1_in", "w_ffn1_out", "g_norm_mix", "w_in", "w_pool", "pool_scale",
           "w_pool_branch", "w_attn_branch", "w_out", "g_norm_ffn2", "w_ffn2_in", "w_ffn2_out", "g_final")


def _pack_small(t):
    return jnp.concatenate([t[n].reshape(-1) for n in SMALL]).reshape(1, -1)


def kernel(x, c, positions, w_ada, b_ada, g_norm_ffn1, w_ffn1_in, w_ffn1_out, g_norm_mix, w_in, w_pool, pool_scale, w_pool_branch, w_attn_branch, w_out, g_norm_ffn2, w_ffn2_in, w_ffn2_out, g_final, loss_target, m_w_ada, m_b_ada, m_g_norm_ffn1, m_w_ffn1_in, m_w_ffn1_out, m_g_norm_mix, m_w_in, m_w_pool, m_pool_scale, m_w_pool_branch, m_w_attn_branch, m_w_out, m_g_norm_ffn2, m_w_ffn2_in, m_w_ffn2_out, m_g_final, v_w_ada, v_b_ada, v_g_norm_ffn1, v_w_ffn1_in, v_w_ffn1_out, v_g_norm_mix, v_w_in, v_w_pool, v_pool_scale, v_w_pool_branch, v_w_attn_branch, v_w_out, v_g_norm_ffn2, v_w_ffn2_in, v_w_ffn2_out, v_g_final):
    w = dict(w_ada=w_ada, b_ada=b_ada, g_norm_ffn1=g_norm_ffn1, w_ffn1_in=w_ffn1_in, w_ffn1_out=w_ffn1_out,
             g_norm_mix=g_norm_mix, w_in=w_in, w_pool=w_pool, pool_scale=pool_scale, w_pool_branch=w_pool_branch,
             w_attn_branch=w_attn_branch, w_out=w_out, g_norm_ffn2=g_norm_ffn2, w_ffn2_in=w_ffn2_in,
             w_ffn2_out=w_ffn2_out, g_final=g_final)
    mom = dict(w_ada=m_w_ada, b_ada=m_b_ada, g_norm_ffn1=m_g_norm_ffn1, w_ffn1_in=m_w_ffn1_in, w_ffn1_out=m_w_ffn1_out,
               g_norm_mix=m_g_norm_mix, w_in=m_w_in, w_pool=m_w_pool, pool_scale=m_pool_scale,
               w_pool_branch=m_w_pool_branch, w_attn_branch=m_w_attn_branch, w_out=m_w_out, g_norm_ffn2=m_g_norm_ffn2,
               w_ffn2_in=m_w_ffn2_in, w_ffn2_out=m_w_ffn2_out, g_final=m_g_final)
    var = dict(w_ada=v_w_ada, b_ada=v_b_ada, g_norm_ffn1=v_g_norm_ffn1, w_ffn1_in=v_w_ffn1_in, w_ffn1_out=v_w_ffn1_out,
               g_norm_mix=v_g_norm_mix, w_in=v_w_in, w_pool=v_w_pool, pool_scale=v_pool_scale,
               w_pool_branch=v_w_pool_branch, w_attn_branch=v_w_attn_branch, w_out=v_w_out, g_norm_ffn2=v_g_norm_ffn2,
               w_ffn2_in=v_w_ffn2_in, w_ffn2_out=v_w_ffn2_out, g_final=v_g_final)
    ix, iy, ic = _place()
    chip = 2 * ix + iy
    me = 4 * ix + 2 * iy + ic
    nada = w_ada.shape[2]

    shards = {n: w[n][0].astype(BF16) for n in BIG}
    half = D // 2
    shards["w_ffn2_in/0"], shards["w_ffn2_in/1"] = shards["w_ffn2_in"][:half], shards["w_ffn2_in"][half:]
    ws = _Sharded(shards)
    c_all = _run_plan(_SmallGatherPlan(c), "gather_c")[0][:, 0, :]
    b_shard = lax.dynamic_slice_in_dim(b_ada, chip * nada, nada, axis=1)
    mod_cols = _ada_fwd(c_all, w_ada[0], b_shard)
    first = _PlanGroup([_SmallGatherPlan(mod_cols), ws.gather_plan(["w_ffn1_in"])])
    (mod_all,), ffn1 = first.split(_run_plan(first, "gather_first"))
    ws.gathered(["w_ffn1_in"], ffn1)
    mod = jnp.concatenate([lax.dynamic_index_in_dim(mod_all[4 * (kk >> 1) + 2 * (kk & 1)], me, axis=0, keepdims=False)
                           for kk in range(4)])

    def pack(loss, dmod, dgains, g_w_pool, g_pool_scale):
        small_g = dict(b_ada=dmod, g_norm_ffn1=dgains[0], g_norm_mix=dgains[1], g_norm_ffn2=dgains[2],
                       g_final=dgains[3], pool_scale=g_pool_scale, w_pool=g_w_pool)
        return jnp.concatenate([_pack_small(small_g), jnp.pad(loss.reshape(1, 1), ((0, 0), (0, 127)))], axis=1)

    _, dx, _, _, _, _, _, gathered = _example_step(
        x[0], loss_target[0], positions[0], mod, (g_norm_ffn1[0], g_norm_mix[0], g_norm_ffn2[0], g_final),
        w_pool[0], pool_scale[0], ws, pack)

    per_weight, loss_tile = _adam_small(*[[t[n].reshape(1, -1) for n in SMALL] for t in (w, mom, var)], gathered)
    small_out = [{n: per_weight[j][kind].reshape(w[n].shape) for j, n in enumerate(SMALL)} for kind in range(4)]
    loss = loss_tile[0, 0]

    dmod_all = gathered[:, 0, :NMOD * D]
    dmod_cols = lax.dynamic_slice_in_dim(dmod_all, chip * nada, nada, axis=1)
    g_ada = _ada_bwd(c_all, dmod_cols)

    ada_out = _adam(w_ada[0], m_w_ada[0], v_w_ada[0], [g_ada], "adam_w_ada")[0]

    sums = {n: _sum4(ws.recv[n], "sum_" + n) for n in BIG}
    other = dict(zip(BIG, _run_plan(_SwapPlan([sums[n] for n in BIG]), "swap_sibling")))
    big_out = {}
    for n in BIG:
        if sums[n].shape[0] < w[n].shape[1]:
            big_out[n] = _adam_halves(w[n][0], mom[n][0], var[n][0], sums[n], other[n], "adam_" + n)
        else:
            big_out[n] = _adam(w[n][0], mom[n][0], var[n][0], [sums[n], other[n]], "adam_" + n)[0]

    def leaf(kind, n):
        if n == "w_ada":
            return ada_out[kind][None]
        if n in big_out:
            return big_out[n][kind][None]
        return small_out[kind][n]

    return (loss, dx[None], *[leaf(kind, n) for kind in range(4) for n in WEIGHTS])
```

```python
import jax
import jax.numpy as jnp
from jax import lax
from jax.experimental import pallas as pl
from jax.experimental.pallas import tpu as pltpu

F32 = jnp.float32
BF16 = jnp.bfloat16

D = 1024
FF = 2816
FC = FF
PW = 256
GA = 256
HD = 64
LANES = 128
NH = GA // HD
NG = 3
DIL = (1, 4, 16)
BLK = 128
GW = 2 * D
INW = PW + 3 * NG * GA + GW
NMOD = 9
POOL_WINDOWS = (2, 4, 8, 16)
HALO = 16
EPS = 1e-6
SCALE = HD ** -0.5
NEG = -1e30

LR, B1, B2, AEPS, WD, STEP = 0.001, 0.9, 0.999, 1e-08, 0.01, 10

VMEM_BIG = 56 * 1024 * 1024
TM = 256

MESH = pl.DeviceIdType.MESH
ANY = pl.BlockSpec(memory_space=pl.ANY)


def _call(body, name, grid, in_specs, out_specs, out_shape, scratch=(), vmem=None, comm=None):
    params = pltpu.CompilerParams(dimension_semantics=("arbitrary",) * len(grid), vmem_limit_bytes=vmem)
    n_in, n_out, n_scr = len(in_specs), len(out_shape), len(scratch)
    if comm is None:
        call = pl.pallas_call(body, name=name, grid=grid, in_specs=list(in_specs), out_specs=list(out_specs),
                              out_shape=list(out_shape), scratch_shapes=list(scratch), compiler_params=params)
        return lambda *args: (call(*args), ())
    nc = len(comm.inputs)

    def body_with_comm(*refs):
        ins, refs = refs[:n_in], refs[n_in:]
        c_ins, refs = refs[:nc], refs[nc:]
        outs, refs = refs[:n_out], refs[n_out:]
        c_outs, refs = refs[:nc], refs[nc:]
        scr, sems = refs[:n_scr], refs[n_scr:]
        first = pl.program_id(0) == 0
        last = pl.program_id(0) == grid[0] - 1
        for ax in range(1, len(grid)):
            first = jnp.logical_and(first, pl.program_id(ax) == 0)
            last = jnp.logical_and(last, pl.program_id(ax) == grid[ax] - 1)

        @pl.when(first)
        def _():
            comm.start(c_ins, c_outs, sems)

        body(*ins, *outs, *scr)
        early_relay = len(grid) == 1 and grid[0] >= 4
        if early_relay:
            @pl.when(pl.program_id(0) == (3 * grid[0]) // 4)
            def _():
                comm.relay(c_ins, c_outs, sems)

        @pl.when(last)
        def _():
            if not early_relay:
                comm.relay(c_ins, c_outs, sems)
            comm.wait(c_ins, c_outs, sems)

    call = pl.pallas_call(
        body_with_comm, name=name, grid=grid, in_specs=list(in_specs) + [ANY] * nc,
        out_specs=list(out_specs) + [ANY] * nc, out_shape=list(out_shape) + list(comm.out_shapes),
        scratch_shapes=list(scratch) + list(comm.sem_shapes), compiler_params=params)

    def run(*args):
        res = call(*args, *comm.inputs)
        return res[:n_out], res[n_out:]

    return run


def _rows(tm, n):
    return pl.BlockSpec((tm, n), lambda i: (i, 0))


def _const(shape):
    return pl.BlockSpec(shape, lambda i: (0,) * len(shape))


def _sds(shape, dtype):
    return jax.ShapeDtypeStruct(shape, dtype)


def _dot(a, b):
    return jnp.dot(a, b, preferred_element_type=F32)


def _dot_nt(a, b):
    return lax.dot_general(a, b, (((1,), (1,)), ((), ())), preferred_element_type=F32)


def _dot_tn(a, b):
    return lax.dot_general(a, b, (((0,), (0,)), ((), ())), preferred_element_type=F32)


def _colsum(v):
    return jnp.sum(v, axis=0, keepdims=True)


def _norm_fwd(h, g, sh, sc):
    r = lax.rsqrt(jnp.mean(h * h, axis=-1, keepdims=True) + EPS)
    xh = h * r
    n = xh * g
    return xh, r, n, n * (1.0 + sc) + sh


def _norm_bwd(du, xh, r, n, g, sc):
    dn = du * (1.0 + sc)
    dxh = dn * g
    dh = r * (dxh - xh * jnp.mean(dxh * xh, axis=-1, keepdims=True))
    return dh, _colsum(du), _colsum(du * n), _colsum(dn * xh)


def _load_once(pairs, sems):
    @pl.when(pl.program_id(0) == 0)
    def _():
        cps = [pltpu.make_async_copy(src, dst, sems.at[j]) for j, (src, dst) in enumerate(pairs)]
        for cp in cps:
            cp.start()
        for cp in cps:
            cp.wait()


def _zero_first(ref):
    @pl.when(pl.program_id(0) == 0)
    def _():
        ref[...] = jnp.zeros(ref.shape, ref.dtype)


def _row_chunks(hbm_refs, vmem_ref):
    pairs, row = [], 0
    for ref in hbm_refs:
        pairs.append((ref, vmem_ref.at[pl.ds(row, ref.shape[0]), :]))
        row += ref.shape[0]
    return pairs


def _loss_head(hh, tgt, g):
    r = lax.rsqrt(jnp.mean(hh * hh, axis=-1, keepdims=True) + EPS)
    xh = hh * r
    err = xh * g - tgt
    dy = err * (1.0 / D)
    dxh = dy * g
    dh = r * (dxh - xh * jnp.mean(dxh * xh, axis=-1, keepdims=True))
    return dh, _colsum(err * err), _colsum(dy * xh)


def _ffn_fwd(h, vec, wins, wout, name, comm=None, head=None):
    T = h.shape[0]
    nwin = len(wins)
    nhead = 0 if head is None else 2

    def body(h_ref, vec_ref, *rest):
        head_refs, rest = rest[:nhead], rest[nhead:]
        win_hbms, rest = rest[:nwin], rest[nwin:]
        (wout_hbm, ho_ref, u_ref, a_ref, b_ref, f_ref), rest = rest[:6], rest[6:]
        lacc_refs, (win_v, wout_v, sems) = rest[:nhead // 2], rest[nhead // 2:]
        _load_once(_row_chunks(win_hbms, win_v) + [(wout_hbm, wout_v)], sems)
        hh = h_ref[...]
        g, sh, sc, gt = vec_ref[0:1, :], vec_ref[1:2, :], vec_ref[2:3, :], vec_ref[3:4, :]
        _, _, _, u = _norm_fwd(hh, g, sh, sc)
        ub = u.astype(BF16)
        u_ref[...] = ub
        acc = None
        for j in range(FF // FC):
            lo, hi = j * FC, (j + 1) * FC
            a = _dot(ub, win_v[:, lo:hi])
            b = _dot(ub, win_v[:, FF + lo:FF + hi])
            a_ref[:, lo:hi] = a.astype(BF16)
            b_ref[:, lo:hi] = b.astype(BF16)
            s = (a * jax.nn.sigmoid(a) * b).astype(BF16)
            part = _dot(s, wout_v[lo:hi, :])
            acc = part if acc is None else acc + part
        f_ref[...] = acc.astype(BF16)
        ho = hh + 0.5 * gt * acc
        if head is None:
            ho_ref[...] = ho
        else:
            _zero_first(lacc_refs[0])
            dh, sq, dg = _loss_head(ho, head_refs[0][...], head_refs[1][0:1, :])
            ho_ref[...] = dh
            lacc_refs[0][0:1, :] += sq
            lacc_refs[0][1:2, :] += dg

    head_specs = [] if head is None else [_rows(TM, D), _const((8, D))]
    lacc_spec = [] if head is None else [_const((8, D))]
    lacc_shape = [] if head is None else [_sds((8, D), F32)]
    return _call(
        body, name, (T // TM,),
        [_rows(TM, D), _const((8, D))] + head_specs + [ANY] * (nwin + 1),
        [_rows(TM, D), _rows(TM, D), _rows(TM, FF), _rows(TM, FF), _rows(TM, D)] + lacc_spec,
        [_sds((T, D), F32), _sds((T, D), BF16), _sds((T, FF), BF16), _sds((T, FF), BF16), _sds((T, D), BF16)] + lacc_shape,
        scratch=[pltpu.VMEM((D, 2 * FF), BF16), pltpu.VMEM((FF, D), BF16), pltpu.SemaphoreType.DMA((nwin + 1,))],
        vmem=VMEM_BIG, comm=comm,
    )(h, vec, *([] if head is None else head), *wins, wout)


def _ffn_ab(h, vec, wins, name, comm=None):
    T = h.shape[0]
    nwin = len(wins)

    def body(h_ref, vec_ref, *rest):
        win_hbms, (u_ref, a_ref, b_ref, win_v, sems) = rest[:nwin], rest[nwin:]
        _load_once(_row_chunks(win_hbms, win_v), sems)
        g, sh, sc = vec_ref[0:1, :], vec_ref[1:2, :], vec_ref[2:3, :]
        _, _, _, u = _norm_fwd(h_ref[...], g, sh, sc)
        ub = u.astype(BF16)
        u_ref[...] = ub
        for j in range(FF // FC):
            lo, hi = j * FC, (j + 1) * FC
            a_ref[:, lo:hi] = _dot(ub, win_v[:, lo:hi]).astype(BF16)
            b_ref[:, lo:hi] = _dot(ub, win_v[:, FF + lo:FF + hi]).astype(BF16)

    return _call(
        body, name, (T // TM,),
        [_rows(TM, D), _const((8, D))] + [ANY] * nwin,
        [_rows(TM, D), _rows(TM, FF), _rows(TM, FF)],
        [_sds((T, D), BF16), _sds((T, FF), BF16), _sds((T, FF), BF16)],
        scratch=[pltpu.VMEM((D, 2 * FF), BF16), pltpu.SemaphoreType.DMA((nwin,))],
        vmem=VMEM_BIG, comm=comm,
    )(h, vec, *wins)


def _ffn_out(h, a, b, vec, wout, name, comm=None):
    T = h.shape[0]

    def body(h_ref, a_ref, b_ref, vec_ref, wout_hbm, ho_ref, f_ref, wout_v, sems):
        _load_once([(wout_hbm, wout_v)], sems)
        gt = vec_ref[3:4, :]
        acc = None
        for j in range(FF // FC):
            lo, hi = j * FC, (j + 1) * FC
            av = a_ref[:, lo:hi].astype(F32)
            s = (av * jax.nn.sigmoid(av) * b_ref[:, lo:hi].astype(F32)).astype(BF16)
            part = _dot(s, wout_v[lo:hi, :])
            acc = part if acc is None else acc + part
        f_ref[...] = acc.astype(BF16)
        ho_ref[...] = h_ref[...] + 0.5 * gt * acc

    return _call(
        body, name, (T // TM,),
        [_rows(TM, D), _rows(TM, FF), _rows(TM, FF), _const((8, D)), ANY],
        [_rows(TM, D), _rows(TM, D)],
        [_sds((T, D), F32), _sds((T, D), BF16)],
        scratch=[pltpu.VMEM((FF, D), BF16), pltpu.SemaphoreType.DMA((1,))],
        vmem=VMEM_BIG, comm=comm,
    )(h, a, b, vec, wout)


def _ffn_bwd(dh, h, a, b, f, vec, wins, wout, name, comm=None):
    T = h.shape[0]
    nwin = len(wins)

    def body(dh_ref, h_ref, a_ref, b_ref, f_ref, vec_ref, *rest):
        win_hbms, (wout_hbm, dhi_ref, dab_ref, s_ref, df_ref, acc_ref, win_v, wout_v, sems) = rest[:nwin], rest[nwin:]
        _load_once(_row_chunks(win_hbms, win_v) + [(wout_hbm, wout_v)], sems)
        _zero_first(acc_ref)
        g, sh, sc, gt = vec_ref[0:1, :], vec_ref[1:2, :], vec_ref[2:3, :], vec_ref[3:4, :]
        dho = dh_ref[...]
        df = (0.5 * gt * dho).astype(BF16)
        df_ref[...] = df
        dgt = _colsum(0.5 * dho * f_ref[...].astype(F32))
        du = None
        for j in range(FF // FC):
            lo, hi = j * FC, (j + 1) * FC
            av = a_ref[:, lo:hi].astype(F32)
            bv = b_ref[:, lo:hi].astype(F32)
            ds = _dot_nt(df, wout_v[lo:hi, :])
            sig = jax.nn.sigmoid(av)
            sa = av * sig
            s_ref[:, lo:hi] = (sa * bv).astype(BF16)
            da = (ds * bv * (sig * (1.0 + av * (1.0 - sig)))).astype(BF16)
            db = (ds * sa).astype(BF16)
            dab_ref[:, lo:hi] = da
            dab_ref[:, FF + lo:FF + hi] = db
            part = _dot_nt(da, win_v[:, lo:hi]) + _dot_nt(db, win_v[:, FF + lo:FF + hi])
            du = part if du is None else du + part
        xh, r, n, _ = _norm_fwd(h_ref[...], g, sh, sc)
        dhn, dsh, dsc, dg = _norm_bwd(du, xh, r, n, g, sc)
        dhi_ref[...] = dho + dhn
        acc_ref[0:1, :] += dsh
        acc_ref[1:2, :] += dsc
        acc_ref[2:3, :] += dg
        acc_ref[3:4, :] += dgt

    return _call(
        body, name, (T // TM,),
        [_rows(TM, D), _rows(TM, D), _rows(TM, FF), _rows(TM, FF), _rows(TM, D), _const((8, D))] + [ANY] * (nwin + 1),
        [_rows(TM, D), _rows(TM, 2 * FF), _rows(TM, FF), _rows(TM, D), _const((8, D))],
        [_sds((T, D), F32), _sds((T, 2 * FF), BF16), _sds((T, FF), BF16), _sds((T, D), BF16), _sds((8, D), F32)],
        scratch=[pltpu.VMEM((D, 2 * FF), BF16), pltpu.VMEM((FF, D), BF16), pltpu.SemaphoreType.DMA((nwin + 1,))],
        vmem=VMEM_BIG, comm=comm,
    )(dh, h, a, b, f, vec, *wins, wout)


def _wgrad(x, y, name, tk, tn, tt, out_dtype=BF16, comm=None):
    T, K = x.shape
    N = y.shape[1]
    nt = T // tt

    def body(x_ref, y_ref, o_ref, acc_ref):
        t = pl.program_id(2)
        part = _dot_tn(x_ref[...], y_ref[...])

        @pl.when(t == 0)
        def _():
            acc_ref[...] = part

        @pl.when(t > 0)
        def _():
            acc_ref[...] += part

        @pl.when(t == nt - 1)
        def _():
            o_ref[...] = acc_ref[...].astype(out_dtype)

    (out,), c_outs = _call(
        body, name, (K // tk, N // tn, nt),
        [pl.BlockSpec((tt, tk), lambda i, j, t: (t, i)), pl.BlockSpec((tt, tn), lambda i, j, t: (t, j))],
        [pl.BlockSpec((tk, tn), lambda i, j, t: (i, j))], [_sds((K, N), out_dtype)],
        scratch=[pltpu.VMEM((tk, tn), F32)], vmem=VMEM_BIG, comm=comm,
    )(x, y)
    return out, c_outs


def _wgrad_scatter(x, y, name, tt, comm=None):
    T, K = x.shape
    n = y.shape[1] // 4
    nt = T // tt
    assert nt >= 2, "a block's hand-over is added one grid step into the next block"
    half = K // 2
    nc = 0 if comm is None else len(comm.inputs)

    def body(chip_ref, x_ref, y_ref, *refs):
        c_ins, refs = refs[:nc], refs[nc:]
        recv_ref, refs = refs[0], refs[1:]
        c_outs, refs = refs[:nc], refs[nc:]
        acc_ref, keep_ref, give_ref, take_ref, local_sem, give_sems, take_sems, send_sems, recv_sems = refs[:9]
        j, t = pl.program_id(0), pl.program_id(1)
        px, py, pc = _place()

        def hand_over(jj):
            return pltpu.make_async_remote_copy(
                src_ref=give_ref.at[jj], dst_ref=take_ref.at[jj], send_sem=give_sems.at[jj], recv_sem=take_sems.at[jj],
                device_id=(px, py, 1 - pc), device_id_type=MESH)

        def send(jj):
            m = (3, 1, 2)[jj]
            return pltpu.make_async_remote_copy(
                src_ref=keep_ref.at[jj], dst_ref=recv_ref.at[m], send_sem=send_sems.at[jj], recv_sem=recv_sems.at[jj],
                device_id=_chip_peer(px, py, pc, m), device_id_type=MESH)

        def add_sibling(jj):
            hand_over(jj).wait_recv()
            keep_ref[jj] = (keep_ref[jj].astype(F32) + take_ref[jj].astype(F32)).astype(BF16)

        if comm is not None:
            @pl.when(jnp.logical_and(j == 0, t == 0))
            def _():
                comm.start(c_ins, c_outs, refs[9:])

        part = _dot_tn(x_ref[...], y_ref[...])

        @pl.when(t == 0)
        def _():
            acc_ref[...] = part

        @pl.when(t > 0)
        def _():
            acc_ref[...] += part

        for jj in range(3):
            @pl.when(jnp.logical_and(j == jj + 1, t == 0))
            def _():
                add_sibling(jj)
                send(jj).start()

        for jj in range(4):
            @pl.when(jnp.logical_and(j == jj, t == nt - 1))
            def _():
                keep_ref[jj] = acc_ref[pl.ds(pl.multiple_of(pc * half, 16), half), :].astype(BF16)
                give_ref[jj] = acc_ref[pl.ds(pl.multiple_of((1 - pc) * half, 16), half), :].astype(BF16)
                hand_over(jj).start()

        @pl.when(jnp.logical_and(j == 3, t == nt - 1))
        def _():
            add_sibling(3)
            own = pltpu.make_async_copy(keep_ref.at[3], recv_ref.at[0], local_sem.at[0])
            own.start()
            for jj in range(3):
                send(jj).wait_recv()
            for jj in range(3):
                send(jj).wait_send()
            for jj in range(4):
                hand_over(jj).wait_send()
            own.wait()
            if comm is not None:
                comm.relay(c_ins, c_outs, refs[9:])
                comm.wait(c_ins, c_outs, refs[9:])

    grid_spec = pltpu.PrefetchScalarGridSpec(
        num_scalar_prefetch=1, grid=(4, nt),
        in_specs=[pl.BlockSpec((tt, K), lambda j, t, chip: (t, 0)),
                  pl.BlockSpec((tt, n), lambda j, t, chip: (t, chip[0] ^ jnp.where(j == 0, 3, jnp.where(j == 3, 0, j))))]
        + [ANY] * nc,
        out_specs=[ANY] * (1 + nc),
        scratch_shapes=[pltpu.VMEM((K, n), F32)] + [pltpu.VMEM((4, half, n), BF16)] * 3
        + [pltpu.SemaphoreType.DMA((1,))] + [pltpu.SemaphoreType.DMA((4,))] * 2 + [pltpu.SemaphoreType.DMA((3,))] * 2
        + ([] if comm is None else list(comm.sem_shapes)))
    px, py, _ = _place()
    res = pl.pallas_call(
        body, name=name, grid_spec=grid_spec,
        out_shape=[_sds((4, half, n), BF16)] + ([] if comm is None else list(comm.out_shapes)),
        compiler_params=pltpu.CompilerParams(dimension_semantics=("arbitrary", "arbitrary"), vmem_limit_bytes=VMEM_BIG),
    )((2 * px + py).astype(jnp.int32).reshape(1), x, y, *([] if comm is None else comm.inputs))
    return res[0], res[1:]


def _swap_halves(t):
    w = t.shape[1]
    lane = lax.broadcasted_iota(jnp.int32, t.shape, 1)
    return jnp.where(lane % HD < HD // 2, pltpu.roll(t, w - HD // 2, 1), pltpu.roll(t, HD // 2, 1))


def _rope(t, cos, sin_signed):
    c = jnp.tile(cos, (1, t.shape[1] // cos.shape[1]))
    s = jnp.tile(sin_signed, (1, t.shape[1] // sin_signed.shape[1]))
    return t * c + _swap_halves(t) * s


def _rope_bwd(dt, cos, sin_signed):
    c = jnp.tile(cos, (1, dt.shape[1] // cos.shape[1]))
    s = jnp.tile(sin_signed, (1, dt.shape[1] // sin_signed.shape[1]))
    return dt * c + _swap_halves(dt * s)


def _rm_spec(dil):
    return pl.BlockSpec((dil, TM // dil, GA), lambda i: (0, i, 0))


def _to_residues(t, dst_ref, scr_ref, dil):
    if dil == 1:
        dst_ref[0] = t.astype(dst_ref.dtype)
        return
    for j in range(GA // LANES):
        scr_ref[j] = t[:, j * LANES:(j + 1) * LANES]
    for r in range(dil):
        for j in range(GA // LANES):
            rows = scr_ref.at[j][pl.ds(r, TM // dil, stride=dil), :]
            dst_ref[r, :, j * LANES:(j + 1) * LANES] = rows.astype(dst_ref.dtype)


def _from_residues(src_ref, scr_ref, dil):
    if dil == 1:
        return src_ref[0].astype(F32)
    for r in range(dil):
        for j in range(GA // LANES):
            scr_ref.at[j][pl.ds(r, TM // dil, stride=dil), :] = src_ref[r, :, j * LANES:(j + 1) * LANES].astype(F32)
    return jnp.concatenate([scr_ref[j] for j in range(GA // LANES)], axis=1)


def _mix_proj(h, vec, win, cos, sin, comm=None):
    T = h.shape[0]

    def body(h_ref, vec_ref, win_hbm, cos_ref, sin_ref, u_ref, p_ref, *rest):
        qkv_refs, gates_ref, win_v, scr_ref, sems = rest[:3 * NG], rest[3 * NG], rest[3 * NG + 1], rest[3 * NG + 2], rest[3 * NG + 3]
        _load_once([(win_hbm, win_v)], sems)
        g, sh, sc = vec_ref[0:1, :], vec_ref[1:2, :], vec_ref[2:3, :]
        _, _, _, u = _norm_fwd(h_ref[...], g, sh, sc)
        ub = u.astype(BF16)
        u_ref[...] = ub
        mixer_cols = PW + 3 * NG * GA
        proj = _dot(ub, win_v[:, 0:mixer_cols])
        p_ref[...] = proj[:, 0:PW]
        cos_t, sin_t = cos_ref[...], sin_ref[...]
        for j in range(3 * NG):
            col = PW + j * GA
            t = proj[:, col:col + GA]
            if j < 2 * NG:
                t = _rope(t, cos_t, sin_t)
            _to_residues(t, qkv_refs[j], scr_ref, DIL[j % NG])
        gates_ref[...] = jax.nn.sigmoid(_dot(ub, win_v[:, mixer_cols:INW])).astype(BF16)

    outs, c_outs = _call(
        body, "mix_proj", (T // TM,),
        [_rows(TM, D), _const((8, D)), ANY, _rows(TM, 128), _rows(TM, 128)],
        [_rows(TM, D), _rows(TM, PW)] + [_rm_spec(d) for d in DIL] * 3 + [_rows(TM, GW)],
        [_sds((T, D), BF16), _sds((T, PW), F32)] + [_sds((d, T // d, GA), BF16) for d in DIL] * 3 + [_sds((T, GW), BF16)],
        scratch=[pltpu.VMEM((D, INW), BF16), pltpu.VMEM((GA // LANES, TM, LANES), F32), pltpu.SemaphoreType.DMA((1,))],
        vmem=VMEM_BIG, comm=comm,
    )(h, vec, win, cos, sin)
    return (outs[0], outs[1], outs[2:2 + NG], outs[2 + NG:2 + 2 * NG], outs[2 + 2 * NG:2 + 3 * NG], outs[2 + 3 * NG]), c_outs


def _head_masks():
    lane_head = lax.broadcasted_iota(jnp.int32, (BLK, GA), 1) // HD
    return [lane_head == hd for hd in range(NH)]


def _expand_heads(t, hm):
    return jnp.concatenate([jnp.where(m, t, jnp.zeros_like(t)) for m in hm], axis=0)


def _collapse_heads(tb, hm):
    out = None
    for hd, m in enumerate(hm):
        part = jnp.where(m, tb[hd * BLK:(hd + 1) * BLK, :], 0.0)
        out = part if out is None else out + part
    return out


def _head_rows(t):
    return jnp.concatenate([t[:, hd * HD:hd * HD + 1] for hd in range(NH)], axis=0)


def _band(has_prev):
    a = lax.broadcasted_iota(jnp.int32, (NH * BLK, 2 * BLK), 0) & (BLK - 1)
    c = lax.broadcasted_iota(jnp.int32, (NH * BLK, 2 * BLK), 1)
    return jnp.logical_and(c >= jnp.where(has_prev, a, BLK), c <= a + BLK)


def _attn_specs(nbt):
    cur = pl.BlockSpec((2 * BLK, GA), lambda i: (i, 0))
    prev = pl.BlockSpec((BLK, GA), lambda i: (jnp.maximum(2 * i - 1, 0), 0))
    nxt = pl.BlockSpec((BLK, GA), lambda i: (jnp.minimum(2 * i + 2, nbt - 1), 0))
    return cur, prev, nxt


def _attn_fwd(q, k, v, nb, name, comm=None):
    T = q.shape[0]
    nbt = T // BLK
    lo, hi = slice(0, BLK), slice(BLK, 2 * BLK)

    def block(qv, kcat, vcat, has_prev, hm):
        s = jnp.where(_band(has_prev), _dot_nt(_expand_heads(qv, hm), kcat) * SCALE, NEG)
        mx = jnp.max(s, axis=-1, keepdims=True)
        e = jnp.exp(s - mx)
        l = jnp.sum(e, axis=-1, keepdims=True)
        ob = _dot((e * (1.0 / l)).astype(BF16), vcat)
        return _collapse_heads(ob, hm), _collapse_heads(jnp.broadcast_to(mx + jnp.log(l), (NH * BLK, GA)), hm)

    def body(q_ref, k_ref, kp_ref, v_ref, vp_ref, o_ref, lse_ref):
        b0 = 2 * pl.program_id(0)
        hm = _head_masks()
        k_first = jnp.concatenate([kp_ref[...], k_ref[lo, :]], axis=0)
        v_first = jnp.concatenate([vp_ref[...], v_ref[lo, :]], axis=0)
        o_ref[lo, :], lse_ref[lo, :] = block(q_ref[lo, :], k_first, v_first, (b0 & (nb - 1)) != 0, hm)
        o_ref[hi, :], lse_ref[hi, :] = block(q_ref[hi, :], k_ref[...], v_ref[...], ((b0 + 1) & (nb - 1)) != 0, hm)

    cur, prev, _ = _attn_specs(nbt)
    return _call(body, name, (nbt // 2,), [cur, cur, prev, cur, prev], [cur, cur],
                 [_sds((T, GA), F32), _sds((T, GA), F32)], comm=comm)(q, k, k, v, v)


def _attn_bwd(q, k, v, do, lse, e, nb, name, comm=None):
    T = q.shape[0]
    nbt = T // BLK

    lo, hi = slice(0, BLK), slice(BLK, 2 * BLK)

    def probs_and_ds(qb, dob, kcat, vcat, lsev, ev, valid):
        p = jnp.where(valid, jnp.exp(_dot_nt(qb, kcat) * SCALE - _head_rows(lsev)), 0.0)
        return p, (p * (_dot_nt(dob, vcat) + _head_rows(ev))).astype(BF16)

    def body(q_ref, k_ref, v_ref, do_ref, lse_ref, e_ref, kp_ref, vp_ref, qn_ref, don_ref, lsen_ref, en_ref,
             dq_ref, dk_ref, dv_ref):
        b0 = 2 * pl.program_id(0)
        hm = _head_masks()
        q1, q2, q3 = _expand_heads(q_ref[lo, :], hm), _expand_heads(q_ref[hi, :], hm), _expand_heads(qn_ref[...], hm)
        do1, do2, do3 = (_expand_heads(do_ref[lo, :], hm), _expand_heads(do_ref[hi, :], hm),
                         _expand_heads(don_ref[...], hm))
        k1 = jnp.concatenate([kp_ref[...], k_ref[lo, :]], axis=0)
        v1 = jnp.concatenate([vp_ref[...], v_ref[lo, :]], axis=0)
        k2, v2 = k_ref[...], v_ref[...]
        p1, ds1 = probs_and_ds(q1, do1, k1, v1, lse_ref[lo, :], e_ref[lo, :], _band((b0 & (nb - 1)) != 0))
        p2, ds2 = probs_and_ds(q2, do2, k2, v2, lse_ref[hi, :], e_ref[hi, :], _band(((b0 + 1) & (nb - 1)) != 0))
        dq_ref[lo, :] = _collapse_heads(_dot(ds1, k1) * SCALE, hm)
        dq_ref[hi, :] = _collapse_heads(_dot(ds2, k2) * SCALE, hm)
        a = lax.broadcasted_iota(jnp.int32, (NH * BLK, BLK), 0) & (BLK - 1)
        c = lax.broadcasted_iota(jnp.int32, (NH * BLK, BLK), 1)
        valid3 = jnp.logical_and(c >= a, ((b0 + 2) & (nb - 1)) != 0)
        p3, ds3 = probs_and_ds(q3, do3, k_ref[hi, :], v_ref[hi, :], lsen_ref[...], en_ref[...], valid3)
        q12, q23 = jnp.concatenate([q1, q2], axis=0), jnp.concatenate([q2, q3], axis=0)
        do12, do23 = jnp.concatenate([do1, do2], axis=0), jnp.concatenate([do2, do3], axis=0)
        dk_ref[lo, :] = _dot_tn(jnp.concatenate([ds1[:, BLK:], ds2[:, :BLK]], axis=0), q12) * SCALE
        dk_ref[hi, :] = _dot_tn(jnp.concatenate([ds2[:, BLK:], ds3], axis=0), q23) * SCALE
        pb1, pb2, pb3 = p1.astype(BF16), p2.astype(BF16), p3.astype(BF16)
        dv_ref[lo, :] = _dot_tn(jnp.concatenate([pb1[:, BLK:], pb2[:, :BLK]], axis=0), do12).astype(BF16)
        dv_ref[hi, :] = _dot_tn(jnp.concatenate([pb2[:, BLK:], pb3], axis=0), do23).astype(BF16)

    cur, prev, nxt = _attn_specs(nbt)
    return _call(body, name, (nbt // 2,), [cur] * 6 + [prev, prev] + [nxt] * 4, [cur, cur, cur],
                 [_sds((T, GA), F32), _sds((T, GA), F32), _sds((T, GA), BF16)],
                 comm=comm)(q, k, v, do, lse, e, k, v, q, do, lse, e)


def _flat(t):
    return t.reshape(t.shape[0] * t.shape[1], t.shape[2])


def _by_residue(t, dil):
    return t.reshape(dil, t.shape[0] // dil, t.shape[1])


def _pool_consts(shape, row0):
    lane = lax.broadcasted_iota(jnp.int32, shape, 1)
    t = lax.broadcasted_iota(jnp.int32, shape, 0) + row0
    grp = lane // (PW // len(POOL_WINDOWS))
    win = jnp.where(grp == 0, POOL_WINDOWS[0], jnp.where(grp == 1, POOL_WINDOWS[1],
                    jnp.where(grp == 2, POOL_WINDOWS[2], POOL_WINDOWS[3])))
    cnt = jnp.minimum(t + 1, win).astype(F32)
    return grp, cnt


def _window_sums(ext_ref, base, step, tm):
    outs, run = [], None
    for j in range(POOL_WINDOWS[-1]):
        sl = ext_ref[pl.ds(base + step * j, tm), :]
        run = sl if run is None else run + sl
        if j + 1 in POOL_WINDOWS:
            outs.append(run)
    return outs


def _select_group(grp, vals):
    return jnp.where(grp == 0, vals[0], jnp.where(grp == 1, vals[1], jnp.where(grp == 2, vals[2], vals[3])))


def _pool_d(pc_ref, pp_ref, ext_ref, i, tm):
    ext_ref[0:HALO, :] = jnp.where(i > 0, pp_ref[tm - HALO:tm, :], 0.0)
    ext_ref[HALO:HALO + tm, :] = pc_ref[...]
    grp, cnt = _pool_consts((tm, PW), i * tm)
    sums = _window_sums(ext_ref, HALO, -1, tm)
    return _select_group(grp, sums) / cnt - pc_ref[...]


def _group_weights(ls):
    mx = jnp.maximum(jnp.maximum(ls[0], ls[1]), ls[2])
    es = [jnp.exp(l - mx) for l in ls]
    inv = 1.0 / (es[0] + es[1] + es[2])
    return [e * inv for e in es]


def _mix_merge(h, vec, p, os, lses, gates, wp_bd, pscale, wpb, wab, wout):
    T = h.shape[0]

    def body(h_ref, vec_ref, pc_ref, pp_ref, o0, o1, o2, l0, l1, l2, gates_ref, wp_ref, ps_ref, wpb_ref, wab_ref, wout_ref,
             ho_ref, yp_ref, ya_ref, mg_ref, mo_ref, d_ref, ext_ref, scr_ref):
        i = pl.program_id(0)
        gt = vec_ref[3:4, :]
        d = _pool_d(pc_ref, pp_ref, ext_ref, i, TM).astype(BF16)
        d_ref[...] = d
        ypool = (_dot(d, wp_ref[...]) * ps_ref[0:1, :]).astype(BF16)
        yp_ref[...] = ypool
        w = _group_weights([_from_residues(r, scr_ref, dl) for r, dl in zip((l0, l1, l2), DIL)])
        yattn = None
        for wg, o_ref, dl in zip(w, (o0, o1, o2), DIL):
            part = wg * _from_residues(o_ref, scr_ref, dl)
            yattn = part if yattn is None else yattn + part
        yattn = yattn.astype(BF16)
        ya_ref[...] = yattn
        merged = (gates_ref[:, 0:D].astype(F32) * _dot(ypool, wpb_ref[...])
                  + gates_ref[:, D:GW].astype(F32) * _dot(yattn, wab_ref[...])).astype(BF16)
        mg_ref[...] = merged
        mo = _dot(merged, wout_ref[...])
        mo_ref[...] = mo.astype(BF16)
        ho_ref[...] = h_ref[...] + gt * mo

    prev = pl.BlockSpec((TM, PW), lambda i: (jnp.maximum(i - 1, 0), 0))
    return _call(
        body, "mix_merge", (T // TM,),
        [_rows(TM, D), _const((8, D)), _rows(TM, PW), prev] + [_rm_spec(dl) for dl in DIL] * 2 + [_rows(TM, GW), _const((PW, PW)),
         _const((8, PW)), _const((PW, D)), _const((GA, D)), _const((D, D))],
        [_rows(TM, D), _rows(TM, PW), _rows(TM, GA), _rows(TM, D), _rows(TM, D), _rows(TM, PW)],
        [_sds((T, D), F32), _sds((T, PW), BF16), _sds((T, GA), BF16), _sds((T, D), BF16), _sds((T, D), BF16), _sds((T, PW), BF16)],
        scratch=[pltpu.VMEM((TM + HALO, PW), F32), pltpu.VMEM((GA // LANES, TM, LANES), F32)],
        vmem=VMEM_BIG,
    )(h, vec, p, p, *os, *lses, gates, wp_bd, pscale, wpb, wab, wout)[0]


def _mix_bwd_a(dh, vec, mixout, merged, gates, ypool, yattn, dpool, os, lses, wp_bd, pscale, wpb, wab, wout, ones_bd,
               comm=None):
    T = dh.shape[0]
    nt = T // TM

    def body(dh_ref, vec_ref, mo_ref, mg_ref, gates_ref, yp_ref, ya_ref, d_ref, o0, o1, o2, l0, l1, l2,
             wp_ref, ps_ref, wpb_ref, wab_ref, wout_ref, ones_ref,
             dgates_ref, do0, do1, do2, e0, e1, e2, dd_ref, acc_ref, acc2_ref, g_out_ref, g_pb_ref, g_ab_ref, g_pool_ref,
             scr_ref, a_out, a_pb, a_ab, a_pool):
        _zero_first(acc_ref)
        _zero_first(acc2_ref)
        for a_ref in (a_out, a_pb, a_ab, a_pool):
            _zero_first(a_ref)
        gt = vec_ref[3:4, :]
        dho = dh_ref[...]
        acc_ref[3:4, :] += _colsum(dho * mo_ref[...].astype(F32))
        dmo = (gt * dho).astype(BF16)
        a_out[...] += _dot_tn(mg_ref[...], dmo)
        dmerged = _dot_nt(dmo, wout_ref[...])
        gp = gates_ref[:, 0:D].astype(F32)
        ga = gates_ref[:, D:GW].astype(F32)
        bp = _dot(yp_ref[...], wpb_ref[...])
        ba = _dot(ya_ref[...], wab_ref[...])
        dgates_ref[:, 0:D] = (dmerged * bp * gp * (1.0 - gp)).astype(BF16)
        dgates_ref[:, D:GW] = (dmerged * ba * ga * (1.0 - ga)).astype(BF16)
        dbp = (dmerged * gp).astype(BF16)
        dba = (dmerged * ga).astype(BF16)
        a_pb[...] += _dot_tn(yp_ref[...], dbp)
        a_ab[...] += _dot_tn(ya_ref[...], dba)
        dypool = _dot_nt(dbp, wpb_ref[...])
        ypre = _dot(d_ref[...], wp_ref[...])
        acc2_ref[0:1, :] += _colsum(dypool * ypre)
        dyp = (dypool * ps_ref[0:1, :]).astype(BF16)
        a_pool[...] += _dot_tn(d_ref[...], dyp)
        dd_ref[...] = _dot_nt(dyp, wp_ref[...])
        dya = _dot_nt(dba, wab_ref[...])
        w = _group_weights([_from_residues(r, scr_ref, dl) for r, dl in zip((l0, l1, l2), DIL)])
        ya = None
        for wg, o_ref, dl in zip(w, (o0, o1, o2), DIL):
            part = wg * _from_residues(o_ref, scr_ref, dl)
            ya = part if ya is None else ya + part
        prod = dya * ya
        hi = prod.astype(BF16)
        lo = (prod - hi.astype(F32)).astype(BF16)
        tot = _dot(hi, ones_ref[...]) + _dot(lo, ones_ref[...])
        for wg, do_ref, e_ref, dl in zip(w, (do0, do1, do2), (e0, e1, e2), DIL):
            _to_residues(wg * dya, do_ref, scr_ref, dl)
            _to_residues(-wg * tot, e_ref, scr_ref, dl)

        @pl.when(pl.program_id(0) == nt - 1)
        def _():
            g_out_ref[...] = a_out[...].astype(BF16)
            g_pb_ref[...] = a_pb[...].astype(BF16)
            g_ab_ref[...] = a_ab[...].astype(BF16)
            g_pool_ref[...] = a_pool[...]

    return _call(
        body, "mix_bwd_a", (nt,),
        [_rows(TM, D), _const((8, D)), _rows(TM, D), _rows(TM, D), _rows(TM, GW), _rows(TM, PW), _rows(TM, GA), _rows(TM, PW)]
        + [_rm_spec(dl) for dl in DIL] * 2
        + [_const((PW, PW)), _const((8, PW)), _const((PW, D)), _const((GA, D)), _const((D, D)), _const((GA, GA))],
        [_rows(TM, GW)] + [_rm_spec(dl) for dl in DIL] * 2 + [_rows(TM, PW), _const((8, D)), _const((8, PW))]
        + [_const((D, D)), _const((PW, D)), _const((GA, D)), _const((PW, PW))],
        [_sds((T, GW), BF16)] + [_sds((dl, T // dl, GA), BF16) for dl in DIL]
        + [_sds((dl, T // dl, GA), F32) for dl in DIL] + [_sds((T, PW), F32), _sds((8, D), F32), _sds((8, PW), F32)]
        + [_sds((D, D), BF16), _sds((PW, D), BF16), _sds((GA, D), BF16), _sds((PW, PW), F32)],
        scratch=[pltpu.VMEM((GA // LANES, TM, LANES), F32), pltpu.VMEM((D, D), F32), pltpu.VMEM((PW, D), F32),
                 pltpu.VMEM((GA, D), F32), pltpu.VMEM((PW, PW), F32)],
        vmem=VMEM_BIG, comm=comm,
    )(dh, vec, mixout, merged, gates, ypool, yattn, dpool, *os, *lses, wp_bd, pscale, wpb, wab, wout, ones_bd)


def _mix_bwd_b(dh, h, vec, dd, dqs, dks, dvs, dgates, cos, sin, win):
    T = h.shape[0]
    nt = T // TM

    def body(dh_ref, h_ref, vec_ref, ddc_ref, ddn_ref, *rest):
        qk_refs, dv_refs = rest[:2 * NG], rest[2 * NG:3 * NG]
        dgates_ref, cos_ref, sin_ref, win_hbm, dhi_ref, dproj_ref, acc_ref, win_v, ext_ref, scr_ref, sems = rest[3 * NG:]
        i = pl.program_id(0)
        _load_once([(win_hbm, win_v)], sems)
        _zero_first(acc_ref)
        g, sh, sc = vec_ref[0:1, :], vec_ref[1:2, :], vec_ref[2:3, :]
        grp, cnt = _pool_consts((TM, PW), i * TM)
        _, cnt_n = _pool_consts((HALO, PW), (i + 1) * TM)
        ext_ref[0:TM, :] = ddc_ref[...] / cnt
        ext_ref[TM:TM + HALO, :] = jnp.where(i < nt - 1, ddn_ref[0:HALO, :] / cnt_n, 0.0)
        dp = _select_group(grp, _window_sums(ext_ref, 0, 1, TM)) - ddc_ref[...]
        dproj_ref[:, 0:PW] = dp.astype(BF16)
        cos_t, sin_t = cos_ref[...], sin_ref[...]
        for j in range(2 * NG):
            col = PW + j * GA
            dt = _from_residues(qk_refs[j], scr_ref, DIL[j % NG])
            dproj_ref[:, col:col + GA] = _rope_bwd(dt, cos_t, sin_t).astype(BF16)
        for j in range(NG):
            col = PW + (2 * NG + j) * GA
            dproj_ref[:, col:col + GA] = _from_residues(dv_refs[j], scr_ref, DIL[j]).astype(BF16)
        dproj_ref[:, PW + 3 * NG * GA:INW] = dgates_ref[...]
        du = None
        for j in range(INW // 512):
            part = _dot_nt(dproj_ref[:, j * 512:(j + 1) * 512], win_v[:, j * 512:(j + 1) * 512])
            du = part if du is None else du + part
        xh, r, n, _ = _norm_fwd(h_ref[...], g, sh, sc)
        dhn, dsh, dsc, dg = _norm_bwd(du, xh, r, n, g, sc)
        dhi_ref[...] = dh_ref[...] + dhn
        acc_ref[0:1, :] += dsh
        acc_ref[1:2, :] += dsc
        acc_ref[2:3, :] += dg

    nxt = pl.BlockSpec((TM, PW), lambda i: (jnp.minimum(i + 1, nt - 1), 0))
    return _call(
        body, "mix_bwd_b", (nt,),
        [_rows(TM, D), _rows(TM, D), _const((8, D)), _rows(TM, PW), nxt] + [_rm_spec(dl) for dl in DIL] * 3
        + [_rows(TM, GW), _rows(TM, 128), _rows(TM, 128), ANY],
        [_rows(TM, D), _rows(TM, INW), _const((8, D))],
        [_sds((T, D), F32), _sds((T, INW), BF16), _sds((8, D), F32)],
        scratch=[pltpu.VMEM((D, INW), BF16), pltpu.VMEM((TM + HALO, PW), F32), pltpu.VMEM((GA // LANES, TM, LANES), F32),
                 pltpu.SemaphoreType.DMA((1,))],
        vmem=VMEM_BIG,
    )(dh, h, vec, dd, dd, *dqs, *dks, *dvs, dgates, cos, sin, win)[0]


def _ada_fwd(c_all, w_shard, b_shard):
    n = w_shard.shape[1]

    def body(c_ref, w_ref, b_ref, o_ref):
        cv = c_ref[...]
        cond = (cv * jax.nn.sigmoid(cv)).astype(BF16)
        o_ref[...] = _dot(cond, w_ref[...].astype(BF16)) + b_ref[...]

    tn = n // 3
    return pl.pallas_call(
        body, name="ada_fwd", grid=(3,),
        in_specs=[pl.BlockSpec((8, D), lambda j: (0, 0)), pl.BlockSpec((D, tn), lambda j: (0, j)), pl.BlockSpec((1, tn), lambda j: (0, j))],
        out_specs=pl.BlockSpec((8, tn), lambda j: (0, j)), out_shape=_sds((8, n), F32),
        compiler_params=pltpu.CompilerParams(dimension_semantics=("arbitrary",)),
    )(c_all, w_shard, b_shard)


def _ada_bwd(c_all, dmod_shard):
    n = dmod_shard.shape[1]

    def body(c_ref, d_ref, o_ref):
        cv = c_ref[...]
        cond = (cv * jax.nn.sigmoid(cv)).astype(BF16)
        o_ref[...] = _dot_tn(cond, d_ref[...].astype(BF16))

    tn = n // 3
    return pl.pallas_call(
        body, name="ada_bwd", grid=(3,),
        in_specs=[pl.BlockSpec((8, D), lambda j: (0, 0)), pl.BlockSpec((8, tn), lambda j: (0, j))],
        out_specs=pl.BlockSpec((D, tn), lambda j: (0, j)), out_shape=_sds((D, n), F32),
        compiler_params=pltpu.CompilerParams(dimension_semantics=("arbitrary",)),
    )(c_all, dmod_shard)


def _adam_math(w, g, m, v):
    m2 = B1 * m + (1.0 - B1) * g
    v2 = B2 * v + (1.0 - B2) * (g * g)
    m_hat = m2 / (1.0 - B1 ** STEP)
    v_hat = v2 / (1.0 - B2 ** STEP)
    delta = -LR * (m_hat / (jnp.sqrt(v_hat) + AEPS) + WD * w)
    return delta, m2, v2


def _adam(w, m, v, parts, name, comm=None):
    R, C = w.shape
    tr = R
    for cand in (128, 64, 32, 16, 8):
        if R % cand == 0:
            tr = cand
            break
    np_ = len(parts)

    def body(w_ref, m_ref, v_ref, *rest):
        p_refs, (g_ref, d_ref, m2_ref, v2_ref) = rest[:np_], rest[np_:]
        g = p_refs[0][...]
        for pr in p_refs[1:]:
            g = g + pr[...]
        delta, m2, v2 = _adam_math(w_ref[...], g, m_ref[...], v_ref[...])
        g_ref[...] = g
        d_ref[...] = delta
        m2_ref[...] = m2
        v2_ref[...] = v2

    spec = pl.BlockSpec((tr, C), lambda i: (i, 0))
    return _call(body, name, (R // tr,), [spec] * (3 + np_), [spec] * 4, [_sds((R, C), F32)] * 4,
                 vmem=VMEM_BIG, comm=comm)(w, m, v, *parts)


def _adam_halves(w, m, v, mine, other, name):
    R, C = w.shape
    tr = 128
    nh = R // 2 // tr

    def body(c_ref, w_ref, m_ref, v_ref, mine_ref, other_ref, g_ref, d_ref, m2_ref, v2_ref):
        i = pl.program_id(0)
        in_mine = jnp.logical_and(i >= c_ref[0] * nh, i < (c_ref[0] + 1) * nh)
        g = jnp.where(in_mine, mine_ref[...], other_ref[...])
        delta, m2, v2 = _adam_math(w_ref[...], g, m_ref[...], v_ref[...])
        g_ref[...] = g
        d_ref[...] = delta
        m2_ref[...] = m2
        v2_ref[...] = v2

    spec = pl.BlockSpec((tr, C), lambda i, c: (i, 0))
    grid_spec = pltpu.PrefetchScalarGridSpec(
        num_scalar_prefetch=1, grid=(R // tr,),
        in_specs=[spec] * 3 + [pl.BlockSpec((tr, C), lambda i, c: (jnp.clip(i - c[0] * nh, 0, nh - 1), 0)),
                               pl.BlockSpec((tr, C), lambda i, c: (jnp.clip(i - (1 - c[0]) * nh, 0, nh - 1), 0))],
        out_specs=[spec] * 4)
    return pl.pallas_call(
        body, name=name, grid_spec=grid_spec, out_shape=[_sds((R, C), F32)] * 4,
        compiler_params=pltpu.CompilerParams(dimension_semantics=("arbitrary",), vmem_limit_bytes=VMEM_BIG),
    )(lax.axis_index("c").astype(jnp.int32).reshape(1), w, m, v, mine, other)


def _adam_small(ws, ms, vs, gathered):
    n = len(ws)
    sizes = [a.shape[1] for a in ws]

    def total(ga_ref, off, size):
        g = ga_ref[0, :, off:off + size]
        for dev in range(1, 8):
            g = g + ga_ref[dev, :, off:off + size]
        return g

    def body(*refs):
        w_refs, m_refs, v_refs, ga_ref, outs = refs[:n], refs[n:2 * n], refs[2 * n:3 * n], refs[3 * n], refs[3 * n + 1:]
        off = 0
        for j, size in enumerate(sizes):
            g = total(ga_ref, off, size)
            delta, m2, v2 = _adam_math(w_refs[j][...], g, m_refs[j][...], v_refs[j][...])
            for ref, val in zip(outs[4 * j:4 * j + 4], (g, delta, m2, v2)):
                ref[...] = val
            off += size
        outs[4 * n][...] = total(ga_ref, off, 128)

    res = pl.pallas_call(
        body, name="adam_small",
        out_shape=[_sds((1, size), F32) for size in sizes for _ in range(4)] + [_sds((1, 128), F32)],
    )(*ws, *ms, *vs, gathered)
    return [res[4 * j:4 * j + 4] for j in range(n)], res[4 * n]


def _sum4(blocks, name):
    _, R, C = blocks.shape
    tr = R
    for cand in (256, 128, 64, 32, 16):
        if R % cand == 0:
            tr = cand
            break

    def body(r_ref, out_ref):
        out_ref[...] = ((r_ref[0].astype(F32) + r_ref[1].astype(F32)) + r_ref[2].astype(F32)) + r_ref[3].astype(F32)

    return pl.pallas_call(
        body, name=name, grid=(R // tr,),
        in_specs=[pl.BlockSpec((4, tr, C), lambda i: (0, i, 0))],
        out_specs=pl.BlockSpec((tr, C), lambda i: (i, 0)), out_shape=_sds((R, C), F32),
        compiler_params=pltpu.CompilerParams(dimension_semantics=("arbitrary",)),
    )(blocks)


def _place():
    return lax.axis_index("x"), lax.axis_index("y"), lax.axis_index("c")


def _chip_peer(x, y, c, m):
    return (x ^ (m >> 1), y ^ (m & 1), c)


def _shard_ref(ref, axis, k, n):
    start = pl.multiple_of(k * n, 128 if axis == 1 else 16)
    return ref.at[:, pl.ds(start, n)] if axis == 1 else ref.at[pl.ds(start, n), :]


def _half_rows(ref, axis, k, n, hc):
    if axis == 1:
        half = ref.shape[0] // 2
        return ref.at[pl.ds(pl.multiple_of(hc * half, 16), half), pl.ds(pl.multiple_of(k * n, 128), n)]
    half = n // 2
    return ref.at[pl.ds(pl.multiple_of(k * n + hc * half, 16), half), :]


class _GatherPlan:
    def __init__(self, shards, axes):
        self.inputs, self.axes, nw = list(shards), list(axes), len(shards)
        self.out_shapes = [_sds((s.shape[0] * (4 if ax == 0 else 1), s.shape[1] * (4 if ax == 1 else 1)), BF16)
                           for s, ax in zip(shards, axes)]
        self.sem_shapes = [pltpu.SemaphoreType.DMA((nw,))] + [pltpu.SemaphoreType.DMA((nw, 3))] * 4

    def _copies(self, ins, outs, sems):
        local_sems, send_sems, recv_sems, pass_sems, got_sems = sems
        x, y, c = _place()
        k = 2 * x + y
        local, sends, arrivals, passes, handed = [], [], [], [], []
        for j, ax in enumerate(self.axes):
            n = ins[j].shape[ax]
            half = ins[j].shape[0] // 2
            local.append(pltpu.make_async_copy(ins[j], _shard_ref(outs[j], ax, k, n), local_sems.at[j]))
            my_half = ins[j].at[pl.ds(pl.multiple_of(c * half, 16), half), :]
            for m in range(1, 4):
                sends.append(pltpu.make_async_remote_copy(
                    src_ref=my_half, dst_ref=_half_rows(outs[j], ax, k, n, c), send_sem=send_sems.at[j, m - 1],
                    recv_sem=recv_sems.at[j, m - 1], device_id=_chip_peer(x, y, c, m), device_id_type=MESH))
                theirs = _half_rows(outs[j], ax, k ^ m, n, c)
                arrivals.append(pltpu.make_async_remote_copy(
                    src_ref=my_half, dst_ref=theirs, send_sem=send_sems.at[j, m - 1], recv_sem=recv_sems.at[j, m - 1],
                    device_id=(x, y, c), device_id_type=MESH))
                passes.append(pltpu.make_async_remote_copy(
                    src_ref=theirs, dst_ref=theirs, send_sem=pass_sems.at[j, m - 1], recv_sem=got_sems.at[j, m - 1],
                    device_id=(x, y, 1 - c), device_id_type=MESH))
                other = _half_rows(outs[j], ax, k ^ m, n, 1 - c)
                handed.append(pltpu.make_async_remote_copy(
                    src_ref=other, dst_ref=other, send_sem=pass_sems.at[j, m - 1], recv_sem=got_sems.at[j, m - 1],
                    device_id=(x, y, c), device_id_type=MESH))
        return local, sends, arrivals, passes, handed

    def start(self, ins, outs, sems):
        local, sends, _, _, _ = self._copies(ins, outs, sems)
        for cp in local + sends:
            cp.start()

    def relay(self, ins, outs, sems):
        _, _, arrivals, passes, _ = self._copies(ins, outs, sems)
        for arrived, onward in zip(arrivals, passes):
            arrived.wait_recv()
            onward.start()

    def wait(self, ins, outs, sems):
        local, sends, _, passes, handed = self._copies(ins, outs, sems)
        for cp in handed:
            cp.wait_recv()
        for cp in sends + passes:
            cp.wait_send()
        for cp in local:
            cp.wait()


class _ScatterPlan:
    def __init__(self, grads, axes):
        self.inputs, self.axes, nw = list(grads), list(axes), len(grads)
        self.shard_shapes = [(g.shape[0] // (4 if ax == 0 else 1), g.shape[1] // (4 if ax == 1 else 1))
                             for g, ax in zip(grads, axes)]
        self.out_shapes = [_sds((4,) + s, BF16) for s in self.shard_shapes]
        self.sem_shapes = [pltpu.SemaphoreType.DMA((nw,)), pltpu.SemaphoreType.DMA((nw, 3)), pltpu.SemaphoreType.DMA((nw, 3))]

    def _copies(self, ins, outs, sems):
        local_sems, send_sems, recv_sems = sems
        x, y, c = _place()
        k = 2 * x + y
        local, remote, arrivals = [], [], []
        for j, ax in enumerate(self.axes):
            n = self.shard_shapes[j][ax]
            local.append(pltpu.make_async_copy(_shard_ref(ins[j], ax, k, n), outs[j].at[0], local_sems.at[j]))
            for m in range(1, 4):
                remote.append(pltpu.make_async_remote_copy(
                    src_ref=_shard_ref(ins[j], ax, k ^ m, n), dst_ref=outs[j].at[m],
                    send_sem=send_sems.at[j, m - 1], recv_sem=recv_sems.at[j, m - 1],
                    device_id=_chip_peer(x, y, c, m), device_id_type=MESH))
                arrivals.append(pltpu.make_async_remote_copy(
                    src_ref=_shard_ref(ins[j], ax, k, n), dst_ref=outs[j].at[m],
                    send_sem=send_sems.at[j, m - 1], recv_sem=recv_sems.at[j, m - 1],
                    device_id=(x, y, c), device_id_type=MESH))
        return local, remote, arrivals

    def start(self, ins, outs, sems):
        local, remote, _ = self._copies(ins, outs, sems)
        for cp in local + remote:
            cp.start()

    def relay(self, ins, outs, sems):
        pass

    def wait(self, ins, outs, sems):
        local, remote, arrivals = self._copies(ins, outs, sems)
        for cp in arrivals:
            cp.wait_recv()
        for cp in remote:
            cp.wait_send()
        for cp in local:
            cp.wait()


def _run_plan(plan, name):
    nc = len(plan.inputs)

    def body(*refs):
        ins, outs, sems = refs[:nc], refs[nc:2 * nc], refs[2 * nc:]
        plan.start(ins, outs, sems)
        plan.relay(ins, outs, sems)
        plan.wait(ins, outs, sems)

    return pl.pallas_call(body, name=name, in_specs=[ANY] * nc, out_specs=[ANY] * nc, out_shape=list(plan.out_shapes),
                          scratch_shapes=list(plan.sem_shapes))(*plan.inputs)


class _SwapPlan:
    def __init__(self, parts):
        self.inputs, nw = list(parts), len(parts)
        self.out_shapes = [_sds(p.shape, p.dtype) for p in parts]
        self.sem_shapes = [pltpu.SemaphoreType.DMA((nw,)), pltpu.SemaphoreType.DMA((nw,))]

    def _copies(self, ins, outs, sems):
        send_sems, recv_sems = sems
        x, y, c = _place()
        return [pltpu.make_async_remote_copy(
            src_ref=ins[j], dst_ref=outs[j], send_sem=send_sems.at[j], recv_sem=recv_sems.at[j],
            device_id=(x, y, 1 - c), device_id_type=MESH) for j in range(len(ins))]

    def start(self, ins, outs, sems):
        for cp in self._copies(ins, outs, sems):
            cp.start()

    def relay(self, ins, outs, sems):
        pass

    def wait(self, ins, outs, sems):
        for cp in self._copies(ins, outs, sems):
            cp.wait()


class _SmallGatherPlan:
    def __init__(self, v):
        self.inputs = [v]
        self.out_shapes = [_sds((8,) + v.shape, v.dtype)]
        self.sem_shapes = [pltpu.SemaphoreType.DMA((1,)), pltpu.SemaphoreType.DMA((7,)), pltpu.SemaphoreType.DMA((7,))]

    def _copies(self, ins, outs, sems):
        (v_ref,), (out_ref,), (local_sem, send_sems, recv_sems) = ins, outs, sems
        x, y, c = _place()
        me = 4 * x + 2 * y + c
        local = pltpu.make_async_copy(v_ref, out_ref.at[me], local_sem.at[0])
        sends, arrivals = [], []
        for m in range(1, 8):
            px, py, pc = x ^ (m >> 2), y ^ ((m >> 1) & 1), c ^ (m & 1)
            sends.append(pltpu.make_async_remote_copy(
                src_ref=v_ref, dst_ref=out_ref.at[me], send_sem=send_sems.at[m - 1], recv_sem=recv_sems.at[m - 1],
                device_id=(px, py, pc), device_id_type=MESH))
            arrivals.append(pltpu.make_async_remote_copy(
                src_ref=v_ref, dst_ref=out_ref.at[4 * px + 2 * py + pc], send_sem=send_sems.at[m - 1],
                recv_sem=recv_sems.at[m - 1], device_id=(x, y, c), device_id_type=MESH))
        return local, sends, arrivals

    def start(self, ins, outs, sems):
        local, sends, _ = self._copies(ins, outs, sems)
        for cp in [local] + sends:
            cp.start()

    def relay(self, ins, outs, sems):
        pass

    def wait(self, ins, outs, sems):
        local, sends, arrivals = self._copies(ins, outs, sems)
        for cp in arrivals:
            cp.wait_recv()
        for cp in sends:
            cp.wait_send()
        local.wait()


class _PlanGroup:
    def __init__(self, plans):
        self.plans = [p for p in plans if p is not None]
        self.inputs = [a for p in self.plans for a in p.inputs]
        self.out_shapes = [s for p in self.plans for s in p.out_shapes]
        self.sem_shapes = [s for p in self.plans for s in p.sem_shapes]

    def _each(self, ins, outs, sems):
        i = s = 0
        for p in self.plans:
            n, ns = len(p.inputs), len(p.sem_shapes)
            yield p, ins[i:i + n], outs[i:i + n], sems[s:s + ns]
            i, s = i + n, s + ns

    def start(self, ins, outs, sems):
        for p, pi, po, ps in self._each(ins, outs, sems):
            p.start(pi, po, ps)

    def relay(self, ins, outs, sems):
        for p, pi, po, ps in self._each(ins, outs, sems):
            p.relay(pi, po, ps)

    def wait(self, ins, outs, sems):
        for p, pi, po, ps in self._each(ins, outs, sems):
            p.wait(pi, po, ps)

    def split(self, outs):
        res, i = [], 0
        for p in self.plans:
            res.append(outs[i:i + len(p.inputs)])
            i += len(p.inputs)
        return res


BIG = ("w_ffn1_in", "w_ffn1_out", "w_in", "w_pool_branch", "w_attn_branch", "w_out", "w_ffn2_in", "w_ffn2_out")
BIG_AXIS = {"w_ffn1_in": 1, "w_ffn1_out": 0, "w_in": 1, "w_pool_branch": 1, "w_attn_branch": 1, "w_out": 0,
            "w_ffn2_in": 1, "w_ffn2_out": 0}


class _Sharded:
    fused_scatter = True

    def __init__(self, shards):
        self.shards, self.full, self.recv = shards, {}, {}

    def gather_plan(self, names):
        return _GatherPlan([self.shards[n] for n in names], [BIG_AXIS[n.split("/")[0]] for n in names])

    def gather_now(self, names):
        self.gathered(names, _run_plan(self.gather_plan(names), "gather_" + names[0]))

    def gathered(self, names, outs):
        self.full.update(zip(names, outs))

    def scatter_plan(self, names, grads):
        return _ScatterPlan([grads[n] for n in names], [BIG_AXIS[n] for n in names])

    def scatter_now(self, names, grads):
        self.scattered(names, _run_plan(self.scatter_plan(names, grads), "scatter_" + names[0]))

    def scattered(self, names, outs):
        self.recv.update(zip(names, outs))


class _Whole:
    fused_scatter = False

    def __init__(self, full):
        self.full, self.recv = dict(full), {}

    def gather_plan(self, names):
        return None

    def gather_now(self, names):
        pass

    def gathered(self, names, outs):
        pass

    def scatter_plan(self, names, grads):
        return None

    def scatter_now(self, names, grads):
        pass

    def scattered(self, names, outs):
        pass


def _vec(rows):
    pad = [jnp.zeros((1, D), F32)] * (8 - len(rows))
    return jnp.concatenate([r.reshape(1, D) for r in rows] + pad, axis=0)


def _block_diag(w_pool):
    n, c = w_pool.shape[0], w_pool.shape[1]
    eye = jnp.eye(n, dtype=w_pool.dtype)
    return (eye[:, None, :, None] * w_pool[:, :, None, :]).reshape(n * c, n * c)


def _example_step(x, tgt, positions, mod, gains, w_pool, pool_scale, ws, pack=None):
    T = x.shape[0]
    assert (T // BLK // DIL[-1]) & (T // BLK // DIL[-1] - 1) == 0, "blocks per sequence must be a power of two"
    sh1, sc1, gt1, sh2, sc2, gt2, sh3, sc3, gt3 = [mod[j * D:(j + 1) * D] for j in range(NMOD)]
    g1, g2, g3, gf = gains
    vec1, vec2, vec3 = _vec([g1, sh1, sc1, gt1]), _vec([g2, sh2, sc2, gt2]), _vec([g3, sh3, sc3, gt3])
    inv_freq = 10000.0 ** (-jnp.arange(0, HD, 2, dtype=F32) / HD)
    ang = positions.astype(F32)[:, None] * inv_freq
    cos = jnp.tile(jnp.cos(ang), (1, 4))
    sin = jnp.tile(jnp.concatenate([-jnp.sin(ang), jnp.sin(ang)], axis=1), (1, 2))
    wp_bd = _block_diag(w_pool).astype(BF16)
    ones_bd = _block_diag(jnp.ones((NH, HD, HD), F32)).astype(BF16)
    ps = jnp.concatenate([pool_scale.reshape(1, PW), jnp.zeros((7, PW), F32)], axis=0)
    wb = ws.full

    if "w_ffn1_in" not in wb:
        ws.gather_now(["w_ffn1_in"])
    (u1, a1, b1), got = _ffn_ab(x, vec1, [wb["w_ffn1_in"]], "ffn1_ab", ws.gather_plan(["w_ffn1_out", "w_in"]))
    ws.gathered(["w_ffn1_out", "w_in"], got)
    mixw = ["w_pool_branch", "w_attn_branch", "w_out"]
    (h1, f1), got = _ffn_out(x, a1, b1, vec1, wb["w_ffn1_out"], "ffn1_out", ws.gather_plan(mixw))
    ws.gathered(mixw, got)
    (u2, p, qs, ks, vs, gates), got = _mix_proj(h1, vec2, wb["w_in"], cos, sin, ws.gather_plan(["w_ffn2_in/0"]))
    ws.gathered(["w_ffn2_in/0"], got)
    qs, ks, vs = [_flat(t) for t in qs], [_flat(t) for t in ks], [_flat(t) for t in vs]
    nbs = [T // d // BLK for d in DIL]
    os, lses = [], []
    for gi, riders in enumerate((["w_ffn2_out"], ["w_ffn2_in/1"], None)):
        (o, lse), got = _attn_fwd(qs[gi], ks[gi], vs[gi], nbs[gi], f"attn_fwd{gi}", riders and ws.gather_plan(riders))
        ws.gathered(riders or [], got)
        os.append(o)
        lses.append(lse)
    win3 = [wb["w_ffn2_in/0"], wb["w_ffn2_in/1"]] if "w_ffn2_in/0" in wb else [wb["w_ffn2_in"]]
    os_r = [_by_residue(t, d) for t, d in zip(os, DIL)]
    lses_r = [_by_residue(t, d) for t, d in zip(lses, DIL)]
    h2, ypool, yattn, merged, mixout, dpool = _mix_merge(
        h1, vec2, p, os_r, lses_r, gates, wp_bd, ps, wb["w_pool_branch"], wb["w_attn_branch"], wb["w_out"])
    (dh3, u3, a3, b3, f3, lacc), _ = _ffn_fwd(h2, vec3, win3, wb["w_ffn2_out"], "ffn2_fwd", head=(tgt, _vec([gf])))
    loss = 0.5 * jnp.sum(lacc[0]) / D

    grads = {}

    def wgrad_cols(name, xx, yy, riders, extra=None):
        group = _PlanGroup([ws.scatter_plan(riders, grads) if riders else None, extra])
        plan = group if group.plans else None
        if ws.fused_scatter:
            blocks, got = _wgrad_scatter(xx, yy, "wg_" + name, min(2048, T // 2), comm=plan)
            ws.scattered([name], [blocks])
        else:
            grads[name], got = _wgrad(xx, yy, "wg_" + name, D, 512, 1024, comm=plan)
        parts = group.split(got)
        if len(parts) > (extra is not None):
            ws.scattered(riders, parts[0])
        return parts[-1] if extra is not None else None

    (dh2, dab3, s3, df3, acc3), _ = _ffn_bwd(dh3, h2, a3, b3, f3, vec3, win3, wb["w_ffn2_out"], "ffn2_bwd")
    grads["w_ffn2_out"], _ = _wgrad(s3, df3, "wg_ffn2_out", FF // 2, 512, min(4096, T // 2))
    wgrad_cols("w_ffn2_in", u3, dab3, ["w_ffn2_out"])
    (dgates, do0, do1, do2, e0, e1, e2, dd, acc2a, accps,
     grads["w_out"], grads["w_pool_branch"], grads["w_attn_branch"], gwp), _ = _mix_bwd_a(
        dh2, vec2, mixout, merged, gates, ypool, yattn, dpool, os_r, lses_r, wp_bd, ps,
        wb["w_pool_branch"], wb["w_attn_branch"], wb["w_out"], ones_bd)
    n = len(POOL_WINDOWS)
    c = PW // n
    grad_w_pool = jnp.stack([gwp[j * c:(j + 1) * c, j * c:(j + 1) * c] for j in range(n)], axis=0)
    small3 = ["w_out", "w_pool_branch", "w_attn_branch"]
    dqs, dks, dvs = [], [], []
    for gi, (do, e) in enumerate(((do0, e0), (do1, e1), (do2, e2))):
        plan = ws.scatter_plan(small3, grads) if gi == 0 else None
        (dq, dk, dv), got = _attn_bwd(qs[gi], ks[gi], vs[gi], _flat(do), lses[gi], _flat(e), nbs[gi], f"attn_bwd{gi}", plan)
        if gi == 0:
            ws.scattered(small3, got)
        dqs.append(_by_residue(dq, DIL[gi]))
        dks.append(_by_residue(dk, DIL[gi]))
        dvs.append(_by_residue(dv, DIL[gi]))
    dh1, dproj, acc2b = _mix_bwd_b(dh2, h1, vec2, dd, dqs, dks, dvs, dgates, cos, sin, wb["w_in"])
    wgrad_cols("w_in", u2, dproj, [])
    (dx, dab1, s1, df1, acc1), _ = _ffn_bwd(dh1, x, a1, b1, f1, vec1, [wb["w_ffn1_in"]], wb["w_ffn1_out"], "ffn1_bwd")
    grads["w_ffn1_out"], _ = _wgrad(s1, df1, "wg_ffn1_out", FF // 2, 512, min(4096, T // 2))
    dmod = jnp.concatenate([acc1[0], acc1[1], acc1[3], acc2b[0], acc2b[1], acc2a[3], acc3[0], acc3[1], acc3[3]])
    dgains = jnp.stack([acc1[2], acc2b[2], acc3[2], lacc[1]], axis=0)
    row = None if pack is None else _SmallGatherPlan(pack(loss, dmod, dgains, grad_w_pool, accps[0]))
    rows = wgrad_cols("w_ffn1_in", u1, dab1, ["w_ffn1_out"], row)
    return loss, dx, dmod, dgains, grad_w_pool, accps[0], grads, None if rows is None else rows[0]


SMALL = ("b_ada", "g_norm_ffn1", "g_norm_mix", "g_norm_ffn2", "g_final", "pool_scale", "w_pool")
WEIGHTS = ("w_ada", "b_ada", "g_norm_ffn1", "w_ffn1_in", "w_ffn1_out", "g_norm_mix", "w_in", "w_pool", "pool_scale",
           "w_pool_branch", "w_attn_branch", "w_out", "g_norm_ffn2", "w_ffn2_in", "w_ffn2_out", "g_final")


def _pack_small(t):
    return jnp.concatenate([t[n].reshape(-1) for n in SMALL]).reshape(1, -1)


def kernel(x, c, positions, w_ada, b_ada, g_norm_ffn1, w_ffn1_in, w_ffn1_out, g_norm_mix, w_in, w_pool, pool_scale, w_pool_branch, w_attn_branch, w_out, g_norm_ffn2, w_ffn2_in, w_ffn2_out, g_final, loss_target, m_w_ada, m_b_ada, m_g_norm_ffn1, m_w_ffn1_in, m_w_ffn1_out, m_g_norm_mix, m_w_in, m_w_pool, m_pool_scale, m_w_pool_branch, m_w_attn_branch, m_w_out, m_g_norm_ffn2, m_w_ffn2_in, m_w_ffn2_out, m_g_final, v_w_ada, v_b_ada, v_g_norm_ffn1, v_w_ffn1_in, v_w_ffn1_out, v_g_norm_mix, v_w_in, v_w_pool, v_pool_scale, v_w_pool_branch, v_w_attn_branch, v_w_out, v_g_norm_ffn2, v_w_ffn2_in, v_w_ffn2_out, v_g_final):
    w = dict(w_ada=w_ada, b_ada=b_ada, g_norm_ffn1=g_norm_ffn1, w_ffn1_in=w_ffn1_in, w_ffn1_out=w_ffn1_out,
             g_norm_mix=g_norm_mix, w_in=w_in, w_pool=w_pool, pool_scale=pool_scale, w_pool_branch=w_pool_branch,
             w_attn_branch=w_attn_branch, w_out=w_out, g_norm_ffn2=g_norm_ffn2, w_ffn2_in=w_ffn2_in,
             w_ffn2_out=w_ffn2_out, g_final=g_final)
    mom = dict(w_ada=m_w_ada, b_ada=m_b_ada, g_norm_ffn1=m_g_norm_ffn1, w_ffn1_in=m_w_ffn1_in, w_ffn1_out=m_w_ffn1_out,
               g_norm_mix=m_g_norm_mix, w_in=m_w_in, w_pool=m_w_pool, pool_scale=m_pool_scale,
               w_pool_branch=m_w_pool_branch, w_attn_branch=m_w_attn_branch, w_out=m_w_out, g_norm_ffn2=m_g_norm_ffn2,
               w_ffn2_in=m_w_ffn2_in, w_ffn2_out=m_w_ffn2_out, g_final=m_g_final)
    var = dict(w_ada=v_w_ada, b_ada=v_b_ada, g_norm_ffn1=v_g_norm_ffn1, w_ffn1_in=v_w_ffn1_in, w_ffn1_out=v_w_ffn1_out,
               g_norm_mix=v_g_norm_mix, w_in=v_w_in, w_pool=v_w_pool, pool_scale=v_pool_scale,
               w_pool_branch=v_w_pool_branch, w_attn_branch=v_w_attn_branch, w_out=v_w_out, g_norm_ffn2=v_g_norm_ffn2,
               w_ffn2_in=v_w_ffn2_in, w_ffn2_out=v_w_ffn2_out, g_final=v_g_final)
    ix, iy, ic = _place()
    chip = 2 * ix + iy
    me = 4 * ix + 2 * iy + ic
    nada = w_ada.shape[2]

    shards = {n: w[n][0].astype(BF16) for n in BIG}
    half = D // 2
    shards["w_ffn2_in/0"], shards["w_ffn2_in/1"] = shards["w_ffn2_in"][:half], shards["w_ffn2_in"][half:]
    ws = _Sharded(shards)
    c_all = _run_plan(_SmallGatherPlan(c), "gather_c")[0][:, 0, :]
    b_shard = lax.dynamic_slice_in_dim(b_ada, chip * nada, nada, axis=1)
    mod_cols = _ada_fwd(c_all, w_ada[0], b_shard)
    first = _PlanGroup([_SmallGatherPlan(mod_cols), ws.gather_plan(["w_ffn1_in"])])
    (mod_all,), ffn1 = first.split(_run_plan(first, "gather_first"))
    ws.gathered(["w_ffn1_in"], ffn1)
    mod = jnp.concatenate([lax.dynamic_index_in_dim(mod_all[4 * (kk >> 1) + 2 * (kk & 1)], me, axis=0, keepdims=False)
                           for kk in range(4)])

    def pack(loss, dmod, dgains, g_w_pool, g_pool_scale):
        small_g = dict(b_ada=dmod, g_norm_ffn1=dgains[0], g_norm_mix=dgains[1], g_norm_ffn2=dgains[2],
                       g_final=dgains[3], pool_scale=g_pool_scale, w_pool=g_w_pool)
        return jnp.concatenate([_pack_small(small_g), jnp.pad(loss.reshape(1, 1), ((0, 0), (0, 127)))], axis=1)

    _, dx, _, _, _, _, _, gathered = _example_step(
        x[0], loss_target[0], positions[0], mod, (g_norm_ffn1[0], g_norm_mix[0], g_norm_ffn2[0], g_final),
        w_pool[0], pool_scale[0], ws, pack)

    per_weight, loss_tile = _adam_small(*[[t[n].reshape(1, -1) for n in SMALL] for t in (w, mom, var)], gathered)
    small_out = [{n: per_weight[j][kind].reshape(w[n].shape) for j, n in enumerate(SMALL)} for kind in range(4)]
    loss = loss_tile[0, 0]

    dmod_all = gathered[:, 0, :NMOD * D]
    dmod_cols = lax.dynamic_slice_in_dim(dmod_all, chip * nada, nada, axis=1)
    g_ada = _ada_bwd(c_all, dmod_cols)

    ada_out = _adam(w_ada[0], m_w_ada[0], v_w_ada[0], [g_ada], "adam_w_ada")[0]

    sums = {n: _sum4(ws.recv[n], "sum_" + n) for n in BIG}
    other = dict(zip(BIG, _run_plan(_SwapPlan([sums[n] for n in BIG]), "swap_sibling")))
    big_out = {}
    for n in BIG:
        if sums[n].shape[0] < w[n].shape[1]:
            big_out[n] = _adam_halves(w[n][0], mom[n][0], var[n][0], sums[n], other[n], "adam_" + n)
        else:
            big_out[n] = _adam(w[n][0], mom[n][0], var[n][0], [sums[n], other[n]], "adam_" + n)[0]

    def leaf(kind, n):
        if n == "w_ada":
            return ada_out[kind][None]
        if n in big_out:
            return big_out[n][kind][None]
        return small_out[kind][n]

    return (loss, dx[None], *[leaf(kind, n) for kind in range(4) for n in WEIGHTS])
```

```python
import jax
import jax.numpy as jnp
from jax import lax
from jax.experimental import pallas as pl
from jax.experimental.pallas import tpu as pltpu

F32 = jnp.float32
BF16 = jnp.bfloat16

D = 1024
FF = 2816
FC = FF
PW = 256
GA = 256
HD = 64
LANES = 128
NH = GA // HD
NG = 3
DIL = (1, 4, 16)
BLK = 128
FWD_BLOCKS = 8
BWD_BLOCKS = 8
GW = 2 * D
INW = PW + 3 * NG * GA + GW
NMOD = 9
POOL_WINDOWS = (2, 4, 8, 16)
HALO = 16
EPS = 1e-6
SCALE = HD ** -0.5
NEG = -1e30

LR, B1, B2, AEPS, WD, STEP = 0.001, 0.9, 0.999, 1e-08, 0.01, 10

VMEM_BIG = 56 * 1024 * 1024
TM = 256

MESH = pl.DeviceIdType.MESH
ANY = pl.BlockSpec(memory_space=pl.ANY)


def _call(body, name, grid, in_specs, out_specs, out_shape, scratch=(), vmem=None, comm=None):
    params = pltpu.CompilerParams(dimension_semantics=("arbitrary",) * len(grid), vmem_limit_bytes=vmem)
    n_in, n_out, n_scr = len(in_specs), len(out_shape), len(scratch)
    if comm is None:
        call = pl.pallas_call(body, name=name, grid=grid, in_specs=list(in_specs), out_specs=list(out_specs),
                              out_shape=list(out_shape), scratch_shapes=list(scratch), compiler_params=params)
        return lambda *args: (call(*args), ())
    nc = len(comm.inputs)

    def body_with_comm(*refs):
        ins, refs = refs[:n_in], refs[n_in:]
        c_ins, refs = refs[:nc], refs[nc:]
        outs, refs = refs[:n_out], refs[n_out:]
        c_outs, refs = refs[:nc], refs[nc:]
        scr, sems = refs[:n_scr], refs[n_scr:]
        first = pl.program_id(0) == 0
        last = pl.program_id(0) == grid[0] - 1
        for ax in range(1, len(grid)):
            first = jnp.logical_and(first, pl.program_id(ax) == 0)
            last = jnp.logical_and(last, pl.program_id(ax) == grid[ax] - 1)

        @pl.when(first)
        def _():
            comm.start(c_ins, c_outs, sems)

        body(*ins, *outs, *scr)
        early_relay = len(grid) == 1 and grid[0] >= 4
        if early_relay:
            @pl.when(pl.program_id(0) == (3 * grid[0]) // 4)
            def _():
                comm.relay(c_ins, c_outs, sems)

        @pl.when(last)
        def _():
            if not early_relay:
                comm.relay(c_ins, c_outs, sems)
            comm.wait(c_ins, c_outs, sems)

    call = pl.pallas_call(
        body_with_comm, name=name, grid=grid, in_specs=list(in_specs) + [ANY] * nc,
        out_specs=list(out_specs) + [ANY] * nc, out_shape=list(out_shape) + list(comm.out_shapes),
        scratch_shapes=list(scratch) + list(comm.sem_shapes), compiler_params=params)

    def run(*args):
        res = call(*args, *comm.inputs)
        return res[:n_out], res[n_out:]

    return run


def _rows(tm, n):
    return pl.BlockSpec((tm, n), lambda i: (i, 0))


def _const(shape):
    return pl.BlockSpec(shape, lambda i: (0,) * len(shape))


def _sds(shape, dtype):
    return jax.ShapeDtypeStruct(shape, dtype)


def _dot(a, b):
    return jnp.dot(a, b, preferred_element_type=F32)


def _dot_nt(a, b):
    return lax.dot_general(a, b, (((1,), (1,)), ((), ())), preferred_element_type=F32)


def _dot_tn(a, b):
    return lax.dot_general(a, b, (((0,), (0,)), ((), ())), preferred_element_type=F32)


def _colsum(v):
    return jnp.sum(v, axis=0, keepdims=True)


def _norm_fwd(h, g, sh, sc):
    r = lax.rsqrt(jnp.mean(h * h, axis=-1, keepdims=True) + EPS)
    xh = h * r
    n = xh * g
    return xh, r, n, n * (1.0 + sc) + sh


def _norm_bwd(du, xh, r, n, g, sc):
    dn = du * (1.0 + sc)
    dxh = dn * g
    dh = r * (dxh - xh * jnp.mean(dxh * xh, axis=-1, keepdims=True))
    return dh, _colsum(du), _colsum(du * n), _colsum(dn * xh)


def _load_once(pairs, sems):
    @pl.when(pl.program_id(0) == 0)
    def _():
        cps = [pltpu.make_async_copy(src, dst, sems.at[j]) for j, (src, dst) in enumerate(pairs)]
        for cp in cps:
            cp.start()
        for cp in cps:
            cp.wait()


def _zero_first(ref):
    @pl.when(pl.program_id(0) == 0)
    def _():
        ref[...] = jnp.zeros(ref.shape, ref.dtype)


def _row_chunks(hbm_refs, vmem_ref):
    pairs, row = [], 0
    for ref in hbm_refs:
        pairs.append((ref, vmem_ref.at[pl.ds(row, ref.shape[0]), :]))
        row += ref.shape[0]
    return pairs


def _loss_head(hh, tgt, g):
    r = lax.rsqrt(jnp.mean(hh * hh, axis=-1, keepdims=True) + EPS)
    xh = hh * r
    err = xh * g - tgt
    dy = err * (1.0 / D)
    dxh = dy * g
    dh = r * (dxh - xh * jnp.mean(dxh * xh, axis=-1, keepdims=True))
    return dh, _colsum(err * err), _colsum(dy * xh)


def _ffn_fwd(h, vec, wins, wout, name, comm=None, head=None):
    T = h.shape[0]
    nwin = len(wins)
    nhead = 0 if head is None else 2

    def body(h_ref, vec_ref, *rest):
        head_refs, rest = rest[:nhead], rest[nhead:]
        win_hbms, rest = rest[:nwin], rest[nwin:]
        (wout_hbm, ho_ref, u_ref, a_ref, b_ref, f_ref), rest = rest[:6], rest[6:]
        lacc_refs, (win_v, wout_v, sems) = rest[:nhead // 2], rest[nhead // 2:]
        _load_once(_row_chunks(win_hbms, win_v) + [(wout_hbm, wout_v)], sems)
        hh = h_ref[...]
        g, sh, sc, gt = vec_ref[0:1, :], vec_ref[1:2, :], vec_ref[2:3, :], vec_ref[3:4, :]
        _, _, _, u = _norm_fwd(hh, g, sh, sc)
        ub = u.astype(BF16)
        u_ref[...] = ub
        acc = None
        for j in range(FF // FC):
            lo, hi = j * FC, (j + 1) * FC
            a = _dot(ub, win_v[:, lo:hi])
            b = _dot(ub, win_v[:, FF + lo:FF + hi])
            a_ref[:, lo:hi] = a.astype(BF16)
            b_ref[:, lo:hi] = b.astype(BF16)
            s = (a * jax.nn.sigmoid(a) * b).astype(BF16)
            part = _dot(s, wout_v[lo:hi, :])
            acc = part if acc is None else acc + part
        f_ref[...] = acc.astype(BF16)
        ho = hh + 0.5 * gt * acc
        if head is None:
            ho_ref[...] = ho
        else:
            _zero_first(lacc_refs[0])
            dh, sq, dg = _loss_head(ho, head_refs[0][...], head_refs[1][0:1, :])
            ho_ref[...] = dh
            lacc_refs[0][0:1, :] += sq
            lacc_refs[0][1:2, :] += dg

    head_specs = [] if head is None else [_rows(TM, D), _const((8, D))]
    lacc_spec = [] if head is None else [_const((8, D))]
    lacc_shape = [] if head is None else [_sds((8, D), F32)]
    return _call(
        body, name, (T // TM,),
        [_rows(TM, D), _const((8, D))] + head_specs + [ANY] * (nwin + 1),
        [_rows(TM, D), _rows(TM, D), _rows(TM, FF), _rows(TM, FF), _rows(TM, D)] + lacc_spec,
        [_sds((T, D), F32), _sds((T, D), BF16), _sds((T, FF), BF16), _sds((T, FF), BF16), _sds((T, D), BF16)] + lacc_shape,
        scratch=[pltpu.VMEM((D, 2 * FF), BF16), pltpu.VMEM((FF, D), BF16), pltpu.SemaphoreType.DMA((nwin + 1,))],
        vmem=VMEM_BIG, comm=comm,
    )(h, vec, *([] if head is None else head), *wins, wout)


def _ffn_ab(h, vec, wins, name, comm=None):
    T = h.shape[0]
    nwin = len(wins)

    def body(h_ref, vec_ref, *rest):
        win_hbms, (u_ref, a_ref, b_ref, win_v, sems) = rest[:nwin], rest[nwin:]
        _load_once(_row_chunks(win_hbms, win_v), sems)
        g, sh, sc = vec_ref[0:1, :], vec_ref[1:2, :], vec_ref[2:3, :]
        _, _, _, u = _norm_fwd(h_ref[...], g, sh, sc)
        ub = u.astype(BF16)
        u_ref[...] = ub
        for j in range(FF // FC):
            lo, hi = j * FC, (j + 1) * FC
            a_ref[:, lo:hi] = _dot(ub, win_v[:, lo:hi]).astype(BF16)
            b_ref[:, lo:hi] = _dot(ub, win_v[:, FF + lo:FF + hi]).astype(BF16)

    return _call(
        body, name, (T // TM,),
        [_rows(TM, D), _const((8, D))] + [ANY] * nwin,
        [_rows(TM, D), _rows(TM, FF), _rows(TM, FF)],
        [_sds((T, D), BF16), _sds((T, FF), BF16), _sds((T, FF), BF16)],
        scratch=[pltpu.VMEM((D, 2 * FF), BF16), pltpu.SemaphoreType.DMA((nwin,))],
        vmem=VMEM_BIG, comm=comm,
    )(h, vec, *wins)


def _ffn_out(h, a, b, vec, wout, name, comm=None):
    T = h.shape[0]

    def body(h_ref, a_ref, b_ref, vec_ref, wout_hbm, ho_ref, f_ref, wout_v, sems):
        _load_once([(wout_hbm, wout_v)], sems)
        gt = vec_ref[3:4, :]
        acc = None
        for j in range(FF // FC):
            lo, hi = j * FC, (j + 1) * FC
            av = a_ref[:, lo:hi].astype(F32)
            s = (av * jax.nn.sigmoid(av) * b_ref[:, lo:hi].astype(F32)).astype(BF16)
            part = _dot(s, wout_v[lo:hi, :])
            acc = part if acc is None else acc + part
        f_ref[...] = acc.astype(BF16)
        ho_ref[...] = h_ref[...] + 0.5 * gt * acc

    return _call(
        body, name, (T // TM,),
        [_rows(TM, D), _rows(TM, FF), _rows(TM, FF), _const((8, D)), ANY],
        [_rows(TM, D), _rows(TM, D)],
        [_sds((T, D), F32), _sds((T, D), BF16)],
        scratch=[pltpu.VMEM((FF, D), BF16), pltpu.SemaphoreType.DMA((1,))],
        vmem=VMEM_BIG, comm=comm,
    )(h, a, b, vec, wout)


def _ffn_bwd(dh, h, a, b, f, vec, wins, wout, name, comm=None):
    T = h.shape[0]
    nwin = len(wins)

    def body(dh_ref, h_ref, a_ref, b_ref, f_ref, vec_ref, *rest):
        win_hbms, (wout_hbm, dhi_ref, dab_ref, s_ref, df_ref, acc_ref, win_v, wout_v, sems) = rest[:nwin], rest[nwin:]
        _load_once(_row_chunks(win_hbms, win_v) + [(wout_hbm, wout_v)], sems)
        _zero_first(acc_ref)
        g, sh, sc, gt = vec_ref[0:1, :], vec_ref[1:2, :], vec_ref[2:3, :], vec_ref[3:4, :]
        dho = dh_ref[...]
        df = (0.5 * gt * dho).astype(BF16)
        df_ref[...] = df
        dgt = _colsum(0.5 * dho * f_ref[...].astype(F32))
        du = None
        for j in range(FF // FC):
            lo, hi = j * FC, (j + 1) * FC
            av = a_ref[:, lo:hi].astype(F32)
            bv = b_ref[:, lo:hi].astype(F32)
            ds = _dot_nt(df, wout_v[lo:hi, :])
            sig = jax.nn.sigmoid(av)
            sa = av * sig
            s_ref[:, lo:hi] = (sa * bv).astype(BF16)
            da = (ds * bv * (sig * (1.0 + av * (1.0 - sig)))).astype(BF16)
            db = (ds * sa).astype(BF16)
            dab_ref[:, lo:hi] = da
            dab_ref[:, FF + lo:FF + hi] = db
            part = _dot_nt(da, win_v[:, lo:hi]) + _dot_nt(db, win_v[:, FF + lo:FF + hi])
            du = part if du is None else du + part
        xh, r, n, _ = _norm_fwd(h_ref[...], g, sh, sc)
        dhn, dsh, dsc, dg = _norm_bwd(du, xh, r, n, g, sc)
        dhi_ref[...] = dho + dhn
        acc_ref[0:1, :] += dsh
        acc_ref[1:2, :] += dsc
        acc_ref[2:3, :] += dg
        acc_ref[3:4, :] += dgt

    return _call(
        body, name, (T // TM,),
        [_rows(TM, D), _rows(TM, D), _rows(TM, FF), _rows(TM, FF), _rows(TM, D), _const((8, D))] + [ANY] * (nwin + 1),
        [_rows(TM, D), _rows(TM, 2 * FF), _rows(TM, FF), _rows(TM, D), _const((8, D))],
        [_sds((T, D), F32), _sds((T, 2 * FF), BF16), _sds((T, FF), BF16), _sds((T, D), BF16), _sds((8, D), F32)],
        scratch=[pltpu.VMEM((D, 2 * FF), BF16), pltpu.VMEM((FF, D), BF16), pltpu.SemaphoreType.DMA((nwin + 1,))],
        vmem=VMEM_BIG, comm=comm,
    )(dh, h, a, b, f, vec, *wins, wout)


def _wgrad(x, y, name, tk, tn, tt, out_dtype=BF16, comm=None):
    T, K = x.shape
    N = y.shape[1]
    nt = T // tt

    def body(x_ref, y_ref, o_ref, acc_ref):
        t = pl.program_id(2)
        part = _dot_tn(x_ref[...], y_ref[...])

        @pl.when(t == 0)
        def _():
            acc_ref[...] = part

        @pl.when(t > 0)
        def _():
            acc_ref[...] += part

        @pl.when(t == nt - 1)
        def _():
            o_ref[...] = acc_ref[...].astype(out_dtype)

    (out,), c_outs = _call(
        body, name, (K // tk, N // tn, nt),
        [pl.BlockSpec((tt, tk), lambda i, j, t: (t, i)), pl.BlockSpec((tt, tn), lambda i, j, t: (t, j))],
        [pl.BlockSpec((tk, tn), lambda i, j, t: (i, j))], [_sds((K, N), out_dtype)],
        scratch=[pltpu.VMEM((tk, tn), F32)], vmem=VMEM_BIG, comm=comm,
    )(x, y)
    return out, c_outs


def _wgrad_scatter(x, y, name, tt, comm=None):
    T, K = x.shape
    n = y.shape[1] // 4
    nt = T // tt
    assert nt >= 2, "a block's hand-over is added one grid step into the next block"
    half = K // 2
    nc = 0 if comm is None else len(comm.inputs)

    def body(chip_ref, x_ref, y_ref, *refs):
        c_ins, refs = refs[:nc], refs[nc:]
        recv_ref, refs = refs[0], refs[1:]
        c_outs, refs = refs[:nc], refs[nc:]
        acc_ref, keep_ref, give_ref, take_ref, local_sem, give_sems, take_sems, send_sems, recv_sems = refs[:9]
        j, t = pl.program_id(0), pl.program_id(1)
        px, py, pc = _place()

        def hand_over(jj):
            return pltpu.make_async_remote_copy(
                src_ref=give_ref.at[jj], dst_ref=take_ref.at[jj], send_sem=give_sems.at[jj], recv_sem=take_sems.at[jj],
                device_id=(px, py, 1 - pc), device_id_type=MESH)

        def send(jj):
            m = (3, 1, 2)[jj]
            return pltpu.make_async_remote_copy(
                src_ref=keep_ref.at[jj], dst_ref=recv_ref.at[m], send_sem=send_sems.at[jj], recv_sem=recv_sems.at[jj],
                device_id=_chip_peer(px, py, pc, m), device_id_type=MESH)

        def add_sibling(jj):
            hand_over(jj).wait_recv()
            keep_ref[jj] = (keep_ref[jj].astype(F32) + take_ref[jj].astype(F32)).astype(BF16)

        if comm is not None:
            @pl.when(jnp.logical_and(j == 0, t == 0))
            def _():
                comm.start(c_ins, c_outs, refs[9:])

        part = _dot_tn(x_ref[...], y_ref[...])

        @pl.when(t == 0)
        def _():
            acc_ref[...] = part

        @pl.when(t > 0)
        def _():
            acc_ref[...] += part

        for jj in range(3):
            @pl.when(jnp.logical_and(j == jj + 1, t == 0))
            def _():
                add_sibling(jj)
                send(jj).start()

        for jj in range(4):
            @pl.when(jnp.logical_and(j == jj, t == nt - 1))
            def _():
                keep_ref[jj] = acc_ref[pl.ds(pl.multiple_of(pc * half, 16), half), :].astype(BF16)
                give_ref[jj] = acc_ref[pl.ds(pl.multiple_of((1 - pc) * half, 16), half), :].astype(BF16)
                hand_over(jj).start()

        @pl.when(jnp.logical_and(j == 3, t == nt - 1))
        def _():
            add_sibling(3)
            own = pltpu.make_async_copy(keep_ref.at[3], recv_ref.at[0], local_sem.at[0])
            own.start()
            for jj in range(3):
                send(jj).wait_recv()
            for jj in range(3):
                send(jj).wait_send()
            for jj in range(4):
                hand_over(jj).wait_send()
            own.wait()
            if comm is not None:
                comm.relay(c_ins, c_outs, refs[9:])
                comm.wait(c_ins, c_outs, refs[9:])

    grid_spec = pltpu.PrefetchScalarGridSpec(
        num_scalar_prefetch=1, grid=(4, nt),
        in_specs=[pl.BlockSpec((tt, K), lambda j, t, chip: (t, 0)),
                  pl.BlockSpec((tt, n), lambda j, t, chip: (t, chip[0] ^ jnp.where(j == 0, 3, jnp.where(j == 3, 0, j))))]
        + [ANY] * nc,
        out_specs=[ANY] * (1 + nc),
        scratch_shapes=[pltpu.VMEM((K, n), F32)] + [pltpu.VMEM((4, half, n), BF16)] * 3
        + [pltpu.SemaphoreType.DMA((1,))] + [pltpu.SemaphoreType.DMA((4,))] * 2 + [pltpu.SemaphoreType.DMA((3,))] * 2
        + ([] if comm is None else list(comm.sem_shapes)))
    px, py, _ = _place()
    res = pl.pallas_call(
        body, name=name, grid_spec=grid_spec,
        out_shape=[_sds((4, half, n), BF16)] + ([] if comm is None else list(comm.out_shapes)),
        compiler_params=pltpu.CompilerParams(dimension_semantics=("arbitrary", "arbitrary"), vmem_limit_bytes=VMEM_BIG),
    )((2 * px + py).astype(jnp.int32).reshape(1), x, y, *([] if comm is None else comm.inputs))
    return res[0], res[1:]


def _swap_halves(t):
    w = t.shape[1]
    lane = lax.broadcasted_iota(jnp.int32, t.shape, 1)
    return jnp.where(lane % HD < HD // 2, pltpu.roll(t, w - HD // 2, 1), pltpu.roll(t, HD // 2, 1))


def _rope(t, cos, sin_signed):
    c = jnp.tile(cos, (1, t.shape[1] // cos.shape[1]))
    s = jnp.tile(sin_signed, (1, t.shape[1] // sin_signed.shape[1]))
    return t * c + _swap_halves(t) * s


def _rope_bwd(dt, cos, sin_signed):
    c = jnp.tile(cos, (1, dt.shape[1] // cos.shape[1]))
    s = jnp.tile(sin_signed, (1, dt.shape[1] // sin_signed.shape[1]))
    return dt * c + _swap_halves(dt * s)


def _rm_spec(dil):
    return pl.BlockSpec((dil, TM // dil, GA), lambda i: (0, i, 0))


def _to_residues(t, dst_ref, scr_ref, dil):
    if dil == 1:
        dst_ref[0] = t.astype(dst_ref.dtype)
        return
    for j in range(GA // LANES):
        scr_ref[j] = t[:, j * LANES:(j + 1) * LANES]
    for r in range(dil):
        for j in range(GA // LANES):
            rows = scr_ref.at[j][pl.ds(r, TM // dil, stride=dil), :]
            dst_ref[r, :, j * LANES:(j + 1) * LANES] = rows.astype(dst_ref.dtype)


def _from_residues(src_ref, scr_ref, dil):
    if dil == 1:
        return src_ref[0].astype(F32)
    for r in range(dil):
        for j in range(GA // LANES):
            scr_ref.at[j][pl.ds(r, TM // dil, stride=dil), :] = src_ref[r, :, j * LANES:(j + 1) * LANES].astype(F32)
    return jnp.concatenate([scr_ref[j] for j in range(GA // LANES)], axis=1)


def _mix_proj(h, vec, win, cos, sin, comm=None):
    T = h.shape[0]

    def body(h_ref, vec_ref, win_hbm, cos_ref, sin_ref, u_ref, p_ref, *rest):
        qkv_refs, gates_ref, win_v, scr_ref, sems = rest[:3 * NG], rest[3 * NG], rest[3 * NG + 1], rest[3 * NG + 2], rest[3 * NG + 3]
        _load_once([(win_hbm, win_v)], sems)
        g, sh, sc = vec_ref[0:1, :], vec_ref[1:2, :], vec_ref[2:3, :]
        _, _, _, u = _norm_fwd(h_ref[...], g, sh, sc)
        ub = u.astype(BF16)
        u_ref[...] = ub
        mixer_cols = PW + 3 * NG * GA
        proj = _dot(ub, win_v[:, 0:mixer_cols])
        p_ref[...] = proj[:, 0:PW]
        cos_t, sin_t = cos_ref[...], sin_ref[...]
        for j in range(3 * NG):
            col = PW + j * GA
            t = proj[:, col:col + GA]
            if j < 2 * NG:
                t = _rope(t, cos_t, sin_t)
            _to_residues(t, qkv_refs[j], scr_ref, DIL[j % NG])
        gates_ref[...] = jax.nn.sigmoid(_dot(ub, win_v[:, mixer_cols:INW])).astype(BF16)

    outs, c_outs = _call(
        body, "mix_proj", (T // TM,),
        [_rows(TM, D), _const((8, D)), ANY, _rows(TM, 128), _rows(TM, 128)],
        [_rows(TM, D), _rows(TM, PW)] + [_rm_spec(d) for d in DIL] * 3 + [_rows(TM, GW)],
        [_sds((T, D), BF16), _sds((T, PW), F32)] + [_sds((d, T // d, GA), BF16) for d in DIL] * 3 + [_sds((T, GW), BF16)],
        scratch=[pltpu.VMEM((D, INW), BF16), pltpu.VMEM((GA // LANES, TM, LANES), F32), pltpu.SemaphoreType.DMA((1,))],
        vmem=VMEM_BIG, comm=comm,
    )(h, vec, win, cos, sin)
    return (outs[0], outs[1], outs[2:2 + NG], outs[2 + NG:2 + 2 * NG], outs[2 + 2 * NG:2 + 3 * NG], outs[2 + 3 * NG]), c_outs


def _head_masks():
    lane_head = lax.broadcasted_iota(jnp.int32, (BLK, GA), 1) // HD
    return [lane_head == hd for hd in range(NH)]


def _expand_heads(t, hm):
    return jnp.concatenate([jnp.where(m, t, jnp.zeros_like(t)) for m in hm], axis=0)


def _collapse_heads(tb, hm):
    out = None
    for hd, m in enumerate(hm):
        part = jnp.where(m, tb[hd * BLK:(hd + 1) * BLK, :], 0.0)
        out = part if out is None else out + part
    return out


def _head_rows(t):
    return jnp.concatenate([t[:, hd * HD:hd * HD + 1] for hd in range(NH)], axis=0)


def _band(has_prev):
    a = lax.broadcasted_iota(jnp.int32, (NH * BLK, 2 * BLK), 0) & (BLK - 1)
    c = lax.broadcasted_iota(jnp.int32, (NH * BLK, 2 * BLK), 1)
    return jnp.logical_and(c >= jnp.where(has_prev, a, BLK), c <= a + BLK)


def _attn_fwd(q, k, v, nb, name, comm=None):
    T = q.shape[0]
    nbt = T // BLK

    def block(qv, kcat, vcat, has_prev, hm):
        s = jnp.where(_band(has_prev), _dot_nt(_expand_heads(qv, hm), kcat) * SCALE, NEG)
        mx = jnp.max(s, axis=-1, keepdims=True)
        e = jnp.exp(s - mx)
        l = jnp.sum(e, axis=-1, keepdims=True)
        ob = _dot((e * (1.0 / l)).astype(BF16), vcat)
        return _collapse_heads(ob, hm), _collapse_heads(jnp.broadcast_to(mx + jnp.log(l), (NH * BLK, GA)), hm)

    def body(q_ref, k_ref, kp_ref, v_ref, vp_ref, o_ref, lse_ref):
        b0 = FWD_BLOCKS * pl.program_id(0)
        hm = _head_masks()
        for b in range(FWD_BLOCKS):
            rows = slice(b * BLK, (b + 1) * BLK)
            if b == 0:
                kcat = jnp.concatenate([kp_ref[...], k_ref[rows, :]], axis=0)
                vcat = jnp.concatenate([vp_ref[...], v_ref[rows, :]], axis=0)
            else:
                kcat, vcat = k_ref[(b - 1) * BLK:(b + 1) * BLK, :], v_ref[(b - 1) * BLK:(b + 1) * BLK, :]
            o_ref[rows, :], lse_ref[rows, :] = block(q_ref[rows, :], kcat, vcat, ((b0 + b) & (nb - 1)) != 0, hm)

    cur = pl.BlockSpec((FWD_BLOCKS * BLK, GA), lambda i: (i, 0))
    prev = pl.BlockSpec((BLK, GA), lambda i: (jnp.maximum(FWD_BLOCKS * i - 1, 0), 0))
    return _call(body, name, (nbt // FWD_BLOCKS,), [cur, cur, prev, cur, prev], [cur, cur],
                 [_sds((T, GA), F32), _sds((T, GA), F32)], comm=comm)(q, k, k, v, v)


def _attn_bwd(q, k, v, do, lse, e, nb, name, comm=None):
    T = q.shape[0]
    nbt = T // BLK

    nblk = BWD_BLOCKS

    def probs_and_ds(qb, dob, kcat, vcat, lsev, ev, valid):
        p = jnp.where(valid, jnp.exp(_dot_nt(qb, kcat) * SCALE - _head_rows(lsev)), 0.0)
        return p.astype(BF16), (p * (_dot_nt(dob, vcat) + _head_rows(ev))).astype(BF16)

    def body(q_ref, k_ref, v_ref, do_ref, lse_ref, e_ref, kp_ref, vp_ref, qn_ref, don_ref, lsen_ref, en_ref,
             dq_ref, dk_ref, dv_ref):
        b0 = nblk * pl.program_id(0)
        hm = _head_masks()
        rows = [slice(b * BLK, (b + 1) * BLK) for b in range(nblk)]
        qs = [_expand_heads(q_ref[r, :], hm) for r in rows] + [_expand_heads(qn_ref[...], hm)]
        dos = [_expand_heads(do_ref[r, :], hm) for r in rows] + [_expand_heads(don_ref[...], hm)]
        ps, dss = [], []
        for b, r in enumerate(rows):
            if b == 0:
                kcat = jnp.concatenate([kp_ref[...], k_ref[r, :]], axis=0)
                vcat = jnp.concatenate([vp_ref[...], v_ref[r, :]], axis=0)
            else:
                kcat, vcat = k_ref[(b - 1) * BLK:(b + 1) * BLK, :], v_ref[(b - 1) * BLK:(b + 1) * BLK, :]
            p, ds = probs_and_ds(qs[b], dos[b], kcat, vcat, lse_ref[r, :], e_ref[r, :], _band(((b0 + b) & (nb - 1)) != 0))
            dq_ref[r, :] = _collapse_heads(_dot(ds, kcat) * SCALE, hm)
            ps.append(p)
            dss.append(ds)
        a = lax.broadcasted_iota(jnp.int32, (NH * BLK, BLK), 0) & (BLK - 1)
        c = lax.broadcasted_iota(jnp.int32, (NH * BLK, BLK), 1)
        valid_n = jnp.logical_and(c >= a, ((b0 + nblk) & (nb - 1)) != 0)
        p_n, ds_n = probs_and_ds(qs[nblk], dos[nblk], k_ref[rows[-1], :], v_ref[rows[-1], :], lsen_ref[...], en_ref[...], valid_n)
        for b, r in enumerate(rows):
            ds_after = dss[b + 1][:, :BLK] if b + 1 < nblk else ds_n
            p_after = ps[b + 1][:, :BLK] if b + 1 < nblk else p_n
            q_pair = jnp.concatenate([qs[b], qs[b + 1]], axis=0)
            do_pair = jnp.concatenate([dos[b], dos[b + 1]], axis=0)
            dk_ref[r, :] = _dot_tn(jnp.concatenate([dss[b][:, BLK:], ds_after], axis=0), q_pair) * SCALE
            dv_ref[r, :] = _dot_tn(jnp.concatenate([ps[b][:, BLK:], p_after], axis=0), do_pair).astype(BF16)

    cur = pl.BlockSpec((nblk * BLK, GA), lambda i: (i, 0))
    prev = pl.BlockSpec((BLK, GA), lambda i: (jnp.maximum(nblk * i - 1, 0), 0))
    nxt = pl.BlockSpec((BLK, GA), lambda i: (jnp.minimum(nblk * i + nblk, nbt - 1), 0))
    return _call(body, name, (nbt // nblk,), [cur] * 6 + [prev, prev] + [nxt] * 4, [cur, cur, cur],
                 [_sds((T, GA), F32), _sds((T, GA), F32), _sds((T, GA), BF16)],
                 comm=comm)(q, k, v, do, lse, e, k, v, q, do, lse, e)


def _flat(t):
    return t.reshape(t.shape[0] * t.shape[1], t.shape[2])


def _by_residue(t, dil):
    return t.reshape(dil, t.shape[0] // dil, t.shape[1])


def _pool_consts(shape, row0):
    lane = lax.broadcasted_iota(jnp.int32, shape, 1)
    t = lax.broadcasted_iota(jnp.int32, shape, 0) + row0
    grp = lane // (PW // len(POOL_WINDOWS))
    win = jnp.where(grp == 0, POOL_WINDOWS[0], jnp.where(grp == 1, POOL_WINDOWS[1],
                    jnp.where(grp == 2, POOL_WINDOWS[2], POOL_WINDOWS[3])))
    cnt = jnp.minimum(t + 1, win).astype(F32)
    return grp, cnt


def _window_sums(ext_ref, base, step, tm):
    outs, run = [], None
    for j in range(POOL_WINDOWS[-1]):
        sl = ext_ref[pl.ds(base + step * j, tm), :]
        run = sl if run is None else run + sl
        if j + 1 in POOL_WINDOWS:
            outs.append(run)
    return outs


def _select_group(grp, vals):
    return jnp.where(grp == 0, vals[0], jnp.where(grp == 1, vals[1], jnp.where(grp == 2, vals[2], vals[3])))


def _pool_d(pc_ref, pp_ref, ext_ref, i, tm):
    ext_ref[0:HALO, :] = jnp.where(i > 0, pp_ref[tm - HALO:tm, :], 0.0)
    ext_ref[HALO:HALO + tm, :] = pc_ref[...]
    grp, cnt = _pool_consts((tm, PW), i * tm)
    sums = _window_sums(ext_ref, HALO, -1, tm)
    return _select_group(grp, sums) / cnt - pc_ref[...]


def _group_weights(ls):
    mx = jnp.maximum(jnp.maximum(ls[0], ls[1]), ls[2])
    es = [jnp.exp(l - mx) for l in ls]
    inv = 1.0 / (es[0] + es[1] + es[2])
    return [e * inv for e in es]


def _mix_merge(h, vec, p, os, lses, gates, wp_bd, pscale, wpb, wab, wout):
    T = h.shape[0]

    def body(h_ref, vec_ref, pc_ref, pp_ref, o0, o1, o2, l0, l1, l2, gates_ref, wp_ref, ps_ref, wpb_ref, wab_ref, wout_ref,
             ho_ref, yp_ref, ya_ref, mg_ref, mo_ref, d_ref, ext_ref, scr_ref):
        i = pl.program_id(0)
        gt = vec_ref[3:4, :]
        d = _pool_d(pc_ref, pp_ref, ext_ref, i, TM).astype(BF16)
        d_ref[...] = d
        ypool = (_dot(d, wp_ref[...]) * ps_ref[0:1, :]).astype(BF16)
        yp_ref[...] = ypool
        w = _group_weights([_from_residues(r, scr_ref, dl) for r, dl in zip((l0, l1, l2), DIL)])
        yattn = None
        for wg, o_ref, dl in zip(w, (o0, o1, o2), DIL):
            part = wg * _from_residues(o_ref, scr_ref, dl)
            yattn = part if yattn is None else yattn + part
        yattn = yattn.astype(BF16)
        ya_ref[...] = yattn
        merged = (gates_ref[:, 0:D].astype(F32) * _dot(ypool, wpb_ref[...])
                  + gates_ref[:, D:GW].astype(F32) * _dot(yattn, wab_ref[...])).astype(BF16)
        mg_ref[...] = merged
        mo = _dot(merged, wout_ref[...])
        mo_ref[...] = mo.astype(BF16)
        ho_ref[...] = h_ref[...] + gt * mo

    prev = pl.BlockSpec((TM, PW), lambda i: (jnp.maximum(i - 1, 0), 0))
    return _call(
        body, "mix_merge", (T // TM,),
        [_rows(TM, D), _const((8, D)), _rows(TM, PW), prev] + [_rm_spec(dl) for dl in DIL] * 2 + [_rows(TM, GW), _const((PW, PW)),
         _const((8, PW)), _const((PW, D)), _const((GA, D)), _const((D, D))],
        [_rows(TM, D), _rows(TM, PW), _rows(TM, GA), _rows(TM, D), _rows(TM, D), _rows(TM, PW)],
        [_sds((T, D), F32), _sds((T, PW), BF16), _sds((T, GA), BF16), _sds((T, D), BF16), _sds((T, D), BF16), _sds((T, PW), BF16)],
        scratch=[pltpu.VMEM((TM + HALO, PW), F32), pltpu.VMEM((GA // LANES, TM, LANES), F32)],
        vmem=VMEM_BIG,
    )(h, vec, p, p, *os, *lses, gates, wp_bd, pscale, wpb, wab, wout)[0]


def _mix_bwd_a(dh, vec, mixout, merged, gates, ypool, yattn, dpool, os, lses, wp_bd, pscale, wpb, wab, wout, ones_bd,
               comm=None):
    T = dh.shape[0]
    nt = T // TM

    def body(dh_ref, vec_ref, mo_ref, mg_ref, gates_ref, yp_ref, ya_ref, d_ref, o0, o1, o2, l0, l1, l2,
             wp_ref, ps_ref, wpb_ref, wab_ref, wout_ref, ones_ref,
             dgates_ref, do0, do1, do2, e0, e1, e2, dd_ref, acc_ref, acc2_ref, g_out_ref, g_pb_ref, g_ab_ref, g_pool_ref,
             scr_ref, a_out, a_pb, a_ab, a_pool):
        _zero_first(acc_ref)
        _zero_first(acc2_ref)
        for a_ref in (a_out, a_pb, a_ab, a_pool):
            _zero_first(a_ref)
        gt = vec_ref[3:4, :]
        dho = dh_ref[...]
        acc_ref[3:4, :] += _colsum(dho * mo_ref[...].astype(F32))
        dmo = (gt * dho).astype(BF16)
        a_out[...] += _dot_tn(mg_ref[...], dmo)
        dmerged = _dot_nt(dmo, wout_ref[...])
        gp = gates_ref[:, 0:D].astype(F32)
        ga = gates_ref[:, D:GW].astype(F32)
        bp = _dot(yp_ref[...], wpb_ref[...])
        ba = _dot(ya_ref[...], wab_ref[...])
        dgates_ref[:, 0:D] = (dmerged * bp * gp * (1.0 - gp)).astype(BF16)
        dgates_ref[:, D:GW] = (dmerged * ba * ga * (1.0 - ga)).astype(BF16)
        dbp = (dmerged * gp).astype(BF16)
        dba = (dmerged * ga).astype(BF16)
        a_pb[...] += _dot_tn(yp_ref[...], dbp)
        a_ab[...] += _dot_tn(ya_ref[...], dba)
        dypool = _dot_nt(dbp, wpb_ref[...])
        ypre = _dot(d_ref[...], wp_ref[...])
        acc2_ref[0:1, :] += _colsum(dypool * ypre)
        dyp = (dypool * ps_ref[0:1, :]).astype(BF16)
        a_pool[...] += _dot_tn(d_ref[...], dyp)
        dd_ref[...] = _dot_nt(dyp, wp_ref[...])
        dya = _dot_nt(dba, wab_ref[...])
        w = _group_weights([_from_residues(r, scr_ref, dl) for r, dl in zip((l0, l1, l2), DIL)])
        ya = None
        for wg, o_ref, dl in zip(w, (o0, o1, o2), DIL):
            part = wg * _from_residues(o_ref, scr_ref, dl)
            ya = part if ya is None else ya + part
        prod = dya * ya
        hi = prod.astype(BF16)
        lo = (prod - hi.astype(F32)).astype(BF16)
        tot = _dot(hi, ones_ref[...]) + _dot(lo, ones_ref[...])
        for wg, do_ref, e_ref, dl in zip(w, (do0, do1, do2), (e0, e1, e2), DIL):
            _to_residues(wg * dya, do_ref, scr_ref, dl)
            _to_residues(-wg * tot, e_ref, scr_ref, dl)

        @pl.when(pl.program_id(0) == nt - 1)
        def _():
            g_out_ref[...] = a_out[...].astype(BF16)
            g_pb_ref[...] = a_pb[...].astype(BF16)
            g_ab_ref[...] = a_ab[...].astype(BF16)
            g_pool_ref[...] = a_pool[...]

    return _call(
        body, "mix_bwd_a", (nt,),
        [_rows(TM, D), _const((8, D)), _rows(TM, D), _rows(TM, D), _rows(TM, GW), _rows(TM, PW), _rows(TM, GA), _rows(TM, PW)]
        + [_rm_spec(dl) for dl in DIL] * 2
        + [_const((PW, PW)), _const((8, PW)), _const((PW, D)), _const((GA, D)), _const((D, D)), _const((GA, GA))],
        [_rows(TM, GW)] + [_rm_spec(dl) for dl in DIL] * 2 + [_rows(TM, PW), _const((8, D)), _const((8, PW))]
        + [_const((D, D)), _const((PW, D)), _const((GA, D)), _const((PW, PW))],
        [_sds((T, GW), BF16)] + [_sds((dl, T // dl, GA), BF16) for dl in DIL]
        + [_sds((dl, T // dl, GA), F32) for dl in DIL] + [_sds((T, PW), F32), _sds((8, D), F32), _sds((8, PW), F32)]
        + [_sds((D, D), BF16), _sds((PW, D), BF16), _sds((GA, D), BF16), _sds((PW, PW), F32)],
        scratch=[pltpu.VMEM((GA // LANES, TM, LANES), F32), pltpu.VMEM((D, D), F32), pltpu.VMEM((PW, D), F32),
                 pltpu.VMEM((GA, D), F32), pltpu.VMEM((PW, PW), F32)],
        vmem=VMEM_BIG, comm=comm,
    )(dh, vec, mixout, merged, gates, ypool, yattn, dpool, *os, *lses, wp_bd, pscale, wpb, wab, wout, ones_bd)


def _mix_bwd_b(dh, h, vec, dd, dqs, dks, dvs, dgates, cos, sin, win):
    T = h.shape[0]
    nt = T // TM

    def body(dh_ref, h_ref, vec_ref, ddc_ref, ddn_ref, *rest):
        qk_refs, dv_refs = rest[:2 * NG], rest[2 * NG:3 * NG]
        dgates_ref, cos_ref, sin_ref, win_hbm, dhi_ref, dproj_ref, acc_ref, win_v, ext_ref, scr_ref, sems = rest[3 * NG:]
        i = pl.program_id(0)
        _load_once([(win_hbm, win_v)], sems)
        _zero_first(acc_ref)
        g, sh, sc = vec_ref[0:1, :], vec_ref[1:2, :], vec_ref[2:3, :]
        grp, cnt = _pool_consts((TM, PW), i * TM)
        _, cnt_n = _pool_consts((HALO, PW), (i + 1) * TM)
        ext_ref[0:TM, :] = ddc_ref[...] / cnt
        ext_ref[TM:TM + HALO, :] = jnp.where(i < nt - 1, ddn_ref[0:HALO, :] / cnt_n, 0.0)
        dp = _select_group(grp, _window_sums(ext_ref, 0, 1, TM)) - ddc_ref[...]
        dproj_ref[:, 0:PW] = dp.astype(BF16)
        cos_t, sin_t = cos_ref[...], sin_ref[...]
        for j in range(2 * NG):
            col = PW + j * GA
            dt = _from_residues(qk_refs[j], scr_ref, DIL[j % NG])
            dproj_ref[:, col:col + GA] = _rope_bwd(dt, cos_t, sin_t).astype(BF16)
        for j in range(NG):
            col = PW + (2 * NG + j) * GA
            dproj_ref[:, col:col + GA] = _from_residues(dv_refs[j], scr_ref, DIL[j]).astype(BF16)
        dproj_ref[:, PW + 3 * NG * GA:INW] = dgates_ref[...]
        du = None
        for j in range(INW // 512):
            part = _dot_nt(dproj_ref[:, j * 512:(j + 1) * 512], win_v[:, j * 512:(j + 1) * 512])
            du = part if du is None else du + part
        xh, r, n, _ = _norm_fwd(h_ref[...], g, sh, sc)
        dhn, dsh, dsc, dg = _norm_bwd(du, xh, r, n, g, sc)
        dhi_ref[...] = dh_ref[...] + dhn
        acc_ref[0:1, :] += dsh
        acc_ref[1:2, :] += dsc
        acc_ref[2:3, :] += dg

    nxt = pl.BlockSpec((TM, PW), lambda i: (jnp.minimum(i + 1, nt - 1), 0))
    return _call(
        body, "mix_bwd_b", (nt,),
        [_rows(TM, D), _rows(TM, D), _const((8, D)), _rows(TM, PW), nxt] + [_rm_spec(dl) for dl in DIL] * 3
        + [_rows(TM, GW), _rows(TM, 128), _rows(TM, 128), ANY],
        [_rows(TM, D), _rows(TM, INW), _const((8, D))],
        [_sds((T, D), F32), _sds((T, INW), BF16), _sds((8, D), F32)],
        scratch=[pltpu.VMEM((D, INW), BF16), pltpu.VMEM((TM + HALO, PW), F32), pltpu.VMEM((GA // LANES, TM, LANES), F32),
                 pltpu.SemaphoreType.DMA((1,))],
        vmem=VMEM_BIG,
    )(dh, h, vec, dd, dd, *dqs, *dks, *dvs, dgates, cos, sin, win)[0]


def _ada_fwd(c_all, w_shard, b_shard):
    n = w_shard.shape[1]

    def body(c_ref, w_ref, b_ref, o_ref):
        cv = c_ref[...]
        cond = (cv * jax.nn.sigmoid(cv)).astype(BF16)
        o_ref[...] = _dot(cond, w_ref[...].astype(BF16)) + b_ref[...]

    tn = n // 3
    return pl.pallas_call(
        body, name="ada_fwd", grid=(3,),
        in_specs=[pl.BlockSpec((8, D), lambda j: (0, 0)), pl.BlockSpec((D, tn), lambda j: (0, j)), pl.BlockSpec((1, tn), lambda j: (0, j))],
        out_specs=pl.BlockSpec((8, tn), lambda j: (0, j)), out_shape=_sds((8, n), F32),
        compiler_params=pltpu.CompilerParams(dimension_semantics=("arbitrary",)),
    )(c_all, w_shard, b_shard)


def _ada_bwd(c_all, dmod_shard):
    n = dmod_shard.shape[1]

    def body(c_ref, d_ref, o_ref):
        cv = c_ref[...]
        cond = (cv * jax.nn.sigmoid(cv)).astype(BF16)
        o_ref[...] = _dot_tn(cond, d_ref[...].astype(BF16))

    tn = n // 3
    return pl.pallas_call(
        body, name="ada_bwd", grid=(3,),
        in_specs=[pl.BlockSpec((8, D), lambda j: (0, 0)), pl.BlockSpec((8, tn), lambda j: (0, j))],
        out_specs=pl.BlockSpec((D, tn), lambda j: (0, j)), out_shape=_sds((D, n), F32),
        compiler_params=pltpu.CompilerParams(dimension_semantics=("arbitrary",)),
    )(c_all, dmod_shard)


def _adam_math(w, g, m, v):
    m2 = B1 * m + (1.0 - B1) * g
    v2 = B2 * v + (1.0 - B2) * (g * g)
    m_hat = m2 / (1.0 - B1 ** STEP)
    v_hat = v2 / (1.0 - B2 ** STEP)
    delta = -LR * (m_hat / (jnp.sqrt(v_hat) + AEPS) + WD * w)
    return delta, m2, v2


def _adam(w, m, v, parts, name, comm=None):
    R, C = w.shape
    tr = R
    for cand in (128, 64, 32, 16, 8):
        if R % cand == 0:
            tr = cand
            break
    np_ = len(parts)

    def body(w_ref, m_ref, v_ref, *rest):
        p_refs, (g_ref, d_ref, m2_ref, v2_ref) = rest[:np_], rest[np_:]
        g = p_refs[0][...]
        for pr in p_refs[1:]:
            g = g + pr[...]
        delta, m2, v2 = _adam_math(w_ref[...], g, m_ref[...], v_ref[...])
        g_ref[...] = g
        d_ref[...] = delta
        m2_ref[...] = m2
        v2_ref[...] = v2

    spec = pl.BlockSpec((tr, C), lambda i: (i, 0))
    return _call(body, name, (R // tr,), [spec] * (3 + np_), [spec] * 4, [_sds((R, C), F32)] * 4,
                 vmem=VMEM_BIG, comm=comm)(w, m, v, *parts)


def _adam_halves(w, m, v, mine, other, name):
    R, C = w.shape
    tr = 128
    nh = R // 2 // tr

    def body(c_ref, w_ref, m_ref, v_ref, mine_ref, other_ref, g_ref, d_ref, m2_ref, v2_ref):
        i = pl.program_id(0)
        in_mine = jnp.logical_and(i >= c_ref[0] * nh, i < (c_ref[0] + 1) * nh)
        g = jnp.where(in_mine, mine_ref[...], other_ref[...])
        delta, m2, v2 = _adam_math(w_ref[...], g, m_ref[...], v_ref[...])
        g_ref[...] = g
        d_ref[...] = delta
        m2_ref[...] = m2
        v2_ref[...] = v2

    spec = pl.BlockSpec((tr, C), lambda i, c: (i, 0))
    grid_spec = pltpu.PrefetchScalarGridSpec(
        num_scalar_prefetch=1, grid=(R // tr,),
        in_specs=[spec] * 3 + [pl.BlockSpec((tr, C), lambda i, c: (jnp.clip(i - c[0] * nh, 0, nh - 1), 0)),
                               pl.BlockSpec((tr, C), lambda i, c: (jnp.clip(i - (1 - c[0]) * nh, 0, nh - 1), 0))],
        out_specs=[spec] * 4)
    return pl.pallas_call(
        body, name=name, grid_spec=grid_spec, out_shape=[_sds((R, C), F32)] * 4,
        compiler_params=pltpu.CompilerParams(dimension_semantics=("arbitrary",), vmem_limit_bytes=VMEM_BIG),
    )(lax.axis_index("c").astype(jnp.int32).reshape(1), w, m, v, mine, other)


def _adam_small(ws, ms, vs, gathered):
    n = len(ws)
    sizes = [a.shape[1] for a in ws]

    def total(ga_ref, off, size):
        g = ga_ref[0, :, off:off + size]
        for dev in range(1, 8):
            g = g + ga_ref[dev, :, off:off + size]
        return g

    def body(*refs):
        w_refs, m_refs, v_refs, ga_ref, outs = refs[:n], refs[n:2 * n], refs[2 * n:3 * n], refs[3 * n], refs[3 * n + 1:]
        off = 0
        for j, size in enumerate(sizes):
            g = total(ga_ref, off, size)
            delta, m2, v2 = _adam_math(w_refs[j][...], g, m_refs[j][...], v_refs[j][...])
            for ref, val in zip(outs[4 * j:4 * j + 4], (g, delta, m2, v2)):
                ref[...] = val
            off += size
        outs[4 * n][...] = total(ga_ref, off, 128)

    res = pl.pallas_call(
        body, name="adam_small",
        out_shape=[_sds((1, size), F32) for size in sizes for _ in range(4)] + [_sds((1, 128), F32)],
    )(*ws, *ms, *vs, gathered)
    return [res[4 * j:4 * j + 4] for j in range(n)], res[4 * n]


def _sum4(blocks, name):
    _, R, C = blocks.shape
    tr = R
    for cand in (256, 128, 64, 32, 16):
        if R % cand == 0:
            tr = cand
            break

    def body(r_ref, out_ref):
        out_ref[...] = ((r_ref[0].astype(F32) + r_ref[1].astype(F32)) + r_ref[2].astype(F32)) + r_ref[3].astype(F32)

    return pl.pallas_call(
        body, name=name, grid=(R // tr,),
        in_specs=[pl.BlockSpec((4, tr, C), lambda i: (0, i, 0))],
        out_specs=pl.BlockSpec((tr, C), lambda i: (i, 0)), out_shape=_sds((R, C), F32),
        compiler_params=pltpu.CompilerParams(dimension_semantics=("arbitrary",)),
    )(blocks)


def _place():
    return lax.axis_index("x"), lax.axis_index("y"), lax.axis_index("c")


def _chip_peer(x, y, c, m):
    return (x ^ (m >> 1), y ^ (m & 1), c)


def _shard_ref(ref, axis, k, n):
    start = pl.multiple_of(k * n, 128 if axis == 1 else 16)
    return ref.at[:, pl.ds(start, n)] if axis == 1 else ref.at[pl.ds(start, n), :]


def _half_rows(ref, axis, k, n, hc):
    if axis == 1:
        half = ref.shape[0] // 2
        return ref.at[pl.ds(pl.multiple_of(hc * half, 16), half), pl.ds(pl.multiple_of(k * n, 128), n)]
    half = n // 2
    return ref.at[pl.ds(pl.multiple_of(k * n + hc * half, 16), half), :]


class _GatherPlan:
    def __init__(self, shards, axes):
        self.inputs, self.axes, nw = list(shards), list(axes), len(shards)
        self.out_shapes = [_sds((s.shape[0] * (4 if ax == 0 else 1), s.shape[1] * (4 if ax == 1 else 1)), BF16)
                           for s, ax in zip(shards, axes)]
        self.sem_shapes = [pltpu.SemaphoreType.DMA((nw,))] + [pltpu.SemaphoreType.DMA((nw, 3))] * 4

    def _copies(self, ins, outs, sems):
        local_sems, send_sems, recv_sems, pass_sems, got_sems = sems
        x, y, c = _place()
        k = 2 * x + y
        local, sends, arrivals, passes, handed = [], [], [], [], []
        for j, ax in enumerate(self.axes):
            n = ins[j].shape[ax]
            half = ins[j].shape[0] // 2
            local.append(pltpu.make_async_copy(ins[j], _shard_ref(outs[j], ax, k, n), local_sems.at[j]))
            my_half = ins[j].at[pl.ds(pl.multiple_of(c * half, 16), half), :]
            for m in range(1, 4):
                sends.append(pltpu.make_async_remote_copy(
                    src_ref=my_half, dst_ref=_half_rows(outs[j], ax, k, n, c), send_sem=send_sems.at[j, m - 1],
                    recv_sem=recv_sems.at[j, m - 1], device_id=_chip_peer(x, y, c, m), device_id_type=MESH))
                theirs = _half_rows(outs[j], ax, k ^ m, n, c)
                arrivals.append(pltpu.make_async_remote_copy(
                    src_ref=my_half, dst_ref=theirs, send_sem=send_sems.at[j, m - 1], recv_sem=recv_sems.at[j, m - 1],
                    device_id=(x, y, c), device_id_type=MESH))
                passes.append(pltpu.make_async_remote_copy(
                    src_ref=theirs, dst_ref=theirs, send_sem=pass_sems.at[j, m - 1], recv_sem=got_sems.at[j, m - 1],
                    device_id=(x, y, 1 - c), device_id_type=MESH))
                other = _half_rows(outs[j], ax, k ^ m, n, 1 - c)
                handed.append(pltpu.make_async_remote_copy(
                    src_ref=other, dst_ref=other, send_sem=pass_sems.at[j, m - 1], recv_sem=got_sems.at[j, m - 1],
                    device_id=(x, y, c), device_id_type=MESH))
        return local, sends, arrivals, passes, handed

    def start(self, ins, outs, sems):
        local, sends, _, _, _ = self._copies(ins, outs, sems)
        for cp in local + sends:
            cp.start()

    def relay(self, ins, outs, sems):
        _, _, arrivals, passes, _ = self._copies(ins, outs, sems)
        for arrived, onward in zip(arrivals, passes):
            arrived.wait_recv()
            onward.start()

    def wait(self, ins, outs, sems):
        local, sends, _, passes, handed = self._copies(ins, outs, sems)
        for cp in handed:
            cp.wait_recv()
        for cp in sends + passes:
            cp.wait_send()
        for cp in local:
            cp.wait()


class _ScatterPlan:
    def __init__(self, grads, axes):
        self.inputs, self.axes, nw = list(grads), list(axes), len(grads)
        self.shard_shapes = [(g.shape[0] // (4 if ax == 0 else 1), g.shape[1] // (4 if ax == 1 else 1))
                             for g, ax in zip(grads, axes)]
        self.out_shapes = [_sds((4,) + s, BF16) for s in self.shard_shapes]
        self.sem_shapes = [pltpu.SemaphoreType.DMA((nw,)), pltpu.SemaphoreType.DMA((nw, 3)), pltpu.SemaphoreType.DMA((nw, 3))]

    def _copies(self, ins, outs, sems):
        local_sems, send_sems, recv_sems = sems
        x, y, c = _place()
        k = 2 * x + y
        local, remote, arrivals = [], [], []
        for j, ax in enumerate(self.axes):
            n = self.shard_shapes[j][ax]
            local.append(pltpu.make_async_copy(_shard_ref(ins[j], ax, k, n), outs[j].at[0], local_sems.at[j]))
            for m in range(1, 4):
                remote.append(pltpu.make_async_remote_copy(
                    src_ref=_shard_ref(ins[j], ax, k ^ m, n), dst_ref=outs[j].at[m],
                    send_sem=send_sems.at[j, m - 1], recv_sem=recv_sems.at[j, m - 1],
                    device_id=_chip_peer(x, y, c, m), device_id_type=MESH))
                arrivals.append(pltpu.make_async_remote_copy(
                    src_ref=_shard_ref(ins[j], ax, k, n), dst_ref=outs[j].at[m],
                    send_sem=send_sems.at[j, m - 1], recv_sem=recv_sems.at[j, m - 1],
                    device_id=(x, y, c), device_id_type=MESH))
        return local, remote, arrivals

    def start(self, ins, outs, sems):
        local, remote, _ = self._copies(ins, outs, sems)
        for cp in local + remote:
            cp.start()

    def relay(self, ins, outs, sems):
        pass

    def wait(self, ins, outs, sems):
        local, remote, arrivals = self._copies(ins, outs, sems)
        for cp in arrivals:
            cp.wait_recv()
        for cp in remote:
            cp.wait_send()
        for cp in local:
            cp.wait()


def _run_plan(plan, name):
    nc = len(plan.inputs)

    def body(*refs):
        ins, outs, sems = refs[:nc], refs[nc:2 * nc], refs[2 * nc:]
        plan.start(ins, outs, sems)
        plan.relay(ins, outs, sems)
        plan.wait(ins, outs, sems)

    return pl.pallas_call(body, name=name, in_specs=[ANY] * nc, out_specs=[ANY] * nc, out_shape=list(plan.out_shapes),
                          scratch_shapes=list(plan.sem_shapes))(*plan.inputs)


class _SwapPlan:
    def __init__(self, parts):
        self.inputs, nw = list(parts), len(parts)
        self.out_shapes = [_sds(p.shape, p.dtype) for p in parts]
        self.sem_shapes = [pltpu.SemaphoreType.DMA((nw,)), pltpu.SemaphoreType.DMA((nw,))]

    def _copies(self, ins, outs, sems):
        send_sems, recv_sems = sems
        x, y, c = _place()
        return [pltpu.make_async_remote_copy(
            src_ref=ins[j], dst_ref=outs[j], send_sem=send_sems.at[j], recv_sem=recv_sems.at[j],
            device_id=(x, y, 1 - c), device_id_type=MESH) for j in range(len(ins))]

    def start(self, ins, outs, sems):
        for cp in self._copies(ins, outs, sems):
            cp.start()

    def relay(self, ins, outs, sems):
        pass

    def wait(self, ins, outs, sems):
        for cp in self._copies(ins, outs, sems):
            cp.wait()


class _SmallGatherPlan:
    def __init__(self, v):
        self.inputs = [v]
        self.out_shapes = [_sds((8,) + v.shape, v.dtype)]
        self.sem_shapes = [pltpu.SemaphoreType.DMA((1,)), pltpu.SemaphoreType.DMA((7,)), pltpu.SemaphoreType.DMA((7,))]

    def _copies(self, ins, outs, sems):
        (v_ref,), (out_ref,), (local_sem, send_sems, recv_sems) = ins, outs, sems
        x, y, c = _place()
        me = 4 * x + 2 * y + c
        local = pltpu.make_async_copy(v_ref, out_ref.at[me], local_sem.at[0])
        sends, arrivals = [], []
        for m in range(1, 8):
            px, py, pc = x ^ (m >> 2), y ^ ((m >> 1) & 1), c ^ (m & 1)
            sends.append(pltpu.make_async_remote_copy(
                src_ref=v_ref, dst_ref=out_ref.at[me], send_sem=send_sems.at[m - 1], recv_sem=recv_sems.at[m - 1],
                device_id=(px, py, pc), device_id_type=MESH))
            arrivals.append(pltpu.make_async_remote_copy(
                src_ref=v_ref, dst_ref=out_ref.at[4 * px + 2 * py + pc], send_sem=send_sems.at[m - 1],
                recv_sem=recv_sems.at[m - 1], device_id=(x, y, c), device_id_type=MESH))
        return local, sends, arrivals

    def start(self, ins, outs, sems):
        local, sends, _ = self._copies(ins, outs, sems)
        for cp in [local] + sends:
            cp.start()

    def relay(self, ins, outs, sems):
        pass

    def wait(self, ins, outs, sems):
        local, sends, arrivals = self._copies(ins, outs, sems)
        for cp in arrivals:
            cp.wait_recv()
        for cp in sends:
            cp.wait_send()
        local.wait()


class _PlanGroup:
    def __init__(self, plans):
        self.plans = [p for p in plans if p is not None]
        self.inputs = [a for p in self.plans for a in p.inputs]
        self.out_shapes = [s for p in self.plans for s in p.out_shapes]
        self.sem_shapes = [s for p in self.plans for s in p.sem_shapes]

    def _each(self, ins, outs, sems):
        i = s = 0
        for p in self.plans:
            n, ns = len(p.inputs), len(p.sem_shapes)
            yield p, ins[i:i + n], outs[i:i + n], sems[s:s + ns]
            i, s = i + n, s + ns

    def start(self, ins, outs, sems):
        for p, pi, po, ps in self._each(ins, outs, sems):
            p.start(pi, po, ps)

    def relay(self, ins, outs, sems):
        for p, pi, po, ps in self._each(ins, outs, sems):
            p.relay(pi, po, ps)

    def wait(self, ins, outs, sems):
        for p, pi, po, ps in self._each(ins, outs, sems):
            p.wait(pi, po, ps)

    def split(self, outs):
        res, i = [], 0
        for p in self.plans:
            res.append(outs[i:i + len(p.inputs)])
            i += len(p.inputs)
        return res


BIG = ("w_ffn1_in", "w_ffn1_out", "w_in", "w_pool_branch", "w_attn_branch", "w_out", "w_ffn2_in", "w_ffn2_out")
BIG_AXIS = {"w_ffn1_in": 1, "w_ffn1_out": 0, "w_in": 1, "w_pool_branch": 1, "w_attn_branch": 1, "w_out": 0,
            "w_ffn2_in": 1, "w_ffn2_out": 0}


class _Sharded:
    fused_scatter = True

    def __init__(self, shards):
        self.shards, self.full, self.recv = shards, {}, {}

    def gather_plan(self, names):
        return _GatherPlan([self.shards[n] for n in names], [BIG_AXIS[n.split("/")[0]] for n in names])

    def gather_now(self, names):
        self.gathered(names, _run_plan(self.gather_plan(names), "gather_" + names[0]))

    def gathered(self, names, outs):
        self.full.update(zip(names, outs))

    def scatter_plan(self, names, grads):
        return _ScatterPlan([grads[n] for n in names], [BIG_AXIS[n] for n in names])

    def scatter_now(self, names, grads):
        self.scattered(names, _run_plan(self.scatter_plan(names, grads), "scatter_" + names[0]))

    def scattered(self, names, outs):
        self.recv.update(zip(names, outs))


class _Whole:
    fused_scatter = False

    def __init__(self, full):
        self.full, self.recv = dict(full), {}

    def gather_plan(self, names):
        return None

    def gather_now(self, names):
        pass

    def gathered(self, names, outs):
        pass

    def scatter_plan(self, names, grads):
        return None

    def scatter_now(self, names, grads):
        pass

    def scattered(self, names, outs):
        pass


def _vec(rows):
    pad = [jnp.zeros((1, D), F32)] * (8 - len(rows))
    return jnp.concatenate([r.reshape(1, D) for r in rows] + pad, axis=0)


def _block_diag(w_pool):
    n, c = w_pool.shape[0], w_pool.shape[1]
    eye = jnp.eye(n, dtype=w_pool.dtype)
    return (eye[:, None, :, None] * w_pool[:, :, None, :]).reshape(n * c, n * c)


def _example_step(x, tgt, positions, mod, gains, w_pool, pool_scale, ws, pack=None):
    T = x.shape[0]
    assert (T // BLK // DIL[-1]) & (T // BLK // DIL[-1] - 1) == 0, "blocks per sequence must be a power of two"
    sh1, sc1, gt1, sh2, sc2, gt2, sh3, sc3, gt3 = [mod[j * D:(j + 1) * D] for j in range(NMOD)]
    g1, g2, g3, gf = gains
    vec1, vec2, vec3 = _vec([g1, sh1, sc1, gt1]), _vec([g2, sh2, sc2, gt2]), _vec([g3, sh3, sc3, gt3])
    inv_freq = 10000.0 ** (-jnp.arange(0, HD, 2, dtype=F32) / HD)
    ang = positions.astype(F32)[:, None] * inv_freq
    cos = jnp.tile(jnp.cos(ang), (1, 4))
    sin = jnp.tile(jnp.concatenate([-jnp.sin(ang), jnp.sin(ang)], axis=1), (1, 2))
    wp_bd = _block_diag(w_pool).astype(BF16)
    ones_bd = _block_diag(jnp.ones((NH, HD, HD), F32)).astype(BF16)
    ps = jnp.concatenate([pool_scale.reshape(1, PW), jnp.zeros((7, PW), F32)], axis=0)
    wb = ws.full

    if "w_ffn1_in" not in wb:
        ws.gather_now(["w_ffn1_in"])
    (u1, a1, b1), got = _ffn_ab(x, vec1, [wb["w_ffn1_in"]], "ffn1_ab", ws.gather_plan(["w_ffn1_out", "w_in"]))
    ws.gathered(["w_ffn1_out", "w_in"], got)
    mixw = ["w_pool_branch", "w_attn_branch", "w_out"]
    (h1, f1), got = _ffn_out(x, a1, b1, vec1, wb["w_ffn1_out"], "ffn1_out", ws.gather_plan(mixw))
    ws.gathered(mixw, got)
    (u2, p, qs, ks, vs, gates), got = _mix_proj(h1, vec2, wb["w_in"], cos, sin, ws.gather_plan(["w_ffn2_in/0"]))
    ws.gathered(["w_ffn2_in/0"], got)
    qs, ks, vs = [_flat(t) for t in qs], [_flat(t) for t in ks], [_flat(t) for t in vs]
    nbs = [T // d // BLK for d in DIL]
    os, lses = [], []
    for gi, riders in enumerate((["w_ffn2_out"], ["w_ffn2_in/1"], None)):
        (o, lse), got = _attn_fwd(qs[gi], ks[gi], vs[gi], nbs[gi], f"attn_fwd{gi}", riders and ws.gather_plan(riders))
        ws.gathered(riders or [], got)
        os.append(o)
        lses.append(lse)
    win3 = [wb["w_ffn2_in/0"], wb["w_ffn2_in/1"]] if "w_ffn2_in/0" in wb else [wb["w_ffn2_in"]]
    os_r = [_by_residue(t, d) for t, d in zip(os, DIL)]
    lses_r = [_by_residue(t, d) for t, d in zip(lses, DIL)]
    h2, ypool, yattn, merged, mixout, dpool = _mix_merge(
        h1, vec2, p, os_r, lses_r, gates, wp_bd, ps, wb["w_pool_branch"], wb["w_attn_branch"], wb["w_out"])
    (dh3, u3, a3, b3, f3, lacc), _ = _ffn_fwd(h2, vec3, win3, wb["w_ffn2_out"], "ffn2_fwd", head=(tgt, _vec([gf])))
    loss = 0.5 * jnp.sum(lacc[0]) / D

    grads = {}

    def wgrad_cols(name, xx, yy, riders, extra=None):
        group = _PlanGroup([ws.scatter_plan(riders, grads) if riders else None, extra])
        plan = group if group.plans else None
        if ws.fused_scatter:
            blocks, got = _wgrad_scatter(xx, yy, "wg_" + name, min(2048, T // 2), comm=plan)
            ws.scattered([name], [blocks])
        else:
            grads[name], got = _wgrad(xx, yy, "wg_" + name, D, 512, 1024, comm=plan)
        parts = group.split(got)
        if len(parts) > (extra is not None):
            ws.scattered(riders, parts[0])
        return parts[-1] if extra is not None else None

    (dh2, dab3, s3, df3, acc3), _ = _ffn_bwd(dh3, h2, a3, b3, f3, vec3, win3, wb["w_ffn2_out"], "ffn2_bwd")
    grads["w_ffn2_out"], _ = _wgrad(s3, df3, "wg_ffn2_out", FF // 2, 512, min(4096, T // 2))
    wgrad_cols("w_ffn2_in", u3, dab3, ["w_ffn2_out"])
    (dgates, do0, do1, do2, e0, e1, e2, dd, acc2a, accps,
     grads["w_out"], grads["w_pool_branch"], grads["w_attn_branch"], gwp), _ = _mix_bwd_a(
        dh2, vec2, mixout, merged, gates, ypool, yattn, dpool, os_r, lses_r, wp_bd, ps,
        wb["w_pool_branch"], wb["w_attn_branch"], wb["w_out"], ones_bd)
    n = len(POOL_WINDOWS)
    c = PW // n
    grad_w_pool = jnp.stack([gwp[j * c:(j + 1) * c, j * c:(j + 1) * c] for j in range(n)], axis=0)
    small3 = ["w_out", "w_pool_branch", "w_attn_branch"]
    dqs, dks, dvs = [], [], []
    for gi, (do, e) in enumerate(((do0, e0), (do1, e1), (do2, e2))):
        plan = ws.scatter_plan(small3, grads) if gi == 0 else None
        (dq, dk, dv), got = _attn_bwd(qs[gi], ks[gi], vs[gi], _flat(do), lses[gi], _flat(e), nbs[gi], f"attn_bwd{gi}", plan)
        if gi == 0:
            ws.scattered(small3, got)
        dqs.append(_by_residue(dq, DIL[gi]))
        dks.append(_by_residue(dk, DIL[gi]))
        dvs.append(_by_residue(dv, DIL[gi]))
    dh1, dproj, acc2b = _mix_bwd_b(dh2, h1, vec2, dd, dqs, dks, dvs, dgates, cos, sin, wb["w_in"])
    wgrad_cols("w_in", u2, dproj, [])
    (dx, dab1, s1, df1, acc1), _ = _ffn_bwd(dh1, x, a1, b1, f1, vec1, [wb["w_ffn1_in"]], wb["w_ffn1_out"], "ffn1_bwd")
    grads["w_ffn1_out"], _ = _wgrad(s1, df1, "wg_ffn1_out", FF // 2, 512, min(4096, T // 2))
    dmod = jnp.concatenate([acc1[0], acc1[1], acc1[3], acc2b[0], acc2b[1], acc2a[3], acc3[0], acc3[1], acc3[3]])
    dgains = jnp.stack([acc1[2], acc2b[2], acc3[2], lacc[1]], axis=0)
    row = None if pack is None else _SmallGatherPlan(pack(loss, dmod, dgains, grad_w_pool, accps[0]))
    rows = wgrad_cols("w_ffn1_in", u1, dab1, ["w_ffn1_out"], row)
    return loss, dx, dmod, dgains, grad_w_pool, accps[0], grads, None if rows is None else rows[0]


SMALL = ("b_ada", "g_norm_ffn1", "g_norm_mix", "g_norm_ffn2", "g_final", "pool_scale", "w_pool")
WEIGHTS = ("w_ada", "b_ada", "g_norm_ffn1", "w_ffn1_in", "w_ffn1_out", "g_norm_mix", "w_in", "w_pool", "pool_scale",
           "w_pool_branch", "w_attn_branch", "w_out", "g_norm_ffn2", "w_ffn2_in", "w_ffn2_out", "g_final")


def _pack_small(t):
    return jnp.concatenate([t[n].reshape(-1) for n in SMALL]).reshape(1, -1)


def kernel(x, c, positions, w_ada, b_ada, g_norm_ffn1, w_ffn1_in, w_ffn1_out, g_norm_mix, w_in, w_pool, pool_scale, w_pool_branch, w_attn_branch, w_out, g_norm_ffn2, w_ffn2_in, w_ffn2_out, g_final, loss_target, m_w_ada, m_b_ada, m_g_norm_ffn1, m_w_ffn1_in, m_w_ffn1_out, m_g_norm_mix, m_w_in, m_w_pool, m_pool_scale, m_w_pool_branch, m_w_attn_branch, m_w_out, m_g_norm_ffn2, m_w_ffn2_in, m_w_ffn2_out, m_g_final, v_w_ada, v_b_ada, v_g_norm_ffn1, v_w_ffn1_in, v_w_ffn1_out, v_g_norm_mix, v_w_in, v_w_pool, v_pool_scale, v_w_pool_branch, v_w_attn_branch, v_w_out, v_g_norm_ffn2, v_w_ffn2_in, v_w_ffn2_out, v_g_final):
    w = dict(w_ada=w_ada, b_ada=b_ada, g_norm_ffn1=g_norm_ffn1, w_ffn1_in=w_ffn1_in, w_ffn1_out=w_ffn1_out,
             g_norm_mix=g_norm_mix, w_in=w_in, w_pool=w_pool, pool_scale=pool_scale, w_pool_branch=w_pool_branch,
             w_attn_branch=w_attn_branch, w_out=w_out, g_norm_ffn2=g_norm_ffn2, w_ffn2_in=w_ffn2_in,
             w_ffn2_out=w_ffn2_out, g_final=g_final)
    mom = dict(w_ada=m_w_ada, b_ada=m_b_ada, g_norm_ffn1=m_g_norm_ffn1, w_ffn1_in=m_w_ffn1_in, w_ffn1_out=m_w_ffn1_out,
               g_norm_mix=m_g_norm_mix, w_in=m_w_in, w_pool=m_w_pool, pool_scale=m_pool_scale,
               w_pool_branch=m_w_pool_branch, w_attn_branch=m_w_attn_branch, w_out=m_w_out, g_norm_ffn2=m_g_norm_ffn2,
               w_ffn2_in=m_w_ffn2_in, w_ffn2_out=m_w_ffn2_out, g_final=m_g_final)
    var = dict(w_ada=v_w_ada, b_ada=v_b_ada, g_norm_ffn1=v_g_norm_ffn1, w_ffn1_in=v_w_ffn1_in, w_ffn1_out=v_w_ffn1_out,
               g_norm_mix=v_g_norm_mix, w_in=v_w_in, w_pool=v_w_pool, pool_scale=v_pool_scale,
               w_pool_branch=v_w_pool_branch, w_attn_branch=v_w_attn_branch, w_out=v_w_out, g_norm_ffn2=v_g_norm_ffn2,
               w_ffn2_in=v_w_ffn2_in, w_ffn2_out=v_w_ffn2_out, g_final=v_g_final)
    ix, iy, ic = _place()
    chip = 2 * ix + iy
    me = 4 * ix + 2 * iy + ic
    nada = w_ada.shape[2]

    shards = {n: w[n][0].astype(BF16) for n in BIG}
    half = D // 2
    shards["w_ffn2_in/0"], shards["w_ffn2_in/1"] = shards["w_ffn2_in"][:half], shards["w_ffn2_in"][half:]
    ws = _Sharded(shards)
    c_all = _run_plan(_SmallGatherPlan(c), "gather_c")[0][:, 0, :]
    b_shard = lax.dynamic_slice_in_dim(b_ada, chip * nada, nada, axis=1)
    mod_cols = _ada_fwd(c_all, w_ada[0], b_shard)
    first = _PlanGroup([_SmallGatherPlan(mod_cols), ws.gather_plan(["w_ffn1_in"])])
    (mod_all,), ffn1 = first.split(_run_plan(first, "gather_first"))
    ws.gathered(["w_ffn1_in"], ffn1)
    mod = jnp.concatenate([lax.dynamic_index_in_dim(mod_all[4 * (kk >> 1) + 2 * (kk & 1)], me, axis=0, keepdims=False)
                           for kk in range(4)])

    def pack(loss, dmod, dgains, g_w_pool, g_pool_scale):
        small_g = dict(b_ada=dmod, g_norm_ffn1=dgains[0], g_norm_mix=dgains[1], g_norm_ffn2=dgains[2],
                       g_final=dgains[3], pool_scale=g_pool_scale, w_pool=g_w_pool)
        return jnp.concatenate([_pack_small(small_g), jnp.pad(loss.reshape(1, 1), ((0, 0), (0, 127)))], axis=1)

    _, dx, _, _, _, _, _, gathered = _example_step(
        x[0], loss_target[0], positions[0], mod, (g_norm_ffn1[0], g_norm_mix[0], g_norm_ffn2[0], g_final),
        w_pool[0], pool_scale[0], ws, pack)

    per_weight, loss_tile = _adam_small(*[[t[n].reshape(1, -1) for n in SMALL] for t in (w, mom, var)], gathered)
    small_out = [{n: per_weight[j][kind].reshape(w[n].shape) for j, n in enumerate(SMALL)} for kind in range(4)]
    loss = loss_tile[0, 0]

    dmod_all = gathered[:, 0, :NMOD * D]
    dmod_cols = lax.dynamic_slice_in_dim(dmod_all, chip * nada, nada, axis=1)
    g_ada = _ada_bwd(c_all, dmod_cols)

    ada_out = _adam(w_ada[0], m_w_ada[0], v_w_ada[0], [g_ada], "adam_w_ada")[0]

    sums = {n: _sum4(ws.recv[n], "sum_" + n) for n in BIG}
    other = dict(zip(BIG, _run_plan(_SwapPlan([sums[n] for n in BIG]), "swap_sibling")))
    big_out = {}
    for n in BIG:
        if sums[n].shape[0] < w[n].shape[1]:
            big_out[n] = _adam_halves(w[n][0], mom[n][0], var[n][0], sums[n], other[n], "adam_" + n)
        else:
            big_out[n] = _adam(w[n][0], mom[n][0], var[n][0], [sums[n], other[n]], "adam_" + n)[0]

    def leaf(kind, n):
        if n == "w_ada":
            return ada_out[kind][None]
        if n in big_out:
            return big_out[n][kind][None]
        return small_out[kind][n]

    return (loss, dx[None], *[leaf(kind, n) for kind in range(4) for n in WEIGHTS])
```

```python
import jax
import jax.numpy as jnp
from jax import lax
from jax.experimental import pallas as pl
from jax.experimental.pallas import tpu as pltpu

F32 = jnp.float32
BF16 = jnp.bfloat16

D = 1024
FF = 2816
FC = FF
PW = 256
GA = 256
HD = 64
LANES = 128
NH = GA // HD
NG = 3
DIL = (1, 4, 16)
BLK = 128
FWD_BLOCKS = 8
BWD_BLOCKS = 8
GW = 2 * D
INW = PW + 3 * NG * GA + GW
NMOD = 9
POOL_WINDOWS = (2, 4, 8, 16)
HALO = 16
EPS = 1e-6
SCALE = HD ** -0.5
NEG = -1e30

LR, B1, B2, AEPS, WD, STEP = 0.001, 0.9, 0.999, 1e-08, 0.01, 10

VMEM_BIG = 56 * 1024 * 1024
TM = 256

MESH = pl.DeviceIdType.MESH
ANY = pl.BlockSpec(memory_space=pl.ANY)


def _call(body, name, grid, in_specs, out_specs, out_shape, scratch=(), vmem=None, comm=None):
    params = pltpu.CompilerParams(dimension_semantics=("arbitrary",) * len(grid), vmem_limit_bytes=vmem)
    n_in, n_out, n_scr = len(in_specs), len(out_shape), len(scratch)
    if comm is None:
        call = pl.pallas_call(body, name=name, grid=grid, in_specs=list(in_specs), out_specs=list(out_specs),
                              out_shape=list(out_shape), scratch_shapes=list(scratch), compiler_params=params)
        return lambda *args: (call(*args), ())
    nc = len(comm.inputs)

    def body_with_comm(*refs):
        ins, refs = refs[:n_in], refs[n_in:]
        c_ins, refs = refs[:nc], refs[nc:]
        outs, refs = refs[:n_out], refs[n_out:]
        c_outs, refs = refs[:nc], refs[nc:]
        scr, sems = refs[:n_scr], refs[n_scr:]
        first = pl.program_id(0) == 0
        last = pl.program_id(0) == grid[0] - 1
        for ax in range(1, len(grid)):
            first = jnp.logical_and(first, pl.program_id(ax) == 0)
            last = jnp.logical_and(last, pl.program_id(ax) == grid[ax] - 1)

        @pl.when(first)
        def _():
            comm.start(c_ins, c_outs, sems)

        body(*ins, *outs, *scr)
        early_relay = len(grid) == 1 and grid[0] >= 4
        if early_relay:
            @pl.when(pl.program_id(0) == (3 * grid[0]) // 4)
            def _():
                comm.relay(c_ins, c_outs, sems)

        @pl.when(last)
        def _():
            if not early_relay:
                comm.relay(c_ins, c_outs, sems)
            comm.wait(c_ins, c_outs, sems)

    call = pl.pallas_call(
        body_with_comm, name=name, grid=grid, in_specs=list(in_specs) + [ANY] * nc,
        out_specs=list(out_specs) + [ANY] * nc, out_shape=list(out_shape) + list(comm.out_shapes),
        scratch_shapes=list(scratch) + list(comm.sem_shapes), compiler_params=params)

    def run(*args):
        res = call(*args, *comm.inputs)
        return res[:n_out], res[n_out:]

    return run


def _rows(tm, n):
    return pl.BlockSpec((tm, n), lambda i: (i, 0))


def _const(shape):
    return pl.BlockSpec(shape, lambda i: (0,) * len(shape))


def _sds(shape, dtype):
    return jax.ShapeDtypeStruct(shape, dtype)


def _dot(a, b):
    return jnp.dot(a, b, preferred_element_type=F32)


def _dot_nt(a, b):
    return lax.dot_general(a, b, (((1,), (1,)), ((), ())), preferred_element_type=F32)


def _dot_tn(a, b):
    return lax.dot_general(a, b, (((0,), (0,)), ((), ())), preferred_element_type=F32)


def _colsum(v):
    return jnp.sum(v, axis=0, keepdims=True)


def _norm_fwd(h, g, sh, sc):
    r = lax.rsqrt(jnp.mean(h * h, axis=-1, keepdims=True) + EPS)
    xh = h * r
    n = xh * g
    return xh, r, n, n * (1.0 + sc) + sh


def _norm_bwd(du, xh, r, n, g, sc):
    dn = du * (1.0 + sc)
    dxh = dn * g
    dh = r * (dxh - xh * jnp.mean(dxh * xh, axis=-1, keepdims=True))
    return dh, _colsum(du), _colsum(du * n), _colsum(dn * xh)


def _load_once(pairs, sems):
    @pl.when(pl.program_id(0) == 0)
    def _():
        cps = [pltpu.make_async_copy(src, dst, sems.at[j]) for j, (src, dst) in enumerate(pairs)]
        for cp in cps:
            cp.start()
        for cp in cps:
            cp.wait()


def _zero_first(ref):
    @pl.when(pl.program_id(0) == 0)
    def _():
        ref[...] = jnp.zeros(ref.shape, ref.dtype)


def _row_chunks(hbm_refs, vmem_ref):
    pairs, row = [], 0
    for ref in hbm_refs:
        pairs.append((ref, vmem_ref.at[pl.ds(row, ref.shape[0]), :]))
        row += ref.shape[0]
    return pairs


def _loss_head(hh, tgt, g):
    r = lax.rsqrt(jnp.mean(hh * hh, axis=-1, keepdims=True) + EPS)
    xh = hh * r
    err = xh * g - tgt
    dy = err * (1.0 / D)
    dxh = dy * g
    dh = r * (dxh - xh * jnp.mean(dxh * xh, axis=-1, keepdims=True))
    return dh, _colsum(err * err), _colsum(dy * xh)


def _ffn_fwd(h, vec, wins, wout, name, comm=None, head=None):
    T = h.shape[0]
    nwin = len(wins)
    nhead = 0 if head is None else 2

    def body(h_ref, vec_ref, *rest):
        head_refs, rest = rest[:nhead], rest[nhead:]
        win_hbms, rest = rest[:nwin], rest[nwin:]
        (wout_hbm, ho_ref, u_ref, a_ref, b_ref, f_ref), rest = rest[:6], rest[6:]
        lacc_refs, (win_v, wout_v, sems) = rest[:nhead // 2], rest[nhead // 2:]
        _load_once(_row_chunks(win_hbms, win_v) + [(wout_hbm, wout_v)], sems)
        hh = h_ref[...]
        g, sh, sc, gt = vec_ref[0:1, :], vec_ref[1:2, :], vec_ref[2:3, :], vec_ref[3:4, :]
        _, _, _, u = _norm_fwd(hh, g, sh, sc)
        ub = u.astype(BF16)
        u_ref[...] = ub
        acc = None
        for j in range(FF // FC):
            lo, hi = j * FC, (j + 1) * FC
            a = _dot(ub, win_v[:, lo:hi])
            b = _dot(ub, win_v[:, FF + lo:FF + hi])
            a_ref[:, lo:hi] = a.astype(BF16)
            b_ref[:, lo:hi] = b.astype(BF16)
            s = (a * jax.nn.sigmoid(a) * b).astype(BF16)
            part = _dot(s, wout_v[lo:hi, :])
            acc = part if acc is None else acc + part
        f_ref[...] = acc.astype(BF16)
        ho = hh + 0.5 * gt * acc
        if head is None:
            ho_ref[...] = ho
        else:
            _zero_first(lacc_refs[0])
            dh, sq, dg = _loss_head(ho, head_refs[0][...], head_refs[1][0:1, :])
            ho_ref[...] = dh
            lacc_refs[0][0:1, :] += sq
            lacc_refs[0][1:2, :] += dg

    head_specs = [] if head is None else [_rows(TM, D), _const((8, D))]
    lacc_spec = [] if head is None else [_const((8, D))]
    lacc_shape = [] if head is None else [_sds((8, D), F32)]
    return _call(
        body, name, (T // TM,),
        [_rows(TM, D), _const((8, D))] + head_specs + [ANY] * (nwin + 1),
        [_rows(TM, D), _rows(TM, D), _rows(TM, FF), _rows(TM, FF), _rows(TM, D)] + lacc_spec,
        [_sds((T, D), F32), _sds((T, D), BF16), _sds((T, FF), BF16), _sds((T, FF), BF16), _sds((T, D), BF16)] + lacc_shape,
        scratch=[pltpu.VMEM((D, 2 * FF), BF16), pltpu.VMEM((FF, D), BF16), pltpu.SemaphoreType.DMA((nwin + 1,))],
        vmem=VMEM_BIG, comm=comm,
    )(h, vec, *([] if head is None else head), *wins, wout)


def _ffn_ab(h, vec, wins, name, comm=None):
    T = h.shape[0]
    nwin = len(wins)

    def body(h_ref, vec_ref, *rest):
        win_hbms, (u_ref, a_ref, b_ref, win_v, sems) = rest[:nwin], rest[nwin:]
        _load_once(_row_chunks(win_hbms, win_v), sems)
        g, sh, sc = vec_ref[0:1, :], vec_ref[1:2, :], vec_ref[2:3, :]
        _, _, _, u = _norm_fwd(h_ref[...], g, sh, sc)
        ub = u.astype(BF16)
        u_ref[...] = ub
        for j in range(FF // FC):
            lo, hi = j * FC, (j + 1) * FC
            a_ref[:, lo:hi] = _dot(ub, win_v[:, lo:hi]).astype(BF16)
            b_ref[:, lo:hi] = _dot(ub, win_v[:, FF + lo:FF + hi]).astype(BF16)

    return _call(
        body, name, (T // TM,),
        [_rows(TM, D), _const((8, D))] + [ANY] * nwin,
        [_rows(TM, D), _rows(TM, FF), _rows(TM, FF)],
        [_sds((T, D), BF16), _sds((T, FF), BF16), _sds((T, FF), BF16)],
        scratch=[pltpu.VMEM((D, 2 * FF), BF16), pltpu.SemaphoreType.DMA((nwin,))],
        vmem=VMEM_BIG, comm=comm,
    )(h, vec, *wins)


def _ffn_out(h, a, b, vec, wout, name, comm=None):
    T = h.shape[0]

    def body(h_ref, a_ref, b_ref, vec_ref, wout_hbm, ho_ref, f_ref, wout_v, sems):
        _load_once([(wout_hbm, wout_v)], sems)
        gt = vec_ref[3:4, :]
        acc = None
        for j in range(FF // FC):
            lo, hi = j * FC, (j + 1) * FC
            av = a_ref[:, lo:hi].astype(F32)
            s = (av * jax.nn.sigmoid(av) * b_ref[:, lo:hi].astype(F32)).astype(BF16)
            part = _dot(s, wout_v[lo:hi, :])
            acc = part if acc is None else acc + part
        f_ref[...] = acc.astype(BF16)
        ho_ref[...] = h_ref[...] + 0.5 * gt * acc

    return _call(
        body, name, (T // TM,),
        [_rows(TM, D), _rows(TM, FF), _rows(TM, FF), _const((8, D)), ANY],
        [_rows(TM, D), _rows(TM, D)],
        [_sds((T, D), F32), _sds((T, D), BF16)],
        scratch=[pltpu.VMEM((FF, D), BF16), pltpu.SemaphoreType.DMA((1,))],
        vmem=VMEM_BIG, comm=comm,
    )(h, a, b, vec, wout)


def _ffn_bwd(dh, h, a, b, f, vec, wins, wout, name, comm=None):
    T = h.shape[0]
    nwin = len(wins)

    def body(dh_ref, h_ref, a_ref, b_ref, f_ref, vec_ref, *rest):
        win_hbms, (wout_hbm, dhi_ref, dab_ref, s_ref, df_ref, acc_ref, win_v, wout_v, sems) = rest[:nwin], rest[nwin:]
        _load_once(_row_chunks(win_hbms, win_v) + [(wout_hbm, wout_v)], sems)
        _zero_first(acc_ref)
        g, sh, sc, gt = vec_ref[0:1, :], vec_ref[1:2, :], vec_ref[2:3, :], vec_ref[3:4, :]
        dho = dh_ref[...]
        df = (0.5 * gt * dho).astype(BF16)
        df_ref[...] = df
        dgt = _colsum(0.5 * dho * f_ref[...].astype(F32))
        du = None
        for j in range(FF // FC):
            lo, hi = j * FC, (j + 1) * FC
            av = a_ref[:, lo:hi].astype(F32)
            bv = b_ref[:, lo:hi].astype(F32)
            ds = _dot_nt(df, wout_v[lo:hi, :])
            sig = jax.nn.sigmoid(av)
            sa = av * sig
            s_ref[:, lo:hi] = (sa * bv).astype(BF16)
            da = (ds * bv * (sig * (1.0 + av * (1.0 - sig)))).astype(BF16)
            db = (ds * sa).astype(BF16)
            dab_ref[:, lo:hi] = da
            dab_ref[:, FF + lo:FF + hi] = db
            part = _dot_nt(da, win_v[:, lo:hi]) + _dot_nt(db, win_v[:, FF + lo:FF + hi])
            du = part if du is None else du + part
        xh, r, n, _ = _norm_fwd(h_ref[...], g, sh, sc)
        dhn, dsh, dsc, dg = _norm_bwd(du, xh, r, n, g, sc)
        dhi_ref[...] = dho + dhn
        acc_ref[0:1, :] += dsh
        acc_ref[1:2, :] += dsc
        acc_ref[2:3, :] += dg
        acc_ref[3:4, :] += dgt

    return _call(
        body, name, (T // TM,),
        [_rows(TM, D), _rows(TM, D), _rows(TM, FF), _rows(TM, FF), _rows(TM, D), _const((8, D))] + [ANY] * (nwin + 1),
        [_rows(TM, D), _rows(TM, 2 * FF), _rows(TM, FF), _rows(TM, D), _const((8, D))],
        [_sds((T, D), F32), _sds((T, 2 * FF), BF16), _sds((T, FF), BF16), _sds((T, D), BF16), _sds((8, D), F32)],
        scratch=[pltpu.VMEM((D, 2 * FF), BF16), pltpu.VMEM((FF, D), BF16), pltpu.SemaphoreType.DMA((nwin + 1,))],
        vmem=VMEM_BIG, comm=comm,
    )(dh, h, a, b, f, vec, *wins, wout)


def _wgrad(x, y, name, tk, tn, tt, out_dtype=BF16, comm=None):
    T, K = x.shape
    N = y.shape[1]
    nt = T // tt

    def body(x_ref, y_ref, o_ref, acc_ref):
        t = pl.program_id(2)
        part = _dot_tn(x_ref[...], y_ref[...])

        @pl.when(t == 0)
        def _():
            acc_ref[...] = part

        @pl.when(t > 0)
        def _():
            acc_ref[...] += part

        @pl.when(t == nt - 1)
        def _():
            o_ref[...] = acc_ref[...].astype(out_dtype)

    (out,), c_outs = _call(
        body, name, (K // tk, N // tn, nt),
        [pl.BlockSpec((tt, tk), lambda i, j, t: (t, i)), pl.BlockSpec((tt, tn), lambda i, j, t: (t, j))],
        [pl.BlockSpec((tk, tn), lambda i, j, t: (i, j))], [_sds((K, N), out_dtype)],
        scratch=[pltpu.VMEM((tk, tn), F32)], vmem=VMEM_BIG, comm=comm,
    )(x, y)
    return out, c_outs


def _wgrad_scatter(x, y, name, tt, comm=None):
    T, K = x.shape
    n = y.shape[1] // 4
    nt = T // tt
    assert nt >= 2, "a block's hand-over is added one grid step into the next block"
    half = K // 2
    nc = 0 if comm is None else len(comm.inputs)

    def body(chip_ref, x_ref, y_ref, *refs):
        c_ins, refs = refs[:nc], refs[nc:]
        recv_ref, refs = refs[0], refs[1:]
        c_outs, refs = refs[:nc], refs[nc:]
        acc_ref, keep_ref, give_ref, take_ref, local_sem, give_sems, take_sems, send_sems, recv_sems = refs[:9]
        j, t = pl.program_id(0), pl.program_id(1)
        px, py, pc = _place()

        def hand_over(jj):
            return pltpu.make_async_remote_copy(
                src_ref=give_ref.at[jj], dst_ref=take_ref.at[jj], send_sem=give_sems.at[jj], recv_sem=take_sems.at[jj],
                device_id=(px, py, 1 - pc), device_id_type=MESH)

        def send(jj):
            m = (3, 1, 2)[jj]
            return pltpu.make_async_remote_copy(
                src_ref=keep_ref.at[jj], dst_ref=recv_ref.at[m], send_sem=send_sems.at[jj], recv_sem=recv_sems.at[jj],
                device_id=_chip_peer(px, py, pc, m), device_id_type=MESH)

        def add_sibling(jj):
            hand_over(jj).wait_recv()
            keep_ref[jj] = (keep_ref[jj].astype(F32) + take_ref[jj].astype(F32)).astype(BF16)

        if comm is not None:
            @pl.when(jnp.logical_and(j == 0, t == 0))
            def _():
                comm.start(c_ins, c_outs, refs[9:])

        part = _dot_tn(x_ref[...], y_ref[...])

        @pl.when(t == 0)
        def _():
            acc_ref[...] = part

        @pl.when(t > 0)
        def _():
            acc_ref[...] += part

        for jj in range(3):
            @pl.when(jnp.logical_and(j == jj + 1, t == 0))
            def _():
                add_sibling(jj)
                send(jj).start()

        for jj in range(4):
            @pl.when(jnp.logical_and(j == jj, t == nt - 1))
            def _():
                keep_ref[jj] = acc_ref[pl.ds(pl.multiple_of(pc * half, 16), half), :].astype(BF16)
                give_ref[jj] = acc_ref[pl.ds(pl.multiple_of((1 - pc) * half, 16), half), :].astype(BF16)
                hand_over(jj).start()

        @pl.when(jnp.logical_and(j == 3, t == nt - 1))
        def _():
            add_sibling(3)
            own = pltpu.make_async_copy(keep_ref.at[3], recv_ref.at[0], local_sem.at[0])
            own.start()
            for jj in range(3):
                send(jj).wait_recv()
            for jj in range(3):
                send(jj).wait_send()
            for jj in range(4):
                hand_over(jj).wait_send()
            own.wait()
            if comm is not None:
                comm.relay(c_ins, c_outs, refs[9:])
                comm.wait(c_ins, c_outs, refs[9:])

    grid_spec = pltpu.PrefetchScalarGridSpec(
        num_scalar_prefetch=1, grid=(4, nt),
        in_specs=[pl.BlockSpec((tt, K), lambda j, t, chip: (t, 0)),
                  pl.BlockSpec((tt, n), lambda j, t, chip: (t, chip[0] ^ jnp.where(j == 0, 3, jnp.where(j == 3, 0, j))))]
        + [ANY] * nc,
        out_specs=[ANY] * (1 + nc),
        scratch_shapes=[pltpu.VMEM((K, n), F32)] + [pltpu.VMEM((4, half, n), BF16)] * 3
        + [pltpu.SemaphoreType.DMA((1,))] + [pltpu.SemaphoreType.DMA((4,))] * 2 + [pltpu.SemaphoreType.DMA((3,))] * 2
        + ([] if comm is None else list(comm.sem_shapes)))
    px, py, _ = _place()
    res = pl.pallas_call(
        body, name=name, grid_spec=grid_spec,
        out_shape=[_sds((4, half, n), BF16)] + ([] if comm is None else list(comm.out_shapes)),
        compiler_params=pltpu.CompilerParams(dimension_semantics=("arbitrary", "arbitrary"), vmem_limit_bytes=VMEM_BIG),
    )((2 * px + py).astype(jnp.int32).reshape(1), x, y, *([] if comm is None else comm.inputs))
    return res[0], res[1:]


def _swap_halves(t):
    w = t.shape[1]
    lane = lax.broadcasted_iota(jnp.int32, t.shape, 1)
    return jnp.where(lane % HD < HD // 2, pltpu.roll(t, w - HD // 2, 1), pltpu.roll(t, HD // 2, 1))


def _rope(t, cos, sin_signed):
    c = jnp.tile(cos, (1, t.shape[1] // cos.shape[1]))
    s = jnp.tile(sin_signed, (1, t.shape[1] // sin_signed.shape[1]))
    return t * c + _swap_halves(t) * s


def _rope_bwd(dt, cos, sin_signed):
    c = jnp.tile(cos, (1, dt.shape[1] // cos.shape[1]))
    s = jnp.tile(sin_signed, (1, dt.shape[1] // sin_signed.shape[1]))
    return dt * c + _swap_halves(dt * s)


def _rm_spec(dil):
    return pl.BlockSpec((dil, TM // dil, GA), lambda i: (0, i, 0))


def _to_residues(t, dst_ref, scr_ref, dil):
    if dil == 1:
        dst_ref[0] = t.astype(dst_ref.dtype)
        return
    for j in range(GA // LANES):
        scr_ref[j] = t[:, j * LANES:(j + 1) * LANES]
    for r in range(dil):
        for j in range(GA // LANES):
            rows = scr_ref.at[j][pl.ds(r, TM // dil, stride=dil), :]
            dst_ref[r, :, j * LANES:(j + 1) * LANES] = rows.astype(dst_ref.dtype)


def _from_residues(src_ref, scr_ref, dil):
    if dil == 1:
        return src_ref[0].astype(F32)
    for r in range(dil):
        for j in range(GA // LANES):
            scr_ref.at[j][pl.ds(r, TM // dil, stride=dil), :] = src_ref[r, :, j * LANES:(j + 1) * LANES].astype(F32)
    return jnp.concatenate([scr_ref[j] for j in range(GA // LANES)], axis=1)


def _mix_proj(h, vec, win, cos, sin, comm=None):
    T = h.shape[0]

    def body(h_ref, vec_ref, win_hbm, cos_ref, sin_ref, u_ref, p_ref, *rest):
        qkv_refs, gates_ref, win_v, scr_ref, sems = rest[:3 * NG], rest[3 * NG], rest[3 * NG + 1], rest[3 * NG + 2], rest[3 * NG + 3]
        _load_once([(win_hbm, win_v)], sems)
        g, sh, sc = vec_ref[0:1, :], vec_ref[1:2, :], vec_ref[2:3, :]
        _, _, _, u = _norm_fwd(h_ref[...], g, sh, sc)
        ub = u.astype(BF16)
        u_ref[...] = ub
        mixer_cols = PW + 3 * NG * GA
        proj = _dot(ub, win_v[:, 0:mixer_cols])
        p_ref[...] = proj[:, 0:PW]
        cos_t, sin_t = cos_ref[...], sin_ref[...]
        for j in range(3 * NG):
            col = PW + j * GA
            t = proj[:, col:col + GA]
            if j < 2 * NG:
                t = _rope(t, cos_t, sin_t)
            _to_residues(t, qkv_refs[j], scr_ref, DIL[j % NG])
        gates_ref[...] = jax.nn.sigmoid(_dot(ub, win_v[:, mixer_cols:INW])).astype(BF16)

    outs, c_outs = _call(
        body, "mix_proj", (T // TM,),
        [_rows(TM, D), _const((8, D)), ANY, _rows(TM, 128), _rows(TM, 128)],
        [_rows(TM, D), _rows(TM, PW)] + [_rm_spec(d) for d in DIL] * 3 + [_rows(TM, GW)],
        [_sds((T, D), BF16), _sds((T, PW), F32)] + [_sds((d, T // d, GA), BF16) for d in DIL] * 3 + [_sds((T, GW), BF16)],
        scratch=[pltpu.VMEM((D, INW), BF16), pltpu.VMEM((GA // LANES, TM, LANES), F32), pltpu.SemaphoreType.DMA((1,))],
        vmem=VMEM_BIG, comm=comm,
    )(h, vec, win, cos, sin)
    return (outs[0], outs[1], outs[2:2 + NG], outs[2 + NG:2 + 2 * NG], outs[2 + 2 * NG:2 + 3 * NG], outs[2 + 3 * NG]), c_outs


def _head_masks():
    lane_head = lax.broadcasted_iota(jnp.int32, (BLK, GA), 1) // HD
    return [lane_head == hd for hd in range(NH)]


def _expand_heads(t, hm):
    return jnp.concatenate([jnp.where(m, t, jnp.zeros_like(t)) for m in hm], axis=0)


def _collapse_heads(tb, hm):
    out = None
    for hd, m in enumerate(hm):
        part = jnp.where(m, tb[hd * BLK:(hd + 1) * BLK, :], 0.0)
        out = part if out is None else out + part
    return out


def _head_rows(t):
    return jnp.concatenate([t[:, hd * HD:hd * HD + 1] for hd in range(NH)], axis=0)


def _band(has_prev):
    a = lax.broadcasted_iota(jnp.int32, (NH * BLK, 2 * BLK), 0) & (BLK - 1)
    c = lax.broadcasted_iota(jnp.int32, (NH * BLK, 2 * BLK), 1)
    return jnp.logical_and(c >= jnp.where(has_prev, a, BLK), c <= a + BLK)


def _attn_fwd(q, k, v, nb, name, comm=None):
    T = q.shape[0]
    nbt = T // BLK

    def block(qv, kcat, vcat, has_prev, hm):
        s = jnp.where(_band(has_prev), _dot_nt(_expand_heads(qv, hm), kcat) * SCALE, NEG)
        mx = jnp.max(s, axis=-1, keepdims=True)
        e = jnp.exp(s - mx)
        l = jnp.sum(e, axis=-1, keepdims=True)
        ob = _dot((e * (1.0 / l)).astype(BF16), vcat)
        return _collapse_heads(ob, hm), _collapse_heads(jnp.broadcast_to(mx + jnp.log(l), (NH * BLK, GA)), hm)

    def body(q_ref, k_ref, kp_ref, v_ref, vp_ref, o_ref, lse_ref):
        b0 = FWD_BLOCKS * pl.program_id(0)
        hm = _head_masks()
        for b in range(FWD_BLOCKS):
            rows = slice(b * BLK, (b + 1) * BLK)
            if b == 0:
                kcat = jnp.concatenate([kp_ref[...], k_ref[rows, :]], axis=0)
                vcat = jnp.concatenate([vp_ref[...], v_ref[rows, :]], axis=0)
            else:
                kcat, vcat = k_ref[(b - 1) * BLK:(b + 1) * BLK, :], v_ref[(b - 1) * BLK:(b + 1) * BLK, :]
            o_ref[rows, :], lse_ref[rows, :] = block(q_ref[rows, :], kcat, vcat, ((b0 + b) & (nb - 1)) != 0, hm)

    cur = pl.BlockSpec((FWD_BLOCKS * BLK, GA), lambda i: (i, 0))
    prev = pl.BlockSpec((BLK, GA), lambda i: (jnp.maximum(FWD_BLOCKS * i - 1, 0), 0))
    return _call(body, name, (nbt // FWD_BLOCKS,), [cur, cur, prev, cur, prev], [cur, cur],
                 [_sds((T, GA), F32), _sds((T, GA), F32)], comm=comm)(q, k, k, v, v)


def _attn_bwd(q, k, v, do, lse, e, nb, name, comm=None):
    T = q.shape[0]
    nbt = T // BLK

    nblk = BWD_BLOCKS

    def probs_and_ds(qb, dob, kcat, vcat, lsev, ev, valid):
        p = jnp.where(valid, jnp.exp(_dot_nt(qb, kcat) * SCALE - _head_rows(lsev)), 0.0)
        return p.astype(BF16), (p * (_dot_nt(dob, vcat) + _head_rows(ev))).astype(BF16)

    def body(q_ref, k_ref, v_ref, do_ref, lse_ref, e_ref, kp_ref, vp_ref, qn_ref, don_ref, lsen_ref, en_ref,
             dq_ref, dk_ref, dv_ref):
        b0 = nblk * pl.program_id(0)
        hm = _head_masks()
        rows = [slice(b * BLK, (b + 1) * BLK) for b in range(nblk)]
        qs = [_expand_heads(q_ref[r, :], hm) for r in rows] + [_expand_heads(qn_ref[...], hm)]
        dos = [_expand_heads(do_ref[r, :], hm) for r in rows] + [_expand_heads(don_ref[...], hm)]
        ps, dss = [], []
        for b, r in enumerate(rows):
            if b == 0:
                kcat = jnp.concatenate([kp_ref[...], k_ref[r, :]], axis=0)
                vcat = jnp.concatenate([vp_ref[...], v_ref[r, :]], axis=0)
            else:
                kcat, vcat = k_ref[(b - 1) * BLK:(b + 1) * BLK, :], v_ref[(b - 1) * BLK:(b + 1) * BLK, :]
            p, ds = probs_and_ds(qs[b], dos[b], kcat, vcat, lse_ref[r, :], e_ref[r, :], _band(((b0 + b) & (nb - 1)) != 0))
            dq_ref[r, :] = _collapse_heads(_dot(ds, kcat) * SCALE, hm)
            ps.append(p)
            dss.append(ds)
        a = lax.broadcasted_iota(jnp.int32, (NH * BLK, BLK), 0) & (BLK - 1)
        c = lax.broadcasted_iota(jnp.int32, (NH * BLK, BLK), 1)
        valid_n = jnp.logical_and(c >= a, ((b0 + nblk) & (nb - 1)) != 0)
        p_n, ds_n = probs_and_ds(qs[nblk], dos[nblk], k_ref[rows[-1], :], v_ref[rows[-1], :], lsen_ref[...], en_ref[...], valid_n)
        for b, r in enumerate(rows):
            ds_after = dss[b + 1][:, :BLK] if b + 1 < nblk else ds_n
            p_after = ps[b + 1][:, :BLK] if b + 1 < nblk else p_n
            q_pair = jnp.concatenate([qs[b], qs[b + 1]], axis=0)
            do_pair = jnp.concatenate([dos[b], dos[b + 1]], axis=0)
            dk_ref[r, :] = _dot_tn(jnp.concatenate([dss[b][:, BLK:], ds_after], axis=0), q_pair) * SCALE
            dv_ref[r, :] = _dot_tn(jnp.concatenate([ps[b][:, BLK:], p_after], axis=0), do_pair).astype(BF16)

    cur = pl.BlockSpec((nblk * BLK, GA), lambda i: (i, 0))
    prev = pl.BlockSpec((BLK, GA), lambda i: (jnp.maximum(nblk * i - 1, 0), 0))
    nxt = pl.BlockSpec((BLK, GA), lambda i: (jnp.minimum(nblk * i + nblk, nbt - 1), 0))
    return _call(body, name, (nbt // nblk,), [cur] * 6 + [prev, prev] + [nxt] * 4, [cur, cur, cur],
                 [_sds((T, GA), F32), _sds((T, GA), F32), _sds((T, GA), BF16)],
                 comm=comm)(q, k, v, do, lse, e, k, v, q, do, lse, e)


def _flat(t):
    return t.reshape(t.shape[0] * t.shape[1], t.shape[2])


def _by_residue(t, dil):
    return t.reshape(dil, t.shape[0] // dil, t.shape[1])


def _pool_consts(shape, row0):
    lane = lax.broadcasted_iota(jnp.int32, shape, 1)
    t = lax.broadcasted_iota(jnp.int32, shape, 0) + row0
    grp = lane // (PW // len(POOL_WINDOWS))
    win = jnp.where(grp == 0, POOL_WINDOWS[0], jnp.where(grp == 1, POOL_WINDOWS[1],
                    jnp.where(grp == 2, POOL_WINDOWS[2], POOL_WINDOWS[3])))
    cnt = jnp.minimum(t + 1, win).astype(F32)
    return grp, cnt


def _window_sums(ext_ref, base, step, tm):
    outs, run = [], None
    for j in range(POOL_WINDOWS[-1]):
        sl = ext_ref[pl.ds(base + step * j, tm), :]
        run = sl if run is None else run + sl
        if j + 1 in POOL_WINDOWS:
            outs.append(run)
    return outs


def _select_group(grp, vals):
    return jnp.where(grp == 0, vals[0], jnp.where(grp == 1, vals[1], jnp.where(grp == 2, vals[2], vals[3])))


def _pool_d(pc_ref, pp_ref, ext_ref, i, tm):
    ext_ref[0:HALO, :] = jnp.where(i > 0, pp_ref[tm - HALO:tm, :], 0.0)
    ext_ref[HALO:HALO + tm, :] = pc_ref[...]
    grp, cnt = _pool_consts((tm, PW), i * tm)
    sums = _window_sums(ext_ref, HALO, -1, tm)
    return _select_group(grp, sums) / cnt - pc_ref[...]


def _group_weights(ls):
    mx = jnp.maximum(jnp.maximum(ls[0], ls[1]), ls[2])
    es = [jnp.exp(l - mx) for l in ls]
    inv = 1.0 / (es[0] + es[1] + es[2])
    return [e * inv for e in es]


def _mix_merge(h, vec, p, os, lses, gates, wp_bd, pscale, wpb, wab, wout, comm=None):
    T = h.shape[0]

    def body(h_ref, vec_ref, pc_ref, pp_ref, o0, o1, o2, l0, l1, l2, gates_ref, wp_ref, ps_ref, wpb_ref, wab_ref, wout_ref,
             ho_ref, yp_ref, ya_ref, mg_ref, mo_ref, d_ref, ext_ref, scr_ref):
        i = pl.program_id(0)
        gt = vec_ref[3:4, :]
        d = _pool_d(pc_ref, pp_ref, ext_ref, i, TM).astype(BF16)
        d_ref[...] = d
        ypool = (_dot(d, wp_ref[...]) * ps_ref[0:1, :]).astype(BF16)
        yp_ref[...] = ypool
        w = _group_weights([_from_residues(r, scr_ref, dl) for r, dl in zip((l0, l1, l2), DIL)])
        yattn = None
        for wg, o_ref, dl in zip(w, (o0, o1, o2), DIL):
            part = wg * _from_residues(o_ref, scr_ref, dl)
            yattn = part if yattn is None else yattn + part
        yattn = yattn.astype(BF16)
        ya_ref[...] = yattn
        merged = (gates_ref[:, 0:D].astype(F32) * _dot(ypool, wpb_ref[...])
                  + gates_ref[:, D:GW].astype(F32) * _dot(yattn, wab_ref[...])).astype(BF16)
        mg_ref[...] = merged
        mo = _dot(merged, wout_ref[...])
        mo_ref[...] = mo.astype(BF16)
        ho_ref[...] = h_ref[...] + gt * mo

    prev = pl.BlockSpec((TM, PW), lambda i: (jnp.maximum(i - 1, 0), 0))
    return _call(
        body, "mix_merge", (T // TM,),
        [_rows(TM, D), _const((8, D)), _rows(TM, PW), prev] + [_rm_spec(dl) for dl in DIL] * 2 + [_rows(TM, GW), _const((PW, PW)),
         _const((8, PW)), _const((PW, D)), _const((GA, D)), _const((D, D))],
        [_rows(TM, D), _rows(TM, PW), _rows(TM, GA), _rows(TM, D), _rows(TM, D), _rows(TM, PW)],
        [_sds((T, D), F32), _sds((T, PW), BF16), _sds((T, GA), BF16), _sds((T, D), BF16), _sds((T, D), BF16), _sds((T, PW), BF16)],
        scratch=[pltpu.VMEM((TM + HALO, PW), F32), pltpu.VMEM((GA // LANES, TM, LANES), F32)],
        vmem=VMEM_BIG, comm=comm,
    )(h, vec, p, p, *os, *lses, gates, wp_bd, pscale, wpb, wab, wout)


def _mix_bwd_a(dh, vec, mixout, merged, gates, ypool, yattn, dpool, os, lses, wp_bd, pscale, wpb, wab, wout, ones_bd,
               comm=None):
    T = dh.shape[0]
    nt = T // TM

    def body(dh_ref, vec_ref, mo_ref, mg_ref, gates_ref, yp_ref, ya_ref, d_ref, o0, o1, o2, l0, l1, l2,
             wp_ref, ps_ref, wpb_ref, wab_ref, wout_ref, ones_ref,
             dgates_ref, do0, do1, do2, e0, e1, e2, dd_ref, acc_ref, acc2_ref, g_out_ref, g_pb_ref, g_ab_ref, g_pool_ref,
             scr_ref, a_out, a_pb, a_ab, a_pool):
        _zero_first(acc_ref)
        _zero_first(acc2_ref)
        for a_ref in (a_out, a_pb, a_ab, a_pool):
            _zero_first(a_ref)
        gt = vec_ref[3:4, :]
        dho = dh_ref[...]
        acc_ref[3:4, :] += _colsum(dho * mo_ref[...].astype(F32))
        dmo = (gt * dho).astype(BF16)
        a_out[...] += _dot_tn(mg_ref[...], dmo)
        dmerged = _dot_nt(dmo, wout_ref[...])
        gp = gates_ref[:, 0:D].astype(F32)
        ga = gates_ref[:, D:GW].astype(F32)
        bp = _dot(yp_ref[...], wpb_ref[...])
        ba = _dot(ya_ref[...], wab_ref[...])
        dgates_ref[:, 0:D] = (dmerged * bp * gp * (1.0 - gp)).astype(BF16)
        dgates_ref[:, D:GW] = (dmerged * ba * ga * (1.0 - ga)).astype(BF16)
        dbp = (dmerged * gp).astype(BF16)
        dba = (dmerged * ga).astype(BF16)
        a_pb[...] += _dot_tn(yp_ref[...], dbp)
        a_ab[...] += _dot_tn(ya_ref[...], dba)
        dypool = _dot_nt(dbp, wpb_ref[...])
        ypre = _dot(d_ref[...], wp_ref[...])
        acc2_ref[0:1, :] += _colsum(dypool * ypre)
        dyp = (dypool * ps_ref[0:1, :]).astype(BF16)
        a_pool[...] += _dot_tn(d_ref[...], dyp)
        dd_ref[...] = _dot_nt(dyp, wp_ref[...])
        dya = _dot_nt(dba, wab_ref[...])
        w = _group_weights([_from_residues(r, scr_ref, dl) for r, dl in zip((l0, l1, l2), DIL)])
        ya = None
        for wg, o_ref, dl in zip(w, (o0, o1, o2), DIL):
            part = wg * _from_residues(o_ref, scr_ref, dl)
            ya = part if ya is None else ya + part
        prod = dya * ya
        hi = prod.astype(BF16)
        lo = (prod - hi.astype(F32)).astype(BF16)
        tot = _dot(hi, ones_ref[...]) + _dot(lo, ones_ref[...])
        for wg, do_ref, e_ref, dl in zip(w, (do0, do1, do2), (e0, e1, e2), DIL):
            _to_residues(wg * dya, do_ref, scr_ref, dl)
            _to_residues(-wg * tot, e_ref, scr_ref, dl)

        @pl.when(pl.program_id(0) == nt - 1)
        def _():
            g_out_ref[...] = a_out[...].astype(BF16)
            g_pb_ref[...] = a_pb[...].astype(BF16)
            g_ab_ref[...] = a_ab[...].astype(BF16)
            g_pool_ref[...] = a_pool[...]

    return _call(
        body, "mix_bwd_a", (nt,),
        [_rows(TM, D), _const((8, D)), _rows(TM, D), _rows(TM, D), _rows(TM, GW), _rows(TM, PW), _rows(TM, GA), _rows(TM, PW)]
        + [_rm_spec(dl) for dl in DIL] * 2
        + [_const((PW, PW)), _const((8, PW)), _const((PW, D)), _const((GA, D)), _const((D, D)), _const((GA, GA))],
        [_rows(TM, GW)] + [_rm_spec(dl) for dl in DIL] * 2 + [_rows(TM, PW), _const((8, D)), _const((8, PW))]
        + [_const((D, D)), _const((PW, D)), _const((GA, D)), _const((PW, PW))],
        [_sds((T, GW), BF16)] + [_sds((dl, T // dl, GA), BF16) for dl in DIL]
        + [_sds((dl, T // dl, GA), F32) for dl in DIL] + [_sds((T, PW), F32), _sds((8, D), F32), _sds((8, PW), F32)]
        + [_sds((D, D), BF16), _sds((PW, D), BF16), _sds((GA, D), BF16), _sds((PW, PW), F32)],
        scratch=[pltpu.VMEM((GA // LANES, TM, LANES), F32), pltpu.VMEM((D, D), F32), pltpu.VMEM((PW, D), F32),
                 pltpu.VMEM((GA, D), F32), pltpu.VMEM((PW, PW), F32)],
        vmem=VMEM_BIG, comm=comm,
    )(dh, vec, mixout, merged, gates, ypool, yattn, dpool, *os, *lses, wp_bd, pscale, wpb, wab, wout, ones_bd)


def _mix_bwd_b(dh, h, vec, dd, dqs, dks, dvs, dgates, cos, sin, win):
    T = h.shape[0]
    nt = T // TM

    def body(dh_ref, h_ref, vec_ref, ddc_ref, ddn_ref, *rest):
        qk_refs, dv_refs = rest[:2 * NG], rest[2 * NG:3 * NG]
        dgates_ref, cos_ref, sin_ref, win_hbm, dhi_ref, dproj_ref, acc_ref, win_v, ext_ref, scr_ref, sems = rest[3 * NG:]
        i = pl.program_id(0)
        _load_once([(win_hbm, win_v)], sems)
        _zero_first(acc_ref)
        g, sh, sc = vec_ref[0:1, :], vec_ref[1:2, :], vec_ref[2:3, :]
        grp, cnt = _pool_consts((TM, PW), i * TM)
        _, cnt_n = _pool_consts((HALO, PW), (i + 1) * TM)
        ext_ref[0:TM, :] = ddc_ref[...] / cnt
        ext_ref[TM:TM + HALO, :] = jnp.where(i < nt - 1, ddn_ref[0:HALO, :] / cnt_n, 0.0)
        dp = _select_group(grp, _window_sums(ext_ref, 0, 1, TM)) - ddc_ref[...]
        dproj_ref[:, 0:PW] = dp.astype(BF16)
        cos_t, sin_t = cos_ref[...], sin_ref[...]
        for j in range(2 * NG):
            col = PW + j * GA
            dt = _from_residues(qk_refs[j], scr_ref, DIL[j % NG])
            dproj_ref[:, col:col + GA] = _rope_bwd(dt, cos_t, sin_t).astype(BF16)
        for j in range(NG):
            col = PW + (2 * NG + j) * GA
            dproj_ref[:, col:col + GA] = _from_residues(dv_refs[j], scr_ref, DIL[j]).astype(BF16)
        dproj_ref[:, PW + 3 * NG * GA:INW] = dgates_ref[...]
        du = None
        for j in range(INW // 512):
            part = _dot_nt(dproj_ref[:, j * 512:(j + 1) * 512], win_v[:, j * 512:(j + 1) * 512])
            du = part if du is None else du + part
        xh, r, n, _ = _norm_fwd(h_ref[...], g, sh, sc)
        dhn, dsh, dsc, dg = _norm_bwd(du, xh, r, n, g, sc)
        dhi_ref[...] = dh_ref[...] + dhn
        acc_ref[0:1, :] += dsh
        acc_ref[1:2, :] += dsc
        acc_ref[2:3, :] += dg

    nxt = pl.BlockSpec((TM, PW), lambda i: (jnp.minimum(i + 1, nt - 1), 0))
    return _call(
        body, "mix_bwd_b", (nt,),
        [_rows(TM, D), _rows(TM, D), _const((8, D)), _rows(TM, PW), nxt] + [_rm_spec(dl) for dl in DIL] * 3
        + [_rows(TM, GW), _rows(TM, 128), _rows(TM, 128), ANY],
        [_rows(TM, D), _rows(TM, INW), _const((8, D))],
        [_sds((T, D), F32), _sds((T, INW), BF16), _sds((8, D), F32)],
        scratch=[pltpu.VMEM((D, INW), BF16), pltpu.VMEM((TM + HALO, PW), F32), pltpu.VMEM((GA // LANES, TM, LANES), F32),
                 pltpu.SemaphoreType.DMA((1,))],
        vmem=VMEM_BIG,
    )(dh, h, vec, dd, dd, *dqs, *dks, *dvs, dgates, cos, sin, win)[0]


def _ada_fwd(c_all, w_shard, b_shard):
    n = w_shard.shape[1]

    def body(c_ref, w_ref, b_ref, o_ref):
        cv = c_ref[...]
        cond = (cv * jax.nn.sigmoid(cv)).astype(BF16)
        o_ref[...] = _dot(cond, w_ref[...].astype(BF16)) + b_ref[...]

    tn = n // 3
    return pl.pallas_call(
        body, name="ada_fwd", grid=(3,),
        in_specs=[pl.BlockSpec((8, D), lambda j: (0, 0)), pl.BlockSpec((D, tn), lambda j: (0, j)), pl.BlockSpec((1, tn), lambda j: (0, j))],
        out_specs=pl.BlockSpec((8, tn), lambda j: (0, j)), out_shape=_sds((8, n), F32),
        compiler_params=pltpu.CompilerParams(dimension_semantics=("arbitrary",)),
    )(c_all, w_shard, b_shard)


def _ada_bwd(c_all, dmod_shard):
    n = dmod_shard.shape[1]

    def body(c_ref, d_ref, o_ref):
        cv = c_ref[...]
        cond = (cv * jax.nn.sigmoid(cv)).astype(BF16)
        o_ref[...] = _dot_tn(cond, d_ref[...].astype(BF16))

    tn = n // 3
    return pl.pallas_call(
        body, name="ada_bwd", grid=(3,),
        in_specs=[pl.BlockSpec((8, D), lambda j: (0, 0)), pl.BlockSpec((8, tn), lambda j: (0, j))],
        out_specs=pl.BlockSpec((D, tn), lambda j: (0, j)), out_shape=_sds((D, n), F32),
        compiler_params=pltpu.CompilerParams(dimension_semantics=("arbitrary",)),
    )(c_all, dmod_shard)


def _adam_math(w, g, m, v):
    m2 = B1 * m + (1.0 - B1) * g
    v2 = B2 * v + (1.0 - B2) * (g * g)
    m_hat = m2 / (1.0 - B1 ** STEP)
    v_hat = v2 / (1.0 - B2 ** STEP)
    delta = -LR * (m_hat / (jnp.sqrt(v_hat) + AEPS) + WD * w)
    return delta, m2, v2


def _adam(w, m, v, parts, name, comm=None):
    R, C = w.shape
    tr = R
    for cand in (128, 64, 32, 16, 8):
        if R % cand == 0:
            tr = cand
            break
    np_ = len(parts)

    def body(w_ref, m_ref, v_ref, *rest):
        p_refs, (g_ref, d_ref, m2_ref, v2_ref) = rest[:np_], rest[np_:]
        g = p_refs[0][...]
        for pr in p_refs[1:]:
            g = g + pr[...]
        delta, m2, v2 = _adam_math(w_ref[...], g, m_ref[...], v_ref[...])
        g_ref[...] = g
        d_ref[...] = delta
        m2_ref[...] = m2
        v2_ref[...] = v2

    spec = pl.BlockSpec((tr, C), lambda i: (i, 0))
    return _call(body, name, (R // tr,), [spec] * (3 + np_), [spec] * 4, [_sds((R, C), F32)] * 4,
                 vmem=VMEM_BIG, comm=comm)(w, m, v, *parts)


def _adam_halves(w, m, v, mine, other, name):
    R, C = w.shape
    tr = 128
    nh = R // 2 // tr

    def body(c_ref, w_ref, m_ref, v_ref, mine_ref, other_ref, g_ref, d_ref, m2_ref, v2_ref):
        i = pl.program_id(0)
        in_mine = jnp.logical_and(i >= c_ref[0] * nh, i < (c_ref[0] + 1) * nh)
        g = jnp.where(in_mine, mine_ref[...], other_ref[...])
        delta, m2, v2 = _adam_math(w_ref[...], g, m_ref[...], v_ref[...])
        g_ref[...] = g
        d_ref[...] = delta
        m2_ref[...] = m2
        v2_ref[...] = v2

    spec = pl.BlockSpec((tr, C), lambda i, c: (i, 0))
    grid_spec = pltpu.PrefetchScalarGridSpec(
        num_scalar_prefetch=1, grid=(R // tr,),
        in_specs=[spec] * 3 + [pl.BlockSpec((tr, C), lambda i, c: (jnp.clip(i - c[0] * nh, 0, nh - 1), 0)),
                               pl.BlockSpec((tr, C), lambda i, c: (jnp.clip(i - (1 - c[0]) * nh, 0, nh - 1), 0))],
        out_specs=[spec] * 4)
    return pl.pallas_call(
        body, name=name, grid_spec=grid_spec, out_shape=[_sds((R, C), F32)] * 4,
        compiler_params=pltpu.CompilerParams(dimension_semantics=("arbitrary",), vmem_limit_bytes=VMEM_BIG),
    )(lax.axis_index("c").astype(jnp.int32).reshape(1), w, m, v, mine, other)


def _adam_small(ws, ms, vs, gathered):
    n = len(ws)
    sizes = [a.shape[1] for a in ws]

    def total(ga_ref, off, size):
        g = ga_ref[0, :, off:off + size]
        for dev in range(1, 8):
            g = g + ga_ref[dev, :, off:off + size]
        return g

    def body(*refs):
        w_refs, m_refs, v_refs, ga_ref, outs = refs[:n], refs[n:2 * n], refs[2 * n:3 * n], refs[3 * n], refs[3 * n + 1:]
        off = 0
        for j, size in enumerate(sizes):
            g = total(ga_ref, off, size)
            delta, m2, v2 = _adam_math(w_refs[j][...], g, m_refs[j][...], v_refs[j][...])
            for ref, val in zip(outs[4 * j:4 * j + 4], (g, delta, m2, v2)):
                ref[...] = val
            off += size
        outs[4 * n][...] = total(ga_ref, off, 128)

    res = pl.pallas_call(
        body, name="adam_small",
        out_shape=[_sds((1, size), F32) for size in sizes for _ in range(4)] + [_sds((1, 128), F32)],
    )(*ws, *ms, *vs, gathered)
    return [res[4 * j:4 * j + 4] for j in range(n)], res[4 * n]


def _sum4(blocks, name):
    _, R, C = blocks.shape
    tr = R
    for cand in (256, 128, 64, 32, 16):
        if R % cand == 0:
            tr = cand
            break

    def body(r_ref, out_ref):
        out_ref[...] = ((r_ref[0].astype(F32) + r_ref[1].astype(F32)) + r_ref[2].astype(F32)) + r_ref[3].astype(F32)

    return pl.pallas_call(
        body, name=name, grid=(R // tr,),
        in_specs=[pl.BlockSpec((4, tr, C), lambda i: (0, i, 0))],
        out_specs=pl.BlockSpec((tr, C), lambda i: (i, 0)), out_shape=_sds((R, C), F32),
        compiler_params=pltpu.CompilerParams(dimension_semantics=("arbitrary",)),
    )(blocks)


def _place():
    return lax.axis_index("x"), lax.axis_index("y"), lax.axis_index("c")


def _chip_peer(x, y, c, m):
    return (x ^ (m >> 1), y ^ (m & 1), c)


def _shard_ref(ref, axis, k, n):
    start = pl.multiple_of(k * n, 128 if axis == 1 else 16)
    return ref.at[:, pl.ds(start, n)] if axis == 1 else ref.at[pl.ds(start, n), :]


def _half_rows(ref, axis, k, n, hc):
    if axis == 1:
        half = ref.shape[0] // 2
        return ref.at[pl.ds(pl.multiple_of(hc * half, 16), half), pl.ds(pl.multiple_of(k * n, 128), n)]
    half = n // 2
    return ref.at[pl.ds(pl.multiple_of(k * n + hc * half, 16), half), :]


class _GatherPlan:
    def __init__(self, shards, axes):
        self.inputs, self.axes, nw = list(shards), list(axes), len(shards)
        self.out_shapes = [_sds((s.shape[0] * (4 if ax == 0 else 1), s.shape[1] * (4 if ax == 1 else 1)), BF16)
                           for s, ax in zip(shards, axes)]
        self.sem_shapes = [pltpu.SemaphoreType.DMA((nw,))] + [pltpu.SemaphoreType.DMA((nw, 3))] * 4

    def _copies(self, ins, outs, sems):
        local_sems, send_sems, recv_sems, pass_sems, got_sems = sems
        x, y, c = _place()
        k = 2 * x + y
        local, sends, arrivals, passes, handed = [], [], [], [], []
        for j, ax in enumerate(self.axes):
            n = ins[j].shape[ax]
            half = ins[j].shape[0] // 2
            local.append(pltpu.make_async_copy(ins[j], _shard_ref(outs[j], ax, k, n), local_sems.at[j]))
            my_half = ins[j].at[pl.ds(pl.multiple_of(c * half, 16), half), :]
            for m in range(1, 4):
                sends.append(pltpu.make_async_remote_copy(
                    src_ref=my_half, dst_ref=_half_rows(outs[j], ax, k, n, c), send_sem=send_sems.at[j, m - 1],
                    recv_sem=recv_sems.at[j, m - 1], device_id=_chip_peer(x, y, c, m), device_id_type=MESH))
                theirs = _half_rows(outs[j], ax, k ^ m, n, c)
                arrivals.append(pltpu.make_async_remote_copy(
                    src_ref=my_half, dst_ref=theirs, send_sem=send_sems.at[j, m - 1], recv_sem=recv_sems.at[j, m - 1],
                    device_id=(x, y, c), device_id_type=MESH))
                passes.append(pltpu.make_async_remote_copy(
                    src_ref=theirs, dst_ref=theirs, send_sem=pass_sems.at[j, m - 1], recv_sem=got_sems.at[j, m - 1],
                    device_id=(x, y, 1 - c), device_id_type=MESH))
                other = _half_rows(outs[j], ax, k ^ m, n, 1 - c)
                handed.append(pltpu.make_async_remote_copy(
                    src_ref=other, dst_ref=other, send_sem=pass_sems.at[j, m - 1], recv_sem=got_sems.at[j, m - 1],
                    device_id=(x, y, c), device_id_type=MESH))
        return local, sends, arrivals, passes, handed

    def start(self, ins, outs, sems):
        local, sends, _, _, _ = self._copies(ins, outs, sems)
        for cp in local + sends:
            cp.start()

    def relay(self, ins, outs, sems):
        _, _, arrivals, passes, _ = self._copies(ins, outs, sems)
        for arrived, onward in zip(arrivals, passes):
            arrived.wait_recv()
            onward.start()

    def wait(self, ins, outs, sems):
        local, sends, _, passes, handed = self._copies(ins, outs, sems)
        for cp in handed:
            cp.wait_recv()
        for cp in sends + passes:
            cp.wait_send()
        for cp in local:
            cp.wait()


class _ScatterPlan:
    def __init__(self, grads, axes):
        self.inputs, self.axes, nw = list(grads), list(axes), len(grads)
        self.shard_shapes = [(g.shape[0] // (4 if ax == 0 else 1), g.shape[1] // (4 if ax == 1 else 1))
                             for g, ax in zip(grads, axes)]
        self.out_shapes = [_sds((4,) + s, BF16) for s in self.shard_shapes]
        self.sem_shapes = [pltpu.SemaphoreType.DMA((nw,)), pltpu.SemaphoreType.DMA((nw, 3)), pltpu.SemaphoreType.DMA((nw, 3))]

    def _copies(self, ins, outs, sems):
        local_sems, send_sems, recv_sems = sems
        x, y, c = _place()
        k = 2 * x + y
        local, remote, arrivals = [], [], []
        for j, ax in enumerate(self.axes):
            n = self.shard_shapes[j][ax]
            local.append(pltpu.make_async_copy(_shard_ref(ins[j], ax, k, n), outs[j].at[0], local_sems.at[j]))
            for m in range(1, 4):
                remote.append(pltpu.make_async_remote_copy(
                    src_ref=_shard_ref(ins[j], ax, k ^ m, n), dst_ref=outs[j].at[m],
                    send_sem=send_sems.at[j, m - 1], recv_sem=recv_sems.at[j, m - 1],
                    device_id=_chip_peer(x, y, c, m), device_id_type=MESH))
                arrivals.append(pltpu.make_async_remote_copy(
                    src_ref=_shard_ref(ins[j], ax, k, n), dst_ref=outs[j].at[m],
                    send_sem=send_sems.at[j, m - 1], recv_sem=recv_sems.at[j, m - 1],
                    device_id=(x, y, c), device_id_type=MESH))
        return local, remote, arrivals

    def start(self, ins, outs, sems):
        local, remote, _ = self._copies(ins, outs, sems)
        for cp in local + remote:
            cp.start()

    def relay(self, ins, outs, sems):
        pass

    def wait(self, ins, outs, sems):
        local, remote, arrivals = self._copies(ins, outs, sems)
        for cp in arrivals:
            cp.wait_recv()
        for cp in remote:
            cp.wait_send()
        for cp in local:
            cp.wait()


def _run_plan(plan, name):
    nc = len(plan.inputs)

    def body(*refs):
        ins, outs, sems = refs[:nc], refs[nc:2 * nc], refs[2 * nc:]
        plan.start(ins, outs, sems)
        plan.relay(ins, outs, sems)
        plan.wait(ins, outs, sems)

    return pl.pallas_call(body, name=name, in_specs=[ANY] * nc, out_specs=[ANY] * nc, out_shape=list(plan.out_shapes),
                          scratch_shapes=list(plan.sem_shapes))(*plan.inputs)


class _SwapPlan:
    def __init__(self, parts):
        self.inputs, nw = list(parts), len(parts)
        self.out_shapes = [_sds(p.shape, p.dtype) for p in parts]
        self.sem_shapes = [pltpu.SemaphoreType.DMA((nw,)), pltpu.SemaphoreType.DMA((nw,))]

    def _copies(self, ins, outs, sems):
        send_sems, recv_sems = sems
        x, y, c = _place()
        return [pltpu.make_async_remote_copy(
            src_ref=ins[j], dst_ref=outs[j], send_sem=send_sems.at[j], recv_sem=recv_sems.at[j],
            device_id=(x, y, 1 - c), device_id_type=MESH) for j in range(len(ins))]

    def start(self, ins, outs, sems):
        for cp in self._copies(ins, outs, sems):
            cp.start()

    def relay(self, ins, outs, sems):
        pass

    def wait(self, ins, outs, sems):
        for cp in self._copies(ins, outs, sems):
            cp.wait()


class _SmallGatherPlan:
    def __init__(self, v):
        self.inputs = [v]
        self.out_shapes = [_sds((8,) + v.shape, v.dtype)]
        self.sem_shapes = [pltpu.SemaphoreType.DMA((1,)), pltpu.SemaphoreType.DMA((7,)), pltpu.SemaphoreType.DMA((7,))]

    def _copies(self, ins, outs, sems):
        (v_ref,), (out_ref,), (local_sem, send_sems, recv_sems) = ins, outs, sems
        x, y, c = _place()
        me = 4 * x + 2 * y + c
        local = pltpu.make_async_copy(v_ref, out_ref.at[me], local_sem.at[0])
        sends, arrivals = [], []
        for m in range(1, 8):
            px, py, pc = x ^ (m >> 2), y ^ ((m >> 1) & 1), c ^ (m & 1)
            sends.append(pltpu.make_async_remote_copy(
                src_ref=v_ref, dst_ref=out_ref.at[me], send_sem=send_sems.at[m - 1], recv_sem=recv_sems.at[m - 1],
                device_id=(px, py, pc), device_id_type=MESH))
            arrivals.append(pltpu.make_async_remote_copy(
                src_ref=v_ref, dst_ref=out_ref.at[4 * px + 2 * py + pc], send_sem=send_sems.at[m - 1],
                recv_sem=recv_sems.at[m - 1], device_id=(x, y, c), device_id_type=MESH))
        return local, sends, arrivals

    def start(self, ins, outs, sems):
        local, sends, _ = self._copies(ins, outs, sems)
        for cp in [local] + sends:
            cp.start()

    def relay(self, ins, outs, sems):
        pass

    def wait(self, ins, outs, sems):
        local, sends, arrivals = self._copies(ins, outs, sems)
        for cp in arrivals:
            cp.wait_recv()
        for cp in sends:
            cp.wait_send()
        local.wait()


class _PlanGroup:
    def __init__(self, plans):
        self.plans = [p for p in plans if p is not None]
        self.inputs = [a for p in self.plans for a in p.inputs]
        self.out_shapes = [s for p in self.plans for s in p.out_shapes]
        self.sem_shapes = [s for p in self.plans for s in p.sem_shapes]

    def _each(self, ins, outs, sems):
        i = s = 0
        for p in self.plans:
            n, ns = len(p.inputs), len(p.sem_shapes)
            yield p, ins[i:i + n], outs[i:i + n], sems[s:s + ns]
            i, s = i + n, s + ns

    def start(self, ins, outs, sems):
        for p, pi, po, ps in self._each(ins, outs, sems):
            p.start(pi, po, ps)

    def relay(self, ins, outs, sems):
        for p, pi, po, ps in self._each(ins, outs, sems):
            p.relay(pi, po, ps)

    def wait(self, ins, outs, sems):
        for p, pi, po, ps in self._each(ins, outs, sems):
            p.wait(pi, po, ps)

    def split(self, outs):
        res, i = [], 0
        for p in self.plans:
            res.append(outs[i:i + len(p.inputs)])
            i += len(p.inputs)
        return res


BIG = ("w_ffn1_in", "w_ffn1_out", "w_in", "w_pool_branch", "w_attn_branch", "w_out", "w_ffn2_in", "w_ffn2_out")
BIG_AXIS = {"w_ffn1_in": 1, "w_ffn1_out": 0, "w_in": 1, "w_pool_branch": 1, "w_attn_branch": 1, "w_out": 0,
            "w_ffn2_in": 1, "w_ffn2_out": 0}


class _Sharded:
    fused_scatter = True

    def __init__(self, shards):
        self.shards, self.full, self.recv = shards, {}, {}

    def gather_plan(self, names):
        return _GatherPlan([self.shards[n] for n in names], [BIG_AXIS[n.split("/")[0]] for n in names])

    def gather_now(self, names):
        self.gathered(names, _run_plan(self.gather_plan(names), "gather_" + names[0]))

    def gathered(self, names, outs):
        self.full.update(zip(names, outs))

    def scatter_plan(self, names, grads):
        return _ScatterPlan([grads[n] for n in names], [BIG_AXIS[n] for n in names])

    def scatter_now(self, names, grads):
        self.scattered(names, _run_plan(self.scatter_plan(names, grads), "scatter_" + names[0]))

    def scattered(self, names, outs):
        self.recv.update(zip(names, outs))


class _Whole:
    fused_scatter = False

    def __init__(self, full):
        self.full, self.recv = dict(full), {}

    def gather_plan(self, names):
        return None

    def gather_now(self, names):
        pass

    def gathered(self, names, outs):
        pass

    def scatter_plan(self, names, grads):
        return None

    def scatter_now(self, names, grads):
        pass

    def scattered(self, names, outs):
        pass


def _vec(rows):
    pad = [jnp.zeros((1, D), F32)] * (8 - len(rows))
    return jnp.concatenate([r.reshape(1, D) for r in rows] + pad, axis=0)


def _block_diag(w_pool):
    n, c = w_pool.shape[0], w_pool.shape[1]
    eye = jnp.eye(n, dtype=w_pool.dtype)
    return (eye[:, None, :, None] * w_pool[:, :, None, :]).reshape(n * c, n * c)


def _example_step(x, tgt, positions, mod, gains, w_pool, pool_scale, ws, pack=None):
    T = x.shape[0]
    assert (T // BLK // DIL[-1]) & (T // BLK // DIL[-1] - 1) == 0, "blocks per sequence must be a power of two"
    sh1, sc1, gt1, sh2, sc2, gt2, sh3, sc3, gt3 = [mod[j * D:(j + 1) * D] for j in range(NMOD)]
    g1, g2, g3, gf = gains
    vec1, vec2, vec3 = _vec([g1, sh1, sc1, gt1]), _vec([g2, sh2, sc2, gt2]), _vec([g3, sh3, sc3, gt3])
    inv_freq = 10000.0 ** (-jnp.arange(0, HD, 2, dtype=F32) / HD)
    ang = positions.astype(F32)[:, None] * inv_freq
    cos = jnp.tile(jnp.cos(ang), (1, 4))
    sin = jnp.tile(jnp.concatenate([-jnp.sin(ang), jnp.sin(ang)], axis=1), (1, 2))
    wp_bd = _block_diag(w_pool).astype(BF16)
    ones_bd = _block_diag(jnp.ones((NH, HD, HD), F32)).astype(BF16)
    ps = jnp.concatenate([pool_scale.reshape(1, PW), jnp.zeros((7, PW), F32)], axis=0)
    wb = ws.full

    if "w_ffn1_in" not in wb:
        ws.gather_now(["w_ffn1_in"])
    (u1, a1, b1), got = _ffn_ab(x, vec1, [wb["w_ffn1_in"]], "ffn1_ab", ws.gather_plan(["w_ffn1_out", "w_in"]))
    ws.gathered(["w_ffn1_out", "w_in"], got)
    mixw = ["w_pool_branch", "w_attn_branch", "w_out"]
    (h1, f1), got = _ffn_out(x, a1, b1, vec1, wb["w_ffn1_out"], "ffn1_out", ws.gather_plan(mixw))
    ws.gathered(mixw, got)
    (u2, p, qs, ks, vs, gates), got = _mix_proj(h1, vec2, wb["w_in"], cos, sin, ws.gather_plan(["w_ffn2_in/0"]))
    ws.gathered(["w_ffn2_in/0"], got)
    qs, ks, vs = [_flat(t) for t in qs], [_flat(t) for t in ks], [_flat(t) for t in vs]
    nbs = [T // d // BLK for d in DIL]
    os, lses = [], []
    for gi in range(NG):
        (o, lse), _ = _attn_fwd(qs[gi], ks[gi], vs[gi], nbs[gi], f"attn_fwd{gi}")
        os.append(o)
        lses.append(lse)
    os_r = [_by_residue(t, d) for t, d in zip(os, DIL)]
    lses_r = [_by_residue(t, d) for t, d in zip(lses, DIL)]
    ffn2w = ["w_ffn2_in/1", "w_ffn2_out"]
    (h2, ypool, yattn, merged, mixout, dpool), got = _mix_merge(
        h1, vec2, p, os_r, lses_r, gates, wp_bd, ps, wb["w_pool_branch"], wb["w_attn_branch"], wb["w_out"],
        ws.gather_plan(ffn2w))
    ws.gathered(ffn2w, got)
    win3 = [wb["w_ffn2_in/0"], wb["w_ffn2_in/1"]] if "w_ffn2_in/0" in wb else [wb["w_ffn2_in"]]
    (dh3, u3, a3, b3, f3, lacc), _ = _ffn_fwd(h2, vec3, win3, wb["w_ffn2_out"], "ffn2_fwd", head=(tgt, _vec([gf])))
    loss = 0.5 * jnp.sum(lacc[0]) / D

    grads = {}

    def wgrad_cols(name, xx, yy, riders, extra=None):
        group = _PlanGroup([ws.scatter_plan(riders, grads) if riders else None, extra])
        plan = group if group.plans else None
        if ws.fused_scatter:
            blocks, got = _wgrad_scatter(xx, yy, "wg_" + name, min(2048, T // 2), comm=plan)
            ws.scattered([name], [blocks])
        else:
            grads[name], got = _wgrad(xx, yy, "wg_" + name, D, 512, 1024, comm=plan)
        parts = group.split(got)
        if len(parts) > (extra is not None):
            ws.scattered(riders, parts[0])
        return parts[-1] if extra is not None else None

    (dh2, dab3, s3, df3, acc3), _ = _ffn_bwd(dh3, h2, a3, b3, f3, vec3, win3, wb["w_ffn2_out"], "ffn2_bwd")
    grads["w_ffn2_out"], _ = _wgrad(s3, df3, "wg_ffn2_out", FF // 2, 512, min(4096, T // 2))
    wgrad_cols("w_ffn2_in", u3, dab3, ["w_ffn2_out"])
    (dgates, do0, do1, do2, e0, e1, e2, dd, acc2a, accps,
     grads["w_out"], grads["w_pool_branch"], grads["w_attn_branch"], gwp), _ = _mix_bwd_a(
        dh2, vec2, mixout, merged, gates, ypool, yattn, dpool, os_r, lses_r, wp_bd, ps,
        wb["w_pool_branch"], wb["w_attn_branch"], wb["w_out"], ones_bd)
    n = len(POOL_WINDOWS)
    c = PW // n
    grad_w_pool = jnp.stack([gwp[j * c:(j + 1) * c, j * c:(j + 1) * c] for j in range(n)], axis=0)
    small3 = ["w_out", "w_pool_branch", "w_attn_branch"]
    dqs, dks, dvs = [], [], []
    for gi, (do, e) in enumerate(((do0, e0), (do1, e1), (do2, e2))):
        plan = ws.scatter_plan(small3, grads) if gi == 0 else None
        (dq, dk, dv), got = _attn_bwd(qs[gi], ks[gi], vs[gi], _flat(do), lses[gi], _flat(e), nbs[gi], f"attn_bwd{gi}", plan)
        if gi == 0:
            ws.scattered(small3, got)
        dqs.append(_by_residue(dq, DIL[gi]))
        dks.append(_by_residue(dk, DIL[gi]))
        dvs.append(_by_residue(dv, DIL[gi]))
    dh1, dproj, acc2b = _mix_bwd_b(dh2, h1, vec2, dd, dqs, dks, dvs, dgates, cos, sin, wb["w_in"])
    wgrad_cols("w_in", u2, dproj, [])
    (dx, dab1, s1, df1, acc1), _ = _ffn_bwd(dh1, x, a1, b1, f1, vec1, [wb["w_ffn1_in"]], wb["w_ffn1_out"], "ffn1_bwd")
    grads["w_ffn1_out"], _ = _wgrad(s1, df1, "wg_ffn1_out", FF // 2, 512, min(4096, T // 2))
    dmod = jnp.concatenate([acc1[0], acc1[1], acc1[3], acc2b[0], acc2b[1], acc2a[3], acc3[0], acc3[1], acc3[3]])
    dgains = jnp.stack([acc1[2], acc2b[2], acc3[2], lacc[1]], axis=0)
    row = None if pack is None else _SmallGatherPlan(pack(loss, dmod, dgains, grad_w_pool, accps[0]))
    rows = wgrad_cols("w_ffn1_in", u1, dab1, ["w_ffn1_out"], row)
    return loss, dx, dmod, dgains, grad_w_pool, accps[0], grads, None if rows is None else rows[0]


SMALL = ("b_ada", "g_norm_ffn1", "g_norm_mix", "g_norm_ffn2", "g_final", "pool_scale", "w_pool")
WEIGHTS = ("w_ada", "b_ada", "g_norm_ffn1", "w_ffn1_in", "w_ffn1_out", "g_norm_mix", "w_in", "w_pool", "pool_scale",
           "w_pool_branch", "w_attn_branch", "w_out", "g_norm_ffn2", "w_ffn2_in", "w_ffn2_out", "g_final")


def _pack_small(t):
    return jnp.concatenate([t[n].reshape(-1) for n in SMALL]).reshape(1, -1)


def kernel(x, c, positions, w_ada, b_ada, g_norm_ffn1, w_ffn1_in, w_ffn1_out, g_norm_mix, w_in, w_pool, pool_scale, w_pool_branch, w_attn_branch, w_out, g_norm_ffn2, w_ffn2_in, w_ffn2_out, g_final, loss_target, m_w_ada, m_b_ada, m_g_norm_ffn1, m_w_ffn1_in, m_w_ffn1_out, m_g_norm_mix, m_w_in, m_w_pool, m_pool_scale, m_w_pool_branch, m_w_attn_branch, m_w_out, m_g_norm_ffn2, m_w_ffn2_in, m_w_ffn2_out, m_g_final, v_w_ada, v_b_ada, v_g_norm_ffn1, v_w_ffn1_in, v_w_ffn1_out, v_g_norm_mix, v_w_in, v_w_pool, v_pool_scale, v_w_pool_branch, v_w_attn_branch, v_w_out, v_g_norm_ffn2, v_w_ffn2_in, v_w_ffn2_out, v_g_final):
    w = dict(w_ada=w_ada, b_ada=b_ada, g_norm_ffn1=g_norm_ffn1, w_ffn1_in=w_ffn1_in, w_ffn1_out=w_ffn1_out,
             g_norm_mix=g_norm_mix, w_in=w_in, w_pool=w_pool, pool_scale=pool_scale, w_pool_branch=w_pool_branch,
             w_attn_branch=w_attn_branch, w_out=w_out, g_norm_ffn2=g_norm_ffn2, w_ffn2_in=w_ffn2_in,
             w_ffn2_out=w_ffn2_out, g_final=g_final)
    mom = dict(w_ada=m_w_ada, b_ada=m_b_ada, g_norm_ffn1=m_g_norm_ffn1, w_ffn1_in=m_w_ffn1_in, w_ffn1_out=m_w_ffn1_out,
               g_norm_mix=m_g_norm_mix, w_in=m_w_in, w_pool=m_w_pool, pool_scale=m_pool_scale,
               w_pool_branch=m_w_pool_branch, w_attn_branch=m_w_attn_branch, w_out=m_w_out, g_norm_ffn2=m_g_norm_ffn2,
               w_ffn2_in=m_w_ffn2_in, w_ffn2_out=m_w_ffn2_out, g_final=m_g_final)
    var = dict(w_ada=v_w_ada, b_ada=v_b_ada, g_norm_ffn1=v_g_norm_ffn1, w_ffn1_in=v_w_ffn1_in, w_ffn1_out=v_w_ffn1_out,
               g_norm_mix=v_g_norm_mix, w_in=v_w_in, w_pool=v_w_pool, pool_scale=v_pool_scale,
               w_pool_branch=v_w_pool_branch, w_attn_branch=v_w_attn_branch, w_out=v_w_out, g_norm_ffn2=v_g_norm_ffn2,
               w_ffn2_in=v_w_ffn2_in, w_ffn2_out=v_w_ffn2_out, g_final=v_g_final)
    ix, iy, ic = _place()
    chip = 2 * ix + iy
    me = 4 * ix + 2 * iy + ic
    nada = w_ada.shape[2]

    shards = {n: w[n][0].astype(BF16) for n in BIG}
    half = D // 2
    shards["w_ffn2_in/0"], shards["w_ffn2_in/1"] = shards["w_ffn2_in"][:half], shards["w_ffn2_in"][half:]
    ws = _Sharded(shards)
    c_all = _run_plan(_SmallGatherPlan(c), "gather_c")[0][:, 0, :]
    b_shard = lax.dynamic_slice_in_dim(b_ada, chip * nada, nada, axis=1)
    mod_cols = _ada_fwd(c_all, w_ada[0], b_shard)
    first = _PlanGroup([_SmallGatherPlan(mod_cols), ws.gather_plan(["w_ffn1_in"])])
    (mod_all,), ffn1 = first.split(_run_plan(first, "gather_first"))
    ws.gathered(["w_ffn1_in"], ffn1)
    mod = jnp.concatenate([lax.dynamic_index_in_dim(mod_all[4 * (kk >> 1) + 2 * (kk & 1)], me, axis=0, keepdims=False)
                           for kk in range(4)])

    def pack(loss, dmod, dgains, g_w_pool, g_pool_scale):
        small_g = dict(b_ada=dmod, g_norm_ffn1=dgains[0], g_norm_mix=dgains[1], g_norm_ffn2=dgains[2],
                       g_final=dgains[3], pool_scale=g_pool_scale, w_pool=g_w_pool)
        return jnp.concatenate([_pack_small(small_g), jnp.pad(loss.reshape(1, 1), ((0, 0), (0, 127)))], axis=1)

    _, dx, _, _, _, _, _, gathered = _example_step(
        x[0], loss_target[0], positions[0], mod, (g_norm_ffn1[0], g_norm_mix[0], g_norm_ffn2[0], g_final),
        w_pool[0], pool_scale[0], ws, pack)

    per_weight, loss_tile = _adam_small(*[[t[n].reshape(1, -1) for n in SMALL] for t in (w, mom, var)], gathered)
    small_out = [{n: per_weight[j][kind].reshape(w[n].shape) for j, n in enumerate(SMALL)} for kind in range(4)]
    loss = loss_tile[0, 0]

    dmod_all = gathered[:, 0, :NMOD * D]
    dmod_cols = lax.dynamic_slice_in_dim(dmod_all, chip * nada, nada, axis=1)
    g_ada = _ada_bwd(c_all, dmod_cols)

    ada_out = _adam(w_ada[0], m_w_ada[0], v_w_ada[0], [g_ada], "adam_w_ada")[0]

    sums = {n: _sum4(ws.recv[n], "sum_" + n) for n in BIG}
    other = dict(zip(BIG, _run_plan(_SwapPlan([sums[n] for n in BIG]), "swap_sibling")))
    big_out = {}
    for n in BIG:
        if sums[n].shape[0] < w[n].shape[1]:
            big_out[n] = _adam_halves(w[n][0], mom[n][0], var[n][0], sums[n], other[n], "adam_" + n)
        else:
            big_out[n] = _adam(w[n][0], mom[n][0], var[n][0], [sums[n], other[n]], "adam_" + n)[0]

    def leaf(kind, n):
        if n == "w_ada":
            return ada_out[kind][None]
        if n in big_out:
            return big_out[n][kind][None]
        return small_out[kind][n]

    return (loss, dx[None], *[leaf(kind, n) for kind in range(4) for n in WEIGHTS])
```

```python
import jax
import jax.numpy as jnp
from jax import lax
from jax.experimental import pallas as pl
from jax.experimental.pallas import tpu as pltpu

F32 = jnp.float32
BF16 = jnp.bfloat16

D = 1024
FF = 2816
FC = FF
PW = 256
GA = 256
HD = 64
LANES = 128
NH = GA // HD
NG = 3
DIL = (1, 4, 16)
BLK = 128
FWD_BLOCKS = 16
BWD_BLOCKS = 16
GW = 2 * D
INW = PW + 3 * NG * GA + GW
NMOD = 9
POOL_WINDOWS = (2, 4, 8, 16)
HALO = 16
EPS = 1e-6
SCALE = HD ** -0.5
NEG = -1e30

LR, B1, B2, AEPS, WD, STEP = 0.001, 0.9, 0.999, 1e-08, 0.01, 10

VMEM_BIG = 56 * 1024 * 1024
TM = 256

MESH = pl.DeviceIdType.MESH
ANY = pl.BlockSpec(memory_space=pl.ANY)


def _call(body, name, grid, in_specs, out_specs, out_shape, scratch=(), vmem=None, comm=None):
    params = pltpu.CompilerParams(dimension_semantics=("arbitrary",) * len(grid), vmem_limit_bytes=vmem)
    n_in, n_out, n_scr = len(in_specs), len(out_shape), len(scratch)
    if comm is None:
        call = pl.pallas_call(body, name=name, grid=grid, in_specs=list(in_specs), out_specs=list(out_specs),
                              out_shape=list(out_shape), scratch_shapes=list(scratch), compiler_params=params)
        return lambda *args: (call(*args), ())
    nc = len(comm.inputs)

    def body_with_comm(*refs):
        ins, refs = refs[:n_in], refs[n_in:]
        c_ins, refs = refs[:nc], refs[nc:]
        outs, refs = refs[:n_out], refs[n_out:]
        c_outs, refs = refs[:nc], refs[nc:]
        scr, sems = refs[:n_scr], refs[n_scr:]
        first = pl.program_id(0) == 0
        last = pl.program_id(0) == grid[0] - 1
        for ax in range(1, len(grid)):
            first = jnp.logical_and(first, pl.program_id(ax) == 0)
            last = jnp.logical_and(last, pl.program_id(ax) == grid[ax] - 1)

        @pl.when(first)
        def _():
            comm.start(c_ins, c_outs, sems)

        body(*ins, *outs, *scr)
        early_relay = len(grid) == 1 and grid[0] >= 4
        if early_relay:
            @pl.when(pl.program_id(0) == (3 * grid[0]) // 4)
            def _():
                comm.relay(c_ins, c_outs, sems)

        @pl.when(last)
        def _():
            if not early_relay:
                comm.relay(c_ins, c_outs, sems)
            comm.wait(c_ins, c_outs, sems)

    call = pl.pallas_call(
        body_with_comm, name=name, grid=grid, in_specs=list(in_specs) + [ANY] * nc,
        out_specs=list(out_specs) + [ANY] * nc, out_shape=list(out_shape) + list(comm.out_shapes),
        scratch_shapes=list(scratch) + list(comm.sem_shapes), compiler_params=params)

    def run(*args):
        res = call(*args, *comm.inputs)
        return res[:n_out], res[n_out:]

    return run


def _rows(tm, n):
    return pl.BlockSpec((tm, n), lambda i: (i, 0))


def _const(shape):
    return pl.BlockSpec(shape, lambda i: (0,) * len(shape))


def _sds(shape, dtype):
    return jax.ShapeDtypeStruct(shape, dtype)


def _dot(a, b):
    return jnp.dot(a, b, preferred_element_type=F32)


def _dot_nt(a, b):
    return lax.dot_general(a, b, (((1,), (1,)), ((), ())), preferred_element_type=F32)


def _dot_tn(a, b):
    return lax.dot_general(a, b, (((0,), (0,)), ((), ())), preferred_element_type=F32)


def _colsum(v):
    return jnp.sum(v, axis=0, keepdims=True)


def _norm_fwd(h, g, sh, sc):
    r = lax.rsqrt(jnp.mean(h * h, axis=-1, keepdims=True) + EPS)
    xh = h * r
    n = xh * g
    return xh, r, n, n * (1.0 + sc) + sh


def _norm_bwd(du, xh, r, n, g, sc):
    dn = du * (1.0 + sc)
    dxh = dn * g
    dh = r * (dxh - xh * jnp.mean(dxh * xh, axis=-1, keepdims=True))
    return dh, _colsum(du), _colsum(du * n), _colsum(dn * xh)


def _load_once(pairs, sems):
    @pl.when(pl.program_id(0) == 0)
    def _():
        cps = [pltpu.make_async_copy(src, dst, sems.at[j]) for j, (src, dst) in enumerate(pairs)]
        for cp in cps:
            cp.start()
        for cp in cps:
            cp.wait()


def _zero_first(ref):
    @pl.when(pl.program_id(0) == 0)
    def _():
        ref[...] = jnp.zeros(ref.shape, ref.dtype)


def _row_chunks(hbm_refs, vmem_ref):
    pairs, row = [], 0
    for ref in hbm_refs:
        pairs.append((ref, vmem_ref.at[pl.ds(row, ref.shape[0]), :]))
        row += ref.shape[0]
    return pairs


def _loss_head(hh, tgt, g):
    r = lax.rsqrt(jnp.mean(hh * hh, axis=-1, keepdims=True) + EPS)
    xh = hh * r
    err = xh * g - tgt
    dy = err * (1.0 / D)
    dxh = dy * g
    dh = r * (dxh - xh * jnp.mean(dxh * xh, axis=-1, keepdims=True))
    return dh, _colsum(err * err), _colsum(dy * xh)


def _ffn_fwd(h, vec, wins, wout, name, comm=None, head=None):
    T = h.shape[0]
    nwin = len(wins)
    nhead = 0 if head is None else 2

    def body(h_ref, vec_ref, *rest):
        head_refs, rest = rest[:nhead], rest[nhead:]
        win_hbms, rest = rest[:nwin], rest[nwin:]
        (wout_hbm, ho_ref, u_ref, a_ref, b_ref, f_ref), rest = rest[:6], rest[6:]
        lacc_refs, (win_v, wout_v, sems) = rest[:nhead // 2], rest[nhead // 2:]
        _load_once(_row_chunks(win_hbms, win_v) + [(wout_hbm, wout_v)], sems)
        hh = h_ref[...]
        g, sh, sc, gt = vec_ref[0:1, :], vec_ref[1:2, :], vec_ref[2:3, :], vec_ref[3:4, :]
        _, _, _, u = _norm_fwd(hh, g, sh, sc)
        ub = u.astype(BF16)
        u_ref[...] = ub
        acc = None
        for j in range(FF // FC):
            lo, hi = j * FC, (j + 1) * FC
            a = _dot(ub, win_v[:, lo:hi])
            b = _dot(ub, win_v[:, FF + lo:FF + hi])
            a_ref[:, lo:hi] = a.astype(BF16)
            b_ref[:, lo:hi] = b.astype(BF16)
            s = (a * jax.nn.sigmoid(a) * b).astype(BF16)
            part = _dot(s, wout_v[lo:hi, :])
            acc = part if acc is None else acc + part
        f_ref[...] = acc.astype(BF16)
        ho = hh + 0.5 * gt * acc
        if head is None:
            ho_ref[...] = ho
        else:
            _zero_first(lacc_refs[0])
            dh, sq, dg = _loss_head(ho, head_refs[0][...], head_refs[1][0:1, :])
            ho_ref[...] = dh
            lacc_refs[0][0:1, :] += sq
            lacc_refs[0][1:2, :] += dg

    head_specs = [] if head is None else [_rows(TM, D), _const((8, D))]
    lacc_spec = [] if head is None else [_const((8, D))]
    lacc_shape = [] if head is None else [_sds((8, D), F32)]
    return _call(
        body, name, (T // TM,),
        [_rows(TM, D), _const((8, D))] + head_specs + [ANY] * (nwin + 1),
        [_rows(TM, D), _rows(TM, D), _rows(TM, FF), _rows(TM, FF), _rows(TM, D)] + lacc_spec,
        [_sds((T, D), F32), _sds((T, D), BF16), _sds((T, FF), BF16), _sds((T, FF), BF16), _sds((T, D), BF16)] + lacc_shape,
        scratch=[pltpu.VMEM((D, 2 * FF), BF16), pltpu.VMEM((FF, D), BF16), pltpu.SemaphoreType.DMA((nwin + 1,))],
        vmem=VMEM_BIG, comm=comm,
    )(h, vec, *([] if head is None else head), *wins, wout)


def _ffn_ab(h, vec, wins, name, comm=None):
    T = h.shape[0]
    nwin = len(wins)

    def body(h_ref, vec_ref, *rest):
        win_hbms, (u_ref, a_ref, b_ref, win_v, sems) = rest[:nwin], rest[nwin:]
        _load_once(_row_chunks(win_hbms, win_v), sems)
        g, sh, sc = vec_ref[0:1, :], vec_ref[1:2, :], vec_ref[2:3, :]
        _, _, _, u = _norm_fwd(h_ref[...], g, sh, sc)
        ub = u.astype(BF16)
        u_ref[...] = ub
        for j in range(FF // FC):
            lo, hi = j * FC, (j + 1) * FC
            a_ref[:, lo:hi] = _dot(ub, win_v[:, lo:hi]).astype(BF16)
            b_ref[:, lo:hi] = _dot(ub, win_v[:, FF + lo:FF + hi]).astype(BF16)

    return _call(
        body, name, (T // TM,),
        [_rows(TM, D), _const((8, D))] + [ANY] * nwin,
        [_rows(TM, D), _rows(TM, FF), _rows(TM, FF)],
        [_sds((T, D), BF16), _sds((T, FF), BF16), _sds((T, FF), BF16)],
        scratch=[pltpu.VMEM((D, 2 * FF), BF16), pltpu.SemaphoreType.DMA((nwin,))],
        vmem=VMEM_BIG, comm=comm,
    )(h, vec, *wins)


def _ffn_out(h, a, b, vec, wout, name, comm=None):
    T = h.shape[0]

    def body(h_ref, a_ref, b_ref, vec_ref, wout_hbm, ho_ref, f_ref, wout_v, sems):
        _load_once([(wout_hbm, wout_v)], sems)
        gt = vec_ref[3:4, :]
        acc = None
        for j in range(FF // FC):
            lo, hi = j * FC, (j + 1) * FC
            av = a_ref[:, lo:hi].astype(F32)
            s = (av * jax.nn.sigmoid(av) * b_ref[:, lo:hi].astype(F32)).astype(BF16)
            part = _dot(s, wout_v[lo:hi, :])
            acc = part if acc is None else acc + part
        f_ref[...] = acc.astype(BF16)
        ho_ref[...] = h_ref[...] + 0.5 * gt * acc

    return _call(
        body, name, (T // TM,),
        [_rows(TM, D), _rows(TM, FF), _rows(TM, FF), _const((8, D)), ANY],
        [_rows(TM, D), _rows(TM, D)],
        [_sds((T, D), F32), _sds((T, D), BF16)],
        scratch=[pltpu.VMEM((FF, D), BF16), pltpu.SemaphoreType.DMA((1,))],
        vmem=VMEM_BIG, comm=comm,
    )(h, a, b, vec, wout)


def _ffn_bwd(dh, h, a, b, f, vec, wins, wout, name, comm=None):
    T = h.shape[0]
    nwin = len(wins)

    def body(dh_ref, h_ref, a_ref, b_ref, f_ref, vec_ref, *rest):
        win_hbms, (wout_hbm, dhi_ref, dab_ref, s_ref, df_ref, acc_ref, win_v, wout_v, sems) = rest[:nwin], rest[nwin:]
        _load_once(_row_chunks(win_hbms, win_v) + [(wout_hbm, wout_v)], sems)
        _zero_first(acc_ref)
        g, sh, sc, gt = vec_ref[0:1, :], vec_ref[1:2, :], vec_ref[2:3, :], vec_ref[3:4, :]
        dho = dh_ref[...]
        df = (0.5 * gt * dho).astype(BF16)
        df_ref[...] = df
        dgt = _colsum(0.5 * dho * f_ref[...].astype(F32))
        for j in range(FF // FC):
            lo, hi = j * FC, (j + 1) * FC
            av = a_ref[:, lo:hi].astype(F32)
            bv = b_ref[:, lo:hi].astype(F32)
            ds = _dot_nt(df, wout_v[lo:hi, :])
            sig = jax.nn.sigmoid(av)
            sa = av * sig
            s_ref[:, lo:hi] = (sa * bv).astype(BF16)
            da = (ds * bv * (sig * (1.0 + av * (1.0 - sig)))).astype(BF16)
            db = (ds * sa).astype(BF16)
            dab_ref[:, lo:hi] = da
            dab_ref[:, FF + lo:FF + hi] = db
        du = _dot_nt(dab_ref[...], win_v[...])
        xh, r, n, _ = _norm_fwd(h_ref[...], g, sh, sc)
        dhn, dsh, dsc, dg = _norm_bwd(du, xh, r, n, g, sc)
        dhi_ref[...] = dho + dhn
        acc_ref[0:1, :] += dsh
        acc_ref[1:2, :] += dsc
        acc_ref[2:3, :] += dg
        acc_ref[3:4, :] += dgt

    return _call(
        body, name, (T // TM,),
        [_rows(TM, D), _rows(TM, D), _rows(TM, FF), _rows(TM, FF), _rows(TM, D), _const((8, D))] + [ANY] * (nwin + 1),
        [_rows(TM, D), _rows(TM, 2 * FF), _rows(TM, FF), _rows(TM, D), _const((8, D))],
        [_sds((T, D), F32), _sds((T, 2 * FF), BF16), _sds((T, FF), BF16), _sds((T, D), BF16), _sds((8, D), F32)],
        scratch=[pltpu.VMEM((D, 2 * FF), BF16), pltpu.VMEM((FF, D), BF16), pltpu.SemaphoreType.DMA((nwin + 1,))],
        vmem=VMEM_BIG, comm=comm,
    )(dh, h, a, b, f, vec, *wins, wout)


def _wgrad(x, y, name, tk, tn, tt, out_dtype=BF16, comm=None):
    T, K = x.shape
    N = y.shape[1]
    nt = T // tt

    def body(x_ref, y_ref, o_ref, acc_ref):
        t = pl.program_id(2)
        part = _dot_tn(x_ref[...], y_ref[...])

        @pl.when(t == 0)
        def _():
            acc_ref[...] = part

        @pl.when(t > 0)
        def _():
            acc_ref[...] += part

        @pl.when(t == nt - 1)
        def _():
            o_ref[...] = acc_ref[...].astype(out_dtype)

    (out,), c_outs = _call(
        body, name, (K // tk, N // tn, nt),
        [pl.BlockSpec((tt, tk), lambda i, j, t: (t, i)), pl.BlockSpec((tt, tn), lambda i, j, t: (t, j))],
        [pl.BlockSpec((tk, tn), lambda i, j, t: (i, j))], [_sds((K, N), out_dtype)],
        scratch=[pltpu.VMEM((tk, tn), F32)], vmem=VMEM_BIG, comm=comm,
    )(x, y)
    return out, c_outs


def _wgrad_scatter(x, y, name, tt, comm=None):
    T, K = x.shape
    n = y.shape[1] // 4
    nt = T // tt
    assert nt >= 2, "a block's hand-over is added one grid step into the next block"
    half = K // 2
    nc = 0 if comm is None else len(comm.inputs)

    def body(chip_ref, x_ref, y_ref, *refs):
        c_ins, refs = refs[:nc], refs[nc:]
        recv_ref, refs = refs[0], refs[1:]
        c_outs, refs = refs[:nc], refs[nc:]
        acc_ref, keep_ref, give_ref, take_ref, local_sem, give_sems, take_sems, send_sems, recv_sems = refs[:9]
        j, t = pl.program_id(0), pl.program_id(1)
        px, py, pc = _place()

        def hand_over(jj):
            return pltpu.make_async_remote_copy(
                src_ref=give_ref.at[jj], dst_ref=take_ref.at[jj], send_sem=give_sems.at[jj], recv_sem=take_sems.at[jj],
                device_id=(px, py, 1 - pc), device_id_type=MESH)

        def send(jj):
            m = (3, 1, 2)[jj]
            return pltpu.make_async_remote_copy(
                src_ref=keep_ref.at[jj], dst_ref=recv_ref.at[m], send_sem=send_sems.at[jj], recv_sem=recv_sems.at[jj],
                device_id=_chip_peer(px, py, pc, m), device_id_type=MESH)

        def add_sibling(jj):
            hand_over(jj).wait_recv()
            keep_ref[jj] = (keep_ref[jj].astype(F32) + take_ref[jj].astype(F32)).astype(BF16)

        if comm is not None:
            @pl.when(jnp.logical_and(j == 0, t == 0))
            def _():
                comm.start(c_ins, c_outs, refs[9:])

        part = _dot_tn(x_ref[...], y_ref[...])

        @pl.when(t == 0)
        def _():
            acc_ref[...] = part

        @pl.when(t > 0)
        def _():
            acc_ref[...] += part

        for jj in range(3):
            @pl.when(jnp.logical_and(j == jj + 1, t == 0))
            def _():
                add_sibling(jj)
                send(jj).start()

        for jj in range(4):
            @pl.when(jnp.logical_and(j == jj, t == nt - 1))
            def _():
                keep_ref[jj] = acc_ref[pl.ds(pl.multiple_of(pc * half, 16), half), :].astype(BF16)
                give_ref[jj] = acc_ref[pl.ds(pl.multiple_of((1 - pc) * half, 16), half), :].astype(BF16)
                hand_over(jj).start()

        @pl.when(jnp.logical_and(j == 3, t == nt - 1))
        def _():
            add_sibling(3)
            own = pltpu.make_async_copy(keep_ref.at[3], recv_ref.at[0], local_sem.at[0])
            own.start()
            for jj in range(3):
                send(jj).wait_recv()
            for jj in range(3):
                send(jj).wait_send()
            for jj in range(4):
                hand_over(jj).wait_send()
            own.wait()
            if comm is not None:
                comm.relay(c_ins, c_outs, refs[9:])
                comm.wait(c_ins, c_outs, refs[9:])

    grid_spec = pltpu.PrefetchScalarGridSpec(
        num_scalar_prefetch=1, grid=(4, nt),
        in_specs=[pl.BlockSpec((tt, K), lambda j, t, chip: (t, 0)),
                  pl.BlockSpec((tt, n), lambda j, t, chip: (t, chip[0] ^ jnp.where(j == 0, 3, jnp.where(j == 3, 0, j))))]
        + [ANY] * nc,
        out_specs=[ANY] * (1 + nc),
        scratch_shapes=[pltpu.VMEM((K, n), F32)] + [pltpu.VMEM((4, half, n), BF16)] * 3
        + [pltpu.SemaphoreType.DMA((1,))] + [pltpu.SemaphoreType.DMA((4,))] * 2 + [pltpu.SemaphoreType.DMA((3,))] * 2
        + ([] if comm is None else list(comm.sem_shapes)))
    px, py, _ = _place()
    res = pl.pallas_call(
        body, name=name, grid_spec=grid_spec,
        out_shape=[_sds((4, half, n), BF16)] + ([] if comm is None else list(comm.out_shapes)),
        compiler_params=pltpu.CompilerParams(dimension_semantics=("arbitrary", "arbitrary"), vmem_limit_bytes=VMEM_BIG),
    )((2 * px + py).astype(jnp.int32).reshape(1), x, y, *([] if comm is None else comm.inputs))
    return res[0], res[1:]


def _swap_halves(t):
    w = t.shape[1]
    lane = lax.broadcasted_iota(jnp.int32, t.shape, 1)
    return jnp.where(lane % HD < HD // 2, pltpu.roll(t, w - HD // 2, 1), pltpu.roll(t, HD // 2, 1))


def _rope(t, cos, sin_signed):
    c = jnp.tile(cos, (1, t.shape[1] // cos.shape[1]))
    s = jnp.tile(sin_signed, (1, t.shape[1] // sin_signed.shape[1]))
    return t * c + _swap_halves(t) * s


def _rope_bwd(dt, cos, sin_signed):
    c = jnp.tile(cos, (1, dt.shape[1] // cos.shape[1]))
    s = jnp.tile(sin_signed, (1, dt.shape[1] // sin_signed.shape[1]))
    return dt * c + _swap_halves(dt * s)


def _rm_spec(dil):
    return pl.BlockSpec((dil, TM // dil, GA), lambda i: (0, i, 0))


def _to_residues(t, dst_ref, scr_ref, dil):
    if dil == 1:
        dst_ref[0] = t.astype(dst_ref.dtype)
        return
    for j in range(GA // LANES):
        scr_ref[j] = t[:, j * LANES:(j + 1) * LANES]
    for r in range(dil):
        for j in range(GA // LANES):
            rows = scr_ref.at[j][pl.ds(r, TM // dil, stride=dil), :]
            dst_ref[r, :, j * LANES:(j + 1) * LANES] = rows.astype(dst_ref.dtype)


def _from_residues(src_ref, scr_ref, dil):
    if dil == 1:
        return src_ref[0].astype(F32)
    for r in range(dil):
        for j in range(GA // LANES):
            scr_ref.at[j][pl.ds(r, TM // dil, stride=dil), :] = src_ref[r, :, j * LANES:(j + 1) * LANES].astype(F32)
    return jnp.concatenate([scr_ref[j] for j in range(GA // LANES)], axis=1)


def _mix_proj(h, vec, win, cos, sin, comm=None):
    T = h.shape[0]

    def body(h_ref, vec_ref, win_hbm, cos_ref, sin_ref, u_ref, p_ref, *rest):
        qkv_refs, gates_ref, win_v, scr_ref, sems = rest[:3 * NG], rest[3 * NG], rest[3 * NG + 1], rest[3 * NG + 2], rest[3 * NG + 3]
        _load_once([(win_hbm, win_v)], sems)
        g, sh, sc = vec_ref[0:1, :], vec_ref[1:2, :], vec_ref[2:3, :]
        _, _, _, u = _norm_fwd(h_ref[...], g, sh, sc)
        ub = u.astype(BF16)
        u_ref[...] = ub
        mixer_cols = PW + 3 * NG * GA
        proj = _dot(ub, win_v[:, 0:mixer_cols])
        p_ref[...] = proj[:, 0:PW]
        cos_t, sin_t = cos_ref[...], sin_ref[...]
        for j in range(3 * NG):
            col = PW + j * GA
            t = proj[:, col:col + GA]
            if j < 2 * NG:
                t = _rope(t, cos_t, sin_t)
            _to_residues(t, qkv_refs[j], scr_ref, DIL[j % NG])
        gates_ref[...] = jax.nn.sigmoid(_dot(ub, win_v[:, mixer_cols:INW])).astype(BF16)

    outs, c_outs = _call(
        body, "mix_proj", (T // TM,),
        [_rows(TM, D), _const((8, D)), ANY, _rows(TM, 128), _rows(TM, 128)],
        [_rows(TM, D), _rows(TM, PW)] + [_rm_spec(d) for d in DIL] * 3 + [_rows(TM, GW)],
        [_sds((T, D), BF16), _sds((T, PW), F32)] + [_sds((d, T // d, GA), BF16) for d in DIL] * 3 + [_sds((T, GW), BF16)],
        scratch=[pltpu.VMEM((D, INW), BF16), pltpu.VMEM((GA // LANES, TM, LANES), F32), pltpu.SemaphoreType.DMA((1,))],
        vmem=VMEM_BIG, comm=comm,
    )(h, vec, win, cos, sin)
    return (outs[0], outs[1], outs[2:2 + NG], outs[2 + NG:2 + 2 * NG], outs[2 + 2 * NG:2 + 3 * NG], outs[2 + 3 * NG]), c_outs


def _head_masks():
    lane_head = lax.broadcasted_iota(jnp.int32, (BLK, GA), 1) // HD
    return [lane_head == hd for hd in range(NH)]


def _expand_heads(t, hm):
    return jnp.concatenate([jnp.where(m, t, jnp.zeros_like(t)) for m in hm], axis=0)


def _collapse_heads(tb, hm):
    out = None
    for hd, m in enumerate(hm):
        part = jnp.where(m, tb[hd * BLK:(hd + 1) * BLK, :], 0.0)
        out = part if out is None else out + part
    return out


def _head_rows(t):
    return jnp.concatenate([t[:, hd * HD:hd * HD + 1] for hd in range(NH)], axis=0)


def _band(has_prev):
    a = lax.broadcasted_iota(jnp.int32, (NH * BLK, 2 * BLK), 0) & (BLK - 1)
    c = lax.broadcasted_iota(jnp.int32, (NH * BLK, 2 * BLK), 1)
    return jnp.logical_and(c >= jnp.where(has_prev, a, BLK), c <= a + BLK)


def _attn_fwd(q, k, v, nb, name, comm=None):
    T = q.shape[0]
    nbt = T // BLK

    def block(qv, kcat, vcat, has_prev, hm):
        s = jnp.where(_band(has_prev), _dot_nt(_expand_heads(qv, hm), kcat) * SCALE, NEG)
        mx = jnp.max(s, axis=-1, keepdims=True)
        e = jnp.exp(s - mx)
        l = jnp.sum(e, axis=-1, keepdims=True)
        ob = _dot((e * (1.0 / l)).astype(BF16), vcat)
        return _collapse_heads(ob, hm), _collapse_heads(jnp.broadcast_to(mx + jnp.log(l), (NH * BLK, GA)), hm)

    def body(q_ref, k_ref, kp_ref, v_ref, vp_ref, o_ref, lse_ref):
        b0 = FWD_BLOCKS * pl.program_id(0)
        hm = _head_masks()
        for b in range(FWD_BLOCKS):
            rows = slice(b * BLK, (b + 1) * BLK)
            if b == 0:
                kcat = jnp.concatenate([kp_ref[...], k_ref[rows, :]], axis=0)
                vcat = jnp.concatenate([vp_ref[...], v_ref[rows, :]], axis=0)
            else:
                kcat, vcat = k_ref[(b - 1) * BLK:(b + 1) * BLK, :], v_ref[(b - 1) * BLK:(b + 1) * BLK, :]
            o_ref[rows, :], lse_ref[rows, :] = block(q_ref[rows, :], kcat, vcat, ((b0 + b) & (nb - 1)) != 0, hm)

    cur = pl.BlockSpec((FWD_BLOCKS * BLK, GA), lambda i: (i, 0))
    prev = pl.BlockSpec((BLK, GA), lambda i: (jnp.maximum(FWD_BLOCKS * i - 1, 0), 0))
    return _call(body, name, (nbt // FWD_BLOCKS,), [cur, cur, prev, cur, prev], [cur, cur],
                 [_sds((T, GA), F32), _sds((T, GA), F32)], comm=comm)(q, k, k, v, v)


def _attn_bwd(q, k, v, do, lse, e, nb, name, comm=None):
    T = q.shape[0]
    nbt = T // BLK

    nblk = BWD_BLOCKS

    def probs_and_ds(qb, dob, kcat, vcat, lsev, ev, valid):
        p = jnp.where(valid, jnp.exp(_dot_nt(qb, kcat) * SCALE - _head_rows(lsev)), 0.0)
        return p.astype(BF16), (p * (_dot_nt(dob, vcat) + _head_rows(ev))).astype(BF16)

    def body(q_ref, k_ref, v_ref, do_ref, lse_ref, e_ref, kp_ref, vp_ref, qn_ref, don_ref, lsen_ref, en_ref,
             dq_ref, dk_ref, dv_ref):
        b0 = nblk * pl.program_id(0)
        hm = _head_masks()
        rows = [slice(b * BLK, (b + 1) * BLK) for b in range(nblk)]
        qs = [_expand_heads(q_ref[r, :], hm) for r in rows] + [_expand_heads(qn_ref[...], hm)]
        dos = [_expand_heads(do_ref[r, :], hm) for r in rows] + [_expand_heads(don_ref[...], hm)]
        ps, dss = [], []
        for b, r in enumerate(rows):
            if b == 0:
                kcat = jnp.concatenate([kp_ref[...], k_ref[r, :]], axis=0)
                vcat = jnp.concatenate([vp_ref[...], v_ref[r, :]], axis=0)
            else:
                kcat, vcat = k_ref[(b - 1) * BLK:(b + 1) * BLK, :], v_ref[(b - 1) * BLK:(b + 1) * BLK, :]
            p, ds = probs_and_ds(qs[b], dos[b], kcat, vcat, lse_ref[r, :], e_ref[r, :], _band(((b0 + b) & (nb - 1)) != 0))
            dq_ref[r, :] = _collapse_heads(_dot(ds, kcat) * SCALE, hm)
            ps.append(p)
            dss.append(ds)
        a = lax.broadcasted_iota(jnp.int32, (NH * BLK, BLK), 0) & (BLK - 1)
        c = lax.broadcasted_iota(jnp.int32, (NH * BLK, BLK), 1)
        valid_n = jnp.logical_and(c >= a, ((b0 + nblk) & (nb - 1)) != 0)
        p_n, ds_n = probs_and_ds(qs[nblk], dos[nblk], k_ref[rows[-1], :], v_ref[rows[-1], :], lsen_ref[...], en_ref[...], valid_n)
        for b, r in enumerate(rows):
            ds_after = dss[b + 1][:, :BLK] if b + 1 < nblk else ds_n
            p_after = ps[b + 1][:, :BLK] if b + 1 < nblk else p_n
            q_pair = jnp.concatenate([qs[b], qs[b + 1]], axis=0)
            do_pair = jnp.concatenate([dos[b], dos[b + 1]], axis=0)
            dk_ref[r, :] = _dot_tn(jnp.concatenate([dss[b][:, BLK:], ds_after], axis=0), q_pair) * SCALE
            dv_ref[r, :] = _dot_tn(jnp.concatenate([ps[b][:, BLK:], p_after], axis=0), do_pair).astype(BF16)

    cur = pl.BlockSpec((nblk * BLK, GA), lambda i: (i, 0))
    prev = pl.BlockSpec((BLK, GA), lambda i: (jnp.maximum(nblk * i - 1, 0), 0))
    nxt = pl.BlockSpec((BLK, GA), lambda i: (jnp.minimum(nblk * i + nblk, nbt - 1), 0))
    return _call(body, name, (nbt // nblk,), [cur] * 6 + [prev, prev] + [nxt] * 4, [cur, cur, cur],
                 [_sds((T, GA), F32), _sds((T, GA), F32), _sds((T, GA), BF16)],
                 comm=comm)(q, k, v, do, lse, e, k, v, q, do, lse, e)


def _flat(t):
    return t.reshape(t.shape[0] * t.shape[1], t.shape[2])


def _by_residue(t, dil):
    return t.reshape(dil, t.shape[0] // dil, t.shape[1])


def _pool_consts(shape, row0):
    lane = lax.broadcasted_iota(jnp.int32, shape, 1)
    t = lax.broadcasted_iota(jnp.int32, shape, 0) + row0
    grp = lane // (PW // len(POOL_WINDOWS))
    win = jnp.where(grp == 0, POOL_WINDOWS[0], jnp.where(grp == 1, POOL_WINDOWS[1],
                    jnp.where(grp == 2, POOL_WINDOWS[2], POOL_WINDOWS[3])))
    cnt = jnp.minimum(t + 1, win).astype(F32)
    return grp, cnt


def _window_sums(ext_ref, base, step, tm):
    outs, run = [], None
    for j in range(POOL_WINDOWS[-1]):
        sl = ext_ref[pl.ds(base + step * j, tm), :]
        run = sl if run is None else run + sl
        if j + 1 in POOL_WINDOWS:
            outs.append(run)
    return outs


def _select_group(grp, vals):
    return jnp.where(grp == 0, vals[0], jnp.where(grp == 1, vals[1], jnp.where(grp == 2, vals[2], vals[3])))


def _pool_d(pc_ref, pp_ref, ext_ref, i, tm):
    ext_ref[0:HALO, :] = jnp.where(i > 0, pp_ref[tm - HALO:tm, :], 0.0)
    ext_ref[HALO:HALO + tm, :] = pc_ref[...]
    grp, cnt = _pool_consts((tm, PW), i * tm)
    sums = _window_sums(ext_ref, HALO, -1, tm)
    return _select_group(grp, sums) / cnt - pc_ref[...]


def _group_weights(ls):
    mx = jnp.maximum(jnp.maximum(ls[0], ls[1]), ls[2])
    es = [jnp.exp(l - mx) for l in ls]
    inv = 1.0 / (es[0] + es[1] + es[2])
    return [e * inv for e in es]


def _mix_merge(h, vec, p, os, lses, gates, wp_bd, pscale, wpb, wab, wout, comm=None):
    T = h.shape[0]

    def body(h_ref, vec_ref, pc_ref, pp_ref, o0, o1, o2, l0, l1, l2, gates_ref, wp_ref, ps_ref, wpb_ref, wab_ref, wout_ref,
             ho_ref, yp_ref, ya_ref, mg_ref, mo_ref, d_ref, ext_ref, scr_ref):
        i = pl.program_id(0)
        gt = vec_ref[3:4, :]
        d = _pool_d(pc_ref, pp_ref, ext_ref, i, TM).astype(BF16)
        d_ref[...] = d
        ypool = (_dot(d, wp_ref[...]) * ps_ref[0:1, :]).astype(BF16)
        yp_ref[...] = ypool
        w = _group_weights([_from_residues(r, scr_ref, dl) for r, dl in zip((l0, l1, l2), DIL)])
        yattn = None
        for wg, o_ref, dl in zip(w, (o0, o1, o2), DIL):
            part = wg * _from_residues(o_ref, scr_ref, dl)
            yattn = part if yattn is None else yattn + part
        yattn = yattn.astype(BF16)
        ya_ref[...] = yattn
        merged = (gates_ref[:, 0:D].astype(F32) * _dot(ypool, wpb_ref[...])
                  + gates_ref[:, D:GW].astype(F32) * _dot(yattn, wab_ref[...])).astype(BF16)
        mg_ref[...] = merged
        mo = _dot(merged, wout_ref[...])
        mo_ref[...] = mo.astype(BF16)
        ho_ref[...] = h_ref[...] + gt * mo

    prev = pl.BlockSpec((TM, PW), lambda i: (jnp.maximum(i - 1, 0), 0))
    return _call(
        body, "mix_merge", (T // TM,),
        [_rows(TM, D), _const((8, D)), _rows(TM, PW), prev] + [_rm_spec(dl) for dl in DIL] * 2 + [_rows(TM, GW), _const((PW, PW)),
         _const((8, PW)), _const((PW, D)), _const((GA, D)), _const((D, D))],
        [_rows(TM, D), _rows(TM, PW), _rows(TM, GA), _rows(TM, D), _rows(TM, D), _rows(TM, PW)],
        [_sds((T, D), F32), _sds((T, PW), BF16), _sds((T, GA), BF16), _sds((T, D), BF16), _sds((T, D), BF16), _sds((T, PW), BF16)],
        scratch=[pltpu.VMEM((TM + HALO, PW), F32), pltpu.VMEM((GA // LANES, TM, LANES), F32)],
        vmem=VMEM_BIG, comm=comm,
    )(h, vec, p, p, *os, *lses, gates, wp_bd, pscale, wpb, wab, wout)


def _mix_bwd_a(dh, vec, mixout, merged, gates, ypool, yattn, dpool, os, lses, wp_bd, pscale, wpb, wab, wout, ones_bd,
               comm=None):
    T = dh.shape[0]
    nt = T // TM

    def body(dh_ref, vec_ref, mo_ref, mg_ref, gates_ref, yp_ref, ya_ref, d_ref, o0, o1, o2, l0, l1, l2,
             wp_ref, ps_ref, wpb_ref, wab_ref, wout_ref, ones_ref,
             dgates_ref, do0, do1, do2, e0, e1, e2, dd_ref, acc_ref, acc2_ref, g_out_ref, g_pb_ref, g_ab_ref, g_pool_ref,
             scr_ref, a_out, a_pb, a_ab, a_pool):
        _zero_first(acc_ref)
        _zero_first(acc2_ref)
        for a_ref in (a_out, a_pb, a_ab, a_pool):
            _zero_first(a_ref)
        gt = vec_ref[3:4, :]
        dho = dh_ref[...]
        acc_ref[3:4, :] += _colsum(dho * mo_ref[...].astype(F32))
        dmo = (gt * dho).astype(BF16)
        a_out[...] += _dot_tn(mg_ref[...], dmo)
        dmerged = _dot_nt(dmo, wout_ref[...])
        gp = gates_ref[:, 0:D].astype(F32)
        ga = gates_ref[:, D:GW].astype(F32)
        bp = _dot(yp_ref[...], wpb_ref[...])
        ba = _dot(ya_ref[...], wab_ref[...])
        dgates_ref[:, 0:D] = (dmerged * bp * gp * (1.0 - gp)).astype(BF16)
        dgates_ref[:, D:GW] = (dmerged * ba * ga * (1.0 - ga)).astype(BF16)
        dbp = (dmerged * gp).astype(BF16)
        dba = (dmerged * ga).astype(BF16)
        a_pb[...] += _dot_tn(yp_ref[...], dbp)
        a_ab[...] += _dot_tn(ya_ref[...], dba)
        dypool = _dot_nt(dbp, wpb_ref[...])
        ypre = _dot(d_ref[...], wp_ref[...])
        acc2_ref[0:1, :] += _colsum(dypool * ypre)
        dyp = (dypool * ps_ref[0:1, :]).astype(BF16)
        a_pool[...] += _dot_tn(d_ref[...], dyp)
        dd_ref[...] = _dot_nt(dyp, wp_ref[...])
        dya = _dot_nt(dba, wab_ref[...])
        w = _group_weights([_from_residues(r, scr_ref, dl) for r, dl in zip((l0, l1, l2), DIL)])
        ya = None
        for wg, o_ref, dl in zip(w, (o0, o1, o2), DIL):
            part = wg * _from_residues(o_ref, scr_ref, dl)
            ya = part if ya is None else ya + part
        prod = dya * ya
        hi = prod.astype(BF16)
        lo = (prod - hi.astype(F32)).astype(BF16)
        tot = _dot(hi, ones_ref[...]) + _dot(lo, ones_ref[...])
        for wg, do_ref, e_ref, dl in zip(w, (do0, do1, do2), (e0, e1, e2), DIL):
            _to_residues(wg * dya, do_ref, scr_ref, dl)
            _to_residues(-wg * tot, e_ref, scr_ref, dl)

        @pl.when(pl.program_id(0) == nt - 1)
        def _():
            g_out_ref[...] = a_out[...].astype(BF16)
            g_pb_ref[...] = a_pb[...].astype(BF16)
            g_ab_ref[...] = a_ab[...].astype(BF16)
            g_pool_ref[...] = a_pool[...]

    return _call(
        body, "mix_bwd_a", (nt,),
        [_rows(TM, D), _const((8, D)), _rows(TM, D), _rows(TM, D), _rows(TM, GW), _rows(TM, PW), _rows(TM, GA), _rows(TM, PW)]
        + [_rm_spec(dl) for dl in DIL] * 2
        + [_const((PW, PW)), _const((8, PW)), _const((PW, D)), _const((GA, D)), _const((D, D)), _const((GA, GA))],
        [_rows(TM, GW)] + [_rm_spec(dl) for dl in DIL] * 2 + [_rows(TM, PW), _const((8, D)), _const((8, PW))]
        + [_const((D, D)), _const((PW, D)), _const((GA, D)), _const((PW, PW))],
        [_sds((T, GW), BF16)] + [_sds((dl, T // dl, GA), BF16) for dl in DIL]
        + [_sds((dl, T // dl, GA), F32) for dl in DIL] + [_sds((T, PW), F32), _sds((8, D), F32), _sds((8, PW), F32)]
        + [_sds((D, D), BF16), _sds((PW, D), BF16), _sds((GA, D), BF16), _sds((PW, PW), F32)],
        scratch=[pltpu.VMEM((GA // LANES, TM, LANES), F32), pltpu.VMEM((D, D), F32), pltpu.VMEM((PW, D), F32),
                 pltpu.VMEM((GA, D), F32), pltpu.VMEM((PW, PW), F32)],
        vmem=VMEM_BIG, comm=comm,
    )(dh, vec, mixout, merged, gates, ypool, yattn, dpool, *os, *lses, wp_bd, pscale, wpb, wab, wout, ones_bd)


def _mix_bwd_b(dh, h, vec, dd, dqs, dks, dvs, dgates, cos, sin, win):
    T = h.shape[0]
    nt = T // TM

    def body(dh_ref, h_ref, vec_ref, ddc_ref, ddn_ref, *rest):
        qk_refs, dv_refs = rest[:2 * NG], rest[2 * NG:3 * NG]
        dgates_ref, cos_ref, sin_ref, win_hbm, dhi_ref, dproj_ref, acc_ref, win_v, ext_ref, scr_ref, sems = rest[3 * NG:]
        i = pl.program_id(0)
        _load_once([(win_hbm, win_v)], sems)
        _zero_first(acc_ref)
        g, sh, sc = vec_ref[0:1, :], vec_ref[1:2, :], vec_ref[2:3, :]
        grp, cnt = _pool_consts((TM, PW), i * TM)
        _, cnt_n = _pool_consts((HALO, PW), (i + 1) * TM)
        ext_ref[0:TM, :] = ddc_ref[...] / cnt
        ext_ref[TM:TM + HALO, :] = jnp.where(i < nt - 1, ddn_ref[0:HALO, :] / cnt_n, 0.0)
        dp = _select_group(grp, _window_sums(ext_ref, 0, 1, TM)) - ddc_ref[...]
        dproj_ref[:, 0:PW] = dp.astype(BF16)
        cos_t, sin_t = cos_ref[...], sin_ref[...]
        for j in range(2 * NG):
            col = PW + j * GA
            dt = _from_residues(qk_refs[j], scr_ref, DIL[j % NG])
            dproj_ref[:, col:col + GA] = _rope_bwd(dt, cos_t, sin_t).astype(BF16)
        for j in range(NG):
            col = PW + (2 * NG + j) * GA
            dproj_ref[:, col:col + GA] = _from_residues(dv_refs[j], scr_ref, DIL[j]).astype(BF16)
        dproj_ref[:, PW + 3 * NG * GA:INW] = dgates_ref[...]
        du = None
        for j in range(INW // 512):
            part = _dot_nt(dproj_ref[:, j * 512:(j + 1) * 512], win_v[:, j * 512:(j + 1) * 512])
            du = part if du is None else du + part
        xh, r, n, _ = _norm_fwd(h_ref[...], g, sh, sc)
        dhn, dsh, dsc, dg = _norm_bwd(du, xh, r, n, g, sc)
        dhi_ref[...] = dh_ref[...] + dhn
        acc_ref[0:1, :] += dsh
        acc_ref[1:2, :] += dsc
        acc_ref[2:3, :] += dg

    nxt = pl.BlockSpec((TM, PW), lambda i: (jnp.minimum(i + 1, nt - 1), 0))
    return _call(
        body, "mix_bwd_b", (nt,),
        [_rows(TM, D), _rows(TM, D), _const((8, D)), _rows(TM, PW), nxt] + [_rm_spec(dl) for dl in DIL] * 3
        + [_rows(TM, GW), _rows(TM, 128), _rows(TM, 128), ANY],
        [_rows(TM, D), _rows(TM, INW), _const((8, D))],
        [_sds((T, D), F32), _sds((T, INW), BF16), _sds((8, D), F32)],
        scratch=[pltpu.VMEM((D, INW), BF16), pltpu.VMEM((TM + HALO, PW), F32), pltpu.VMEM((GA // LANES, TM, LANES), F32),
                 pltpu.SemaphoreType.DMA((1,))],
        vmem=VMEM_BIG,
    )(dh, h, vec, dd, dd, *dqs, *dks, *dvs, dgates, cos, sin, win)[0]


def _ada_fwd(c_all, w_shard, b_shard):
    n = w_shard.shape[1]

    def body(c_ref, w_ref, b_ref, o_ref):
        cv = c_ref[...]
        cond = (cv * jax.nn.sigmoid(cv)).astype(BF16)
        o_ref[...] = _dot(cond, w_ref[...].astype(BF16)) + b_ref[...]

    tn = n // 3
    return pl.pallas_call(
        body, name="ada_fwd", grid=(3,),
        in_specs=[pl.BlockSpec((8, D), lambda j: (0, 0)), pl.BlockSpec((D, tn), lambda j: (0, j)), pl.BlockSpec((1, tn), lambda j: (0, j))],
        out_specs=pl.BlockSpec((8, tn), lambda j: (0, j)), out_shape=_sds((8, n), F32),
        compiler_params=pltpu.CompilerParams(dimension_semantics=("arbitrary",)),
    )(c_all, w_shard, b_shard)


def _ada_bwd(c_all, dmod_shard):
    n = dmod_shard.shape[1]

    def body(c_ref, d_ref, o_ref):
        cv = c_ref[...]
        cond = (cv * jax.nn.sigmoid(cv)).astype(BF16)
        o_ref[...] = _dot_tn(cond, d_ref[...].astype(BF16))

    tn = n // 3
    return pl.pallas_call(
        body, name="ada_bwd", grid=(3,),
        in_specs=[pl.BlockSpec((8, D), lambda j: (0, 0)), pl.BlockSpec((8, tn), lambda j: (0, j))],
        out_specs=pl.BlockSpec((D, tn), lambda j: (0, j)), out_shape=_sds((D, n), F32),
        compiler_params=pltpu.CompilerParams(dimension_semantics=("arbitrary",)),
    )(c_all, dmod_shard)


def _adam_math(w, g, m, v):
    m2 = B1 * m + (1.0 - B1) * g
    v2 = B2 * v + (1.0 - B2) * (g * g)
    m_hat = m2 / (1.0 - B1 ** STEP)
    v_hat = v2 / (1.0 - B2 ** STEP)
    delta = -LR * (m_hat / (jnp.sqrt(v_hat) + AEPS) + WD * w)
    return delta, m2, v2


def _adam(w, m, v, parts, name, comm=None):
    R, C = w.shape
    tr = R
    for cand in (128, 64, 32, 16, 8):
        if R % cand == 0:
            tr = cand
            break
    np_ = len(parts)

    def body(w_ref, m_ref, v_ref, *rest):
        p_refs, (g_ref, d_ref, m2_ref, v2_ref) = rest[:np_], rest[np_:]
        g = p_refs[0][...]
        for pr in p_refs[1:]:
            g = g + pr[...]
        delta, m2, v2 = _adam_math(w_ref[...], g, m_ref[...], v_ref[...])
        g_ref[...] = g
        d_ref[...] = delta
        m2_ref[...] = m2
        v2_ref[...] = v2

    spec = pl.BlockSpec((tr, C), lambda i: (i, 0))
    return _call(body, name, (R // tr,), [spec] * (3 + np_), [spec] * 4, [_sds((R, C), F32)] * 4,
                 vmem=VMEM_BIG, comm=comm)(w, m, v, *parts)


def _adam_halves(w, m, v, mine, other, name):
    R, C = w.shape
    tr = 128
    nh = R // 2 // tr

    def body(c_ref, w_ref, m_ref, v_ref, mine_ref, other_ref, g_ref, d_ref, m2_ref, v2_ref):
        i = pl.program_id(0)
        in_mine = jnp.logical_and(i >= c_ref[0] * nh, i < (c_ref[0] + 1) * nh)
        g = jnp.where(in_mine, mine_ref[...], other_ref[...])
        delta, m2, v2 = _adam_math(w_ref[...], g, m_ref[...], v_ref[...])
        g_ref[...] = g
        d_ref[...] = delta
        m2_ref[...] = m2
        v2_ref[...] = v2

    spec = pl.BlockSpec((tr, C), lambda i, c: (i, 0))
    grid_spec = pltpu.PrefetchScalarGridSpec(
        num_scalar_prefetch=1, grid=(R // tr,),
        in_specs=[spec] * 3 + [pl.BlockSpec((tr, C), lambda i, c: (jnp.clip(i - c[0] * nh, 0, nh - 1), 0)),
                               pl.BlockSpec((tr, C), lambda i, c: (jnp.clip(i - (1 - c[0]) * nh, 0, nh - 1), 0))],
        out_specs=[spec] * 4)
    return pl.pallas_call(
        body, name=name, grid_spec=grid_spec, out_shape=[_sds((R, C), F32)] * 4,
        compiler_params=pltpu.CompilerParams(dimension_semantics=("arbitrary",), vmem_limit_bytes=VMEM_BIG),
    )(lax.axis_index("c").astype(jnp.int32).reshape(1), w, m, v, mine, other)


def _adam_small(ws, ms, vs, gathered):
    n = len(ws)
    sizes = [a.shape[1] for a in ws]

    def total(ga_ref, off, size):
        g = ga_ref[0, :, off:off + size]
        for dev in range(1, 8):
            g = g + ga_ref[dev, :, off:off + size]
        return g

    def body(*refs):
        w_refs, m_refs, v_refs, ga_ref, outs = refs[:n], refs[n:2 * n], refs[2 * n:3 * n], refs[3 * n], refs[3 * n + 1:]
        off = 0
        for j, size in enumerate(sizes):
            g = total(ga_ref, off, size)
            delta, m2, v2 = _adam_math(w_refs[j][...], g, m_refs[j][...], v_refs[j][...])
            for ref, val in zip(outs[4 * j:4 * j + 4], (g, delta, m2, v2)):
                ref[...] = val
            off += size
        outs[4 * n][...] = total(ga_ref, off, 128)

    res = pl.pallas_call(
        body, name="adam_small",
        out_shape=[_sds((1, size), F32) for size in sizes for _ in range(4)] + [_sds((1, 128), F32)],
    )(*ws, *ms, *vs, gathered)
    return [res[4 * j:4 * j + 4] for j in range(n)], res[4 * n]


def _sum4(blocks, name):
    _, R, C = blocks.shape
    tr = R
    for cand in (256, 128, 64, 32, 16):
        if R % cand == 0:
            tr = cand
            break

    def body(r_ref, out_ref):
        out_ref[...] = ((r_ref[0].astype(F32) + r_ref[1].astype(F32)) + r_ref[2].astype(F32)) + r_ref[3].astype(F32)

    return pl.pallas_call(
        body, name=name, grid=(R // tr,),
        in_specs=[pl.BlockSpec((4, tr, C), lambda i: (0, i, 0))],
        out_specs=pl.BlockSpec((tr, C), lambda i: (i, 0)), out_shape=_sds((R, C), F32),
        compiler_params=pltpu.CompilerParams(dimension_semantics=("arbitrary",)),
    )(blocks)


def _place():
    return lax.axis_index("x"), lax.axis_index("y"), lax.axis_index("c")


def _chip_peer(x, y, c, m):
    return (x ^ (m >> 1), y ^ (m & 1), c)


def _shard_ref(ref, axis, k, n):
    start = pl.multiple_of(k * n, 128 if axis == 1 else 16)
    return ref.at[:, pl.ds(start, n)] if axis == 1 else ref.at[pl.ds(start, n), :]


def _half_rows(ref, axis, k, n, hc):
    if axis == 1:
        half = ref.shape[0] // 2
        return ref.at[pl.ds(pl.multiple_of(hc * half, 16), half), pl.ds(pl.multiple_of(k * n, 128), n)]
    half = n // 2
    return ref.at[pl.ds(pl.multiple_of(k * n + hc * half, 16), half), :]


class _GatherPlan:
    def __init__(self, shards, axes):
        self.inputs, self.axes, nw = list(shards), list(axes), len(shards)
        self.out_shapes = [_sds((s.shape[0] * (4 if ax == 0 else 1), s.shape[1] * (4 if ax == 1 else 1)), BF16)
                           for s, ax in zip(shards, axes)]
        self.sem_shapes = [pltpu.SemaphoreType.DMA((nw,))] + [pltpu.SemaphoreType.DMA((nw, 3))] * 4

    def _copies(self, ins, outs, sems):
        local_sems, send_sems, recv_sems, pass_sems, got_sems = sems
        x, y, c = _place()
        k = 2 * x + y
        local, sends, arrivals, passes, handed = [], [], [], [], []
        for j, ax in enumerate(self.axes):
            n = ins[j].shape[ax]
            half = ins[j].shape[0] // 2
            local.append(pltpu.make_async_copy(ins[j], _shard_ref(outs[j], ax, k, n), local_sems.at[j]))
            my_half = ins[j].at[pl.ds(pl.multiple_of(c * half, 16), half), :]
            for m in range(1, 4):
                sends.append(pltpu.make_async_remote_copy(
                    src_ref=my_half, dst_ref=_half_rows(outs[j], ax, k, n, c), send_sem=send_sems.at[j, m - 1],
                    recv_sem=recv_sems.at[j, m - 1], device_id=_chip_peer(x, y, c, m), device_id_type=MESH))
                theirs = _half_rows(outs[j], ax, k ^ m, n, c)
                arrivals.append(pltpu.make_async_remote_copy(
                    src_ref=my_half, dst_ref=theirs, send_sem=send_sems.at[j, m - 1], recv_sem=recv_sems.at[j, m - 1],
                    device_id=(x, y, c), device_id_type=MESH))
                passes.append(pltpu.make_async_remote_copy(
                    src_ref=theirs, dst_ref=theirs, send_sem=pass_sems.at[j, m - 1], recv_sem=got_sems.at[j, m - 1],
                    device_id=(x, y, 1 - c), device_id_type=MESH))
                other = _half_rows(outs[j], ax, k ^ m, n, 1 - c)
                handed.append(pltpu.make_async_remote_copy(
                    src_ref=other, dst_ref=other, send_sem=pass_sems.at[j, m - 1], recv_sem=got_sems.at[j, m - 1],
                    device_id=(x, y, c), device_id_type=MESH))
        return local, sends, arrivals, passes, handed

    def start(self, ins, outs, sems):
        local, sends, _, _, _ = self._copies(ins, outs, sems)
        for cp in local + sends:
            cp.start()

    def relay(self, ins, outs, sems):
        _, _, arrivals, passes, _ = self._copies(ins, outs, sems)
        for arrived, onward in zip(arrivals, passes):
            arrived.wait_recv()
            onward.start()

    def wait(self, ins, outs, sems):
        local, sends, _, passes, handed = self._copies(ins, outs, sems)
        for cp in handed:
            cp.wait_recv()
        for cp in sends + passes:
            cp.wait_send()
        for cp in local:
            cp.wait()


class _ScatterPlan:
    def __init__(self, grads, axes):
        self.inputs, self.axes, nw = list(grads), list(axes), len(grads)
        self.shard_shapes = [(g.shape[0] // (4 if ax == 0 else 1), g.shape[1] // (4 if ax == 1 else 1))
                             for g, ax in zip(grads, axes)]
        self.out_shapes = [_sds((4,) + s, BF16) for s in self.shard_shapes]
        self.sem_shapes = [pltpu.SemaphoreType.DMA((nw,)), pltpu.SemaphoreType.DMA((nw, 3)), pltpu.SemaphoreType.DMA((nw, 3))]

    def _copies(self, ins, outs, sems):
        local_sems, send_sems, recv_sems = sems
        x, y, c = _place()
        k = 2 * x + y
        local, remote, arrivals = [], [], []
        for j, ax in enumerate(self.axes):
            n = self.shard_shapes[j][ax]
            local.append(pltpu.make_async_copy(_shard_ref(ins[j], ax, k, n), outs[j].at[0], local_sems.at[j]))
            for m in range(1, 4):
                remote.append(pltpu.make_async_remote_copy(
                    src_ref=_shard_ref(ins[j], ax, k ^ m, n), dst_ref=outs[j].at[m],
                    send_sem=send_sems.at[j, m - 1], recv_sem=recv_sems.at[j, m - 1],
                    device_id=_chip_peer(x, y, c, m), device_id_type=MESH))
                arrivals.append(pltpu.make_async_remote_copy(
                    src_ref=_shard_ref(ins[j], ax, k, n), dst_ref=outs[j].at[m],
                    send_sem=send_sems.at[j, m - 1], recv_sem=recv_sems.at[j, m - 1],
                    device_id=(x, y, c), device_id_type=MESH))
        return local, remote, arrivals

    def start(self, ins, outs, sems):
        local, remote, _ = self._copies(ins, outs, sems)
        for cp in local + remote:
            cp.start()

    def relay(self, ins, outs, sems):
        pass

    def wait(self, ins, outs, sems):
        local, remote, arrivals = self._copies(ins, outs, sems)
        for cp in arrivals:
            cp.wait_recv()
        for cp in remote:
            cp.wait_send()
        for cp in local:
            cp.wait()


def _run_plan(plan, name):
    nc = len(plan.inputs)

    def body(*refs):
        ins, outs, sems = refs[:nc], refs[nc:2 * nc], refs[2 * nc:]
        plan.start(ins, outs, sems)
        plan.relay(ins, outs, sems)
        plan.wait(ins, outs, sems)

    return pl.pallas_call(body, name=name, in_specs=[ANY] * nc, out_specs=[ANY] * nc, out_shape=list(plan.out_shapes),
                          scratch_shapes=list(plan.sem_shapes))(*plan.inputs)


class _SwapPlan:
    def __init__(self, parts):
        self.inputs, nw = list(parts), len(parts)
        self.out_shapes = [_sds(p.shape, p.dtype) for p in parts]
        self.sem_shapes = [pltpu.SemaphoreType.DMA((nw,)), pltpu.SemaphoreType.DMA((nw,))]

    def _copies(self, ins, outs, sems):
        send_sems, recv_sems = sems
        x, y, c = _place()
        return [pltpu.make_async_remote_copy(
            src_ref=ins[j], dst_ref=outs[j], send_sem=send_sems.at[j], recv_sem=recv_sems.at[j],
            device_id=(x, y, 1 - c), device_id_type=MESH) for j in range(len(ins))]

    def start(self, ins, outs, sems):
        for cp in self._copies(ins, outs, sems):
            cp.start()

    def relay(self, ins, outs, sems):
        pass

    def wait(self, ins, outs, sems):
        for cp in self._copies(ins, outs, sems):
            cp.wait()


class _SmallGatherPlan:
    def __init__(self, v):
        self.inputs = [v]
        self.out_shapes = [_sds((8,) + v.shape, v.dtype)]
        self.sem_shapes = [pltpu.SemaphoreType.DMA((1,)), pltpu.SemaphoreType.DMA((7,)), pltpu.SemaphoreType.DMA((7,))]

    def _copies(self, ins, outs, sems):
        (v_ref,), (out_ref,), (local_sem, send_sems, recv_sems) = ins, outs, sems
        x, y, c = _place()
        me = 4 * x + 2 * y + c
        local = pltpu.make_async_copy(v_ref, out_ref.at[me], local_sem.at[0])
        sends, arrivals = [], []
        for m in range(1, 8):
            px, py, pc = x ^ (m >> 2), y ^ ((m >> 1) & 1), c ^ (m & 1)
            sends.append(pltpu.make_async_remote_copy(
                src_ref=v_ref, dst_ref=out_ref.at[me], send_sem=send_sems.at[m - 1], recv_sem=recv_sems.at[m - 1],
                device_id=(px, py, pc), device_id_type=MESH))
            arrivals.append(pltpu.make_async_remote_copy(
                src_ref=v_ref, dst_ref=out_ref.at[4 * px + 2 * py + pc], send_sem=send_sems.at[m - 1],
                recv_sem=recv_sems.at[m - 1], device_id=(x, y, c), device_id_type=MESH))
        return local, sends, arrivals

    def start(self, ins, outs, sems):
        local, sends, _ = self._copies(ins, outs, sems)
        for cp in [local] + sends:
            cp.start()

    def relay(self, ins, outs, sems):
        pass

    def wait(self, ins, outs, sems):
        local, sends, arrivals = self._copies(ins, outs, sems)
        for cp in arrivals:
            cp.wait_recv()
        for cp in sends:
            cp.wait_send()
        local.wait()


class _PlanGroup:
    def __init__(self, plans):
        self.plans = [p for p in plans if p is not None]
        self.inputs = [a for p in self.plans for a in p.inputs]
        self.out_shapes = [s for p in self.plans for s in p.out_shapes]
        self.sem_shapes = [s for p in self.plans for s in p.sem_shapes]

    def _each(self, ins, outs, sems):
        i = s = 0
        for p in self.plans:
            n, ns = len(p.inputs), len(p.sem_shapes)
            yield p, ins[i:i + n], outs[i:i + n], sems[s:s + ns]
            i, s = i + n, s + ns

    def start(self, ins, outs, sems):
        for p, pi, po, ps in self._each(ins, outs, sems):
            p.start(pi, po, ps)

    def relay(self, ins, outs, sems):
        for p, pi, po, ps in self._each(ins, outs, sems):
            p.relay(pi, po, ps)

    def wait(self, ins, outs, sems):
        for p, pi, po, ps in self._each(ins, outs, sems):
            p.wait(pi, po, ps)

    def split(self, outs):
        res, i = [], 0
        for p in self.plans:
            res.append(outs[i:i + len(p.inputs)])
            i += len(p.inputs)
        return res


BIG = ("w_ffn1_in", "w_ffn1_out", "w_in", "w_pool_branch", "w_attn_branch", "w_out", "w_ffn2_in", "w_ffn2_out")
BIG_AXIS = {"w_ffn1_in": 1, "w_ffn1_out": 0, "w_in": 1, "w_pool_branch": 1, "w_attn_branch": 1, "w_out": 0,
            "w_ffn2_in": 1, "w_ffn2_out": 0}


class _Sharded:
    fused_scatter = True

    def __init__(self, shards):
        self.shards, self.full, self.recv = shards, {}, {}

    def gather_plan(self, names):
        return _GatherPlan([self.shards[n] for n in names], [BIG_AXIS[n.split("/")[0]] for n in names])

    def gather_now(self, names):
        self.gathered(names, _run_plan(self.gather_plan(names), "gather_" + names[0]))

    def gathered(self, names, outs):
        self.full.update(zip(names, outs))

    def scatter_plan(self, names, grads):
        return _ScatterPlan([grads[n] for n in names], [BIG_AXIS[n] for n in names])

    def scatter_now(self, names, grads):
        self.scattered(names, _run_plan(self.scatter_plan(names, grads), "scatter_" + names[0]))

    def scattered(self, names, outs):
        self.recv.update(zip(names, outs))


class _Whole:
    fused_scatter = False

    def __init__(self, full):
        self.full, self.recv = dict(full), {}

    def gather_plan(self, names):
        return None

    def gather_now(self, names):
        pass

    def gathered(self, names, outs):
        pass

    def scatter_plan(self, names, grads):
        return None

    def scatter_now(self, names, grads):
        pass

    def scattered(self, names, outs):
        pass


def _vec(rows):
    pad = [jnp.zeros((1, D), F32)] * (8 - len(rows))
    return jnp.concatenate([r.reshape(1, D) for r in rows] + pad, axis=0)


def _block_diag(w_pool):
    n, c = w_pool.shape[0], w_pool.shape[1]
    eye = jnp.eye(n, dtype=w_pool.dtype)
    return (eye[:, None, :, None] * w_pool[:, :, None, :]).reshape(n * c, n * c)


def _example_step(x, tgt, positions, mod, gains, w_pool, pool_scale, ws, pack=None):
    T = x.shape[0]
    assert (T // BLK // DIL[-1]) & (T // BLK // DIL[-1] - 1) == 0, "blocks per sequence must be a power of two"
    sh1, sc1, gt1, sh2, sc2, gt2, sh3, sc3, gt3 = [mod[j * D:(j + 1) * D] for j in range(NMOD)]
    g1, g2, g3, gf = gains
    vec1, vec2, vec3 = _vec([g1, sh1, sc1, gt1]), _vec([g2, sh2, sc2, gt2]), _vec([g3, sh3, sc3, gt3])
    inv_freq = 10000.0 ** (-jnp.arange(0, HD, 2, dtype=F32) / HD)
    ang = positions.astype(F32)[:, None] * inv_freq
    cos = jnp.tile(jnp.cos(ang), (1, 4))
    sin = jnp.tile(jnp.concatenate([-jnp.sin(ang), jnp.sin(ang)], axis=1), (1, 2))
    wp_bd = _block_diag(w_pool).astype(BF16)
    ones_bd = _block_diag(jnp.ones((NH, HD, HD), F32)).astype(BF16)
    ps = jnp.concatenate([pool_scale.reshape(1, PW), jnp.zeros((7, PW), F32)], axis=0)
    wb = ws.full

    if "w_ffn1_in" not in wb:
        ws.gather_now(["w_ffn1_in"])
    (u1, a1, b1), got = _ffn_ab(x, vec1, [wb["w_ffn1_in"]], "ffn1_ab", ws.gather_plan(["w_ffn1_out", "w_in"]))
    ws.gathered(["w_ffn1_out", "w_in"], got)
    mixw = ["w_pool_branch", "w_attn_branch", "w_out"]
    (h1, f1), got = _ffn_out(x, a1, b1, vec1, wb["w_ffn1_out"], "ffn1_out", ws.gather_plan(mixw))
    ws.gathered(mixw, got)
    (u2, p, qs, ks, vs, gates), got = _mix_proj(h1, vec2, wb["w_in"], cos, sin, ws.gather_plan(["w_ffn2_in/0"]))
    ws.gathered(["w_ffn2_in/0"], got)
    qs, ks, vs = [_flat(t) for t in qs], [_flat(t) for t in ks], [_flat(t) for t in vs]
    nbs = [T // d // BLK for d in DIL]
    os, lses = [], []
    for gi in range(NG):
        (o, lse), _ = _attn_fwd(qs[gi], ks[gi], vs[gi], nbs[gi], f"attn_fwd{gi}")
        os.append(o)
        lses.append(lse)
    os_r = [_by_residue(t, d) for t, d in zip(os, DIL)]
    lses_r = [_by_residue(t, d) for t, d in zip(lses, DIL)]
    ffn2w = ["w_ffn2_in/1", "w_ffn2_out"]
    (h2, ypool, yattn, merged, mixout, dpool), got = _mix_merge(
        h1, vec2, p, os_r, lses_r, gates, wp_bd, ps, wb["w_pool_branch"], wb["w_attn_branch"], wb["w_out"],
        ws.gather_plan(ffn2w))
    ws.gathered(ffn2w, got)
    win3 = [wb["w_ffn2_in/0"], wb["w_ffn2_in/1"]] if "w_ffn2_in/0" in wb else [wb["w_ffn2_in"]]
    (dh3, u3, a3, b3, f3, lacc), _ = _ffn_fwd(h2, vec3, win3, wb["w_ffn2_out"], "ffn2_fwd", head=(tgt, _vec([gf])))
    loss = 0.5 * jnp.sum(lacc[0]) / D

    grads = {}

    def wgrad_cols(name, xx, yy, riders, extra=None):
        group = _PlanGroup([ws.scatter_plan(riders, grads) if riders else None, extra])
        plan = group if group.plans else None
        if ws.fused_scatter:
            blocks, got = _wgrad_scatter(xx, yy, "wg_" + name, min(2048, T // 2), comm=plan)
            ws.scattered([name], [blocks])
        else:
            grads[name], got = _wgrad(xx, yy, "wg_" + name, D, 512, 1024, comm=plan)
        parts = group.split(got)
        if len(parts) > (extra is not None):
            ws.scattered(riders, parts[0])
        return parts[-1] if extra is not None else None

    (dh2, dab3, s3, df3, acc3), _ = _ffn_bwd(dh3, h2, a3, b3, f3, vec3, win3, wb["w_ffn2_out"], "ffn2_bwd")
    grads["w_ffn2_out"], _ = _wgrad(s3, df3, "wg_ffn2_out", FF // 2, 512, min(4096, T // 2))
    wgrad_cols("w_ffn2_in", u3, dab3, ["w_ffn2_out"])
    (dgates, do0, do1, do2, e0, e1, e2, dd, acc2a, accps,
     grads["w_out"], grads["w_pool_branch"], grads["w_attn_branch"], gwp), _ = _mix_bwd_a(
        dh2, vec2, mixout, merged, gates, ypool, yattn, dpool, os_r, lses_r, wp_bd, ps,
        wb["w_pool_branch"], wb["w_attn_branch"], wb["w_out"], ones_bd)
    n = len(POOL_WINDOWS)
    c = PW // n
    grad_w_pool = jnp.stack([gwp[j * c:(j + 1) * c, j * c:(j + 1) * c] for j in range(n)], axis=0)
    small3 = ["w_out", "w_pool_branch", "w_attn_branch"]
    dqs, dks, dvs = [], [], []
    for gi, (do, e) in enumerate(((do0, e0), (do1, e1), (do2, e2))):
        plan = ws.scatter_plan(small3, grads) if gi == 0 else None
        (dq, dk, dv), got = _attn_bwd(qs[gi], ks[gi], vs[gi], _flat(do), lses[gi], _flat(e), nbs[gi], f"attn_bwd{gi}", plan)
        if gi == 0:
            ws.scattered(small3, got)
        dqs.append(_by_residue(dq, DIL[gi]))
        dks.append(_by_residue(dk, DIL[gi]))
        dvs.append(_by_residue(dv, DIL[gi]))
    dh1, dproj, acc2b = _mix_bwd_b(dh2, h1, vec2, dd, dqs, dks, dvs, dgates, cos, sin, wb["w_in"])
    wgrad_cols("w_in", u2, dproj, [])
    (dx, dab1, s1, df1, acc1), _ = _ffn_bwd(dh1, x, a1, b1, f1, vec1, [wb["w_ffn1_in"]], wb["w_ffn1_out"], "ffn1_bwd")
    grads["w_ffn1_out"], _ = _wgrad(s1, df1, "wg_ffn1_out", FF // 2, 512, min(4096, T // 2))
    dmod = jnp.concatenate([acc1[0], acc1[1], acc1[3], acc2b[0], acc2b[1], acc2a[3], acc3[0], acc3[1], acc3[3]])
    dgains = jnp.stack([acc1[2], acc2b[2], acc3[2], lacc[1]], axis=0)
    row = None if pack is None else _SmallGatherPlan(pack(loss, dmod, dgains, grad_w_pool, accps[0]))
    rows = wgrad_cols("w_ffn1_in", u1, dab1, ["w_ffn1_out"], row)
    return loss, dx, dmod, dgains, grad_w_pool, accps[0], grads, None if rows is None else rows[0]


SMALL = ("b_ada", "g_norm_ffn1", "g_norm_mix", "g_norm_ffn2", "g_final", "pool_scale", "w_pool")
WEIGHTS = ("w_ada", "b_ada", "g_norm_ffn1", "w_ffn1_in", "w_ffn1_out", "g_norm_mix", "w_in", "w_pool", "pool_scale",
           "w_pool_branch", "w_attn_branch", "w_out", "g_norm_ffn2", "w_ffn2_in", "w_ffn2_out", "g_final")


def _pack_small(t):
    return jnp.concatenate([t[n].reshape(-1) for n in SMALL]).reshape(1, -1)


def kernel(x, c, positions, w_ada, b_ada, g_norm_ffn1, w_ffn1_in, w_ffn1_out, g_norm_mix, w_in, w_pool, pool_scale, w_pool_branch, w_attn_branch, w_out, g_norm_ffn2, w_ffn2_in, w_ffn2_out, g_final, loss_target, m_w_ada, m_b_ada, m_g_norm_ffn1, m_w_ffn1_in, m_w_ffn1_out, m_g_norm_mix, m_w_in, m_w_pool, m_pool_scale, m_w_pool_branch, m_w_attn_branch, m_w_out, m_g_norm_ffn2, m_w_ffn2_in, m_w_ffn2_out, m_g_final, v_w_ada, v_b_ada, v_g_norm_ffn1, v_w_ffn1_in, v_w_ffn1_out, v_g_norm_mix, v_w_in, v_w_pool, v_pool_scale, v_w_pool_branch, v_w_attn_branch, v_w_out, v_g_norm_ffn2, v_w_ffn2_in, v_w_ffn2_out, v_g_final):
    w = dict(w_ada=w_ada, b_ada=b_ada, g_norm_ffn1=g_norm_ffn1, w_ffn1_in=w_ffn1_in, w_ffn1_out=w_ffn1_out,
             g_norm_mix=g_norm_mix, w_in=w_in, w_pool=w_pool, pool_scale=pool_scale, w_pool_branch=w_pool_branch,
             w_attn_branch=w_attn_branch, w_out=w_out, g_norm_ffn2=g_norm_ffn2, w_ffn2_in=w_ffn2_in,
             w_ffn2_out=w_ffn2_out, g_final=g_final)
    mom = dict(w_ada=m_w_ada, b_ada=m_b_ada, g_norm_ffn1=m_g_norm_ffn1, w_ffn1_in=m_w_ffn1_in, w_ffn1_out=m_w_ffn1_out,
               g_norm_mix=m_g_norm_mix, w_in=m_w_in, w_pool=m_w_pool, pool_scale=m_pool_scale,
               w_pool_branch=m_w_pool_branch, w_attn_branch=m_w_attn_branch, w_out=m_w_out, g_norm_ffn2=m_g_norm_ffn2,
               w_ffn2_in=m_w_ffn2_in, w_ffn2_out=m_w_ffn2_out, g_final=m_g_final)
    var = dict(w_ada=v_w_ada, b_ada=v_b_ada, g_norm_ffn1=v_g_norm_ffn1, w_ffn1_in=v_w_ffn1_in, w_ffn1_out=v_w_ffn1_out,
               g_norm_mix=v_g_norm_mix, w_in=v_w_in, w_pool=v_w_pool, pool_scale=v_pool_scale,
               w_pool_branch=v_w_pool_branch, w_attn_branch=v_w_attn_branch, w_out=v_w_out, g_norm_ffn2=v_g_norm_ffn2,
               w_ffn2_in=v_w_ffn2_in, w_ffn2_out=v_w_ffn2_out, g_final=v_g_final)
    ix, iy, ic = _place()
    chip = 2 * ix + iy
    me = 4 * ix + 2 * iy + ic
    nada = w_ada.shape[2]

    shards = {n: w[n][0].astype(BF16) for n in BIG}
    half = D // 2
    shards["w_ffn2_in/0"], shards["w_ffn2_in/1"] = shards["w_ffn2_in"][:half], shards["w_ffn2_in"][half:]
    ws = _Sharded(shards)
    c_all = _run_plan(_SmallGatherPlan(c), "gather_c")[0][:, 0, :]
    b_shard = lax.dynamic_slice_in_dim(b_ada, chip * nada, nada, axis=1)
    mod_cols = _ada_fwd(c_all, w_ada[0], b_shard)
    first = _PlanGroup([_SmallGatherPlan(mod_cols), ws.gather_plan(["w_ffn1_in"])])
    (mod_all,), ffn1 = first.split(_run_plan(first, "gather_first"))
    ws.gathered(["w_ffn1_in"], ffn1)
    mod = jnp.concatenate([lax.dynamic_index_in_dim(mod_all[4 * (kk >> 1) + 2 * (kk & 1)], me, axis=0, keepdims=False)
                           for kk in range(4)])

    def pack(loss, dmod, dgains, g_w_pool, g_pool_scale):
        small_g = dict(b_ada=dmod, g_norm_ffn1=dgains[0], g_norm_mix=dgains[1], g_norm_ffn2=dgains[2],
                       g_final=dgains[3], pool_scale=g_pool_scale, w_pool=g_w_pool)
        return jnp.concatenate([_pack_small(small_g), jnp.pad(loss.reshape(1, 1), ((0, 0), (0, 127)))], axis=1)

    _, dx, _, _, _, _, _, gathered = _example_step(
        x[0], loss_target[0], positions[0], mod, (g_norm_ffn1[0], g_norm_mix[0], g_norm_ffn2[0], g_final),
        w_pool[0], pool_scale[0], ws, pack)

    per_weight, loss_tile = _adam_small(*[[t[n].reshape(1, -1) for n in SMALL] for t in (w, mom, var)], gathered)
    small_out = [{n: per_weight[j][kind].reshape(w[n].shape) for j, n in enumerate(SMALL)} for kind in range(4)]
    loss = loss_tile[0, 0]

    dmod_all = gathered[:, 0, :NMOD * D]
    dmod_cols = lax.dynamic_slice_in_dim(dmod_all, chip * nada, nada, axis=1)
    g_ada = _ada_bwd(c_all, dmod_cols)

    ada_out = _adam(w_ada[0], m_w_ada[0], v_w_ada[0], [g_ada], "adam_w_ada")[0]

    sums = {n: _sum4(ws.recv[n], "sum_" + n) for n in BIG}
    other = dict(zip(BIG, _run_plan(_SwapPlan([sums[n] for n in BIG]), "swap_sibling")))
    big_out = {}
    for n in BIG:
        if sums[n].shape[0] < w[n].shape[1]:
            big_out[n] = _adam_halves(w[n][0], mom[n][0], var[n][0], sums[n], other[n], "adam_" + n)
        else:
            big_out[n] = _adam(w[n][0], mom[n][0], var[n][0], [sums[n], other[n]], "adam_" + n)[0]

    def leaf(kind, n):
        if n == "w_ada":
            return ada_out[kind][None]
        if n in big_out:
            return big_out[n][kind][None]
        return small_out[kind][n]

    return (loss, dx[None], *[leaf(kind, n) for kind in range(4) for n in WEIGHTS])
```

```python
import jax
import jax.numpy as jnp
from jax import lax
from jax.experimental import pallas as pl
from jax.experimental.pallas import tpu as pltpu

F32 = jnp.float32
BF16 = jnp.bfloat16

D = 1024
FF = 2816
FC = FF
PW = 256
GA = 256
HD = 64
LANES = 128
NH = GA // HD
NG = 3
DIL = (1, 4, 16)
BLK = 128
FWD_BLOCKS = 8
BWD_BLOCKS = 8
GW = 2 * D
INW = PW + 3 * NG * GA + GW
NMOD = 9
POOL_WINDOWS = (2, 4, 8, 16)
HALO = 16
EPS = 1e-6
SCALE = HD ** -0.5
NEG = -1e30

LR, B1, B2, AEPS, WD, STEP = 0.001, 0.9, 0.999, 1e-08, 0.01, 10

VMEM_BIG = 56 * 1024 * 1024
TM = 256

MESH = pl.DeviceIdType.MESH
ANY = pl.BlockSpec(memory_space=pl.ANY)


def _call(body, name, grid, in_specs, out_specs, out_shape, scratch=(), vmem=None, comm=None):
    params = pltpu.CompilerParams(dimension_semantics=("arbitrary",) * len(grid), vmem_limit_bytes=vmem)
    n_in, n_out, n_scr = len(in_specs), len(out_shape), len(scratch)
    if comm is None:
        call = pl.pallas_call(body, name=name, grid=grid, in_specs=list(in_specs), out_specs=list(out_specs),
                              out_shape=list(out_shape), scratch_shapes=list(scratch), compiler_params=params)
        return lambda *args: (call(*args), ())
    nc = len(comm.inputs)

    def body_with_comm(*refs):
        ins, refs = refs[:n_in], refs[n_in:]
        c_ins, refs = refs[:nc], refs[nc:]
        outs, refs = refs[:n_out], refs[n_out:]
        c_outs, refs = refs[:nc], refs[nc:]
        scr, sems = refs[:n_scr], refs[n_scr:]
        first = pl.program_id(0) == 0
        last = pl.program_id(0) == grid[0] - 1
        for ax in range(1, len(grid)):
            first = jnp.logical_and(first, pl.program_id(ax) == 0)
            last = jnp.logical_and(last, pl.program_id(ax) == grid[ax] - 1)

        @pl.when(first)
        def _():
            comm.start(c_ins, c_outs, sems)

        body(*ins, *outs, *scr)
        early_relay = len(grid) == 1 and grid[0] >= 4
        if early_relay:
            @pl.when(pl.program_id(0) == (3 * grid[0]) // 4)
            def _():
                comm.relay(c_ins, c_outs, sems)

        @pl.when(last)
        def _():
            if not early_relay:
                comm.relay(c_ins, c_outs, sems)
            comm.wait(c_ins, c_outs, sems)

    call = pl.pallas_call(
        body_with_comm, name=name, grid=grid, in_specs=list(in_specs) + [ANY] * nc,
        out_specs=list(out_specs) + [ANY] * nc, out_shape=list(out_shape) + list(comm.out_shapes),
        scratch_shapes=list(scratch) + list(comm.sem_shapes), compiler_params=params)

    def run(*args):
        res = call(*args, *comm.inputs)
        return res[:n_out], res[n_out:]

    return run


def _rows(tm, n):
    return pl.BlockSpec((tm, n), lambda i: (i, 0))


def _const(shape):
    return pl.BlockSpec(shape, lambda i: (0,) * len(shape))


def _sds(shape, dtype):
    return jax.ShapeDtypeStruct(shape, dtype)


def _dot(a, b):
    return jnp.dot(a, b, preferred_element_type=F32)


def _dot_nt(a, b):
    return lax.dot_general(a, b, (((1,), (1,)), ((), ())), preferred_element_type=F32)


def _dot_tn(a, b):
    return lax.dot_general(a, b, (((0,), (0,)), ((), ())), preferred_element_type=F32)


def _colsum(v):
    return jnp.sum(v, axis=0, keepdims=True)


def _norm_fwd(h, g, sh, sc):
    r = lax.rsqrt(jnp.mean(h * h, axis=-1, keepdims=True) + EPS)
    xh = h * r
    n = xh * g
    return xh, r, n, n * (1.0 + sc) + sh


def _norm_bwd(du, xh, r, n, g, sc):
    dn = du * (1.0 + sc)
    dxh = dn * g
    dh = r * (dxh - xh * jnp.mean(dxh * xh, axis=-1, keepdims=True))
    return dh, _colsum(du), _colsum(du * n), _colsum(dn * xh)


def _load_once(pairs, sems):
    @pl.when(pl.program_id(0) == 0)
    def _():
        cps = [pltpu.make_async_copy(src, dst, sems.at[j]) for j, (src, dst) in enumerate(pairs)]
        for cp in cps:
            cp.start()
        for cp in cps:
            cp.wait()


def _zero_first(ref):
    @pl.when(pl.program_id(0) == 0)
    def _():
        ref[...] = jnp.zeros(ref.shape, ref.dtype)


def _row_chunks(hbm_refs, vmem_ref):
    pairs, row = [], 0
    for ref in hbm_refs:
        pairs.append((ref, vmem_ref.at[pl.ds(row, ref.shape[0]), :]))
        row += ref.shape[0]
    return pairs


def _loss_head(hh, tgt, g):
    r = lax.rsqrt(jnp.mean(hh * hh, axis=-1, keepdims=True) + EPS)
    xh = hh * r
    err = xh * g - tgt
    dy = err * (1.0 / D)
    dxh = dy * g
    dh = r * (dxh - xh * jnp.mean(dxh * xh, axis=-1, keepdims=True))
    return dh, _colsum(err * err), _colsum(dy * xh)


def _ffn_fwd(h, vec, wins, wout, name, comm=None, head=None):
    T = h.shape[0]
    nwin = len(wins)
    nhead = 0 if head is None else 2

    def body(h_ref, vec_ref, *rest):
        head_refs, rest = rest[:nhead], rest[nhead:]
        win_hbms, rest = rest[:nwin], rest[nwin:]
        (wout_hbm, ho_ref, u_ref, a_ref, b_ref, f_ref), rest = rest[:6], rest[6:]
        lacc_refs, (win_v, wout_v, sems) = rest[:nhead // 2], rest[nhead // 2:]
        _load_once(_row_chunks(win_hbms, win_v) + [(wout_hbm, wout_v)], sems)
        hh = h_ref[...]
        g, sh, sc, gt = vec_ref[0:1, :], vec_ref[1:2, :], vec_ref[2:3, :], vec_ref[3:4, :]
        _, _, _, u = _norm_fwd(hh, g, sh, sc)
        ub = u.astype(BF16)
        u_ref[...] = ub
        acc = None
        for j in range(FF // FC):
            lo, hi = j * FC, (j + 1) * FC
            a = _dot(ub, win_v[:, lo:hi])
            b = _dot(ub, win_v[:, FF + lo:FF + hi])
            a_ref[:, lo:hi] = a.astype(BF16)
            b_ref[:, lo:hi] = b.astype(BF16)
            s = (a * jax.nn.sigmoid(a) * b).astype(BF16)
            part = _dot(s, wout_v[lo:hi, :])
            acc = part if acc is None else acc + part
        f_ref[...] = acc.astype(BF16)
        ho = hh + 0.5 * gt * acc
        if head is None:
            ho_ref[...] = ho
        else:
            _zero_first(lacc_refs[0])
            dh, sq, dg = _loss_head(ho, head_refs[0][...], head_refs[1][0:1, :])
            ho_ref[...] = dh
            lacc_refs[0][0:1, :] += sq
            lacc_refs[0][1:2, :] += dg

    head_specs = [] if head is None else [_rows(TM, D), _const((8, D))]
    lacc_spec = [] if head is None else [_const((8, D))]
    lacc_shape = [] if head is None else [_sds((8, D), F32)]
    return _call(
        body, name, (T // TM,),
        [_rows(TM, D), _const((8, D))] + head_specs + [ANY] * (nwin + 1),
        [_rows(TM, D), _rows(TM, D), _rows(TM, FF), _rows(TM, FF), _rows(TM, D)] + lacc_spec,
        [_sds((T, D), F32), _sds((T, D), BF16), _sds((T, FF), BF16), _sds((T, FF), BF16), _sds((T, D), BF16)] + lacc_shape,
        scratch=[pltpu.VMEM((D, 2 * FF), BF16), pltpu.VMEM((FF, D), BF16), pltpu.SemaphoreType.DMA((nwin + 1,))],
        vmem=VMEM_BIG, comm=comm,
    )(h, vec, *([] if head is None else head), *wins, wout)


def _ffn_ab(h, vec, wins, name, comm=None):
    T = h.shape[0]
    nwin = len(wins)

    def body(h_ref, vec_ref, *rest):
        win_hbms, (u_ref, a_ref, b_ref, win_v, sems) = rest[:nwin], rest[nwin:]
        _load_once(_row_chunks(win_hbms, win_v), sems)
        g, sh, sc = vec_ref[0:1, :], vec_ref[1:2, :], vec_ref[2:3, :]
        _, _, _, u = _norm_fwd(h_ref[...], g, sh, sc)
        ub = u.astype(BF16)
        u_ref[...] = ub
        for j in range(FF // FC):
            lo, hi = j * FC, (j + 1) * FC
            a_ref[:, lo:hi] = _dot(ub, win_v[:, lo:hi]).astype(BF16)
            b_ref[:, lo:hi] = _dot(ub, win_v[:, FF + lo:FF + hi]).astype(BF16)

    return _call(
        body, name, (T // TM,),
        [_rows(TM, D), _const((8, D))] + [ANY] * nwin,
        [_rows(TM, D), _rows(TM, FF), _rows(TM, FF)],
        [_sds((T, D), BF16), _sds((T, FF), BF16), _sds((T, FF), BF16)],
        scratch=[pltpu.VMEM((D, 2 * FF), BF16), pltpu.SemaphoreType.DMA((nwin,))],
        vmem=VMEM_BIG, comm=comm,
    )(h, vec, *wins)


def _ffn_out(h, a, b, vec, wout, name, comm=None):
    T = h.shape[0]

    def body(h_ref, a_ref, b_ref, vec_ref, wout_hbm, ho_ref, f_ref, wout_v, sems):
        _load_once([(wout_hbm, wout_v)], sems)
        gt = vec_ref[3:4, :]
        acc = None
        for j in range(FF // FC):
            lo, hi = j * FC, (j + 1) * FC
            av = a_ref[:, lo:hi].astype(F32)
            s = (av * jax.nn.sigmoid(av) * b_ref[:, lo:hi].astype(F32)).astype(BF16)
            part = _dot(s, wout_v[lo:hi, :])
            acc = part if acc is None else acc + part
        f_ref[...] = acc.astype(BF16)
        ho_ref[...] = h_ref[...] + 0.5 * gt * acc

    return _call(
        body, name, (T // TM,),
        [_rows(TM, D), _rows(TM, FF), _rows(TM, FF), _const((8, D)), ANY],
        [_rows(TM, D), _rows(TM, D)],
        [_sds((T, D), F32), _sds((T, D), BF16)],
        scratch=[pltpu.VMEM((FF, D), BF16), pltpu.SemaphoreType.DMA((1,))],
        vmem=VMEM_BIG, comm=comm,
    )(h, a, b, vec, wout)


def _ffn_bwd(dh, h, a, b, f, vec, wins, wout, name, comm=None):
    T = h.shape[0]
    nwin = len(wins)

    def body(dh_ref, h_ref, a_ref, b_ref, f_ref, vec_ref, *rest):
        win_hbms, (wout_hbm, dhi_ref, dab_ref, s_ref, df_ref, acc_ref, win_v, wout_v, sems) = rest[:nwin], rest[nwin:]
        _load_once(_row_chunks(win_hbms, win_v) + [(wout_hbm, wout_v)], sems)
        _zero_first(acc_ref)
        g, sh, sc, gt = vec_ref[0:1, :], vec_ref[1:2, :], vec_ref[2:3, :], vec_ref[3:4, :]
        dho = dh_ref[...]
        df = (0.5 * gt * dho).astype(BF16)
        df_ref[...] = df
        dgt = _colsum(0.5 * dho * f_ref[...].astype(F32))
        du = None
        for j in range(FF // FC):
            lo, hi = j * FC, (j + 1) * FC
            av = a_ref[:, lo:hi].astype(F32)
            bv = b_ref[:, lo:hi].astype(F32)
            ds = _dot_nt(df, wout_v[lo:hi, :])
            sig = jax.nn.sigmoid(av)
            sa = av * sig
            s_ref[:, lo:hi] = (sa * bv).astype(BF16)
            da = (ds * bv * (sig * (1.0 + av * (1.0 - sig)))).astype(BF16)
            db = (ds * sa).astype(BF16)
            dab_ref[:, lo:hi] = da
            dab_ref[:, FF + lo:FF + hi] = db
            part = _dot_nt(da, win_v[:, lo:hi]) + _dot_nt(db, win_v[:, FF + lo:FF + hi])
            du = part if du is None else du + part
        xh, r, n, _ = _norm_fwd(h_ref[...], g, sh, sc)
        dhn, dsh, dsc, dg = _norm_bwd(du, xh, r, n, g, sc)
        dhi_ref[...] = dho + dhn
        acc_ref[0:1, :] += dsh
        acc_ref[1:2, :] += dsc
        acc_ref[2:3, :] += dg
        acc_ref[3:4, :] += dgt

    return _call(
        body, name, (T // TM,),
        [_rows(TM, D), _rows(TM, D), _rows(TM, FF), _rows(TM, FF), _rows(TM, D), _const((8, D))] + [ANY] * (nwin + 1),
        [_rows(TM, D), _rows(TM, 2 * FF), _rows(TM, FF), _rows(TM, D), _const((8, D))],
        [_sds((T, D), F32), _sds((T, 2 * FF), BF16), _sds((T, FF), BF16), _sds((T, D), BF16), _sds((8, D), F32)],
        scratch=[pltpu.VMEM((D, 2 * FF), BF16), pltpu.VMEM((FF, D), BF16), pltpu.SemaphoreType.DMA((nwin + 1,))],
        vmem=VMEM_BIG, comm=comm,
    )(dh, h, a, b, f, vec, *wins, wout)


def _wgrad(x, y, name, tk, tn, tt, out_dtype=BF16, comm=None):
    T, K = x.shape
    N = y.shape[1]
    nt = T // tt

    def body(x_ref, y_ref, o_ref, acc_ref):
        t = pl.program_id(2)
        part = _dot_tn(x_ref[...], y_ref[...])

        @pl.when(t == 0)
        def _():
            acc_ref[...] = part

        @pl.when(t > 0)
        def _():
            acc_ref[...] += part

        @pl.when(t == nt - 1)
        def _():
            o_ref[...] = acc_ref[...].astype(out_dtype)

    (out,), c_outs = _call(
        body, name, (K // tk, N // tn, nt),
        [pl.BlockSpec((tt, tk), lambda i, j, t: (t, i)), pl.BlockSpec((tt, tn), lambda i, j, t: (t, j))],
        [pl.BlockSpec((tk, tn), lambda i, j, t: (i, j))], [_sds((K, N), out_dtype)],
        scratch=[pltpu.VMEM((tk, tn), F32)], vmem=VMEM_BIG, comm=comm,
    )(x, y)
    return out, c_outs


def _wgrad_scatter(x, y, name, tt, comm=None):
    T, K = x.shape
    n = y.shape[1] // 4
    nt = T // tt
    assert nt >= 2, "a block's hand-over is added one grid step into the next block"
    half = K // 2
    nc = 0 if comm is None else len(comm.inputs)

    def body(chip_ref, x_ref, y_ref, *refs):
        c_ins, refs = refs[:nc], refs[nc:]
        recv_ref, refs = refs[0], refs[1:]
        c_outs, refs = refs[:nc], refs[nc:]
        acc_ref, keep_ref, give_ref, take_ref, local_sem, give_sems, take_sems, send_sems, recv_sems = refs[:9]
        j, t = pl.program_id(0), pl.program_id(1)
        px, py, pc = _place()

        def hand_over(jj):
            return pltpu.make_async_remote_copy(
                src_ref=give_ref.at[jj], dst_ref=take_ref.at[jj], send_sem=give_sems.at[jj], recv_sem=take_sems.at[jj],
                device_id=(px, py, 1 - pc), device_id_type=MESH)

        def send(jj):
            m = (3, 1, 2)[jj]
            return pltpu.make_async_remote_copy(
                src_ref=keep_ref.at[jj], dst_ref=recv_ref.at[m], send_sem=send_sems.at[jj], recv_sem=recv_sems.at[jj],
                device_id=_chip_peer(px, py, pc, m), device_id_type=MESH)

        def add_sibling(jj):
            hand_over(jj).wait_recv()
            keep_ref[jj] = (keep_ref[jj].astype(F32) + take_ref[jj].astype(F32)).astype(BF16)

        if comm is not None:
            @pl.when(jnp.logical_and(j == 0, t == 0))
            def _():
                comm.start(c_ins, c_outs, refs[9:])

        part = _dot_tn(x_ref[...], y_ref[...])

        @pl.when(t == 0)
        def _():
            acc_ref[...] = part

        @pl.when(t > 0)
        def _():
            acc_ref[...] += part

        for jj in range(3):
            @pl.when(jnp.logical_and(j == jj + 1, t == 0))
            def _():
                add_sibling(jj)
                send(jj).start()

        for jj in range(4):
            @pl.when(jnp.logical_and(j == jj, t == nt - 1))
            def _():
                keep_ref[jj] = acc_ref[pl.ds(pl.multiple_of(pc * half, 16), half), :].astype(BF16)
                give_ref[jj] = acc_ref[pl.ds(pl.multiple_of((1 - pc) * half, 16), half), :].astype(BF16)
                hand_over(jj).start()

        @pl.when(jnp.logical_and(j == 3, t == nt - 1))
        def _():
            add_sibling(3)
            own = pltpu.make_async_copy(keep_ref.at[3], recv_ref.at[0], local_sem.at[0])
            own.start()
            for jj in range(3):
                send(jj).wait_recv()
            for jj in range(3):
                send(jj).wait_send()
            for jj in range(4):
                hand_over(jj).wait_send()
            own.wait()
            if comm is not None:
                comm.relay(c_ins, c_outs, refs[9:])
                comm.wait(c_ins, c_outs, refs[9:])

    grid_spec = pltpu.PrefetchScalarGridSpec(
        num_scalar_prefetch=1, grid=(4, nt),
        in_specs=[pl.BlockSpec((tt, K), lambda j, t, chip: (t, 0)),
                  pl.BlockSpec((tt, n), lambda j, t, chip: (t, chip[0] ^ jnp.where(j == 0, 3, jnp.where(j == 3, 0, j))))]
        + [ANY] * nc,
        out_specs=[ANY] * (1 + nc),
        scratch_shapes=[pltpu.VMEM((K, n), F32)] + [pltpu.VMEM((4, half, n), BF16)] * 3
        + [pltpu.SemaphoreType.DMA((1,))] + [pltpu.SemaphoreType.DMA((4,))] * 2 + [pltpu.SemaphoreType.DMA((3,))] * 2
        + ([] if comm is None else list(comm.sem_shapes)))
    px, py, _ = _place()
    res = pl.pallas_call(
        body, name=name, grid_spec=grid_spec,
        out_shape=[_sds((4, half, n), BF16)] + ([] if comm is None else list(comm.out_shapes)),
        compiler_params=pltpu.CompilerParams(dimension_semantics=("arbitrary", "arbitrary"), vmem_limit_bytes=VMEM_BIG),
    )((2 * px + py).astype(jnp.int32).reshape(1), x, y, *([] if comm is None else comm.inputs))
    return res[0], res[1:]


def _swap_halves(t):
    w = t.shape[1]
    lane = lax.broadcasted_iota(jnp.int32, t.shape, 1)
    return jnp.where(lane % HD < HD // 2, pltpu.roll(t, w - HD // 2, 1), pltpu.roll(t, HD // 2, 1))


def _rope(t, cos, sin_signed):
    c = jnp.tile(cos, (1, t.shape[1] // cos.shape[1]))
    s = jnp.tile(sin_signed, (1, t.shape[1] // sin_signed.shape[1]))
    return t * c + _swap_halves(t) * s


def _rope_bwd(dt, cos, sin_signed):
    c = jnp.tile(cos, (1, dt.shape[1] // cos.shape[1]))
    s = jnp.tile(sin_signed, (1, dt.shape[1] // sin_signed.shape[1]))
    return dt * c + _swap_halves(dt * s)


def _rm_spec(dil):
    return pl.BlockSpec((dil, TM // dil, GA), lambda i: (0, i, 0))


def _to_residues(t, dst_ref, scr_ref, dil):
    if dil == 1:
        dst_ref[0] = t.astype(dst_ref.dtype)
        return
    for j in range(GA // LANES):
        scr_ref[j] = t[:, j * LANES:(j + 1) * LANES]
    for r in range(dil):
        for j in range(GA // LANES):
            rows = scr_ref.at[j][pl.ds(r, TM // dil, stride=dil), :]
            dst_ref[r, :, j * LANES:(j + 1) * LANES] = rows.astype(dst_ref.dtype)


def _from_residues(src_ref, scr_ref, dil):
    if dil == 1:
        return src_ref[0].astype(F32)
    for r in range(dil):
        for j in range(GA // LANES):
            scr_ref.at[j][pl.ds(r, TM // dil, stride=dil), :] = src_ref[r, :, j * LANES:(j + 1) * LANES].astype(F32)
    return jnp.concatenate([scr_ref[j] for j in range(GA // LANES)], axis=1)


def _mix_proj(h, vec, win, cos, sin, comm=None):
    T = h.shape[0]

    def body(h_ref, vec_ref, win_hbm, cos_ref, sin_ref, u_ref, p_ref, *rest):
        qkv_refs, gates_ref, win_v, scr_ref, sems = rest[:3 * NG], rest[3 * NG], rest[3 * NG + 1], rest[3 * NG + 2], rest[3 * NG + 3]
        _load_once([(win_hbm, win_v)], sems)
        g, sh, sc = vec_ref[0:1, :], vec_ref[1:2, :], vec_ref[2:3, :]
        _, _, _, u = _norm_fwd(h_ref[...], g, sh, sc)
        ub = u.astype(BF16)
        u_ref[...] = ub
        mixer_cols = PW + 3 * NG * GA
        proj = _dot(ub, win_v[:, 0:mixer_cols])
        p_ref[...] = proj[:, 0:PW]
        cos_t, sin_t = cos_ref[...], sin_ref[...]
        for j in range(3 * NG):
            col = PW + j * GA
            t = proj[:, col:col + GA]
            if j < 2 * NG:
                t = _rope(t, cos_t, sin_t)
            _to_residues(t, qkv_refs[j], scr_ref, DIL[j % NG])
        gates_ref[...] = jax.nn.sigmoid(_dot(ub, win_v[:, mixer_cols:INW])).astype(BF16)

    outs, c_outs = _call(
        body, "mix_proj", (T // TM,),
        [_rows(TM, D), _const((8, D)), ANY, _rows(TM, 128), _rows(TM, 128)],
        [_rows(TM, D), _rows(TM, PW)] + [_rm_spec(d) for d in DIL] * 3 + [_rows(TM, GW)],
        [_sds((T, D), BF16), _sds((T, PW), F32)] + [_sds((d, T // d, GA), BF16) for d in DIL] * 3 + [_sds((T, GW), BF16)],
        scratch=[pltpu.VMEM((D, INW), BF16), pltpu.VMEM((GA // LANES, TM, LANES), F32), pltpu.SemaphoreType.DMA((1,))],
        vmem=VMEM_BIG, comm=comm,
    )(h, vec, win, cos, sin)
    return (outs[0], outs[1], outs[2:2 + NG], outs[2 + NG:2 + 2 * NG], outs[2 + 2 * NG:2 + 3 * NG], outs[2 + 3 * NG]), c_outs


def _head_masks():
    lane_head = lax.broadcasted_iota(jnp.int32, (BLK, GA), 1) // HD
    return [lane_head == hd for hd in range(NH)]


def _expand_heads(t, hm):
    return jnp.concatenate([jnp.where(m, t, jnp.zeros_like(t)) for m in hm], axis=0)


def _collapse_heads(tb, hm):
    out = None
    for hd, m in enumerate(hm):
        part = jnp.where(m, tb[hd * BLK:(hd + 1) * BLK, :], 0.0)
        out = part if out is None else out + part
    return out


def _head_rows(t):
    return jnp.concatenate([t[:, hd * HD:hd * HD + 1] for hd in range(NH)], axis=0)


def _band(has_prev):
    a = lax.broadcasted_iota(jnp.int32, (NH * BLK, 2 * BLK), 0) & (BLK - 1)
    c = lax.broadcasted_iota(jnp.int32, (NH * BLK, 2 * BLK), 1)
    return jnp.logical_and(c >= jnp.where(has_prev, a, BLK), c <= a + BLK)


def _attn_fwd(q, k, v, nb, name, comm=None):
    T = q.shape[0]
    nbt = T // BLK

    def block(qv, kcat, vcat, has_prev, hm):
        s = jnp.where(_band(has_prev), _dot_nt(_expand_heads(qv, hm), kcat) * SCALE, NEG)
        mx = jnp.max(s, axis=-1, keepdims=True)
        e = jnp.exp(s - mx)
        l = jnp.sum(e, axis=-1, keepdims=True)
        ob = _dot((e * (1.0 / l)).astype(BF16), vcat)
        return _collapse_heads(ob, hm), _collapse_heads(jnp.broadcast_to(mx + jnp.log(l), (NH * BLK, GA)), hm)

    def body(q_ref, k_ref, kp_ref, v_ref, vp_ref, o_ref, lse_ref):
        b0 = FWD_BLOCKS * pl.program_id(0)
        hm = _head_masks()
        for b in range(FWD_BLOCKS):
            rows = slice(b * BLK, (b + 1) * BLK)
            if b == 0:
                kcat = jnp.concatenate([kp_ref[...], k_ref[rows, :]], axis=0)
                vcat = jnp.concatenate([vp_ref[...], v_ref[rows, :]], axis=0)
            else:
                kcat, vcat = k_ref[(b - 1) * BLK:(b + 1) * BLK, :], v_ref[(b - 1) * BLK:(b + 1) * BLK, :]
            o_ref[rows, :], lse_ref[rows, :] = block(q_ref[rows, :], kcat, vcat, ((b0 + b) & (nb - 1)) != 0, hm)

    cur = pl.BlockSpec((FWD_BLOCKS * BLK, GA), lambda i: (i, 0))
    prev = pl.BlockSpec((BLK, GA), lambda i: (jnp.maximum(FWD_BLOCKS * i - 1, 0), 0))
    return _call(body, name, (nbt // FWD_BLOCKS,), [cur, cur, prev, cur, prev], [cur, cur],
                 [_sds((T, GA), F32), _sds((T, GA), F32)], comm=comm)(q, k, k, v, v)


def _attn_bwd(q, k, v, do, lse, e, nb, name, comm=None):
    T = q.shape[0]
    nbt = T // BLK

    nblk = BWD_BLOCKS

    def probs_and_ds(qb, dob, kcat, vcat, lsev, ev, valid):
        p = jnp.where(valid, jnp.exp(_dot_nt(qb, kcat) * SCALE - _head_rows(lsev)), 0.0)
        return p.astype(BF16), (p * (_dot_nt(dob, vcat) + _head_rows(ev))).astype(BF16)

    def body(q_ref, k_ref, v_ref, do_ref, lse_ref, e_ref, kp_ref, vp_ref, qn_ref, don_ref, lsen_ref, en_ref,
             dq_ref, dk_ref, dv_ref):
        b0 = nblk * pl.program_id(0)
        hm = _head_masks()
        rows = [slice(b * BLK, (b + 1) * BLK) for b in range(nblk)]
        qs = [_expand_heads(q_ref[r, :], hm) for r in rows] + [_expand_heads(qn_ref[...], hm)]
        dos = [_expand_heads(do_ref[r, :], hm) for r in rows] + [_expand_heads(don_ref[...], hm)]
        ps, dss = [], []
        for b, r in enumerate(rows):
            if b == 0:
                kcat = jnp.concatenate([kp_ref[...], k_ref[r, :]], axis=0)
                vcat = jnp.concatenate([vp_ref[...], v_ref[r, :]], axis=0)
            else:
                kcat, vcat = k_ref[(b - 1) * BLK:(b + 1) * BLK, :], v_ref[(b - 1) * BLK:(b + 1) * BLK, :]
            p, ds = probs_and_ds(qs[b], dos[b], kcat, vcat, lse_ref[r, :], e_ref[r, :], _band(((b0 + b) & (nb - 1)) != 0))
            dq_ref[r, :] = _collapse_heads(_dot(ds, kcat) * SCALE, hm)
            ps.append(p)
            dss.append(ds)
        a = lax.broadcasted_iota(jnp.int32, (NH * BLK, BLK), 0) & (BLK - 1)
        c = lax.broadcasted_iota(jnp.int32, (NH * BLK, BLK), 1)
        valid_n = jnp.logical_and(c >= a, ((b0 + nblk) & (nb - 1)) != 0)
        p_n, ds_n = probs_and_ds(qs[nblk], dos[nblk], k_ref[rows[-1], :], v_ref[rows[-1], :], lsen_ref[...], en_ref[...], valid_n)
        for b, r in enumerate(rows):
            ds_after = dss[b + 1][:, :BLK] if b + 1 < nblk else ds_n
            p_after = ps[b + 1][:, :BLK] if b + 1 < nblk else p_n
            q_pair = jnp.concatenate([qs[b], qs[b + 1]], axis=0)
            do_pair = jnp.concatenate([dos[b], dos[b + 1]], axis=0)
            dk_ref[r, :] = _dot_tn(jnp.concatenate([dss[b][:, BLK:], ds_after], axis=0), q_pair) * SCALE
            dv_ref[r, :] = _dot_tn(jnp.concatenate([ps[b][:, BLK:], p_after], axis=0), do_pair).astype(BF16)

    cur = pl.BlockSpec((nblk * BLK, GA), lambda i: (i, 0))
    prev = pl.BlockSpec((BLK, GA), lambda i: (jnp.maximum(nblk * i - 1, 0), 0))
    nxt = pl.BlockSpec((BLK, GA), lambda i: (jnp.minimum(nblk * i + nblk, nbt - 1), 0))
    return _call(body, name, (nbt // nblk,), [cur] * 6 + [prev, prev] + [nxt] * 4, [cur, cur, cur],
                 [_sds((T, GA), F32), _sds((T, GA), F32), _sds((T, GA), BF16)],
                 comm=comm)(q, k, v, do, lse, e, k, v, q, do, lse, e)


def _flat(t):
    return t.reshape(t.shape[0] * t.shape[1], t.shape[2])


def _by_residue(t, dil):
    return t.reshape(dil, t.shape[0] // dil, t.shape[1])


def _pool_consts(shape, row0):
    lane = lax.broadcasted_iota(jnp.int32, shape, 1)
    t = lax.broadcasted_iota(jnp.int32, shape, 0) + row0
    grp = lane // (PW // len(POOL_WINDOWS))
    win = jnp.where(grp == 0, POOL_WINDOWS[0], jnp.where(grp == 1, POOL_WINDOWS[1],
                    jnp.where(grp == 2, POOL_WINDOWS[2], POOL_WINDOWS[3])))
    cnt = jnp.minimum(t + 1, win).astype(F32)
    return grp, cnt


def _window_sums(ext_ref, base, step, tm):
    outs, run = [], None
    for j in range(POOL_WINDOWS[-1]):
        sl = ext_ref[pl.ds(base + step * j, tm), :]
        run = sl if run is None else run + sl
        if j + 1 in POOL_WINDOWS:
            outs.append(run)
    return outs


def _select_group(grp, vals):
    return jnp.where(grp == 0, vals[0], jnp.where(grp == 1, vals[1], jnp.where(grp == 2, vals[2], vals[3])))


def _pool_d(pc_ref, pp_ref, ext_ref, i, tm):
    ext_ref[0:HALO, :] = jnp.where(i > 0, pp_ref[tm - HALO:tm, :], 0.0)
    ext_ref[HALO:HALO + tm, :] = pc_ref[...]
    grp, cnt = _pool_consts((tm, PW), i * tm)
    sums = _window_sums(ext_ref, HALO, -1, tm)
    return _select_group(grp, sums) / cnt - pc_ref[...]


def _group_weights(ls):
    mx = jnp.maximum(jnp.maximum(ls[0], ls[1]), ls[2])
    es = [jnp.exp(l - mx) for l in ls]
    inv = 1.0 / (es[0] + es[1] + es[2])
    return [e * inv for e in es]


def _mix_merge(h, vec, p, os, lses, gates, wp_bd, pscale, wpb, wab, wout, comm=None):
    T = h.shape[0]

    def body(h_ref, vec_ref, pc_ref, pp_ref, o0, o1, o2, l0, l1, l2, gates_ref, wp_ref, ps_ref, wpb_ref, wab_ref, wout_ref,
             ho_ref, yp_ref, ya_ref, mg_ref, mo_ref, d_ref, ext_ref, scr_ref):
        i = pl.program_id(0)
        gt = vec_ref[3:4, :]
        d = _pool_d(pc_ref, pp_ref, ext_ref, i, TM).astype(BF16)
        d_ref[...] = d
        ypool = (_dot(d, wp_ref[...]) * ps_ref[0:1, :]).astype(BF16)
        yp_ref[...] = ypool
        w = _group_weights([_from_residues(r, scr_ref, dl) for r, dl in zip((l0, l1, l2), DIL)])
        yattn = None
        for wg, o_ref, dl in zip(w, (o0, o1, o2), DIL):
            part = wg * _from_residues(o_ref, scr_ref, dl)
            yattn = part if yattn is None else yattn + part
        yattn = yattn.astype(BF16)
        ya_ref[...] = yattn
        merged = (gates_ref[:, 0:D].astype(F32) * _dot(ypool, wpb_ref[...])
                  + gates_ref[:, D:GW].astype(F32) * _dot(yattn, wab_ref[...])).astype(BF16)
        mg_ref[...] = merged
        mo = _dot(merged, wout_ref[...])
        mo_ref[...] = mo.astype(BF16)
        ho_ref[...] = h_ref[...] + gt * mo

    prev = pl.BlockSpec((TM, PW), lambda i: (jnp.maximum(i - 1, 0), 0))
    return _call(
        body, "mix_merge", (T // TM,),
        [_rows(TM, D), _const((8, D)), _rows(TM, PW), prev] + [_rm_spec(dl) for dl in DIL] * 2 + [_rows(TM, GW), _const((PW, PW)),
         _const((8, PW)), _const((PW, D)), _const((GA, D)), _const((D, D))],
        [_rows(TM, D), _rows(TM, PW), _rows(TM, GA), _rows(TM, D), _rows(TM, D), _rows(TM, PW)],
        [_sds((T, D), F32), _sds((T, PW), BF16), _sds((T, GA), BF16), _sds((T, D), BF16), _sds((T, D), BF16), _sds((T, PW), BF16)],
        scratch=[pltpu.VMEM((TM + HALO, PW), F32), pltpu.VMEM((GA // LANES, TM, LANES), F32)],
        vmem=VMEM_BIG, comm=comm,
    )(h, vec, p, p, *os, *lses, gates, wp_bd, pscale, wpb, wab, wout)


def _mix_bwd_a(dh, vec, mixout, merged, gates, ypool, yattn, dpool, os, lses, wp_bd, pscale, wpb, wab, wout, ones_bd,
               comm=None):
    T = dh.shape[0]
    nt = T // TM

    def body(dh_ref, vec_ref, mo_ref, mg_ref, gates_ref, yp_ref, ya_ref, d_ref, o0, o1, o2, l0, l1, l2,
             wp_ref, ps_ref, wpb_ref, wab_ref, wout_ref, ones_ref,
             dgates_ref, do0, do1, do2, e0, e1, e2, dd_ref, acc_ref, acc2_ref, g_out_ref, g_pb_ref, g_ab_ref, g_pool_ref,
             scr_ref, a_out, a_pb, a_ab, a_pool):
        _zero_first(acc_ref)
        _zero_first(acc2_ref)
        for a_ref in (a_out, a_pb, a_ab, a_pool):
            _zero_first(a_ref)
        gt = vec_ref[3:4, :]
        dho = dh_ref[...]
        acc_ref[3:4, :] += _colsum(dho * mo_ref[...].astype(F32))
        dmo = (gt * dho).astype(BF16)
        a_out[...] += _dot_tn(mg_ref[...], dmo)
        dmerged = _dot_nt(dmo, wout_ref[...])
        gp = gates_ref[:, 0:D].astype(F32)
        ga = gates_ref[:, D:GW].astype(F32)
        bp = _dot(yp_ref[...], wpb_ref[...])
        ba = _dot(ya_ref[...], wab_ref[...])
        dgates_ref[:, 0:D] = (dmerged * bp * gp * (1.0 - gp)).astype(BF16)
        dgates_ref[:, D:GW] = (dmerged * ba * ga * (1.0 - ga)).astype(BF16)
        dbp = (dmerged * gp).astype(BF16)
        dba = (dmerged * ga).astype(BF16)
        a_pb[...] += _dot_tn(yp_ref[...], dbp)
        a_ab[...] += _dot_tn(ya_ref[...], dba)
        dypool = _dot_nt(dbp, wpb_ref[...])
        ypre = _dot(d_ref[...], wp_ref[...])
        acc2_ref[0:1, :] += _colsum(dypool * ypre)
        dyp = (dypool * ps_ref[0:1, :]).astype(BF16)
        a_pool[...] += _dot_tn(d_ref[...], dyp)
        dd_ref[...] = _dot_nt(dyp, wp_ref[...])
        dya = _dot_nt(dba, wab_ref[...])
        w = _group_weights([_from_residues(r, scr_ref, dl) for r, dl in zip((l0, l1, l2), DIL)])
        ya = None
        for wg, o_ref, dl in zip(w, (o0, o1, o2), DIL):
            part = wg * _from_residues(o_ref, scr_ref, dl)
            ya = part if ya is None else ya + part
        prod = dya * ya
        hi = prod.astype(BF16)
        lo = (prod - hi.astype(F32)).astype(BF16)
        tot = _dot(hi, ones_ref[...]) + _dot(lo, ones_ref[...])
        for wg, do_ref, e_ref, dl in zip(w, (do0, do1, do2), (e0, e1, e2), DIL):
            _to_residues(wg * dya, do_ref, scr_ref, dl)
            _to_residues(-wg * tot, e_ref, scr_ref, dl)

        @pl.when(pl.program_id(0) == nt - 1)
        def _():
            g_out_ref[...] = a_out[...].astype(BF16)
            g_pb_ref[...] = a_pb[...].astype(BF16)
            g_ab_ref[...] = a_ab[...].astype(BF16)
            g_pool_ref[...] = a_pool[...]

    return _call(
        body, "mix_bwd_a", (nt,),
        [_rows(TM, D), _const((8, D)), _rows(TM, D), _rows(TM, D), _rows(TM, GW), _rows(TM, PW), _rows(TM, GA), _rows(TM, PW)]
        + [_rm_spec(dl) for dl in DIL] * 2
        + [_const((PW, PW)), _const((8, PW)), _const((PW, D)), _const((GA, D)), _const((D, D)), _const((GA, GA))],
        [_rows(TM, GW)] + [_rm_spec(dl) for dl in DIL] * 2 + [_rows(TM, PW), _const((8, D)), _const((8, PW))]
        + [_const((D, D)), _const((PW, D)), _const((GA, D)), _const((PW, PW))],
        [_sds((T, GW), BF16)] + [_sds((dl, T // dl, GA), BF16) for dl in DIL]
        + [_sds((dl, T // dl, GA), F32) for dl in DIL] + [_sds((T, PW), F32), _sds((8, D), F32), _sds((8, PW), F32)]
        + [_sds((D, D), BF16), _sds((PW, D), BF16), _sds((GA, D), BF16), _sds((PW, PW), F32)],
        scratch=[pltpu.VMEM((GA // LANES, TM, LANES), F32), pltpu.VMEM((D, D), F32), pltpu.VMEM((PW, D), F32),
                 pltpu.VMEM((GA, D), F32), pltpu.VMEM((PW, PW), F32)],
        vmem=VMEM_BIG, comm=comm,
    )(dh, vec, mixout, merged, gates, ypool, yattn, dpool, *os, *lses, wp_bd, pscale, wpb, wab, wout, ones_bd)


def _mix_bwd_b(dh, h, vec, dd, dqs, dks, dvs, dgates, cos, sin, win):
    T = h.shape[0]
    nt = T // TM

    def body(dh_ref, h_ref, vec_ref, ddc_ref, ddn_ref, *rest):
        qk_refs, dv_refs = rest[:2 * NG], rest[2 * NG:3 * NG]
        dgates_ref, cos_ref, sin_ref, win_hbm, dhi_ref, dproj_ref, acc_ref, win_v, ext_ref, scr_ref, sems = rest[3 * NG:]
        i = pl.program_id(0)
        _load_once([(win_hbm, win_v)], sems)
        _zero_first(acc_ref)
        g, sh, sc = vec_ref[0:1, :], vec_ref[1:2, :], vec_ref[2:3, :]
        grp, cnt = _pool_consts((TM, PW), i * TM)
        _, cnt_n = _pool_consts((HALO, PW), (i + 1) * TM)
        ext_ref[0:TM, :] = ddc_ref[...] / cnt
        ext_ref[TM:TM + HALO, :] = jnp.where(i < nt - 1, ddn_ref[0:HALO, :] / cnt_n, 0.0)
        dp = _select_group(grp, _window_sums(ext_ref, 0, 1, TM)) - ddc_ref[...]
        dproj_ref[:, 0:PW] = dp.astype(BF16)
        cos_t, sin_t = cos_ref[...], sin_ref[...]
        for j in range(2 * NG):
            col = PW + j * GA
            dt = _from_residues(qk_refs[j], scr_ref, DIL[j % NG])
            dproj_ref[:, col:col + GA] = _rope_bwd(dt, cos_t, sin_t).astype(BF16)
        for j in range(NG):
            col = PW + (2 * NG + j) * GA
            dproj_ref[:, col:col + GA] = _from_residues(dv_refs[j], scr_ref, DIL[j]).astype(BF16)
        dproj_ref[:, PW + 3 * NG * GA:INW] = dgates_ref[...]
        du = None
        for j in range(INW // 512):
            part = _dot_nt(dproj_ref[:, j * 512:(j + 1) * 512], win_v[:, j * 512:(j + 1) * 512])
            du = part if du is None else du + part
        xh, r, n, _ = _norm_fwd(h_ref[...], g, sh, sc)
        dhn, dsh, dsc, dg = _norm_bwd(du, xh, r, n, g, sc)
        dhi_ref[...] = dh_ref[...] + dhn
        acc_ref[0:1, :] += dsh
        acc_ref[1:2, :] += dsc
        acc_ref[2:3, :] += dg

    nxt = pl.BlockSpec((TM, PW), lambda i: (jnp.minimum(i + 1, nt - 1), 0))
    return _call(
        body, "mix_bwd_b", (nt,),
        [_rows(TM, D), _rows(TM, D), _const((8, D)), _rows(TM, PW), nxt] + [_rm_spec(dl) for dl in DIL] * 3
        + [_rows(TM, GW), _rows(TM, 128), _rows(TM, 128), ANY],
        [_rows(TM, D), _rows(TM, INW), _const((8, D))],
        [_sds((T, D), F32), _sds((T, INW), BF16), _sds((8, D), F32)],
        scratch=[pltpu.VMEM((D, INW), BF16), pltpu.VMEM((TM + HALO, PW), F32), pltpu.VMEM((GA // LANES, TM, LANES), F32),
                 pltpu.SemaphoreType.DMA((1,))],
        vmem=VMEM_BIG,
    )(dh, h, vec, dd, dd, *dqs, *dks, *dvs, dgates, cos, sin, win)[0]


def _ada_fwd(c_all, w_shard, b_shard):
    n = w_shard.shape[1]

    def body(c_ref, w_ref, b_ref, o_ref):
        cv = c_ref[...]
        cond = (cv * jax.nn.sigmoid(cv)).astype(BF16)
        o_ref[...] = _dot(cond, w_ref[...].astype(BF16)) + b_ref[...]

    tn = n // 3
    return pl.pallas_call(
        body, name="ada_fwd", grid=(3,),
        in_specs=[pl.BlockSpec((8, D), lambda j: (0, 0)), pl.BlockSpec((D, tn), lambda j: (0, j)), pl.BlockSpec((1, tn), lambda j: (0, j))],
        out_specs=pl.BlockSpec((8, tn), lambda j: (0, j)), out_shape=_sds((8, n), F32),
        compiler_params=pltpu.CompilerParams(dimension_semantics=("arbitrary",)),
    )(c_all, w_shard, b_shard)


def _ada_bwd(c_all, dmod_shard):
    n = dmod_shard.shape[1]

    def body(c_ref, d_ref, o_ref):
        cv = c_ref[...]
        cond = (cv * jax.nn.sigmoid(cv)).astype(BF16)
        o_ref[...] = _dot_tn(cond, d_ref[...].astype(BF16))

    tn = n // 3
    return pl.pallas_call(
        body, name="ada_bwd", grid=(3,),
        in_specs=[pl.BlockSpec((8, D), lambda j: (0, 0)), pl.BlockSpec((8, tn), lambda j: (0, j))],
        out_specs=pl.BlockSpec((D, tn), lambda j: (0, j)), out_shape=_sds((D, n), F32),
        compiler_params=pltpu.CompilerParams(dimension_semantics=("arbitrary",)),
    )(c_all, dmod_shard)


def _adam_math(w, g, m, v):
    m2 = B1 * m + (1.0 - B1) * g
    v2 = B2 * v + (1.0 - B2) * (g * g)
    m_hat = m2 / (1.0 - B1 ** STEP)
    v_hat = v2 / (1.0 - B2 ** STEP)
    delta = -LR * (m_hat / (jnp.sqrt(v_hat) + AEPS) + WD * w)
    return delta, m2, v2


def _adam(w, m, v, parts, name, comm=None):
    R, C = w.shape
    tr = R
    for cand in (128, 64, 32, 16, 8):
        if R % cand == 0:
            tr = cand
            break
    np_ = len(parts)

    def body(w_ref, m_ref, v_ref, *rest):
        p_refs, (g_ref, d_ref, m2_ref, v2_ref) = rest[:np_], rest[np_:]
        g = p_refs[0][...]
        for pr in p_refs[1:]:
            g = g + pr[...]
        delta, m2, v2 = _adam_math(w_ref[...], g, m_ref[...], v_ref[...])
        g_ref[...] = g
        d_ref[...] = delta
        m2_ref[...] = m2
        v2_ref[...] = v2

    spec = pl.BlockSpec((tr, C), lambda i: (i, 0))
    return _call(body, name, (R // tr,), [spec] * (3 + np_), [spec] * 4, [_sds((R, C), F32)] * 4,
                 vmem=VMEM_BIG, comm=comm)(w, m, v, *parts)


def _adam_halves(w, m, v, mine, other, name):
    R, C = w.shape
    tr = 128
    nh = R // 2 // tr

    def body(c_ref, w_ref, m_ref, v_ref, mine_ref, other_ref, g_ref, d_ref, m2_ref, v2_ref):
        i = pl.program_id(0)
        in_mine = jnp.logical_and(i >= c_ref[0] * nh, i < (c_ref[0] + 1) * nh)
        g = jnp.where(in_mine, mine_ref[...], other_ref[...])
        delta, m2, v2 = _adam_math(w_ref[...], g, m_ref[...], v_ref[...])
        g_ref[...] = g
        d_ref[...] = delta
        m2_ref[...] = m2
        v2_ref[...] = v2

    spec = pl.BlockSpec((tr, C), lambda i, c: (i, 0))
    grid_spec = pltpu.PrefetchScalarGridSpec(
        num_scalar_prefetch=1, grid=(R // tr,),
        in_specs=[spec] * 3 + [pl.BlockSpec((tr, C), lambda i, c: (jnp.clip(i - c[0] * nh, 0, nh - 1), 0)),
                               pl.BlockSpec((tr, C), lambda i, c: (jnp.clip(i - (1 - c[0]) * nh, 0, nh - 1), 0))],
        out_specs=[spec] * 4)
    return pl.pallas_call(
        body, name=name, grid_spec=grid_spec, out_shape=[_sds((R, C), F32)] * 4,
        compiler_params=pltpu.CompilerParams(dimension_semantics=("arbitrary",), vmem_limit_bytes=VMEM_BIG),
    )(lax.axis_index("c").astype(jnp.int32).reshape(1), w, m, v, mine, other)


def _adam_small(ws, ms, vs, gathered):
    n = len(ws)
    sizes = [a.shape[1] for a in ws]

    def total(ga_ref, off, size):
        g = ga_ref[0, :, off:off + size]
        for dev in range(1, 8):
            g = g + ga_ref[dev, :, off:off + size]
        return g

    def body(*refs):
        w_refs, m_refs, v_refs, ga_ref, outs = refs[:n], refs[n:2 * n], refs[2 * n:3 * n], refs[3 * n], refs[3 * n + 1:]
        off = 0
        for j, size in enumerate(sizes):
            g = total(ga_ref, off, size)
            delta, m2, v2 = _adam_math(w_refs[j][...], g, m_refs[j][...], v_refs[j][...])
            for ref, val in zip(outs[4 * j:4 * j + 4], (g, delta, m2, v2)):
                ref[...] = val
            off += size
        outs[4 * n][...] = total(ga_ref, off, 128)

    res = pl.pallas_call(
        body, name="adam_small",
        out_shape=[_sds((1, size), F32) for size in sizes for _ in range(4)] + [_sds((1, 128), F32)],
    )(*ws, *ms, *vs, gathered)
    return [res[4 * j:4 * j + 4] for j in range(n)], res[4 * n]


def _sum4(blocks, name):
    _, R, C = blocks.shape
    tr = R
    for cand in (256, 128, 64, 32, 16):
        if R % cand == 0:
            tr = cand
            break

    def body(r_ref, out_ref):
        out_ref[...] = ((r_ref[0].astype(F32) + r_ref[1].astype(F32)) + r_ref[2].astype(F32)) + r_ref[3].astype(F32)

    return pl.pallas_call(
        body, name=name, grid=(R // tr,),
        in_specs=[pl.BlockSpec((4, tr, C), lambda i: (0, i, 0))],
        out_specs=pl.BlockSpec((tr, C), lambda i: (i, 0)), out_shape=_sds((R, C), F32),
        compiler_params=pltpu.CompilerParams(dimension_semantics=("arbitrary",)),
    )(blocks)


def _place():
    return lax.axis_index("x"), lax.axis_index("y"), lax.axis_index("c")


def _chip_peer(x, y, c, m):
    return (x ^ (m >> 1), y ^ (m & 1), c)


def _shard_ref(ref, axis, k, n):
    start = pl.multiple_of(k * n, 128 if axis == 1 else 16)
    return ref.at[:, pl.ds(start, n)] if axis == 1 else ref.at[pl.ds(start, n), :]


def _half_rows(ref, axis, k, n, hc):
    if axis == 1:
        half = ref.shape[0] // 2
        return ref.at[pl.ds(pl.multiple_of(hc * half, 16), half), pl.ds(pl.multiple_of(k * n, 128), n)]
    half = n // 2
    return ref.at[pl.ds(pl.multiple_of(k * n + hc * half, 16), half), :]


class _GatherPlan:
    def __init__(self, shards, axes):
        self.inputs, self.axes, nw = list(shards), list(axes), len(shards)
        self.out_shapes = [_sds((s.shape[0] * (4 if ax == 0 else 1), s.shape[1] * (4 if ax == 1 else 1)), BF16)
                           for s, ax in zip(shards, axes)]
        self.sem_shapes = [pltpu.SemaphoreType.DMA((nw,))] + [pltpu.SemaphoreType.DMA((nw, 3))] * 4

    def _copies(self, ins, outs, sems):
        local_sems, send_sems, recv_sems, pass_sems, got_sems = sems
        x, y, c = _place()
        k = 2 * x + y
        local, sends, arrivals, passes, handed = [], [], [], [], []
        for j, ax in enumerate(self.axes):
            n = ins[j].shape[ax]
            half = ins[j].shape[0] // 2
            local.append(pltpu.make_async_copy(ins[j], _shard_ref(outs[j], ax, k, n), local_sems.at[j]))
            my_half = ins[j].at[pl.ds(pl.multiple_of(c * half, 16), half), :]
            for m in range(1, 4):
                sends.append(pltpu.make_async_remote_copy(
                    src_ref=my_half, dst_ref=_half_rows(outs[j], ax, k, n, c), send_sem=send_sems.at[j, m - 1],
                    recv_sem=recv_sems.at[j, m - 1], device_id=_chip_peer(x, y, c, m), device_id_type=MESH))
                theirs = _half_rows(outs[j], ax, k ^ m, n, c)
                arrivals.append(pltpu.make_async_remote_copy(
                    src_ref=my_half, dst_ref=theirs, send_sem=send_sems.at[j, m - 1], recv_sem=recv_sems.at[j, m - 1],
                    device_id=(x, y, c), device_id_type=MESH))
                passes.append(pltpu.make_async_remote_copy(
                    src_ref=theirs, dst_ref=theirs, send_sem=pass_sems.at[j, m - 1], recv_sem=got_sems.at[j, m - 1],
                    device_id=(x, y, 1 - c), device_id_type=MESH))
                other = _half_rows(outs[j], ax, k ^ m, n, 1 - c)
                handed.append(pltpu.make_async_remote_copy(
                    src_ref=other, dst_ref=other, send_sem=pass_sems.at[j, m - 1], recv_sem=got_sems.at[j, m - 1],
                    device_id=(x, y, c), device_id_type=MESH))
        return local, sends, arrivals, passes, handed

    def start(self, ins, outs, sems):
        local, sends, _, _, _ = self._copies(ins, outs, sems)
        for cp in local + sends:
            cp.start()

    def relay(self, ins, outs, sems):
        _, _, arrivals, passes, _ = self._copies(ins, outs, sems)
        for arrived, onward in zip(arrivals, passes):
            arrived.wait_recv()
            onward.start()

    def wait(self, ins, outs, sems):
        local, sends, _, passes, handed = self._copies(ins, outs, sems)
        for cp in handed:
            cp.wait_recv()
        for cp in sends + passes:
            cp.wait_send()
        for cp in local:
            cp.wait()


class _ScatterPlan:
    def __init__(self, grads, axes):
        self.inputs, self.axes, nw = list(grads), list(axes), len(grads)
        self.shard_shapes = [(g.shape[0] // (4 if ax == 0 else 1), g.shape[1] // (4 if ax == 1 else 1))
                             for g, ax in zip(grads, axes)]
        self.out_shapes = [_sds((4,) + s, BF16) for s in self.shard_shapes]
        self.sem_shapes = [pltpu.SemaphoreType.DMA((nw,)), pltpu.SemaphoreType.DMA((nw, 3)), pltpu.SemaphoreType.DMA((nw, 3))]

    def _copies(self, ins, outs, sems):
        local_sems, send_sems, recv_sems = sems
        x, y, c = _place()
        k = 2 * x + y
        local, remote, arrivals = [], [], []
        for j, ax in enumerate(self.axes):
            n = self.shard_shapes[j][ax]
            local.append(pltpu.make_async_copy(_shard_ref(ins[j], ax, k, n), outs[j].at[0], local_sems.at[j]))
            for m in range(1, 4):
                remote.append(pltpu.make_async_remote_copy(
                    src_ref=_shard_ref(ins[j], ax, k ^ m, n), dst_ref=outs[j].at[m],
                    send_sem=send_sems.at[j, m - 1], recv_sem=recv_sems.at[j, m - 1],
                    device_id=_chip_peer(x, y, c, m), device_id_type=MESH))
                arrivals.append(pltpu.make_async_remote_copy(
                    src_ref=_shard_ref(ins[j], ax, k, n), dst_ref=outs[j].at[m],
                    send_sem=send_sems.at[j, m - 1], recv_sem=recv_sems.at[j, m - 1],
                    device_id=(x, y, c), device_id_type=MESH))
        return local, remote, arrivals

    def start(self, ins, outs, sems):
        local, remote, _ = self._copies(ins, outs, sems)
        for cp in local + remote:
            cp.start()

    def relay(self, ins, outs, sems):
        pass

    def wait(self, ins, outs, sems):
        local, remote, arrivals = self._copies(ins, outs, sems)
        for cp in arrivals:
            cp.wait_recv()
        for cp in remote:
            cp.wait_send()
        for cp in local:
            cp.wait()


def _run_plan(plan, name):
    nc = len(plan.inputs)

    def body(*refs):
        ins, outs, sems = refs[:nc], refs[nc:2 * nc], refs[2 * nc:]
        plan.start(ins, outs, sems)
        plan.relay(ins, outs, sems)
        plan.wait(ins, outs, sems)

    return pl.pallas_call(body, name=name, in_specs=[ANY] * nc, out_specs=[ANY] * nc, out_shape=list(plan.out_shapes),
                          scratch_shapes=list(plan.sem_shapes))(*plan.inputs)


class _SwapPlan:
    def __init__(self, parts):
        self.inputs, nw = list(parts), len(parts)
        self.out_shapes = [_sds(p.shape, p.dtype) for p in parts]
        self.sem_shapes = [pltpu.SemaphoreType.DMA((nw,)), pltpu.SemaphoreType.DMA((nw,))]

    def _copies(self, ins, outs, sems):
        send_sems, recv_sems = sems
        x, y, c = _place()
        return [pltpu.make_async_remote_copy(
            src_ref=ins[j], dst_ref=outs[j], send_sem=send_sems.at[j], recv_sem=recv_sems.at[j],
            device_id=(x, y, 1 - c), device_id_type=MESH) for j in range(len(ins))]

    def start(self, ins, outs, sems):
        for cp in self._copies(ins, outs, sems):
            cp.start()

    def relay(self, ins, outs, sems):
        pass

    def wait(self, ins, outs, sems):
        for cp in self._copies(ins, outs, sems):
            cp.wait()


class _SmallGatherPlan:
    def __init__(self, v):
        self.inputs = [v]
        self.out_shapes = [_sds((8,) + v.shape, v.dtype)]
        self.sem_shapes = [pltpu.SemaphoreType.DMA((1,)), pltpu.SemaphoreType.DMA((7,)), pltpu.SemaphoreType.DMA((7,))]

    def _copies(self, ins, outs, sems):
        (v_ref,), (out_ref,), (local_sem, send_sems, recv_sems) = ins, outs, sems
        x, y, c = _place()
        me = 4 * x + 2 * y + c
        local = pltpu.make_async_copy(v_ref, out_ref.at[me], local_sem.at[0])
        sends, arrivals = [], []
        for m in range(1, 8):
            px, py, pc = x ^ (m >> 2), y ^ ((m >> 1) & 1), c ^ (m & 1)
            sends.append(pltpu.make_async_remote_copy(
                src_ref=v_ref, dst_ref=out_ref.at[me], send_sem=send_sems.at[m - 1], recv_sem=recv_sems.at[m - 1],
                device_id=(px, py, pc), device_id_type=MESH))
            arrivals.append(pltpu.make_async_remote_copy(
                src_ref=v_ref, dst_ref=out_ref.at[4 * px + 2 * py + pc], send_sem=send_sems.at[m - 1],
                recv_sem=recv_sems.at[m - 1], device_id=(x, y, c), device_id_type=MESH))
        return local, sends, arrivals

    def start(self, ins, outs, sems):
        local, sends, _ = self._copies(ins, outs, sems)
        for cp in [local] + sends:
            cp.start()

    def relay(self, ins, outs, sems):
        pass

    def wait(self, ins, outs, sems):
        local, sends, arrivals = self._copies(ins, outs, sems)
        for cp in arrivals:
            cp.wait_recv()
        for cp in sends:
            cp.wait_send()
        local.wait()


class _PlanGroup:
    def __init__(self, plans):
        self.plans = [p for p in plans if p is not None]
        self.inputs = [a for p in self.plans for a in p.inputs]
        self.out_shapes = [s for p in self.plans for s in p.out_shapes]
        self.sem_shapes = [s for p in self.plans for s in p.sem_shapes]

    def _each(self, ins, outs, sems):
        i = s = 0
        for p in self.plans:
            n, ns = len(p.inputs), len(p.sem_shapes)
            yield p, ins[i:i + n], outs[i:i + n], sems[s:s + ns]
            i, s = i + n, s + ns

    def start(self, ins, outs, sems):
        for p, pi, po, ps in self._each(ins, outs, sems):
            p.start(pi, po, ps)

    def relay(self, ins, outs, sems):
        for p, pi, po, ps in self._each(ins, outs, sems):
            p.relay(pi, po, ps)

    def wait(self, ins, outs, sems):
        for p, pi, po, ps in self._each(ins, outs, sems):
            p.wait(pi, po, ps)

    def split(self, outs):
        res, i = [], 0
        for p in self.plans:
            res.append(outs[i:i + len(p.inputs)])
            i += len(p.inputs)
        return res


BIG = ("w_ffn1_in", "w_ffn1_out", "w_in", "w_pool_branch", "w_attn_branch", "w_out", "w_ffn2_in", "w_ffn2_out")
BIG_AXIS = {"w_ffn1_in": 1, "w_ffn1_out": 0, "w_in": 1, "w_pool_branch": 1, "w_attn_branch": 1, "w_out": 0,
            "w_ffn2_in": 1, "w_ffn2_out": 0}


class _Sharded:
    fused_scatter = True

    def __init__(self, shards):
        self.shards, self.full, self.recv = shards, {}, {}

    def gather_plan(self, names):
        return _GatherPlan([self.shards[n] for n in names], [BIG_AXIS[n.split("/")[0]] for n in names])

    def gather_now(self, names):
        self.gathered(names, _run_plan(self.gather_plan(names), "gather_" + names[0]))

    def gathered(self, names, outs):
        self.full.update(zip(names, outs))

    def scatter_plan(self, names, grads):
        return _ScatterPlan([grads[n] for n in names], [BIG_AXIS[n] for n in names])

    def scatter_now(self, names, grads):
        self.scattered(names, _run_plan(self.scatter_plan(names, grads), "scatter_" + names[0]))

    def scattered(self, names, outs):
        self.recv.update(zip(names, outs))


class _Whole:
    fused_scatter = False

    def __init__(self, full):
        self.full, self.recv = dict(full), {}

    def gather_plan(self, names):
        return None

    def gather_now(self, names):
        pass

    def gathered(self, names, outs):
        pass

    def scatter_plan(self, names, grads):
        return None

    def scatter_now(self, names, grads):
        pass

    def scattered(self, names, outs):
        pass


def _vec(rows):
    pad = [jnp.zeros((1, D), F32)] * (8 - len(rows))
    return jnp.concatenate([r.reshape(1, D) for r in rows] + pad, axis=0)


def _block_diag(w_pool):
    n, c = w_pool.shape[0], w_pool.shape[1]
    eye = jnp.eye(n, dtype=w_pool.dtype)
    return (eye[:, None, :, None] * w_pool[:, :, None, :]).reshape(n * c, n * c)


def _example_step(x, tgt, positions, mod, gains, w_pool, pool_scale, ws, pack=None):
    T = x.shape[0]
    assert (T // BLK // DIL[-1]) & (T // BLK // DIL[-1] - 1) == 0, "blocks per sequence must be a power of two"
    sh1, sc1, gt1, sh2, sc2, gt2, sh3, sc3, gt3 = [mod[j * D:(j + 1) * D] for j in range(NMOD)]
    g1, g2, g3, gf = gains
    vec1, vec2, vec3 = _vec([g1, sh1, sc1, gt1]), _vec([g2, sh2, sc2, gt2]), _vec([g3, sh3, sc3, gt3])
    inv_freq = 10000.0 ** (-jnp.arange(0, HD, 2, dtype=F32) / HD)
    ang = positions.astype(F32)[:, None] * inv_freq
    cos = jnp.tile(jnp.cos(ang), (1, 4))
    sin = jnp.tile(jnp.concatenate([-jnp.sin(ang), jnp.sin(ang)], axis=1), (1, 2))
    wp_bd = _block_diag(w_pool).astype(BF16)
    ones_bd = _block_diag(jnp.ones((NH, HD, HD), F32)).astype(BF16)
    ps = jnp.concatenate([pool_scale.reshape(1, PW), jnp.zeros((7, PW), F32)], axis=0)
    wb = ws.full

    if "w_ffn1_in" not in wb:
        ws.gather_now(["w_ffn1_in"])
    (u1, a1, b1), got = _ffn_ab(x, vec1, [wb["w_ffn1_in"]], "ffn1_ab", ws.gather_plan(["w_ffn1_out", "w_in"]))
    ws.gathered(["w_ffn1_out", "w_in"], got)
    mixw = ["w_pool_branch", "w_attn_branch", "w_out"]
    (h1, f1), got = _ffn_out(x, a1, b1, vec1, wb["w_ffn1_out"], "ffn1_out", ws.gather_plan(mixw))
    ws.gathered(mixw, got)
    (u2, p, qs, ks, vs, gates), got = _mix_proj(h1, vec2, wb["w_in"], cos, sin, ws.gather_plan(["w_ffn2_in/0"]))
    ws.gathered(["w_ffn2_in/0"], got)
    qs, ks, vs = [_flat(t) for t in qs], [_flat(t) for t in ks], [_flat(t) for t in vs]
    nbs = [T // d // BLK for d in DIL]
    os, lses = [], []
    for gi in range(NG):
        (o, lse), _ = _attn_fwd(qs[gi], ks[gi], vs[gi], nbs[gi], f"attn_fwd{gi}")
        os.append(o)
        lses.append(lse)
    os_r = [_by_residue(t, d) for t, d in zip(os, DIL)]
    lses_r = [_by_residue(t, d) for t, d in zip(lses, DIL)]
    ffn2w = ["w_ffn2_in/1", "w_ffn2_out"]
    (h2, ypool, yattn, merged, mixout, dpool), got = _mix_merge(
        h1, vec2, p, os_r, lses_r, gates, wp_bd, ps, wb["w_pool_branch"], wb["w_attn_branch"], wb["w_out"],
        ws.gather_plan(ffn2w))
    ws.gathered(ffn2w, got)
    win3 = [wb["w_ffn2_in/0"], wb["w_ffn2_in/1"]] if "w_ffn2_in/0" in wb else [wb["w_ffn2_in"]]
    (dh3, u3, a3, b3, f3, lacc), _ = _ffn_fwd(h2, vec3, win3, wb["w_ffn2_out"], "ffn2_fwd", head=(tgt, _vec([gf])))
    loss = 0.5 * jnp.sum(lacc[0]) / D

    grads = {}

    def wgrad_cols(name, xx, yy, riders, extra=None):
        group = _PlanGroup([ws.scatter_plan(riders, grads) if riders else None, extra])
        plan = group if group.plans else None
        if ws.fused_scatter:
            blocks, got = _wgrad_scatter(xx, yy, "wg_" + name, min(2048, T // 2), comm=plan)
            ws.scattered([name], [blocks])
        else:
            grads[name], got = _wgrad(xx, yy, "wg_" + name, D, 512, 1024, comm=plan)
        parts = group.split(got)
        if len(parts) > (extra is not None):
            ws.scattered(riders, parts[0])
        return parts[-1] if extra is not None else None

    (dh2, dab3, s3, df3, acc3), _ = _ffn_bwd(dh3, h2, a3, b3, f3, vec3, win3, wb["w_ffn2_out"], "ffn2_bwd")
    grads["w_ffn2_out"], _ = _wgrad(s3, df3, "wg_ffn2_out", FF // 2, 512, min(4096, T // 2))
    wgrad_cols("w_ffn2_in", u3, dab3, ["w_ffn2_out"])
    (dgates, do0, do1, do2, e0, e1, e2, dd, acc2a, accps,
     grads["w_out"], grads["w_pool_branch"], grads["w_attn_branch"], gwp), _ = _mix_bwd_a(
        dh2, vec2, mixout, merged, gates, ypool, yattn, dpool, os_r, lses_r, wp_bd, ps,
        wb["w_pool_branch"], wb["w_attn_branch"], wb["w_out"], ones_bd)
    n = len(POOL_WINDOWS)
    c = PW // n
    grad_w_pool = jnp.stack([gwp[j * c:(j + 1) * c, j * c:(j + 1) * c] for j in range(n)], axis=0)
    small3 = ["w_out", "w_pool_branch", "w_attn_branch"]
    dqs, dks, dvs = [], [], []
    for gi, (do, e) in enumerate(((do0, e0), (do1, e1), (do2, e2))):
        plan = ws.scatter_plan(small3, grads) if gi == 0 else None
        (dq, dk, dv), got = _attn_bwd(qs[gi], ks[gi], vs[gi], _flat(do), lses[gi], _flat(e), nbs[gi], f"attn_bwd{gi}", plan)
        if gi == 0:
            ws.scattered(small3, got)
        dqs.append(_by_residue(dq, DIL[gi]))
        dks.append(_by_residue(dk, DIL[gi]))
        dvs.append(_by_residue(dv, DIL[gi]))
    dh1, dproj, acc2b = _mix_bwd_b(dh2, h1, vec2, dd, dqs, dks, dvs, dgates, cos, sin, wb["w_in"])
    wgrad_cols("w_in", u2, dproj, [])
    (dx, dab1, s1, df1, acc1), _ = _ffn_bwd(dh1, x, a1, b1, f1, vec1, [wb["w_ffn1_in"]], wb["w_ffn1_out"], "ffn1_bwd")
    grads["w_ffn1_out"], _ = _wgrad(s1, df1, "wg_ffn1_out", FF // 2, 512, min(4096, T // 2))
    dmod = jnp.concatenate([acc1[0], acc1[1], acc1[3], acc2b[0], acc2b[1], acc2a[3], acc3[0], acc3[1], acc3[3]])
    dgains = jnp.stack([acc1[2], acc2b[2], acc3[2], lacc[1]], axis=0)
    row = None if pack is None else _SmallGatherPlan(pack(loss, dmod, dgains, grad_w_pool, accps[0]))
    early = [n for n in BIG if n in ws.recv] if pack is not None else []
    ws.sums = {n: _sum4(ws.recv[n], "sum_" + n) for n in early}
    ws.other = {}
    tail = _PlanGroup([row, _SwapPlan([ws.sums[n] for n in early]) if early else None])
    got = wgrad_cols("w_ffn1_in", u1, dab1, ["w_ffn1_out"], tail if tail.plans else None)
    gathered = None
    if got is not None:
        parts = tail.split(got)
        gathered = parts[0][0]
        ws.other = dict(zip(early, parts[1])) if early else {}
    return loss, dx, dmod, dgains, grad_w_pool, accps[0], grads, gathered


SMALL = ("b_ada", "g_norm_ffn1", "g_norm_mix", "g_norm_ffn2", "g_final", "pool_scale", "w_pool")
WEIGHTS = ("w_ada", "b_ada", "g_norm_ffn1", "w_ffn1_in", "w_ffn1_out", "g_norm_mix", "w_in", "w_pool", "pool_scale",
           "w_pool_branch", "w_attn_branch", "w_out", "g_norm_ffn2", "w_ffn2_in", "w_ffn2_out", "g_final")


def _pack_small(t):
    return jnp.concatenate([t[n].reshape(-1) for n in SMALL]).reshape(1, -1)


def kernel(x, c, positions, w_ada, b_ada, g_norm_ffn1, w_ffn1_in, w_ffn1_out, g_norm_mix, w_in, w_pool, pool_scale, w_pool_branch, w_attn_branch, w_out, g_norm_ffn2, w_ffn2_in, w_ffn2_out, g_final, loss_target, m_w_ada, m_b_ada, m_g_norm_ffn1, m_w_ffn1_in, m_w_ffn1_out, m_g_norm_mix, m_w_in, m_w_pool, m_pool_scale, m_w_pool_branch, m_w_attn_branch, m_w_out, m_g_norm_ffn2, m_w_ffn2_in, m_w_ffn2_out, m_g_final, v_w_ada, v_b_ada, v_g_norm_ffn1, v_w_ffn1_in, v_w_ffn1_out, v_g_norm_mix, v_w_in, v_w_pool, v_pool_scale, v_w_pool_branch, v_w_attn_branch, v_w_out, v_g_norm_ffn2, v_w_ffn2_in, v_w_ffn2_out, v_g_final):
    w = dict(w_ada=w_ada, b_ada=b_ada, g_norm_ffn1=g_norm_ffn1, w_ffn1_in=w_ffn1_in, w_ffn1_out=w_ffn1_out,
             g_norm_mix=g_norm_mix, w_in=w_in, w_pool=w_pool, pool_scale=pool_scale, w_pool_branch=w_pool_branch,
             w_attn_branch=w_attn_branch, w_out=w_out, g_norm_ffn2=g_norm_ffn2, w_ffn2_in=w_ffn2_in,
             w_ffn2_out=w_ffn2_out, g_final=g_final)
    mom = dict(w_ada=m_w_ada, b_ada=m_b_ada, g_norm_ffn1=m_g_norm_ffn1, w_ffn1_in=m_w_ffn1_in, w_ffn1_out=m_w_ffn1_out,
               g_norm_mix=m_g_norm_mix, w_in=m_w_in, w_pool=m_w_pool, pool_scale=m_pool_scale,
               w_pool_branch=m_w_pool_branch, w_attn_branch=m_w_attn_branch, w_out=m_w_out, g_norm_ffn2=m_g_norm_ffn2,
               w_ffn2_in=m_w_ffn2_in, w_ffn2_out=m_w_ffn2_out, g_final=m_g_final)
    var = dict(w_ada=v_w_ada, b_ada=v_b_ada, g_norm_ffn1=v_g_norm_ffn1, w_ffn1_in=v_w_ffn1_in, w_ffn1_out=v_w_ffn1_out,
               g_norm_mix=v_g_norm_mix, w_in=v_w_in, w_pool=v_w_pool, pool_scale=v_pool_scale,
               w_pool_branch=v_w_pool_branch, w_attn_branch=v_w_attn_branch, w_out=v_w_out, g_norm_ffn2=v_g_norm_ffn2,
               w_ffn2_in=v_w_ffn2_in, w_ffn2_out=v_w_ffn2_out, g_final=v_g_final)
    ix, iy, ic = _place()
    chip = 2 * ix + iy
    me = 4 * ix + 2 * iy + ic
    nada = w_ada.shape[2]

    shards = {n: w[n][0].astype(BF16) for n in BIG}
    half = D // 2
    shards["w_ffn2_in/0"], shards["w_ffn2_in/1"] = shards["w_ffn2_in"][:half], shards["w_ffn2_in"][half:]
    ws = _Sharded(shards)
    c_all = _run_plan(_SmallGatherPlan(c), "gather_c")[0][:, 0, :]
    b_shard = lax.dynamic_slice_in_dim(b_ada, chip * nada, nada, axis=1)
    mod_cols = _ada_fwd(c_all, w_ada[0], b_shard)
    first = _PlanGroup([_SmallGatherPlan(mod_cols), ws.gather_plan(["w_ffn1_in"])])
    (mod_all,), ffn1 = first.split(_run_plan(first, "gather_first"))
    ws.gathered(["w_ffn1_in"], ffn1)
    mod = jnp.concatenate([lax.dynamic_index_in_dim(mod_all[4 * (kk >> 1) + 2 * (kk & 1)], me, axis=0, keepdims=False)
                           for kk in range(4)])

    def pack(loss, dmod, dgains, g_w_pool, g_pool_scale):
        small_g = dict(b_ada=dmod, g_norm_ffn1=dgains[0], g_norm_mix=dgains[1], g_norm_ffn2=dgains[2],
                       g_final=dgains[3], pool_scale=g_pool_scale, w_pool=g_w_pool)
        return jnp.concatenate([_pack_small(small_g), jnp.pad(loss.reshape(1, 1), ((0, 0), (0, 127)))], axis=1)

    _, dx, _, _, _, _, _, gathered = _example_step(
        x[0], loss_target[0], positions[0], mod, (g_norm_ffn1[0], g_norm_mix[0], g_norm_ffn2[0], g_final),
        w_pool[0], pool_scale[0], ws, pack)

    per_weight, loss_tile = _adam_small(*[[t[n].reshape(1, -1) for n in SMALL] for t in (w, mom, var)], gathered)
    small_out = [{n: per_weight[j][kind].reshape(w[n].shape) for j, n in enumerate(SMALL)} for kind in range(4)]
    loss = loss_tile[0, 0]

    dmod_all = gathered[:, 0, :NMOD * D]
    dmod_cols = lax.dynamic_slice_in_dim(dmod_all, chip * nada, nada, axis=1)
    g_ada = _ada_bwd(c_all, dmod_cols)

    ada_out = _adam(w_ada[0], m_w_ada[0], v_w_ada[0], [g_ada], "adam_w_ada")[0]

    sums, other = dict(ws.sums), dict(ws.other)
    late = [n for n in BIG if n not in sums]
    sums.update({n: _sum4(ws.recv[n], "sum_" + n) for n in late})
    other.update(zip(late, _run_plan(_SwapPlan([sums[n] for n in late]), "swap_sibling")))
    big_out = {}
    for n in BIG:
        if sums[n].shape[0] < w[n].shape[1]:
            big_out[n] = _adam_halves(w[n][0], mom[n][0], var[n][0], sums[n], other[n], "adam_" + n)
        else:
            big_out[n] = _adam(w[n][0], mom[n][0], var[n][0], [sums[n], other[n]], "adam_" + n)[0]

    def leaf(kind, n):
        if n == "w_ada":
            return ada_out[kind][None]
        if n in big_out:
            return big_out[n][kind][None]
        return small_out[kind][n]

    return (loss, dx[None], *[leaf(kind, n) for kind in range(4) for n in WEIGHTS])
```

```python
import jax
import jax.numpy as jnp
from jax import lax
from jax.experimental import pallas as pl
from jax.experimental.pallas import tpu as pltpu

F32 = jnp.float32
BF16 = jnp.bfloat16

D = 1024
FF = 2816
FC = FF
PW = 256
GA = 256
HD = 64
LANES = 128
NH = GA // HD
NG = 3
DIL = (1, 4, 16)
BLK = 128
FWD_BLOCKS = 16
BWD_BLOCKS = 16
GW = 2 * D
INW = PW + 3 * NG * GA + GW
NMOD = 9
POOL_WINDOWS = (2, 4, 8, 16)
HALO = 16
EPS = 1e-6
SCALE = HD ** -0.5
NEG = -1e30

LR, B1, B2, AEPS, WD, STEP = 0.001, 0.9, 0.999, 1e-08, 0.01, 10

VMEM_BIG = 56 * 1024 * 1024
TM = 256

MESH = pl.DeviceIdType.MESH
ANY = pl.BlockSpec(memory_space=pl.ANY)


def _call(body, name, grid, in_specs, out_specs, out_shape, scratch=(), vmem=None, comm=None):
    params = pltpu.CompilerParams(dimension_semantics=("arbitrary",) * len(grid), vmem_limit_bytes=vmem)
    n_in, n_out, n_scr = len(in_specs), len(out_shape), len(scratch)
    if comm is None:
        call = pl.pallas_call(body, name=name, grid=grid, in_specs=list(in_specs), out_specs=list(out_specs),
                              out_shape=list(out_shape), scratch_shapes=list(scratch), compiler_params=params)
        return lambda *args: (call(*args), ())
    nc = len(comm.inputs)

    def body_with_comm(*refs):
        ins, refs = refs[:n_in], refs[n_in:]
        c_ins, refs = refs[:nc], refs[nc:]
        outs, refs = refs[:n_out], refs[n_out:]
        c_outs, refs = refs[:nc], refs[nc:]
        scr, sems = refs[:n_scr], refs[n_scr:]
        first = pl.program_id(0) == 0
        last = pl.program_id(0) == grid[0] - 1
        for ax in range(1, len(grid)):
            first = jnp.logical_and(first, pl.program_id(ax) == 0)
            last = jnp.logical_and(last, pl.program_id(ax) == grid[ax] - 1)

        @pl.when(first)
        def _():
            comm.start(c_ins, c_outs, sems)

        body(*ins, *outs, *scr)
        early_relay = len(grid) == 1 and grid[0] >= 4
        if early_relay:
            @pl.when(pl.program_id(0) == (3 * grid[0]) // 4)
            def _():
                comm.relay(c_ins, c_outs, sems)

        @pl.when(last)
        def _():
            if not early_relay:
                comm.relay(c_ins, c_outs, sems)
            comm.wait(c_ins, c_outs, sems)

    call = pl.pallas_call(
        body_with_comm, name=name, grid=grid, in_specs=list(in_specs) + [ANY] * nc,
        out_specs=list(out_specs) + [ANY] * nc, out_shape=list(out_shape) + list(comm.out_shapes),
        scratch_shapes=list(scratch) + list(comm.sem_shapes), compiler_params=params)

    def run(*args):
        res = call(*args, *comm.inputs)
        return res[:n_out], res[n_out:]

    return run


def _rows(tm, n):
    return pl.BlockSpec((tm, n), lambda i: (i, 0))


def _const(shape):
    return pl.BlockSpec(shape, lambda i: (0,) * len(shape))


def _sds(shape, dtype):
    return jax.ShapeDtypeStruct(shape, dtype)


def _dot(a, b):
    return jnp.dot(a, b, preferred_element_type=F32)


def _dot_nt(a, b):
    return lax.dot_general(a, b, (((1,), (1,)), ((), ())), preferred_element_type=F32)


def _dot_tn(a, b):
    return lax.dot_general(a, b, (((0,), (0,)), ((), ())), preferred_element_type=F32)


def _colsum(v):
    return jnp.sum(v, axis=0, keepdims=True)


def _norm_fwd(h, g, sh, sc):
    r = lax.rsqrt(jnp.mean(h * h, axis=-1, keepdims=True) + EPS)
    xh = h * r
    n = xh * g
    return xh, r, n, n * (1.0 + sc) + sh


def _norm_bwd(du, xh, r, n, g, sc):
    dn = du * (1.0 + sc)
    dxh = dn * g
    dh = r * (dxh - xh * jnp.mean(dxh * xh, axis=-1, keepdims=True))
    return dh, _colsum(du), _colsum(du * n), _colsum(dn * xh)


def _load_once(pairs, sems):
    @pl.when(pl.program_id(0) == 0)
    def _():
        cps = [pltpu.make_async_copy(src, dst, sems.at[j]) for j, (src, dst) in enumerate(pairs)]
        for cp in cps:
            cp.start()
        for cp in cps:
            cp.wait()


def _zero_first(ref):
    @pl.when(pl.program_id(0) == 0)
    def _():
        ref[...] = jnp.zeros(ref.shape, ref.dtype)


def _row_chunks(hbm_refs, vmem_ref):
    pairs, row = [], 0
    for ref in hbm_refs:
        pairs.append((ref, vmem_ref.at[pl.ds(row, ref.shape[0]), :]))
        row += ref.shape[0]
    return pairs


def _loss_head(hh, tgt, g):
    r = lax.rsqrt(jnp.mean(hh * hh, axis=-1, keepdims=True) + EPS)
    xh = hh * r
    err = xh * g - tgt
    dy = err * (1.0 / D)
    dxh = dy * g
    dh = r * (dxh - xh * jnp.mean(dxh * xh, axis=-1, keepdims=True))
    return dh, _colsum(err * err), _colsum(dy * xh)


def _ffn_fwd(h, vec, wins, wout, name, comm=None, head=None):
    T = h.shape[0]
    nwin = len(wins)
    nhead = 0 if head is None else 2

    def body(h_ref, vec_ref, *rest):
        head_refs, rest = rest[:nhead], rest[nhead:]
        win_hbms, rest = rest[:nwin], rest[nwin:]
        (wout_hbm, ho_ref, u_ref, a_ref, b_ref, f_ref), rest = rest[:6], rest[6:]
        lacc_refs, (win_v, wout_v, sems) = rest[:nhead // 2], rest[nhead // 2:]
        _load_once(_row_chunks(win_hbms, win_v) + [(wout_hbm, wout_v)], sems)
        hh = h_ref[...]
        g, sh, sc, gt = vec_ref[0:1, :], vec_ref[1:2, :], vec_ref[2:3, :], vec_ref[3:4, :]
        _, _, _, u = _norm_fwd(hh, g, sh, sc)
        ub = u.astype(BF16)
        u_ref[...] = ub
        acc = None
        for j in range(FF // FC):
            lo, hi = j * FC, (j + 1) * FC
            a = _dot(ub, win_v[:, lo:hi])
            b = _dot(ub, win_v[:, FF + lo:FF + hi])
            a_ref[:, lo:hi] = a.astype(BF16)
            b_ref[:, lo:hi] = b.astype(BF16)
            s = (a * jax.nn.sigmoid(a) * b).astype(BF16)
            part = _dot(s, wout_v[lo:hi, :])
            acc = part if acc is None else acc + part
        f_ref[...] = acc.astype(BF16)
        ho = hh + 0.5 * gt * acc
        if head is None:
            ho_ref[...] = ho
        else:
            _zero_first(lacc_refs[0])
            dh, sq, dg = _loss_head(ho, head_refs[0][...], head_refs[1][0:1, :])
            ho_ref[...] = dh
            lacc_refs[0][0:1, :] += sq
            lacc_refs[0][1:2, :] += dg

    head_specs = [] if head is None else [_rows(TM, D), _const((8, D))]
    lacc_spec = [] if head is None else [_const((8, D))]
    lacc_shape = [] if head is None else [_sds((8, D), F32)]
    return _call(
        body, name, (T // TM,),
        [_rows(TM, D), _const((8, D))] + head_specs + [ANY] * (nwin + 1),
        [_rows(TM, D), _rows(TM, D), _rows(TM, FF), _rows(TM, FF), _rows(TM, D)] + lacc_spec,
        [_sds((T, D), F32), _sds((T, D), BF16), _sds((T, FF), BF16), _sds((T, FF), BF16), _sds((T, D), BF16)] + lacc_shape,
        scratch=[pltpu.VMEM((D, 2 * FF), BF16), pltpu.VMEM((FF, D), BF16), pltpu.SemaphoreType.DMA((nwin + 1,))],
        vmem=VMEM_BIG, comm=comm,
    )(h, vec, *([] if head is None else head), *wins, wout)


def _ffn_ab(h, vec, wins, name, comm=None):
    T = h.shape[0]
    nwin = len(wins)

    def body(h_ref, vec_ref, *rest):
        win_hbms, (u_ref, a_ref, b_ref, win_v, sems) = rest[:nwin], rest[nwin:]
        _load_once(_row_chunks(win_hbms, win_v), sems)
        g, sh, sc = vec_ref[0:1, :], vec_ref[1:2, :], vec_ref[2:3, :]
        _, _, _, u = _norm_fwd(h_ref[...], g, sh, sc)
        ub = u.astype(BF16)
        u_ref[...] = ub
        for j in range(FF // FC):
            lo, hi = j * FC, (j + 1) * FC
            a_ref[:, lo:hi] = _dot(ub, win_v[:, lo:hi]).astype(BF16)
            b_ref[:, lo:hi] = _dot(ub, win_v[:, FF + lo:FF + hi]).astype(BF16)

    return _call(
        body, name, (T // TM,),
        [_rows(TM, D), _const((8, D))] + [ANY] * nwin,
        [_rows(TM, D), _rows(TM, FF), _rows(TM, FF)],
        [_sds((T, D), BF16), _sds((T, FF), BF16), _sds((T, FF), BF16)],
        scratch=[pltpu.VMEM((D, 2 * FF), BF16), pltpu.SemaphoreType.DMA((nwin,))],
        vmem=VMEM_BIG, comm=comm,
    )(h, vec, *wins)


def _ffn_out(h, a, b, vec, wout, name, comm=None):
    T = h.shape[0]

    def body(h_ref, a_ref, b_ref, vec_ref, wout_hbm, ho_ref, f_ref, wout_v, sems):
        _load_once([(wout_hbm, wout_v)], sems)
        gt = vec_ref[3:4, :]
        acc = None
        for j in range(FF // FC):
            lo, hi = j * FC, (j + 1) * FC
            av = a_ref[:, lo:hi].astype(F32)
            s = (av * jax.nn.sigmoid(av) * b_ref[:, lo:hi].astype(F32)).astype(BF16)
            part = _dot(s, wout_v[lo:hi, :])
            acc = part if acc is None else acc + part
        f_ref[...] = acc.astype(BF16)
        ho_ref[...] = h_ref[...] + 0.5 * gt * acc

    return _call(
        body, name, (T // TM,),
        [_rows(TM, D), _rows(TM, FF), _rows(TM, FF), _const((8, D)), ANY],
        [_rows(TM, D), _rows(TM, D)],
        [_sds((T, D), F32), _sds((T, D), BF16)],
        scratch=[pltpu.VMEM((FF, D), BF16), pltpu.SemaphoreType.DMA((1,))],
        vmem=VMEM_BIG, comm=comm,
    )(h, a, b, vec, wout)


def _ffn_bwd(dh, h, a, b, f, vec, wins, wout, name, comm=None):
    T = h.shape[0]
    nwin = len(wins)

    def body(dh_ref, h_ref, a_ref, b_ref, f_ref, vec_ref, *rest):
        win_hbms, (wout_hbm, dhi_ref, dab_ref, s_ref, df_ref, acc_ref, win_v, wout_v, sems) = rest[:nwin], rest[nwin:]
        _load_once(_row_chunks(win_hbms, win_v) + [(wout_hbm, wout_v)], sems)
        _zero_first(acc_ref)
        g, sh, sc, gt = vec_ref[0:1, :], vec_ref[1:2, :], vec_ref[2:3, :], vec_ref[3:4, :]
        dho = dh_ref[...]
        df = (0.5 * gt * dho).astype(BF16)
        df_ref[...] = df
        dgt = _colsum(0.5 * dho * f_ref[...].astype(F32))
        du = None
        for j in range(FF // FC):
            lo, hi = j * FC, (j + 1) * FC
            av = a_ref[:, lo:hi].astype(F32)
            bv = b_ref[:, lo:hi].astype(F32)
            ds = _dot_nt(df, wout_v[lo:hi, :])
            sig = jax.nn.sigmoid(av)
            sa = av * sig
            s_ref[:, lo:hi] = (sa * bv).astype(BF16)
            da = (ds * bv * (sig * (1.0 + av * (1.0 - sig)))).astype(BF16)
            db = (ds * sa).astype(BF16)
            dab_ref[:, lo:hi] = da
            dab_ref[:, FF + lo:FF + hi] = db
            part = _dot_nt(da, win_v[:, lo:hi]) + _dot_nt(db, win_v[:, FF + lo:FF + hi])
            du = part if du is None else du + part
        xh, r, n, _ = _norm_fwd(h_ref[...], g, sh, sc)
        dhn, dsh, dsc, dg = _norm_bwd(du, xh, r, n, g, sc)
        dhi_ref[...] = dho + dhn
        acc_ref[0:1, :] += dsh
        acc_ref[1:2, :] += dsc
        acc_ref[2:3, :] += dg
        acc_ref[3:4, :] += dgt

    return _call(
        body, name, (T // TM,),
        [_rows(TM, D), _rows(TM, D), _rows(TM, FF), _rows(TM, FF), _rows(TM, D), _const((8, D))] + [ANY] * (nwin + 1),
        [_rows(TM, D), _rows(TM, 2 * FF), _rows(TM, FF), _rows(TM, D), _const((8, D))],
        [_sds((T, D), F32), _sds((T, 2 * FF), BF16), _sds((T, FF), BF16), _sds((T, D), BF16), _sds((8, D), F32)],
        scratch=[pltpu.VMEM((D, 2 * FF), BF16), pltpu.VMEM((FF, D), BF16), pltpu.SemaphoreType.DMA((nwin + 1,))],
        vmem=VMEM_BIG, comm=comm,
    )(dh, h, a, b, f, vec, *wins, wout)


def _wgrad(x, y, name, tk, tn, tt, out_dtype=BF16, comm=None):
    T, K = x.shape
    N = y.shape[1]
    nt = T // tt

    def body(x_ref, y_ref, o_ref, acc_ref):
        t = pl.program_id(2)
        part = _dot_tn(x_ref[...], y_ref[...])

        @pl.when(t == 0)
        def _():
            acc_ref[...] = part

        @pl.when(t > 0)
        def _():
            acc_ref[...] += part

        @pl.when(t == nt - 1)
        def _():
            o_ref[...] = acc_ref[...].astype(out_dtype)

    (out,), c_outs = _call(
        body, name, (K // tk, N // tn, nt),
        [pl.BlockSpec((tt, tk), lambda i, j, t: (t, i)), pl.BlockSpec((tt, tn), lambda i, j, t: (t, j))],
        [pl.BlockSpec((tk, tn), lambda i, j, t: (i, j))], [_sds((K, N), out_dtype)],
        scratch=[pltpu.VMEM((tk, tn), F32)], vmem=VMEM_BIG, comm=comm,
    )(x, y)
    return out, c_outs


def _wgrad_scatter(x, y, name, tt, comm=None):
    T, K = x.shape
    n = y.shape[1] // 4
    nt = T // tt
    assert nt >= 2, "a block's hand-over is added one grid step into the next block"
    half = K // 2
    nc = 0 if comm is None else len(comm.inputs)

    def body(chip_ref, x_ref, y_ref, *refs):
        c_ins, refs = refs[:nc], refs[nc:]
        recv_ref, refs = refs[0], refs[1:]
        c_outs, refs = refs[:nc], refs[nc:]
        acc_ref, keep_ref, give_ref, take_ref, local_sem, give_sems, take_sems, send_sems, recv_sems = refs[:9]
        j, t = pl.program_id(0), pl.program_id(1)
        px, py, pc = _place()

        def hand_over(jj):
            return pltpu.make_async_remote_copy(
                src_ref=give_ref.at[jj], dst_ref=take_ref.at[jj], send_sem=give_sems.at[jj], recv_sem=take_sems.at[jj],
                device_id=(px, py, 1 - pc), device_id_type=MESH)

        def send(jj):
            m = (3, 1, 2)[jj]
            return pltpu.make_async_remote_copy(
                src_ref=keep_ref.at[jj], dst_ref=recv_ref.at[m], send_sem=send_sems.at[jj], recv_sem=recv_sems.at[jj],
                device_id=_chip_peer(px, py, pc, m), device_id_type=MESH)

        def add_sibling(jj):
            hand_over(jj).wait_recv()
            keep_ref[jj] = (keep_ref[jj].astype(F32) + take_ref[jj].astype(F32)).astype(BF16)

        if comm is not None:
            @pl.when(jnp.logical_and(j == 0, t == 0))
            def _():
                comm.start(c_ins, c_outs, refs[9:])

        part = _dot_tn(x_ref[...], y_ref[...])

        @pl.when(t == 0)
        def _():
            acc_ref[...] = part

        @pl.when(t > 0)
        def _():
            acc_ref[...] += part

        for jj in range(3):
            @pl.when(jnp.logical_and(j == jj + 1, t == 0))
            def _():
                add_sibling(jj)
                send(jj).start()

        for jj in range(4):
            @pl.when(jnp.logical_and(j == jj, t == nt - 1))
            def _():
                keep_ref[jj] = acc_ref[pl.ds(pl.multiple_of(pc * half, 16), half), :].astype(BF16)
                give_ref[jj] = acc_ref[pl.ds(pl.multiple_of((1 - pc) * half, 16), half), :].astype(BF16)
                hand_over(jj).start()

        @pl.when(jnp.logical_and(j == 3, t == nt - 1))
        def _():
            add_sibling(3)
            own = pltpu.make_async_copy(keep_ref.at[3], recv_ref.at[0], local_sem.at[0])
            own.start()
            for jj in range(3):
                send(jj).wait_recv()
            for jj in range(3):
                send(jj).wait_send()
            for jj in range(4):
                hand_over(jj).wait_send()
            own.wait()
            if comm is not None:
                comm.relay(c_ins, c_outs, refs[9:])
                comm.wait(c_ins, c_outs, refs[9:])

    grid_spec = pltpu.PrefetchScalarGridSpec(
        num_scalar_prefetch=1, grid=(4, nt),
        in_specs=[pl.BlockSpec((tt, K), lambda j, t, chip: (t, 0)),
                  pl.BlockSpec((tt, n), lambda j, t, chip: (t, chip[0] ^ jnp.where(j == 0, 3, jnp.where(j == 3, 0, j))))]
        + [ANY] * nc,
        out_specs=[ANY] * (1 + nc),
        scratch_shapes=[pltpu.VMEM((K, n), F32)] + [pltpu.VMEM((4, half, n), BF16)] * 3
        + [pltpu.SemaphoreType.DMA((1,))] + [pltpu.SemaphoreType.DMA((4,))] * 2 + [pltpu.SemaphoreType.DMA((3,))] * 2
        + ([] if comm is None else list(comm.sem_shapes)))
    px, py, _ = _place()
    res = pl.pallas_call(
        body, name=name, grid_spec=grid_spec,
        out_shape=[_sds((4, half, n), BF16)] + ([] if comm is None else list(comm.out_shapes)),
        compiler_params=pltpu.CompilerParams(dimension_semantics=("arbitrary", "arbitrary"), vmem_limit_bytes=VMEM_BIG),
    )((2 * px + py).astype(jnp.int32).reshape(1), x, y, *([] if comm is None else comm.inputs))
    return res[0], res[1:]


def _swap_halves(t):
    w = t.shape[1]
    lane = lax.broadcasted_iota(jnp.int32, t.shape, 1)
    return jnp.where(lane % HD < HD // 2, pltpu.roll(t, w - HD // 2, 1), pltpu.roll(t, HD // 2, 1))


def _rope(t, cos, sin_signed):
    c = jnp.tile(cos, (1, t.shape[1] // cos.shape[1]))
    s = jnp.tile(sin_signed, (1, t.shape[1] // sin_signed.shape[1]))
    return t * c + _swap_halves(t) * s


def _rope_bwd(dt, cos, sin_signed):
    c = jnp.tile(cos, (1, dt.shape[1] // cos.shape[1]))
    s = jnp.tile(sin_signed, (1, dt.shape[1] // sin_signed.shape[1]))
    return dt * c + _swap_halves(dt * s)


def _rm_spec(dil):
    return pl.BlockSpec((dil, TM // dil, GA), lambda i: (0, i, 0))


def _to_residues(t, dst_ref, scr_ref, dil):
    if dil == 1:
        dst_ref[0] = t.astype(dst_ref.dtype)
        return
    for j in range(GA // LANES):
        scr_ref[j] = t[:, j * LANES:(j + 1) * LANES]
    for r in range(dil):
        for j in range(GA // LANES):
            rows = scr_ref.at[j][pl.ds(r, TM // dil, stride=dil), :]
            dst_ref[r, :, j * LANES:(j + 1) * LANES] = rows.astype(dst_ref.dtype)


def _from_residues(src_ref, scr_ref, dil):
    if dil == 1:
        return src_ref[0].astype(F32)
    for r in range(dil):
        for j in range(GA // LANES):
            scr_ref.at[j][pl.ds(r, TM // dil, stride=dil), :] = src_ref[r, :, j * LANES:(j + 1) * LANES].astype(F32)
    return jnp.concatenate([scr_ref[j] for j in range(GA // LANES)], axis=1)


def _mix_proj(h, vec, win, cos, sin, comm=None):
    T = h.shape[0]

    def body(h_ref, vec_ref, win_hbm, cos_ref, sin_ref, u_ref, p_ref, *rest):
        qkv_refs, gates_ref, win_v, scr_ref, sems = rest[:3 * NG], rest[3 * NG], rest[3 * NG + 1], rest[3 * NG + 2], rest[3 * NG + 3]
        _load_once([(win_hbm, win_v)], sems)
        g, sh, sc = vec_ref[0:1, :], vec_ref[1:2, :], vec_ref[2:3, :]
        _, _, _, u = _norm_fwd(h_ref[...], g, sh, sc)
        ub = u.astype(BF16)
        u_ref[...] = ub
        mixer_cols = PW + 3 * NG * GA
        proj = _dot(ub, win_v[:, 0:mixer_cols])
        p_ref[...] = proj[:, 0:PW]
        cos_t, sin_t = cos_ref[...], sin_ref[...]
        for j in range(3 * NG):
            col = PW + j * GA
            t = proj[:, col:col + GA]
            if j < 2 * NG:
                t = _rope(t, cos_t, sin_t)
            _to_residues(t, qkv_refs[j], scr_ref, DIL[j % NG])
        gates_ref[...] = jax.nn.sigmoid(_dot(ub, win_v[:, mixer_cols:INW])).astype(BF16)

    outs, c_outs = _call(
        body, "mix_proj", (T // TM,),
        [_rows(TM, D), _const((8, D)), ANY, _rows(TM, 128), _rows(TM, 128)],
        [_rows(TM, D), _rows(TM, PW)] + [_rm_spec(d) for d in DIL] * 3 + [_rows(TM, GW)],
        [_sds((T, D), BF16), _sds((T, PW), F32)] + [_sds((d, T // d, GA), BF16) for d in DIL] * 3 + [_sds((T, GW), BF16)],
        scratch=[pltpu.VMEM((D, INW), BF16), pltpu.VMEM((GA // LANES, TM, LANES), F32), pltpu.SemaphoreType.DMA((1,))],
        vmem=VMEM_BIG, comm=comm,
    )(h, vec, win, cos, sin)
    return (outs[0], outs[1], outs[2:2 + NG], outs[2 + NG:2 + 2 * NG], outs[2 + 2 * NG:2 + 3 * NG], outs[2 + 3 * NG]), c_outs


def _head_masks():
    lane_head = lax.broadcasted_iota(jnp.int32, (BLK, GA), 1) // HD
    return [lane_head == hd for hd in range(NH)]


def _expand_heads(t, hm):
    return jnp.concatenate([jnp.where(m, t, jnp.zeros_like(t)) for m in hm], axis=0)


def _collapse_heads(tb, hm):
    out = None
    for hd, m in enumerate(hm):
        part = jnp.where(m, tb[hd * BLK:(hd + 1) * BLK, :], 0.0)
        out = part if out is None else out + part
    return out


def _head_rows(t):
    return jnp.concatenate([t[:, hd * HD:hd * HD + 1] for hd in range(NH)], axis=0)


def _band(has_prev):
    a = lax.broadcasted_iota(jnp.int32, (NH * BLK, 2 * BLK), 0) & (BLK - 1)
    c = lax.broadcasted_iota(jnp.int32, (NH * BLK, 2 * BLK), 1)
    return jnp.logical_and(c >= jnp.where(has_prev, a, BLK), c <= a + BLK)


def _attn_fwd(q, k, v, nb, name, comm=None):
    T = q.shape[0]
    nbt = T // BLK

    def block(qv, kcat, vcat, has_prev, hm):
        s = jnp.where(_band(has_prev), _dot_nt(_expand_heads(qv, hm), kcat) * SCALE, NEG)
        mx = jnp.max(s, axis=-1, keepdims=True)
        e = jnp.exp(s - mx)
        l = jnp.sum(e, axis=-1, keepdims=True)
        ob = _dot((e * (1.0 / l)).astype(BF16), vcat)
        return _collapse_heads(ob, hm), _collapse_heads(jnp.broadcast_to(mx + jnp.log(l), (NH * BLK, GA)), hm)

    def body(q_ref, k_ref, kp_ref, v_ref, vp_ref, o_ref, lse_ref):
        b0 = FWD_BLOCKS * pl.program_id(0)
        hm = _head_masks()
        for b in range(FWD_BLOCKS):
            rows = slice(b * BLK, (b + 1) * BLK)
            if b == 0:
                kcat = jnp.concatenate([kp_ref[...], k_ref[rows, :]], axis=0)
                vcat = jnp.concatenate([vp_ref[...], v_ref[rows, :]], axis=0)
            else:
                kcat, vcat = k_ref[(b - 1) * BLK:(b + 1) * BLK, :], v_ref[(b - 1) * BLK:(b + 1) * BLK, :]
            o_ref[rows, :], lse_ref[rows, :] = block(q_ref[rows, :], kcat, vcat, ((b0 + b) & (nb - 1)) != 0, hm)

    cur = pl.BlockSpec((FWD_BLOCKS * BLK, GA), lambda i: (i, 0))
    prev = pl.BlockSpec((BLK, GA), lambda i: (jnp.maximum(FWD_BLOCKS * i - 1, 0), 0))
    return _call(body, name, (nbt // FWD_BLOCKS,), [cur, cur, prev, cur, prev], [cur, cur],
                 [_sds((T, GA), F32), _sds((T, GA), F32)], comm=comm)(q, k, k, v, v)


def _attn_bwd(q, k, v, do, lse, e, nb, name, comm=None):
    T = q.shape[0]
    nbt = T // BLK

    nblk = BWD_BLOCKS

    def probs_and_ds(qb, dob, kcat, vcat, lsev, ev, valid):
        p = jnp.where(valid, jnp.exp(_dot_nt(qb, kcat) * SCALE - _head_rows(lsev)), 0.0)
        return p.astype(BF16), (p * (_dot_nt(dob, vcat) + _head_rows(ev))).astype(BF16)

    def body(q_ref, k_ref, v_ref, do_ref, lse_ref, e_ref, kp_ref, vp_ref, qn_ref, don_ref, lsen_ref, en_ref,
             dq_ref, dk_ref, dv_ref):
        b0 = nblk * pl.program_id(0)
        hm = _head_masks()
        rows = [slice(b * BLK, (b + 1) * BLK) for b in range(nblk)]
        qs = [_expand_heads(q_ref[r, :], hm) for r in rows] + [_expand_heads(qn_ref[...], hm)]
        dos = [_expand_heads(do_ref[r, :], hm) for r in rows] + [_expand_heads(don_ref[...], hm)]
        ps, dss = [], []
        for b, r in enumerate(rows):
            if b == 0:
                kcat = jnp.concatenate([kp_ref[...], k_ref[r, :]], axis=0)
                vcat = jnp.concatenate([vp_ref[...], v_ref[r, :]], axis=0)
            else:
                kcat, vcat = k_ref[(b - 1) * BLK:(b + 1) * BLK, :], v_ref[(b - 1) * BLK:(b + 1) * BLK, :]
            p, ds = probs_and_ds(qs[b], dos[b], kcat, vcat, lse_ref[r, :], e_ref[r, :], _band(((b0 + b) & (nb - 1)) != 0))
            dq_ref[r, :] = _collapse_heads(_dot(ds, kcat) * SCALE, hm)
            ps.append(p)
            dss.append(ds)
        a = lax.broadcasted_iota(jnp.int32, (NH * BLK, BLK), 0) & (BLK - 1)
        c = lax.broadcasted_iota(jnp.int32, (NH * BLK, BLK), 1)
        valid_n = jnp.logical_and(c >= a, ((b0 + nblk) & (nb - 1)) != 0)
        p_n, ds_n = probs_and_ds(qs[nblk], dos[nblk], k_ref[rows[-1], :], v_ref[rows[-1], :], lsen_ref[...], en_ref[...], valid_n)
        for b, r in enumerate(rows):
            ds_after = dss[b + 1][:, :BLK] if b + 1 < nblk else ds_n
            p_after = ps[b + 1][:, :BLK] if b + 1 < nblk else p_n
            q_pair = jnp.concatenate([qs[b], qs[b + 1]], axis=0)
            do_pair = jnp.concatenate([dos[b], dos[b + 1]], axis=0)
            dk_ref[r, :] = _dot_tn(jnp.concatenate([dss[b][:, BLK:], ds_after], axis=0), q_pair) * SCALE
            dv_ref[r, :] = _dot_tn(jnp.concatenate([ps[b][:, BLK:], p_after], axis=0), do_pair).astype(BF16)

    cur = pl.BlockSpec((nblk * BLK, GA), lambda i: (i, 0))
    prev = pl.BlockSpec((BLK, GA), lambda i: (jnp.maximum(nblk * i - 1, 0), 0))
    nxt = pl.BlockSpec((BLK, GA), lambda i: (jnp.minimum(nblk * i + nblk, nbt - 1), 0))
    return _call(body, name, (nbt // nblk,), [cur] * 6 + [prev, prev] + [nxt] * 4, [cur, cur, cur],
                 [_sds((T, GA), F32), _sds((T, GA), F32), _sds((T, GA), BF16)],
                 comm=comm)(q, k, v, do, lse, e, k, v, q, do, lse, e)


def _flat(t):
    return t.reshape(t.shape[0] * t.shape[1], t.shape[2])


def _by_residue(t, dil):
    return t.reshape(dil, t.shape[0] // dil, t.shape[1])


def _pool_consts(shape, row0):
    lane = lax.broadcasted_iota(jnp.int32, shape, 1)
    t = lax.broadcasted_iota(jnp.int32, shape, 0) + row0
    grp = lane // (PW // len(POOL_WINDOWS))
    win = jnp.where(grp == 0, POOL_WINDOWS[0], jnp.where(grp == 1, POOL_WINDOWS[1],
                    jnp.where(grp == 2, POOL_WINDOWS[2], POOL_WINDOWS[3])))
    cnt = jnp.minimum(t + 1, win).astype(F32)
    return grp, cnt


def _window_sums(ext_ref, base, step, tm):
    outs, run = [], None
    for j in range(POOL_WINDOWS[-1]):
        sl = ext_ref[pl.ds(base + step * j, tm), :]
        run = sl if run is None else run + sl
        if j + 1 in POOL_WINDOWS:
            outs.append(run)
    return outs


def _select_group(grp, vals):
    return jnp.where(grp == 0, vals[0], jnp.where(grp == 1, vals[1], jnp.where(grp == 2, vals[2], vals[3])))


def _pool_d(pc_ref, pp_ref, ext_ref, i, tm):
    ext_ref[0:HALO, :] = jnp.where(i > 0, pp_ref[tm - HALO:tm, :], 0.0)
    ext_ref[HALO:HALO + tm, :] = pc_ref[...]
    grp, cnt = _pool_consts((tm, PW), i * tm)
    sums = _window_sums(ext_ref, HALO, -1, tm)
    return _select_group(grp, sums) / cnt - pc_ref[...]


def _group_weights(ls):
    mx = jnp.maximum(jnp.maximum(ls[0], ls[1]), ls[2])
    es = [jnp.exp(l - mx) for l in ls]
    inv = 1.0 / (es[0] + es[1] + es[2])
    return [e * inv for e in es]


def _mix_merge(h, vec, p, os, lses, gates, wp_bd, pscale, wpb, wab, wout, comm=None):
    T = h.shape[0]

    def body(h_ref, vec_ref, pc_ref, pp_ref, o0, o1, o2, l0, l1, l2, gates_ref, wp_ref, ps_ref, wpb_ref, wab_ref, wout_ref,
             ho_ref, yp_ref, ya_ref, mg_ref, mo_ref, d_ref, ext_ref, scr_ref):
        i = pl.program_id(0)
        gt = vec_ref[3:4, :]
        d = _pool_d(pc_ref, pp_ref, ext_ref, i, TM).astype(BF16)
        d_ref[...] = d
        ypool = (_dot(d, wp_ref[...]) * ps_ref[0:1, :]).astype(BF16)
        yp_ref[...] = ypool
        w = _group_weights([_from_residues(r, scr_ref, dl) for r, dl in zip((l0, l1, l2), DIL)])
        yattn = None
        for wg, o_ref, dl in zip(w, (o0, o1, o2), DIL):
            part = wg * _from_residues(o_ref, scr_ref, dl)
            yattn = part if yattn is None else yattn + part
        yattn = yattn.astype(BF16)
        ya_ref[...] = yattn
        merged = (gates_ref[:, 0:D].astype(F32) * _dot(ypool, wpb_ref[...])
                  + gates_ref[:, D:GW].astype(F32) * _dot(yattn, wab_ref[...])).astype(BF16)
        mg_ref[...] = merged
        mo = _dot(merged, wout_ref[...])
        mo_ref[...] = mo.astype(BF16)
        ho_ref[...] = h_ref[...] + gt * mo

    prev = pl.BlockSpec((TM, PW), lambda i: (jnp.maximum(i - 1, 0), 0))
    return _call(
        body, "mix_merge", (T // TM,),
        [_rows(TM, D), _const((8, D)), _rows(TM, PW), prev] + [_rm_spec(dl) for dl in DIL] * 2 + [_rows(TM, GW), _const((PW, PW)),
         _const((8, PW)), _const((PW, D)), _const((GA, D)), _const((D, D))],
        [_rows(TM, D), _rows(TM, PW), _rows(TM, GA), _rows(TM, D), _rows(TM, D), _rows(TM, PW)],
        [_sds((T, D), F32), _sds((T, PW), BF16), _sds((T, GA), BF16), _sds((T, D), BF16), _sds((T, D), BF16), _sds((T, PW), BF16)],
        scratch=[pltpu.VMEM((TM + HALO, PW), F32), pltpu.VMEM((GA // LANES, TM, LANES), F32)],
        vmem=VMEM_BIG, comm=comm,
    )(h, vec, p, p, *os, *lses, gates, wp_bd, pscale, wpb, wab, wout)


def _mix_bwd_a(dh, vec, mixout, merged, gates, ypool, yattn, dpool, os, lses, wp_bd, pscale, wpb, wab, wout, ones_bd,
               comm=None):
    T = dh.shape[0]
    nt = T // TM

    def body(dh_ref, vec_ref, mo_ref, mg_ref, gates_ref, yp_ref, ya_ref, d_ref, o0, o1, o2, l0, l1, l2,
             wp_ref, ps_ref, wpb_ref, wab_ref, wout_ref, ones_ref,
             dgates_ref, do0, do1, do2, e0, e1, e2, dd_ref, acc_ref, acc2_ref, g_out_ref, g_pb_ref, g_ab_ref, g_pool_ref,
             scr_ref, a_out, a_pb, a_ab, a_pool):
        _zero_first(acc_ref)
        _zero_first(acc2_ref)
        for a_ref in (a_out, a_pb, a_ab, a_pool):
            _zero_first(a_ref)
        gt = vec_ref[3:4, :]
        dho = dh_ref[...]
        acc_ref[3:4, :] += _colsum(dho * mo_ref[...].astype(F32))
        dmo = (gt * dho).astype(BF16)
        a_out[...] += _dot_tn(mg_ref[...], dmo)
        dmerged = _dot_nt(dmo, wout_ref[...])
        gp = gates_ref[:, 0:D].astype(F32)
        ga = gates_ref[:, D:GW].astype(F32)
        bp = _dot(yp_ref[...], wpb_ref[...])
        ba = _dot(ya_ref[...], wab_ref[...])
        dgates_ref[:, 0:D] = (dmerged * bp * gp * (1.0 - gp)).astype(BF16)
        dgates_ref[:, D:GW] = (dmerged * ba * ga * (1.0 - ga)).astype(BF16)
        dbp = (dmerged * gp).astype(BF16)
        dba = (dmerged * ga).astype(BF16)
        a_pb[...] += _dot_tn(yp_ref[...], dbp)
        a_ab[...] += _dot_tn(ya_ref[...], dba)
        dypool = _dot_nt(dbp, wpb_ref[...])
        ypre = _dot(d_ref[...], wp_ref[...])
        acc2_ref[0:1, :] += _colsum(dypool * ypre)
        dyp = (dypool * ps_ref[0:1, :]).astype(BF16)
        a_pool[...] += _dot_tn(d_ref[...], dyp)
        dd_ref[...] = _dot_nt(dyp, wp_ref[...])
        dya = _dot_nt(dba, wab_ref[...])
        w = _group_weights([_from_residues(r, scr_ref, dl) for r, dl in zip((l0, l1, l2), DIL)])
        ya = None
        for wg, o_ref, dl in zip(w, (o0, o1, o2), DIL):
            part = wg * _from_residues(o_ref, scr_ref, dl)
            ya = part if ya is None else ya + part
        prod = dya * ya
        hi = prod.astype(BF16)
        lo = (prod - hi.astype(F32)).astype(BF16)
        tot = _dot(hi, ones_ref[...]) + _dot(lo, ones_ref[...])
        for wg, do_ref, e_ref, dl in zip(w, (do0, do1, do2), (e0, e1, e2), DIL):
            _to_residues(wg * dya, do_ref, scr_ref, dl)
            _to_residues(-wg * tot, e_ref, scr_ref, dl)

        @pl.when(pl.program_id(0) == nt - 1)
        def _():
            g_out_ref[...] = a_out[...].astype(BF16)
            g_pb_ref[...] = a_pb[...].astype(BF16)
            g_ab_ref[...] = a_ab[...].astype(BF16)
            g_pool_ref[...] = a_pool[...]

    return _call(
        body, "mix_bwd_a", (nt,),
        [_rows(TM, D), _const((8, D)), _rows(TM, D), _rows(TM, D), _rows(TM, GW), _rows(TM, PW), _rows(TM, GA), _rows(TM, PW)]
        + [_rm_spec(dl) for dl in DIL] * 2
        + [_const((PW, PW)), _const((8, PW)), _const((PW, D)), _const((GA, D)), _const((D, D)), _const((GA, GA))],
        [_rows(TM, GW)] + [_rm_spec(dl) for dl in DIL] * 2 + [_rows(TM, PW), _const((8, D)), _const((8, PW))]
        + [_const((D, D)), _const((PW, D)), _const((GA, D)), _const((PW, PW))],
        [_sds((T, GW), BF16)] + [_sds((dl, T // dl, GA), BF16) for dl in DIL]
        + [_sds((dl, T // dl, GA), F32) for dl in DIL] + [_sds((T, PW), F32), _sds((8, D), F32), _sds((8, PW), F32)]
        + [_sds((D, D), BF16), _sds((PW, D), BF16), _sds((GA, D), BF16), _sds((PW, PW), F32)],
        scratch=[pltpu.VMEM((GA // LANES, TM, LANES), F32), pltpu.VMEM((D, D), F32), pltpu.VMEM((PW, D), F32),
                 pltpu.VMEM((GA, D), F32), pltpu.VMEM((PW, PW), F32)],
        vmem=VMEM_BIG, comm=comm,
    )(dh, vec, mixout, merged, gates, ypool, yattn, dpool, *os, *lses, wp_bd, pscale, wpb, wab, wout, ones_bd)


def _mix_bwd_b(dh, h, vec, dd, dqs, dks, dvs, dgates, cos, sin, win):
    T = h.shape[0]
    nt = T // TM

    def body(dh_ref, h_ref, vec_ref, ddc_ref, ddn_ref, *rest):
        qk_refs, dv_refs = rest[:2 * NG], rest[2 * NG:3 * NG]
        dgates_ref, cos_ref, sin_ref, win_hbm, dhi_ref, dproj_ref, acc_ref, win_v, ext_ref, scr_ref, sems = rest[3 * NG:]
        i = pl.program_id(0)
        _load_once([(win_hbm, win_v)], sems)
        _zero_first(acc_ref)
        g, sh, sc = vec_ref[0:1, :], vec_ref[1:2, :], vec_ref[2:3, :]
        grp, cnt = _pool_consts((TM, PW), i * TM)
        _, cnt_n = _pool_consts((HALO, PW), (i + 1) * TM)
        ext_ref[0:TM, :] = ddc_ref[...] / cnt
        ext_ref[TM:TM + HALO, :] = jnp.where(i < nt - 1, ddn_ref[0:HALO, :] / cnt_n, 0.0)
        dp = _select_group(grp, _window_sums(ext_ref, 0, 1, TM)) - ddc_ref[...]
        dproj_ref[:, 0:PW] = dp.astype(BF16)
        cos_t, sin_t = cos_ref[...], sin_ref[...]
        for j in range(2 * NG):
            col = PW + j * GA
            dt = _from_residues(qk_refs[j], scr_ref, DIL[j % NG])
            dproj_ref[:, col:col + GA] = _rope_bwd(dt, cos_t, sin_t).astype(BF16)
        for j in range(NG):
            col = PW + (2 * NG + j) * GA
            dproj_ref[:, col:col + GA] = _from_residues(dv_refs[j], scr_ref, DIL[j]).astype(BF16)
        dproj_ref[:, PW + 3 * NG * GA:INW] = dgates_ref[...]
        du = None
        for j in range(INW // 512):
            part = _dot_nt(dproj_ref[:, j * 512:(j + 1) * 512], win_v[:, j * 512:(j + 1) * 512])
            du = part if du is None else du + part
        xh, r, n, _ = _norm_fwd(h_ref[...], g, sh, sc)
        dhn, dsh, dsc, dg = _norm_bwd(du, xh, r, n, g, sc)
        dhi_ref[...] = dh_ref[...] + dhn
        acc_ref[0:1, :] += dsh
        acc_ref[1:2, :] += dsc
        acc_ref[2:3, :] += dg

    nxt = pl.BlockSpec((TM, PW), lambda i: (jnp.minimum(i + 1, nt - 1), 0))
    return _call(
        body, "mix_bwd_b", (nt,),
        [_rows(TM, D), _rows(TM, D), _const((8, D)), _rows(TM, PW), nxt] + [_rm_spec(dl) for dl in DIL] * 3
        + [_rows(TM, GW), _rows(TM, 128), _rows(TM, 128), ANY],
        [_rows(TM, D), _rows(TM, INW), _const((8, D))],
        [_sds((T, D), F32), _sds((T, INW), BF16), _sds((8, D), F32)],
        scratch=[pltpu.VMEM((D, INW), BF16), pltpu.VMEM((TM + HALO, PW), F32), pltpu.VMEM((GA // LANES, TM, LANES), F32),
                 pltpu.SemaphoreType.DMA((1,))],
        vmem=VMEM_BIG,
    )(dh, h, vec, dd, dd, *dqs, *dks, *dvs, dgates, cos, sin, win)[0]


def _ada_fwd(c_all, w_shard, b_shard):
    n = w_shard.shape[1]

    def body(c_ref, w_ref, b_ref, o_ref):
        cv = c_ref[...]
        cond = (cv * jax.nn.sigmoid(cv)).astype(BF16)
        o_ref[...] = _dot(cond, w_ref[...].astype(BF16)) + b_ref[...]

    tn = n // 3
    return pl.pallas_call(
        body, name="ada_fwd", grid=(3,),
        in_specs=[pl.BlockSpec((8, D), lambda j: (0, 0)), pl.BlockSpec((D, tn), lambda j: (0, j)), pl.BlockSpec((1, tn), lambda j: (0, j))],
        out_specs=pl.BlockSpec((8, tn), lambda j: (0, j)), out_shape=_sds((8, n), F32),
        compiler_params=pltpu.CompilerParams(dimension_semantics=("arbitrary",)),
    )(c_all, w_shard, b_shard)


def _ada_bwd(c_all, dmod_shard):
    n = dmod_shard.shape[1]

    def body(c_ref, d_ref, o_ref):
        cv = c_ref[...]
        cond = (cv * jax.nn.sigmoid(cv)).astype(BF16)
        o_ref[...] = _dot_tn(cond, d_ref[...].astype(BF16))

    tn = n // 3
    return pl.pallas_call(
        body, name="ada_bwd", grid=(3,),
        in_specs=[pl.BlockSpec((8, D), lambda j: (0, 0)), pl.BlockSpec((8, tn), lambda j: (0, j))],
        out_specs=pl.BlockSpec((D, tn), lambda j: (0, j)), out_shape=_sds((D, n), F32),
        compiler_params=pltpu.CompilerParams(dimension_semantics=("arbitrary",)),
    )(c_all, dmod_shard)


def _adam_math(w, g, m, v):
    m2 = B1 * m + (1.0 - B1) * g
    v2 = B2 * v + (1.0 - B2) * (g * g)
    m_hat = m2 / (1.0 - B1 ** STEP)
    v_hat = v2 / (1.0 - B2 ** STEP)
    delta = -LR * (m_hat / (jnp.sqrt(v_hat) + AEPS) + WD * w)
    return delta, m2, v2


def _adam(w, m, v, parts, name, comm=None):
    R, C = w.shape
    tr = R
    for cand in (128, 64, 32, 16, 8):
        if R % cand == 0:
            tr = cand
            break
    np_ = len(parts)

    def body(w_ref, m_ref, v_ref, *rest):
        p_refs, (g_ref, d_ref, m2_ref, v2_ref) = rest[:np_], rest[np_:]
        g = p_refs[0][...]
        for pr in p_refs[1:]:
            g = g + pr[...]
        delta, m2, v2 = _adam_math(w_ref[...], g, m_ref[...], v_ref[...])
        g_ref[...] = g
        d_ref[...] = delta
        m2_ref[...] = m2
        v2_ref[...] = v2

    spec = pl.BlockSpec((tr, C), lambda i: (i, 0))
    return _call(body, name, (R // tr,), [spec] * (3 + np_), [spec] * 4, [_sds((R, C), F32)] * 4,
                 vmem=VMEM_BIG, comm=comm)(w, m, v, *parts)


def _adam_halves(w, m, v, mine, other, name):
    R, C = w.shape
    tr = 128
    nh = R // 2 // tr

    def body(c_ref, w_ref, m_ref, v_ref, mine_ref, other_ref, g_ref, d_ref, m2_ref, v2_ref):
        i = pl.program_id(0)
        in_mine = jnp.logical_and(i >= c_ref[0] * nh, i < (c_ref[0] + 1) * nh)
        g = jnp.where(in_mine, mine_ref[...], other_ref[...])
        delta, m2, v2 = _adam_math(w_ref[...], g, m_ref[...], v_ref[...])
        g_ref[...] = g
        d_ref[...] = delta
        m2_ref[...] = m2
        v2_ref[...] = v2

    spec = pl.BlockSpec((tr, C), lambda i, c: (i, 0))
    grid_spec = pltpu.PrefetchScalarGridSpec(
        num_scalar_prefetch=1, grid=(R // tr,),
        in_specs=[spec] * 3 + [pl.BlockSpec((tr, C), lambda i, c: (jnp.clip(i - c[0] * nh, 0, nh - 1), 0)),
                               pl.BlockSpec((tr, C), lambda i, c: (jnp.clip(i - (1 - c[0]) * nh, 0, nh - 1), 0))],
        out_specs=[spec] * 4)
    return pl.pallas_call(
        body, name=name, grid_spec=grid_spec, out_shape=[_sds((R, C), F32)] * 4,
        compiler_params=pltpu.CompilerParams(dimension_semantics=("arbitrary",), vmem_limit_bytes=VMEM_BIG),
    )(lax.axis_index("c").astype(jnp.int32).reshape(1), w, m, v, mine, other)


def _adam_small(ws, ms, vs, gathered):
    n = len(ws)
    sizes = [a.shape[1] for a in ws]

    def total(ga_ref, off, size):
        g = ga_ref[0, :, off:off + size]
        for dev in range(1, 8):
            g = g + ga_ref[dev, :, off:off + size]
        return g

    def body(*refs):
        w_refs, m_refs, v_refs, ga_ref, outs = refs[:n], refs[n:2 * n], refs[2 * n:3 * n], refs[3 * n], refs[3 * n + 1:]
        off = 0
        for j, size in enumerate(sizes):
            g = total(ga_ref, off, size)
            delta, m2, v2 = _adam_math(w_refs[j][...], g, m_refs[j][...], v_refs[j][...])
            for ref, val in zip(outs[4 * j:4 * j + 4], (g, delta, m2, v2)):
                ref[...] = val
            off += size
        outs[4 * n][...] = total(ga_ref, off, 128)

    res = pl.pallas_call(
        body, name="adam_small",
        out_shape=[_sds((1, size), F32) for size in sizes for _ in range(4)] + [_sds((1, 128), F32)],
    )(*ws, *ms, *vs, gathered)
    return [res[4 * j:4 * j + 4] for j in range(n)], res[4 * n]


def _sum4(blocks, name):
    _, R, C = blocks.shape
    tr = R
    for cand in (256, 128, 64, 32, 16):
        if R % cand == 0:
            tr = cand
            break

    def body(r_ref, out_ref):
        out_ref[...] = ((r_ref[0].astype(F32) + r_ref[1].astype(F32)) + r_ref[2].astype(F32)) + r_ref[3].astype(F32)

    return pl.pallas_call(
        body, name=name, grid=(R // tr,),
        in_specs=[pl.BlockSpec((4, tr, C), lambda i: (0, i, 0))],
        out_specs=pl.BlockSpec((tr, C), lambda i: (i, 0)), out_shape=_sds((R, C), F32),
        compiler_params=pltpu.CompilerParams(dimension_semantics=("arbitrary",)),
    )(blocks)


def _place():
    return lax.axis_index("x"), lax.axis_index("y"), lax.axis_index("c")


def _chip_peer(x, y, c, m):
    return (x ^ (m >> 1), y ^ (m & 1), c)


def _shard_ref(ref, axis, k, n):
    start = pl.multiple_of(k * n, 128 if axis == 1 else 16)
    return ref.at[:, pl.ds(start, n)] if axis == 1 else ref.at[pl.ds(start, n), :]


def _half_rows(ref, axis, k, n, hc):
    if axis == 1:
        half = ref.shape[0] // 2
        return ref.at[pl.ds(pl.multiple_of(hc * half, 16), half), pl.ds(pl.multiple_of(k * n, 128), n)]
    half = n // 2
    return ref.at[pl.ds(pl.multiple_of(k * n + hc * half, 16), half), :]


class _GatherPlan:
    def __init__(self, shards, axes):
        self.inputs, self.axes, nw = list(shards), list(axes), len(shards)
        self.out_shapes = [_sds((s.shape[0] * (4 if ax == 0 else 1), s.shape[1] * (4 if ax == 1 else 1)), BF16)
                           for s, ax in zip(shards, axes)]
        self.sem_shapes = [pltpu.SemaphoreType.DMA((nw,))] + [pltpu.SemaphoreType.DMA((nw, 3))] * 4

    def _copies(self, ins, outs, sems):
        local_sems, send_sems, recv_sems, pass_sems, got_sems = sems
        x, y, c = _place()
        k = 2 * x + y
        local, sends, arrivals, passes, handed = [], [], [], [], []
        for j, ax in enumerate(self.axes):
            n = ins[j].shape[ax]
            half = ins[j].shape[0] // 2
            local.append(pltpu.make_async_copy(ins[j], _shard_ref(outs[j], ax, k, n), local_sems.at[j]))
            my_half = ins[j].at[pl.ds(pl.multiple_of(c * half, 16), half), :]
            for m in range(1, 4):
                sends.append(pltpu.make_async_remote_copy(
                    src_ref=my_half, dst_ref=_half_rows(outs[j], ax, k, n, c), send_sem=send_sems.at[j, m - 1],
                    recv_sem=recv_sems.at[j, m - 1], device_id=_chip_peer(x, y, c, m), device_id_type=MESH))
                theirs = _half_rows(outs[j], ax, k ^ m, n, c)
                arrivals.append(pltpu.make_async_remote_copy(
                    src_ref=my_half, dst_ref=theirs, send_sem=send_sems.at[j, m - 1], recv_sem=recv_sems.at[j, m - 1],
                    device_id=(x, y, c), device_id_type=MESH))
                passes.append(pltpu.make_async_remote_copy(
                    src_ref=theirs, dst_ref=theirs, send_sem=pass_sems.at[j, m - 1], recv_sem=got_sems.at[j, m - 1],
                    device_id=(x, y, 1 - c), device_id_type=MESH))
                other = _half_rows(outs[j], ax, k ^ m, n, 1 - c)
                handed.append(pltpu.make_async_remote_copy(
                    src_ref=other, dst_ref=other, send_sem=pass_sems.at[j, m - 1], recv_sem=got_sems.at[j, m - 1],
                    device_id=(x, y, c), device_id_type=MESH))
        return local, sends, arrivals, passes, handed

    def start(self, ins, outs, sems):
        local, sends, _, _, _ = self._copies(ins, outs, sems)
        for cp in local + sends:
            cp.start()

    def relay(self, ins, outs, sems):
        _, _, arrivals, passes, _ = self._copies(ins, outs, sems)
        for arrived, onward in zip(arrivals, passes):
            arrived.wait_recv()
            onward.start()

    def wait(self, ins, outs, sems):
        local, sends, _, passes, handed = self._copies(ins, outs, sems)
        for cp in handed:
            cp.wait_recv()
        for cp in sends + passes:
            cp.wait_send()
        for cp in local:
            cp.wait()


class _ScatterPlan:
    def __init__(self, grads, axes):
        self.inputs, self.axes, nw = list(grads), list(axes), len(grads)
        self.shard_shapes = [(g.shape[0] // (4 if ax == 0 else 1), g.shape[1] // (4 if ax == 1 else 1))
                             for g, ax in zip(grads, axes)]
        self.out_shapes = [_sds((4,) + s, BF16) for s in self.shard_shapes]
        self.sem_shapes = [pltpu.SemaphoreType.DMA((nw,)), pltpu.SemaphoreType.DMA((nw, 3)), pltpu.SemaphoreType.DMA((nw, 3))]

    def _copies(self, ins, outs, sems):
        local_sems, send_sems, recv_sems = sems
        x, y, c = _place()
        k = 2 * x + y
        local, remote, arrivals = [], [], []
        for j, ax in enumerate(self.axes):
            n = self.shard_shapes[j][ax]
            local.append(pltpu.make_async_copy(_shard_ref(ins[j], ax, k, n), outs[j].at[0], local_sems.at[j]))
            for m in range(1, 4):
                remote.append(pltpu.make_async_remote_copy(
                    src_ref=_shard_ref(ins[j], ax, k ^ m, n), dst_ref=outs[j].at[m],
                    send_sem=send_sems.at[j, m - 1], recv_sem=recv_sems.at[j, m - 1],
                    device_id=_chip_peer(x, y, c, m), device_id_type=MESH))
                arrivals.append(pltpu.make_async_remote_copy(
                    src_ref=_shard_ref(ins[j], ax, k, n), dst_ref=outs[j].at[m],
                    send_sem=send_sems.at[j, m - 1], recv_sem=recv_sems.at[j, m - 1],
                    device_id=(x, y, c), device_id_type=MESH))
        return local, remote, arrivals

    def start(self, ins, outs, sems):
        local, remote, _ = self._copies(ins, outs, sems)
        for cp in local + remote:
            cp.start()

    def relay(self, ins, outs, sems):
        pass

    def wait(self, ins, outs, sems):
        local, remote, arrivals = self._copies(ins, outs, sems)
        for cp in arrivals:
            cp.wait_recv()
        for cp in remote:
            cp.wait_send()
        for cp in local:
            cp.wait()


def _run_plan(plan, name):
    nc = len(plan.inputs)

    def body(*refs):
        ins, outs, sems = refs[:nc], refs[nc:2 * nc], refs[2 * nc:]
        plan.start(ins, outs, sems)
        plan.relay(ins, outs, sems)
        plan.wait(ins, outs, sems)

    return pl.pallas_call(body, name=name, in_specs=[ANY] * nc, out_specs=[ANY] * nc, out_shape=list(plan.out_shapes),
                          scratch_shapes=list(plan.sem_shapes))(*plan.inputs)


class _SwapPlan:
    def __init__(self, parts):
        self.inputs, nw = list(parts), len(parts)
        self.out_shapes = [_sds(p.shape, p.dtype) for p in parts]
        self.sem_shapes = [pltpu.SemaphoreType.DMA((nw,)), pltpu.SemaphoreType.DMA((nw,))]

    def _copies(self, ins, outs, sems):
        send_sems, recv_sems = sems
        x, y, c = _place()
        return [pltpu.make_async_remote_copy(
            src_ref=ins[j], dst_ref=outs[j], send_sem=send_sems.at[j], recv_sem=recv_sems.at[j],
            device_id=(x, y, 1 - c), device_id_type=MESH) for j in range(len(ins))]

    def start(self, ins, outs, sems):
        for cp in self._copies(ins, outs, sems):
            cp.start()

    def relay(self, ins, outs, sems):
        pass

    def wait(self, ins, outs, sems):
        for cp in self._copies(ins, outs, sems):
            cp.wait()


class _SmallGatherPlan:
    def __init__(self, v):
        self.inputs = [v]
        self.out_shapes = [_sds((8,) + v.shape, v.dtype)]
        self.sem_shapes = [pltpu.SemaphoreType.DMA((1,)), pltpu.SemaphoreType.DMA((7,)), pltpu.SemaphoreType.DMA((7,))]

    def _copies(self, ins, outs, sems):
        (v_ref,), (out_ref,), (local_sem, send_sems, recv_sems) = ins, outs, sems
        x, y, c = _place()
        me = 4 * x + 2 * y + c
        local = pltpu.make_async_copy(v_ref, out_ref.at[me], local_sem.at[0])
        sends, arrivals = [], []
        for m in range(1, 8):
            px, py, pc = x ^ (m >> 2), y ^ ((m >> 1) & 1), c ^ (m & 1)
            sends.append(pltpu.make_async_remote_copy(
                src_ref=v_ref, dst_ref=out_ref.at[me], send_sem=send_sems.at[m - 1], recv_sem=recv_sems.at[m - 1],
                device_id=(px, py, pc), device_id_type=MESH))
            arrivals.append(pltpu.make_async_remote_copy(
                src_ref=v_ref, dst_ref=out_ref.at[4 * px + 2 * py + pc], send_sem=send_sems.at[m - 1],
                recv_sem=recv_sems.at[m - 1], device_id=(x, y, c), device_id_type=MESH))
        return local, sends, arrivals

    def start(self, ins, outs, sems):
        local, sends, _ = self._copies(ins, outs, sems)
        for cp in [local] + sends:
            cp.start()

    def relay(self, ins, outs, sems):
        pass

    def wait(self, ins, outs, sems):
        local, sends, arrivals = self._copies(ins, outs, sems)
        for cp in arrivals:
            cp.wait_recv()
        for cp in sends:
            cp.wait_send()
        local.wait()


class _PlanGroup:
    def __init__(self, plans):
        self.plans = [p for p in plans if p is not None]
        self.inputs = [a for p in self.plans for a in p.inputs]
        self.out_shapes = [s for p in self.plans for s in p.out_shapes]
        self.sem_shapes = [s for p in self.plans for s in p.sem_shapes]

    def _each(self, ins, outs, sems):
        i = s = 0
        for p in self.plans:
            n, ns = len(p.inputs), len(p.sem_shapes)
            yield p, ins[i:i + n], outs[i:i + n], sems[s:s + ns]
            i, s = i + n, s + ns

    def start(self, ins, outs, sems):
        for p, pi, po, ps in self._each(ins, outs, sems):
            p.start(pi, po, ps)

    def relay(self, ins, outs, sems):
        for p, pi, po, ps in self._each(ins, outs, sems):
            p.relay(pi, po, ps)

    def wait(self, ins, outs, sems):
        for p, pi, po, ps in self._each(ins, outs, sems):
            p.wait(pi, po, ps)

    def split(self, outs):
        res, i = [], 0
        for p in self.plans:
            res.append(outs[i:i + len(p.inputs)])
            i += len(p.inputs)
        return res


BIG = ("w_ffn1_in", "w_ffn1_out", "w_in", "w_pool_branch", "w_attn_branch", "w_out", "w_ffn2_in", "w_ffn2_out")
BIG_AXIS = {"w_ffn1_in": 1, "w_ffn1_out": 0, "w_in": 1, "w_pool_branch": 1, "w_attn_branch": 1, "w_out": 0,
            "w_ffn2_in": 1, "w_ffn2_out": 0}


class _Sharded:
    fused_scatter = True

    def __init__(self, shards):
        self.shards, self.full, self.recv = shards, {}, {}

    def gather_plan(self, names):
        return _GatherPlan([self.shards[n] for n in names], [BIG_AXIS[n.split("/")[0]] for n in names])

    def gather_now(self, names):
        self.gathered(names, _run_plan(self.gather_plan(names), "gather_" + names[0]))

    def gathered(self, names, outs):
        self.full.update(zip(names, outs))

    def scatter_plan(self, names, grads):
        return _ScatterPlan([grads[n] for n in names], [BIG_AXIS[n] for n in names])

    def scatter_now(self, names, grads):
        self.scattered(names, _run_plan(self.scatter_plan(names, grads), "scatter_" + names[0]))

    def scattered(self, names, outs):
        self.recv.update(zip(names, outs))


class _Whole:
    fused_scatter = False

    def __init__(self, full):
        self.full, self.recv = dict(full), {}

    def gather_plan(self, names):
        return None

    def gather_now(self, names):
        pass

    def gathered(self, names, outs):
        pass

    def scatter_plan(self, names, grads):
        return None

    def scatter_now(self, names, grads):
        pass

    def scattered(self, names, outs):
        pass


def _vec(rows):
    pad = [jnp.zeros((1, D), F32)] * (8 - len(rows))
    return jnp.concatenate([r.reshape(1, D) for r in rows] + pad, axis=0)


def _block_diag(w_pool):
    n, c = w_pool.shape[0], w_pool.shape[1]
    eye = jnp.eye(n, dtype=w_pool.dtype)
    return (eye[:, None, :, None] * w_pool[:, :, None, :]).reshape(n * c, n * c)


def _example_step(x, tgt, positions, mod, gains, w_pool, pool_scale, ws, pack=None):
    T = x.shape[0]
    assert (T // BLK // DIL[-1]) & (T // BLK // DIL[-1] - 1) == 0, "blocks per sequence must be a power of two"
    sh1, sc1, gt1, sh2, sc2, gt2, sh3, sc3, gt3 = [mod[j * D:(j + 1) * D] for j in range(NMOD)]
    g1, g2, g3, gf = gains
    vec1, vec2, vec3 = _vec([g1, sh1, sc1, gt1]), _vec([g2, sh2, sc2, gt2]), _vec([g3, sh3, sc3, gt3])
    inv_freq = 10000.0 ** (-jnp.arange(0, HD, 2, dtype=F32) / HD)
    ang = positions.astype(F32)[:, None] * inv_freq
    cos = jnp.tile(jnp.cos(ang), (1, 4))
    sin = jnp.tile(jnp.concatenate([-jnp.sin(ang), jnp.sin(ang)], axis=1), (1, 2))
    wp_bd = _block_diag(w_pool).astype(BF16)
    ones_bd = _block_diag(jnp.ones((NH, HD, HD), F32)).astype(BF16)
    ps = jnp.concatenate([pool_scale.reshape(1, PW), jnp.zeros((7, PW), F32)], axis=0)
    wb = ws.full

    if "w_ffn1_in" not in wb:
        ws.gather_now(["w_ffn1_in"])
    (u1, a1, b1), got = _ffn_ab(x, vec1, [wb["w_ffn1_in"]], "ffn1_ab", ws.gather_plan(["w_ffn1_out", "w_in"]))
    ws.gathered(["w_ffn1_out", "w_in"], got)
    mixw = ["w_pool_branch", "w_attn_branch", "w_out"]
    (h1, f1), got = _ffn_out(x, a1, b1, vec1, wb["w_ffn1_out"], "ffn1_out", ws.gather_plan(mixw))
    ws.gathered(mixw, got)
    (u2, p, qs, ks, vs, gates), got = _mix_proj(h1, vec2, wb["w_in"], cos, sin, ws.gather_plan(["w_ffn2_in/0", "w_ffn2_out"]))
    ws.gathered(["w_ffn2_in/0", "w_ffn2_out"], got)
    qs, ks, vs = [_flat(t) for t in qs], [_flat(t) for t in ks], [_flat(t) for t in vs]
    nbs = [T // d // BLK for d in DIL]
    os, lses = [], []
    for gi in range(NG):
        (o, lse), _ = _attn_fwd(qs[gi], ks[gi], vs[gi], nbs[gi], f"attn_fwd{gi}")
        os.append(o)
        lses.append(lse)
    os_r = [_by_residue(t, d) for t, d in zip(os, DIL)]
    lses_r = [_by_residue(t, d) for t, d in zip(lses, DIL)]
    ffn2w = ["w_ffn2_in/1"]
    (h2, ypool, yattn, merged, mixout, dpool), got = _mix_merge(
        h1, vec2, p, os_r, lses_r, gates, wp_bd, ps, wb["w_pool_branch"], wb["w_attn_branch"], wb["w_out"],
        ws.gather_plan(ffn2w))
    ws.gathered(ffn2w, got)
    win3 = [wb["w_ffn2_in/0"], wb["w_ffn2_in/1"]] if "w_ffn2_in/0" in wb else [wb["w_ffn2_in"]]
    (dh3, u3, a3, b3, f3, lacc), _ = _ffn_fwd(h2, vec3, win3, wb["w_ffn2_out"], "ffn2_fwd", head=(tgt, _vec([gf])))
    loss = 0.5 * jnp.sum(lacc[0]) / D

    grads = {}

    def wgrad_cols(name, xx, yy, riders, extra=None):
        group = _PlanGroup([ws.scatter_plan(riders, grads) if riders else None, extra])
        plan = group if group.plans else None
        if ws.fused_scatter:
            blocks, got = _wgrad_scatter(xx, yy, "wg_" + name, min(2048, T // 2), comm=plan)
            ws.scattered([name], [blocks])
        else:
            grads[name], got = _wgrad(xx, yy, "wg_" + name, D, 512, 1024, comm=plan)
        parts = group.split(got)
        if len(parts) > (extra is not None):
            ws.scattered(riders, parts[0])
        return parts[-1] if extra is not None else None

    (dh2, dab3, s3, df3, acc3), _ = _ffn_bwd(dh3, h2, a3, b3, f3, vec3, win3, wb["w_ffn2_out"], "ffn2_bwd")
    grads["w_ffn2_out"], _ = _wgrad(s3, df3, "wg_ffn2_out", FF // 2, 512, min(4096, T // 2))
    wgrad_cols("w_ffn2_in", u3, dab3, ["w_ffn2_out"])
    (dgates, do0, do1, do2, e0, e1, e2, dd, acc2a, accps,
     grads["w_out"], grads["w_pool_branch"], grads["w_attn_branch"], gwp), _ = _mix_bwd_a(
        dh2, vec2, mixout, merged, gates, ypool, yattn, dpool, os_r, lses_r, wp_bd, ps,
        wb["w_pool_branch"], wb["w_attn_branch"], wb["w_out"], ones_bd)
    n = len(POOL_WINDOWS)
    c = PW // n
    grad_w_pool = jnp.stack([gwp[j * c:(j + 1) * c, j * c:(j + 1) * c] for j in range(n)], axis=0)
    small3 = ["w_out", "w_pool_branch", "w_attn_branch"]
    dqs, dks, dvs = [], [], []
    for gi, (do, e) in enumerate(((do0, e0), (do1, e1), (do2, e2))):
        plan = ws.scatter_plan(small3, grads) if gi == 0 else None
        (dq, dk, dv), got = _attn_bwd(qs[gi], ks[gi], vs[gi], _flat(do), lses[gi], _flat(e), nbs[gi], f"attn_bwd{gi}", plan)
        if gi == 0:
            ws.scattered(small3, got)
        dqs.append(_by_residue(dq, DIL[gi]))
        dks.append(_by_residue(dk, DIL[gi]))
        dvs.append(_by_residue(dv, DIL[gi]))
    dh1, dproj, acc2b = _mix_bwd_b(dh2, h1, vec2, dd, dqs, dks, dvs, dgates, cos, sin, wb["w_in"])
    wgrad_cols("w_in", u2, dproj, [])
    (dx, dab1, s1, df1, acc1), _ = _ffn_bwd(dh1, x, a1, b1, f1, vec1, [wb["w_ffn1_in"]], wb["w_ffn1_out"], "ffn1_bwd")
    grads["w_ffn1_out"], _ = _wgrad(s1, df1, "wg_ffn1_out", FF // 2, 512, min(4096, T // 2))
    dmod = jnp.concatenate([acc1[0], acc1[1], acc1[3], acc2b[0], acc2b[1], acc2a[3], acc3[0], acc3[1], acc3[3]])
    dgains = jnp.stack([acc1[2], acc2b[2], acc3[2], lacc[1]], axis=0)
    row = None if pack is None else _SmallGatherPlan(pack(loss, dmod, dgains, grad_w_pool, accps[0]))
    early = [n for n in BIG if n in ws.recv] if pack is not None else []
    ws.sums = {n: _sum4(ws.recv[n], "sum_" + n) for n in early}
    ws.other = {}
    tail = _PlanGroup([row, _SwapPlan([ws.sums[n] for n in early]) if early else None])
    got = wgrad_cols("w_ffn1_in", u1, dab1, ["w_ffn1_out"], tail if tail.plans else None)
    gathered = None
    if got is not None:
        parts = tail.split(got)
        gathered = parts[0][0]
        ws.other = dict(zip(early, parts[1])) if early else {}
    return loss, dx, dmod, dgains, grad_w_pool, accps[0], grads, gathered


SMALL = ("b_ada", "g_norm_ffn1", "g_norm_mix", "g_norm_ffn2", "g_final", "pool_scale", "w_pool")
WEIGHTS = ("w_ada", "b_ada", "g_norm_ffn1", "w_ffn1_in", "w_ffn1_out", "g_norm_mix", "w_in", "w_pool", "pool_scale",
           "w_pool_branch", "w_attn_branch", "w_out", "g_norm_ffn2", "w_ffn2_in", "w_ffn2_out", "g_final")


def _pack_small(t):
    return jnp.concatenate([t[n].reshape(-1) for n in SMALL]).reshape(1, -1)


def kernel(x, c, positions, w_ada, b_ada, g_norm_ffn1, w_ffn1_in, w_ffn1_out, g_norm_mix, w_in, w_pool, pool_scale, w_pool_branch, w_attn_branch, w_out, g_norm_ffn2, w_ffn2_in, w_ffn2_out, g_final, loss_target, m_w_ada, m_b_ada, m_g_norm_ffn1, m_w_ffn1_in, m_w_ffn1_out, m_g_norm_mix, m_w_in, m_w_pool, m_pool_scale, m_w_pool_branch, m_w_attn_branch, m_w_out, m_g_norm_ffn2, m_w_ffn2_in, m_w_ffn2_out, m_g_final, v_w_ada, v_b_ada, v_g_norm_ffn1, v_w_ffn1_in, v_w_ffn1_out, v_g_norm_mix, v_w_in, v_w_pool, v_pool_scale, v_w_pool_branch, v_w_attn_branch, v_w_out, v_g_norm_ffn2, v_w_ffn2_in, v_w_ffn2_out, v_g_final):
    w = dict(w_ada=w_ada, b_ada=b_ada, g_norm_ffn1=g_norm_ffn1, w_ffn1_in=w_ffn1_in, w_ffn1_out=w_ffn1_out,
             g_norm_mix=g_norm_mix, w_in=w_in, w_pool=w_pool, pool_scale=pool_scale, w_pool_branch=w_pool_branch,
             w_attn_branch=w_attn_branch, w_out=w_out, g_norm_ffn2=g_norm_ffn2, w_ffn2_in=w_ffn2_in,
             w_ffn2_out=w_ffn2_out, g_final=g_final)
    mom = dict(w_ada=m_w_ada, b_ada=m_b_ada, g_norm_ffn1=m_g_norm_ffn1, w_ffn1_in=m_w_ffn1_in, w_ffn1_out=m_w_ffn1_out,
               g_norm_mix=m_g_norm_mix, w_in=m_w_in, w_pool=m_w_pool, pool_scale=m_pool_scale,
               w_pool_branch=m_w_pool_branch, w_attn_branch=m_w_attn_branch, w_out=m_w_out, g_norm_ffn2=m_g_norm_ffn2,
               w_ffn2_in=m_w_ffn2_in, w_ffn2_out=m_w_ffn2_out, g_final=m_g_final)
    var = dict(w_ada=v_w_ada, b_ada=v_b_ada, g_norm_ffn1=v_g_norm_ffn1, w_ffn1_in=v_w_ffn1_in, w_ffn1_out=v_w_ffn1_out,
               g_norm_mix=v_g_norm_mix, w_in=v_w_in, w_pool=v_w_pool, pool_scale=v_pool_scale,
               w_pool_branch=v_w_pool_branch, w_attn_branch=v_w_attn_branch, w_out=v_w_out, g_norm_ffn2=v_g_norm_ffn2,
               w_ffn2_in=v_w_ffn2_in, w_ffn2_out=v_w_ffn2_out, g_final=v_g_final)
    ix, iy, ic = _place()
    chip = 2 * ix + iy
    me = 4 * ix + 2 * iy + ic
    nada = w_ada.shape[2]

    shards = {n: w[n][0].astype(BF16) for n in BIG}
    half = D // 2
    shards["w_ffn2_in/0"], shards["w_ffn2_in/1"] = shards["w_ffn2_in"][:half], shards["w_ffn2_in"][half:]
    ws = _Sharded(shards)
    c_all = _run_plan(_SmallGatherPlan(c), "gather_c")[0][:, 0, :]
    b_shard = lax.dynamic_slice_in_dim(b_ada, chip * nada, nada, axis=1)
    mod_cols = _ada_fwd(c_all, w_ada[0], b_shard)
    first = _PlanGroup([_SmallGatherPlan(mod_cols), ws.gather_plan(["w_ffn1_in"])])
    (mod_all,), ffn1 = first.split(_run_plan(first, "gather_first"))
    ws.gathered(["w_ffn1_in"], ffn1)
    mod = jnp.concatenate([lax.dynamic_index_in_dim(mod_all[4 * (kk >> 1) + 2 * (kk & 1)], me, axis=0, keepdims=False)
                           for kk in range(4)])

    def pack(loss, dmod, dgains, g_w_pool, g_pool_scale):
        small_g = dict(b_ada=dmod, g_norm_ffn1=dgains[0], g_norm_mix=dgains[1], g_norm_ffn2=dgains[2],
                       g_final=dgains[3], pool_scale=g_pool_scale, w_pool=g_w_pool)
        return jnp.concatenate([_pack_small(small_g), jnp.pad(loss.reshape(1, 1), ((0, 0), (0, 127)))], axis=1)

    _, dx, _, _, _, _, _, gathered = _example_step(
        x[0], loss_target[0], positions[0], mod, (g_norm_ffn1[0], g_norm_mix[0], g_norm_ffn2[0], g_final),
        w_pool[0], pool_scale[0], ws, pack)

    per_weight, loss_tile = _adam_small(*[[t[n].reshape(1, -1) for n in SMALL] for t in (w, mom, var)], gathered)
    small_out = [{n: per_weight[j][kind].reshape(w[n].shape) for j, n in enumerate(SMALL)} for kind in range(4)]
    loss = loss_tile[0, 0]

    dmod_all = gathered[:, 0, :NMOD * D]
    dmod_cols = lax.dynamic_slice_in_dim(dmod_all, chip * nada, nada, axis=1)
    g_ada = _ada_bwd(c_all, dmod_cols)

    ada_out = _adam(w_ada[0], m_w_ada[0], v_w_ada[0], [g_ada], "adam_w_ada")[0]

    sums, other = dict(ws.sums), dict(ws.other)
    late = [n for n in BIG if n not in sums]
    sums.update({n: _sum4(ws.recv[n], "sum_" + n) for n in late})
    other.update(zip(late, _run_plan(_SwapPlan([sums[n] for n in late]), "swap_sibling")))
    big_out = {}
    for n in BIG:
        if sums[n].shape[0] < w[n].shape[1]:
            big_out[n] = _adam_halves(w[n][0], mom[n][0], var[n][0], sums[n], other[n], "adam_" + n)
        else:
            big_out[n] = _adam(w[n][0], mom[n][0], var[n][0], [sums[n], other[n]], "adam_" + n)[0]

    def leaf(kind, n):
        if n == "w_ada":
            return ada_out[kind][None]
        if n in big_out:
            return big_out[n][kind][None]
        return small_out[kind][n]

    return (loss, dx[None], *[leaf(kind, n) for kind in range(4) for n in WEIGHTS])
```

```python
import jax
import jax.numpy as jnp
from jax import lax
from jax.experimental import pallas as pl
from jax.experimental.pallas import tpu as pltpu

F32 = jnp.float32
BF16 = jnp.bfloat16

D = 1024
FF = 2816
FC = FF
PW = 256
GA = 256
HD = 64
LANES = 128
NH = GA // HD
NG = 3
DIL = (1, 4, 16)
BLK = 128
FWD_BLOCKS = 16
BWD_BLOCKS = 16
GW = 2 * D
INW = PW + 3 * NG * GA + GW
NMOD = 9
POOL_WINDOWS = (2, 4, 8, 16)
HALO = 16
EPS = 1e-6
SCALE = HD ** -0.5
NEG = -1e30

LR, B1, B2, AEPS, WD, STEP = 0.001, 0.9, 0.999, 1e-08, 0.01, 10

VMEM_BIG = 56 * 1024 * 1024
TM = 256

MESH = pl.DeviceIdType.MESH
ANY = pl.BlockSpec(memory_space=pl.ANY)


def _call(body, name, grid, in_specs, out_specs, out_shape, scratch=(), vmem=None, comm=None):
    params = pltpu.CompilerParams(dimension_semantics=("arbitrary",) * len(grid), vmem_limit_bytes=vmem)
    n_in, n_out, n_scr = len(in_specs), len(out_shape), len(scratch)
    if comm is None:
        call = pl.pallas_call(body, name=name, grid=grid, in_specs=list(in_specs), out_specs=list(out_specs),
                              out_shape=list(out_shape), scratch_shapes=list(scratch), compiler_params=params)
        return lambda *args: (call(*args), ())
    nc = len(comm.inputs)

    def body_with_comm(*refs):
        ins, refs = refs[:n_in], refs[n_in:]
        c_ins, refs = refs[:nc], refs[nc:]
        outs, refs = refs[:n_out], refs[n_out:]
        c_outs, refs = refs[:nc], refs[nc:]
        scr, sems = refs[:n_scr], refs[n_scr:]
        first = pl.program_id(0) == 0
        last = pl.program_id(0) == grid[0] - 1
        for ax in range(1, len(grid)):
            first = jnp.logical_and(first, pl.program_id(ax) == 0)
            last = jnp.logical_and(last, pl.program_id(ax) == grid[ax] - 1)

        @pl.when(first)
        def _():
            comm.start(c_ins, c_outs, sems)

        body(*ins, *outs, *scr)
        early_relay = len(grid) == 1 and grid[0] >= 4
        if early_relay:
            @pl.when(pl.program_id(0) == (3 * grid[0]) // 4)
            def _():
                comm.relay(c_ins, c_outs, sems)

        @pl.when(last)
        def _():
            if not early_relay:
                comm.relay(c_ins, c_outs, sems)
            comm.wait(c_ins, c_outs, sems)

    call = pl.pallas_call(
        body_with_comm, name=name, grid=grid, in_specs=list(in_specs) + [ANY] * nc,
        out_specs=list(out_specs) + [ANY] * nc, out_shape=list(out_shape) + list(comm.out_shapes),
        scratch_shapes=list(scratch) + list(comm.sem_shapes), compiler_params=params)

    def run(*args):
        res = call(*args, *comm.inputs)
        return res[:n_out], res[n_out:]

    return run


def _rows(tm, n):
    return pl.BlockSpec((tm, n), lambda i: (i, 0))


def _const(shape):
    return pl.BlockSpec(shape, lambda i: (0,) * len(shape))


def _sds(shape, dtype):
    return jax.ShapeDtypeStruct(shape, dtype)


def _dot(a, b):
    return jnp.dot(a, b, preferred_element_type=F32)


def _dot_nt(a, b):
    return lax.dot_general(a, b, (((1,), (1,)), ((), ())), preferred_element_type=F32)


def _dot_tn(a, b):
    return lax.dot_general(a, b, (((0,), (0,)), ((), ())), preferred_element_type=F32)


def _colsum(v):
    return jnp.sum(v, axis=0, keepdims=True)


def _norm_fwd(h, g, sh, sc):
    r = lax.rsqrt(jnp.mean(h * h, axis=-1, keepdims=True) + EPS)
    xh = h * r
    n = xh * g
    return xh, r, n, n * (1.0 + sc) + sh


def _norm_bwd(du, xh, r, n, g, sc):
    dn = du * (1.0 + sc)
    dxh = dn * g
    dh = r * (dxh - xh * jnp.mean(dxh * xh, axis=-1, keepdims=True))
    return dh, _colsum(du), _colsum(du * n), _colsum(dn * xh)


def _load_once(pairs, sems):
    @pl.when(pl.program_id(0) == 0)
    def _():
        cps = [pltpu.make_async_copy(src, dst, sems.at[j]) for j, (src, dst) in enumerate(pairs)]
        for cp in cps:
            cp.start()
        for cp in cps:
            cp.wait()


def _zero_first(ref):
    @pl.when(pl.program_id(0) == 0)
    def _():
        ref[...] = jnp.zeros(ref.shape, ref.dtype)


def _row_chunks(hbm_refs, vmem_ref):
    pairs, row = [], 0
    for ref in hbm_refs:
        pairs.append((ref, vmem_ref.at[pl.ds(row, ref.shape[0]), :]))
        row += ref.shape[0]
    return pairs


def _loss_head(hh, tgt, g):
    r = lax.rsqrt(jnp.mean(hh * hh, axis=-1, keepdims=True) + EPS)
    xh = hh * r
    err = xh * g - tgt
    dy = err * (1.0 / D)
    dxh = dy * g
    dh = r * (dxh - xh * jnp.mean(dxh * xh, axis=-1, keepdims=True))
    return dh, _colsum(err * err), _colsum(dy * xh)


def _ffn_fwd(h, vec, wins, wout, name, comm=None, head=None):
    T = h.shape[0]
    nwin = len(wins)
    nhead = 0 if head is None else 2

    def body(h_ref, vec_ref, *rest):
        head_refs, rest = rest[:nhead], rest[nhead:]
        win_hbms, rest = rest[:nwin], rest[nwin:]
        (wout_hbm, ho_ref, u_ref, a_ref, b_ref, f_ref), rest = rest[:6], rest[6:]
        lacc_refs, (win_v, wout_v, sems) = rest[:nhead // 2], rest[nhead // 2:]
        _load_once(_row_chunks(win_hbms, win_v) + [(wout_hbm, wout_v)], sems)
        hh = h_ref[...]
        g, sh, sc, gt = vec_ref[0:1, :], vec_ref[1:2, :], vec_ref[2:3, :], vec_ref[3:4, :]
        _, _, _, u = _norm_fwd(hh, g, sh, sc)
        ub = u.astype(BF16)
        u_ref[...] = ub
        acc = None
        for j in range(FF // FC):
            lo, hi = j * FC, (j + 1) * FC
            a = _dot(ub, win_v[:, lo:hi])
            b = _dot(ub, win_v[:, FF + lo:FF + hi])
            a_ref[:, lo:hi] = a.astype(BF16)
            b_ref[:, lo:hi] = b.astype(BF16)
            s = (a * jax.nn.sigmoid(a) * b).astype(BF16)
            part = _dot(s, wout_v[lo:hi, :])
            acc = part if acc is None else acc + part
        f_ref[...] = acc.astype(BF16)
        ho = hh + 0.5 * gt * acc
        if head is None:
            ho_ref[...] = ho
        else:
            _zero_first(lacc_refs[0])
            dh, sq, dg = _loss_head(ho, head_refs[0][...], head_refs[1][0:1, :])
            ho_ref[...] = dh
            lacc_refs[0][0:1, :] += sq
            lacc_refs[0][1:2, :] += dg

    head_specs = [] if head is None else [_rows(TM, D), _const((8, D))]
    lacc_spec = [] if head is None else [_const((8, D))]
    lacc_shape = [] if head is None else [_sds((8, D), F32)]
    return _call(
        body, name, (T // TM,),
        [_rows(TM, D), _const((8, D))] + head_specs + [ANY] * (nwin + 1),
        [_rows(TM, D), _rows(TM, D), _rows(TM, FF), _rows(TM, FF), _rows(TM, D)] + lacc_spec,
        [_sds((T, D), F32), _sds((T, D), BF16), _sds((T, FF), BF16), _sds((T, FF), BF16), _sds((T, D), BF16)] + lacc_shape,
        scratch=[pltpu.VMEM((D, 2 * FF), BF16), pltpu.VMEM((FF, D), BF16), pltpu.SemaphoreType.DMA((nwin + 1,))],
        vmem=VMEM_BIG, comm=comm,
    )(h, vec, *([] if head is None else head), *wins, wout)


def _ffn_ab(h, vec, wins, name, comm=None):
    T = h.shape[0]
    nwin = len(wins)

    def body(h_ref, vec_ref, *rest):
        win_hbms, (u_ref, a_ref, b_ref, win_v, sems) = rest[:nwin], rest[nwin:]
        _load_once(_row_chunks(win_hbms, win_v), sems)
        g, sh, sc = vec_ref[0:1, :], vec_ref[1:2, :], vec_ref[2:3, :]
        _, _, _, u = _norm_fwd(h_ref[...], g, sh, sc)
        ub = u.astype(BF16)
        u_ref[...] = ub
        for j in range(FF // FC):
            lo, hi = j * FC, (j + 1) * FC
            a_ref[:, lo:hi] = _dot(ub, win_v[:, lo:hi]).astype(BF16)
            b_ref[:, lo:hi] = _dot(ub, win_v[:, FF + lo:FF + hi]).astype(BF16)

    return _call(
        body, name, (T // TM,),
        [_rows(TM, D), _const((8, D))] + [ANY] * nwin,
        [_rows(TM, D), _rows(TM, FF), _rows(TM, FF)],
        [_sds((T, D), BF16), _sds((T, FF), BF16), _sds((T, FF), BF16)],
        scratch=[pltpu.VMEM((D, 2 * FF), BF16), pltpu.SemaphoreType.DMA((nwin,))],
        vmem=VMEM_BIG, comm=comm,
    )(h, vec, *wins)


def _ffn_out(h, a, b, vec, wout, name, comm=None):
    T = h.shape[0]

    def body(h_ref, a_ref, b_ref, vec_ref, wout_hbm, ho_ref, f_ref, wout_v, sems):
        _load_once([(wout_hbm, wout_v)], sems)
        gt = vec_ref[3:4, :]
        acc = None
        for j in range(FF // FC):
            lo, hi = j * FC, (j + 1) * FC
            av = a_ref[:, lo:hi].astype(F32)
            s = (av * jax.nn.sigmoid(av) * b_ref[:, lo:hi].astype(F32)).astype(BF16)
            part = _dot(s, wout_v[lo:hi, :])
            acc = part if acc is None else acc + part
        f_ref[...] = acc.astype(BF16)
        ho_ref[...] = h_ref[...] + 0.5 * gt * acc

    return _call(
        body, name, (T // TM,),
        [_rows(TM, D), _rows(TM, FF), _rows(TM, FF), _const((8, D)), ANY],
        [_rows(TM, D), _rows(TM, D)],
        [_sds((T, D), F32), _sds((T, D), BF16)],
        scratch=[pltpu.VMEM((FF, D), BF16), pltpu.SemaphoreType.DMA((1,))],
        vmem=VMEM_BIG, comm=comm,
    )(h, a, b, vec, wout)


def _ffn_bwd(dh, h, a, b, f, vec, wins, wout, name, comm=None):
    T = h.shape[0]
    nwin = len(wins)

    def body(dh_ref, h_ref, a_ref, b_ref, f_ref, vec_ref, *rest):
        win_hbms, (wout_hbm, dhi_ref, dab_ref, s_ref, df_ref, acc_ref, win_v, wout_v, sems) = rest[:nwin], rest[nwin:]
        _load_once(_row_chunks(win_hbms, win_v) + [(wout_hbm, wout_v)], sems)
        _zero_first(acc_ref)
        g, sh, sc, gt = vec_ref[0:1, :], vec_ref[1:2, :], vec_ref[2:3, :], vec_ref[3:4, :]
        dho = dh_ref[...]
        df = (0.5 * gt * dho).astype(BF16)
        df_ref[...] = df
        dgt = _colsum(0.5 * dho * f_ref[...].astype(F32))
        du = None
        for j in range(FF // FC):
            lo, hi = j * FC, (j + 1) * FC
            av = a_ref[:, lo:hi].astype(F32)
            bv = b_ref[:, lo:hi].astype(F32)
            ds = _dot_nt(df, wout_v[lo:hi, :])
            sig = jax.nn.sigmoid(av)
            sa = av * sig
            s_ref[:, lo:hi] = (sa * bv).astype(BF16)
            da = (ds * bv * (sig * (1.0 + av * (1.0 - sig)))).astype(BF16)
            db = (ds * sa).astype(BF16)
            dab_ref[:, lo:hi] = da
            dab_ref[:, FF + lo:FF + hi] = db
            part = _dot_nt(da, win_v[:, lo:hi]) + _dot_nt(db, win_v[:, FF + lo:FF + hi])
            du = part if du is None else du + part
        xh, r, n, _ = _norm_fwd(h_ref[...], g, sh, sc)
        dhn, dsh, dsc, dg = _norm_bwd(du, xh, r, n, g, sc)
        dhi_ref[...] = dho + dhn
        acc_ref[0:1, :] += dsh
        acc_ref[1:2, :] += dsc
        acc_ref[2:3, :] += dg
        acc_ref[3:4, :] += dgt

    return _call(
        body, name, (T // TM,),
        [_rows(TM, D), _rows(TM, D), _rows(TM, FF), _rows(TM, FF), _rows(TM, D), _const((8, D))] + [ANY] * (nwin + 1),
        [_rows(TM, D), _rows(TM, 2 * FF), _rows(TM, FF), _rows(TM, D), _const((8, D))],
        [_sds((T, D), F32), _sds((T, 2 * FF), BF16), _sds((T, FF), BF16), _sds((T, D), BF16), _sds((8, D), F32)],
        scratch=[pltpu.VMEM((D, 2 * FF), BF16), pltpu.VMEM((FF, D), BF16), pltpu.SemaphoreType.DMA((nwin + 1,))],
        vmem=VMEM_BIG, comm=comm,
    )(dh, h, a, b, f, vec, *wins, wout)


def _wgrad(x, y, name, tk, tn, tt, out_dtype=BF16, comm=None):
    T, K = x.shape
    N = y.shape[1]
    nt = T // tt

    def body(x_ref, y_ref, o_ref, acc_ref):
        t = pl.program_id(2)
        part = _dot_tn(x_ref[...], y_ref[...])

        @pl.when(t == 0)
        def _():
            acc_ref[...] = part

        @pl.when(t > 0)
        def _():
            acc_ref[...] += part

        @pl.when(t == nt - 1)
        def _():
            o_ref[...] = acc_ref[...].astype(out_dtype)

    (out,), c_outs = _call(
        body, name, (K // tk, N // tn, nt),
        [pl.BlockSpec((tt, tk), lambda i, j, t: (t, i)), pl.BlockSpec((tt, tn), lambda i, j, t: (t, j))],
        [pl.BlockSpec((tk, tn), lambda i, j, t: (i, j))], [_sds((K, N), out_dtype)],
        scratch=[pltpu.VMEM((tk, tn), F32)], vmem=VMEM_BIG, comm=comm,
    )(x, y)
    return out, c_outs


def _wgrad_scatter(x, y, name, tt, comm=None):
    T, K = x.shape
    n = y.shape[1] // 4
    nt = T // tt
    assert nt >= 2, "a block's hand-over is added one grid step into the next block"
    half = K // 2
    nc = 0 if comm is None else len(comm.inputs)

    def body(chip_ref, x_ref, y_ref, *refs):
        c_ins, refs = refs[:nc], refs[nc:]
        recv_ref, refs = refs[0], refs[1:]
        c_outs, refs = refs[:nc], refs[nc:]
        acc_ref, keep_ref, give_ref, take_ref, local_sem, give_sems, take_sems, send_sems, recv_sems = refs[:9]
        j, t = pl.program_id(0), pl.program_id(1)
        px, py, pc = _place()

        def hand_over(jj):
            return pltpu.make_async_remote_copy(
                src_ref=give_ref.at[jj], dst_ref=take_ref.at[jj], send_sem=give_sems.at[jj], recv_sem=take_sems.at[jj],
                device_id=(px, py, 1 - pc), device_id_type=MESH)

        def send(jj):
            m = (3, 1, 2)[jj]
            return pltpu.make_async_remote_copy(
                src_ref=keep_ref.at[jj], dst_ref=recv_ref.at[m], send_sem=send_sems.at[jj], recv_sem=recv_sems.at[jj],
                device_id=_chip_peer(px, py, pc, m), device_id_type=MESH)

        def add_sibling(jj):
            hand_over(jj).wait_recv()
            keep_ref[jj] = (keep_ref[jj].astype(F32) + take_ref[jj].astype(F32)).astype(BF16)

        if comm is not None:
            @pl.when(jnp.logical_and(j == 0, t == 0))
            def _():
                comm.start(c_ins, c_outs, refs[9:])

        part = _dot_tn(x_ref[...], y_ref[...])

        @pl.when(t == 0)
        def _():
            acc_ref[...] = part

        @pl.when(t > 0)
        def _():
            acc_ref[...] += part

        for jj in range(3):
            @pl.when(jnp.logical_and(j == jj + 1, t == 0))
            def _():
                add_sibling(jj)
                send(jj).start()

        for jj in range(4):
            @pl.when(jnp.logical_and(j == jj, t == nt - 1))
            def _():
                keep_ref[jj] = acc_ref[pl.ds(pl.multiple_of(pc * half, 16), half), :].astype(BF16)
                give_ref[jj] = acc_ref[pl.ds(pl.multiple_of((1 - pc) * half, 16), half), :].astype(BF16)
                hand_over(jj).start()

        @pl.when(jnp.logical_and(j == 3, t == nt - 1))
        def _():
            add_sibling(3)
            own = pltpu.make_async_copy(keep_ref.at[3], recv_ref.at[0], local_sem.at[0])
            own.start()
            for jj in range(3):
                send(jj).wait_recv()
            for jj in range(3):
                send(jj).wait_send()
            for jj in range(4):
                hand_over(jj).wait_send()
            own.wait()
            if comm is not None:
                comm.relay(c_ins, c_outs, refs[9:])
                comm.wait(c_ins, c_outs, refs[9:])

    grid_spec = pltpu.PrefetchScalarGridSpec(
        num_scalar_prefetch=1, grid=(4, nt),
        in_specs=[pl.BlockSpec((tt, K), lambda j, t, chip: (t, 0)),
                  pl.BlockSpec((tt, n), lambda j, t, chip: (t, chip[0] ^ jnp.where(j == 0, 3, jnp.where(j == 3, 0, j))))]
        + [ANY] * nc,
        out_specs=[ANY] * (1 + nc),
        scratch_shapes=[pltpu.VMEM((K, n), F32)] + [pltpu.VMEM((4, half, n), BF16)] * 3
        + [pltpu.SemaphoreType.DMA((1,))] + [pltpu.SemaphoreType.DMA((4,))] * 2 + [pltpu.SemaphoreType.DMA((3,))] * 2
        + ([] if comm is None else list(comm.sem_shapes)))
    px, py, _ = _place()
    res = pl.pallas_call(
        body, name=name, grid_spec=grid_spec,
        out_shape=[_sds((4, half, n), BF16)] + ([] if comm is None else list(comm.out_shapes)),
        compiler_params=pltpu.CompilerParams(dimension_semantics=("arbitrary", "arbitrary"), vmem_limit_bytes=VMEM_BIG),
    )((2 * px + py).astype(jnp.int32).reshape(1), x, y, *([] if comm is None else comm.inputs))
    return res[0], res[1:]


def _swap_halves(t):
    w = t.shape[1]
    lane = lax.broadcasted_iota(jnp.int32, t.shape, 1)
    return jnp.where(lane % HD < HD // 2, pltpu.roll(t, w - HD // 2, 1), pltpu.roll(t, HD // 2, 1))


def _rope(t, cos, sin_signed):
    c = jnp.tile(cos, (1, t.shape[1] // cos.shape[1]))
    s = jnp.tile(sin_signed, (1, t.shape[1] // sin_signed.shape[1]))
    return t * c + _swap_halves(t) * s


def _rope_bwd(dt, cos, sin_signed):
    c = jnp.tile(cos, (1, dt.shape[1] // cos.shape[1]))
    s = jnp.tile(sin_signed, (1, dt.shape[1] // sin_signed.shape[1]))
    return dt * c + _swap_halves(dt * s)


def _rm_spec(dil):
    return pl.BlockSpec((dil, TM // dil, GA), lambda i: (0, i, 0))


def _to_residues(t, dst_ref, scr_ref, dil):
    if dil == 1:
        dst_ref[0] = t.astype(dst_ref.dtype)
        return
    for j in range(GA // LANES):
        scr_ref[j] = t[:, j * LANES:(j + 1) * LANES]
    for r in range(dil):
        for j in range(GA // LANES):
            rows = scr_ref.at[j][pl.ds(r, TM // dil, stride=dil), :]
            dst_ref[r, :, j * LANES:(j + 1) * LANES] = rows.astype(dst_ref.dtype)


def _from_residues(src_ref, scr_ref, dil):
    if dil == 1:
        return src_ref[0].astype(F32)
    for r in range(dil):
        for j in range(GA // LANES):
            scr_ref.at[j][pl.ds(r, TM // dil, stride=dil), :] = src_ref[r, :, j * LANES:(j + 1) * LANES].astype(F32)
    return jnp.concatenate([scr_ref[j] for j in range(GA // LANES)], axis=1)


def _mix_proj(h, vec, win, cos, sin, comm=None):
    T = h.shape[0]

    def body(h_ref, vec_ref, win_hbm, cos_ref, sin_ref, u_ref, p_ref, *rest):
        qkv_refs, gates_ref, win_v, scr_ref, sems = rest[:3 * NG], rest[3 * NG], rest[3 * NG + 1], rest[3 * NG + 2], rest[3 * NG + 3]
        _load_once([(win_hbm, win_v)], sems)
        g, sh, sc = vec_ref[0:1, :], vec_ref[1:2, :], vec_ref[2:3, :]
        _, _, _, u = _norm_fwd(h_ref[...], g, sh, sc)
        ub = u.astype(BF16)
        u_ref[...] = ub
        mixer_cols = PW + 3 * NG * GA
        proj = _dot(ub, win_v[:, 0:mixer_cols])
        p_ref[...] = proj[:, 0:PW]
        cos_t, sin_t = cos_ref[...], sin_ref[...]
        for j in range(3 * NG):
            col = PW + j * GA
            t = proj[:, col:col + GA]
            if j < 2 * NG:
                t = _rope(t, cos_t, sin_t)
            _to_residues(t, qkv_refs[j], scr_ref, DIL[j % NG])
        gates_ref[...] = jax.nn.sigmoid(_dot(ub, win_v[:, mixer_cols:INW])).astype(BF16)

    outs, c_outs = _call(
        body, "mix_proj", (T // TM,),
        [_rows(TM, D), _const((8, D)), ANY, _rows(TM, 128), _rows(TM, 128)],
        [_rows(TM, D), _rows(TM, PW)] + [_rm_spec(d) for d in DIL] * 3 + [_rows(TM, GW)],
        [_sds((T, D), BF16), _sds((T, PW), F32)] + [_sds((d, T // d, GA), BF16) for d in DIL] * 3 + [_sds((T, GW), BF16)],
        scratch=[pltpu.VMEM((D, INW), BF16), pltpu.VMEM((GA // LANES, TM, LANES), F32), pltpu.SemaphoreType.DMA((1,))],
        vmem=VMEM_BIG, comm=comm,
    )(h, vec, win, cos, sin)
    return (outs[0], outs[1], outs[2:2 + NG], outs[2 + NG:2 + 2 * NG], outs[2 + 2 * NG:2 + 3 * NG], outs[2 + 3 * NG]), c_outs


def _head_masks():
    lane_head = lax.broadcasted_iota(jnp.int32, (BLK, GA), 1) // HD
    return [lane_head == hd for hd in range(NH)]


def _expand_heads(t, hm):
    return jnp.concatenate([jnp.where(m, t, jnp.zeros_like(t)) for m in hm], axis=0)


def _collapse_heads(tb, hm):
    out = None
    for hd, m in enumerate(hm):
        part = jnp.where(m, tb[hd * BLK:(hd + 1) * BLK, :], 0.0)
        out = part if out is None else out + part
    return out


def _head_rows(t):
    return jnp.concatenate([t[:, hd * HD:hd * HD + 1] for hd in range(NH)], axis=0)


def _band(has_prev):
    a = lax.broadcasted_iota(jnp.int32, (NH * BLK, 2 * BLK), 0) & (BLK - 1)
    c = lax.broadcasted_iota(jnp.int32, (NH * BLK, 2 * BLK), 1)
    return jnp.logical_and(c >= jnp.where(has_prev, a, BLK), c <= a + BLK)


def _attn_fwd(q, k, v, nb, name, comm=None):
    T = q.shape[0]
    nbt = T // BLK

    def block(qv, kcat, vcat, has_prev, hm):
        s = jnp.where(_band(has_prev), _dot_nt(_expand_heads(qv, hm), kcat) * SCALE, NEG)
        mx = jnp.max(s, axis=-1, keepdims=True)
        e = jnp.exp(s - mx)
        l = jnp.sum(e, axis=-1, keepdims=True)
        ob = _dot((e * (1.0 / l)).astype(BF16), vcat)
        return _collapse_heads(ob, hm), _collapse_heads(jnp.broadcast_to(mx + jnp.log(l), (NH * BLK, GA)), hm)

    def body(q_ref, k_ref, kp_ref, v_ref, vp_ref, o_ref, lse_ref):
        b0 = FWD_BLOCKS * pl.program_id(0)
        hm = _head_masks()
        for b in range(FWD_BLOCKS):
            rows = slice(b * BLK, (b + 1) * BLK)
            if b == 0:
                kcat = jnp.concatenate([kp_ref[...], k_ref[rows, :]], axis=0)
                vcat = jnp.concatenate([vp_ref[...], v_ref[rows, :]], axis=0)
            else:
                kcat, vcat = k_ref[(b - 1) * BLK:(b + 1) * BLK, :], v_ref[(b - 1) * BLK:(b + 1) * BLK, :]
            o_ref[rows, :], lse_ref[rows, :] = block(q_ref[rows, :], kcat, vcat, ((b0 + b) & (nb - 1)) != 0, hm)

    cur = pl.BlockSpec((FWD_BLOCKS * BLK, GA), lambda i: (i, 0))
    prev = pl.BlockSpec((BLK, GA), lambda i: (jnp.maximum(FWD_BLOCKS * i - 1, 0), 0))
    return _call(body, name, (nbt // FWD_BLOCKS,), [cur, cur, prev, cur, prev], [cur, cur],
                 [_sds((T, GA), F32), _sds((T, GA), F32)], comm=comm)(q, k, k, v, v)


def _attn_bwd(q, k, v, do, lse, e, nb, name, comm=None):
    T = q.shape[0]
    nbt = T // BLK

    nblk = BWD_BLOCKS

    def probs_and_ds(qb, dob, kcat, vcat, lsev, ev, valid):
        p = jnp.where(valid, jnp.exp(_dot_nt(qb, kcat) * SCALE - _head_rows(lsev)), 0.0)
        return p.astype(BF16), (p * (_dot_nt(dob, vcat) + _head_rows(ev))).astype(BF16)

    def body(q_ref, k_ref, v_ref, do_ref, lse_ref, e_ref, kp_ref, vp_ref, qn_ref, don_ref, lsen_ref, en_ref,
             dq_ref, dk_ref, dv_ref):
        b0 = nblk * pl.program_id(0)
        hm = _head_masks()
        rows = [slice(b * BLK, (b + 1) * BLK) for b in range(nblk)]
        qs = [_expand_heads(q_ref[r, :], hm) for r in rows] + [_expand_heads(qn_ref[...], hm)]
        dos = [_expand_heads(do_ref[r, :], hm) for r in rows] + [_expand_heads(don_ref[...], hm)]
        ps, dss = [], []
        for b, r in enumerate(rows):
            if b == 0:
                kcat = jnp.concatenate([kp_ref[...], k_ref[r, :]], axis=0)
                vcat = jnp.concatenate([vp_ref[...], v_ref[r, :]], axis=0)
            else:
                kcat, vcat = k_ref[(b - 1) * BLK:(b + 1) * BLK, :], v_ref[(b - 1) * BLK:(b + 1) * BLK, :]
            p, ds = probs_and_ds(qs[b], dos[b], kcat, vcat, lse_ref[r, :], e_ref[r, :], _band(((b0 + b) & (nb - 1)) != 0))
            dq_ref[r, :] = _collapse_heads(_dot(ds, kcat) * SCALE, hm)
            ps.append(p)
            dss.append(ds)
        a = lax.broadcasted_iota(jnp.int32, (NH * BLK, BLK), 0) & (BLK - 1)
        c = lax.broadcasted_iota(jnp.int32, (NH * BLK, BLK), 1)
        valid_n = jnp.logical_and(c >= a, ((b0 + nblk) & (nb - 1)) != 0)
        p_n, ds_n = probs_and_ds(qs[nblk], dos[nblk], k_ref[rows[-1], :], v_ref[rows[-1], :], lsen_ref[...], en_ref[...], valid_n)
        for b, r in enumerate(rows):
            ds_after = dss[b + 1][:, :BLK] if b + 1 < nblk else ds_n
            p_after = ps[b + 1][:, :BLK] if b + 1 < nblk else p_n
            q_pair = jnp.concatenate([qs[b], qs[b + 1]], axis=0)
            do_pair = jnp.concatenate([dos[b], dos[b + 1]], axis=0)
            dk_ref[r, :] = _dot_tn(jnp.concatenate([dss[b][:, BLK:], ds_after], axis=0), q_pair) * SCALE
            dv_ref[r, :] = _dot_tn(jnp.concatenate([ps[b][:, BLK:], p_after], axis=0), do_pair).astype(BF16)

    cur = pl.BlockSpec((nblk * BLK, GA), lambda i: (i, 0))
    prev = pl.BlockSpec((BLK, GA), lambda i: (jnp.maximum(nblk * i - 1, 0), 0))
    nxt = pl.BlockSpec((BLK, GA), lambda i: (jnp.minimum(nblk * i + nblk, nbt - 1), 0))
    return _call(body, name, (nbt // nblk,), [cur] * 6 + [prev, prev] + [nxt] * 4, [cur, cur, cur],
                 [_sds((T, GA), F32), _sds((T, GA), F32), _sds((T, GA), BF16)],
                 comm=comm)(q, k, v, do, lse, e, k, v, q, do, lse, e)


def _flat(t):
    return t.reshape(t.shape[0] * t.shape[1], t.shape[2])


def _by_residue(t, dil):
    return t.reshape(dil, t.shape[0] // dil, t.shape[1])


def _pool_consts(shape, row0):
    lane = lax.broadcasted_iota(jnp.int32, shape, 1)
    t = lax.broadcasted_iota(jnp.int32, shape, 0) + row0
    grp = lane // (PW // len(POOL_WINDOWS))
    win = jnp.where(grp == 0, POOL_WINDOWS[0], jnp.where(grp == 1, POOL_WINDOWS[1],
                    jnp.where(grp == 2, POOL_WINDOWS[2], POOL_WINDOWS[3])))
    cnt = jnp.minimum(t + 1, win).astype(F32)
    return grp, cnt


def _window_sums(ext_ref, base, step, tm):
    outs, run = [], None
    for j in range(POOL_WINDOWS[-1]):
        sl = ext_ref[pl.ds(base + step * j, tm), :]
        run = sl if run is None else run + sl
        if j + 1 in POOL_WINDOWS:
            outs.append(run)
    return outs


def _select_group(grp, vals):
    return jnp.where(grp == 0, vals[0], jnp.where(grp == 1, vals[1], jnp.where(grp == 2, vals[2], vals[3])))


def _pool_d(pc_ref, pp_ref, ext_ref, i, tm):
    ext_ref[0:HALO, :] = jnp.where(i > 0, pp_ref[tm - HALO:tm, :], 0.0)
    ext_ref[HALO:HALO + tm, :] = pc_ref[...]
    grp, cnt = _pool_consts((tm, PW), i * tm)
    sums = _window_sums(ext_ref, HALO, -1, tm)
    return _select_group(grp, sums) / cnt - pc_ref[...]


def _group_weights(ls):
    mx = jnp.maximum(jnp.maximum(ls[0], ls[1]), ls[2])
    es = [jnp.exp(l - mx) for l in ls]
    inv = 1.0 / (es[0] + es[1] + es[2])
    return [e * inv for e in es]


def _mix_merge(h, vec, p, os, lses, gates, wp_bd, pscale, wpb, wab, wout, comm=None):
    T = h.shape[0]

    def body(h_ref, vec_ref, pc_ref, pp_ref, o0, o1, o2, l0, l1, l2, gates_ref, wp_ref, ps_ref, wpb_ref, wab_ref, wout_ref,
             ho_ref, yp_ref, ya_ref, mg_ref, mo_ref, d_ref, ext_ref, scr_ref):
        i = pl.program_id(0)
        gt = vec_ref[3:4, :]
        d = _pool_d(pc_ref, pp_ref, ext_ref, i, TM).astype(BF16)
        d_ref[...] = d
        ypool = (_dot(d, wp_ref[...]) * ps_ref[0:1, :]).astype(BF16)
        yp_ref[...] = ypool
        w = _group_weights([_from_residues(r, scr_ref, dl) for r, dl in zip((l0, l1, l2), DIL)])
        yattn = None
        for wg, o_ref, dl in zip(w, (o0, o1, o2), DIL):
            part = wg * _from_residues(o_ref, scr_ref, dl)
            yattn = part if yattn is None else yattn + part
        yattn = yattn.astype(BF16)
        ya_ref[...] = yattn
        merged = (gates_ref[:, 0:D].astype(F32) * _dot(ypool, wpb_ref[...])
                  + gates_ref[:, D:GW].astype(F32) * _dot(yattn, wab_ref[...])).astype(BF16)
        mg_ref[...] = merged
        mo = _dot(merged, wout_ref[...])
        mo_ref[...] = mo.astype(BF16)
        ho_ref[...] = h_ref[...] + gt * mo

    prev = pl.BlockSpec((TM, PW), lambda i: (jnp.maximum(i - 1, 0), 0))
    return _call(
        body, "mix_merge", (T // TM,),
        [_rows(TM, D), _const((8, D)), _rows(TM, PW), prev] + [_rm_spec(dl) for dl in DIL] * 2 + [_rows(TM, GW), _const((PW, PW)),
         _const((8, PW)), _const((PW, D)), _const((GA, D)), _const((D, D))],
        [_rows(TM, D), _rows(TM, PW), _rows(TM, GA), _rows(TM, D), _rows(TM, D), _rows(TM, PW)],
        [_sds((T, D), F32), _sds((T, PW), BF16), _sds((T, GA), BF16), _sds((T, D), BF16), _sds((T, D), BF16), _sds((T, PW), BF16)],
        scratch=[pltpu.VMEM((TM + HALO, PW), F32), pltpu.VMEM((GA // LANES, TM, LANES), F32)],
        vmem=VMEM_BIG, comm=comm,
    )(h, vec, p, p, *os, *lses, gates, wp_bd, pscale, wpb, wab, wout)


def _mix_bwd_a(dh, vec, mixout, merged, gates, ypool, yattn, dpool, os, lses, wp_bd, pscale, wpb, wab, wout, ones_bd,
               comm=None):
    T = dh.shape[0]
    nt = T // TM

    def body(dh_ref, vec_ref, mo_ref, mg_ref, gates_ref, yp_ref, ya_ref, d_ref, o0, o1, o2, l0, l1, l2,
             wp_ref, ps_ref, wpb_ref, wab_ref, wout_ref, ones_ref,
             dgates_ref, do0, do1, do2, e0, e1, e2, dd_ref, acc_ref, acc2_ref, g_out_ref, g_pb_ref, g_ab_ref, g_pool_ref,
             scr_ref, a_out, a_pb, a_ab, a_pool):
        _zero_first(acc_ref)
        _zero_first(acc2_ref)
        for a_ref in (a_out, a_pb, a_ab, a_pool):
            _zero_first(a_ref)
        gt = vec_ref[3:4, :]
        dho = dh_ref[...]
        acc_ref[3:4, :] += _colsum(dho * mo_ref[...].astype(F32))
        dmo = (gt * dho).astype(BF16)
        a_out[...] += _dot_tn(mg_ref[...], dmo)
        dmerged = _dot_nt(dmo, wout_ref[...])
        gp = gates_ref[:, 0:D].astype(F32)
        ga = gates_ref[:, D:GW].astype(F32)
        bp = _dot(yp_ref[...], wpb_ref[...])
        ba = _dot(ya_ref[...], wab_ref[...])
        dgates_ref[:, 0:D] = (dmerged * bp * gp * (1.0 - gp)).astype(BF16)
        dgates_ref[:, D:GW] = (dmerged * ba * ga * (1.0 - ga)).astype(BF16)
        dbp = (dmerged * gp).astype(BF16)
        dba = (dmerged * ga).astype(BF16)
        a_pb[...] += _dot_tn(yp_ref[...], dbp)
        a_ab[...] += _dot_tn(ya_ref[...], dba)
        dypool = _dot_nt(dbp, wpb_ref[...])
        ypre = _dot(d_ref[...], wp_ref[...])
        acc2_ref[0:1, :] += _colsum(dypool * ypre)
        dyp = (dypool * ps_ref[0:1, :]).astype(BF16)
        a_pool[...] += _dot_tn(d_ref[...], dyp)
        dd_ref[...] = _dot_nt(dyp, wp_ref[...])
        dya = _dot_nt(dba, wab_ref[...])
        w = _group_weights([_from_residues(r, scr_ref, dl) for r, dl in zip((l0, l1, l2), DIL)])
        ya = None
        for wg, o_ref, dl in zip(w, (o0, o1, o2), DIL):
            part = wg * _from_residues(o_ref, scr_ref, dl)
            ya = part if ya is None else ya + part
        prod = dya * ya
        hi = prod.astype(BF16)
        lo = (prod - hi.astype(F32)).astype(BF16)
        tot = _dot(hi, ones_ref[...]) + _dot(lo, ones_ref[...])
        for wg, do_ref, e_ref, dl in zip(w, (do0, do1, do2), (e0, e1, e2), DIL):
            _to_residues(wg * dya, do_ref, scr_ref, dl)
            _to_residues(-wg * tot, e_ref, scr_ref, dl)

        @pl.when(pl.program_id(0) == nt - 1)
        def _():
            g_out_ref[...] = a_out[...].astype(BF16)
            g_pb_ref[...] = a_pb[...].astype(BF16)
            g_ab_ref[...] = a_ab[...].astype(BF16)
            g_pool_ref[...] = a_pool[...]

    return _call(
        body, "mix_bwd_a", (nt,),
        [_rows(TM, D), _const((8, D)), _rows(TM, D), _rows(TM, D), _rows(TM, GW), _rows(TM, PW), _rows(TM, GA), _rows(TM, PW)]
        + [_rm_spec(dl) for dl in DIL] * 2
        + [_const((PW, PW)), _const((8, PW)), _const((PW, D)), _const((GA, D)), _const((D, D)), _const((GA, GA))],
        [_rows(TM, GW)] + [_rm_spec(dl) for dl in DIL] * 2 + [_rows(TM, PW), _const((8, D)), _const((8, PW))]
        + [_const((D, D)), _const((PW, D)), _const((GA, D)), _const((PW, PW))],
        [_sds((T, GW), BF16)] + [_sds((dl, T // dl, GA), BF16) for dl in DIL]
        + [_sds((dl, T // dl, GA), F32) for dl in DIL] + [_sds((T, PW), F32), _sds((8, D), F32), _sds((8, PW), F32)]
        + [_sds((D, D), BF16), _sds((PW, D), BF16), _sds((GA, D), BF16), _sds((PW, PW), F32)],
        scratch=[pltpu.VMEM((GA // LANES, TM, LANES), F32), pltpu.VMEM((D, D), F32), pltpu.VMEM((PW, D), F32),
                 pltpu.VMEM((GA, D), F32), pltpu.VMEM((PW, PW), F32)],
        vmem=VMEM_BIG, comm=comm,
    )(dh, vec, mixout, merged, gates, ypool, yattn, dpool, *os, *lses, wp_bd, pscale, wpb, wab, wout, ones_bd)


def _mix_bwd_b(dh, h, vec, dd, dqs, dks, dvs, dgates, cos, sin, win):
    T = h.shape[0]
    nt = T // TM

    def body(dh_ref, h_ref, vec_ref, ddc_ref, ddn_ref, *rest):
        qk_refs, dv_refs = rest[:2 * NG], rest[2 * NG:3 * NG]
        dgates_ref, cos_ref, sin_ref, win_hbm, dhi_ref, dproj_ref, acc_ref, win_v, ext_ref, scr_ref, sems = rest[3 * NG:]
        i = pl.program_id(0)
        _load_once([(win_hbm, win_v)], sems)
        _zero_first(acc_ref)
        g, sh, sc = vec_ref[0:1, :], vec_ref[1:2, :], vec_ref[2:3, :]
        grp, cnt = _pool_consts((TM, PW), i * TM)
        _, cnt_n = _pool_consts((HALO, PW), (i + 1) * TM)
        ext_ref[0:TM, :] = ddc_ref[...] / cnt
        ext_ref[TM:TM + HALO, :] = jnp.where(i < nt - 1, ddn_ref[0:HALO, :] / cnt_n, 0.0)
        dp = _select_group(grp, _window_sums(ext_ref, 0, 1, TM)) - ddc_ref[...]
        dproj_ref[:, 0:PW] = dp.astype(BF16)
        cos_t, sin_t = cos_ref[...], sin_ref[...]
        for j in range(2 * NG):
            col = PW + j * GA
            dt = _from_residues(qk_refs[j], scr_ref, DIL[j % NG])
            dproj_ref[:, col:col + GA] = _rope_bwd(dt, cos_t, sin_t).astype(BF16)
        for j in range(NG):
            col = PW + (2 * NG + j) * GA
            dproj_ref[:, col:col + GA] = _from_residues(dv_refs[j], scr_ref, DIL[j]).astype(BF16)
        dproj_ref[:, PW + 3 * NG * GA:INW] = dgates_ref[...]
        du = None
        for j in range(INW // 512):
            part = _dot_nt(dproj_ref[:, j * 512:(j + 1) * 512], win_v[:, j * 512:(j + 1) * 512])
            du = part if du is None else du + part
        xh, r, n, _ = _norm_fwd(h_ref[...], g, sh, sc)
        dhn, dsh, dsc, dg = _norm_bwd(du, xh, r, n, g, sc)
        dhi_ref[...] = dh_ref[...] + dhn
        acc_ref[0:1, :] += dsh
        acc_ref[1:2, :] += dsc
        acc_ref[2:3, :] += dg

    nxt = pl.BlockSpec((TM, PW), lambda i: (jnp.minimum(i + 1, nt - 1), 0))
    return _call(
        body, "mix_bwd_b", (nt,),
        [_rows(TM, D), _rows(TM, D), _const((8, D)), _rows(TM, PW), nxt] + [_rm_spec(dl) for dl in DIL] * 3
        + [_rows(TM, GW), _rows(TM, 128), _rows(TM, 128), ANY],
        [_rows(TM, D), _rows(TM, INW), _const((8, D))],
        [_sds((T, D), F32), _sds((T, INW), BF16), _sds((8, D), F32)],
        scratch=[pltpu.VMEM((D, INW), BF16), pltpu.VMEM((TM + HALO, PW), F32), pltpu.VMEM((GA // LANES, TM, LANES), F32),
                 pltpu.SemaphoreType.DMA((1,))],
        vmem=VMEM_BIG,
    )(dh, h, vec, dd, dd, *dqs, *dks, *dvs, dgates, cos, sin, win)[0]


def _ada_fwd(c_all, w_shard, b_shard):
    n = w_shard.shape[1]

    def body(c_ref, w_ref, b_ref, o_ref):
        cv = c_ref[...]
        cond = (cv * jax.nn.sigmoid(cv)).astype(BF16)
        o_ref[...] = _dot(cond, w_ref[...].astype(BF16)) + b_ref[...]

    tn = n // 3
    return pl.pallas_call(
        body, name="ada_fwd", grid=(3,),
        in_specs=[pl.BlockSpec((8, D), lambda j: (0, 0)), pl.BlockSpec((D, tn), lambda j: (0, j)), pl.BlockSpec((1, tn), lambda j: (0, j))],
        out_specs=pl.BlockSpec((8, tn), lambda j: (0, j)), out_shape=_sds((8, n), F32),
        compiler_params=pltpu.CompilerParams(dimension_semantics=("arbitrary",)),
    )(c_all, w_shard, b_shard)


def _ada_bwd(c_all, dmod_shard):
    n = dmod_shard.shape[1]

    def body(c_ref, d_ref, o_ref):
        cv = c_ref[...]
        cond = (cv * jax.nn.sigmoid(cv)).astype(BF16)
        o_ref[...] = _dot_tn(cond, d_ref[...].astype(BF16))

    tn = n // 3
    return pl.pallas_call(
        body, name="ada_bwd", grid=(3,),
        in_specs=[pl.BlockSpec((8, D), lambda j: (0, 0)), pl.BlockSpec((8, tn), lambda j: (0, j))],
        out_specs=pl.BlockSpec((D, tn), lambda j: (0, j)), out_shape=_sds((D, n), F32),
        compiler_params=pltpu.CompilerParams(dimension_semantics=("arbitrary",)),
    )(c_all, dmod_shard)


def _adam_math(w, g, m, v):
    m2 = B1 * m + (1.0 - B1) * g
    v2 = B2 * v + (1.0 - B2) * (g * g)
    m_hat = m2 / (1.0 - B1 ** STEP)
    v_hat = v2 / (1.0 - B2 ** STEP)
    delta = -LR * (m_hat / (jnp.sqrt(v_hat) + AEPS) + WD * w)
    return delta, m2, v2


def _adam(w, m, v, parts, name, comm=None):
    R, C = w.shape
    tr = R
    for cand in (128, 64, 32, 16, 8):
        if R % cand == 0:
            tr = cand
            break
    np_ = len(parts)

    def body(w_ref, m_ref, v_ref, *rest):
        p_refs, (g_ref, d_ref, m2_ref, v2_ref) = rest[:np_], rest[np_:]
        g = p_refs[0][...]
        for pr in p_refs[1:]:
            g = g + pr[...]
        delta, m2, v2 = _adam_math(w_ref[...], g, m_ref[...], v_ref[...])
        g_ref[...] = g
        d_ref[...] = delta
        m2_ref[...] = m2
        v2_ref[...] = v2

    spec = pl.BlockSpec((tr, C), lambda i: (i, 0))
    return _call(body, name, (R // tr,), [spec] * (3 + np_), [spec] * 4, [_sds((R, C), F32)] * 4,
                 vmem=VMEM_BIG, comm=comm)(w, m, v, *parts)


def _adam_halves(w, m, v, mine, other, name):
    R, C = w.shape
    tr = 128
    nh = R // 2 // tr

    def body(c_ref, w_ref, m_ref, v_ref, mine_ref, other_ref, g_ref, d_ref, m2_ref, v2_ref):
        i = pl.program_id(0)
        in_mine = jnp.logical_and(i >= c_ref[0] * nh, i < (c_ref[0] + 1) * nh)
        g = jnp.where(in_mine, mine_ref[...], other_ref[...])
        delta, m2, v2 = _adam_math(w_ref[...], g, m_ref[...], v_ref[...])
        g_ref[...] = g
        d_ref[...] = delta
        m2_ref[...] = m2
        v2_ref[...] = v2

    spec = pl.BlockSpec((tr, C), lambda i, c: (i, 0))
    grid_spec = pltpu.PrefetchScalarGridSpec(
        num_scalar_prefetch=1, grid=(R // tr,),
        in_specs=[spec] * 3 + [pl.BlockSpec((tr, C), lambda i, c: (jnp.clip(i - c[0] * nh, 0, nh - 1), 0)),
                               pl.BlockSpec((tr, C), lambda i, c: (jnp.clip(i - (1 - c[0]) * nh, 0, nh - 1), 0))],
        out_specs=[spec] * 4)
    return pl.pallas_call(
        body, name=name, grid_spec=grid_spec, out_shape=[_sds((R, C), F32)] * 4,
        compiler_params=pltpu.CompilerParams(dimension_semantics=("arbitrary",), vmem_limit_bytes=VMEM_BIG),
    )(lax.axis_index("c").astype(jnp.int32).reshape(1), w, m, v, mine, other)


def _adam_small(ws, ms, vs, gathered):
    n = len(ws)
    sizes = [a.shape[1] for a in ws]

    def total(ga_ref, off, size):
        g = ga_ref[0, :, off:off + size]
        for dev in range(1, 8):
            g = g + ga_ref[dev, :, off:off + size]
        return g

    def body(*refs):
        w_refs, m_refs, v_refs, ga_ref, outs = refs[:n], refs[n:2 * n], refs[2 * n:3 * n], refs[3 * n], refs[3 * n + 1:]
        off = 0
        for j, size in enumerate(sizes):
            g = total(ga_ref, off, size)
            delta, m2, v2 = _adam_math(w_refs[j][...], g, m_refs[j][...], v_refs[j][...])
            for ref, val in zip(outs[4 * j:4 * j + 4], (g, delta, m2, v2)):
                ref[...] = val
            off += size
        outs[4 * n][...] = total(ga_ref, off, 128)

    res = pl.pallas_call(
        body, name="adam_small",
        out_shape=[_sds((1, size), F32) for size in sizes for _ in range(4)] + [_sds((1, 128), F32)],
    )(*ws, *ms, *vs, gathered)
    return [res[4 * j:4 * j + 4] for j in range(n)], res[4 * n]


def _sum4(blocks, name):
    _, R, C = blocks.shape
    tr = R
    for cand in (256, 128, 64, 32, 16):
        if R % cand == 0:
            tr = cand
            break

    def body(r_ref, out_ref):
        out_ref[...] = ((r_ref[0].astype(F32) + r_ref[1].astype(F32)) + r_ref[2].astype(F32)) + r_ref[3].astype(F32)

    return pl.pallas_call(
        body, name=name, grid=(R // tr,),
        in_specs=[pl.BlockSpec((4, tr, C), lambda i: (0, i, 0))],
        out_specs=pl.BlockSpec((tr, C), lambda i: (i, 0)), out_shape=_sds((R, C), F32),
        compiler_params=pltpu.CompilerParams(dimension_semantics=("arbitrary",)),
    )(blocks)


def _place():
    return lax.axis_index("x"), lax.axis_index("y"), lax.axis_index("c")


def _chip_peer(x, y, c, m):
    return (x ^ (m >> 1), y ^ (m & 1), c)


def _shard_ref(ref, axis, k, n):
    start = pl.multiple_of(k * n, 128 if axis == 1 else 16)
    return ref.at[:, pl.ds(start, n)] if axis == 1 else ref.at[pl.ds(start, n), :]


def _half_rows(ref, axis, k, n, hc):
    if axis == 1:
        half = ref.shape[0] // 2
        return ref.at[pl.ds(pl.multiple_of(hc * half, 16), half), pl.ds(pl.multiple_of(k * n, 128), n)]
    half = n // 2
    return ref.at[pl.ds(pl.multiple_of(k * n + hc * half, 16), half), :]


class _GatherPlan:
    def __init__(self, shards, axes):
        self.inputs, self.axes, nw = list(shards), list(axes), len(shards)
        self.out_shapes = [_sds((s.shape[0] * (4 if ax == 0 else 1), s.shape[1] * (4 if ax == 1 else 1)), BF16)
                           for s, ax in zip(shards, axes)]
        self.sem_shapes = [pltpu.SemaphoreType.DMA((nw,))] + [pltpu.SemaphoreType.DMA((nw, 3))] * 4

    def _copies(self, ins, outs, sems):
        local_sems, send_sems, recv_sems, pass_sems, got_sems = sems
        x, y, c = _place()
        k = 2 * x + y
        local, sends, arrivals, passes, handed = [], [], [], [], []
        for j, ax in enumerate(self.axes):
            n = ins[j].shape[ax]
            half = ins[j].shape[0] // 2
            local.append(pltpu.make_async_copy(ins[j], _shard_ref(outs[j], ax, k, n), local_sems.at[j]))
            my_half = ins[j].at[pl.ds(pl.multiple_of(c * half, 16), half), :]
            for m in range(1, 4):
                sends.append(pltpu.make_async_remote_copy(
                    src_ref=my_half, dst_ref=_half_rows(outs[j], ax, k, n, c), send_sem=send_sems.at[j, m - 1],
                    recv_sem=recv_sems.at[j, m - 1], device_id=_chip_peer(x, y, c, m), device_id_type=MESH))
                theirs = _half_rows(outs[j], ax, k ^ m, n, c)
                arrivals.append(pltpu.make_async_remote_copy(
                    src_ref=my_half, dst_ref=theirs, send_sem=send_sems.at[j, m - 1], recv_sem=recv_sems.at[j, m - 1],
                    device_id=(x, y, c), device_id_type=MESH))
                passes.append(pltpu.make_async_remote_copy(
                    src_ref=theirs, dst_ref=theirs, send_sem=pass_sems.at[j, m - 1], recv_sem=got_sems.at[j, m - 1],
                    device_id=(x, y, 1 - c), device_id_type=MESH))
                other = _half_rows(outs[j], ax, k ^ m, n, 1 - c)
                handed.append(pltpu.make_async_remote_copy(
                    src_ref=other, dst_ref=other, send_sem=pass_sems.at[j, m - 1], recv_sem=got_sems.at[j, m - 1],
                    device_id=(x, y, c), device_id_type=MESH))
        return local, sends, arrivals, passes, handed

    def start(self, ins, outs, sems):
        local, sends, _, _, _ = self._copies(ins, outs, sems)
        for cp in local + sends:
            cp.start()

    def relay(self, ins, outs, sems):
        _, _, arrivals, passes, _ = self._copies(ins, outs, sems)
        for arrived, onward in zip(arrivals, passes):
            arrived.wait_recv()
            onward.start()

    def wait(self, ins, outs, sems):
        local, sends, _, passes, handed = self._copies(ins, outs, sems)
        for cp in handed:
            cp.wait_recv()
        for cp in sends + passes:
            cp.wait_send()
        for cp in local:
            cp.wait()


class _ScatterPlan:
    def __init__(self, grads, axes):
        self.inputs, self.axes, nw = list(grads), list(axes), len(grads)
        self.shard_shapes = [(g.shape[0] // (4 if ax == 0 else 1), g.shape[1] // (4 if ax == 1 else 1))
                             for g, ax in zip(grads, axes)]
        self.out_shapes = [_sds((4,) + s, BF16) for s in self.shard_shapes]
        self.sem_shapes = [pltpu.SemaphoreType.DMA((nw,)), pltpu.SemaphoreType.DMA((nw, 3)), pltpu.SemaphoreType.DMA((nw, 3))]

    def _copies(self, ins, outs, sems):
        local_sems, send_sems, recv_sems = sems
        x, y, c = _place()
        k = 2 * x + y
        local, remote, arrivals = [], [], []
        for j, ax in enumerate(self.axes):
            n = self.shard_shapes[j][ax]
            local.append(pltpu.make_async_copy(_shard_ref(ins[j], ax, k, n), outs[j].at[0], local_sems.at[j]))
            for m in range(1, 4):
                remote.append(pltpu.make_async_remote_copy(
                    src_ref=_shard_ref(ins[j], ax, k ^ m, n), dst_ref=outs[j].at[m],
                    send_sem=send_sems.at[j, m - 1], recv_sem=recv_sems.at[j, m - 1],
                    device_id=_chip_peer(x, y, c, m), device_id_type=MESH))
                arrivals.append(pltpu.make_async_remote_copy(
                    src_ref=_shard_ref(ins[j], ax, k, n), dst_ref=outs[j].at[m],
                    send_sem=send_sems.at[j, m - 1], recv_sem=recv_sems.at[j, m - 1],
                    device_id=(x, y, c), device_id_type=MESH))
        return local, remote, arrivals

    def start(self, ins, outs, sems):
        local, remote, _ = self._copies(ins, outs, sems)
        for cp in local + remote:
            cp.start()

    def relay(self, ins, outs, sems):
        pass

    def wait(self, ins, outs, sems):
        local, remote, arrivals = self._copies(ins, outs, sems)
        for cp in arrivals:
            cp.wait_recv()
        for cp in remote:
            cp.wait_send()
        for cp in local:
            cp.wait()


def _run_plan(plan, name):
    nc = len(plan.inputs)

    def body(*refs):
        ins, outs, sems = refs[:nc], refs[nc:2 * nc], refs[2 * nc:]
        plan.start(ins, outs, sems)
        plan.relay(ins, outs, sems)
        plan.wait(ins, outs, sems)

    return pl.pallas_call(body, name=name, in_specs=[ANY] * nc, out_specs=[ANY] * nc, out_shape=list(plan.out_shapes),
                          scratch_shapes=list(plan.sem_shapes))(*plan.inputs)


class _SwapPlan:
    def __init__(self, parts):
        self.inputs, nw = list(parts), len(parts)
        self.out_shapes = [_sds(p.shape, p.dtype) for p in parts]
        self.sem_shapes = [pltpu.SemaphoreType.DMA((nw,)), pltpu.SemaphoreType.DMA((nw,))]

    def _copies(self, ins, outs, sems):
        send_sems, recv_sems = sems
        x, y, c = _place()
        return [pltpu.make_async_remote_copy(
            src_ref=ins[j], dst_ref=outs[j], send_sem=send_sems.at[j], recv_sem=recv_sems.at[j],
            device_id=(x, y, 1 - c), device_id_type=MESH) for j in range(len(ins))]

    def start(self, ins, outs, sems):
        for cp in self._copies(ins, outs, sems):
            cp.start()

    def relay(self, ins, outs, sems):
        pass

    def wait(self, ins, outs, sems):
        for cp in self._copies(ins, outs, sems):
            cp.wait()


class _SmallGatherPlan:
    def __init__(self, v):
        self.inputs = [v]
        self.out_shapes = [_sds((8,) + v.shape, v.dtype)]
        self.sem_shapes = [pltpu.SemaphoreType.DMA((1,)), pltpu.SemaphoreType.DMA((7,)), pltpu.SemaphoreType.DMA((7,))]

    def _copies(self, ins, outs, sems):
        (v_ref,), (out_ref,), (local_sem, send_sems, recv_sems) = ins, outs, sems
        x, y, c = _place()
        me = 4 * x + 2 * y + c
        local = pltpu.make_async_copy(v_ref, out_ref.at[me], local_sem.at[0])
        sends, arrivals = [], []
        for m in range(1, 8):
            px, py, pc = x ^ (m >> 2), y ^ ((m >> 1) & 1), c ^ (m & 1)
            sends.append(pltpu.make_async_remote_copy(
                src_ref=v_ref, dst_ref=out_ref.at[me], send_sem=send_sems.at[m - 1], recv_sem=recv_sems.at[m - 1],
                device_id=(px, py, pc), device_id_type=MESH))
            arrivals.append(pltpu.make_async_remote_copy(
                src_ref=v_ref, dst_ref=out_ref.at[4 * px + 2 * py + pc], send_sem=send_sems.at[m - 1],
                recv_sem=recv_sems.at[m - 1], device_id=(x, y, c), device_id_type=MESH))
        return local, sends, arrivals

    def start(self, ins, outs, sems):
        local, sends, _ = self._copies(ins, outs, sems)
        for cp in [local] + sends:
            cp.start()

    def relay(self, ins, outs, sems):
        pass

    def wait(self, ins, outs, sems):
        local, sends, arrivals = self._copies(ins, outs, sems)
        for cp in arrivals:
            cp.wait_recv()
        for cp in sends:
            cp.wait_send()
        local.wait()


class _PlanGroup:
    def __init__(self, plans):
        self.plans = [p for p in plans if p is not None]
        self.inputs = [a for p in self.plans for a in p.inputs]
        self.out_shapes = [s for p in self.plans for s in p.out_shapes]
        self.sem_shapes = [s for p in self.plans for s in p.sem_shapes]

    def _each(self, ins, outs, sems):
        i = s = 0
        for p in self.plans:
            n, ns = len(p.inputs), len(p.sem_shapes)
            yield p, ins[i:i + n], outs[i:i + n], sems[s:s + ns]
            i, s = i + n, s + ns

    def start(self, ins, outs, sems):
        for p, pi, po, ps in self._each(ins, outs, sems):
            p.start(pi, po, ps)

    def relay(self, ins, outs, sems):
        for p, pi, po, ps in self._each(ins, outs, sems):
            p.relay(pi, po, ps)

    def wait(self, ins, outs, sems):
        for p, pi, po, ps in self._each(ins, outs, sems):
            p.wait(pi, po, ps)

    def split(self, outs):
        res, i = [], 0
        for p in self.plans:
            res.append(outs[i:i + len(p.inputs)])
            i += len(p.inputs)
        return res


BIG = ("w_ffn1_in", "w_ffn1_out", "w_in", "w_pool_branch", "w_attn_branch", "w_out", "w_ffn2_in", "w_ffn2_out")
BIG_AXIS = {"w_ffn1_in": 1, "w_ffn1_out": 0, "w_in": 1, "w_pool_branch": 1, "w_attn_branch": 1, "w_out": 0,
            "w_ffn2_in": 1, "w_ffn2_out": 0}


class _Sharded:
    fused_scatter = True

    def __init__(self, shards):
        self.shards, self.full, self.recv = shards, {}, {}

    def gather_plan(self, names):
        return _GatherPlan([self.shards[n] for n in names], [BIG_AXIS[n.split("/")[0]] for n in names])

    def gather_now(self, names):
        self.gathered(names, _run_plan(self.gather_plan(names), "gather_" + names[0]))

    def gathered(self, names, outs):
        self.full.update(zip(names, outs))

    def scatter_plan(self, names, grads):
        return _ScatterPlan([grads[n] for n in names], [BIG_AXIS[n] for n in names])

    def scatter_now(self, names, grads):
        self.scattered(names, _run_plan(self.scatter_plan(names, grads), "scatter_" + names[0]))

    def scattered(self, names, outs):
        self.recv.update(zip(names, outs))


class _Whole:
    fused_scatter = False

    def __init__(self, full):
        self.full, self.recv = dict(full), {}

    def gather_plan(self, names):
        return None

    def gather_now(self, names):
        pass

    def gathered(self, names, outs):
        pass

    def scatter_plan(self, names, grads):
        return None

    def scatter_now(self, names, grads):
        pass

    def scattered(self, names, outs):
        pass


def _vec(rows):
    pad = [jnp.zeros((1, D), F32)] * (8 - len(rows))
    return jnp.concatenate([r.reshape(1, D) for r in rows] + pad, axis=0)


def _block_diag(w_pool):
    n, c = w_pool.shape[0], w_pool.shape[1]
    eye = jnp.eye(n, dtype=w_pool.dtype)
    return (eye[:, None, :, None] * w_pool[:, :, None, :]).reshape(n * c, n * c)


def _example_step(x, tgt, positions, mod, gains, w_pool, pool_scale, ws, pack=None):
    T = x.shape[0]
    assert (T // BLK // DIL[-1]) & (T // BLK // DIL[-1] - 1) == 0, "blocks per sequence must be a power of two"
    sh1, sc1, gt1, sh2, sc2, gt2, sh3, sc3, gt3 = [mod[j * D:(j + 1) * D] for j in range(NMOD)]
    g1, g2, g3, gf = gains
    vec1, vec2, vec3 = _vec([g1, sh1, sc1, gt1]), _vec([g2, sh2, sc2, gt2]), _vec([g3, sh3, sc3, gt3])
    inv_freq = 10000.0 ** (-jnp.arange(0, HD, 2, dtype=F32) / HD)
    ang = positions.astype(F32)[:, None] * inv_freq
    cos = jnp.tile(jnp.cos(ang), (1, 4))
    sin = jnp.tile(jnp.concatenate([-jnp.sin(ang), jnp.sin(ang)], axis=1), (1, 2))
    wp_bd = _block_diag(w_pool).astype(BF16)
    ones_bd = _block_diag(jnp.ones((NH, HD, HD), F32)).astype(BF16)
    ps = jnp.concatenate([pool_scale.reshape(1, PW), jnp.zeros((7, PW), F32)], axis=0)
    wb = ws.full

    if "w_ffn1_in" not in wb:
        ws.gather_now(["w_ffn1_in"])
    (u1, a1, b1), got = _ffn_ab(x, vec1, [wb["w_ffn1_in"]], "ffn1_ab", ws.gather_plan(["w_ffn1_out", "w_in"]))
    ws.gathered(["w_ffn1_out", "w_in"], got)
    mixw = ["w_pool_branch", "w_attn_branch", "w_out", "w_ffn2_in/1"]
    (h1, f1), got = _ffn_out(x, a1, b1, vec1, wb["w_ffn1_out"], "ffn1_out", ws.gather_plan(mixw))
    ws.gathered(mixw, got)
    (u2, p, qs, ks, vs, gates), got = _mix_proj(h1, vec2, wb["w_in"], cos, sin, ws.gather_plan(["w_ffn2_in/0", "w_ffn2_out"]))
    ws.gathered(["w_ffn2_in/0", "w_ffn2_out"], got)
    qs, ks, vs = [_flat(t) for t in qs], [_flat(t) for t in ks], [_flat(t) for t in vs]
    nbs = [T // d // BLK for d in DIL]
    os, lses = [], []
    for gi in range(NG):
        (o, lse), _ = _attn_fwd(qs[gi], ks[gi], vs[gi], nbs[gi], f"attn_fwd{gi}")
        os.append(o)
        lses.append(lse)
    os_r = [_by_residue(t, d) for t, d in zip(os, DIL)]
    lses_r = [_by_residue(t, d) for t, d in zip(lses, DIL)]
    (h2, ypool, yattn, merged, mixout, dpool), _ = _mix_merge(
        h1, vec2, p, os_r, lses_r, gates, wp_bd, ps, wb["w_pool_branch"], wb["w_attn_branch"], wb["w_out"])
    win3 = [wb["w_ffn2_in/0"], wb["w_ffn2_in/1"]] if "w_ffn2_in/0" in wb else [wb["w_ffn2_in"]]
    (dh3, u3, a3, b3, f3, lacc), _ = _ffn_fwd(h2, vec3, win3, wb["w_ffn2_out"], "ffn2_fwd", head=(tgt, _vec([gf])))
    loss = 0.5 * jnp.sum(lacc[0]) / D

    grads = {}

    def wgrad_cols(name, xx, yy, riders, extra=None):
        group = _PlanGroup([ws.scatter_plan(riders, grads) if riders else None, extra])
        plan = group if group.plans else None
        if ws.fused_scatter:
            blocks, got = _wgrad_scatter(xx, yy, "wg_" + name, min(2048, T // 2), comm=plan)
            ws.scattered([name], [blocks])
        else:
            grads[name], got = _wgrad(xx, yy, "wg_" + name, D, 512, 1024, comm=plan)
        parts = group.split(got)
        if len(parts) > (extra is not None):
            ws.scattered(riders, parts[0])
        return parts[-1] if extra is not None else None

    (dh2, dab3, s3, df3, acc3), _ = _ffn_bwd(dh3, h2, a3, b3, f3, vec3, win3, wb["w_ffn2_out"], "ffn2_bwd")
    grads["w_ffn2_out"], _ = _wgrad(s3, df3, "wg_ffn2_out", FF // 2, 512, min(4096, T // 2))
    wgrad_cols("w_ffn2_in", u3, dab3, ["w_ffn2_out"])
    (dgates, do0, do1, do2, e0, e1, e2, dd, acc2a, accps,
     grads["w_out"], grads["w_pool_branch"], grads["w_attn_branch"], gwp), _ = _mix_bwd_a(
        dh2, vec2, mixout, merged, gates, ypool, yattn, dpool, os_r, lses_r, wp_bd, ps,
        wb["w_pool_branch"], wb["w_attn_branch"], wb["w_out"], ones_bd)
    n = len(POOL_WINDOWS)
    c = PW // n
    grad_w_pool = jnp.stack([gwp[j * c:(j + 1) * c, j * c:(j + 1) * c] for j in range(n)], axis=0)
    small3 = ["w_out", "w_pool_branch", "w_attn_branch"]
    dqs, dks, dvs = [], [], []
    for gi, (do, e) in enumerate(((do0, e0), (do1, e1), (do2, e2))):
        (dq, dk, dv), _ = _attn_bwd(qs[gi], ks[gi], vs[gi], _flat(do), lses[gi], _flat(e), nbs[gi], f"attn_bwd{gi}")
        dqs.append(_by_residue(dq, DIL[gi]))
        dks.append(_by_residue(dk, DIL[gi]))
        dvs.append(_by_residue(dv, DIL[gi]))
    dh1, dproj, acc2b = _mix_bwd_b(dh2, h1, vec2, dd, dqs, dks, dvs, dgates, cos, sin, wb["w_in"])
    wgrad_cols("w_in", u2, dproj, small3)
    (dx, dab1, s1, df1, acc1), _ = _ffn_bwd(dh1, x, a1, b1, f1, vec1, [wb["w_ffn1_in"]], wb["w_ffn1_out"], "ffn1_bwd")
    grads["w_ffn1_out"], _ = _wgrad(s1, df1, "wg_ffn1_out", FF // 2, 512, min(4096, T // 2))
    dmod = jnp.concatenate([acc1[0], acc1[1], acc1[3], acc2b[0], acc2b[1], acc2a[3], acc3[0], acc3[1], acc3[3]])
    dgains = jnp.stack([acc1[2], acc2b[2], acc3[2], lacc[1]], axis=0)
    row = None if pack is None else _SmallGatherPlan(pack(loss, dmod, dgains, grad_w_pool, accps[0]))
    early = [n for n in BIG if n in ws.recv] if pack is not None else []
    ws.sums = {n: _sum4(ws.recv[n], "sum_" + n) for n in early}
    ws.other = {}
    tail = _PlanGroup([row, _SwapPlan([ws.sums[n] for n in early]) if early else None])
    got = wgrad_cols("w_ffn1_in", u1, dab1, ["w_ffn1_out"], tail if tail.plans else None)
    gathered = None
    if got is not None:
        parts = tail.split(got)
        gathered = parts[0][0]
        ws.other = dict(zip(early, parts[1])) if early else {}
    return loss, dx, dmod, dgains, grad_w_pool, accps[0], grads, gathered


SMALL = ("b_ada", "g_norm_ffn1", "g_norm_mix", "g_norm_ffn2", "g_final", "pool_scale", "w_pool")
WEIGHTS = ("w_ada", "b_ada", "g_norm_ffn1", "w_ffn1_in", "w_ffn1_out", "g_norm_mix", "w_in", "w_pool", "pool_scale",
           "w_pool_branch", "w_attn_branch", "w_out", "g_norm_ffn2", "w_ffn2_in", "w_ffn2_out", "g_final")


def _pack_small(t):
    return jnp.concatenate([t[n].reshape(-1) for n in SMALL]).reshape(1, -1)


def kernel(x, c, positions, w_ada, b_ada, g_norm_ffn1, w_ffn1_in, w_ffn1_out, g_norm_mix, w_in, w_pool, pool_scale, w_pool_branch, w_attn_branch, w_out, g_norm_ffn2, w_ffn2_in, w_ffn2_out, g_final, loss_target, m_w_ada, m_b_ada, m_g_norm_ffn1, m_w_ffn1_in, m_w_ffn1_out, m_g_norm_mix, m_w_in, m_w_pool, m_pool_scale, m_w_pool_branch, m_w_attn_branch, m_w_out, m_g_norm_ffn2, m_w_ffn2_in, m_w_ffn2_out, m_g_final, v_w_ada, v_b_ada, v_g_norm_ffn1, v_w_ffn1_in, v_w_ffn1_out, v_g_norm_mix, v_w_in, v_w_pool, v_pool_scale, v_w_pool_branch, v_w_attn_branch, v_w_out, v_g_norm_ffn2, v_w_ffn2_in, v_w_ffn2_out, v_g_final):
    w = dict(w_ada=w_ada, b_ada=b_ada, g_norm_ffn1=g_norm_ffn1, w_ffn1_in=w_ffn1_in, w_ffn1_out=w_ffn1_out,
             g_norm_mix=g_norm_mix, w_in=w_in, w_pool=w_pool, pool_scale=pool_scale, w_pool_branch=w_pool_branch,
             w_attn_branch=w_attn_branch, w_out=w_out, g_norm_ffn2=g_norm_ffn2, w_ffn2_in=w_ffn2_in,
             w_ffn2_out=w_ffn2_out, g_final=g_final)
    mom = dict(w_ada=m_w_ada, b_ada=m_b_ada, g_norm_ffn1=m_g_norm_ffn1, w_ffn1_in=m_w_ffn1_in, w_ffn1_out=m_w_ffn1_out,
               g_norm_mix=m_g_norm_mix, w_in=m_w_in, w_pool=m_w_pool, pool_scale=m_pool_scale,
               w_pool_branch=m_w_pool_branch, w_attn_branch=m_w_attn_branch, w_out=m_w_out, g_norm_ffn2=m_g_norm_ffn2,
               w_ffn2_in=m_w_ffn2_in, w_ffn2_out=m_w_ffn2_out, g_final=m_g_final)
    var = dict(w_ada=v_w_ada, b_ada=v_b_ada, g_norm_ffn1=v_g_norm_ffn1, w_ffn1_in=v_w_ffn1_in, w_ffn1_out=v_w_ffn1_out,
               g_norm_mix=v_g_norm_mix, w_in=v_w_in, w_pool=v_w_pool, pool_scale=v_pool_scale,
               w_pool_branch=v_w_pool_branch, w_attn_branch=v_w_attn_branch, w_out=v_w_out, g_norm_ffn2=v_g_norm_ffn2,
               w_ffn2_in=v_w_ffn2_in, w_ffn2_out=v_w_ffn2_out, g_final=v_g_final)
    ix, iy, ic = _place()
    chip = 2 * ix + iy
    me = 4 * ix + 2 * iy + ic
    nada = w_ada.shape[2]

    shards = {n: w[n][0].astype(BF16) for n in BIG}
    half = D // 2
    shards["w_ffn2_in/0"], shards["w_ffn2_in/1"] = shards["w_ffn2_in"][:half], shards["w_ffn2_in"][half:]
    ws = _Sharded(shards)
    c_all = _run_plan(_SmallGatherPlan(c), "gather_c")[0][:, 0, :]
    b_shard = lax.dynamic_slice_in_dim(b_ada, chip * nada, nada, axis=1)
    mod_cols = _ada_fwd(c_all, w_ada[0], b_shard)
    first = _PlanGroup([_SmallGatherPlan(mod_cols), ws.gather_plan(["w_ffn1_in"])])
    (mod_all,), ffn1 = first.split(_run_plan(first, "gather_first"))
    ws.gathered(["w_ffn1_in"], ffn1)
    mod = jnp.concatenate([lax.dynamic_index_in_dim(mod_all[4 * (kk >> 1) + 2 * (kk & 1)], me, axis=0, keepdims=False)
                           for kk in range(4)])

    def pack(loss, dmod, dgains, g_w_pool, g_pool_scale):
        small_g = dict(b_ada=dmod, g_norm_ffn1=dgains[0], g_norm_mix=dgains[1], g_norm_ffn2=dgains[2],
                       g_final=dgains[3], pool_scale=g_pool_scale, w_pool=g_w_pool)
        return jnp.concatenate([_pack_small(small_g), jnp.pad(loss.reshape(1, 1), ((0, 0), (0, 127)))], axis=1)

    _, dx, _, _, _, _, _, gathered = _example_step(
        x[0], loss_target[0], positions[0], mod, (g_norm_ffn1[0], g_norm_mix[0], g_norm_ffn2[0], g_final),
        w_pool[0], pool_scale[0], ws, pack)

    per_weight, loss_tile = _adam_small(*[[t[n].reshape(1, -1) for n in SMALL] for t in (w, mom, var)], gathered)
    small_out = [{n: per_weight[j][kind].reshape(w[n].shape) for j, n in enumerate(SMALL)} for kind in range(4)]
    loss = loss_tile[0, 0]

    dmod_all = gathered[:, 0, :NMOD * D]
    dmod_cols = lax.dynamic_slice_in_dim(dmod_all, chip * nada, nada, axis=1)
    g_ada = _ada_bwd(c_all, dmod_cols)

    ada_out = _adam(w_ada[0], m_w_ada[0], v_w_ada[0], [g_ada], "adam_w_ada")[0]

    sums, other = dict(ws.sums), dict(ws.other)
    late = [n for n in BIG if n not in sums]
    sums.update({n: _sum4(ws.recv[n], "sum_" + n) for n in late})
    other.update(zip(late, _run_plan(_SwapPlan([sums[n] for n in late]), "swap_sibling")))
    big_out = {}
    for n in BIG:
        if sums[n].shape[0] < w[n].shape[1]:
            big_out[n] = _adam_halves(w[n][0], mom[n][0], var[n][0], sums[n], other[n], "adam_" + n)
        else:
            big_out[n] = _adam(w[n][0], mom[n][0], var[n][0], [sums[n], other[n]], "adam_" + n)[0]

    def leaf(kind, n):
        if n == "w_ada":
            return ada_out[kind][None]
        if n in big_out:
            return big_out[n][kind][None]
        return small_out[kind][n]

    return (loss, dx[None], *[leaf(kind, n) for kind in range(4) for n in WEIGHTS])
```

```python
import jax
import jax.numpy as jnp
from jax import lax
from jax.experimental import pallas as pl
from jax.experimental.pallas import tpu as pltpu

F32 = jnp.float32
BF16 = jnp.bfloat16

D = 1024
FF = 2816
FC = FF
PW = 256
GA = 256
HD = 64
LANES = 128
NH = GA // HD
NG = 3
DIL = (1, 4, 16)
BLK = 128
FWD_BLOCKS = 16
BWD_BLOCKS = 16
GW = 2 * D
INW = PW + 3 * NG * GA + GW
NMOD = 9
POOL_WINDOWS = (2, 4, 8, 16)
HALO = 16
EPS = 1e-6
SCALE = HD ** -0.5
NEG = -1e30

LR, B1, B2, AEPS, WD, STEP = 0.001, 0.9, 0.999, 1e-08, 0.01, 10

VMEM_BIG = 56 * 1024 * 1024
TM = 256

MESH = pl.DeviceIdType.MESH
ANY = pl.BlockSpec(memory_space=pl.ANY)


def _call(body, name, grid, in_specs, out_specs, out_shape, scratch=(), vmem=None, comm=None):
    params = pltpu.CompilerParams(dimension_semantics=("arbitrary",) * len(grid), vmem_limit_bytes=vmem)
    n_in, n_out, n_scr = len(in_specs), len(out_shape), len(scratch)
    if comm is None:
        call = pl.pallas_call(body, name=name, grid=grid, in_specs=list(in_specs), out_specs=list(out_specs),
                              out_shape=list(out_shape), scratch_shapes=list(scratch), compiler_params=params)
        return lambda *args: (call(*args), ())
    nc = len(comm.inputs)

    def body_with_comm(*refs):
        ins, refs = refs[:n_in], refs[n_in:]
        c_ins, refs = refs[:nc], refs[nc:]
        outs, refs = refs[:n_out], refs[n_out:]
        c_outs, refs = refs[:nc], refs[nc:]
        scr, sems = refs[:n_scr], refs[n_scr:]
        first = pl.program_id(0) == 0
        last = pl.program_id(0) == grid[0] - 1
        for ax in range(1, len(grid)):
            first = jnp.logical_and(first, pl.program_id(ax) == 0)
            last = jnp.logical_and(last, pl.program_id(ax) == grid[ax] - 1)

        @pl.when(first)
        def _():
            comm.start(c_ins, c_outs, sems)

        body(*ins, *outs, *scr)
        early_relay = len(grid) == 1 and grid[0] >= 4
        if early_relay:
            @pl.when(pl.program_id(0) == (3 * grid[0]) // 4)
            def _():
                comm.relay(c_ins, c_outs, sems)

        @pl.when(last)
        def _():
            if not early_relay:
                comm.relay(c_ins, c_outs, sems)
            comm.wait(c_ins, c_outs, sems)

    call = pl.pallas_call(
        body_with_comm, name=name, grid=grid, in_specs=list(in_specs) + [ANY] * nc,
        out_specs=list(out_specs) + [ANY] * nc, out_shape=list(out_shape) + list(comm.out_shapes),
        scratch_shapes=list(scratch) + list(comm.sem_shapes), compiler_params=params)

    def run(*args):
        res = call(*args, *comm.inputs)
        return res[:n_out], res[n_out:]

    return run


def _rows(tm, n):
    return pl.BlockSpec((tm, n), lambda i: (i, 0))


def _const(shape):
    return pl.BlockSpec(shape, lambda i: (0,) * len(shape))


def _sds(shape, dtype):
    return jax.ShapeDtypeStruct(shape, dtype)


def _dot(a, b):
    return jnp.dot(a, b, preferred_element_type=F32)


def _dot_nt(a, b):
    return lax.dot_general(a, b, (((1,), (1,)), ((), ())), preferred_element_type=F32)


def _dot_tn(a, b):
    return lax.dot_general(a, b, (((0,), (0,)), ((), ())), preferred_element_type=F32)


def _colsum(v):
    return jnp.sum(v, axis=0, keepdims=True)


def _norm_fwd(h, g, sh, sc):
    r = lax.rsqrt(jnp.mean(h * h, axis=-1, keepdims=True) + EPS)
    xh = h * r
    n = xh * g
    return xh, r, n, n * (1.0 + sc) + sh


def _norm_bwd(du, xh, r, n, g, sc):
    dn = du * (1.0 + sc)
    dxh = dn * g
    dh = r * (dxh - xh * jnp.mean(dxh * xh, axis=-1, keepdims=True))
    return dh, _colsum(du), _colsum(du * n), _colsum(dn * xh)


def _load_once(pairs, sems):
    @pl.when(pl.program_id(0) == 0)
    def _():
        cps = [pltpu.make_async_copy(src, dst, sems.at[j]) for j, (src, dst) in enumerate(pairs)]
        for cp in cps:
            cp.start()
        for cp in cps:
            cp.wait()


def _zero_first(ref):
    @pl.when(pl.program_id(0) == 0)
    def _():
        ref[...] = jnp.zeros(ref.shape, ref.dtype)


def _row_chunks(hbm_refs, vmem_ref):
    pairs, row = [], 0
    for ref in hbm_refs:
        pairs.append((ref, vmem_ref.at[pl.ds(row, ref.shape[0]), :]))
        row += ref.shape[0]
    return pairs


def _loss_head(hh, tgt, g):
    r = lax.rsqrt(jnp.mean(hh * hh, axis=-1, keepdims=True) + EPS)
    xh = hh * r
    err = xh * g - tgt
    dy = err * (1.0 / D)
    dxh = dy * g
    dh = r * (dxh - xh * jnp.mean(dxh * xh, axis=-1, keepdims=True))
    return dh, _colsum(err * err), _colsum(dy * xh)


def _ffn_fwd(h, vec, wins, wout, name, comm=None, head=None):
    T = h.shape[0]
    nwin = len(wins)
    nhead = 0 if head is None else 2

    def body(h_ref, vec_ref, *rest):
        head_refs, rest = rest[:nhead], rest[nhead:]
        win_hbms, rest = rest[:nwin], rest[nwin:]
        (wout_hbm, ho_ref, u_ref, a_ref, b_ref, f_ref), rest = rest[:6], rest[6:]
        lacc_refs, (win_v, wout_v, sems) = rest[:nhead // 2], rest[nhead // 2:]
        _load_once(_row_chunks(win_hbms, win_v) + [(wout_hbm, wout_v)], sems)
        hh = h_ref[...]
        g, sh, sc, gt = vec_ref[0:1, :], vec_ref[1:2, :], vec_ref[2:3, :], vec_ref[3:4, :]
        _, _, _, u = _norm_fwd(hh, g, sh, sc)
        ub = u.astype(BF16)
        u_ref[...] = ub
        acc = None
        for j in range(FF // FC):
            lo, hi = j * FC, (j + 1) * FC
            a = _dot(ub, win_v[:, lo:hi])
            b = _dot(ub, win_v[:, FF + lo:FF + hi])
            a_ref[:, lo:hi] = a.astype(BF16)
            b_ref[:, lo:hi] = b.astype(BF16)
            s = (a * jax.nn.sigmoid(a) * b).astype(BF16)
            part = _dot(s, wout_v[lo:hi, :])
            acc = part if acc is None else acc + part
        f_ref[...] = acc.astype(BF16)
        ho = hh + 0.5 * gt * acc
        if head is None:
            ho_ref[...] = ho
        else:
            _zero_first(lacc_refs[0])
            dh, sq, dg = _loss_head(ho, head_refs[0][...], head_refs[1][0:1, :])
            ho_ref[...] = dh
            lacc_refs[0][0:1, :] += sq
            lacc_refs[0][1:2, :] += dg

    head_specs = [] if head is None else [_rows(TM, D), _const((8, D))]
    lacc_spec = [] if head is None else [_const((8, D))]
    lacc_shape = [] if head is None else [_sds((8, D), F32)]
    return _call(
        body, name, (T // TM,),
        [_rows(TM, D), _const((8, D))] + head_specs + [ANY] * (nwin + 1),
        [_rows(TM, D), _rows(TM, D), _rows(TM, FF), _rows(TM, FF), _rows(TM, D)] + lacc_spec,
        [_sds((T, D), F32), _sds((T, D), BF16), _sds((T, FF), BF16), _sds((T, FF), BF16), _sds((T, D), BF16)] + lacc_shape,
        scratch=[pltpu.VMEM((D, 2 * FF), BF16), pltpu.VMEM((FF, D), BF16), pltpu.SemaphoreType.DMA((nwin + 1,))],
        vmem=VMEM_BIG, comm=comm,
    )(h, vec, *([] if head is None else head), *wins, wout)


def _ffn_ab(h, vec, wins, name, comm=None):
    T = h.shape[0]
    nwin = len(wins)

    def body(h_ref, vec_ref, *rest):
        win_hbms, (u_ref, a_ref, b_ref, win_v, sems) = rest[:nwin], rest[nwin:]
        _load_once(_row_chunks(win_hbms, win_v), sems)
        g, sh, sc = vec_ref[0:1, :], vec_ref[1:2, :], vec_ref[2:3, :]
        _, _, _, u = _norm_fwd(h_ref[...], g, sh, sc)
        ub = u.astype(BF16)
        u_ref[...] = ub
        for j in range(FF // FC):
            lo, hi = j * FC, (j + 1) * FC
            a_ref[:, lo:hi] = _dot(ub, win_v[:, lo:hi]).astype(BF16)
            b_ref[:, lo:hi] = _dot(ub, win_v[:, FF + lo:FF + hi]).astype(BF16)

    return _call(
        body, name, (T // TM,),
        [_rows(TM, D), _const((8, D))] + [ANY] * nwin,
        [_rows(TM, D), _rows(TM, FF), _rows(TM, FF)],
        [_sds((T, D), BF16), _sds((T, FF), BF16), _sds((T, FF), BF16)],
        scratch=[pltpu.VMEM((D, 2 * FF), BF16), pltpu.SemaphoreType.DMA((nwin,))],
        vmem=VMEM_BIG, comm=comm,
    )(h, vec, *wins)


def _ffn_out(h, a, b, vec, wout, name, comm=None):
    T = h.shape[0]

    def body(h_ref, a_ref, b_ref, vec_ref, wout_hbm, ho_ref, f_ref, wout_v, sems):
        _load_once([(wout_hbm, wout_v)], sems)
        gt = vec_ref[3:4, :]
        acc = None
        for j in range(FF // FC):
            lo, hi = j * FC, (j + 1) * FC
            av = a_ref[:, lo:hi].astype(F32)
            s = (av * jax.nn.sigmoid(av) * b_ref[:, lo:hi].astype(F32)).astype(BF16)
            part = _dot(s, wout_v[lo:hi, :])
            acc = part if acc is None else acc + part
        f_ref[...] = acc.astype(BF16)
        ho_ref[...] = h_ref[...] + 0.5 * gt * acc

    return _call(
        body, name, (T // TM,),
        [_rows(TM, D), _rows(TM, FF), _rows(TM, FF), _const((8, D)), ANY],
        [_rows(TM, D), _rows(TM, D)],
        [_sds((T, D), F32), _sds((T, D), BF16)],
        scratch=[pltpu.VMEM((FF, D), BF16), pltpu.SemaphoreType.DMA((1,))],
        vmem=VMEM_BIG, comm=comm,
    )(h, a, b, vec, wout)


def _ffn_bwd(dh, h, a, b, f, vec, wins, wout, name, comm=None):
    T = h.shape[0]
    nwin = len(wins)

    def body(dh_ref, h_ref, a_ref, b_ref, f_ref, vec_ref, *rest):
        win_hbms, (wout_hbm, dhi_ref, dab_ref, s_ref, df_ref, acc_ref, win_v, wout_v, sems) = rest[:nwin], rest[nwin:]
        _load_once(_row_chunks(win_hbms, win_v) + [(wout_hbm, wout_v)], sems)
        _zero_first(acc_ref)
        g, sh, sc, gt = vec_ref[0:1, :], vec_ref[1:2, :], vec_ref[2:3, :], vec_ref[3:4, :]
        dho = dh_ref[...]
        df = (0.5 * gt * dho).astype(BF16)
        df_ref[...] = df
        dgt = _colsum(0.5 * dho * f_ref[...].astype(F32))
        du = None
        for j in range(FF // FC):
            lo, hi = j * FC, (j + 1) * FC
            av = a_ref[:, lo:hi].astype(F32)
            bv = b_ref[:, lo:hi].astype(F32)
            ds = _dot_nt(df, wout_v[lo:hi, :])
            sig = jax.nn.sigmoid(av)
            sa = av * sig
            s_ref[:, lo:hi] = (sa * bv).astype(BF16)
            da = (ds * bv * (sig * (1.0 + av * (1.0 - sig)))).astype(BF16)
            db = (ds * sa).astype(BF16)
            dab_ref[:, lo:hi] = da
            dab_ref[:, FF + lo:FF + hi] = db
            part = _dot_nt(da, win_v[:, lo:hi]) + _dot_nt(db, win_v[:, FF + lo:FF + hi])
            du = part if du is None else du + part
        xh, r, n, _ = _norm_fwd(h_ref[...], g, sh, sc)
        dhn, dsh, dsc, dg = _norm_bwd(du, xh, r, n, g, sc)
        dhi_ref[...] = dho + dhn
        acc_ref[0:1, :] += dsh
        acc_ref[1:2, :] += dsc
        acc_ref[2:3, :] += dg
        acc_ref[3:4, :] += dgt

    return _call(
        body, name, (T // TM,),
        [_rows(TM, D), _rows(TM, D), _rows(TM, FF), _rows(TM, FF), _rows(TM, D), _const((8, D))] + [ANY] * (nwin + 1),
        [_rows(TM, D), _rows(TM, 2 * FF), _rows(TM, FF), _rows(TM, D), _const((8, D))],
        [_sds((T, D), F32), _sds((T, 2 * FF), BF16), _sds((T, FF), BF16), _sds((T, D), BF16), _sds((8, D), F32)],
        scratch=[pltpu.VMEM((D, 2 * FF), BF16), pltpu.VMEM((FF, D), BF16), pltpu.SemaphoreType.DMA((nwin + 1,))],
        vmem=VMEM_BIG, comm=comm,
    )(dh, h, a, b, f, vec, *wins, wout)


def _wgrad(x, y, name, tk, tn, tt, out_dtype=BF16, comm=None):
    T, K = x.shape
    N = y.shape[1]
    nt = T // tt

    def body(x_ref, y_ref, o_ref, acc_ref):
        t = pl.program_id(2)
        part = _dot_tn(x_ref[...], y_ref[...])

        @pl.when(t == 0)
        def _():
            acc_ref[...] = part

        @pl.when(t > 0)
        def _():
            acc_ref[...] += part

        @pl.when(t == nt - 1)
        def _():
            o_ref[...] = acc_ref[...].astype(out_dtype)

    (out,), c_outs = _call(
        body, name, (K // tk, N // tn, nt),
        [pl.BlockSpec((tt, tk), lambda i, j, t: (t, i)), pl.BlockSpec((tt, tn), lambda i, j, t: (t, j))],
        [pl.BlockSpec((tk, tn), lambda i, j, t: (i, j))], [_sds((K, N), out_dtype)],
        scratch=[pltpu.VMEM((tk, tn), F32)], vmem=VMEM_BIG, comm=comm,
    )(x, y)
    return out, c_outs


def _wgrad_scatter(x, y, name, tt, comm=None):
    T, K = x.shape
    n = y.shape[1] // 4
    nt = T // tt
    assert nt >= 2, "a block's hand-over is added one grid step into the next block"
    half = K // 2
    nc = 0 if comm is None else len(comm.inputs)

    def body(chip_ref, x_ref, y_ref, *refs):
        c_ins, refs = refs[:nc], refs[nc:]
        recv_ref, refs = refs[0], refs[1:]
        c_outs, refs = refs[:nc], refs[nc:]
        acc_ref, keep_ref, give_ref, take_ref, local_sem, give_sems, take_sems, send_sems, recv_sems = refs[:9]
        j, t = pl.program_id(0), pl.program_id(1)
        px, py, pc = _place()

        def hand_over(jj):
            return pltpu.make_async_remote_copy(
                src_ref=give_ref.at[jj], dst_ref=take_ref.at[jj], send_sem=give_sems.at[jj], recv_sem=take_sems.at[jj],
                device_id=(px, py, 1 - pc), device_id_type=MESH)

        def send(jj):
            m = (3, 1, 2)[jj]
            return pltpu.make_async_remote_copy(
                src_ref=keep_ref.at[jj], dst_ref=recv_ref.at[m], send_sem=send_sems.at[jj], recv_sem=recv_sems.at[jj],
                device_id=_chip_peer(px, py, pc, m), device_id_type=MESH)

        def add_sibling(jj):
            hand_over(jj).wait_recv()
            keep_ref[jj] = (keep_ref[jj].astype(F32) + take_ref[jj].astype(F32)).astype(BF16)

        if comm is not None:
            @pl.when(jnp.logical_and(j == 0, t == 0))
            def _():
                comm.start(c_ins, c_outs, refs[9:])

        part = _dot_tn(x_ref[...], y_ref[...])

        @pl.when(t == 0)
        def _():
            acc_ref[...] = part

        @pl.when(t > 0)
        def _():
            acc_ref[...] += part

        for jj in range(3):
            @pl.when(jnp.logical_and(j == jj + 1, t == 0))
            def _():
                add_sibling(jj)
                send(jj).start()

        for jj in range(4):
            @pl.when(jnp.logical_and(j == jj, t == nt - 1))
            def _():
                keep_ref[jj] = acc_ref[pl.ds(pl.multiple_of(pc * half, 16), half), :].astype(BF16)
                give_ref[jj] = acc_ref[pl.ds(pl.multiple_of((1 - pc) * half, 16), half), :].astype(BF16)
                hand_over(jj).start()

        @pl.when(jnp.logical_and(j == 3, t == nt - 1))
        def _():
            add_sibling(3)
            own = pltpu.make_async_copy(keep_ref.at[3], recv_ref.at[0], local_sem.at[0])
            own.start()
            for jj in range(3):
                send(jj).wait_recv()
            for jj in range(3):
                send(jj).wait_send()
            for jj in range(4):
                hand_over(jj).wait_send()
            own.wait()
            if comm is not None:
                comm.relay(c_ins, c_outs, refs[9:])
                comm.wait(c_ins, c_outs, refs[9:])

    grid_spec = pltpu.PrefetchScalarGridSpec(
        num_scalar_prefetch=1, grid=(4, nt),
        in_specs=[pl.BlockSpec((tt, K), lambda j, t, chip: (t, 0)),
                  pl.BlockSpec((tt, n), lambda j, t, chip: (t, chip[0] ^ jnp.where(j == 0, 3, jnp.where(j == 3, 0, j))))]
        + [ANY] * nc,
        out_specs=[ANY] * (1 + nc),
        scratch_shapes=[pltpu.VMEM((K, n), F32)] + [pltpu.VMEM((4, half, n), BF16)] * 3
        + [pltpu.SemaphoreType.DMA((1,))] + [pltpu.SemaphoreType.DMA((4,))] * 2 + [pltpu.SemaphoreType.DMA((3,))] * 2
        + ([] if comm is None else list(comm.sem_shapes)))
    px, py, _ = _place()
    res = pl.pallas_call(
        body, name=name, grid_spec=grid_spec,
        out_shape=[_sds((4, half, n), BF16)] + ([] if comm is None else list(comm.out_shapes)),
        compiler_params=pltpu.CompilerParams(dimension_semantics=("arbitrary", "arbitrary"), vmem_limit_bytes=VMEM_BIG),
    )((2 * px + py).astype(jnp.int32).reshape(1), x, y, *([] if comm is None else comm.inputs))
    return res[0], res[1:]


def _swap_halves(t):
    w = t.shape[1]
    lane = lax.broadcasted_iota(jnp.int32, t.shape, 1)
    return jnp.where(lane % HD < HD // 2, pltpu.roll(t, w - HD // 2, 1), pltpu.roll(t, HD // 2, 1))


def _rope(t, cos, sin_signed):
    c = jnp.tile(cos, (1, t.shape[1] // cos.shape[1]))
    s = jnp.tile(sin_signed, (1, t.shape[1] // sin_signed.shape[1]))
    return t * c + _swap_halves(t) * s


def _rope_bwd(dt, cos, sin_signed):
    c = jnp.tile(cos, (1, dt.shape[1] // cos.shape[1]))
    s = jnp.tile(sin_signed, (1, dt.shape[1] // sin_signed.shape[1]))
    return dt * c + _swap_halves(dt * s)


def _rm_spec(dil):
    return pl.BlockSpec((dil, TM // dil, GA), lambda i: (0, i, 0))


def _to_residues(t, dst_ref, scr_ref, dil):
    if dil == 1:
        dst_ref[0] = t.astype(dst_ref.dtype)
        return
    for j in range(GA // LANES):
        scr_ref[j] = t[:, j * LANES:(j + 1) * LANES]
    for r in range(dil):
        for j in range(GA // LANES):
            rows = scr_ref.at[j][pl.ds(r, TM // dil, stride=dil), :]
            dst_ref[r, :, j * LANES:(j + 1) * LANES] = rows.astype(dst_ref.dtype)


def _from_residues(src_ref, scr_ref, dil):
    if dil == 1:
        return src_ref[0].astype(F32)
    for r in range(dil):
        for j in range(GA // LANES):
            scr_ref.at[j][pl.ds(r, TM // dil, stride=dil), :] = src_ref[r, :, j * LANES:(j + 1) * LANES].astype(F32)
    return jnp.concatenate([scr_ref[j] for j in range(GA // LANES)], axis=1)


def _mix_proj(h, vec, win, cos, sin, comm=None):
    T = h.shape[0]

    def body(h_ref, vec_ref, win_hbm, cos_ref, sin_ref, u_ref, p_ref, *rest):
        qkv_refs, gates_ref, win_v, scr_ref, sems = rest[:3 * NG], rest[3 * NG], rest[3 * NG + 1], rest[3 * NG + 2], rest[3 * NG + 3]
        _load_once([(win_hbm, win_v)], sems)
        g, sh, sc = vec_ref[0:1, :], vec_ref[1:2, :], vec_ref[2:3, :]
        _, _, _, u = _norm_fwd(h_ref[...], g, sh, sc)
        ub = u.astype(BF16)
        u_ref[...] = ub
        mixer_cols = PW + 3 * NG * GA
        proj = _dot(ub, win_v[:, 0:mixer_cols])
        p_ref[...] = proj[:, 0:PW]
        cos_t, sin_t = cos_ref[...], sin_ref[...]
        for j in range(3 * NG):
            col = PW + j * GA
            t = proj[:, col:col + GA]
            if j < 2 * NG:
                t = _rope(t, cos_t, sin_t)
            _to_residues(t, qkv_refs[j], scr_ref, DIL[j % NG])
        gates_ref[...] = jax.nn.sigmoid(_dot(ub, win_v[:, mixer_cols:INW])).astype(BF16)

    outs, c_outs = _call(
        body, "mix_proj", (T // TM,),
        [_rows(TM, D), _const((8, D)), ANY, _rows(TM, 128), _rows(TM, 128)],
        [_rows(TM, D), _rows(TM, PW)] + [_rm_spec(d) for d in DIL] * 3 + [_rows(TM, GW)],
        [_sds((T, D), BF16), _sds((T, PW), F32)] + [_sds((d, T // d, GA), BF16) for d in DIL] * 3 + [_sds((T, GW), BF16)],
        scratch=[pltpu.VMEM((D, INW), BF16), pltpu.VMEM((GA // LANES, TM, LANES), F32), pltpu.SemaphoreType.DMA((1,))],
        vmem=VMEM_BIG, comm=comm,
    )(h, vec, win, cos, sin)
    return (outs[0], outs[1], outs[2:2 + NG], outs[2 + NG:2 + 2 * NG], outs[2 + 2 * NG:2 + 3 * NG], outs[2 + 3 * NG]), c_outs


def _head_masks():
    lane_head = lax.broadcasted_iota(jnp.int32, (BLK, GA), 1) // HD
    return [lane_head == hd for hd in range(NH)]


def _expand_heads(t, hm):
    return jnp.concatenate([jnp.where(m, t, jnp.zeros_like(t)) for m in hm], axis=0)


def _collapse_heads(tb, hm):
    out = None
    for hd, m in enumerate(hm):
        part = jnp.where(m, tb[hd * BLK:(hd + 1) * BLK, :], 0.0)
        out = part if out is None else out + part
    return out


def _head_rows(t):
    return jnp.concatenate([t[:, hd * HD:hd * HD + 1] for hd in range(NH)], axis=0)


def _band(has_prev):
    a = lax.broadcasted_iota(jnp.int32, (NH * BLK, 2 * BLK), 0) & (BLK - 1)
    c = lax.broadcasted_iota(jnp.int32, (NH * BLK, 2 * BLK), 1)
    return jnp.logical_and(c >= jnp.where(has_prev, a, BLK), c <= a + BLK)


def _attn_fwd(q, k, v, nb, name, comm=None):
    T = q.shape[0]
    nbt = T // BLK

    def block(qv, kcat, vcat, has_prev, hm):
        s = jnp.where(_band(has_prev), _dot_nt(_expand_heads(qv, hm), kcat) * SCALE, NEG)
        mx = jnp.max(s, axis=-1, keepdims=True)
        e = jnp.exp(s - mx)
        l = jnp.sum(e, axis=-1, keepdims=True)
        ob = _dot((e * (1.0 / l)).astype(BF16), vcat)
        return _collapse_heads(ob, hm), _collapse_heads(jnp.broadcast_to(mx + jnp.log(l), (NH * BLK, GA)), hm)

    def body(q_ref, k_ref, kp_ref, v_ref, vp_ref, o_ref, lse_ref):
        b0 = FWD_BLOCKS * pl.program_id(0)
        hm = _head_masks()
        for b in range(FWD_BLOCKS):
            rows = slice(b * BLK, (b + 1) * BLK)
            if b == 0:
                kcat = jnp.concatenate([kp_ref[...], k_ref[rows, :]], axis=0)
                vcat = jnp.concatenate([vp_ref[...], v_ref[rows, :]], axis=0)
            else:
                kcat, vcat = k_ref[(b - 1) * BLK:(b + 1) * BLK, :], v_ref[(b - 1) * BLK:(b + 1) * BLK, :]
            o_ref[rows, :], lse_ref[rows, :] = block(q_ref[rows, :], kcat, vcat, ((b0 + b) & (nb - 1)) != 0, hm)

    cur = pl.BlockSpec((FWD_BLOCKS * BLK, GA), lambda i: (i, 0))
    prev = pl.BlockSpec((BLK, GA), lambda i: (jnp.maximum(FWD_BLOCKS * i - 1, 0), 0))
    return _call(body, name, (nbt // FWD_BLOCKS,), [cur, cur, prev, cur, prev], [cur, cur],
                 [_sds((T, GA), F32), _sds((T, GA), F32)], comm=comm)(q, k, k, v, v)


def _attn_bwd(q, k, v, do, lse, e, nb, name, comm=None):
    T = q.shape[0]
    nbt = T // BLK

    nblk = BWD_BLOCKS

    def probs_and_ds(qb, dob, kcat, vcat, lsev, ev, valid):
        p = jnp.where(valid, jnp.exp(_dot_nt(qb, kcat) * SCALE - _head_rows(lsev)), 0.0)
        return p.astype(BF16), (p * (_dot_nt(dob, vcat) + _head_rows(ev))).astype(BF16)

    def body(q_ref, k_ref, v_ref, do_ref, lse_ref, e_ref, kp_ref, vp_ref, qn_ref, don_ref, lsen_ref, en_ref,
             dq_ref, dk_ref, dv_ref):
        b0 = nblk * pl.program_id(0)
        hm = _head_masks()
        rows = [slice(b * BLK, (b + 1) * BLK) for b in range(nblk)]
        qs = [_expand_heads(q_ref[r, :], hm) for r in rows] + [_expand_heads(qn_ref[...], hm)]
        dos = [_expand_heads(do_ref[r, :], hm) for r in rows] + [_expand_heads(don_ref[...], hm)]
        ps, dss = [], []
        for b, r in enumerate(rows):
            if b == 0:
                kcat = jnp.concatenate([kp_ref[...], k_ref[r, :]], axis=0)
                vcat = jnp.concatenate([vp_ref[...], v_ref[r, :]], axis=0)
            else:
                kcat, vcat = k_ref[(b - 1) * BLK:(b + 1) * BLK, :], v_ref[(b - 1) * BLK:(b + 1) * BLK, :]
            p, ds = probs_and_ds(qs[b], dos[b], kcat, vcat, lse_ref[r, :], e_ref[r, :], _band(((b0 + b) & (nb - 1)) != 0))
            dq_ref[r, :] = _collapse_heads(_dot(ds, kcat) * SCALE, hm)
            ps.append(p)
            dss.append(ds)
        a = lax.broadcasted_iota(jnp.int32, (NH * BLK, BLK), 0) & (BLK - 1)
        c = lax.broadcasted_iota(jnp.int32, (NH * BLK, BLK), 1)
        valid_n = jnp.logical_and(c >= a, ((b0 + nblk) & (nb - 1)) != 0)
        p_n, ds_n = probs_and_ds(qs[nblk], dos[nblk], k_ref[rows[-1], :], v_ref[rows[-1], :], lsen_ref[...], en_ref[...], valid_n)
        for b, r in enumerate(rows):
            ds_after = dss[b + 1][:, :BLK] if b + 1 < nblk else ds_n
            p_after = ps[b + 1][:, :BLK] if b + 1 < nblk else p_n
            q_pair = jnp.concatenate([qs[b], qs[b + 1]], axis=0)
            do_pair = jnp.concatenate([dos[b], dos[b + 1]], axis=0)
            dk_ref[r, :] = _dot_tn(jnp.concatenate([dss[b][:, BLK:], ds_after], axis=0), q_pair) * SCALE
            dv_ref[r, :] = _dot_tn(jnp.concatenate([ps[b][:, BLK:], p_after], axis=0), do_pair).astype(BF16)

    cur = pl.BlockSpec((nblk * BLK, GA), lambda i: (i, 0))
    prev = pl.BlockSpec((BLK, GA), lambda i: (jnp.maximum(nblk * i - 1, 0), 0))
    nxt = pl.BlockSpec((BLK, GA), lambda i: (jnp.minimum(nblk * i + nblk, nbt - 1), 0))
    return _call(body, name, (nbt // nblk,), [cur] * 6 + [prev, prev] + [nxt] * 4, [cur, cur, cur],
                 [_sds((T, GA), F32), _sds((T, GA), F32), _sds((T, GA), BF16)],
                 comm=comm)(q, k, v, do, lse, e, k, v, q, do, lse, e)


def _flat(t):
    return t.reshape(t.shape[0] * t.shape[1], t.shape[2])


def _by_residue(t, dil):
    return t.reshape(dil, t.shape[0] // dil, t.shape[1])


def _pool_consts(shape, row0):
    lane = lax.broadcasted_iota(jnp.int32, shape, 1)
    t = lax.broadcasted_iota(jnp.int32, shape, 0) + row0
    grp = lane // (PW // len(POOL_WINDOWS))
    win = jnp.where(grp == 0, POOL_WINDOWS[0], jnp.where(grp == 1, POOL_WINDOWS[1],
                    jnp.where(grp == 2, POOL_WINDOWS[2], POOL_WINDOWS[3])))
    cnt = jnp.minimum(t + 1, win).astype(F32)
    return grp, cnt


def _window_sums(ext_ref, base, step, tm):
    outs, run = [], None
    for j in range(POOL_WINDOWS[-1]):
        sl = ext_ref[pl.ds(base + step * j, tm), :]
        run = sl if run is None else run + sl
        if j + 1 in POOL_WINDOWS:
            outs.append(run)
    return outs


def _select_group(grp, vals):
    return jnp.where(grp == 0, vals[0], jnp.where(grp == 1, vals[1], jnp.where(grp == 2, vals[2], vals[3])))


def _pool_d(pc_ref, pp_ref, ext_ref, i, tm):
    ext_ref[0:HALO, :] = jnp.where(i > 0, pp_ref[tm - HALO:tm, :], 0.0)
    ext_ref[HALO:HALO + tm, :] = pc_ref[...]
    grp, cnt = _pool_consts((tm, PW), i * tm)
    sums = _window_sums(ext_ref, HALO, -1, tm)
    return _select_group(grp, sums) / cnt - pc_ref[...]


def _group_weights(ls):
    mx = jnp.maximum(jnp.maximum(ls[0], ls[1]), ls[2])
    es = [jnp.exp(l - mx) for l in ls]
    inv = 1.0 / (es[0] + es[1] + es[2])
    return [e * inv for e in es]


def _mix_merge(h, vec, p, os, lses, gates, wp_bd, pscale, wpb, wab, wout, comm=None):
    T = h.shape[0]

    def body(h_ref, vec_ref, pc_ref, pp_ref, o0, o1, o2, l0, l1, l2, gates_ref, wp_ref, ps_ref, wpb_ref, wab_ref, wout_ref,
             ho_ref, yp_ref, ya_ref, mg_ref, mo_ref, d_ref, ext_ref, scr_ref):
        i = pl.program_id(0)
        gt = vec_ref[3:4, :]
        d = _pool_d(pc_ref, pp_ref, ext_ref, i, TM).astype(BF16)
        d_ref[...] = d
        ypool = (_dot(d, wp_ref[...]) * ps_ref[0:1, :]).astype(BF16)
        yp_ref[...] = ypool
        w = _group_weights([_from_residues(r, scr_ref, dl) for r, dl in zip((l0, l1, l2), DIL)])
        yattn = None
        for wg, o_ref, dl in zip(w, (o0, o1, o2), DIL):
            part = wg * _from_residues(o_ref, scr_ref, dl)
            yattn = part if yattn is None else yattn + part
        yattn = yattn.astype(BF16)
        ya_ref[...] = yattn
        merged = (gates_ref[:, 0:D].astype(F32) * _dot(ypool, wpb_ref[...])
                  + gates_ref[:, D:GW].astype(F32) * _dot(yattn, wab_ref[...])).astype(BF16)
        mg_ref[...] = merged
        mo = _dot(merged, wout_ref[...])
        mo_ref[...] = mo.astype(BF16)
        ho_ref[...] = h_ref[...] + gt * mo

    prev = pl.BlockSpec((TM, PW), lambda i: (jnp.maximum(i - 1, 0), 0))
    return _call(
        body, "mix_merge", (T // TM,),
        [_rows(TM, D), _const((8, D)), _rows(TM, PW), prev] + [_rm_spec(dl) for dl in DIL] * 2 + [_rows(TM, GW), _const((PW, PW)),
         _const((8, PW)), _const((PW, D)), _const((GA, D)), _const((D, D))],
        [_rows(TM, D), _rows(TM, PW), _rows(TM, GA), _rows(TM, D), _rows(TM, D), _rows(TM, PW)],
        [_sds((T, D), F32), _sds((T, PW), BF16), _sds((T, GA), BF16), _sds((T, D), BF16), _sds((T, D), BF16), _sds((T, PW), BF16)],
        scratch=[pltpu.VMEM((TM + HALO, PW), F32), pltpu.VMEM((GA // LANES, TM, LANES), F32)],
        vmem=VMEM_BIG, comm=comm,
    )(h, vec, p, p, *os, *lses, gates, wp_bd, pscale, wpb, wab, wout)


def _mix_bwd_a(dh, vec, mixout, merged, gates, ypool, yattn, dpool, os, lses, wp_bd, pscale, wpb, wab, wout, ones_bd,
               comm=None):
    T = dh.shape[0]
    nt = T // TM

    def body(dh_ref, vec_ref, mo_ref, mg_ref, gates_ref, yp_ref, ya_ref, d_ref, o0, o1, o2, l0, l1, l2,
             wp_ref, ps_ref, wpb_ref, wab_ref, wout_ref, ones_ref,
             dgates_ref, do0, do1, do2, e0, e1, e2, dd_ref, acc_ref, acc2_ref, g_out_ref, g_pb_ref, g_ab_ref, g_pool_ref,
             scr_ref, a_out, a_pb, a_ab, a_pool):
        _zero_first(acc_ref)
        _zero_first(acc2_ref)
        for a_ref in (a_out, a_pb, a_ab, a_pool):
            _zero_first(a_ref)
        gt = vec_ref[3:4, :]
        dho = dh_ref[...]
        acc_ref[3:4, :] += _colsum(dho * mo_ref[...].astype(F32))
        dmo = (gt * dho).astype(BF16)
        a_out[...] += _dot_tn(mg_ref[...], dmo)
        dmerged = _dot_nt(dmo, wout_ref[...])
        gp = gates_ref[:, 0:D].astype(F32)
        ga = gates_ref[:, D:GW].astype(F32)
        bp = _dot(yp_ref[...], wpb_ref[...])
        ba = _dot(ya_ref[...], wab_ref[...])
        dgates_ref[:, 0:D] = (dmerged * bp * gp * (1.0 - gp)).astype(BF16)
        dgates_ref[:, D:GW] = (dmerged * ba * ga * (1.0 - ga)).astype(BF16)
        dbp = (dmerged * gp).astype(BF16)
        dba = (dmerged * ga).astype(BF16)
        a_pb[...] += _dot_tn(yp_ref[...], dbp)
        a_ab[...] += _dot_tn(ya_ref[...], dba)
        dypool = _dot_nt(dbp, wpb_ref[...])
        ypre = _dot(d_ref[...], wp_ref[...])
        acc2_ref[0:1, :] += _colsum(dypool * ypre)
        dyp = (dypool * ps_ref[0:1, :]).astype(BF16)
        a_pool[...] += _dot_tn(d_ref[...], dyp)
        dd_ref[...] = _dot_nt(dyp, wp_ref[...])
        dya = _dot_nt(dba, wab_ref[...])
        w = _group_weights([_from_residues(r, scr_ref, dl) for r, dl in zip((l0, l1, l2), DIL)])
        ya = None
        for wg, o_ref, dl in zip(w, (o0, o1, o2), DIL):
            part = wg * _from_residues(o_ref, scr_ref, dl)
            ya = part if ya is None else ya + part
        prod = dya * ya
        hi = prod.astype(BF16)
        lo = (prod - hi.astype(F32)).astype(BF16)
        tot = _dot(hi, ones_ref[...]) + _dot(lo, ones_ref[...])
        for wg, do_ref, e_ref, dl in zip(w, (do0, do1, do2), (e0, e1, e2), DIL):
            _to_residues(wg * dya, do_ref, scr_ref, dl)
            _to_residues(-wg * tot, e_ref, scr_ref, dl)

        @pl.when(pl.program_id(0) == nt - 1)
        def _():
            g_out_ref[...] = a_out[...].astype(BF16)
            g_pb_ref[...] = a_pb[...].astype(BF16)
            g_ab_ref[...] = a_ab[...].astype(BF16)
            g_pool_ref[...] = a_pool[...]

    return _call(
        body, "mix_bwd_a", (nt,),
        [_rows(TM, D), _const((8, D)), _rows(TM, D), _rows(TM, D), _rows(TM, GW), _rows(TM, PW), _rows(TM, GA), _rows(TM, PW)]
        + [_rm_spec(dl) for dl in DIL] * 2
        + [_const((PW, PW)), _const((8, PW)), _const((PW, D)), _const((GA, D)), _const((D, D)), _const((GA, GA))],
        [_rows(TM, GW)] + [_rm_spec(dl) for dl in DIL] * 2 + [_rows(TM, PW), _const((8, D)), _const((8, PW))]
        + [_const((D, D)), _const((PW, D)), _const((GA, D)), _const((PW, PW))],
        [_sds((T, GW), BF16)] + [_sds((dl, T // dl, GA), BF16) for dl in DIL]
        + [_sds((dl, T // dl, GA), F32) for dl in DIL] + [_sds((T, PW), F32), _sds((8, D), F32), _sds((8, PW), F32)]
        + [_sds((D, D), BF16), _sds((PW, D), BF16), _sds((GA, D), BF16), _sds((PW, PW), F32)],
        scratch=[pltpu.VMEM((GA // LANES, TM, LANES), F32), pltpu.VMEM((D, D), F32), pltpu.VMEM((PW, D), F32),
                 pltpu.VMEM((GA, D), F32), pltpu.VMEM((PW, PW), F32)],
        vmem=VMEM_BIG, comm=comm,
    )(dh, vec, mixout, merged, gates, ypool, yattn, dpool, *os, *lses, wp_bd, pscale, wpb, wab, wout, ones_bd)


def _mix_bwd_b(dh, h, vec, dd, dqs, dks, dvs, dgates, cos, sin, win):
    T = h.shape[0]
    nt = T // TM

    def body(dh_ref, h_ref, vec_ref, ddc_ref, ddn_ref, *rest):
        qk_refs, dv_refs = rest[:2 * NG], rest[2 * NG:3 * NG]
        dgates_ref, cos_ref, sin_ref, win_hbm, dhi_ref, dproj_ref, acc_ref, win_v, ext_ref, scr_ref, sems = rest[3 * NG:]
        i = pl.program_id(0)
        _load_once([(win_hbm, win_v)], sems)
        _zero_first(acc_ref)
        g, sh, sc = vec_ref[0:1, :], vec_ref[1:2, :], vec_ref[2:3, :]
        grp, cnt = _pool_consts((TM, PW), i * TM)
        _, cnt_n = _pool_consts((HALO, PW), (i + 1) * TM)
        ext_ref[0:TM, :] = ddc_ref[...] / cnt
        ext_ref[TM:TM + HALO, :] = jnp.where(i < nt - 1, ddn_ref[0:HALO, :] / cnt_n, 0.0)
        dp = _select_group(grp, _window_sums(ext_ref, 0, 1, TM)) - ddc_ref[...]
        dproj_ref[:, 0:PW] = dp.astype(BF16)
        cos_t, sin_t = cos_ref[...], sin_ref[...]
        for j in range(2 * NG):
            col = PW + j * GA
            dt = _from_residues(qk_refs[j], scr_ref, DIL[j % NG])
            dproj_ref[:, col:col + GA] = _rope_bwd(dt, cos_t, sin_t).astype(BF16)
        for j in range(NG):
            col = PW + (2 * NG + j) * GA
            dproj_ref[:, col:col + GA] = _from_residues(dv_refs[j], scr_ref, DIL[j]).astype(BF16)
        dproj_ref[:, PW + 3 * NG * GA:INW] = dgates_ref[...]
        du = None
        for j in range(INW // 512):
            part = _dot_nt(dproj_ref[:, j * 512:(j + 1) * 512], win_v[:, j * 512:(j + 1) * 512])
            du = part if du is None else du + part
        xh, r, n, _ = _norm_fwd(h_ref[...], g, sh, sc)
        dhn, dsh, dsc, dg = _norm_bwd(du, xh, r, n, g, sc)
        dhi_ref[...] = dh_ref[...] + dhn
        acc_ref[0:1, :] += dsh
        acc_ref[1:2, :] += dsc
        acc_ref[2:3, :] += dg

    nxt = pl.BlockSpec((TM, PW), lambda i: (jnp.minimum(i + 1, nt - 1), 0))
    return _call(
        body, "mix_bwd_b", (nt,),
        [_rows(TM, D), _rows(TM, D), _const((8, D)), _rows(TM, PW), nxt] + [_rm_spec(dl) for dl in DIL] * 3
        + [_rows(TM, GW), _rows(TM, 128), _rows(TM, 128), ANY],
        [_rows(TM, D), _rows(TM, INW), _const((8, D))],
        [_sds((T, D), F32), _sds((T, INW), BF16), _sds((8, D), F32)],
        scratch=[pltpu.VMEM((D, INW), BF16), pltpu.VMEM((TM + HALO, PW), F32), pltpu.VMEM((GA // LANES, TM, LANES), F32),
                 pltpu.SemaphoreType.DMA((1,))],
        vmem=VMEM_BIG,
    )(dh, h, vec, dd, dd, *dqs, *dks, *dvs, dgates, cos, sin, win)[0]


def _ada_fwd(c_all, w_shard, b_shard):
    n = w_shard.shape[1]

    def body(c_ref, w_ref, b_ref, o_ref):
        cv = c_ref[...]
        cond = (cv * jax.nn.sigmoid(cv)).astype(BF16)
        o_ref[...] = _dot(cond, w_ref[...].astype(BF16)) + b_ref[...]

    tn = n // 3
    return pl.pallas_call(
        body, name="ada_fwd", grid=(3,),
        in_specs=[pl.BlockSpec((8, D), lambda j: (0, 0)), pl.BlockSpec((D, tn), lambda j: (0, j)), pl.BlockSpec((1, tn), lambda j: (0, j))],
        out_specs=pl.BlockSpec((8, tn), lambda j: (0, j)), out_shape=_sds((8, n), F32),
        compiler_params=pltpu.CompilerParams(dimension_semantics=("arbitrary",)),
    )(c_all, w_shard, b_shard)


def _ada_bwd(c_all, dmod_shard, comm=None):
    n = dmod_shard.shape[1]

    def body(c_ref, d_ref, o_ref):
        cv = c_ref[...]
        cond = (cv * jax.nn.sigmoid(cv)).astype(BF16)
        o_ref[...] = _dot_tn(cond, d_ref[...].astype(BF16))

    tn = n // 3
    return _call(
        body, "ada_bwd", (3,),
        [pl.BlockSpec((8, D), lambda j: (0, 0)), pl.BlockSpec((8, tn), lambda j: (0, j))],
        [pl.BlockSpec((D, tn), lambda j: (0, j))], [_sds((D, n), F32)], comm=comm,
    )(c_all, dmod_shard)


def _adam_math(w, g, m, v):
    m2 = B1 * m + (1.0 - B1) * g
    v2 = B2 * v + (1.0 - B2) * (g * g)
    m_hat = m2 / (1.0 - B1 ** STEP)
    v_hat = v2 / (1.0 - B2 ** STEP)
    delta = -LR * (m_hat / (jnp.sqrt(v_hat) + AEPS) + WD * w)
    return delta, m2, v2


def _adam(w, m, v, parts, name, comm=None):
    R, C = w.shape
    tr = R
    for cand in (128, 64, 32, 16, 8):
        if R % cand == 0:
            tr = cand
            break
    np_ = len(parts)

    def body(w_ref, m_ref, v_ref, *rest):
        p_refs, (g_ref, d_ref, m2_ref, v2_ref) = rest[:np_], rest[np_:]
        g = p_refs[0][...]
        for pr in p_refs[1:]:
            g = g + pr[...]
        delta, m2, v2 = _adam_math(w_ref[...], g, m_ref[...], v_ref[...])
        g_ref[...] = g
        d_ref[...] = delta
        m2_ref[...] = m2
        v2_ref[...] = v2

    spec = pl.BlockSpec((tr, C), lambda i: (i, 0))
    return _call(body, name, (R // tr,), [spec] * (3 + np_), [spec] * 4, [_sds((R, C), F32)] * 4,
                 vmem=VMEM_BIG, comm=comm)(w, m, v, *parts)


def _adam_halves(w, m, v, mine, other, name):
    R, C = w.shape
    tr = 128
    nh = R // 2 // tr

    def body(c_ref, w_ref, m_ref, v_ref, mine_ref, other_ref, g_ref, d_ref, m2_ref, v2_ref):
        i = pl.program_id(0)
        in_mine = jnp.logical_and(i >= c_ref[0] * nh, i < (c_ref[0] + 1) * nh)
        g = jnp.where(in_mine, mine_ref[...], other_ref[...])
        delta, m2, v2 = _adam_math(w_ref[...], g, m_ref[...], v_ref[...])
        g_ref[...] = g
        d_ref[...] = delta
        m2_ref[...] = m2
        v2_ref[...] = v2

    spec = pl.BlockSpec((tr, C), lambda i, c: (i, 0))
    grid_spec = pltpu.PrefetchScalarGridSpec(
        num_scalar_prefetch=1, grid=(R // tr,),
        in_specs=[spec] * 3 + [pl.BlockSpec((tr, C), lambda i, c: (jnp.clip(i - c[0] * nh, 0, nh - 1), 0)),
                               pl.BlockSpec((tr, C), lambda i, c: (jnp.clip(i - (1 - c[0]) * nh, 0, nh - 1), 0))],
        out_specs=[spec] * 4)
    return pl.pallas_call(
        body, name=name, grid_spec=grid_spec, out_shape=[_sds((R, C), F32)] * 4,
        compiler_params=pltpu.CompilerParams(dimension_semantics=("arbitrary",), vmem_limit_bytes=VMEM_BIG),
    )(lax.axis_index("c").astype(jnp.int32).reshape(1), w, m, v, mine, other)


def _adam_small(ws, ms, vs, gathered):
    n = len(ws)
    sizes = [a.shape[1] for a in ws]

    def total(ga_ref, off, size):
        g = ga_ref[0, :, off:off + size]
        for dev in range(1, 8):
            g = g + ga_ref[dev, :, off:off + size]
        return g

    def body(*refs):
        w_refs, m_refs, v_refs, ga_ref, outs = refs[:n], refs[n:2 * n], refs[2 * n:3 * n], refs[3 * n], refs[3 * n + 1:]
        off = 0
        for j, size in enumerate(sizes):
            g = total(ga_ref, off, size)
            delta, m2, v2 = _adam_math(w_refs[j][...], g, m_refs[j][...], v_refs[j][...])
            for ref, val in zip(outs[4 * j:4 * j + 4], (g, delta, m2, v2)):
                ref[...] = val
            off += size
        outs[4 * n][...] = total(ga_ref, off, 128)

    res = pl.pallas_call(
        body, name="adam_small",
        out_shape=[_sds((1, size), F32) for size in sizes for _ in range(4)] + [_sds((1, 128), F32)],
    )(*ws, *ms, *vs, gathered)
    return [res[4 * j:4 * j + 4] for j in range(n)], res[4 * n]


def _sum4(blocks, name):
    _, R, C = blocks.shape
    tr = R
    for cand in (256, 128, 64, 32, 16):
        if R % cand == 0:
            tr = cand
            break

    def body(r_ref, out_ref):
        out_ref[...] = ((r_ref[0].astype(F32) + r_ref[1].astype(F32)) + r_ref[2].astype(F32)) + r_ref[3].astype(F32)

    return pl.pallas_call(
        body, name=name, grid=(R // tr,),
        in_specs=[pl.BlockSpec((4, tr, C), lambda i: (0, i, 0))],
        out_specs=pl.BlockSpec((tr, C), lambda i: (i, 0)), out_shape=_sds((R, C), F32),
        compiler_params=pltpu.CompilerParams(dimension_semantics=("arbitrary",)),
    )(blocks)


def _place():
    return lax.axis_index("x"), lax.axis_index("y"), lax.axis_index("c")


def _chip_peer(x, y, c, m):
    return (x ^ (m >> 1), y ^ (m & 1), c)


def _shard_ref(ref, axis, k, n):
    start = pl.multiple_of(k * n, 128 if axis == 1 else 16)
    return ref.at[:, pl.ds(start, n)] if axis == 1 else ref.at[pl.ds(start, n), :]


def _half_rows(ref, axis, k, n, hc):
    if axis == 1:
        half = ref.shape[0] // 2
        return ref.at[pl.ds(pl.multiple_of(hc * half, 16), half), pl.ds(pl.multiple_of(k * n, 128), n)]
    half = n // 2
    return ref.at[pl.ds(pl.multiple_of(k * n + hc * half, 16), half), :]


class _GatherPlan:
    def __init__(self, shards, axes):
        self.inputs, self.axes, nw = list(shards), list(axes), len(shards)
        self.out_shapes = [_sds((s.shape[0] * (4 if ax == 0 else 1), s.shape[1] * (4 if ax == 1 else 1)), BF16)
                           for s, ax in zip(shards, axes)]
        self.sem_shapes = [pltpu.SemaphoreType.DMA((nw,))] + [pltpu.SemaphoreType.DMA((nw, 3))] * 4

    def _copies(self, ins, outs, sems):
        local_sems, send_sems, recv_sems, pass_sems, got_sems = sems
        x, y, c = _place()
        k = 2 * x + y
        local, sends, arrivals, passes, handed = [], [], [], [], []
        for j, ax in enumerate(self.axes):
            n = ins[j].shape[ax]
            half = ins[j].shape[0] // 2
            local.append(pltpu.make_async_copy(ins[j], _shard_ref(outs[j], ax, k, n), local_sems.at[j]))
            my_half = ins[j].at[pl.ds(pl.multiple_of(c * half, 16), half), :]
            for m in range(1, 4):
                sends.append(pltpu.make_async_remote_copy(
                    src_ref=my_half, dst_ref=_half_rows(outs[j], ax, k, n, c), send_sem=send_sems.at[j, m - 1],
                    recv_sem=recv_sems.at[j, m - 1], device_id=_chip_peer(x, y, c, m), device_id_type=MESH))
                theirs = _half_rows(outs[j], ax, k ^ m, n, c)
                arrivals.append(pltpu.make_async_remote_copy(
                    src_ref=my_half, dst_ref=theirs, send_sem=send_sems.at[j, m - 1], recv_sem=recv_sems.at[j, m - 1],
                    device_id=(x, y, c), device_id_type=MESH))
                passes.append(pltpu.make_async_remote_copy(
                    src_ref=theirs, dst_ref=theirs, send_sem=pass_sems.at[j, m - 1], recv_sem=got_sems.at[j, m - 1],
                    device_id=(x, y, 1 - c), device_id_type=MESH))
                other = _half_rows(outs[j], ax, k ^ m, n, 1 - c)
                handed.append(pltpu.make_async_remote_copy(
                    src_ref=other, dst_ref=other, send_sem=pass_sems.at[j, m - 1], recv_sem=got_sems.at[j, m - 1],
                    device_id=(x, y, c), device_id_type=MESH))
        return local, sends, arrivals, passes, handed

    def start(self, ins, outs, sems):
        local, sends, _, _, _ = self._copies(ins, outs, sems)
        for cp in local + sends:
            cp.start()

    def relay(self, ins, outs, sems):
        _, _, arrivals, passes, _ = self._copies(ins, outs, sems)
        for arrived, onward in zip(arrivals, passes):
            arrived.wait_recv()
            onward.start()

    def wait(self, ins, outs, sems):
        local, sends, _, passes, handed = self._copies(ins, outs, sems)
        for cp in handed:
            cp.wait_recv()
        for cp in sends + passes:
            cp.wait_send()
        for cp in local:
            cp.wait()


class _ScatterPlan:
    def __init__(self, grads, axes):
        self.inputs, self.axes, nw = list(grads), list(axes), len(grads)
        self.shard_shapes = [(g.shape[0] // (4 if ax == 0 else 1), g.shape[1] // (4 if ax == 1 else 1))
                             for g, ax in zip(grads, axes)]
        self.out_shapes = [_sds((4,) + s, BF16) for s in self.shard_shapes]
        self.sem_shapes = [pltpu.SemaphoreType.DMA((nw,)), pltpu.SemaphoreType.DMA((nw, 3)), pltpu.SemaphoreType.DMA((nw, 3))]

    def _copies(self, ins, outs, sems):
        local_sems, send_sems, recv_sems = sems
        x, y, c = _place()
        k = 2 * x + y
        local, remote, arrivals = [], [], []
        for j, ax in enumerate(self.axes):
            n = self.shard_shapes[j][ax]
            local.append(pltpu.make_async_copy(_shard_ref(ins[j], ax, k, n), outs[j].at[0], local_sems.at[j]))
            for m in range(1, 4):
                remote.append(pltpu.make_async_remote_copy(
                    src_ref=_shard_ref(ins[j], ax, k ^ m, n), dst_ref=outs[j].at[m],
                    send_sem=send_sems.at[j, m - 1], recv_sem=recv_sems.at[j, m - 1],
                    device_id=_chip_peer(x, y, c, m), device_id_type=MESH))
                arrivals.append(pltpu.make_async_remote_copy(
                    src_ref=_shard_ref(ins[j], ax, k, n), dst_ref=outs[j].at[m],
                    send_sem=send_sems.at[j, m - 1], recv_sem=recv_sems.at[j, m - 1],
                    device_id=(x, y, c), device_id_type=MESH))
        return local, remote, arrivals

    def start(self, ins, outs, sems):
        local, remote, _ = self._copies(ins, outs, sems)
        for cp in local + remote:
            cp.start()

    def relay(self, ins, outs, sems):
        pass

    def wait(self, ins, outs, sems):
        local, remote, arrivals = self._copies(ins, outs, sems)
        for cp in arrivals:
            cp.wait_recv()
        for cp in remote:
            cp.wait_send()
        for cp in local:
            cp.wait()


def _run_plan(plan, name):
    nc = len(plan.inputs)

    def body(*refs):
        ins, outs, sems = refs[:nc], refs[nc:2 * nc], refs[2 * nc:]
        plan.start(ins, outs, sems)
        plan.relay(ins, outs, sems)
        plan.wait(ins, outs, sems)

    return pl.pallas_call(body, name=name, in_specs=[ANY] * nc, out_specs=[ANY] * nc, out_shape=list(plan.out_shapes),
                          scratch_shapes=list(plan.sem_shapes))(*plan.inputs)


class _SwapPlan:
    def __init__(self, parts):
        self.inputs, nw = list(parts), len(parts)
        self.out_shapes = [_sds(p.shape, p.dtype) for p in parts]
        self.sem_shapes = [pltpu.SemaphoreType.DMA((nw,)), pltpu.SemaphoreType.DMA((nw,))]

    def _copies(self, ins, outs, sems):
        send_sems, recv_sems = sems
        x, y, c = _place()
        return [pltpu.make_async_remote_copy(
            src_ref=ins[j], dst_ref=outs[j], send_sem=send_sems.at[j], recv_sem=recv_sems.at[j],
            device_id=(x, y, 1 - c), device_id_type=MESH) for j in range(len(ins))]

    def start(self, ins, outs, sems):
        for cp in self._copies(ins, outs, sems):
            cp.start()

    def relay(self, ins, outs, sems):
        pass

    def wait(self, ins, outs, sems):
        for cp in self._copies(ins, outs, sems):
            cp.wait()


class _SmallGatherPlan:
    def __init__(self, v):
        self.inputs = [v]
        self.out_shapes = [_sds((8,) + v.shape, v.dtype)]
        self.sem_shapes = [pltpu.SemaphoreType.DMA((1,)), pltpu.SemaphoreType.DMA((7,)), pltpu.SemaphoreType.DMA((7,))]

    def _copies(self, ins, outs, sems):
        (v_ref,), (out_ref,), (local_sem, send_sems, recv_sems) = ins, outs, sems
        x, y, c = _place()
        me = 4 * x + 2 * y + c
        local = pltpu.make_async_copy(v_ref, out_ref.at[me], local_sem.at[0])
        sends, arrivals = [], []
        for m in range(1, 8):
            px, py, pc = x ^ (m >> 2), y ^ ((m >> 1) & 1), c ^ (m & 1)
            sends.append(pltpu.make_async_remote_copy(
                src_ref=v_ref, dst_ref=out_ref.at[me], send_sem=send_sems.at[m - 1], recv_sem=recv_sems.at[m - 1],
                device_id=(px, py, pc), device_id_type=MESH))
            arrivals.append(pltpu.make_async_remote_copy(
                src_ref=v_ref, dst_ref=out_ref.at[4 * px + 2 * py + pc], send_sem=send_sems.at[m - 1],
                recv_sem=recv_sems.at[m - 1], device_id=(x, y, c), device_id_type=MESH))
        return local, sends, arrivals

    def start(self, ins, outs, sems):
        local, sends, _ = self._copies(ins, outs, sems)
        for cp in [local] + sends:
            cp.start()

    def relay(self, ins, outs, sems):
        pass

    def wait(self, ins, outs, sems):
        local, sends, arrivals = self._copies(ins, outs, sems)
        for cp in arrivals:
            cp.wait_recv()
        for cp in sends:
            cp.wait_send()
        local.wait()


class _PlanGroup:
    def __init__(self, plans):
        self.plans = [p for p in plans if p is not None]
        self.inputs = [a for p in self.plans for a in p.inputs]
        self.out_shapes = [s for p in self.plans for s in p.out_shapes]
        self.sem_shapes = [s for p in self.plans for s in p.sem_shapes]

    def _each(self, ins, outs, sems):
        i = s = 0
        for p in self.plans:
            n, ns = len(p.inputs), len(p.sem_shapes)
            yield p, ins[i:i + n], outs[i:i + n], sems[s:s + ns]
            i, s = i + n, s + ns

    def start(self, ins, outs, sems):
        for p, pi, po, ps in self._each(ins, outs, sems):
            p.start(pi, po, ps)

    def relay(self, ins, outs, sems):
        for p, pi, po, ps in self._each(ins, outs, sems):
            p.relay(pi, po, ps)

    def wait(self, ins, outs, sems):
        for p, pi, po, ps in self._each(ins, outs, sems):
            p.wait(pi, po, ps)

    def split(self, outs):
        res, i = [], 0
        for p in self.plans:
            res.append(outs[i:i + len(p.inputs)])
            i += len(p.inputs)
        return res


BIG = ("w_ffn1_in", "w_ffn1_out", "w_in", "w_pool_branch", "w_attn_branch", "w_out", "w_ffn2_in", "w_ffn2_out")
BIG_AXIS = {"w_ffn1_in": 1, "w_ffn1_out": 0, "w_in": 1, "w_pool_branch": 1, "w_attn_branch": 1, "w_out": 0,
            "w_ffn2_in": 1, "w_ffn2_out": 0}


class _Sharded:
    fused_scatter = True

    def __init__(self, shards):
        self.shards, self.full, self.recv = shards, {}, {}

    def gather_plan(self, names):
        return _GatherPlan([self.shards[n] for n in names], [BIG_AXIS[n.split("/")[0]] for n in names])

    def gather_now(self, names):
        self.gathered(names, _run_plan(self.gather_plan(names), "gather_" + names[0]))

    def gathered(self, names, outs):
        self.full.update(zip(names, outs))

    def scatter_plan(self, names, grads):
        return _ScatterPlan([grads[n] for n in names], [BIG_AXIS[n] for n in names])

    def scatter_now(self, names, grads):
        self.scattered(names, _run_plan(self.scatter_plan(names, grads), "scatter_" + names[0]))

    def scattered(self, names, outs):
        self.recv.update(zip(names, outs))


class _Whole:
    fused_scatter = False

    def __init__(self, full):
        self.full, self.recv = dict(full), {}

    def gather_plan(self, names):
        return None

    def gather_now(self, names):
        pass

    def gathered(self, names, outs):
        pass

    def scatter_plan(self, names, grads):
        return None

    def scatter_now(self, names, grads):
        pass

    def scattered(self, names, outs):
        pass


def _vec(rows):
    pad = [jnp.zeros((1, D), F32)] * (8 - len(rows))
    return jnp.concatenate([r.reshape(1, D) for r in rows] + pad, axis=0)


def _block_diag(w_pool):
    n, c = w_pool.shape[0], w_pool.shape[1]
    eye = jnp.eye(n, dtype=w_pool.dtype)
    return (eye[:, None, :, None] * w_pool[:, :, None, :]).reshape(n * c, n * c)


def _example_step(x, tgt, positions, mod, gains, w_pool, pool_scale, ws, pack=None):
    T = x.shape[0]
    assert (T // BLK // DIL[-1]) & (T // BLK // DIL[-1] - 1) == 0, "blocks per sequence must be a power of two"
    sh1, sc1, gt1, sh2, sc2, gt2, sh3, sc3, gt3 = [mod[j * D:(j + 1) * D] for j in range(NMOD)]
    g1, g2, g3, gf = gains
    vec1, vec2, vec3 = _vec([g1, sh1, sc1, gt1]), _vec([g2, sh2, sc2, gt2]), _vec([g3, sh3, sc3, gt3])
    inv_freq = 10000.0 ** (-jnp.arange(0, HD, 2, dtype=F32) / HD)
    ang = positions.astype(F32)[:, None] * inv_freq
    cos = jnp.tile(jnp.cos(ang), (1, 4))
    sin = jnp.tile(jnp.concatenate([-jnp.sin(ang), jnp.sin(ang)], axis=1), (1, 2))
    wp_bd = _block_diag(w_pool).astype(BF16)
    ones_bd = _block_diag(jnp.ones((NH, HD, HD), F32)).astype(BF16)
    ps = jnp.concatenate([pool_scale.reshape(1, PW), jnp.zeros((7, PW), F32)], axis=0)
    wb = ws.full

    if "w_ffn1_in" not in wb:
        ws.gather_now(["w_ffn1_in"])
    (u1, a1, b1), got = _ffn_ab(x, vec1, [wb["w_ffn1_in"]], "ffn1_ab", ws.gather_plan(["w_ffn1_out", "w_in"]))
    ws.gathered(["w_ffn1_out", "w_in"], got)
    mixw = ["w_pool_branch", "w_attn_branch", "w_out", "w_ffn2_in/1"]
    (h1, f1), got = _ffn_out(x, a1, b1, vec1, wb["w_ffn1_out"], "ffn1_out", ws.gather_plan(mixw))
    ws.gathered(mixw, got)
    (u2, p, qs, ks, vs, gates), got = _mix_proj(h1, vec2, wb["w_in"], cos, sin, ws.gather_plan(["w_ffn2_in/0", "w_ffn2_out"]))
    ws.gathered(["w_ffn2_in/0", "w_ffn2_out"], got)
    qs, ks, vs = [_flat(t) for t in qs], [_flat(t) for t in ks], [_flat(t) for t in vs]
    nbs = [T // d // BLK for d in DIL]
    os, lses = [], []
    for gi in range(NG):
        (o, lse), _ = _attn_fwd(qs[gi], ks[gi], vs[gi], nbs[gi], f"attn_fwd{gi}")
        os.append(o)
        lses.append(lse)
    os_r = [_by_residue(t, d) for t, d in zip(os, DIL)]
    lses_r = [_by_residue(t, d) for t, d in zip(lses, DIL)]
    (h2, ypool, yattn, merged, mixout, dpool), _ = _mix_merge(
        h1, vec2, p, os_r, lses_r, gates, wp_bd, ps, wb["w_pool_branch"], wb["w_attn_branch"], wb["w_out"])
    win3 = [wb["w_ffn2_in/0"], wb["w_ffn2_in/1"]] if "w_ffn2_in/0" in wb else [wb["w_ffn2_in"]]
    (dh3, u3, a3, b3, f3, lacc), _ = _ffn_fwd(h2, vec3, win3, wb["w_ffn2_out"], "ffn2_fwd", head=(tgt, _vec([gf])))
    loss = 0.5 * jnp.sum(lacc[0]) / D

    grads = {}

    def wgrad_cols(name, xx, yy, riders, extra=None):
        group = _PlanGroup([ws.scatter_plan(riders, grads) if riders else None, extra])
        plan = group if group.plans else None
        if ws.fused_scatter:
            blocks, got = _wgrad_scatter(xx, yy, "wg_" + name, min(2048, T // 2), comm=plan)
            ws.scattered([name], [blocks])
        else:
            grads[name], got = _wgrad(xx, yy, "wg_" + name, D, 512, 1024, comm=plan)
        parts = group.split(got)
        if len(parts) > (extra is not None):
            ws.scattered(riders, parts[0])
        return parts[-1] if extra is not None else None

    (dh2, dab3, s3, df3, acc3), _ = _ffn_bwd(dh3, h2, a3, b3, f3, vec3, win3, wb["w_ffn2_out"], "ffn2_bwd")
    grads["w_ffn2_out"], _ = _wgrad(s3, df3, "wg_ffn2_out", FF // 2, 512, min(4096, T // 2))
    wgrad_cols("w_ffn2_in", u3, dab3, ["w_ffn2_out"])
    (dgates, do0, do1, do2, e0, e1, e2, dd, acc2a, accps,
     grads["w_out"], grads["w_pool_branch"], grads["w_attn_branch"], gwp), _ = _mix_bwd_a(
        dh2, vec2, mixout, merged, gates, ypool, yattn, dpool, os_r, lses_r, wp_bd, ps,
        wb["w_pool_branch"], wb["w_attn_branch"], wb["w_out"], ones_bd)
    n = len(POOL_WINDOWS)
    c = PW // n
    grad_w_pool = jnp.stack([gwp[j * c:(j + 1) * c, j * c:(j + 1) * c] for j in range(n)], axis=0)
    small3 = ["w_out", "w_pool_branch", "w_attn_branch"]
    dqs, dks, dvs = [], [], []
    for gi, (do, e) in enumerate(((do0, e0), (do1, e1), (do2, e2))):
        (dq, dk, dv), _ = _attn_bwd(qs[gi], ks[gi], vs[gi], _flat(do), lses[gi], _flat(e), nbs[gi], f"attn_bwd{gi}")
        dqs.append(_by_residue(dq, DIL[gi]))
        dks.append(_by_residue(dk, DIL[gi]))
        dvs.append(_by_residue(dv, DIL[gi]))
    dh1, dproj, acc2b = _mix_bwd_b(dh2, h1, vec2, dd, dqs, dks, dvs, dgates, cos, sin, wb["w_in"])
    wgrad_cols("w_in", u2, dproj, small3)
    (dx, dab1, s1, df1, acc1), _ = _ffn_bwd(dh1, x, a1, b1, f1, vec1, [wb["w_ffn1_in"]], wb["w_ffn1_out"], "ffn1_bwd")
    grads["w_ffn1_out"], _ = _wgrad(s1, df1, "wg_ffn1_out", FF // 2, 512, min(4096, T // 2))
    dmod = jnp.concatenate([acc1[0], acc1[1], acc1[3], acc2b[0], acc2b[1], acc2a[3], acc3[0], acc3[1], acc3[3]])
    dgains = jnp.stack([acc1[2], acc2b[2], acc3[2], lacc[1]], axis=0)
    row = None if pack is None else _SmallGatherPlan(pack(loss, dmod, dgains, grad_w_pool, accps[0]))
    early = [n for n in BIG if n in ws.recv] if pack is not None else []
    ws.sums = {n: _sum4(ws.recv[n], "sum_" + n) for n in early}
    ws.other = {}
    tail = _PlanGroup([row, _SwapPlan([ws.sums[n] for n in early]) if early else None])
    got = wgrad_cols("w_ffn1_in", u1, dab1, ["w_ffn1_out"], tail if tail.plans else None)
    gathered = None
    if got is not None:
        parts = tail.split(got)
        gathered = parts[0][0]
        ws.other = dict(zip(early, parts[1])) if early else {}
    return loss, dx, dmod, dgains, grad_w_pool, accps[0], grads, gathered


SMALL = ("b_ada", "g_norm_ffn1", "g_norm_mix", "g_norm_ffn2", "g_final", "pool_scale", "w_pool")
WEIGHTS = ("w_ada", "b_ada", "g_norm_ffn1", "w_ffn1_in", "w_ffn1_out", "g_norm_mix", "w_in", "w_pool", "pool_scale",
           "w_pool_branch", "w_attn_branch", "w_out", "g_norm_ffn2", "w_ffn2_in", "w_ffn2_out", "g_final")


def _pack_small(t):
    return jnp.concatenate([t[n].reshape(-1) for n in SMALL]).reshape(1, -1)


def kernel(x, c, positions, w_ada, b_ada, g_norm_ffn1, w_ffn1_in, w_ffn1_out, g_norm_mix, w_in, w_pool, pool_scale, w_pool_branch, w_attn_branch, w_out, g_norm_ffn2, w_ffn2_in, w_ffn2_out, g_final, loss_target, m_w_ada, m_b_ada, m_g_norm_ffn1, m_w_ffn1_in, m_w_ffn1_out, m_g_norm_mix, m_w_in, m_w_pool, m_pool_scale, m_w_pool_branch, m_w_attn_branch, m_w_out, m_g_norm_ffn2, m_w_ffn2_in, m_w_ffn2_out, m_g_final, v_w_ada, v_b_ada, v_g_norm_ffn1, v_w_ffn1_in, v_w_ffn1_out, v_g_norm_mix, v_w_in, v_w_pool, v_pool_scale, v_w_pool_branch, v_w_attn_branch, v_w_out, v_g_norm_ffn2, v_w_ffn2_in, v_w_ffn2_out, v_g_final):
    w = dict(w_ada=w_ada, b_ada=b_ada, g_norm_ffn1=g_norm_ffn1, w_ffn1_in=w_ffn1_in, w_ffn1_out=w_ffn1_out,
             g_norm_mix=g_norm_mix, w_in=w_in, w_pool=w_pool, pool_scale=pool_scale, w_pool_branch=w_pool_branch,
             w_attn_branch=w_attn_branch, w_out=w_out, g_norm_ffn2=g_norm_ffn2, w_ffn2_in=w_ffn2_in,
             w_ffn2_out=w_ffn2_out, g_final=g_final)
    mom = dict(w_ada=m_w_ada, b_ada=m_b_ada, g_norm_ffn1=m_g_norm_ffn1, w_ffn1_in=m_w_ffn1_in, w_ffn1_out=m_w_ffn1_out,
               g_norm_mix=m_g_norm_mix, w_in=m_w_in, w_pool=m_w_pool, pool_scale=m_pool_scale,
               w_pool_branch=m_w_pool_branch, w_attn_branch=m_w_attn_branch, w_out=m_w_out, g_norm_ffn2=m_g_norm_ffn2,
               w_ffn2_in=m_w_ffn2_in, w_ffn2_out=m_w_ffn2_out, g_final=m_g_final)
    var = dict(w_ada=v_w_ada, b_ada=v_b_ada, g_norm_ffn1=v_g_norm_ffn1, w_ffn1_in=v_w_ffn1_in, w_ffn1_out=v_w_ffn1_out,
               g_norm_mix=v_g_norm_mix, w_in=v_w_in, w_pool=v_w_pool, pool_scale=v_pool_scale,
               w_pool_branch=v_w_pool_branch, w_attn_branch=v_w_attn_branch, w_out=v_w_out, g_norm_ffn2=v_g_norm_ffn2,
               w_ffn2_in=v_w_ffn2_in, w_ffn2_out=v_w_ffn2_out, g_final=v_g_final)
    ix, iy, ic = _place()
    chip = 2 * ix + iy
    me = 4 * ix + 2 * iy + ic
    nada = w_ada.shape[2]

    shards = {n: w[n][0].astype(BF16) for n in BIG}
    half = D // 2
    shards["w_ffn2_in/0"], shards["w_ffn2_in/1"] = shards["w_ffn2_in"][:half], shards["w_ffn2_in"][half:]
    ws = _Sharded(shards)
    c_all = _run_plan(_SmallGatherPlan(c), "gather_c")[0][:, 0, :]
    b_shard = lax.dynamic_slice_in_dim(b_ada, chip * nada, nada, axis=1)
    mod_cols = _ada_fwd(c_all, w_ada[0], b_shard)
    first = _PlanGroup([_SmallGatherPlan(mod_cols), ws.gather_plan(["w_ffn1_in"])])
    (mod_all,), ffn1 = first.split(_run_plan(first, "gather_first"))
    ws.gathered(["w_ffn1_in"], ffn1)
    mod = jnp.concatenate([lax.dynamic_index_in_dim(mod_all[4 * (kk >> 1) + 2 * (kk & 1)], me, axis=0, keepdims=False)
                           for kk in range(4)])

    def pack(loss, dmod, dgains, g_w_pool, g_pool_scale):
        small_g = dict(b_ada=dmod, g_norm_ffn1=dgains[0], g_norm_mix=dgains[1], g_norm_ffn2=dgains[2],
                       g_final=dgains[3], pool_scale=g_pool_scale, w_pool=g_w_pool)
        return jnp.concatenate([_pack_small(small_g), jnp.pad(loss.reshape(1, 1), ((0, 0), (0, 127)))], axis=1)

    _, dx, _, _, _, _, _, gathered = _example_step(
        x[0], loss_target[0], positions[0], mod, (g_norm_ffn1[0], g_norm_mix[0], g_norm_ffn2[0], g_final),
        w_pool[0], pool_scale[0], ws, pack)

    per_weight, loss_tile = _adam_small(*[[t[n].reshape(1, -1) for n in SMALL] for t in (w, mom, var)], gathered)
    small_out = [{n: per_weight[j][kind].reshape(w[n].shape) for j, n in enumerate(SMALL)} for kind in range(4)]
    loss = loss_tile[0, 0]

    dmod_all = gathered[:, 0, :NMOD * D]
    dmod_cols = lax.dynamic_slice_in_dim(dmod_all, chip * nada, nada, axis=1)
    sums, other = dict(ws.sums), dict(ws.other)
    late = [n for n in BIG if n not in sums]
    sums.update({n: _sum4(ws.recv[n], "sum_" + n) for n in late})
    (g_ada,), swapped = _ada_bwd(c_all, dmod_cols, _SwapPlan([sums[n] for n in late]))
    other.update(zip(late, swapped))
    ada_out = _adam(w_ada[0], m_w_ada[0], v_w_ada[0], [g_ada], "adam_w_ada")[0]
    big_out = {}
    for n in BIG:
        if sums[n].shape[0] < w[n].shape[1]:
            big_out[n] = _adam_halves(w[n][0], mom[n][0], var[n][0], sums[n], other[n], "adam_" + n)
        else:
            big_out[n] = _adam(w[n][0], mom[n][0], var[n][0], [sums[n], other[n]], "adam_" + n)[0]

    def leaf(kind, n):
        if n == "w_ada":
            return ada_out[kind][None]
        if n in big_out:
            return big_out[n][kind][None]
        return small_out[kind][n]

    return (loss, dx[None], *[leaf(kind, n) for kind in range(4) for n in WEIGHTS])
```
